```python
import math
import jax, jax.numpy as jnp
from jax import lax
import numpy as np

D_MODEL = 1024
BATCH = 8
SEQ = 2048
DEPTH = 2

F32 = jnp.float32
EPS = 1e-6
N_SUB = 3
D_FF = ((8 * D_MODEL // 3 + 127) // 128) * 128
FFN_RES_WEIGHT = 0.5
D_MIX = D_MODEL
POOL_WINDOWS = (2, 4, 8, 16)
N_POOL = len(POOL_WINDOWS)
POOL_WIDTH = D_MIX // 2
POOL_GD = POOL_WIDTH // N_POOL
SGU_WIDTH = D_MIX // 2
SGU_HEADS = 4
SGU_HD = SGU_WIDTH // SGU_HEADS
CHUNK = 128
SSM_WIDTH = D_MIX
SSM_GROUP = 16
SSM_GROUPS = SSM_WIDTH // SSM_GROUP
SSM_STATE = 64
DT_MIN = 1e-3
DT_MAX = 1e-1
N_EVEN = (DEPTH + 1) // 2
N_ODD = DEPTH // 2

kernel_name = 'hybrid_pool_sgu_s5_macaron_adaln'


def rmsnorm(x, g):
    xf = x.astype(F32)
    y = xf * lax.rsqrt(jnp.mean(xf * xf, axis=-1, keepdims=True) + EPS)
    return (y * g.astype(F32)).astype(x.dtype)


def sublayer(x, fn, mod_k, g_pre, g_post, res_weight):
    shift, scale, gate = mod_k[:, 0, None, :], mod_k[:, 1, None, :], mod_k[:, 2, None, :]
    h = rmsnorm(x, g_pre) * (1.0 + scale) + shift
    y = rmsnorm(fn(h), g_post)
    return x + res_weight * gate * y


def swiglu(h, w_in, w_out):
    a, b = jnp.split(h @ w_in, 2, axis=-1)
    return (jax.nn.silu(a) * b) @ w_out


def pool_mixer(a, w_group, ch_scale):
    s_len = a.shape[1]
    cs = jnp.cumsum(a.astype(F32), axis=1)
    pos = jnp.arange(1, s_len + 1, dtype=F32)[None, :, None]
    diffs = []
    for g, w in enumerate(POOL_WINDOWS):
        sl = slice(g * POOL_GD, (g + 1) * POOL_GD)
        c_g = cs[..., sl]
        lagged = jnp.pad(c_g, ((0, 0), (w, 0), (0, 0)))[:, :s_len]
        mean = (c_g - lagged) / jnp.minimum(pos, float(w))
        diffs.append(mean - a[..., sl].astype(F32))
    d = jnp.stack(diffs, axis=2).astype(a.dtype)
    y = jnp.einsum('bsgi,gio->bsgo', d, w_group)
    return y.reshape(a.shape) * ch_scale


def sgu_mixer(z, ln_g, ln_b, w_s, b_s):
    u, v = jnp.split(z, 2, axis=-1)
    bsz, s_len, _ = u.shape
    vf = v.astype(F32).reshape(bsz, s_len, SGU_HEADS, SGU_HD)
    mu = jnp.mean(vf, axis=-1, keepdims=True)
    var = jnp.mean(jnp.square(vf - mu), axis=-1, keepdims=True)
    vn = ((vf - mu) * lax.rsqrt(var + EPS)).reshape(bsz, s_len, SGU_WIDTH) * ln_g + ln_b
    vn = vn.astype(u.dtype).reshape(bsz, s_len // CHUNK, CHUNK, SGU_HEADS, SGU_HD)
    causal = jnp.tril(jnp.ones((CHUNK, CHUNK), dtype=bool))
    w = jnp.where(causal[None], w_s, 0.0)
    s = jnp.einsum('hts,bcshd->bcthd', w, vn) + b_s.T[None, None, :, :, None]
    return u * s.reshape(bsz, s_len, SGU_WIDTH)


def s5_mixer(u, lam_re, lam_im, b_re, b_im, c_re, c_im, d_skip, log_dt, w_glu):
    bsz, s_len, _ = u.shape
    lam = lax.complex(lam_re.astype(F32), lam_im.astype(F32))
    dt = jnp.exp(log_dt.astype(F32))[:, None]
    lam_bar = jnp.exp(lam * dt)
    bmat = lax.complex(b_re.astype(F32), b_im.astype(F32))
    b_bar = ((lam_bar - 1.0) / lam)[..., None] * bmat
    cmat = lax.complex(c_re.astype(F32), c_im.astype(F32))
    uf = u.astype(F32)
    ug = uf.reshape(bsz, s_len, SSM_GROUPS, SSM_GROUP).astype(jnp.complex64)
    bu = jnp.einsum('gpn,bsgn->bsgp', b_bar, ug)
    a_seq = jnp.broadcast_to(lam_bar, (1, s_len) + lam_bar.shape)

    def combine(left, right):
        a_l, x_l = left
        a_r, x_r = right
        return a_r * a_l, a_r * x_l + x_r

    _, states = lax.associative_scan(combine, (a_seq, bu), axis=1)
    y = jnp.einsum('gnp,bsgp->bsgn', cmat, states).real.reshape(bsz, s_len, SSM_WIDTH)
    y = y + d_skip.astype(F32) * uf
    g = jax.nn.gelu(y).astype(u.dtype)
    a, b = jnp.split(g @ w_glu, 2, axis=-1)
    return a * jax.nn.sigmoid(b)


def _fwd_setup_inputs(seed: int = 0) -> dict:
    key = jax.random.key(seed)
    ks = iter(jax.random.split(key, 32))

    def nrm(shape, scale):
        return jax.random.normal(next(ks), shape, F32) * scale

    D = D_MODEL
    x = nrm((BATCH, SEQ, D), 1.0)
    c = nrm((BATCH, D), 1.0)
    ada_w = nrm((DEPTH, D, N_SUB * 3 * D), 0.5 * D ** -0.5)
    ada_b = nrm((DEPTH, N_SUB * 3 * D), 0.02)
    norm_pre = 1.0 + nrm((DEPTH, N_SUB, D), 0.02)
    norm_post = 1.0 + nrm((DEPTH, N_SUB, D), 0.02)
    ffn_w_in = nrm((DEPTH, 2, D, 2 * D_FF), D ** -0.5)
    ffn_w_out = nrm((DEPTH, 2, D_FF, D), D_FF ** -0.5)
    ab_w_in = nrm((N_EVEN, D, POOL_WIDTH + 2 * SGU_WIDTH), D ** -0.5)
    pool_w = nrm((N_EVEN, N_POOL, POOL_GD, POOL_GD), POOL_GD ** -0.5)
    pool_scale = 1.0 + nrm((N_EVEN, POOL_WIDTH), 0.1)
    sgu_ln_g = 1.0 + nrm((N_EVEN, SGU_WIDTH), 0.02)
    sgu_ln_b = nrm((N_EVEN, SGU_WIDTH), 0.02)
    sgu_w = nrm((N_EVEN, SGU_HEADS, CHUNK, CHUNK), 0.5 * CHUNK ** -0.5)
    sgu_b = 1.0 + nrm((N_EVEN, SGU_HEADS, CHUNK), 0.02)
    ab_w_out = nrm((N_EVEN, POOL_WIDTH + SGU_WIDTH, D), (POOL_WIDTH + SGU_WIDTH) ** -0.5)
    ssm_w_in = nrm((N_ODD, D, SSM_WIDTH), D ** -0.5)
    n_idx = jnp.arange(SSM_STATE, dtype=F32)
    ssm_lam_re = -0.5 + nrm((N_ODD, SSM_GROUPS, SSM_STATE), 0.01)
    ssm_lam_im = jnp.pi * n_idx + nrm((N_ODD, SSM_GROUPS, SSM_STATE), 0.01)
    ssm_b_re = nrm((N_ODD, SSM_GROUPS, SSM_STATE, SSM_GROUP), (2 * SSM_GROUP) ** -0.5)
    ssm_b_im = nrm((N_ODD, SSM_GROUPS, SSM_STATE, SSM_GROUP), (2 * SSM_GROUP) ** -0.5)
    ssm_c_re = nrm((N_ODD, SSM_GROUPS, SSM_GROUP, SSM_STATE), SSM_STATE ** -0.5)
    ssm_c_im = nrm((N_ODD, SSM_GROUPS, SSM_GROUP, SSM_STATE), SSM_STATE ** -0.5)
    ssm_d = nrm((N_ODD, SSM_WIDTH), 1.0)
    ssm_log_dt = jax.random.uniform(next(ks), (N_ODD, SSM_GROUPS), F32,
                                    math.log(DT_MIN), math.log(DT_MAX))
    ssm_w_glu = nrm((N_ODD, SSM_WIDTH, 2 * D), SSM_WIDTH ** -0.5)
    return {'x': x, 'c': c, 'ada_w': ada_w, 'ada_b': ada_b,
            'norm_pre': norm_pre, 'norm_post': norm_post,
            'ffn_w_in': ffn_w_in, 'ffn_w_out': ffn_w_out,
            'ab_w_in': ab_w_in, 'pool_w': pool_w, 'pool_scale': pool_scale,
            'sgu_ln_g': sgu_ln_g, 'sgu_ln_b': sgu_ln_b, 'sgu_w': sgu_w, 'sgu_b': sgu_b,
            'ab_w_out': ab_w_out, 'ssm_w_in': ssm_w_in,
            'ssm_lam_re': ssm_lam_re, 'ssm_lam_im': ssm_lam_im,
            'ssm_b_re': ssm_b_re, 'ssm_b_im': ssm_b_im,
            'ssm_c_re': ssm_c_re, 'ssm_c_im': ssm_c_im,
            'ssm_d': ssm_d, 'ssm_log_dt': ssm_log_dt, 'ssm_w_glu': ssm_w_glu}


def _fwd_reference(x, c, ada_w, ada_b, norm_pre, norm_post, ffn_w_in, ffn_w_out,
              ab_w_in, pool_w, pool_scale, sgu_ln_g, sgu_ln_b, sgu_w, sgu_b, ab_w_out,
              ssm_w_in, ssm_lam_re, ssm_lam_im, ssm_b_re, ssm_b_im, ssm_c_re, ssm_c_im,
              ssm_d, ssm_log_dt, ssm_w_glu):
    cond = jax.nn.silu(c)
    for l in range(DEPTH):
        mod = (cond @ ada_w[l] + ada_b[l]).reshape(-1, N_SUB, 3, D_MODEL)
        i = l // 2

        x = sublayer(x, lambda h: swiglu(h, ffn_w_in[l, 0], ffn_w_out[l, 0]),
                     mod[:, 0], norm_pre[l, 0], norm_post[l, 0], FFN_RES_WEIGHT)

        if l % 2 == 0:
            def mix(h):
                z = h @ ab_w_in[i]
                y_a = pool_mixer(z[..., :POOL_WIDTH], pool_w[i], pool_scale[i])
                y_b = sgu_mixer(jax.nn.gelu(z[..., POOL_WIDTH:]), sgu_ln_g[i], sgu_ln_b[i],
                                sgu_w[i], sgu_b[i])
                return jnp.concatenate([y_a, y_b], axis=-1) @ ab_w_out[i]
        else:
            def mix(h):
                return s5_mixer(h @ ssm_w_in[i], ssm_lam_re[i], ssm_lam_im[i],
                                ssm_b_re[i], ssm_b_im[i], ssm_c_re[i], ssm_c_im[i],
                                ssm_d[i], ssm_log_dt[i], ssm_w_glu[i])
        x = sublayer(x, mix, mod[:, 1], norm_pre[l, 1], norm_post[l, 1], 1.0)

        x = sublayer(x, lambda h: swiglu(h, ffn_w_in[l, 1], ffn_w_out[l, 1]),
                     mod[:, 2], norm_pre[l, 2], norm_post[l, 2], FFN_RES_WEIGHT)
    return x


import jax as _jax
import jax.numpy as _jnp

TWIN_FORMAT = 'train_step'
FWD_PARAMS = ['x', 'c', 'ada_w', 'ada_b', 'norm_pre', 'norm_post', 'ffn_w_in', 'ffn_w_out', 'ab_w_in', 'pool_w', 'pool_scale', 'sgu_ln_g', 'sgu_ln_b', 'sgu_w', 'sgu_b', 'ab_w_out', 'ssm_w_in', 'ssm_lam_re', 'ssm_lam_im', 'ssm_b_re', 'ssm_b_im', 'ssm_c_re', 'ssm_c_im', 'ssm_d', 'ssm_log_dt', 'ssm_w_glu']
TWIN_WEIGHTS = ['ada_w', 'ada_b', 'norm_pre', 'norm_post', 'ffn_w_in', 'ffn_w_out', 'ab_w_in', 'pool_w', 'pool_scale', 'sgu_ln_g', 'sgu_ln_b', 'sgu_w', 'sgu_b', 'ab_w_out', 'ssm_w_in', 'ssm_lam_re', 'ssm_lam_im', 'ssm_b_re', 'ssm_b_im', 'ssm_c_re', 'ssm_c_im', 'ssm_d', 'ssm_log_dt', 'ssm_w_glu']
TWIN_DIFF_INPUT = 'x'
TWIN_INPUTS = ['x', 'c', 'ada_w', 'ada_b', 'norm_pre', 'norm_post', 'ffn_w_in', 'ffn_w_out', 'ab_w_in', 'pool_w', 'pool_scale', 'sgu_ln_g', 'sgu_ln_b', 'sgu_w', 'sgu_b', 'ab_w_out', 'ssm_w_in', 'ssm_lam_re', 'ssm_lam_im', 'ssm_b_re', 'ssm_b_im', 'ssm_c_re', 'ssm_c_im', 'ssm_d', 'ssm_log_dt', 'ssm_w_glu', 'loss_target', 'm_ada_w', 'm_ada_b', 'm_norm_pre', 'm_norm_post', 'm_ffn_w_in', 'm_ffn_w_out', 'm_ab_w_in', 'm_pool_w', 'm_pool_scale', 'm_sgu_ln_g', 'm_sgu_ln_b', 'm_sgu_w', 'm_sgu_b', 'm_ab_w_out', 'm_ssm_w_in', 'm_ssm_lam_re', 'm_ssm_lam_im', 'm_ssm_b_re', 'm_ssm_b_im', 'm_ssm_c_re', 'm_ssm_c_im', 'm_ssm_d', 'm_ssm_log_dt', 'm_ssm_w_glu', 'v_ada_w', 'v_ada_b', 'v_norm_pre', 'v_norm_post', 'v_ffn_w_in', 'v_ffn_w_out', 'v_ab_w_in', 'v_pool_w', 'v_pool_scale', 'v_sgu_ln_g', 'v_sgu_ln_b', 'v_sgu_w', 'v_sgu_b', 'v_ab_w_out', 'v_ssm_w_in', 'v_ssm_lam_re', 'v_ssm_lam_im', 'v_ssm_b_re', 'v_ssm_b_im', 'v_ssm_c_re', 'v_ssm_c_im', 'v_ssm_d', 'v_ssm_log_dt', 'v_ssm_w_glu']
TWIN_OUTPUTS = ['loss', 'grad_x', 'grad_ada_w', 'grad_ada_b', 'grad_norm_pre', 'grad_norm_post', 'grad_ffn_w_in', 'grad_ffn_w_out', 'grad_ab_w_in', 'grad_pool_w', 'grad_pool_scale', 'grad_sgu_ln_g', 'grad_sgu_ln_b', 'grad_sgu_w', 'grad_sgu_b', 'grad_ab_w_out', 'grad_ssm_w_in', 'grad_ssm_lam_re', 'grad_ssm_lam_im', 'grad_ssm_b_re', 'grad_ssm_b_im', 'grad_ssm_c_re', 'grad_ssm_c_im', 'grad_ssm_d', 'grad_ssm_log_dt', 'grad_ssm_w_glu', 'delta_ada_w', 'delta_ada_b', 'delta_norm_pre', 'delta_norm_post', 'delta_ffn_w_in', 'delta_ffn_w_out', 'delta_ab_w_in', 'delta_pool_w', 'delta_pool_scale', 'delta_sgu_ln_g', 'delta_sgu_ln_b', 'delta_sgu_w', 'delta_sgu_b', 'delta_ab_w_out', 'delta_ssm_w_in', 'delta_ssm_lam_re', 'delta_ssm_lam_im', 'delta_ssm_b_re', 'delta_ssm_b_im', 'delta_ssm_c_re', 'delta_ssm_c_im', 'delta_ssm_d', 'delta_ssm_log_dt', 'delta_ssm_w_glu', 'new_m_ada_w', 'new_m_ada_b', 'new_m_norm_pre', 'new_m_norm_post', 'new_m_ffn_w_in', 'new_m_ffn_w_out', 'new_m_ab_w_in', 'new_m_pool_w', 'new_m_pool_scale', 'new_m_sgu_ln_g', 'new_m_sgu_ln_b', 'new_m_sgu_w', 'new_m_sgu_b', 'new_m_ab_w_out', 'new_m_ssm_w_in', 'new_m_ssm_lam_re', 'new_m_ssm_lam_im', 'new_m_ssm_b_re', 'new_m_ssm_b_im', 'new_m_ssm_c_re', 'new_m_ssm_c_im', 'new_m_ssm_d', 'new_m_ssm_log_dt', 'new_m_ssm_w_glu', 'new_v_ada_w', 'new_v_ada_b', 'new_v_norm_pre', 'new_v_norm_post', 'new_v_ffn_w_in', 'new_v_ffn_w_out', 'new_v_ab_w_in', 'new_v_pool_w', 'new_v_pool_scale', 'new_v_sgu_ln_g', 'new_v_sgu_ln_b', 'new_v_sgu_w', 'new_v_sgu_b', 'new_v_ab_w_out', 'new_v_ssm_w_in', 'new_v_ssm_lam_re', 'new_v_ssm_lam_im', 'new_v_ssm_b_re', 'new_v_ssm_b_im', 'new_v_ssm_c_re', 'new_v_ssm_c_im', 'new_v_ssm_d', 'new_v_ssm_log_dt', 'new_v_ssm_w_glu']
TWIN_LEAF_KINDS = {'loss': 'loss', 'grad_x': 'grad_x', 'grad_ada_w': 'grad_w', 'grad_ada_b': 'grad_w', 'grad_norm_pre': 'grad_w', 'grad_norm_post': 'grad_w', 'grad_ffn_w_in': 'grad_w', 'grad_ffn_w_out': 'grad_w', 'grad_ab_w_in': 'grad_w', 'grad_pool_w': 'grad_w', 'grad_pool_scale': 'grad_w', 'grad_sgu_ln_g': 'grad_w', 'grad_sgu_ln_b': 'grad_w', 'grad_sgu_w': 'grad_w', 'grad_sgu_b': 'grad_w', 'grad_ab_w_out': 'grad_w', 'grad_ssm_w_in': 'grad_w', 'grad_ssm_lam_re': 'grad_w', 'grad_ssm_lam_im': 'grad_w', 'grad_ssm_b_re': 'grad_w', 'grad_ssm_b_im': 'grad_w', 'grad_ssm_c_re': 'grad_w', 'grad_ssm_c_im': 'grad_w', 'grad_ssm_d': 'grad_w', 'grad_ssm_log_dt': 'grad_w', 'grad_ssm_w_glu': 'grad_w', 'delta_ada_w': 'delta_w', 'delta_ada_b': 'delta_w', 'delta_norm_pre': 'delta_w', 'delta_norm_post': 'delta_w', 'delta_ffn_w_in': 'delta_w', 'delta_ffn_w_out': 'delta_w', 'delta_ab_w_in': 'delta_w', 'delta_pool_w': 'delta_w', 'delta_pool_scale': 'delta_w', 'delta_sgu_ln_g': 'delta_w', 'delta_sgu_ln_b': 'delta_w', 'delta_sgu_w': 'delta_w', 'delta_sgu_b': 'delta_w', 'delta_ab_w_out': 'delta_w', 'delta_ssm_w_in': 'delta_w', 'delta_ssm_lam_re': 'delta_w', 'delta_ssm_lam_im': 'delta_w', 'delta_ssm_b_re': 'delta_w', 'delta_ssm_b_im': 'delta_w', 'delta_ssm_c_re': 'delta_w', 'delta_ssm_c_im': 'delta_w', 'delta_ssm_d': 'delta_w', 'delta_ssm_log_dt': 'delta_w', 'delta_ssm_w_glu': 'delta_w', 'new_m_ada_w': 'new_m', 'new_m_ada_b': 'new_m', 'new_m_norm_pre': 'new_m', 'new_m_norm_post': 'new_m', 'new_m_ffn_w_in': 'new_m', 'new_m_ffn_w_out': 'new_m', 'new_m_ab_w_in': 'new_m', 'new_m_pool_w': 'new_m', 'new_m_pool_scale': 'new_m', 'new_m_sgu_ln_g': 'new_m', 'new_m_sgu_ln_b': 'new_m', 'new_m_sgu_w': 'new_m', 'new_m_sgu_b': 'new_m', 'new_m_ab_w_out': 'new_m', 'new_m_ssm_w_in': 'new_m', 'new_m_ssm_lam_re': 'new_m', 'new_m_ssm_lam_im': 'new_m', 'new_m_ssm_b_re': 'new_m', 'new_m_ssm_b_im': 'new_m', 'new_m_ssm_c_re': 'new_m', 'new_m_ssm_c_im': 'new_m', 'new_m_ssm_d': 'new_m', 'new_m_ssm_log_dt': 'new_m', 'new_m_ssm_w_glu': 'new_m', 'new_v_ada_w': 'new_v', 'new_v_ada_b': 'new_v', 'new_v_norm_pre': 'new_v', 'new_v_norm_post': 'new_v', 'new_v_ffn_w_in': 'new_v', 'new_v_ffn_w_out': 'new_v', 'new_v_ab_w_in': 'new_v', 'new_v_pool_w': 'new_v', 'new_v_pool_scale': 'new_v', 'new_v_sgu_ln_g': 'new_v', 'new_v_sgu_ln_b': 'new_v', 'new_v_sgu_w': 'new_v', 'new_v_sgu_b': 'new_v', 'new_v_ab_w_out': 'new_v', 'new_v_ssm_w_in': 'new_v', 'new_v_ssm_lam_re': 'new_v', 'new_v_ssm_lam_im': 'new_v', 'new_v_ssm_b_re': 'new_v', 'new_v_ssm_b_im': 'new_v', 'new_v_ssm_c_re': 'new_v', 'new_v_ssm_c_im': 'new_v', 'new_v_ssm_d': 'new_v', 'new_v_ssm_log_dt': 'new_v', 'new_v_ssm_w_glu': 'new_v'}


def _forward(args):
    return _fwd_reference(*[args[k] for k in FWD_PARAMS])


def _output_shape():
    out = _jax.eval_shape(lambda: _forward(_fwd_setup_inputs(0)))
    return out.shape, out.dtype

N_MICROBATCH = 1
ADAM_LR = 0.001
ADAM_B1 = 0.9
ADAM_B2 = 0.999
ADAM_EPS = 1e-08
ADAM_WD = 0.01
ADAM_STEP = 10
PER_EXAMPLE_BATCH_AXIS = {'x': 0, 'c': 0, 'loss_target': 0}
SHARED_INPUTS = []
_WEIGHT_DTYPES = {'ada_w': _jnp.float32, 'ada_b': _jnp.float32, 'norm_pre': _jnp.float32, 'norm_post': _jnp.float32, 'ffn_w_in': _jnp.float32, 'ffn_w_out': _jnp.float32, 'ab_w_in': _jnp.float32, 'pool_w': _jnp.float32, 'pool_scale': _jnp.float32, 'sgu_ln_g': _jnp.float32, 'sgu_ln_b': _jnp.float32, 'sgu_w': _jnp.float32, 'sgu_b': _jnp.float32, 'ab_w_out': _jnp.float32, 'ssm_w_in': _jnp.float32, 'ssm_lam_re': _jnp.float32, 'ssm_lam_im': _jnp.float32, 'ssm_b_re': _jnp.float32, 'ssm_b_im': _jnp.float32, 'ssm_c_re': _jnp.float32, 'ssm_c_im': _jnp.float32, 'ssm_d': _jnp.float32, 'ssm_log_dt': _jnp.float32, 'ssm_w_glu': _jnp.float32}
MOMENT_SCALE = {'ada_w': 5.719326e-01, 'ada_b': 1.045797e+00, 'norm_pre': 6.407003e-02, 'norm_post': 1.163401e+00, 'ffn_w_in': 2.405305e-02, 'ffn_w_out': 4.268536e-02, 'ab_w_in': 8.691621e-02, 'pool_w': 1.043155e-01, 'pool_scale': 1.073353e-01, 'sgu_ln_g': 2.067285e-02, 'sgu_ln_b': 2.053291e-02, 'sgu_w': 4.031022e-02, 'sgu_b': 6.915476e-02, 'ab_w_out': 1.678672e-01, 'ssm_w_in': 1.563326e-01, 'ssm_lam_re': 2.667077e-02, 'ssm_lam_im': 2.888524e-02, 'ssm_b_re': 1.577396e-02, 'ssm_b_im': 1.664835e-02, 'ssm_c_re': 2.235656e-02, 'ssm_c_im': 2.563248e-02, 'ssm_d': 3.787132e-01, 'ssm_log_dt': 4.190620e+00, 'ssm_w_glu': 2.509657e-01}


def _to_microbatches(a, axis):
    t = _jnp.moveaxis(a, axis, 0)
    t = t.reshape((N_MICROBATCH, t.shape[0] // N_MICROBATCH) + t.shape[1:])
    return _jnp.moveaxis(t, 1, axis + 1)


def setup_inputs(seed: int = 0) -> dict:
    inp = _fwd_setup_inputs(seed)
    key = _jax.random.fold_in(_jax.random.key(seed), 7919)
    shape, _ = _output_shape()
    out = dict(inp)
    out["loss_target"] = _jax.random.normal(_jax.random.fold_in(key, 0), shape, _jnp.float32)
    for i, name in enumerate(TWIN_WEIGHTS):
        w = inp[name].astype(_jnp.float32)
        if MOMENT_SCALE is None:
            s = _jnp.sqrt(_jnp.mean(_jnp.square(w)) + 1e-30)
        else:
            s = MOMENT_SCALE[name]
        km, kv = _jax.random.split(_jax.random.fold_in(key, i + 1))
        out[name] = w
        out["m_" + name] = s * _jax.random.normal(km, w.shape, _jnp.float32)
        out["v_" + name] = (s * s) * _jax.random.uniform(kv, w.shape, _jnp.float32, 0.5, 1.5)
    if N_MICROBATCH > 1:
        for name, axis in PER_EXAMPLE_BATCH_AXIS.items():
            out[name] = _to_microbatches(out[name], axis)
    return {'x': out['x'], 'c': out['c'], 'ada_w': out['ada_w'], 'ada_b': out['ada_b'], 'norm_pre': out['norm_pre'], 'norm_post': out['norm_post'], 'ffn_w_in': out['ffn_w_in'], 'ffn_w_out': out['ffn_w_out'], 'ab_w_in': out['ab_w_in'], 'pool_w': out['pool_w'], 'pool_scale': out['pool_scale'], 'sgu_ln_g': out['sgu_ln_g'], 'sgu_ln_b': out['sgu_ln_b'], 'sgu_w': out['sgu_w'], 'sgu_b': out['sgu_b'], 'ab_w_out': out['ab_w_out'], 'ssm_w_in': out['ssm_w_in'], 'ssm_lam_re': out['ssm_lam_re'], 'ssm_lam_im': out['ssm_lam_im'], 'ssm_b_re': out['ssm_b_re'], 'ssm_b_im': out['ssm_b_im'], 'ssm_c_re': out['ssm_c_re'], 'ssm_c_im': out['ssm_c_im'], 'ssm_d': out['ssm_d'], 'ssm_log_dt': out['ssm_log_dt'], 'ssm_w_glu': out['ssm_w_glu'], 'loss_target': out['loss_target'], 'm_ada_w': out['m_ada_w'], 'm_ada_b': out['m_ada_b'], 'm_norm_pre': out['m_norm_pre'], 'm_norm_post': out['m_norm_post'], 'm_ffn_w_in': out['m_ffn_w_in'], 'm_ffn_w_out': out['m_ffn_w_out'], 'm_ab_w_in': out['m_ab_w_in'], 'm_pool_w': out['m_pool_w'], 'm_pool_scale': out['m_pool_scale'], 'm_sgu_ln_g': out['m_sgu_ln_g'], 'm_sgu_ln_b': out['m_sgu_ln_b'], 'm_sgu_w': out['m_sgu_w'], 'm_sgu_b': out['m_sgu_b'], 'm_ab_w_out': out['m_ab_w_out'], 'm_ssm_w_in': out['m_ssm_w_in'], 'm_ssm_lam_re': out['m_ssm_lam_re'], 'm_ssm_lam_im': out['m_ssm_lam_im'], 'm_ssm_b_re': out['m_ssm_b_re'], 'm_ssm_b_im': out['m_ssm_b_im'], 'm_ssm_c_re': out['m_ssm_c_re'], 'm_ssm_c_im': out['m_ssm_c_im'], 'm_ssm_d': out['m_ssm_d'], 'm_ssm_log_dt': out['m_ssm_log_dt'], 'm_ssm_w_glu': out['m_ssm_w_glu'], 'v_ada_w': out['v_ada_w'], 'v_ada_b': out['v_ada_b'], 'v_norm_pre': out['v_norm_pre'], 'v_norm_post': out['v_norm_post'], 'v_ffn_w_in': out['v_ffn_w_in'], 'v_ffn_w_out': out['v_ffn_w_out'], 'v_ab_w_in': out['v_ab_w_in'], 'v_pool_w': out['v_pool_w'], 'v_pool_scale': out['v_pool_scale'], 'v_sgu_ln_g': out['v_sgu_ln_g'], 'v_sgu_ln_b': out['v_sgu_ln_b'], 'v_sgu_w': out['v_sgu_w'], 'v_sgu_b': out['v_sgu_b'], 'v_ab_w_out': out['v_ab_w_out'], 'v_ssm_w_in': out['v_ssm_w_in'], 'v_ssm_lam_re': out['v_ssm_lam_re'], 'v_ssm_lam_im': out['v_ssm_lam_im'], 'v_ssm_b_re': out['v_ssm_b_re'], 'v_ssm_b_im': out['v_ssm_b_im'], 'v_ssm_c_re': out['v_ssm_c_re'], 'v_ssm_c_im': out['v_ssm_c_im'], 'v_ssm_d': out['v_ssm_d'], 'v_ssm_log_dt': out['v_ssm_log_dt'], 'v_ssm_w_glu': out['v_ssm_w_glu']}


def _loss(weights, diff, rest, loss_target):
    with _jax.named_scope("forward"):
        args = {**rest, TWIN_DIFF_INPUT: diff, **{k: w.astype(_WEIGHT_DTYPES[k]) for k, w in weights.items()}}
        y = _forward(args)
    with _jax.named_scope("loss_head"):
        err = _jnp.square(y.astype(_jnp.float32) - loss_target)
        return 0.5 * _jnp.sum(_jnp.mean(err, axis=-1)) if err.ndim else 0.5 * err


def _adamw(w, g, m, v):
    m = ADAM_B1 * m + (1.0 - ADAM_B1) * g
    v = ADAM_B2 * v + (1.0 - ADAM_B2) * _jnp.square(g)
    m_hat = m / (1.0 - ADAM_B1 ** ADAM_STEP)
    v_hat = v / (1.0 - ADAM_B2 ** ADAM_STEP)
    delta = -ADAM_LR * (m_hat / (_jnp.sqrt(v_hat) + ADAM_EPS) + ADAM_WD * w)
    return delta, m, v


def reference(x, c, ada_w, ada_b, norm_pre, norm_post, ffn_w_in, ffn_w_out, ab_w_in, pool_w, pool_scale, sgu_ln_g, sgu_ln_b, sgu_w, sgu_b, ab_w_out, ssm_w_in, ssm_lam_re, ssm_lam_im, ssm_b_re, ssm_b_im, ssm_c_re, ssm_c_im, ssm_d, ssm_log_dt, ssm_w_glu, loss_target, m_ada_w, m_ada_b, m_norm_pre, m_norm_post, m_ffn_w_in, m_ffn_w_out, m_ab_w_in, m_pool_w, m_pool_scale, m_sgu_ln_g, m_sgu_ln_b, m_sgu_w, m_sgu_b, m_ab_w_out, m_ssm_w_in, m_ssm_lam_re, m_ssm_lam_im, m_ssm_b_re, m_ssm_b_im, m_ssm_c_re, m_ssm_c_im, m_ssm_d, m_ssm_log_dt, m_ssm_w_glu, v_ada_w, v_ada_b, v_norm_pre, v_norm_post, v_ffn_w_in, v_ffn_w_out, v_ab_w_in, v_pool_w, v_pool_scale, v_sgu_ln_g, v_sgu_ln_b, v_sgu_w, v_sgu_b, v_ab_w_out, v_ssm_w_in, v_ssm_lam_re, v_ssm_lam_im, v_ssm_b_re, v_ssm_b_im, v_ssm_c_re, v_ssm_c_im, v_ssm_d, v_ssm_log_dt, v_ssm_w_glu):
    given = dict(x=x, c=c, ada_w=ada_w, ada_b=ada_b, norm_pre=norm_pre, norm_post=norm_post, ffn_w_in=ffn_w_in, ffn_w_out=ffn_w_out, ab_w_in=ab_w_in, pool_w=pool_w, pool_scale=pool_scale, sgu_ln_g=sgu_ln_g, sgu_ln_b=sgu_ln_b, sgu_w=sgu_w, sgu_b=sgu_b, ab_w_out=ab_w_out, ssm_w_in=ssm_w_in, ssm_lam_re=ssm_lam_re, ssm_lam_im=ssm_lam_im, ssm_b_re=ssm_b_re, ssm_b_im=ssm_b_im, ssm_c_re=ssm_c_re, ssm_c_im=ssm_c_im, ssm_d=ssm_d, ssm_log_dt=ssm_log_dt, ssm_w_glu=ssm_w_glu, loss_target=loss_target, m_ada_w=m_ada_w, m_ada_b=m_ada_b, m_norm_pre=m_norm_pre, m_norm_post=m_norm_post, m_ffn_w_in=m_ffn_w_in, m_ffn_w_out=m_ffn_w_out, m_ab_w_in=m_ab_w_in, m_pool_w=m_pool_w, m_pool_scale=m_pool_scale, m_sgu_ln_g=m_sgu_ln_g, m_sgu_ln_b=m_sgu_ln_b, m_sgu_w=m_sgu_w, m_sgu_b=m_sgu_b, m_ab_w_out=m_ab_w_out, m_ssm_w_in=m_ssm_w_in, m_ssm_lam_re=m_ssm_lam_re, m_ssm_lam_im=m_ssm_lam_im, m_ssm_b_re=m_ssm_b_re, m_ssm_b_im=m_ssm_b_im, m_ssm_c_re=m_ssm_c_re, m_ssm_c_im=m_ssm_c_im, m_ssm_d=m_ssm_d, m_ssm_log_dt=m_ssm_log_dt, m_ssm_w_glu=m_ssm_w_glu, v_ada_w=v_ada_w, v_ada_b=v_ada_b, v_norm_pre=v_norm_pre, v_norm_post=v_norm_post, v_ffn_w_in=v_ffn_w_in, v_ffn_w_out=v_ffn_w_out, v_ab_w_in=v_ab_w_in, v_pool_w=v_pool_w, v_pool_scale=v_pool_scale, v_sgu_ln_g=v_sgu_ln_g, v_sgu_ln_b=v_sgu_ln_b, v_sgu_w=v_sgu_w, v_sgu_b=v_sgu_b, v_ab_w_out=v_ab_w_out, v_ssm_w_in=v_ssm_w_in, v_ssm_lam_re=v_ssm_lam_re, v_ssm_lam_im=v_ssm_lam_im, v_ssm_b_re=v_ssm_b_re, v_ssm_b_im=v_ssm_b_im, v_ssm_c_re=v_ssm_c_re, v_ssm_c_im=v_ssm_c_im, v_ssm_d=v_ssm_d, v_ssm_log_dt=v_ssm_log_dt, v_ssm_w_glu=v_ssm_w_glu)
    weights = {n: given[n] for n in TWIN_WEIGHTS}
    shared = {n: given[n] for n in SHARED_INPUTS}
    per_example = {n: given[n] for n in ['x', 'c']}
    grad_fn = _jax.value_and_grad(_loss, argnums=(0, 1))

    def one_microbatch(ex, loss_target):
        ex = dict(ex)
        diff = ex.pop(TWIN_DIFF_INPUT)
        return grad_fn(weights, diff, {**shared, **ex}, loss_target)

    if N_MICROBATCH == 1:
        loss, (grad_w, grad_x) = one_microbatch(per_example, given["loss_target"])
    else:
        def body(carry, xs):
            loss_sum, grad_sum = carry
            l_k, (gw_k, gx_k) = one_microbatch(xs[0], xs[1])
            with _jax.named_scope("update"):
                return (loss_sum + l_k, _jax.tree.map(_jnp.add, grad_sum, gw_k)), gx_k

        init = (_jnp.zeros((), _jnp.float32), _jax.tree.map(_jnp.zeros_like, weights))
        (loss, grad_w), grad_x = _jax.lax.scan(body, init, (per_example, given["loss_target"]))
    with _jax.named_scope("update"):
        delta_w, new_m, new_v = {}, {}, {}
        for n in TWIN_WEIGHTS:
            delta_w[n], new_m[n], new_v[n] = _adamw(weights[n], grad_w[n], given["m_" + n], given["v_" + n])
    return (loss, grad_x, *[grad_w[n] for n in TWIN_WEIGHTS], *[delta_w[n] for n in TWIN_WEIGHTS],
            *[new_m[n] for n in TWIN_WEIGHTS], *[new_v[n] for n in TWIN_WEIGHTS])
```

```python
import math

import jax
import jax.numpy as jnp
from jax import lax
from jax.experimental import pallas as pl
from jax.experimental.pallas import tpu as pltpu

F32 = jnp.float32
BF16 = jnp.bfloat16
EPS = 1e-6
MESH_T = pl.DeviceIdType.MESH
VMEM_LIMIT_BYTES = 56 * 1024 * 1024
N_CHIPS = 4
N_DEV = 8
POOL_WINDOWS = (2, 4, 8, 16)
CHUNK = 128
SCAN_LANES = 8
ADAM_LR = 0.001
ADAM_B1 = 0.9
ADAM_B2 = 0.999
ADAM_EPS = 1e-08
ADAM_WD = 0.01
ADAM_STEP = 10
GELU_C = math.sqrt(2.0 / math.pi)
GELU_K = 0.044715

V_SHIFT, V_SCALE, V_GATE, V_GPRE, V_GPOST = 0, 1, 2, 3, 4
A_SHIFT, A_SCALE, A_GATE, A_GPRE, A_GPOST = 0, 1, 2, 3, 4

ANY = pl.BlockSpec(memory_space=pl.ANY)


def _cp(sem=None):
    if sem is None:
        return pltpu.CompilerParams(vmem_limit_bytes=VMEM_LIMIT_BYTES)
    return pltpu.CompilerParams(vmem_limit_bytes=VMEM_LIMIT_BYTES, dimension_semantics=sem)


def _dot(a, b):
    return jnp.dot(a, b, preferred_element_type=F32)


def _dot_nt(a, b):
    return lax.dot_general(a, b, (((1,), (1,)), ((), ())), preferred_element_type=F32)


def _dot_tn(a, b):
    return lax.dot_general(a, b, (((0,), (0,)), ((), ())), preferred_element_type=F32)


def _sigmoid(x):
    return 1.0 / (1.0 + jnp.exp(-x))


def _gelu(x):
    return 0.5 * x * (1.0 + jnp.tanh(GELU_C * (x + GELU_K * x * x * x)))


def _gelu_grad(x):
    t = jnp.tanh(GELU_C * (x + GELU_K * x * x * x))
    return 0.5 * (1.0 + t) + 0.5 * x * (1.0 - t * t) * GELU_C * (1.0 + 3.0 * GELU_K * x * x)


def _rowsum(v):
    return jnp.sum(v, axis=0, keepdims=True)


def _lanemean(v):
    return jnp.mean(v, axis=-1, keepdims=True)


def _row(ref, i):
    return ref[pl.ds(i, 1), :]


def _pre_fwd(x, vec_ref):
    r = lax.rsqrt(_lanemean(x * x) + EPS)
    return (x * r) * _row(vec_ref, V_GPRE) * (1.0 + _row(vec_ref, V_SCALE)) + _row(vec_ref, V_SHIFT)


def _pre_bwd(x, dh, vec_ref):
    g = _row(vec_ref, V_GPRE)
    sc = 1.0 + _row(vec_ref, V_SCALE)
    r = lax.rsqrt(_lanemean(x * x) + EPS)
    xn = x * r
    dhx = dh * xn
    t2 = dh * (g * sc)
    dx = r * (t2 - xn * _lanemean(t2 * xn))
    return dx, _rowsum(dh), _rowsum(dhx * g), _rowsum(dhx * sc)


def _post_fwd(x, f, vec_ref, rw):
    q = lax.rsqrt(_lanemean(f * f) + EPS)
    return x + (rw * _row(vec_ref, V_GATE)) * (f * q * _row(vec_ref, V_GPOST))


def _post_bwd(dout, f, vec_ref, rw):
    gp = _row(vec_ref, V_GPOST)
    gate = _row(vec_ref, V_GATE)
    q = lax.rsqrt(_lanemean(f * f) + EPS)
    fhat = f * q
    dgate = _rowsum(dout * (rw * fhat * gp))
    dy = dout * (rw * gate)
    dgpost = _rowsum(dy * fhat)
    t = dy * gp
    df = q * (t - fhat * _lanemean(t * fhat))
    return df, dgate, dgpost


def _acc_add(acc_ref, row, v):
    acc_ref[pl.ds(row, 1), :] += v


def ffn_fwd(x, vec, win_g, wout_g, l, k, name, tm=256):
    S, D = x.shape
    FS = win_g.shape[-1]
    FR = wout_g.shape[-2]

    def body(x_ref, vec_ref, win_hbm, wout_hbm, out_ref, h_ref, u_ref, f_ref, win_s, wout_s, sem):
        @pl.when(pl.program_id(0) == 0)
        def _():
            cps = [pltpu.make_async_copy(win_hbm.at[j, l, k], win_s.at[j], sem.at[j]) for j in range(N_CHIPS)]
            cps += [pltpu.make_async_copy(wout_hbm.at[j, l, k], wout_s.at[pl.ds(j * FR, FR), :], sem.at[N_CHIPS + j])
                    for j in range(N_CHIPS)]
            for cp in cps:
                cp.start()
            for cp in cps:
                cp.wait()

        xv = x_ref[...]
        h = _pre_fwd(xv, vec_ref).astype(BF16)
        h_ref[...] = h
        f = jnp.zeros((tm, D), F32)
        for j in range(2):
            a = _dot(h, win_s[j])
            b = _dot(h, win_s[2 + j])
            u_ref[j] = a.astype(BF16)
            u_ref[2 + j] = b.astype(BF16)
            act = (a * _sigmoid(a) * b).astype(BF16)
            f = f + _dot(act, wout_s[pl.ds(j * FS, FS), :])
        f_ref[...] = f
        out_ref[...] = _post_fwd(xv, f, vec_ref, 0.5)

    row = pl.BlockSpec((tm, D), lambda i: (i, 0))
    return pl.pallas_call(
        body, name=name, grid=(S // tm,),
        in_specs=[row, pl.BlockSpec((8, D), lambda i: (0, 0)), ANY, ANY],
        out_specs=[row, row, pl.BlockSpec((N_CHIPS, tm, FS), lambda i: (0, i, 0)), row],
        out_shape=[jax.ShapeDtypeStruct((S, D), F32), jax.ShapeDtypeStruct((S, D), BF16),
                   jax.ShapeDtypeStruct((N_CHIPS, S, FS), BF16), jax.ShapeDtypeStruct((S, D), F32)],
        scratch_shapes=[pltpu.VMEM((N_CHIPS, D, FS), BF16), pltpu.VMEM((N_CHIPS * FR, D), BF16),
                        pltpu.SemaphoreType.DMA((2 * N_CHIPS,))],
        compiler_params=_cp(("arbitrary",)),
    )(x, vec, win_g, wout_g)


def ffn_bwd(dout, x, f, u, vec, win_g, wout_g, l, k, name, tm=256):
    S, D = x.shape
    FS = win_g.shape[-1]
    FR = wout_g.shape[-2]

    def body(dout_ref, x_ref, f_ref, u_ref, vec_ref, win_hbm, wout_hbm,
             dx_ref, df_ref, du_ref, act_ref, acc_ref, win_s, wout_s, sem):
        @pl.when(pl.program_id(0) == 0)
        def _():
            cps = [pltpu.make_async_copy(win_hbm.at[j, l, k], win_s.at[j], sem.at[j]) for j in range(N_CHIPS)]
            cps += [pltpu.make_async_copy(wout_hbm.at[j, l, k], wout_s.at[pl.ds(j * FR, FR), :], sem.at[N_CHIPS + j])
                    for j in range(N_CHIPS)]
            for cp in cps:
                cp.start()
            acc_ref[...] = jnp.zeros_like(acc_ref)
            for cp in cps:
                cp.wait()

        dout_v = dout_ref[...]
        df, dgate, dgpost = _post_bwd(dout_v, f_ref[...], vec_ref, 0.5)
        dfb = df.astype(BF16)
        df_ref[...] = dfb
        dh = jnp.zeros((tm, D), F32)
        for j in range(2):
            a = u_ref[j].astype(F32)
            b = u_ref[2 + j].astype(F32)
            sig = _sigmoid(a)
            sl = a * sig
            dact = _dot_nt(dfb, wout_s[pl.ds(j * FS, FS), :])
            da = (dact * b * (sig * (1.0 + a * (1.0 - sig)))).astype(BF16)
            db = (dact * sl).astype(BF16)
            du_ref[j] = da
            du_ref[2 + j] = db
            act_ref[:, j * FS:(j + 1) * FS] = (sl * b).astype(BF16)
            dh = dh + _dot_nt(da, win_s[j]) + _dot_nt(db, win_s[2 + j])
        dx, dshift, dscale, dgpre = _pre_bwd(x_ref[...], dh, vec_ref)
        dx_ref[...] = dout_v + dx
        _acc_add(acc_ref, A_SHIFT, dshift)
        _acc_add(acc_ref, A_SCALE, dscale)
        _acc_add(acc_ref, A_GATE, dgate)
        _acc_add(acc_ref, A_GPRE, dgpre)
        _acc_add(acc_ref, A_GPOST, dgpost)

    row = pl.BlockSpec((tm, D), lambda i: (i, 0))
    ublk = pl.BlockSpec((N_CHIPS, tm, FS), lambda i: (0, i, 0))
    const = pl.BlockSpec((8, D), lambda i: (0, 0))
    return pl.pallas_call(
        body, name=name, grid=(S // tm,),
        in_specs=[row, row, row, ublk, const, ANY, ANY],
        out_specs=[row, row, ublk, pl.BlockSpec((tm, 2 * FS), lambda i: (i, 0)), const],
        out_shape=[jax.ShapeDtypeStruct((S, D), F32), jax.ShapeDtypeStruct((S, D), BF16),
                   jax.ShapeDtypeStruct((N_CHIPS, S, FS), BF16), jax.ShapeDtypeStruct((S, 2 * FS), BF16),
                   jax.ShapeDtypeStruct((8, D), F32)],
        scratch_shapes=[pltpu.VMEM((N_CHIPS, D, FS), BF16), pltpu.VMEM((N_CHIPS * FR, D), BF16),
                        pltpu.SemaphoreType.DMA((2 * N_CHIPS,))],
        compiler_params=_cp(("arbitrary",)),
    )(dout, x, f, u, vec, win_g, wout_g)


def tn_matmul(a, b, tk, name):
    S, K = a.shape
    nb, _, tn = b.shape

    def body(a_ref, b_ref, o_ref):
        o_ref[...] = _dot_tn(a_ref[...], b_ref[...])

    return pl.pallas_call(
        body, name=name, grid=(K // tk, nb),
        in_specs=[pl.BlockSpec((S, tk), lambda i, j: (0, i)), pl.BlockSpec((None, S, tn), lambda i, j: (j, 0, 0))],
        out_specs=pl.BlockSpec((None, tk, tn), lambda i, j: (j, i, 0)),
        out_shape=jax.ShapeDtypeStruct((nb, K, tn), F32),
        compiler_params=_cp(("arbitrary", "arbitrary")),
    )(a, b)


def tn_matmul_cols(a, b, tk, tn, name):
    S, K = a.shape
    N = b.shape[1]

    def body(a_ref, b_ref, o_ref):
        o_ref[...] = _dot_tn(a_ref[...], b_ref[...])

    return pl.pallas_call(
        body, name=name, grid=(K // tk, N // tn),
        in_specs=[pl.BlockSpec((S, tk), lambda i, j: (0, i)), pl.BlockSpec((S, tn), lambda i, j: (0, j))],
        out_specs=pl.BlockSpec((None, tk, tn), lambda i, j: (j, i, 0)),
        out_shape=jax.ShapeDtypeStruct((N // tn, K, tn), F32),
        compiler_params=_cp(("arbitrary", "arbitrary")),
    )(a, b)


def pre_matmul(x, vec, w3, name, tm=256):
    S, D = x.shape
    nj, _, Nj = w3.shape

    def body(x_ref, vec_ref, w_ref, h_ref, z_ref):
        h = _pre_fwd(x_ref[...], vec_ref).astype(BF16)
        h_ref[...] = h
        for j in range(nj):
            z_ref[:, j * Nj:(j + 1) * Nj] = _dot(h, w_ref[j])

    row = pl.BlockSpec((tm, D), lambda i: (i, 0))
    return pl.pallas_call(
        body, name=name, grid=(S // tm,),
        in_specs=[row, pl.BlockSpec((8, D), lambda i: (0, 0)), pl.BlockSpec((nj, D, Nj), lambda i: (0, 0, 0))],
        out_specs=[row, pl.BlockSpec((tm, nj * Nj), lambda i: (i, 0))],
        out_shape=[jax.ShapeDtypeStruct((S, D), BF16), jax.ShapeDtypeStruct((S, nj * Nj), F32)],
        compiler_params=_cp(("arbitrary",)),
    )(x, vec, w3)


def matmul_pre_bwd(dz, w3, x, dres, vec, name, tm=256):
    S, D = x.shape
    nj, _, Nj = w3.shape

    def body(dz_ref, w_ref, x_ref, dres_ref, vec_ref, dx_ref, acc_ref):
        @pl.when(pl.program_id(0) == 0)
        def _():
            acc_ref[...] = jnp.zeros_like(acc_ref)

        dh = jnp.zeros((tm, D), F32)
        for j in range(nj):
            dh = dh + _dot_nt(dz_ref[:, j * Nj:(j + 1) * Nj], w_ref[j])
        dx, dshift, dscale, dgpre = _pre_bwd(x_ref[...], dh, vec_ref)
        dx_ref[...] = dres_ref[...] + dx
        _acc_add(acc_ref, A_SHIFT, dshift)
        _acc_add(acc_ref, A_SCALE, dscale)
        _acc_add(acc_ref, A_GPRE, dgpre)

    row = pl.BlockSpec((tm, D), lambda i: (i, 0))
    const = pl.BlockSpec((8, D), lambda i: (0, 0))
    return pl.pallas_call(
        body, name=name, grid=(S // tm,),
        in_specs=[pl.BlockSpec((tm, nj * Nj), lambda i: (i, 0)), pl.BlockSpec((nj, D, Nj), lambda i: (0, 0, 0)),
                  row, row, const],
        out_specs=[row, const],
        out_shape=[jax.ShapeDtypeStruct((S, D), F32), jax.ShapeDtypeStruct((8, D), F32)],
        compiler_params=_cp(("arbitrary",)),
    )(dz, w3, x, dres, vec)


def matmul_post(act, w, x, vec, name, tm=256):
    S, D = x.shape
    K = act.shape[1]

    def body(act_ref, w_ref, x_ref, vec_ref, out_ref, f_ref):
        f = _dot(act_ref[...], w_ref[...])
        f_ref[...] = f
        out_ref[...] = _post_fwd(x_ref[...], f, vec_ref, 1.0)

    row = pl.BlockSpec((tm, D), lambda i: (i, 0))
    return pl.pallas_call(
        body, name=name, grid=(S // tm,),
        in_specs=[pl.BlockSpec((tm, K), lambda i: (i, 0)), pl.BlockSpec((K, D), lambda i: (0, 0)), row,
                  pl.BlockSpec((8, D), lambda i: (0, 0))],
        out_specs=[row, row],
        out_shape=[jax.ShapeDtypeStruct((S, D), F32), jax.ShapeDtypeStruct((S, D), F32)],
        compiler_params=_cp(("arbitrary",)),
    )(act, w, x, vec)


def post_bwd_matmul(dout, f, vec, w, name, tm=256):
    S, D = dout.shape
    K = w.shape[0]

    def body(dout_ref, f_ref, vec_ref, w_ref, df_ref, dact_ref, acc_ref):
        @pl.when(pl.program_id(0) == 0)
        def _():
            acc_ref[...] = jnp.zeros_like(acc_ref)

        df, dgate, dgpost = _post_bwd(dout_ref[...], f_ref[...], vec_ref, 1.0)
        dfb = df.astype(BF16)
        df_ref[...] = dfb
        dact_ref[...] = _dot_nt(dfb, w_ref[...])
        _acc_add(acc_ref, A_GATE, dgate)
        _acc_add(acc_ref, A_GPOST, dgpost)

    row = pl.BlockSpec((tm, D), lambda i: (i, 0))
    const = pl.BlockSpec((8, D), lambda i: (0, 0))
    return pl.pallas_call(
        body, name=name, grid=(S // tm,),
        in_specs=[row, row, const, pl.BlockSpec((K, D), lambda i: (0, 0))],
        out_specs=[row, pl.BlockSpec((tm, K), lambda i: (i, 0)), const],
        out_shape=[jax.ShapeDtypeStruct((S, D), BF16), jax.ShapeDtypeStruct((S, K), F32),
                   jax.ShapeDtypeStruct((8, D), F32)],
        compiler_params=_cp(("arbitrary",)),
    )(dout, f, vec, w)


def _band(w, transposed, prev):
    r = lax.broadcasted_iota(jnp.int32, (CHUNK, CHUNK), 1 if transposed else 0)
    c = lax.broadcasted_iota(jnp.int32, (CHUNK, CHUNK), 0 if transposed else 1)
    d = r - c
    m = (d + CHUNK < w) if prev else ((d >= 0) & (d < w))
    return jnp.where(m, 1.0, 0.0).astype(BF16)


def _split_hi_lo(a):
    hi = a.astype(BF16)
    lo = (a - hi.astype(F32)).astype(BF16)
    return hi, lo


def _pool_diff(a, ap, g, denom):
    w = POOL_WINDOWS[g]
    a_hi, a_lo = _split_hi_lo(a)
    p_hi, p_lo = _split_hi_lo(ap)
    mc = _band(w, False, False)
    mp = _band(w, False, True)
    win = _dot(mc, a_hi) + _dot(mc, a_lo) + _dot(mp, p_hi) + _dot(mp, p_lo)
    return win / denom - a


def _sgu_norm(v, lg, lb):
    mu = _lanemean(v)
    xc = v - mu
    rstd = lax.rsqrt(_lanemean(xc * xc) + EPS)
    xhat = xc * rstd
    return xhat, rstd, xhat * lg + lb


def _tril_mask():
    r = lax.broadcasted_iota(jnp.int32, (CHUNK, CHUNK), 0)
    c = lax.broadcasted_iota(jnp.int32, (CHUNK, CHUNK), 1)
    return r >= c


def _positions(i, w):
    r = lax.broadcasted_iota(jnp.int32, (CHUNK, 128), 0)
    pos = (i * CHUNK + r + 1).astype(F32)
    return jnp.minimum(pos, float(w))


def mixa_core_fwd(z, pool_w, pool_scale, ln_g, ln_b, sgu_w, sgu_bexp, name):
    S = z.shape[0]
    W = z.shape[1] // 3
    G = len(POOL_WINDOWS)
    GD = W // G

    def body(zc_ref, zp_ref, pw_ref, ps_ref, lg_ref, lb_ref, sw_ref, sb_ref, y_ref):
        i = pl.program_id(0)
        has_prev = jnp.where(i > 0, 1.0, 0.0)
        for g in range(G):
            sl = slice(g * GD, (g + 1) * GD)
            a = zc_ref[:, sl]
            ap = zp_ref[:, sl] * has_prev
            d = _pool_diff(a, ap, g, _positions(i, POOL_WINDOWS[g])).astype(BF16)
            y_ref[:, sl] = (_dot(d, pw_ref[g]) * ps_ref[:, sl]).astype(BF16)
        tril = _tril_mask()
        for hh in range(G):
            u = _gelu(zc_ref[:, W + hh * GD:W + (hh + 1) * GD])
            v = _gelu(zc_ref[:, 2 * W + hh * GD:2 * W + (hh + 1) * GD])
            sl = slice(hh * GD, (hh + 1) * GD)
            _, _, vn = _sgu_norm(v, lg_ref[:, sl], lb_ref[:, sl])
            wm = jnp.where(tril, sw_ref[hh], 0.0).astype(BF16)
            s = _dot(wm, vn.astype(BF16)) + sb_ref[hh]
            y_ref[:, W + hh * GD:W + (hh + 1) * GD] = (u * s).astype(BF16)

    vecw = pl.BlockSpec((1, W), lambda i: (0, 0))
    mats = pl.BlockSpec((G, GD, GD), lambda i: (0, 0, 0))
    return pl.pallas_call(
        body, name=name, grid=(S // CHUNK,),
        in_specs=[pl.BlockSpec((CHUNK, 3 * W), lambda i: (i, 0)),
                  pl.BlockSpec((CHUNK, W), lambda i: (jnp.maximum(i - 1, 0), 0)),
                  mats, vecw, vecw, vecw, mats, mats],
        out_specs=pl.BlockSpec((CHUNK, 2 * W), lambda i: (i, 0)),
        out_shape=jax.ShapeDtypeStruct((S, 2 * W), BF16),
        compiler_params=_cp(("arbitrary",)),
    )(z, z, pool_w, pool_scale, ln_g, ln_b, sgu_w, sgu_bexp)


def mixa_core_bwd(z, dy, pool_w, pool_scale, ln_g, ln_b, sgu_w, sgu_bexp, name):
    S = z.shape[0]
    W = z.shape[1] // 3
    G = len(POOL_WINDOWS)
    GD = W // G
    n_tiles = S // CHUNK

    def body(zc_ref, zp_ref, dyc_ref, dyn_ref, pw_ref, ps_ref, lg_ref, lb_ref, sw_ref, sb_ref,
             dz_ref, dpw_ref, dvec_ref, dsw_ref, dsb_ref):
        i = pl.program_id(0)

        @pl.when(i == 0)
        def _():
            dpw_ref[...] = jnp.zeros_like(dpw_ref)
            dvec_ref[...] = jnp.zeros_like(dvec_ref)
            dsw_ref[...] = jnp.zeros_like(dsw_ref)
            dsb_ref[...] = jnp.zeros_like(dsb_ref)

        has_prev = jnp.where(i > 0, 1.0, 0.0)
        has_next = jnp.where(i < n_tiles - 1, 1.0, 0.0)
        for g in range(G):
            w = POOL_WINDOWS[g]
            sl = slice(g * GD, (g + 1) * GD)
            a = zc_ref[:, sl]
            ap = zp_ref[:, sl] * has_prev
            den_c = _positions(i, w)
            den_n = _positions(i + 1, w)
            d = _pool_diff(a, ap, g, den_c).astype(BF16)
            ps = ps_ref[:, sl]
            pw = pw_ref[g]
            dyc = dyc_ref[:, sl]
            dvec_ref[pl.ds(0, 1), sl] += _rowsum(dyc * _dot(d, pw))
            dyp_c = (dyc * ps).astype(BF16)
            dyp_n = (dyn_ref[:, sl] * (ps * has_next)).astype(BF16)
            dpw_ref[g] += _dot_tn(d, dyp_c)
            dd_c = _dot_nt(dyp_c, pw)
            dd_n = _dot_nt(dyp_n, pw)
            da = (_dot(_band(w, True, False), (dd_c / den_c).astype(BF16))
                  + _dot(_band(w, True, True), (dd_n / den_n).astype(BF16)) - dd_c)
            dz_ref[:, sl] = da.astype(BF16)
        tril = _tril_mask()
        for hh in range(G):
            sl = slice(hh * GD, (hh + 1) * GD)
            zu = zc_ref[:, W + hh * GD:W + (hh + 1) * GD]
            zv = zc_ref[:, 2 * W + hh * GD:2 * W + (hh + 1) * GD]
            u = _gelu(zu)
            v = _gelu(zv)
            lg = lg_ref[:, sl]
            xhat, rstd, vn = _sgu_norm(v, lg, lb_ref[:, sl])
            vnb = vn.astype(BF16)
            wm = jnp.where(tril, sw_ref[hh], 0.0).astype(BF16)
            s = _dot(wm, vnb) + sb_ref[hh]
            dyb = dyc_ref[:, W + hh * GD:W + (hh + 1) * GD]
            du = dyb * s
            ds = dyb * u
            dsb_ref[:, hh:hh + 1] += jnp.sum(ds, axis=1, keepdims=True)
            dsb16 = ds.astype(BF16)
            dsw_ref[hh] += jnp.where(tril, _dot_nt(dsb16, vnb), 0.0)
            dvn = _dot_tn(wm, dsb16)
            dvec_ref[pl.ds(1, 1), sl] += _rowsum(dvn * xhat)
            dvec_ref[pl.ds(2, 1), sl] += _rowsum(dvn)
            dxh = dvn * lg
            dv = rstd * (dxh - _lanemean(dxh) - xhat * _lanemean(dxh * xhat))
            dz_ref[:, W + hh * GD:W + (hh + 1) * GD] = (du * _gelu_grad(zu)).astype(BF16)
            dz_ref[:, 2 * W + hh * GD:2 * W + (hh + 1) * GD] = (dv * _gelu_grad(zv)).astype(BF16)

    vecw = pl.BlockSpec((1, W), lambda i: (0, 0))
    mats = pl.BlockSpec((G, GD, GD), lambda i: (0, 0, 0))
    return pl.pallas_call(
        body, name=name, grid=(n_tiles,),
        in_specs=[pl.BlockSpec((CHUNK, 3 * W), lambda i: (i, 0)),
                  pl.BlockSpec((CHUNK, W), lambda i: (jnp.maximum(i - 1, 0), 0)),
                  pl.BlockSpec((CHUNK, 2 * W), lambda i: (i, 0)),
                  pl.BlockSpec((CHUNK, W), lambda i: (jnp.minimum(i + 1, n_tiles - 1), 0)),
                  mats, vecw, vecw, vecw, mats, mats],
        out_specs=[pl.BlockSpec((CHUNK, 3 * W), lambda i: (i, 0)), mats,
                   pl.BlockSpec((8, W), lambda i: (0, 0)), mats, pl.BlockSpec((CHUNK, G), lambda i: (0, 0))],
        out_shape=[jax.ShapeDtypeStruct((S, 3 * W), BF16), jax.ShapeDtypeStruct((G, GD, GD), F32),
                   jax.ShapeDtypeStruct((8, W), F32), jax.ShapeDtypeStruct((G, CHUNK, CHUNK), F32),
                   jax.ShapeDtypeStruct((CHUNK, G), F32)],
        compiler_params=_cp(("arbitrary",)),
    )(z, z, dy, dy, pool_w, pool_scale, ln_g, ln_b, sgu_w, sgu_bexp)


def _cmul(ar, ai, br, bi):
    return ar * br - ai * bi, ar * bi + ai * br


def _cpow(ar, ai, n):
    rr, ri = None, None
    br, bi = ar, ai
    while n:
        if n & 1:
            rr, ri = (br, bi) if rr is None else _cmul(rr, ri, br, bi)
        n >>= 1
        if n:
            br, bi = _cmul(br, bi, br, bi)
    return rr, ri


def _seg_rows(k):
    return pl.ds(pl.multiple_of(k * SCAN_LANES, SCAN_LANES), SCAN_LANES)


def _scan_fwd(xre, xim, carry, ar, ai, K):
    P = xre.shape[1]

    def local(k, c):
        pr, pi = c
        rows = _seg_rows(k)
        nr = ar * pr - ai * pi + xre[rows, :]
        ni = ar * pi + ai * pr + xim[rows, :]
        xre[rows, :] = nr
        xim[rows, :] = ni
        return nr, ni

    er, ei = lax.fori_loop(1, K, local, (xre[pl.ds(0, SCAN_LANES), :], xim[pl.ds(0, SCAN_LANES), :]))
    akr, aki = _cpow(ar, ai, K)
    cr = jnp.zeros((1, P), F32)
    ci = jnp.zeros((1, P), F32)
    carry[pl.ds(0, 1), :] = cr
    carry[pl.ds(SCAN_LANES, 1), :] = ci
    for j in range(1, SCAN_LANES):
        tr, ti = _cmul(akr, aki, cr, ci)
        cr = er[j - 1:j, :] + tr
        ci = ei[j - 1:j, :] + ti
        carry[pl.ds(j, 1), :] = cr
        carry[pl.ds(SCAN_LANES + j, 1), :] = ci
    cmr = carry[pl.ds(0, SCAN_LANES), :]
    cmi = carry[pl.ds(SCAN_LANES, SCAN_LANES), :]

    def fix(k, c):
        pr, pi = c
        rows = _seg_rows(k)
        tr, ti = _cmul(pr, pi, cmr, cmi)
        xre[rows, :] += tr
        xim[rows, :] += ti
        return _cmul(pr, pi, ar, ai)

    lax.fori_loop(0, K, fix, (jnp.broadcast_to(ar, (SCAN_LANES, P)), jnp.broadcast_to(ai, (SCAN_LANES, P))))


def s5_core_fwd(u, ar, ai, bre, bim, cre_t, cim_t, dskip, name):
    S, D = u.shape
    nblk, UB, PB = bre.shape
    K = S // SCAN_LANES

    def body(u_ref, ar_ref, ai_ref, bre_ref, bim_ref, cre_ref, cim_ref, d_ref, y_ref, xre, xim, carry):
        uv = u_ref[...]
        ub = uv.astype(BF16)
        xre[...] = _dot(ub, bre_ref[...])
        xim[...] = _dot(ub, bim_ref[...])
        _scan_fwd(xre, xim, carry, ar_ref[...], ai_ref[...], K)
        y_ref[...] = (_dot(xre[...].astype(BF16), cre_ref[...]) - _dot(xim[...].astype(BF16), cim_ref[...])
                      + d_ref[...] * uv)

    ucol = pl.BlockSpec((S, UB), lambda i: (0, i))
    pvec = pl.BlockSpec((None, 1, PB), lambda i: (i, 0, 0))
    bmat = pl.BlockSpec((None, UB, PB), lambda i: (i, 0, 0))
    cmat = pl.BlockSpec((None, PB, UB), lambda i: (i, 0, 0))
    return pl.pallas_call(
        body, name=name, grid=(nblk,),
        in_specs=[ucol, pvec, pvec, bmat, bmat, cmat, cmat, pl.BlockSpec((1, UB), lambda i: (0, i))],
        out_specs=ucol,
        out_shape=jax.ShapeDtypeStruct((S, D), F32),
        scratch_shapes=[pltpu.VMEM((S, PB), F32), pltpu.VMEM((S, PB), F32), pltpu.VMEM((2 * SCAN_LANES, PB), F32)],
        compiler_params=_cp(("arbitrary",)),
    )(u, ar, ai, bre, bim, cre_t, cim_t, dskip)


def s5_core_bwd(u, dy, ar, ai, bre, bim, cre, cim, dskip, name):
    S, D = u.shape
    nblk, UB, PB = bre.shape
    K = S // SCAN_LANES

    def body(u_ref, dy_ref, ar_ref, ai_ref, bre_ref, bim_ref, cre_ref, cim_ref, d_ref,
             du_ref, dbre_ref, dbim_ref, dcre_ref, dcim_ref, dar_ref, dai_ref, dd_ref,
             xre, xim, gre, gim, carry, carry_b):
        ar = ar_ref[...]
        ai = ai_ref[...]
        uv = u_ref[...]
        ub = uv.astype(BF16)
        dyv = dy_ref[...]
        dyb = dyv.astype(BF16)
        xre[...] = _dot(ub, bre_ref[...])
        xim[...] = _dot(ub, bim_ref[...])
        _scan_fwd(xre, xim, carry, ar, ai, K)
        dcre_ref[...] = _dot_tn(xre[...].astype(BF16), dyb)
        dcim_ref[...] = -_dot_tn(xim[...].astype(BF16), dyb)
        gre[...] = _dot(dyb, cre_ref[...])
        gim[...] = -_dot(dyb, cim_ref[...])

        def local(s, c):
            k = K - 2 - s
            nr, ni = c
            rows = _seg_rows(k)
            tr = gre[rows, :] + ar * nr + ai * ni
            ti = gim[rows, :] + ar * ni - ai * nr
            gre[rows, :] = tr
            gim[rows, :] = ti
            return tr, ti

        last = _seg_rows(K - 1)
        fr, fi = lax.fori_loop(0, K - 1, local, (gre[last, :], gim[last, :]))
        akr, aki = _cpow(ar, -ai, K)
        cr = jnp.zeros((1, PB), F32)
        ci = jnp.zeros((1, PB), F32)
        carry_b[pl.ds(SCAN_LANES - 1, 1), :] = cr
        carry_b[pl.ds(2 * SCAN_LANES - 1, 1), :] = ci
        for j in range(SCAN_LANES - 2, -1, -1):
            tr, ti = _cmul(akr, aki, cr, ci)
            cr = fr[j + 1:j + 2, :] + tr
            ci = fi[j + 1:j + 2, :] + ti
            carry_b[pl.ds(j, 1), :] = cr
            carry_b[pl.ds(SCAN_LANES + j, 1), :] = ci
        cbr = carry_b[pl.ds(0, SCAN_LANES), :]
        cbi = carry_b[pl.ds(SCAN_LANES, SCAN_LANES), :]

        def fix(s, c):
            k = K - 1 - s
            pr, pi, dar, dai = c
            rows = _seg_rows(k)
            tr, ti = _cmul(pr, pi, cbr, cbi)
            g_r = gre[rows, :] + tr
            g_i = gim[rows, :] + ti
            gre[rows, :] = g_r
            gim[rows, :] = g_i
            prev = _seg_rows(jnp.maximum(k - 1, 0))
            first = k == 0
            xr = jnp.where(first, carry[pl.ds(0, SCAN_LANES), :], xre[prev, :])
            xi = jnp.where(first, carry[pl.ds(SCAN_LANES, SCAN_LANES), :], xim[prev, :])
            dar = dar + g_r * xr + g_i * xi
            dai = dai + g_i * xr - g_r * xi
            nr, ni = _cmul(pr, pi, ar, -ai)
            return nr, ni, dar, dai

        z8 = jnp.zeros((SCAN_LANES, PB), F32)
        _, _, dar, dai = lax.fori_loop(
            0, K, fix, (jnp.broadcast_to(ar, (SCAN_LANES, PB)), jnp.broadcast_to(-ai, (SCAN_LANES, PB)), z8, z8))
        dar_ref[...] = _rowsum(dar)
        dai_ref[...] = _rowsum(dai)
        grb = gre[...].astype(BF16)
        gib = gim[...].astype(BF16)
        dbre_ref[...] = _dot_tn(ub, grb)
        dbim_ref[...] = _dot_tn(ub, gib)
        du_ref[...] = _dot_nt(grb, bre_ref[...]) + _dot_nt(gib, bim_ref[...]) + d_ref[...] * dyv
        dd_ref[...] = _rowsum(dyv * uv)

    ucol = pl.BlockSpec((S, UB), lambda i: (0, i))
    pvec = pl.BlockSpec((None, 1, PB), lambda i: (i, 0, 0))
    bmat = pl.BlockSpec((None, UB, PB), lambda i: (i, 0, 0))
    cmat = pl.BlockSpec((None, PB, UB), lambda i: (i, 0, 0))
    dvec = pl.BlockSpec((1, UB), lambda i: (0, i))
    return pl.pallas_call(
        body, name=name, grid=(nblk,),
        in_specs=[ucol, ucol, pvec, pvec, bmat, bmat, bmat, bmat, dvec],
        out_specs=[ucol, bmat, bmat, cmat, cmat, pvec, pvec, dvec],
        out_shape=[jax.ShapeDtypeStruct((S, D), F32),
                   jax.ShapeDtypeStruct((nblk, UB, PB), F32), jax.ShapeDtypeStruct((nblk, UB, PB), F32),
                   jax.ShapeDtypeStruct((nblk, PB, UB), F32), jax.ShapeDtypeStruct((nblk, PB, UB), F32),
                   jax.ShapeDtypeStruct((nblk, 1, PB), F32), jax.ShapeDtypeStruct((nblk, 1, PB), F32),
                   jax.ShapeDtypeStruct((1, D), F32)],
        scratch_shapes=[pltpu.VMEM((S, PB), F32), pltpu.VMEM((S, PB), F32), pltpu.VMEM((S, PB), F32),
                        pltpu.VMEM((S, PB), F32), pltpu.VMEM((2 * SCAN_LANES, PB), F32),
                        pltpu.VMEM((2 * SCAN_LANES, PB), F32)],
        compiler_params=_cp(("arbitrary",)),
    )(u, dy, ar, ai, bre, bim, cre, cim, dskip)


def s5_glu_fwd(y, wglu, x, vec, name, tm=256):
    S, D = x.shape
    NJ = wglu.shape[-1]

    def body(y_ref, w_ref, x_ref, vec_ref, out_ref, f_ref):
        g = _gelu(y_ref[...]).astype(BF16)
        f = jnp.concatenate([_dot(g, w_ref[j]) * _sigmoid(_dot(g, w_ref[2 + j])) for j in range(2)], axis=1)
        f_ref[...] = f
        out_ref[...] = _post_fwd(x_ref[...], f, vec_ref, 1.0)

    row = pl.BlockSpec((tm, D), lambda i: (i, 0))
    return pl.pallas_call(
        body, name=name, grid=(S // tm,),
        in_specs=[row, pl.BlockSpec((N_CHIPS, D, NJ), lambda i: (0, 0, 0)), row, pl.BlockSpec((8, D), lambda i: (0, 0))],
        out_specs=[row, row],
        out_shape=[jax.ShapeDtypeStruct((S, D), F32), jax.ShapeDtypeStruct((S, D), F32)],
        compiler_params=_cp(("arbitrary",)),
    )(y, wglu, x, vec)


def s5_glu_bwd(dout, f, y, vec, wglu, name, tm=256):
    S, D = dout.shape
    NJ = wglu.shape[-1]

    def body(dout_ref, f_ref, y_ref, vec_ref, w_ref, dy_ref, dab_ref, g_ref, acc_ref):
        @pl.when(pl.program_id(0) == 0)
        def _():
            acc_ref[...] = jnp.zeros_like(acc_ref)

        df, dgate, dgpost = _post_bwd(dout_ref[...], f_ref[...], vec_ref, 1.0)
        yv = y_ref[...]
        g = _gelu(yv).astype(BF16)
        g_ref[...] = g
        dg = jnp.zeros((tm, D), F32)
        for j in range(2):
            a = _dot(g, w_ref[j])
            sig = _sigmoid(_dot(g, w_ref[2 + j]))
            dfj = df[:, j * NJ:(j + 1) * NJ]
            da = (dfj * sig).astype(BF16)
            db = (dfj * a * sig * (1.0 - sig)).astype(BF16)
            dab_ref[:, j * NJ:(j + 1) * NJ] = da
            dab_ref[:, (2 + j) * NJ:(3 + j) * NJ] = db
            dg = dg + _dot_nt(da, w_ref[j]) + _dot_nt(db, w_ref[2 + j])
        dy_ref[...] = dg * _gelu_grad(yv)
        _acc_add(acc_ref, A_GATE, dgate)
        _acc_add(acc_ref, A_GPOST, dgpost)

    row = pl.BlockSpec((tm, D), lambda i: (i, 0))
    const = pl.BlockSpec((8, D), lambda i: (0, 0))
    return pl.pallas_call(
        body, name=name, grid=(S // tm,),
        in_specs=[row, row, row, const, pl.BlockSpec((N_CHIPS, D, NJ), lambda i: (0, 0, 0))],
        out_specs=[row, pl.BlockSpec((tm, N_CHIPS * NJ), lambda i: (i, 0)), row, const],
        out_shape=[jax.ShapeDtypeStruct((S, D), F32), jax.ShapeDtypeStruct((S, N_CHIPS * NJ), BF16),
                   jax.ShapeDtypeStruct((S, D), BF16), jax.ShapeDtypeStruct((8, D), F32)],
        compiler_params=_cp(("arbitrary",)),
    )(dout, f, y, vec, wglu)


def loss_head(y, target, name, tm=256):
    S, D = y.shape

    def body(y_ref, t_ref, d_ref, l_ref):
        @pl.when(pl.program_id(0) == 0)
        def _():
            l_ref[...] = jnp.zeros_like(l_ref)

        err = y_ref[...] - t_ref[...]
        d_ref[...] = err * (1.0 / D)
        l_ref[...] += jnp.sum(_rowsum(err * err), axis=1, keepdims=True)

    row = pl.BlockSpec((tm, D), lambda i: (i, 0))
    return pl.pallas_call(
        body, name=name, grid=(S // tm,),
        in_specs=[row, row],
        out_specs=[row, pl.BlockSpec((1, 1), lambda i: (0, 0))],
        out_shape=[jax.ShapeDtypeStruct((S, D), F32), jax.ShapeDtypeStruct((1, 1), F32)],
        compiler_params=_cp(("arbitrary",)),
    )(y, target)


def ada_mod(c_all, ada_w, ada_b_shard, name, tn=768):
    B, D = c_all.shape
    L, _, NS = ada_w.shape

    def body(c_ref, w_ref, b_ref, o_ref):
        cv = c_ref[...]
        cond = (cv * _sigmoid(cv)).astype(BF16)
        o_ref[...] = _dot(cond, w_ref[...].astype(BF16)) + b_ref[...]

    return pl.pallas_call(
        body, name=name, grid=(L, NS // tn),
        in_specs=[pl.BlockSpec((B, D), lambda l, j: (0, 0)), pl.BlockSpec((None, D, tn), lambda l, j: (l, 0, j)),
                  pl.BlockSpec((None, 1, tn), lambda l, j: (l, 0, j))],
        out_specs=pl.BlockSpec((None, B, tn), lambda l, j: (l, 0, j)),
        out_shape=jax.ShapeDtypeStruct((L, B, NS), F32),
        compiler_params=_cp(("arbitrary", "arbitrary")),
    )(c_all, ada_w, ada_b_shard)


def ada_grad(c_all, dmod, name, tn=768):
    B, D = c_all.shape
    L, _, NS = dmod.shape

    def body(c_ref, d_ref, o_ref):
        cv = c_ref[...]
        cond = (cv * _sigmoid(cv)).astype(BF16)
        o_ref[...] = _dot_tn(cond, d_ref[...].astype(BF16))

    return pl.pallas_call(
        body, name=name, grid=(L, NS // tn),
        in_specs=[pl.BlockSpec((B, D), lambda l, j: (0, 0)), pl.BlockSpec((None, B, tn), lambda l, j: (l, 0, j))],
        out_specs=pl.BlockSpec((None, D, tn), lambda l, j: (l, 0, j)),
        out_shape=jax.ShapeDtypeStruct((L, D, NS), F32),
        compiler_params=_cp(("arbitrary", "arbitrary")),
    )(c_all, dmod)


def sum_leading(x, name):
    n, R, C = x.shape

    def body(x_ref, o_ref):
        acc = x_ref[0]
        for i in range(1, n):
            acc = acc + x_ref[i]
        o_ref[...] = acc

    tr = _row_tile(R, 128)
    return pl.pallas_call(
        body, name=name, grid=(R // tr,),
        in_specs=[pl.BlockSpec((n, tr, C), lambda i: (0, i, 0))],
        out_specs=pl.BlockSpec((tr, C), lambda i: (i, 0)),
        out_shape=jax.ShapeDtypeStruct((R, C), F32),
        compiler_params=_cp(("arbitrary",)),
    )(x)


def _row_tile(R, cap=512):
    for cand in (512, 256, 128, 64, 32, 16, 8):
        if cand <= cap and R % cand == 0:
            return cand
    return R


def adamw(w, g, m, v, name):
    R, C = w.shape
    tr = _row_tile(R, 256 if C > 1024 else 512)
    bc1 = 1.0 - ADAM_B1 ** ADAM_STEP
    bc2 = 1.0 - ADAM_B2 ** ADAM_STEP

    def body(w_ref, g_ref, m_ref, v_ref, d_ref, nm_ref, nv_ref):
        gv = g_ref[...]
        nm = ADAM_B1 * m_ref[...] + (1.0 - ADAM_B1) * gv
        nv = ADAM_B2 * v_ref[...] + (1.0 - ADAM_B2) * (gv * gv)
        nm_ref[...] = nm
        nv_ref[...] = nv
        d_ref[...] = -ADAM_LR * ((nm / bc1) / (jnp.sqrt(nv / bc2) + ADAM_EPS) + ADAM_WD * w_ref[...])

    blk = pl.BlockSpec((tr, C), lambda i: (i, 0))
    sd = jax.ShapeDtypeStruct((R, C), F32)
    return pl.pallas_call(
        body, name=name, grid=(R // tr,),
        in_specs=[blk, blk, blk, blk], out_specs=[blk, blk, blk], out_shape=[sd, sd, sd],
        compiler_params=_cp(("arbitrary",)),
    )(w, g, m, v)


def _as2d(a):
    if a.ndim == 1:
        return a.reshape(1, -1)
    return a.reshape(-1, a.shape[-1])


def adamw_nd(w, g, m, v, name):
    outs = adamw(_as2d(w), _as2d(g.reshape(w.shape)), _as2d(m), _as2d(v), name)
    return tuple(o.reshape(w.shape) for o in outs)


def _place():
    x, y, c = lax.axis_index("x"), lax.axis_index("y"), lax.axis_index("c")
    chips = [(1 - x, y), (x, 1 - y), (1 - x, 1 - y)]
    return x, y, c, chips


def allgather_small(xs, name):
    m_per, n = xs.shape

    def body(x_ref, out_ref, send_sems, recv_sems, local_sem):
        x, y, c, chips = _place()
        me, sibling = (x, y, c), (x, y, 1 - c)

        def rows(px, py, pc):
            return out_ref.at[pl.ds((4 * px + 2 * py + pc) * m_per, m_per), :]

        def copy(k, block, to, src=None):
            return pltpu.make_async_remote_copy(
                src_ref=rows(*block) if src is None else src, dst_ref=rows(*block),
                send_sem=send_sems.at[k], recv_sem=recv_sems.at[k], device_id=to, device_id_type=MESH_T)

        mine = pltpu.make_async_copy(x_ref, rows(*me), local_sem)
        mine.start()
        first = [copy(0, me, sibling, src=x_ref)]
        first += [copy(1 + j, me, (*chip, c), src=x_ref) for j, chip in enumerate(chips)]
        for cp in first:
            cp.start()
        passed = [copy(4 + j, (*chip, c), sibling) for j, chip in enumerate(chips)]
        for j, chip in enumerate(chips):
            copy(1 + j, (*chip, c), me).wait_recv()
            passed[j].start()
        copy(0, sibling, me).wait_recv()
        for j, chip in enumerate(chips):
            copy(4 + j, (*chip, 1 - c), me).wait_recv()
        for cp in first + passed:
            cp.wait_send()
        mine.wait()

    return pl.pallas_call(
        body, name=name,
        out_shape=jax.ShapeDtypeStruct((N_DEV * m_per, n), xs.dtype),
        in_specs=[pl.BlockSpec(memory_space=pltpu.VMEM)],
        out_specs=pl.BlockSpec(memory_space=pltpu.VMEM),
        scratch_shapes=[pltpu.SemaphoreType.DMA((7,)), pltpu.SemaphoreType.DMA((7,)), pltpu.SemaphoreType.DMA],
        compiler_params=_cp(),
    )(xs)


def gather_weights(shards, name):
    n = len(shards)

    def body(*refs):
        ins, outs = refs[:n], refs[n:2 * n]
        send_sems, recv_sems, fsend_sems, frecv_sems, local_sems = refs[2 * n:]
        x, y, c, chips = _place()
        me_chip = 2 * x + y
        sibling = (x, y, 1 - c)
        locs = []
        for a in range(n):
            cp = pltpu.make_async_copy(ins[a], outs[a].at[me_chip], local_sems.at[a])
            cp.start()
            locs.append(cp)

        def ici(a, k, src_chip, to):
            return pltpu.make_async_remote_copy(
                src_ref=ins[a].at[c], dst_ref=outs[a].at[src_chip, c],
                send_sem=send_sems.at[a * 3 + k], recv_sem=recv_sems.at[a * 3 + k], device_id=to, device_id_type=MESH_T)

        def d2d(a, k, src_chip, half):
            return pltpu.make_async_remote_copy(
                src_ref=outs[a].at[src_chip, half], dst_ref=outs[a].at[src_chip, half],
                send_sem=fsend_sems.at[a * 3 + k], recv_sem=frecv_sems.at[a * 3 + k],
                device_id=sibling, device_id_type=MESH_T)

        firsts = []
        for k, (cx, cy) in enumerate(chips):
            for a in range(n):
                cp = ici(a, k, me_chip, (cx, cy, c))
                cp.start()
                firsts.append(cp)
        passed = []
        for k, (cx, cy) in enumerate(chips):
            for a in range(n):
                ici(a, k, 2 * cx + cy, (cx, cy, c)).wait_recv()
                cp = d2d(a, k, 2 * cx + cy, c)
                cp.start()
                passed.append(cp)
        for k, (cx, cy) in enumerate(chips):
            for a in range(n):
                d2d(a, k, 2 * cx + cy, 1 - c).wait_recv()
        for cp in firsts + passed:
            cp.wait_send()
        for cp in locs:
            cp.wait()

    return pl.pallas_call(
        body, name=name,
        out_shape=[jax.ShapeDtypeStruct((N_CHIPS,) + s.shape, s.dtype) for s in shards],
        in_specs=[ANY] * n, out_specs=[ANY] * n,
        scratch_shapes=[pltpu.SemaphoreType.DMA((3 * n,)), pltpu.SemaphoreType.DMA((3 * n,)),
                        pltpu.SemaphoreType.DMA((3 * n,)), pltpu.SemaphoreType.DMA((3 * n,)),
                        pltpu.SemaphoreType.DMA((n,))],
        compiler_params=_cp(),
    )(*shards)


def _half_rows(ref, half, rh):
    idx = (slice(None),) * (len(ref.shape) - 2) + (pl.ds(pl.multiple_of(half * rh, 8), rh), slice(None))
    return ref.at[idx]


def sibling_exchange_halves(grads, name):
    n = len(grads)

    def body(*refs):
        ins, outs = refs[:n], refs[n:2 * n]
        send_sems, recv_sems = refs[2 * n:]
        x, y, c, _ = _place()
        cps = []
        for a in range(n):
            rh = ins[a].shape[-2] // 2
            cp = pltpu.make_async_remote_copy(
                src_ref=_half_rows(ins[a], 1 - c, rh), dst_ref=outs[a],
                send_sem=send_sems.at[a], recv_sem=recv_sems.at[a], device_id=(x, y, 1 - c), device_id_type=MESH_T)
            cp.start()
            cps.append(cp)
        for cp in cps:
            cp.wait()

    return pl.pallas_call(
        body, name=name,
        out_shape=[jax.ShapeDtypeStruct(g.shape[:-2] + (g.shape[-2] // 2, g.shape[-1]), g.dtype) for g in grads],
        in_specs=[ANY] * n, out_specs=[ANY] * n,
        scratch_shapes=[pltpu.SemaphoreType.DMA((n,)), pltpu.SemaphoreType.DMA((n,))],
        compiler_params=_cp(),
    )(*grads)


def pair_sum(g, recv, cidx, name):
    L, NS, R, C = g.shape
    rh = R // 2
    tr = _row_tile(rh, 256 if C > 1024 else 512)
    nt = rh // tr

    def body(c_ref, g_ref, r_ref, o_ref):
        o_ref[...] = (g_ref[...] + r_ref[...]).astype(BF16)

    grid_spec = pltpu.PrefetchScalarGridSpec(
        num_scalar_prefetch=1, grid=(L, NS, nt),
        in_specs=[pl.BlockSpec((None, None, tr, C), lambda l, s, t, c: (l, s, c[0] * nt + t, 0)),
                  pl.BlockSpec((None, None, tr, C), lambda l, s, t, c: (l, s, t, 0))],
        out_specs=pl.BlockSpec((None, None, tr, C), lambda l, s, t, c: (l, s, t, 0)))
    return pl.pallas_call(
        body, name=name, grid_spec=grid_spec,
        out_shape=jax.ShapeDtypeStruct((L, NS, rh, C), BF16),
        compiler_params=_cp(("arbitrary", "arbitrary", "arbitrary")),
    )(cidx, g, recv)


def chip_exchange(parts, name):
    n = len(parts)

    def body(*refs):
        ins, outs = refs[:n], refs[n:2 * n]
        send_sems, recv_sems = refs[2 * n:]
        x, y, c, chips = _place()
        cps = []
        for k, (cx, cy) in enumerate(chips):
            for a in range(n):
                cp = pltpu.make_async_remote_copy(
                    src_ref=ins[a].at[:, 2 * cx + cy], dst_ref=outs[a].at[k],
                    send_sem=send_sems.at[a * 3 + k], recv_sem=recv_sems.at[a * 3 + k],
                    device_id=(cx, cy, c), device_id_type=MESH_T)
                cp.start()
                cps.append(cp)
        for cp in cps:
            cp.wait()

    return pl.pallas_call(
        body, name=name,
        out_shape=[jax.ShapeDtypeStruct((3, p.shape[0]) + p.shape[2:], p.dtype) for p in parts],
        in_specs=[ANY] * n, out_specs=[ANY] * n,
        scratch_shapes=[pltpu.SemaphoreType.DMA((3 * n,)), pltpu.SemaphoreType.DMA((3 * n,))],
        compiler_params=_cp(),
    )(*parts)


def chip_sum(part, recv, chip_c, name):
    L, NS, RH, C = part.shape
    tr = _row_tile(RH, 256 if C > 1024 else 512)
    nt = RH // tr

    def body(cc_ref, p_ref, r_ref, o_ref):
        acc = p_ref[...].astype(F32)
        for k in range(3):
            acc = acc + r_ref[k].astype(F32)
        o_ref[...] = acc

    grid_spec = pltpu.PrefetchScalarGridSpec(
        num_scalar_prefetch=1, grid=(L, nt),
        in_specs=[pl.BlockSpec((None, None, tr, C), lambda l, t, cc: (l, cc[0], t, 0)),
                  pl.BlockSpec((3, None, tr, C), lambda l, t, cc: (0, l, t, 0))],
        out_specs=pl.BlockSpec((None, tr, C), lambda l, t, cc: (l, cc[1] * nt + t, 0)))
    return pl.pallas_call(
        body, name=name, grid_spec=grid_spec,
        out_shape=jax.ShapeDtypeStruct((L, 2 * RH, C), F32),
        compiler_params=_cp(("arbitrary", "arbitrary")),
    )(chip_c, part, recv)


def sibling_fill_halves(bufs, name):
    n = len(bufs)

    def body(*refs):
        outs = refs[n:2 * n]
        send_sems, recv_sems = refs[2 * n:]
        x, y, c, _ = _place()
        cps = []
        for a in range(n):
            rh = outs[a].shape[-2] // 2
            cp = pltpu.make_async_remote_copy(
                src_ref=_half_rows(outs[a], c, rh), dst_ref=_half_rows(outs[a], c, rh),
                send_sem=send_sems.at[a], recv_sem=recv_sems.at[a], device_id=(x, y, 1 - c), device_id_type=MESH_T)
            cp.start()
            cps.append(cp)
        for a, cp in enumerate(cps):
            cp.wait_send()
        for a in range(n):
            rh = outs[a].shape[-2] // 2
            pltpu.make_async_remote_copy(
                src_ref=_half_rows(outs[a], 1 - c, rh), dst_ref=_half_rows(outs[a], 1 - c, rh),
                send_sem=send_sems.at[a], recv_sem=recv_sems.at[a], device_id=(x, y, 1 - c),
                device_id_type=MESH_T).wait_recv()

    return pl.pallas_call(
        body, name=name,
        out_shape=[jax.ShapeDtypeStruct(b.shape, b.dtype) for b in bufs],
        in_specs=[ANY] * n, out_specs=[ANY] * n,
        input_output_aliases={a: a for a in range(n)},
        scratch_shapes=[pltpu.SemaphoreType.DMA((n,)), pltpu.SemaphoreType.DMA((n,))],
        compiler_params=_cp(),
    )(*bufs)


def _s5_prepare(lam_re, lam_im, log_dt, b_re, b_im, c_re, c_im, groups_per_block):
    G, P = lam_re.shape
    N = b_re.shape[-1]
    gb = groups_per_block
    nblk = G // gb
    dt = jnp.exp(log_dt)[:, None]
    e = jnp.exp(lam_re * dt)
    a_re = e * jnp.cos(lam_im * dt)
    a_im = e * jnp.sin(lam_im * dt)
    n2 = lam_re * lam_re + lam_im * lam_im
    co_re = ((a_re - 1.0) * lam_re + a_im * lam_im) / n2
    co_im = (a_im * lam_re - (a_re - 1.0) * lam_im) / n2
    bb_re = co_re[..., None] * b_re - co_im[..., None] * b_im
    bb_im = co_re[..., None] * b_im + co_im[..., None] * b_re
    eye = jnp.eye(gb, dtype=F32)

    def blockdiag_np(m):
        m = m.reshape(nblk, gb, N, P)
        return jnp.einsum('bgnp,gh->bgnhp', m, eye).reshape(nblk, gb * N, gb * P)

    b_np_re = jnp.swapaxes(bb_re, 1, 2)
    b_np_im = jnp.swapaxes(bb_im, 1, 2)
    return (a_re.reshape(nblk, 1, gb * P), a_im.reshape(nblk, 1, gb * P),
            blockdiag_np(b_np_re), blockdiag_np(b_np_im), blockdiag_np(c_re), blockdiag_np(c_im))


def _to_scan_order(a):
    S, D = a.shape
    return a.reshape(SCAN_LANES, S // SCAN_LANES, D).transpose(1, 0, 2).reshape(S, D)


def _from_scan_order(a):
    S, D = a.shape
    return a.reshape(S // SCAN_LANES, SCAN_LANES, D).transpose(1, 0, 2).reshape(S, D)


def _pad_rows(a, mult=8):
    r = (-a.shape[0]) % mult
    if r:
        a = jnp.concatenate([a, jnp.zeros((r, a.shape[1]), a.dtype)], axis=0)
    return a


def _pack_rows(arrs, width):
    parts, offs, o = [], [], 0
    for a in arrs:
        flat = a.reshape(-1)
        r = (-flat.shape[0]) % (8 * width)
        if r:
            flat = jnp.concatenate([flat, jnp.zeros((r,), flat.dtype)])
        p = flat.reshape(-1, width)
        parts.append(p)
        offs.append((o, a.shape, a.size))
        o += p.shape[0]
    return jnp.concatenate(parts, axis=0), offs


def _unpack_rows(packed, offs):
    outs = []
    for o, shape, size in offs:
        rows = -(-size // packed.shape[1])
        outs.append(packed[o:o + rows].reshape(-1)[:size].reshape(shape))
    return outs


def kernel(x, c, ada_w, ada_b, norm_pre, norm_post, ffn_w_in, ffn_w_out, ab_w_in, pool_w, pool_scale, sgu_ln_g, sgu_ln_b, sgu_w, sgu_b, ab_w_out, ssm_w_in, ssm_lam_re, ssm_lam_im, ssm_b_re, ssm_b_im, ssm_c_re, ssm_c_im, ssm_d, ssm_log_dt, ssm_w_glu, loss_target, m_ada_w, m_ada_b, m_norm_pre, m_norm_post, m_ffn_w_in, m_ffn_w_out, m_ab_w_in, m_pool_w, m_pool_scale, m_sgu_ln_g, m_sgu_ln_b, m_sgu_w, m_sgu_b, m_ab_w_out, m_ssm_w_in, m_ssm_lam_re, m_ssm_lam_im, m_ssm_b_re, m_ssm_b_im, m_ssm_c_re, m_ssm_c_im, m_ssm_d, m_ssm_log_dt, m_ssm_w_glu, v_ada_w, v_ada_b, v_norm_pre, v_norm_post, v_ffn_w_in, v_ffn_w_out, v_ab_w_in, v_pool_w, v_pool_scale, v_sgu_ln_g, v_sgu_ln_b, v_sgu_w, v_sgu_b, v_ab_w_out, v_ssm_w_in, v_ssm_lam_re, v_ssm_lam_im, v_ssm_b_re, v_ssm_b_im, v_ssm_c_re, v_ssm_c_im, v_ssm_d, v_ssm_log_dt, v_ssm_w_glu):
    weights = dict(ada_w=ada_w, ada_b=ada_b, norm_pre=norm_pre, norm_post=norm_post, ffn_w_in=ffn_w_in,
                   ffn_w_out=ffn_w_out, ab_w_in=ab_w_in, pool_w=pool_w, pool_scale=pool_scale, sgu_ln_g=sgu_ln_g,
                   sgu_ln_b=sgu_ln_b, sgu_w=sgu_w, sgu_b=sgu_b, ab_w_out=ab_w_out, ssm_w_in=ssm_w_in,
                   ssm_lam_re=ssm_lam_re, ssm_lam_im=ssm_lam_im, ssm_b_re=ssm_b_re, ssm_b_im=ssm_b_im,
                   ssm_c_re=ssm_c_re, ssm_c_im=ssm_c_im, ssm_d=ssm_d, ssm_log_dt=ssm_log_dt, ssm_w_glu=ssm_w_glu)
    m_in = dict(ada_w=m_ada_w, ada_b=m_ada_b, norm_pre=m_norm_pre, norm_post=m_norm_post, ffn_w_in=m_ffn_w_in,
                ffn_w_out=m_ffn_w_out, ab_w_in=m_ab_w_in, pool_w=m_pool_w, pool_scale=m_pool_scale,
                sgu_ln_g=m_sgu_ln_g, sgu_ln_b=m_sgu_ln_b, sgu_w=m_sgu_w, sgu_b=m_sgu_b, ab_w_out=m_ab_w_out,
                ssm_w_in=m_ssm_w_in, ssm_lam_re=m_ssm_lam_re, ssm_lam_im=m_ssm_lam_im, ssm_b_re=m_ssm_b_re,
                ssm_b_im=m_ssm_b_im, ssm_c_re=m_ssm_c_re, ssm_c_im=m_ssm_c_im, ssm_d=m_ssm_d,
                ssm_log_dt=m_ssm_log_dt, ssm_w_glu=m_ssm_w_glu)
    v_in = dict(ada_w=v_ada_w, ada_b=v_ada_b, norm_pre=v_norm_pre, norm_post=v_norm_post, ffn_w_in=v_ffn_w_in,
                ffn_w_out=v_ffn_w_out, ab_w_in=v_ab_w_in, pool_w=v_pool_w, pool_scale=v_pool_scale,
                sgu_ln_g=v_sgu_ln_g, sgu_ln_b=v_sgu_ln_b, sgu_w=v_sgu_w, sgu_b=v_sgu_b, ab_w_out=v_ab_w_out,
                ssm_w_in=v_ssm_w_in, ssm_lam_re=v_ssm_lam_re, ssm_lam_im=v_ssm_lam_im, ssm_b_re=v_ssm_b_re,
                ssm_b_im=v_ssm_b_im, ssm_c_re=v_ssm_c_re, ssm_c_im=v_ssm_c_im, ssm_d=v_ssm_d,
                ssm_log_dt=v_ssm_log_dt, ssm_w_glu=v_ssm_w_glu)
    names = list(weights.keys())

    xi, yi, ci = lax.axis_index("x"), lax.axis_index("y"), lax.axis_index("c")
    chip = 2 * xi + yi
    me = 4 * xi + 2 * yi + ci
    S, D = x.shape[1], x.shape[2]
    L = ada_w.shape[0]
    NSUB = norm_pre.shape[1]
    DS = norm_pre.shape[2]
    FS = ffn_w_in.shape[-1]
    FR = ffn_w_out.shape[-2]
    x0 = x[0]
    target = loss_target[0]

    small = jnp.concatenate([c.reshape(D // DS, DS), norm_pre.reshape(L * NSUB, DS), norm_post.reshape(L * NSUB, DS),
                             ssm_d.reshape(1, DS)], axis=0)
    n_small = small.shape[0]
    small_all = allgather_small(_pad_rows(small), "ag_small").reshape(N_DEV, -1, DS)
    c_all = small_all[:, :D // DS].reshape(N_DEV, D)
    per_chip = small_all[0::2]
    o = D // DS
    g_pre_full = jnp.moveaxis(per_chip[:, o:o + L * NSUB], 0, 1).reshape(L, NSUB, D)
    o += L * NSUB
    g_post_full = jnp.moveaxis(per_chip[:, o:o + L * NSUB], 0, 1).reshape(L, NSUB, D)
    o += L * NSUB
    d_full = jnp.moveaxis(per_chip[:, o:o + 1], 0, 1).reshape(1, D)

    NS = ada_w.shape[-1]
    ada_b_shard = lax.dynamic_slice_in_dim(ada_b, chip * NS, NS, axis=1).reshape(L, 1, NS)
    mod_part = ada_mod(c_all, ada_w, ada_b_shard, "ada_mod")
    mod_all = allgather_small(mod_part.reshape(L * N_DEV, NS), "ag_mod").reshape(N_DEV, L, N_DEV, NS)
    mod_mine = lax.dynamic_index_in_dim(mod_all[0::2], me, axis=2, keepdims=False)
    mod = jnp.moveaxis(mod_mine, 0, 1).reshape(L, NSUB, 3, D)

    def vec_of(l, s):
        return jnp.concatenate([mod[l, s], g_pre_full[l, s][None], g_post_full[l, s][None],
                                jnp.zeros((3, D), F32)], axis=0)

    def halves(a):
        a = a.astype(BF16)
        if a.shape[0] == 2:
            return a
        a = a[0]
        return a.reshape((2, a.shape[0] // 2) + a.shape[1:])

    gathered = gather_weights([halves(ffn_w_in), halves(ffn_w_out), halves(ab_w_in), halves(ab_w_out),
                               halves(ssm_w_in), halves(ssm_w_glu)], "gather_weights")
    win_g, wout_g = gathered[0], gathered[1]
    abin_g = gathered[2].reshape(N_CHIPS, D, -1)
    about_g = gathered[3].reshape(-1, D)
    sin_g = gathered[4].reshape(1, -1, D)
    glu_g = gathered[5].reshape(N_CHIPS, D, -1)

    GB = 8
    s5_args = (ssm_lam_re[0], ssm_lam_im[0], ssm_log_dt[0], ssm_b_re[0], ssm_b_im[0], ssm_c_re[0], ssm_c_im[0])
    (a_re, a_im, bblk_re, bblk_im, cblk_re, cblk_im), s5_vjp = jax.vjp(lambda *p: _s5_prepare(*p, GB), *s5_args)
    bre16, bim16 = bblk_re.astype(BF16), bblk_im.astype(BF16)
    cre16, cim16 = cblk_re.astype(BF16), cblk_im.astype(BF16)
    cre16_t, cim16_t = jnp.swapaxes(cre16, 1, 2), jnp.swapaxes(cim16, 1, 2)

    pool_w16 = pool_w[0].astype(BF16)
    sgu_bexp = jnp.broadcast_to(sgu_b[0][:, :, None], sgu_w[0].shape)

    saved = {}
    xcur = x0
    for l in range(L):
        v0 = vec_of(l, 0)
        out, h, u, f = ffn_fwd(xcur, v0, win_g, wout_g, l, 0, f"ffn_fwd_{l}_0")
        saved[(l, 0)] = (xcur, v0, h, u, f)
        xcur = out
        v1 = vec_of(l, 1)
        if l % 2 == 0:
            h, z = pre_matmul(xcur, v1, abin_g, f"mixa_in_{l}")
            ycat = mixa_core_fwd(z, pool_w16, pool_scale, sgu_ln_g, sgu_ln_b, sgu_w[0], sgu_bexp, f"mixa_core_{l}")
            out, f = matmul_post(ycat, about_g, xcur, v1, f"mixa_out_{l}")
            saved[(l, 1)] = (xcur, v1, h, z, ycat, f)
        else:
            h, uu = pre_matmul(xcur, v1, sin_g, f"s5_in_{l}")
            us = _to_scan_order(uu)
            ys = s5_core_fwd(us, a_re, a_im, bre16, bim16, cre16_t, cim16_t, d_full, f"s5_core_{l}")
            yy = _from_scan_order(ys)
            out, f = s5_glu_fwd(yy, glu_g, xcur, v1, f"s5_glu_{l}")
            saved[(l, 1)] = (xcur, v1, h, us, yy, f)
        xcur = out
        v2 = vec_of(l, 2)
        out, h, u, f = ffn_fwd(xcur, v2, win_g, wout_g, l, 1, f"ffn_fwd_{l}_1")
        saved[(l, 2)] = (xcur, v2, h, u, f)
        xcur = out

    dcur, sq = loss_head(xcur, target, "loss_head")
    loss = lax.psum(sq[0, 0], ("x", "y", "c")) * (0.5 / D)

    accs = {}
    g_win = [[None, None] for _ in range(L)]
    g_wout = [[None, None] for _ in range(L)]
    small_g = {}

    def ffn_back(l, s, k, dcur):
        xin, vv, h, u, f = saved[(l, s)]
        dx, df, du, act, acc = ffn_bwd(dcur, xin, f, u, vv, win_g, wout_g, l, k, f"ffn_bwd_{l}_{k}")
        accs[(l, s)] = acc
        g_win[l][k] = tn_matmul(h, du, 512, f"ffn_dwin_{l}_{k}")
        g_wout[l][k] = tn_matmul_cols(act, df, FS, D, f"ffn_dwout_{l}_{k}")[0]
        return dx

    for l in reversed(range(L)):
        dcur = ffn_back(l, 2, 1, dcur)
        if l % 2 == 0:
            xin, vv, h, z, ycat, f = saved[(l, 1)]
            df, dact, acc_post = post_bwd_matmul(dcur, f, vv, about_g, f"mixa_out_bwd_{l}")
            g_about = tn_matmul_cols(ycat, df, 512, D, f"mixa_dwout_{l}")[0]
            dz, dpw, dvecw, dsw, dsb = mixa_core_bwd(z, dact, pool_w16, pool_scale, sgu_ln_g, sgu_ln_b, sgu_w[0],
                                                     sgu_bexp, f"mixa_core_bwd_{l}")
            g_abin = tn_matmul_cols(h, dz, 512, abin_g.shape[-1], f"mixa_dwin_{l}")
            dcur, acc_pre = matmul_pre_bwd(dz, abin_g, xin, dcur, vv, f"mixa_in_bwd_{l}")
            accs[(l, 1)] = acc_pre + acc_post
            small_g.update(pool_w=dpw[None], pool_scale=dvecw[0:1], sgu_ln_g=dvecw[1:2], sgu_ln_b=dvecw[2:3],
                           sgu_w=dsw[None], sgu_b=dsb.T[None])
        else:
            xin, vv, h, us, yy, f = saved[(l, 1)]
            dy, dab, gact, acc_post = s5_glu_bwd(dcur, f, yy, vv, glu_g, f"s5_glu_bwd_{l}")
            g_glu = tn_matmul_cols(gact, dab, 512, glu_g.shape[-1], f"s5_dwglu_{l}")
            dys = _to_scan_order(dy)
            dus, dbre, dbim, dcre_t, dcim_t, dar, dai, dd = s5_core_bwd(
                us, dys, a_re, a_im, bre16, bim16, cre16, cim16, d_full, f"s5_core_bwd_{l}")
            du = _from_scan_order(dus).astype(BF16)
            g_sin = tn_matmul_cols(h, du, 512, D, f"s5_dwin_{l}")[0]
            dcur, acc_pre = matmul_pre_bwd(du, sin_g, xin, dcur, vv, f"s5_in_bwd_{l}")
            accs[(l, 1)] = acc_pre + acc_post
            s5_grads = s5_vjp((dar, dai, dbre, dbim, jnp.swapaxes(dcre_t, 1, 2), jnp.swapaxes(dcim_t, 1, 2)))
            small_g.update(ssm_lam_re=s5_grads[0][None], ssm_lam_im=s5_grads[1][None], ssm_log_dt=s5_grads[2][None],
                           ssm_b_re=s5_grads[3][None], ssm_b_im=s5_grads[4][None], ssm_c_re=s5_grads[5][None],
                           ssm_c_im=s5_grads[6][None])
            dd_mine = dd
        dcur = ffn_back(l, 0, 0, dcur)
    grad_x = dcur[None]

    acc_all = jnp.stack([jnp.stack([accs[(l, s)] for s in range(NSUB)]) for l in range(L)])
    dmod_mine = acc_all[:, :, A_SHIFT:A_GATE + 1].reshape(L, NSUB * 3 * D)
    dgpre_mine = acc_all[:, :, A_GPRE]
    dgpost_mine = acc_all[:, :, A_GPOST]
    small_names = ["pool_w", "pool_scale", "sgu_ln_g", "sgu_ln_b", "sgu_w", "sgu_b", "ssm_lam_re", "ssm_lam_im",
                   "ssm_b_re", "ssm_b_im", "ssm_c_re", "ssm_c_im", "ssm_log_dt"]
    packed, offs = _pack_rows([dmod_mine, dgpre_mine, dgpost_mine, dd_mine] + [small_g[n] for n in small_names], D)
    rows = packed.shape[0]
    packed_all = allgather_small(packed, "ag_grads").reshape(N_DEV, rows, D)
    n_mod_rows = (offs[1][0])
    summed = sum_leading(packed_all[:, n_mod_rows:], "sum_small")
    offs_rest = [(o - n_mod_rows, shp, sz) for (o, shp, sz) in offs[1:]]
    rest = _unpack_rows(summed, offs_rest)
    dgpre_sum, dgpost_sum, dd_sum = rest[0], rest[1], rest[2]
    grads = {n: g for n, g in zip(small_names, rest[3:])}
    grads["norm_pre"] = lax.dynamic_slice_in_dim(dgpre_sum, chip * DS, DS, axis=2)
    grads["norm_post"] = lax.dynamic_slice_in_dim(dgpost_sum, chip * DS, DS, axis=2)
    grads["ssm_d"] = lax.dynamic_slice_in_dim(dd_sum, chip * DS, DS, axis=1)
    dmod_all = packed_all[:, :L * NSUB * 3].reshape(N_DEV, L, NSUB * 3 * D)
    dmod_all = jnp.moveaxis(dmod_all, 0, 1)
    grads["ada_b"] = sum_leading(dmod_all.reshape(L, N_DEV, NSUB * 3 * D).transpose(1, 0, 2), "sum_ada_b")
    dmod_shard = lax.dynamic_slice_in_dim(dmod_all, chip * NS, NS, axis=2)
    grads["ada_w"] = ada_grad(c_all, dmod_shard, "ada_grad")

    big = [
        jnp.stack([jnp.stack(g_win[l]) for l in range(L)]).reshape(L * 2, N_CHIPS, D, FS),
        jnp.stack([jnp.stack(g_wout[l]) for l in range(L)]).reshape(L * 2, N_CHIPS, FR, D),
        g_abin[None],
        g_about.reshape(1, N_CHIPS, -1, D),
        g_sin.reshape(1, N_CHIPS, -1, D),
        g_glu[None],
    ]
    cidx = ci.reshape(1).astype(jnp.int32)
    chip_c = jnp.stack([chip, ci]).astype(jnp.int32)
    recv_a = sibling_exchange_halves(big, "rs_pair_exchange")
    parts = [pair_sum(g, r, cidx, f"rs_pair_sum_{i}") for i, (g, r) in enumerate(zip(big, recv_a))]
    recv_b = chip_exchange(parts, "rs_chip_exchange")
    fins = [chip_sum(p, r, chip_c, f"rs_chip_sum_{i}") for i, (p, r) in enumerate(zip(parts, recv_b))]
    fins = sibling_fill_halves(fins, "rs_fill")
    grads["ffn_w_in"] = fins[0].reshape(ffn_w_in.shape)
    grads["ffn_w_out"] = fins[1].reshape(ffn_w_out.shape)
    grads["ab_w_in"] = fins[2].reshape(ab_w_in.shape)
    grads["ab_w_out"] = fins[3].reshape(ab_w_out.shape)
    grads["ssm_w_in"] = fins[4].reshape(ssm_w_in.shape)
    grads["ssm_w_glu"] = fins[5].reshape(ssm_w_glu.shape)

    deltas, new_m, new_v = {}, {}, {}
    for n in names:
        g = grads[n].reshape(weights[n].shape)
        grads[n] = g
        deltas[n], new_m[n], new_v[n] = adamw_nd(weights[n], g, m_in[n], v_in[n], f"adamw_{n}")

    return (loss, grad_x, *[grads[n] for n in names], *[deltas[n] for n in names],
            *[new_m[n] for n in names], *[new_v[n] for n in names])
```

```python
import math

import jax
import jax.numpy as jnp
from jax import lax
from jax.experimental import pallas as pl
from jax.experimental.pallas import tpu as pltpu

F32 = jnp.float32
BF16 = jnp.bfloat16
EPS = 1e-6
MESH_T = pl.DeviceIdType.MESH
VMEM_LIMIT_BYTES = 56 * 1024 * 1024
N_CHIPS = 4
N_DEV = 8
POOL_WINDOWS = (2, 4, 8, 16)
CHUNK = 128
SCAN_LANES = 8
ADAM_LR = 0.001
ADAM_B1 = 0.9
ADAM_B2 = 0.999
ADAM_EPS = 1e-08
ADAM_WD = 0.01
ADAM_STEP = 10
GELU_C = math.sqrt(2.0 / math.pi)
GELU_K = 0.044715

V_SHIFT, V_SCALE, V_GATE, V_GPRE, V_GPOST = 0, 1, 2, 3, 4
A_SHIFT, A_SCALE, A_GATE, A_GPRE, A_GPOST = 0, 1, 2, 3, 4

ANY = pl.BlockSpec(memory_space=pl.ANY)


def _cp(sem=None):
    if sem is None:
        return pltpu.CompilerParams(vmem_limit_bytes=VMEM_LIMIT_BYTES)
    return pltpu.CompilerParams(vmem_limit_bytes=VMEM_LIMIT_BYTES, dimension_semantics=sem)


def _dot(a, b):
    return jnp.dot(a, b, preferred_element_type=F32)


def _dot_nt(a, b):
    return lax.dot_general(a, b, (((1,), (1,)), ((), ())), preferred_element_type=F32)


def _dot_tn(a, b):
    return lax.dot_general(a, b, (((0,), (0,)), ((), ())), preferred_element_type=F32)


def _sigmoid(x):
    return 1.0 / (1.0 + jnp.exp(-x))


def _gelu(x):
    return 0.5 * x * (1.0 + jnp.tanh(GELU_C * (x + GELU_K * x * x * x)))


def _gelu_grad(x):
    t = jnp.tanh(GELU_C * (x + GELU_K * x * x * x))
    return 0.5 * (1.0 + t) + 0.5 * x * (1.0 - t * t) * GELU_C * (1.0 + 3.0 * GELU_K * x * x)


def _rowsum(v):
    return jnp.sum(v, axis=0, keepdims=True)


def _lanemean(v):
    return jnp.mean(v, axis=-1, keepdims=True)


def _row(ref, i):
    return ref[pl.ds(i, 1), :]


def _pre_fwd(x, vec_ref):
    r = lax.rsqrt(_lanemean(x * x) + EPS)
    return (x * r) * _row(vec_ref, V_GPRE) * (1.0 + _row(vec_ref, V_SCALE)) + _row(vec_ref, V_SHIFT)


def _pre_bwd(x, dh, vec_ref):
    g = _row(vec_ref, V_GPRE)
    sc = 1.0 + _row(vec_ref, V_SCALE)
    r = lax.rsqrt(_lanemean(x * x) + EPS)
    xn = x * r
    dhx = dh * xn
    t2 = dh * (g * sc)
    dx = r * (t2 - xn * _lanemean(t2 * xn))
    return dx, _rowsum(dh), _rowsum(dhx * g), _rowsum(dhx * sc)


def _post_fwd(x, f, vec_ref, rw):
    q = lax.rsqrt(_lanemean(f * f) + EPS)
    return x + (rw * _row(vec_ref, V_GATE)) * (f * q * _row(vec_ref, V_GPOST))


def _post_bwd(dout, f, vec_ref, rw):
    gp = _row(vec_ref, V_GPOST)
    gate = _row(vec_ref, V_GATE)
    q = lax.rsqrt(_lanemean(f * f) + EPS)
    fhat = f * q
    dgate = _rowsum(dout * (rw * fhat * gp))
    dy = dout * (rw * gate)
    dgpost = _rowsum(dy * fhat)
    t = dy * gp
    df = q * (t - fhat * _lanemean(t * fhat))
    return df, dgate, dgpost


def _acc_add(acc_ref, row, v):
    acc_ref[pl.ds(row, 1), :] += v


def ffn_fwd(x, vec, win_g, wout_g, l, k, name, tm=256):
    S, D = x.shape
    FS = win_g.shape[-1]
    FR = wout_g.shape[-2]

    def body(x_ref, vec_ref, win_hbm, wout_hbm, out_ref, h_ref, u_ref, f_ref, win_s, wout_s, sem):
        @pl.when(pl.program_id(0) == 0)
        def _():
            cps = [pltpu.make_async_copy(win_hbm.at[j, l, k], win_s.at[j], sem.at[j]) for j in range(N_CHIPS)]
            cps += [pltpu.make_async_copy(wout_hbm.at[j, l, k], wout_s.at[pl.ds(j * FR, FR), :], sem.at[N_CHIPS + j])
                    for j in range(N_CHIPS)]
            for cp in cps:
                cp.start()
            for cp in cps:
                cp.wait()

        xv = x_ref[...]
        h = _pre_fwd(xv, vec_ref).astype(BF16)
        h_ref[...] = h
        f = jnp.zeros((tm, D), F32)
        for j in range(2):
            a = _dot(h, win_s[j])
            b = _dot(h, win_s[2 + j])
            u_ref[j] = a.astype(BF16)
            u_ref[2 + j] = b.astype(BF16)
            act = (a * _sigmoid(a) * b).astype(BF16)
            f = f + _dot(act, wout_s[pl.ds(j * FS, FS), :])
        f_ref[...] = f
        out_ref[...] = _post_fwd(xv, f, vec_ref, 0.5)

    row = pl.BlockSpec((tm, D), lambda i: (i, 0))
    return pl.pallas_call(
        body, name=name, grid=(S // tm,),
        in_specs=[row, pl.BlockSpec((8, D), lambda i: (0, 0)), ANY, ANY],
        out_specs=[row, row, pl.BlockSpec((N_CHIPS, tm, FS), lambda i: (0, i, 0)), row],
        out_shape=[jax.ShapeDtypeStruct((S, D), F32), jax.ShapeDtypeStruct((S, D), BF16),
                   jax.ShapeDtypeStruct((N_CHIPS, S, FS), BF16), jax.ShapeDtypeStruct((S, D), F32)],
        scratch_shapes=[pltpu.VMEM((N_CHIPS, D, FS), BF16), pltpu.VMEM((N_CHIPS * FR, D), BF16),
                        pltpu.SemaphoreType.DMA((2 * N_CHIPS,))],
        compiler_params=_cp(("arbitrary",)),
    )(x, vec, win_g, wout_g)


def ffn_bwd(dout, x, f, u, vec, win_g, wout_g, l, k, name, tm=256):
    S, D = x.shape
    FS = win_g.shape[-1]
    FR = wout_g.shape[-2]

    def body(dout_ref, x_ref, f_ref, u_ref, vec_ref, win_hbm, wout_hbm,
             dx_ref, df_ref, du_ref, act_ref, acc_ref, win_s, wout_s, sem):
        @pl.when(pl.program_id(0) == 0)
        def _():
            cps = [pltpu.make_async_copy(win_hbm.at[j, l, k], win_s.at[j], sem.at[j]) for j in range(N_CHIPS)]
            cps += [pltpu.make_async_copy(wout_hbm.at[j, l, k], wout_s.at[pl.ds(j * FR, FR), :], sem.at[N_CHIPS + j])
                    for j in range(N_CHIPS)]
            for cp in cps:
                cp.start()
            acc_ref[...] = jnp.zeros_like(acc_ref)
            for cp in cps:
                cp.wait()

        dout_v = dout_ref[...]
        df, dgate, dgpost = _post_bwd(dout_v, f_ref[...], vec_ref, 0.5)
        dfb = df.astype(BF16)
        df_ref[...] = dfb
        dh = jnp.zeros((tm, D), F32)
        for j in range(2):
            a = u_ref[j].astype(F32)
            b = u_ref[2 + j].astype(F32)
            sig = _sigmoid(a)
            sl = a * sig
            dact = _dot_nt(dfb, wout_s[pl.ds(j * FS, FS), :])
            da = (dact * b * (sig * (1.0 + a * (1.0 - sig)))).astype(BF16)
            db = (dact * sl).astype(BF16)
            du_ref[j] = da
            du_ref[2 + j] = db
            act_ref[:, j * FS:(j + 1) * FS] = (sl * b).astype(BF16)
            dh = dh + _dot_nt(da, win_s[j]) + _dot_nt(db, win_s[2 + j])
        dx, dshift, dscale, dgpre = _pre_bwd(x_ref[...], dh, vec_ref)
        dx_ref[...] = dout_v + dx
        _acc_add(acc_ref, A_SHIFT, dshift)
        _acc_add(acc_ref, A_SCALE, dscale)
        _acc_add(acc_ref, A_GATE, dgate)
        _acc_add(acc_ref, A_GPRE, dgpre)
        _acc_add(acc_ref, A_GPOST, dgpost)

    row = pl.BlockSpec((tm, D), lambda i: (i, 0))
    ublk = pl.BlockSpec((N_CHIPS, tm, FS), lambda i: (0, i, 0))
    const = pl.BlockSpec((8, D), lambda i: (0, 0))
    return pl.pallas_call(
        body, name=name, grid=(S // tm,),
        in_specs=[row, row, row, ublk, const, ANY, ANY],
        out_specs=[row, row, ublk, pl.BlockSpec((tm, 2 * FS), lambda i: (i, 0)), const],
        out_shape=[jax.ShapeDtypeStruct((S, D), F32), jax.ShapeDtypeStruct((S, D), BF16),
                   jax.ShapeDtypeStruct((N_CHIPS, S, FS), BF16), jax.ShapeDtypeStruct((S, 2 * FS), BF16),
                   jax.ShapeDtypeStruct((8, D), F32)],
        scratch_shapes=[pltpu.VMEM((N_CHIPS, D, FS), BF16), pltpu.VMEM((N_CHIPS * FR, D), BF16),
                        pltpu.SemaphoreType.DMA((2 * N_CHIPS,))],
        compiler_params=_cp(("arbitrary",)),
    )(dout, x, f, u, vec, win_g, wout_g)


def tn_matmul(a, b, tk, name):
    S, K = a.shape
    nb, _, tn = b.shape

    def body(a_ref, b_ref, o_ref):
        o_ref[...] = _dot_tn(a_ref[...], b_ref[...]).astype(BF16)

    return pl.pallas_call(
        body, name=name, grid=(K // tk, nb),
        in_specs=[pl.BlockSpec((S, tk), lambda i, j: (0, i)), pl.BlockSpec((None, S, tn), lambda i, j: (j, 0, 0))],
        out_specs=pl.BlockSpec((None, tk, tn), lambda i, j: (j, i, 0)),
        out_shape=jax.ShapeDtypeStruct((nb, K, tn), BF16),
        compiler_params=_cp(("arbitrary", "arbitrary")),
    )(a, b)


def tn_matmul_cols(a, b, tk, tn, name):
    S, K = a.shape
    N = b.shape[1]

    def body(a_ref, b_ref, o_ref):
        o_ref[...] = _dot_tn(a_ref[...], b_ref[...]).astype(BF16)

    return pl.pallas_call(
        body, name=name, grid=(K // tk, N // tn),
        in_specs=[pl.BlockSpec((S, tk), lambda i, j: (0, i)), pl.BlockSpec((S, tn), lambda i, j: (0, j))],
        out_specs=pl.BlockSpec((None, tk, tn), lambda i, j: (j, i, 0)),
        out_shape=jax.ShapeDtypeStruct((N // tn, K, tn), BF16),
        compiler_params=_cp(("arbitrary", "arbitrary")),
    )(a, b)


def pre_matmul(x, vec, w3, name, tm=256):
    S, D = x.shape
    nj, _, Nj = w3.shape

    def body(x_ref, vec_ref, w_ref, h_ref, z_ref):
        h = _pre_fwd(x_ref[...], vec_ref).astype(BF16)
        h_ref[...] = h
        for j in range(nj):
            z_ref[:, j * Nj:(j + 1) * Nj] = _dot(h, w_ref[j])

    row = pl.BlockSpec((tm, D), lambda i: (i, 0))
    return pl.pallas_call(
        body, name=name, grid=(S // tm,),
        in_specs=[row, pl.BlockSpec((8, D), lambda i: (0, 0)), pl.BlockSpec((nj, D, Nj), lambda i: (0, 0, 0))],
        out_specs=[row, pl.BlockSpec((tm, nj * Nj), lambda i: (i, 0))],
        out_shape=[jax.ShapeDtypeStruct((S, D), BF16), jax.ShapeDtypeStruct((S, nj * Nj), F32)],
        compiler_params=_cp(("arbitrary",)),
    )(x, vec, w3)


def matmul_pre_bwd(dz, w3, x, dres, vec, name, tm=256):
    S, D = x.shape
    nj, _, Nj = w3.shape

    def body(dz_ref, w_ref, x_ref, dres_ref, vec_ref, dx_ref, acc_ref):
        @pl.when(pl.program_id(0) == 0)
        def _():
            acc_ref[...] = jnp.zeros_like(acc_ref)

        dh = jnp.zeros((tm, D), F32)
        for j in range(nj):
            dh = dh + _dot_nt(dz_ref[:, j * Nj:(j + 1) * Nj], w_ref[j])
        dx, dshift, dscale, dgpre = _pre_bwd(x_ref[...], dh, vec_ref)
        dx_ref[...] = dres_ref[...] + dx
        _acc_add(acc_ref, A_SHIFT, dshift)
        _acc_add(acc_ref, A_SCALE, dscale)
        _acc_add(acc_ref, A_GPRE, dgpre)

    row = pl.BlockSpec((tm, D), lambda i: (i, 0))
    const = pl.BlockSpec((8, D), lambda i: (0, 0))
    return pl.pallas_call(
        body, name=name, grid=(S // tm,),
        in_specs=[pl.BlockSpec((tm, nj * Nj), lambda i: (i, 0)), pl.BlockSpec((nj, D, Nj), lambda i: (0, 0, 0)),
                  row, row, const],
        out_specs=[row, const],
        out_shape=[jax.ShapeDtypeStruct((S, D), F32), jax.ShapeDtypeStruct((8, D), F32)],
        compiler_params=_cp(("arbitrary",)),
    )(dz, w3, x, dres, vec)


def matmul_post(act, w, x, vec, name, tm=256):
    S, D = x.shape
    K = act.shape[1]

    def body(act_ref, w_ref, x_ref, vec_ref, out_ref, f_ref):
        f = _dot(act_ref[...], w_ref[...])
        f_ref[...] = f
        out_ref[...] = _post_fwd(x_ref[...], f, vec_ref, 1.0)

    row = pl.BlockSpec((tm, D), lambda i: (i, 0))
    return pl.pallas_call(
        body, name=name, grid=(S // tm,),
        in_specs=[pl.BlockSpec((tm, K), lambda i: (i, 0)), pl.BlockSpec((K, D), lambda i: (0, 0)), row,
                  pl.BlockSpec((8, D), lambda i: (0, 0))],
        out_specs=[row, row],
        out_shape=[jax.ShapeDtypeStruct((S, D), F32), jax.ShapeDtypeStruct((S, D), F32)],
        compiler_params=_cp(("arbitrary",)),
    )(act, w, x, vec)


def post_bwd_matmul(dout, f, vec, w, name, tm=256):
    S, D = dout.shape
    K = w.shape[0]

    def body(dout_ref, f_ref, vec_ref, w_ref, df_ref, dact_ref, acc_ref):
        @pl.when(pl.program_id(0) == 0)
        def _():
            acc_ref[...] = jnp.zeros_like(acc_ref)

        df, dgate, dgpost = _post_bwd(dout_ref[...], f_ref[...], vec_ref, 1.0)
        dfb = df.astype(BF16)
        df_ref[...] = dfb
        dact_ref[...] = _dot_nt(dfb, w_ref[...])
        _acc_add(acc_ref, A_GATE, dgate)
        _acc_add(acc_ref, A_GPOST, dgpost)

    row = pl.BlockSpec((tm, D), lambda i: (i, 0))
    const = pl.BlockSpec((8, D), lambda i: (0, 0))
    return pl.pallas_call(
        body, name=name, grid=(S // tm,),
        in_specs=[row, row, const, pl.BlockSpec((K, D), lambda i: (0, 0))],
        out_specs=[row, pl.BlockSpec((tm, K), lambda i: (i, 0)), const],
        out_shape=[jax.ShapeDtypeStruct((S, D), BF16), jax.ShapeDtypeStruct((S, K), F32),
                   jax.ShapeDtypeStruct((8, D), F32)],
        compiler_params=_cp(("arbitrary",)),
    )(dout, f, vec, w)


def _band(w, transposed, prev):
    r = lax.broadcasted_iota(jnp.int32, (CHUNK, CHUNK), 1 if transposed else 0)
    c = lax.broadcasted_iota(jnp.int32, (CHUNK, CHUNK), 0 if transposed else 1)
    d = r - c
    m = (d + CHUNK < w) if prev else ((d >= 0) & (d < w))
    return jnp.where(m, 1.0, 0.0).astype(BF16)


def _split_hi_lo(a):
    hi = a.astype(BF16)
    lo = (a - hi.astype(F32)).astype(BF16)
    return hi, lo


def _pool_diff(a, ap, g, denom):
    w = POOL_WINDOWS[g]
    a_hi, a_lo = _split_hi_lo(a)
    p_hi, p_lo = _split_hi_lo(ap)
    mc = _band(w, False, False)
    mp = _band(w, False, True)
    win = _dot(mc, a_hi) + _dot(mc, a_lo) + _dot(mp, p_hi) + _dot(mp, p_lo)
    return win / denom - a


def _sgu_norm(v, lg, lb):
    mu = _lanemean(v)
    xc = v - mu
    rstd = lax.rsqrt(_lanemean(xc * xc) + EPS)
    xhat = xc * rstd
    return xhat, rstd, xhat * lg + lb


def _tril_mask():
    r = lax.broadcasted_iota(jnp.int32, (CHUNK, CHUNK), 0)
    c = lax.broadcasted_iota(jnp.int32, (CHUNK, CHUNK), 1)
    return r >= c


def _positions(i, w):
    r = lax.broadcasted_iota(jnp.int32, (CHUNK, 128), 0)
    pos = (i * CHUNK + r + 1).astype(F32)
    return jnp.minimum(pos, float(w))


def mixa_core_fwd(z, pool_w, pool_scale, ln_g, ln_b, sgu_w, sgu_bexp, name):
    S = z.shape[0]
    W = z.shape[1] // 3
    G = len(POOL_WINDOWS)
    GD = W // G

    def body(zc_ref, zp_ref, pw_ref, ps_ref, lg_ref, lb_ref, sw_ref, sb_ref, y_ref):
        i = pl.program_id(0)
        has_prev = jnp.where(i > 0, 1.0, 0.0)
        for g in range(G):
            sl = slice(g * GD, (g + 1) * GD)
            a = zc_ref[:, sl]
            ap = zp_ref[:, sl] * has_prev
            d = _pool_diff(a, ap, g, _positions(i, POOL_WINDOWS[g])).astype(BF16)
            y_ref[:, sl] = (_dot(d, pw_ref[g]) * ps_ref[:, sl]).astype(BF16)
        tril = _tril_mask()
        for hh in range(G):
            u = _gelu(zc_ref[:, W + hh * GD:W + (hh + 1) * GD])
            v = _gelu(zc_ref[:, 2 * W + hh * GD:2 * W + (hh + 1) * GD])
            sl = slice(hh * GD, (hh + 1) * GD)
            _, _, vn = _sgu_norm(v, lg_ref[:, sl], lb_ref[:, sl])
            wm = jnp.where(tril, sw_ref[hh], 0.0).astype(BF16)
            s = _dot(wm, vn.astype(BF16)) + sb_ref[hh]
            y_ref[:, W + hh * GD:W + (hh + 1) * GD] = (u * s).astype(BF16)

    vecw = pl.BlockSpec((1, W), lambda i: (0, 0))
    mats = pl.BlockSpec((G, GD, GD), lambda i: (0, 0, 0))
    return pl.pallas_call(
        body, name=name, grid=(S // CHUNK,),
        in_specs=[pl.BlockSpec((CHUNK, 3 * W), lambda i: (i, 0)),
                  pl.BlockSpec((CHUNK, W), lambda i: (jnp.maximum(i - 1, 0), 0)),
                  mats, vecw, vecw, vecw, mats, mats],
        out_specs=pl.BlockSpec((CHUNK, 2 * W), lambda i: (i, 0)),
        out_shape=jax.ShapeDtypeStruct((S, 2 * W), BF16),
        compiler_params=_cp(("arbitrary",)),
    )(z, z, pool_w, pool_scale, ln_g, ln_b, sgu_w, sgu_bexp)


def mixa_core_bwd(z, dy, pool_w, pool_scale, ln_g, ln_b, sgu_w, sgu_bexp, name):
    S = z.shape[0]
    W = z.shape[1] // 3
    G = len(POOL_WINDOWS)
    GD = W // G
    n_tiles = S // CHUNK

    def body(zc_ref, zp_ref, dyc_ref, dyn_ref, pw_ref, ps_ref, lg_ref, lb_ref, sw_ref, sb_ref,
             dz_ref, dpw_ref, dvec_ref, dsw_ref, dsb_ref):
        i = pl.program_id(0)

        @pl.when(i == 0)
        def _():
            dpw_ref[...] = jnp.zeros_like(dpw_ref)
            dvec_ref[...] = jnp.zeros_like(dvec_ref)
            dsw_ref[...] = jnp.zeros_like(dsw_ref)
            dsb_ref[...] = jnp.zeros_like(dsb_ref)

        has_prev = jnp.where(i > 0, 1.0, 0.0)
        has_next = jnp.where(i < n_tiles - 1, 1.0, 0.0)
        for g in range(G):
            w = POOL_WINDOWS[g]
            sl = slice(g * GD, (g + 1) * GD)
            a = zc_ref[:, sl]
            ap = zp_ref[:, sl] * has_prev
            den_c = _positions(i, w)
            den_n = _positions(i + 1, w)
            d = _pool_diff(a, ap, g, den_c).astype(BF16)
            ps = ps_ref[:, sl]
            pw = pw_ref[g]
            dyc = dyc_ref[:, sl]
            dvec_ref[pl.ds(0, 1), sl] += _rowsum(dyc * _dot(d, pw))
            dyp_c = (dyc * ps).astype(BF16)
            dyp_n = (dyn_ref[:, sl] * (ps * has_next)).astype(BF16)
            dpw_ref[g] += _dot_tn(d, dyp_c)
            dd_c = _dot_nt(dyp_c, pw)
            dd_n = _dot_nt(dyp_n, pw)
            da = (_dot(_band(w, True, False), (dd_c / den_c).astype(BF16))
                  + _dot(_band(w, True, True), (dd_n / den_n).astype(BF16)) - dd_c)
            dz_ref[:, sl] = da.astype(BF16)
        tril = _tril_mask()
        for hh in range(G):
            sl = slice(hh * GD, (hh + 1) * GD)
            zu = zc_ref[:, W + hh * GD:W + (hh + 1) * GD]
            zv = zc_ref[:, 2 * W + hh * GD:2 * W + (hh + 1) * GD]
            u = _gelu(zu)
            v = _gelu(zv)
            lg = lg_ref[:, sl]
            xhat, rstd, vn = _sgu_norm(v, lg, lb_ref[:, sl])
            vnb = vn.astype(BF16)
            wm = jnp.where(tril, sw_ref[hh], 0.0).astype(BF16)
            s = _dot(wm, vnb) + sb_ref[hh]
            dyb = dyc_ref[:, W + hh * GD:W + (hh + 1) * GD]
            du = dyb * s
            ds = dyb * u
            dsb_ref[:, hh:hh + 1] += jnp.sum(ds, axis=1, keepdims=True)
            dsb16 = ds.astype(BF16)
            dsw_ref[hh] += jnp.where(tril, _dot_nt(dsb16, vnb), 0.0)
            dvn = _dot_tn(wm, dsb16)
            dvec_ref[pl.ds(1, 1), sl] += _rowsum(dvn * xhat)
            dvec_ref[pl.ds(2, 1), sl] += _rowsum(dvn)
            dxh = dvn * lg
            dv = rstd * (dxh - _lanemean(dxh) - xhat * _lanemean(dxh * xhat))
            dz_ref[:, W + hh * GD:W + (hh + 1) * GD] = (du * _gelu_grad(zu)).astype(BF16)
            dz_ref[:, 2 * W + hh * GD:2 * W + (hh + 1) * GD] = (dv * _gelu_grad(zv)).astype(BF16)

    vecw = pl.BlockSpec((1, W), lambda i: (0, 0))
    mats = pl.BlockSpec((G, GD, GD), lambda i: (0, 0, 0))
    return pl.pallas_call(
        body, name=name, grid=(n_tiles,),
        in_specs=[pl.BlockSpec((CHUNK, 3 * W), lambda i: (i, 0)),
                  pl.BlockSpec((CHUNK, W), lambda i: (jnp.maximum(i - 1, 0), 0)),
                  pl.BlockSpec((CHUNK, 2 * W), lambda i: (i, 0)),
                  pl.BlockSpec((CHUNK, W), lambda i: (jnp.minimum(i + 1, n_tiles - 1), 0)),
                  mats, vecw, vecw, vecw, mats, mats],
        out_specs=[pl.BlockSpec((CHUNK, 3 * W), lambda i: (i, 0)), mats,
                   pl.BlockSpec((8, W), lambda i: (0, 0)), mats, pl.BlockSpec((CHUNK, G), lambda i: (0, 0))],
        out_shape=[jax.ShapeDtypeStruct((S, 3 * W), BF16), jax.ShapeDtypeStruct((G, GD, GD), F32),
                   jax.ShapeDtypeStruct((8, W), F32), jax.ShapeDtypeStruct((G, CHUNK, CHUNK), F32),
                   jax.ShapeDtypeStruct((CHUNK, G), F32)],
        compiler_params=_cp(("arbitrary",)),
    )(z, z, dy, dy, pool_w, pool_scale, ln_g, ln_b, sgu_w, sgu_bexp)


def _cmul(ar, ai, br, bi):
    return ar * br - ai * bi, ar * bi + ai * br


def _cpow(ar, ai, n):
    rr, ri = None, None
    br, bi = ar, ai
    while n:
        if n & 1:
            rr, ri = (br, bi) if rr is None else _cmul(rr, ri, br, bi)
        n >>= 1
        if n:
            br, bi = _cmul(br, bi, br, bi)
    return rr, ri


def _seg_rows(k):
    return pl.ds(pl.multiple_of(k * SCAN_LANES, SCAN_LANES), SCAN_LANES)


def _scan_fwd(xre, xim, carry, ar, ai, K):
    P = xre.shape[1]

    def local(k, c):
        pr, pi = c
        rows = _seg_rows(k)
        nr = ar * pr - ai * pi + xre[rows, :]
        ni = ar * pi + ai * pr + xim[rows, :]
        xre[rows, :] = nr
        xim[rows, :] = ni
        return nr, ni

    er, ei = lax.fori_loop(1, K, local, (xre[pl.ds(0, SCAN_LANES), :], xim[pl.ds(0, SCAN_LANES), :]))
    akr, aki = _cpow(ar, ai, K)
    cr = jnp.zeros((1, P), F32)
    ci = jnp.zeros((1, P), F32)
    carry[pl.ds(0, 1), :] = cr
    carry[pl.ds(SCAN_LANES, 1), :] = ci
    for j in range(1, SCAN_LANES):
        tr, ti = _cmul(akr, aki, cr, ci)
        cr = er[j - 1:j, :] + tr
        ci = ei[j - 1:j, :] + ti
        carry[pl.ds(j, 1), :] = cr
        carry[pl.ds(SCAN_LANES + j, 1), :] = ci
    cmr = carry[pl.ds(0, SCAN_LANES), :]
    cmi = carry[pl.ds(SCAN_LANES, SCAN_LANES), :]

    def fix(k, c):
        pr, pi = c
        rows = _seg_rows(k)
        tr, ti = _cmul(pr, pi, cmr, cmi)
        xre[rows, :] += tr
        xim[rows, :] += ti
        return _cmul(pr, pi, ar, ai)

    lax.fori_loop(0, K, fix, (jnp.broadcast_to(ar, (SCAN_LANES, P)), jnp.broadcast_to(ai, (SCAN_LANES, P))))


def s5_core_fwd(u, ar, ai, bre, bim, cre_t, cim_t, dskip, name):
    S, D = u.shape
    nblk, UB, PB = bre.shape
    K = S // SCAN_LANES

    def body(u_ref, ar_ref, ai_ref, bre_ref, bim_ref, cre_ref, cim_ref, d_ref, y_ref, xre, xim, carry):
        uv = u_ref[...]
        ub = uv.astype(BF16)
        xre[...] = _dot(ub, bre_ref[...])
        xim[...] = _dot(ub, bim_ref[...])
        _scan_fwd(xre, xim, carry, ar_ref[...], ai_ref[...], K)
        y_ref[...] = (_dot(xre[...].astype(BF16), cre_ref[...]) - _dot(xim[...].astype(BF16), cim_ref[...])
                      + d_ref[...] * uv)

    ucol = pl.BlockSpec((S, UB), lambda i: (0, i))
    pvec = pl.BlockSpec((None, 1, PB), lambda i: (i, 0, 0))
    bmat = pl.BlockSpec((None, UB, PB), lambda i: (i, 0, 0))
    cmat = pl.BlockSpec((None, PB, UB), lambda i: (i, 0, 0))
    return pl.pallas_call(
        body, name=name, grid=(nblk,),
        in_specs=[ucol, pvec, pvec, bmat, bmat, cmat, cmat, pl.BlockSpec((1, UB), lambda i: (0, i))],
        out_specs=ucol,
        out_shape=jax.ShapeDtypeStruct((S, D), F32),
        scratch_shapes=[pltpu.VMEM((S, PB), F32), pltpu.VMEM((S, PB), F32), pltpu.VMEM((2 * SCAN_LANES, PB), F32)],
        compiler_params=_cp(("arbitrary",)),
    )(u, ar, ai, bre, bim, cre_t, cim_t, dskip)


def s5_core_bwd(u, dy, ar, ai, bre, bim, cre, cim, dskip, name):
    S, D = u.shape
    nblk, UB, PB = bre.shape
    K = S // SCAN_LANES

    def body(u_ref, dy_ref, ar_ref, ai_ref, bre_ref, bim_ref, cre_ref, cim_ref, d_ref,
             du_ref, dbre_ref, dbim_ref, dcre_ref, dcim_ref, dar_ref, dai_ref, dd_ref,
             xre, xim, gre, gim, carry, carry_b):
        ar = ar_ref[...]
        ai = ai_ref[...]
        uv = u_ref[...]
        ub = uv.astype(BF16)
        dyv = dy_ref[...]
        dyb = dyv.astype(BF16)
        xre[...] = _dot(ub, bre_ref[...])
        xim[...] = _dot(ub, bim_ref[...])
        _scan_fwd(xre, xim, carry, ar, ai, K)
        dcre_ref[...] = _dot_tn(xre[...].astype(BF16), dyb)
        dcim_ref[...] = -_dot_tn(xim[...].astype(BF16), dyb)
        gre[...] = _dot(dyb, cre_ref[...])
        gim[...] = -_dot(dyb, cim_ref[...])

        def local(s, c):
            k = K - 2 - s
            nr, ni = c
            rows = _seg_rows(k)
            tr = gre[rows, :] + ar * nr + ai * ni
            ti = gim[rows, :] + ar * ni - ai * nr
            gre[rows, :] = tr
            gim[rows, :] = ti
            return tr, ti

        last = _seg_rows(K - 1)
        fr, fi = lax.fori_loop(0, K - 1, local, (gre[last, :], gim[last, :]))
        akr, aki = _cpow(ar, -ai, K)
        cr = jnp.zeros((1, PB), F32)
        ci = jnp.zeros((1, PB), F32)
        carry_b[pl.ds(SCAN_LANES - 1, 1), :] = cr
        carry_b[pl.ds(2 * SCAN_LANES - 1, 1), :] = ci
        for j in range(SCAN_LANES - 2, -1, -1):
            tr, ti = _cmul(akr, aki, cr, ci)
            cr = fr[j + 1:j + 2, :] + tr
            ci = fi[j + 1:j + 2, :] + ti
            carry_b[pl.ds(j, 1), :] = cr
            carry_b[pl.ds(SCAN_LANES + j, 1), :] = ci
        cbr = carry_b[pl.ds(0, SCAN_LANES), :]
        cbi = carry_b[pl.ds(SCAN_LANES, SCAN_LANES), :]

        def fix(s, c):
            k = K - 1 - s
            pr, pi, dar, dai = c
            rows = _seg_rows(k)
            tr, ti = _cmul(pr, pi, cbr, cbi)
            g_r = gre[rows, :] + tr
            g_i = gim[rows, :] + ti
            gre[rows, :] = g_r
            gim[rows, :] = g_i
            prev = _seg_rows(jnp.maximum(k - 1, 0))
            first = k == 0
            xr = jnp.where(first, carry[pl.ds(0, SCAN_LANES), :], xre[prev, :])
            xi = jnp.where(first, carry[pl.ds(SCAN_LANES, SCAN_LANES), :], xim[prev, :])
            dar = dar + g_r * xr + g_i * xi
            dai = dai + g_i * xr - g_r * xi
            nr, ni = _cmul(pr, pi, ar, -ai)
            return nr, ni, dar, dai

        z8 = jnp.zeros((SCAN_LANES, PB), F32)
        _, _, dar, dai = lax.fori_loop(
            0, K, fix, (jnp.broadcast_to(ar, (SCAN_LANES, PB)), jnp.broadcast_to(-ai, (SCAN_LANES, PB)), z8, z8))
        dar_ref[...] = _rowsum(dar)
        dai_ref[...] = _rowsum(dai)
        grb = gre[...].astype(BF16)
        gib = gim[...].astype(BF16)
        dbre_ref[...] = _dot_tn(ub, grb)
        dbim_ref[...] = _dot_tn(ub, gib)
        du_ref[...] = _dot_nt(grb, bre_ref[...]) + _dot_nt(gib, bim_ref[...]) + d_ref[...] * dyv
        dd_ref[...] = _rowsum(dyv * uv)

    ucol = pl.BlockSpec((S, UB), lambda i: (0, i))
    pvec = pl.BlockSpec((None, 1, PB), lambda i: (i, 0, 0))
    bmat = pl.BlockSpec((None, UB, PB), lambda i: (i, 0, 0))
    cmat = pl.BlockSpec((None, PB, UB), lambda i: (i, 0, 0))
    dvec = pl.BlockSpec((1, UB), lambda i: (0, i))
    return pl.pallas_call(
        body, name=name, grid=(nblk,),
        in_specs=[ucol, ucol, pvec, pvec, bmat, bmat, bmat, bmat, dvec],
        out_specs=[ucol, bmat, bmat, cmat, cmat, pvec, pvec, dvec],
        out_shape=[jax.ShapeDtypeStruct((S, D), F32),
                   jax.ShapeDtypeStruct((nblk, UB, PB), F32), jax.ShapeDtypeStruct((nblk, UB, PB), F32),
                   jax.ShapeDtypeStruct((nblk, PB, UB), F32), jax.ShapeDtypeStruct((nblk, PB, UB), F32),
                   jax.ShapeDtypeStruct((nblk, 1, PB), F32), jax.ShapeDtypeStruct((nblk, 1, PB), F32),
                   jax.ShapeDtypeStruct((1, D), F32)],
        scratch_shapes=[pltpu.VMEM((S, PB), F32), pltpu.VMEM((S, PB), F32), pltpu.VMEM((S, PB), F32),
                        pltpu.VMEM((S, PB), F32), pltpu.VMEM((2 * SCAN_LANES, PB), F32),
                        pltpu.VMEM((2 * SCAN_LANES, PB), F32)],
        compiler_params=_cp(("arbitrary",)),
    )(u, dy, ar, ai, bre, bim, cre, cim, dskip)


def s5_glu_fwd(y, wglu, x, vec, name, tm=256):
    S, D = x.shape
    NJ = wglu.shape[-1]

    def body(y_ref, w_ref, x_ref, vec_ref, out_ref, f_ref):
        g = _gelu(y_ref[...]).astype(BF16)
        f = jnp.concatenate([_dot(g, w_ref[j]) * _sigmoid(_dot(g, w_ref[2 + j])) for j in range(2)], axis=1)
        f_ref[...] = f
        out_ref[...] = _post_fwd(x_ref[...], f, vec_ref, 1.0)

    row = pl.BlockSpec((tm, D), lambda i: (i, 0))
    return pl.pallas_call(
        body, name=name, grid=(S // tm,),
        in_specs=[row, pl.BlockSpec((N_CHIPS, D, NJ), lambda i: (0, 0, 0)), row, pl.BlockSpec((8, D), lambda i: (0, 0))],
        out_specs=[row, row],
        out_shape=[jax.ShapeDtypeStruct((S, D), F32), jax.ShapeDtypeStruct((S, D), F32)],
        compiler_params=_cp(("arbitrary",)),
    )(y, wglu, x, vec)


def s5_glu_bwd(dout, f, y, vec, wglu, name, tm=256):
    S, D = dout.shape
    NJ = wglu.shape[-1]

    def body(dout_ref, f_ref, y_ref, vec_ref, w_ref, dy_ref, dab_ref, g_ref, acc_ref):
        @pl.when(pl.program_id(0) == 0)
        def _():
            acc_ref[...] = jnp.zeros_like(acc_ref)

        df, dgate, dgpost = _post_bwd(dout_ref[...], f_ref[...], vec_ref, 1.0)
        yv = y_ref[...]
        g = _gelu(yv).astype(BF16)
        g_ref[...] = g
        dg = jnp.zeros((tm, D), F32)
        for j in range(2):
            a = _dot(g, w_ref[j])
            sig = _sigmoid(_dot(g, w_ref[2 + j]))
            dfj = df[:, j * NJ:(j + 1) * NJ]
            da = (dfj * sig).astype(BF16)
            db = (dfj * a * sig * (1.0 - sig)).astype(BF16)
            dab_ref[:, j * NJ:(j + 1) * NJ] = da
            dab_ref[:, (2 + j) * NJ:(3 + j) * NJ] = db
            dg = dg + _dot_nt(da, w_ref[j]) + _dot_nt(db, w_ref[2 + j])
        dy_ref[...] = dg * _gelu_grad(yv)
        _acc_add(acc_ref, A_GATE, dgate)
        _acc_add(acc_ref, A_GPOST, dgpost)

    row = pl.BlockSpec((tm, D), lambda i: (i, 0))
    const = pl.BlockSpec((8, D), lambda i: (0, 0))
    return pl.pallas_call(
        body, name=name, grid=(S // tm,),
        in_specs=[row, row, row, const, pl.BlockSpec((N_CHIPS, D, NJ), lambda i: (0, 0, 0))],
        out_specs=[row, pl.BlockSpec((tm, N_CHIPS * NJ), lambda i: (i, 0)), row, const],
        out_shape=[jax.ShapeDtypeStruct((S, D), F32), jax.ShapeDtypeStruct((S, N_CHIPS * NJ), BF16),
                   jax.ShapeDtypeStruct((S, D), BF16), jax.ShapeDtypeStruct((8, D), F32)],
        compiler_params=_cp(("arbitrary",)),
    )(dout, f, y, vec, wglu)


def loss_head(y, target, name, tm=256):
    S, D = y.shape

    def body(y_ref, t_ref, d_ref, l_ref):
        @pl.when(pl.program_id(0) == 0)
        def _():
            l_ref[...] = jnp.zeros_like(l_ref)

        err = y_ref[...] - t_ref[...]
        d_ref[...] = err * (1.0 / D)
        l_ref[...] += jnp.sum(_rowsum(err * err), axis=1, keepdims=True)

    row = pl.BlockSpec((tm, D), lambda i: (i, 0))
    return pl.pallas_call(
        body, name=name, grid=(S // tm,),
        in_specs=[row, row],
        out_specs=[row, pl.BlockSpec((1, 1), lambda i: (0, 0))],
        out_shape=[jax.ShapeDtypeStruct((S, D), F32), jax.ShapeDtypeStruct((1, 1), F32)],
        compiler_params=_cp(("arbitrary",)),
    )(y, target)


def ada_mod(c_all, ada_w, ada_b_shard, name, tn=768):
    B, D = c_all.shape
    L, _, NS = ada_w.shape

    def body(c_ref, w_ref, b_ref, o_ref):
        cv = c_ref[...]
        cond = (cv * _sigmoid(cv)).astype(BF16)
        o_ref[...] = _dot(cond, w_ref[...].astype(BF16)) + b_ref[...]

    return pl.pallas_call(
        body, name=name, grid=(L, NS // tn),
        in_specs=[pl.BlockSpec((B, D), lambda l, j: (0, 0)), pl.BlockSpec((None, D, tn), lambda l, j: (l, 0, j)),
                  pl.BlockSpec((None, 1, tn), lambda l, j: (l, 0, j))],
        out_specs=pl.BlockSpec((None, B, tn), lambda l, j: (l, 0, j)),
        out_shape=jax.ShapeDtypeStruct((L, B, NS), F32),
        compiler_params=_cp(("arbitrary", "arbitrary")),
    )(c_all, ada_w, ada_b_shard)


def ada_grad(c_all, dmod, name, tn=768):
    B, D = c_all.shape
    L, _, NS = dmod.shape

    def body(c_ref, d_ref, o_ref):
        cv = c_ref[...]
        cond = (cv * _sigmoid(cv)).astype(BF16)
        o_ref[...] = _dot_tn(cond, d_ref[...].astype(BF16))

    return pl.pallas_call(
        body, name=name, grid=(L, NS // tn),
        in_specs=[pl.BlockSpec((B, D), lambda l, j: (0, 0)), pl.BlockSpec((None, B, tn), lambda l, j: (l, 0, j))],
        out_specs=pl.BlockSpec((None, D, tn), lambda l, j: (l, 0, j)),
        out_shape=jax.ShapeDtypeStruct((L, D, NS), F32),
        compiler_params=_cp(("arbitrary", "arbitrary")),
    )(c_all, dmod)


def sum_leading(x, name):
    n, R, C = x.shape

    def body(x_ref, o_ref):
        acc = x_ref[0].astype(F32)
        for i in range(1, n):
            acc = acc + x_ref[i].astype(F32)
        o_ref[...] = acc

    tr = _row_tile(R, 64)
    return pl.pallas_call(
        body, name=name, grid=(R // tr,),
        in_specs=[pl.BlockSpec((n, tr, C), lambda i: (0, i, 0))],
        out_specs=pl.BlockSpec((tr, C), lambda i: (i, 0)),
        out_shape=jax.ShapeDtypeStruct((R, C), F32),
        compiler_params=_cp(("arbitrary",)),
    )(x)


def _row_tile(R, cap=512):
    if R <= cap:
        return R
    for cand in (512, 384, 352, 256, 128, 64, 32, 16, 8):
        if cand <= cap and R % cand == 0:
            return cand
    return R


def adamw(w, g, m, v, name):
    R, C = w.shape
    tr = _row_tile(R, 256 if C > 1024 else 512)
    bc1 = 1.0 - ADAM_B1 ** ADAM_STEP
    bc2 = 1.0 - ADAM_B2 ** ADAM_STEP

    def body(w_ref, g_ref, m_ref, v_ref, d_ref, nm_ref, nv_ref):
        gv = g_ref[...]
        nm = ADAM_B1 * m_ref[...] + (1.0 - ADAM_B1) * gv
        nv = ADAM_B2 * v_ref[...] + (1.0 - ADAM_B2) * (gv * gv)
        nm_ref[...] = nm
        nv_ref[...] = nv
        d_ref[...] = -ADAM_LR * ((nm / bc1) / (jnp.sqrt(nv / bc2) + ADAM_EPS) + ADAM_WD * w_ref[...])

    blk = pl.BlockSpec((tr, C), lambda i: (i, 0))
    sd = jax.ShapeDtypeStruct((R, C), F32)
    return pl.pallas_call(
        body, name=name, grid=(R // tr,),
        in_specs=[blk, blk, blk, blk], out_specs=[blk, blk, blk], out_shape=[sd, sd, sd],
        compiler_params=_cp(("arbitrary",)),
    )(w, g, m, v)


def _as2d(a):
    if a.ndim == 1:
        return a.reshape(1, -1)
    return a.reshape(-1, a.shape[-1])


def adamw_nd(w, g, m, v, name):
    outs = adamw(_as2d(w), _as2d(g.reshape(w.shape)), _as2d(m), _as2d(v), name)
    return tuple(o.reshape(w.shape) for o in outs)


def _place():
    x, y, c = lax.axis_index("x"), lax.axis_index("y"), lax.axis_index("c")
    chips = [(1 - x, y), (x, 1 - y), (1 - x, 1 - y)]
    return x, y, c, chips


def allgather_small(xs, name):
    m_per, n = xs.shape

    def body(x_ref, out_ref, send_sems, recv_sems, local_sem):
        x, y, c, chips = _place()
        me, sibling = (x, y, c), (x, y, 1 - c)

        def rows(px, py, pc):
            return out_ref.at[pl.ds((4 * px + 2 * py + pc) * m_per, m_per), :]

        def copy(k, block, to, src=None):
            return pltpu.make_async_remote_copy(
                src_ref=rows(*block) if src is None else src, dst_ref=rows(*block),
                send_sem=send_sems.at[k], recv_sem=recv_sems.at[k], device_id=to, device_id_type=MESH_T)

        mine = pltpu.make_async_copy(x_ref, rows(*me), local_sem)
        mine.start()
        first = [copy(0, me, sibling, src=x_ref)]
        first += [copy(1 + j, me, (*chip, c), src=x_ref) for j, chip in enumerate(chips)]
        for cp in first:
            cp.start()
        passed = [copy(4 + j, (*chip, c), sibling) for j, chip in enumerate(chips)]
        for j, chip in enumerate(chips):
            copy(1 + j, (*chip, c), me).wait_recv()
            passed[j].start()
        copy(0, sibling, me).wait_recv()
        for j, chip in enumerate(chips):
            copy(4 + j, (*chip, 1 - c), me).wait_recv()
        for cp in first + passed:
            cp.wait_send()
        mine.wait()

    return pl.pallas_call(
        body, name=name,
        out_shape=jax.ShapeDtypeStruct((N_DEV * m_per, n), xs.dtype),
        in_specs=[pl.BlockSpec(memory_space=pltpu.VMEM)],
        out_specs=pl.BlockSpec(memory_space=pltpu.VMEM),
        scratch_shapes=[pltpu.SemaphoreType.DMA((7,)), pltpu.SemaphoreType.DMA((7,)), pltpu.SemaphoreType.DMA],
        compiler_params=_cp(),
    )(xs)


def gather_weights(shards, name):
    n = len(shards)

    def body(*refs):
        ins, outs = refs[:n], refs[n:2 * n]
        send_sems, recv_sems, fsend_sems, frecv_sems, local_sems = refs[2 * n:]
        x, y, c, chips = _place()
        me_chip = 2 * x + y
        sibling = (x, y, 1 - c)
        locs = []
        for a in range(n):
            cp = pltpu.make_async_copy(ins[a], outs[a].at[me_chip], local_sems.at[a])
            cp.start()
            locs.append(cp)

        def ici(a, k, src_chip, to):
            return pltpu.make_async_remote_copy(
                src_ref=ins[a].at[c], dst_ref=outs[a].at[src_chip, c],
                send_sem=send_sems.at[a * 3 + k], recv_sem=recv_sems.at[a * 3 + k], device_id=to, device_id_type=MESH_T)

        def d2d(a, k, src_chip, half):
            return pltpu.make_async_remote_copy(
                src_ref=outs[a].at[src_chip, half], dst_ref=outs[a].at[src_chip, half],
                send_sem=fsend_sems.at[a * 3 + k], recv_sem=frecv_sems.at[a * 3 + k],
                device_id=sibling, device_id_type=MESH_T)

        firsts = []
        for k, (cx, cy) in enumerate(chips):
            for a in range(n):
                cp = ici(a, k, me_chip, (cx, cy, c))
                cp.start()
                firsts.append(cp)
        passed = []
        for k, (cx, cy) in enumerate(chips):
            for a in range(n):
                ici(a, k, 2 * cx + cy, (cx, cy, c)).wait_recv()
                cp = d2d(a, k, 2 * cx + cy, c)
                cp.start()
                passed.append(cp)
        for k, (cx, cy) in enumerate(chips):
            for a in range(n):
                d2d(a, k, 2 * cx + cy, 1 - c).wait_recv()
        for cp in firsts + passed:
            cp.wait_send()
        for cp in locs:
            cp.wait()

    return pl.pallas_call(
        body, name=name,
        out_shape=[jax.ShapeDtypeStruct((N_CHIPS,) + s.shape, s.dtype) for s in shards],
        in_specs=[ANY] * n, out_specs=[ANY] * n,
        scratch_shapes=[pltpu.SemaphoreType.DMA((3 * n,)), pltpu.SemaphoreType.DMA((3 * n,)),
                        pltpu.SemaphoreType.DMA((3 * n,)), pltpu.SemaphoreType.DMA((3 * n,)),
                        pltpu.SemaphoreType.DMA((n,))],
        compiler_params=_cp(),
    )(*shards)


def _half_rows(ref, half, rh):
    idx = (slice(None),) * (len(ref.shape) - 2) + (pl.ds(pl.multiple_of(half * rh, 16), rh), slice(None))
    return ref.at[idx]


def sibling_exchange_halves(grads, name):
    n = len(grads)

    def body(*refs):
        ins, outs = refs[:n], refs[n:2 * n]
        send_sems, recv_sems = refs[2 * n:]
        x, y, c, _ = _place()
        cps = []
        for a in range(n):
            rh = ins[a].shape[-2] // 2
            cp = pltpu.make_async_remote_copy(
                src_ref=_half_rows(ins[a], 1 - c, rh), dst_ref=outs[a],
                send_sem=send_sems.at[a], recv_sem=recv_sems.at[a], device_id=(x, y, 1 - c), device_id_type=MESH_T)
            cp.start()
            cps.append(cp)
        for cp in cps:
            cp.wait()

    return pl.pallas_call(
        body, name=name,
        out_shape=[jax.ShapeDtypeStruct(g.shape[:-2] + (g.shape[-2] // 2, g.shape[-1]), g.dtype) for g in grads],
        in_specs=[ANY] * n, out_specs=[ANY] * n,
        scratch_shapes=[pltpu.SemaphoreType.DMA((n,)), pltpu.SemaphoreType.DMA((n,))],
        compiler_params=_cp(),
    )(*grads)


def pair_sum(g, recv, cidx, name):
    L, NS, R, C = g.shape
    rh = R // 2
    tr = _row_tile(rh, 256 if C > 1024 else 512)
    nt = rh // tr

    def body(c_ref, g_ref, r_ref, o_ref):
        o_ref[...] = (g_ref[...].astype(F32) + r_ref[...].astype(F32)).astype(BF16)

    grid_spec = pltpu.PrefetchScalarGridSpec(
        num_scalar_prefetch=1, grid=(L, NS, nt),
        in_specs=[pl.BlockSpec((None, None, tr, C), lambda l, s, t, c: (l, s, c[0] * nt + t, 0)),
                  pl.BlockSpec((None, None, tr, C), lambda l, s, t, c: (l, s, t, 0))],
        out_specs=pl.BlockSpec((None, None, tr, C), lambda l, s, t, c: (l, s, t, 0)))
    return pl.pallas_call(
        body, name=name, grid_spec=grid_spec,
        out_shape=jax.ShapeDtypeStruct((L, NS, rh, C), BF16),
        compiler_params=_cp(("arbitrary", "arbitrary", "arbitrary")),
    )(cidx, g, recv)


def chip_exchange(parts, name):
    n = len(parts)

    def body(*refs):
        ins, outs = refs[:n], refs[n:2 * n]
        send_sems, recv_sems = refs[2 * n:]
        x, y, c, chips = _place()
        cps = []
        for k, (cx, cy) in enumerate(chips):
            for a in range(n):
                cp = pltpu.make_async_remote_copy(
                    src_ref=ins[a].at[:, 2 * cx + cy], dst_ref=outs[a].at[k],
                    send_sem=send_sems.at[a * 3 + k], recv_sem=recv_sems.at[a * 3 + k],
                    device_id=(cx, cy, c), device_id_type=MESH_T)
                cp.start()
                cps.append(cp)
        for cp in cps:
            cp.wait()

    return pl.pallas_call(
        body, name=name,
        out_shape=[jax.ShapeDtypeStruct((3, p.shape[0]) + p.shape[2:], p.dtype) for p in parts],
        in_specs=[ANY] * n, out_specs=[ANY] * n,
        scratch_shapes=[pltpu.SemaphoreType.DMA((3 * n,)), pltpu.SemaphoreType.DMA((3 * n,))],
        compiler_params=_cp(),
    )(*parts)


def chip_sum(part, recv, chip_c, name, dest=None, slot=0, n_slots=1):
    _, NS, RH, C = part.shape
    tr = _row_tile(RH, 256 if C > 1024 else 512)
    nt = RH // tr

    def body(cc_ref, p_ref, r_ref, *rest):
        o_ref = rest[-1]
        acc = p_ref[...].astype(F32)
        for k in range(3):
            acc = acc + r_ref[k].astype(F32)
        o_ref[...] = acc

    in_specs = [pl.BlockSpec((None, None, tr, C), lambda t, cc: (0, cc[0], t, 0)),
                pl.BlockSpec((3, None, tr, C), lambda t, cc: (0, 0, t, 0))]
    args = [chip_c, part, recv]
    aliases = {}
    if dest is not None:
        in_specs.append(ANY)
        args.append(dest)
        aliases = {3: 0}
    grid_spec = pltpu.PrefetchScalarGridSpec(
        num_scalar_prefetch=1, grid=(nt,), in_specs=in_specs,
        out_specs=pl.BlockSpec((None, tr, C), lambda t, cc: (slot, cc[1] * nt + t, 0)))
    return pl.pallas_call(
        body, name=name, grid_spec=grid_spec,
        out_shape=jax.ShapeDtypeStruct((n_slots, 2 * RH, C), F32),
        input_output_aliases=aliases,
        compiler_params=_cp(("arbitrary",)),
    )(*args)


def sibling_fill_halves(bufs, name):
    n = len(bufs)

    def body(*refs):
        outs = refs[n:2 * n]
        send_sems, recv_sems = refs[2 * n:]
        x, y, c, _ = _place()
        cps = []
        for a in range(n):
            rh = outs[a].shape[-2] // 2
            cp = pltpu.make_async_remote_copy(
                src_ref=_half_rows(outs[a], c, rh), dst_ref=_half_rows(outs[a], c, rh),
                send_sem=send_sems.at[a], recv_sem=recv_sems.at[a], device_id=(x, y, 1 - c), device_id_type=MESH_T)
            cp.start()
            cps.append(cp)
        for a, cp in enumerate(cps):
            cp.wait_send()
        for a in range(n):
            rh = outs[a].shape[-2] // 2
            pltpu.make_async_remote_copy(
                src_ref=_half_rows(outs[a], 1 - c, rh), dst_ref=_half_rows(outs[a], 1 - c, rh),
                send_sem=send_sems.at[a], recv_sem=recv_sems.at[a], device_id=(x, y, 1 - c),
                device_id_type=MESH_T).wait_recv()

    return pl.pallas_call(
        body, name=name,
        out_shape=[jax.ShapeDtypeStruct(b.shape, b.dtype) for b in bufs],
        in_specs=[ANY] * n, out_specs=[ANY] * n,
        input_output_aliases={a: a for a in range(n)},
        scratch_shapes=[pltpu.SemaphoreType.DMA((n,)), pltpu.SemaphoreType.DMA((n,))],
        compiler_params=_cp(),
    )(*bufs)


def _s5_prepare(lam_re, lam_im, log_dt, b_re, b_im, c_re, c_im, groups_per_block):
    G, P = lam_re.shape
    N = b_re.shape[-1]
    gb = groups_per_block
    nblk = G // gb
    dt = jnp.exp(log_dt)[:, None]
    e = jnp.exp(lam_re * dt)
    a_re = e * jnp.cos(lam_im * dt)
    a_im = e * jnp.sin(lam_im * dt)
    n2 = lam_re * lam_re + lam_im * lam_im
    co_re = ((a_re - 1.0) * lam_re + a_im * lam_im) / n2
    co_im = (a_im * lam_re - (a_re - 1.0) * lam_im) / n2
    bb_re = co_re[..., None] * b_re - co_im[..., None] * b_im
    bb_im = co_re[..., None] * b_im + co_im[..., None] * b_re
    eye = jnp.eye(gb, dtype=F32)

    def blockdiag_np(m):
        m = m.reshape(nblk, gb, N, P)
        return jnp.einsum('bgnp,gh->bgnhp', m, eye).reshape(nblk, gb * N, gb * P)

    b_np_re = jnp.swapaxes(bb_re, 1, 2)
    b_np_im = jnp.swapaxes(bb_im, 1, 2)
    return (a_re.reshape(nblk, 1, gb * P), a_im.reshape(nblk, 1, gb * P),
            blockdiag_np(b_np_re), blockdiag_np(b_np_im), blockdiag_np(c_re), blockdiag_np(c_im))


def _to_scan_order(a):
    S, D = a.shape
    return a.reshape(SCAN_LANES, S // SCAN_LANES, D).transpose(1, 0, 2).reshape(S, D)


def _from_scan_order(a):
    S, D = a.shape
    return a.reshape(S // SCAN_LANES, SCAN_LANES, D).transpose(1, 0, 2).reshape(S, D)


def _pad_rows(a, mult=8):
    r = (-a.shape[0]) % mult
    if r:
        a = jnp.concatenate([a, jnp.zeros((r, a.shape[1]), a.dtype)], axis=0)
    return a


def _pack_rows(arrs, width):
    parts, offs, o = [], [], 0
    for a in arrs:
        flat = a.reshape(-1)
        r = (-flat.shape[0]) % (16 * width)
        if r:
            flat = jnp.concatenate([flat, jnp.zeros((r,), flat.dtype)])
        p = flat.reshape(-1, width)
        parts.append(p)
        offs.append((o, a.shape, a.size))
        o += p.shape[0]
    if o % 64:
        parts.append(jnp.zeros((64 - o % 64, width), parts[0].dtype))
    return jnp.concatenate(parts, axis=0), offs


def _unpack_rows(packed, offs):
    outs = []
    for o, shape, size in offs:
        rows = -(-size // packed.shape[1])
        outs.append(packed[o:o + rows].reshape(-1)[:size].reshape(shape))
    return outs


def kernel(x, c, ada_w, ada_b, norm_pre, norm_post, ffn_w_in, ffn_w_out, ab_w_in, pool_w, pool_scale, sgu_ln_g, sgu_ln_b, sgu_w, sgu_b, ab_w_out, ssm_w_in, ssm_lam_re, ssm_lam_im, ssm_b_re, ssm_b_im, ssm_c_re, ssm_c_im, ssm_d, ssm_log_dt, ssm_w_glu, loss_target, m_ada_w, m_ada_b, m_norm_pre, m_norm_post, m_ffn_w_in, m_ffn_w_out, m_ab_w_in, m_pool_w, m_pool_scale, m_sgu_ln_g, m_sgu_ln_b, m_sgu_w, m_sgu_b, m_ab_w_out, m_ssm_w_in, m_ssm_lam_re, m_ssm_lam_im, m_ssm_b_re, m_ssm_b_im, m_ssm_c_re, m_ssm_c_im, m_ssm_d, m_ssm_log_dt, m_ssm_w_glu, v_ada_w, v_ada_b, v_norm_pre, v_norm_post, v_ffn_w_in, v_ffn_w_out, v_ab_w_in, v_pool_w, v_pool_scale, v_sgu_ln_g, v_sgu_ln_b, v_sgu_w, v_sgu_b, v_ab_w_out, v_ssm_w_in, v_ssm_lam_re, v_ssm_lam_im, v_ssm_b_re, v_ssm_b_im, v_ssm_c_re, v_ssm_c_im, v_ssm_d, v_ssm_log_dt, v_ssm_w_glu):
    weights = dict(ada_w=ada_w, ada_b=ada_b, norm_pre=norm_pre, norm_post=norm_post, ffn_w_in=ffn_w_in,
                   ffn_w_out=ffn_w_out, ab_w_in=ab_w_in, pool_w=pool_w, pool_scale=pool_scale, sgu_ln_g=sgu_ln_g,
                   sgu_ln_b=sgu_ln_b, sgu_w=sgu_w, sgu_b=sgu_b, ab_w_out=ab_w_out, ssm_w_in=ssm_w_in,
                   ssm_lam_re=ssm_lam_re, ssm_lam_im=ssm_lam_im, ssm_b_re=ssm_b_re, ssm_b_im=ssm_b_im,
                   ssm_c_re=ssm_c_re, ssm_c_im=ssm_c_im, ssm_d=ssm_d, ssm_log_dt=ssm_log_dt, ssm_w_glu=ssm_w_glu)
    m_in = dict(ada_w=m_ada_w, ada_b=m_ada_b, norm_pre=m_norm_pre, norm_post=m_norm_post, ffn_w_in=m_ffn_w_in,
                ffn_w_out=m_ffn_w_out, ab_w_in=m_ab_w_in, pool_w=m_pool_w, pool_scale=m_pool_scale,
                sgu_ln_g=m_sgu_ln_g, sgu_ln_b=m_sgu_ln_b, sgu_w=m_sgu_w, sgu_b=m_sgu_b, ab_w_out=m_ab_w_out,
                ssm_w_in=m_ssm_w_in, ssm_lam_re=m_ssm_lam_re, ssm_lam_im=m_ssm_lam_im, ssm_b_re=m_ssm_b_re,
                ssm_b_im=m_ssm_b_im, ssm_c_re=m_ssm_c_re, ssm_c_im=m_ssm_c_im, ssm_d=m_ssm_d,
                ssm_log_dt=m_ssm_log_dt, ssm_w_glu=m_ssm_w_glu)
    v_in = dict(ada_w=v_ada_w, ada_b=v_ada_b, norm_pre=v_norm_pre, norm_post=v_norm_post, ffn_w_in=v_ffn_w_in,
                ffn_w_out=v_ffn_w_out, ab_w_in=v_ab_w_in, pool_w=v_pool_w, pool_scale=v_pool_scale,
                sgu_ln_g=v_sgu_ln_g, sgu_ln_b=v_sgu_ln_b, sgu_w=v_sgu_w, sgu_b=v_sgu_b, ab_w_out=v_ab_w_out,
                ssm_w_in=v_ssm_w_in, ssm_lam_re=v_ssm_lam_re, ssm_lam_im=v_ssm_lam_im, ssm_b_re=v_ssm_b_re,
                ssm_b_im=v_ssm_b_im, ssm_c_re=v_ssm_c_re, ssm_c_im=v_ssm_c_im, ssm_d=v_ssm_d,
                ssm_log_dt=v_ssm_log_dt, ssm_w_glu=v_ssm_w_glu)
    names = list(weights.keys())

    xi, yi, ci = lax.axis_index("x"), lax.axis_index("y"), lax.axis_index("c")
    chip = 2 * xi + yi
    me = 4 * xi + 2 * yi + ci
    S, D = x.shape[1], x.shape[2]
    L = ada_w.shape[0]
    NSUB = norm_pre.shape[1]
    DS = norm_pre.shape[2]
    FS = ffn_w_in.shape[-1]
    FR = ffn_w_out.shape[-2]
    x0 = x[0]
    target = loss_target[0]

    small_parts = [_pad_rows(p) for p in (c.reshape(D // DS, DS), norm_pre.reshape(L * NSUB, DS),
                                          norm_post.reshape(L * NSUB, DS), ssm_d.reshape(1, DS))]
    small_offs = [0]
    for p in small_parts:
        small_offs.append(small_offs[-1] + p.shape[0])
    small_all = allgather_small(jnp.concatenate(small_parts, axis=0), "ag_small").reshape(N_DEV, -1, DS)
    c_all = small_all[:, :D // DS].reshape(N_DEV, D)
    per_chip = small_all[0::2]
    o = small_offs[1]
    g_pre_full = jnp.moveaxis(per_chip[:, o:o + L * NSUB], 0, 1).reshape(L, NSUB, D)
    o = small_offs[2]
    g_post_full = jnp.moveaxis(per_chip[:, o:o + L * NSUB], 0, 1).reshape(L, NSUB, D)
    o = small_offs[3]
    d_full = jnp.moveaxis(per_chip[:, o:o + 1], 0, 1).reshape(1, D)

    NS = ada_w.shape[-1]
    ada_b_shard = lax.dynamic_slice_in_dim(ada_b, chip * NS, NS, axis=1).reshape(L, 1, NS)
    mod_part = ada_mod(c_all, ada_w, ada_b_shard, "ada_mod")
    mod_all = allgather_small(mod_part.reshape(L * N_DEV, NS), "ag_mod").reshape(N_DEV, L, N_DEV, NS)
    mod_mine = lax.dynamic_index_in_dim(mod_all[0::2], me, axis=2, keepdims=False)
    mod = jnp.moveaxis(mod_mine, 0, 1).reshape(L, NSUB, 3, D)

    def vec_of(l, s):
        return jnp.concatenate([mod[l, s], g_pre_full[l, s][None], g_post_full[l, s][None],
                                jnp.zeros((3, D), F32)], axis=0)

    def halves(a):
        a = a.astype(BF16)
        if a.shape[0] == 2:
            return a
        a = a[0]
        return a.reshape((2, a.shape[0] // 2) + a.shape[1:])

    gathered = gather_weights([halves(ffn_w_in), halves(ffn_w_out), halves(ab_w_in), halves(ab_w_out),
                               halves(ssm_w_in), halves(ssm_w_glu)], "gather_weights")
    win_g, wout_g = gathered[0], gathered[1]
    abin_g = gathered[2].reshape(N_CHIPS, D, -1)
    about_g = gathered[3].reshape(-1, D)
    sin_g = gathered[4].reshape(1, -1, D)
    glu_g = gathered[5].reshape(N_CHIPS, D, -1)

    GB = 8
    s5_args = (ssm_lam_re[0], ssm_lam_im[0], ssm_log_dt[0], ssm_b_re[0], ssm_b_im[0], ssm_c_re[0], ssm_c_im[0])
    (a_re, a_im, bblk_re, bblk_im, cblk_re, cblk_im), s5_vjp = jax.vjp(lambda *p: _s5_prepare(*p, GB), *s5_args)
    bre16, bim16 = bblk_re.astype(BF16), bblk_im.astype(BF16)
    cre16, cim16 = cblk_re.astype(BF16), cblk_im.astype(BF16)
    cre16_t, cim16_t = jnp.swapaxes(cre16, 1, 2), jnp.swapaxes(cim16, 1, 2)

    pool_w16 = pool_w[0].astype(BF16)
    sgu_bexp = jnp.broadcast_to(sgu_b[0][:, :, None], sgu_w[0].shape)

    saved = {}
    xcur = x0
    for l in range(L):
        v0 = vec_of(l, 0)
        out, h, u, f = ffn_fwd(xcur, v0, win_g, wout_g, l, 0, f"ffn_fwd_{l}_0")
        saved[(l, 0)] = (xcur, v0, h, u, f)
        xcur = out
        v1 = vec_of(l, 1)
        if l % 2 == 0:
            h, z = pre_matmul(xcur, v1, abin_g, f"mixa_in_{l}")
            ycat = mixa_core_fwd(z, pool_w16, pool_scale, sgu_ln_g, sgu_ln_b, sgu_w[0], sgu_bexp, f"mixa_core_{l}")
            out, f = matmul_post(ycat, about_g, xcur, v1, f"mixa_out_{l}")
            saved[(l, 1)] = (xcur, v1, h, z, ycat, f)
        else:
            h, uu = pre_matmul(xcur, v1, sin_g, f"s5_in_{l}")
            us = _to_scan_order(uu)
            ys = s5_core_fwd(us, a_re, a_im, bre16, bim16, cre16_t, cim16_t, d_full, f"s5_core_{l}")
            yy = _from_scan_order(ys)
            out, f = s5_glu_fwd(yy, glu_g, xcur, v1, f"s5_glu_{l}")
            saved[(l, 1)] = (xcur, v1, h, us, yy, f)
        xcur = out
        v2 = vec_of(l, 2)
        out, h, u, f = ffn_fwd(xcur, v2, win_g, wout_g, l, 1, f"ffn_fwd_{l}_1")
        saved[(l, 2)] = (xcur, v2, h, u, f)
        xcur = out

    dcur, sq = loss_head(xcur, target, "loss_head")
    loss = lax.psum(sq[0, 0], ("x", "y", "c")) * (0.5 / D)

    accs = {}
    g_win = [[None, None] for _ in range(L)]
    g_wout = [[None, None] for _ in range(L)]
    small_g = {}

    def ffn_back(l, s, k, dcur):
        xin, vv, h, u, f = saved[(l, s)]
        dx, df, du, act, acc = ffn_bwd(dcur, xin, f, u, vv, win_g, wout_g, l, k, f"ffn_bwd_{l}_{k}")
        accs[(l, s)] = acc
        g_win[l][k] = tn_matmul(h, du, 512, f"ffn_dwin_{l}_{k}")
        g_wout[l][k] = tn_matmul_cols(act, df, FS, D, f"ffn_dwout_{l}_{k}")[0]
        return dx

    for l in reversed(range(L)):
        dcur = ffn_back(l, 2, 1, dcur)
        if l % 2 == 0:
            xin, vv, h, z, ycat, f = saved[(l, 1)]
            df, dact, acc_post = post_bwd_matmul(dcur, f, vv, about_g, f"mixa_out_bwd_{l}")
            g_about = tn_matmul_cols(ycat, df, 512, D, f"mixa_dwout_{l}")[0]
            dz, dpw, dvecw, dsw, dsb = mixa_core_bwd(z, dact, pool_w16, pool_scale, sgu_ln_g, sgu_ln_b, sgu_w[0],
                                                     sgu_bexp, f"mixa_core_bwd_{l}")
            g_abin = tn_matmul_cols(h, dz, 512, abin_g.shape[-1], f"mixa_dwin_{l}")
            dcur, acc_pre = matmul_pre_bwd(dz, abin_g, xin, dcur, vv, f"mixa_in_bwd_{l}")
            accs[(l, 1)] = acc_pre + acc_post
            small_g.update(pool_w=dpw[None], pool_scale=dvecw[0:1], sgu_ln_g=dvecw[1:2], sgu_ln_b=dvecw[2:3],
                           sgu_w=dsw[None], sgu_b=dsb.T[None])
        else:
            xin, vv, h, us, yy, f = saved[(l, 1)]
            dy, dab, gact, acc_post = s5_glu_bwd(dcur, f, yy, vv, glu_g, f"s5_glu_bwd_{l}")
            g_glu = tn_matmul_cols(gact, dab, 512, glu_g.shape[-1], f"s5_dwglu_{l}")
            dys = _to_scan_order(dy)
            dus, dbre, dbim, dcre_t, dcim_t, dar, dai, dd = s5_core_bwd(
                us, dys, a_re, a_im, bre16, bim16, cre16, cim16, d_full, f"s5_core_bwd_{l}")
            du = _from_scan_order(dus).astype(BF16)
            g_sin = tn_matmul_cols(h, du, 512, D, f"s5_dwin_{l}")[0]
            dcur, acc_pre = matmul_pre_bwd(du, sin_g, xin, dcur, vv, f"s5_in_bwd_{l}")
            accs[(l, 1)] = acc_pre + acc_post
            s5_grads = s5_vjp((dar, dai, dbre, dbim, jnp.swapaxes(dcre_t, 1, 2), jnp.swapaxes(dcim_t, 1, 2)))
            small_g.update(ssm_lam_re=s5_grads[0][None], ssm_lam_im=s5_grads[1][None], ssm_log_dt=s5_grads[2][None],
                           ssm_b_re=s5_grads[3][None], ssm_b_im=s5_grads[4][None], ssm_c_re=s5_grads[5][None],
                           ssm_c_im=s5_grads[6][None])
            dd_mine = dd
        dcur = ffn_back(l, 0, 0, dcur)
    grad_x = dcur[None]

    acc_all = jnp.stack([jnp.stack([accs[(l, s)] for s in range(NSUB)]) for l in range(L)])
    dmod_mine = acc_all[:, :, A_SHIFT:A_GATE + 1].reshape(L, NSUB * 3 * D)
    dgpre_mine = acc_all[:, :, A_GPRE]
    dgpost_mine = acc_all[:, :, A_GPOST]
    small_names = ["pool_w", "pool_scale", "sgu_ln_g", "sgu_ln_b", "sgu_w", "sgu_b", "ssm_lam_re", "ssm_lam_im",
                   "ssm_b_re", "ssm_b_im", "ssm_c_re", "ssm_c_im", "ssm_log_dt"]
    packed, offs = _pack_rows([dmod_mine, dgpre_mine, dgpost_mine, dd_mine] + [small_g[n] for n in small_names], D)
    rows = packed.shape[0]
    packed_all = allgather_small(packed.astype(BF16), "ag_grads").reshape(N_DEV, rows, D)
    summed = _unpack_rows(sum_leading(packed_all, "sum_small"), offs)
    grads = {n: g for n, g in zip(small_names, summed[4:])}
    grads["ada_b"] = summed[0]
    grads["norm_pre"] = lax.dynamic_slice_in_dim(summed[1], chip * DS, DS, axis=2)
    grads["norm_post"] = lax.dynamic_slice_in_dim(summed[2], chip * DS, DS, axis=2)
    grads["ssm_d"] = lax.dynamic_slice_in_dim(summed[3], chip * DS, DS, axis=1)
    dmod_all = jnp.moveaxis(packed_all[:, :L * NSUB * 3].reshape(N_DEV, L, NSUB * 3 * D), 0, 1)
    dmod_shard = lax.dynamic_slice_in_dim(dmod_all, chip * NS, NS, axis=2)
    grads["ada_w"] = ada_grad(c_all, dmod_shard, "ada_grad")

    big = [(g_win[l][k][None], "ffn_w_in", 2 * l + k, 2 * L) for l in range(L) for k in range(2)]
    big += [(g_wout[l][k].reshape(1, N_CHIPS, FR, D), "ffn_w_out", 2 * l + k, 2 * L) for l in range(L) for k in range(2)]
    big += [(g_abin[None], "ab_w_in", 0, 1), (g_about.reshape(1, N_CHIPS, -1, D), "ab_w_out", 0, 1),
            (g_sin.reshape(1, N_CHIPS, -1, D), "ssm_w_in", 0, 1), (g_glu[None], "ssm_w_glu", 0, 1)]
    cidx = ci.reshape(1).astype(jnp.int32)
    chip_c = jnp.stack([chip, ci]).astype(jnp.int32)
    recv_a = sibling_exchange_halves([b[0] for b in big], "rs_pair_exchange")
    parts = [pair_sum(b[0], r, cidx, f"rs_pair_sum_{i}") for i, (b, r) in enumerate(zip(big, recv_a))]
    recv_b = chip_exchange(parts, "rs_chip_exchange")
    fin = {}
    for i, (b, p, r) in enumerate(zip(big, parts, recv_b)):
        fin[b[1]] = chip_sum(p, r, chip_c, f"rs_chip_sum_{i}", dest=fin.get(b[1]), slot=b[2], n_slots=b[3])
    big_names = ["ffn_w_in", "ffn_w_out", "ab_w_in", "ab_w_out", "ssm_w_in", "ssm_w_glu"]
    filled = sibling_fill_halves([fin[n] for n in big_names], "rs_fill")
    for n, g in zip(big_names, filled):
        grads[n] = g

    deltas, new_m, new_v = {}, {}, {}
    for n in names:
        g = grads[n].reshape(weights[n].shape)
        grads[n] = g
        deltas[n], new_m[n], new_v[n] = adamw_nd(weights[n], g, m_in[n], v_in[n], f"adamw_{n}")

    return (loss, grad_x, *[grads[n] for n in names], *[deltas[n] for n in names],
            *[new_m[n] for n in names], *[new_v[n] for n in names])
```

```python
import math

import jax
import jax.numpy as jnp
from jax import lax
from jax.experimental import pallas as pl
from jax.experimental.pallas import tpu as pltpu

F32 = jnp.float32
BF16 = jnp.bfloat16
EPS = 1e-6
MESH_T = pl.DeviceIdType.MESH
VMEM_LIMIT_BYTES = 56 * 1024 * 1024
N_CHIPS = 4
N_DEV = 8
POOL_WINDOWS = (2, 4, 8, 16)
CHUNK = 128
SCAN_LANES = 8
ADAM_LR = 0.001
ADAM_B1 = 0.9
ADAM_B2 = 0.999
ADAM_EPS = 1e-08
ADAM_WD = 0.01
ADAM_STEP = 10
GELU_C = math.sqrt(2.0 / math.pi)
GELU_K = 0.044715

V_SHIFT, V_SCALE, V_GATE, V_GPRE, V_GPOST = 0, 1, 2, 3, 4
A_SHIFT, A_SCALE, A_GATE, A_GPRE, A_GPOST = 0, 1, 2, 3, 4

ANY = pl.BlockSpec(memory_space=pl.ANY)


def _cp(sem=None):
    if sem is None:
        return pltpu.CompilerParams(vmem_limit_bytes=VMEM_LIMIT_BYTES)
    return pltpu.CompilerParams(vmem_limit_bytes=VMEM_LIMIT_BYTES, dimension_semantics=sem)


def _dot(a, b):
    return jnp.dot(a, b, preferred_element_type=F32)


def _dot_nt(a, b):
    return lax.dot_general(a, b, (((1,), (1,)), ((), ())), preferred_element_type=F32)


def _dot_tn(a, b):
    return lax.dot_general(a, b, (((0,), (0,)), ((), ())), preferred_element_type=F32)


def _sigmoid(x):
    return 1.0 / (1.0 + jnp.exp(-x))


def _gelu(x):
    return 0.5 * x * (1.0 + jnp.tanh(GELU_C * (x + GELU_K * x * x * x)))


def _gelu_grad(x):
    t = jnp.tanh(GELU_C * (x + GELU_K * x * x * x))
    return 0.5 * (1.0 + t) + 0.5 * x * (1.0 - t * t) * GELU_C * (1.0 + 3.0 * GELU_K * x * x)


def _rowsum(v):
    return jnp.sum(v, axis=0, keepdims=True)


def _lanemean(v):
    return jnp.mean(v, axis=-1, keepdims=True)


def _row(ref, i):
    return ref[pl.ds(i, 1), :]


def _pre_fwd(x, vec_ref):
    r = lax.rsqrt(_lanemean(x * x) + EPS)
    return (x * r) * _row(vec_ref, V_GPRE) * (1.0 + _row(vec_ref, V_SCALE)) + _row(vec_ref, V_SHIFT)


def _pre_bwd(x, dh, vec_ref):
    g = _row(vec_ref, V_GPRE)
    sc = 1.0 + _row(vec_ref, V_SCALE)
    r = lax.rsqrt(_lanemean(x * x) + EPS)
    xn = x * r
    dhx = dh * xn
    t2 = dh * (g * sc)
    dx = r * (t2 - xn * _lanemean(t2 * xn))
    return dx, _rowsum(dh), _rowsum(dhx * g), _rowsum(dhx * sc)


def _post_fwd(x, f, vec_ref, rw):
    q = lax.rsqrt(_lanemean(f * f) + EPS)
    return x + (rw * _row(vec_ref, V_GATE)) * (f * q * _row(vec_ref, V_GPOST))


def _post_bwd(dout, f, vec_ref, rw):
    gp = _row(vec_ref, V_GPOST)
    gate = _row(vec_ref, V_GATE)
    q = lax.rsqrt(_lanemean(f * f) + EPS)
    fhat = f * q
    dgate = _rowsum(dout * (rw * fhat * gp))
    dy = dout * (rw * gate)
    dgpost = _rowsum(dy * fhat)
    t = dy * gp
    df = q * (t - fhat * _lanemean(t * fhat))
    return df, dgate, dgpost


def _acc_add(acc_ref, row, v):
    acc_ref[pl.ds(row, 1), :] += v


def ffn_fwd(x, vec, win_g, wout_g, l, k, name, tm=256):
    S, D = x.shape
    FS = win_g.shape[-1]
    FR = wout_g.shape[-2]

    def body(x_ref, vec_ref, win_hbm, wout_hbm, out_ref, h_ref, u_ref, f_ref, win_s, wout_s, sem):
        @pl.when(pl.program_id(0) == 0)
        def _():
            cps = [pltpu.make_async_copy(win_hbm.at[j, l, k], win_s.at[j], sem.at[j]) for j in range(N_CHIPS)]
            cps += [pltpu.make_async_copy(wout_hbm.at[j, l, k], wout_s.at[pl.ds(j * FR, FR), :], sem.at[N_CHIPS + j])
                    for j in range(N_CHIPS)]
            for cp in cps:
                cp.start()
            for cp in cps:
                cp.wait()

        xv = x_ref[...]
        h = _pre_fwd(xv, vec_ref).astype(BF16)
        h_ref[...] = h
        f = jnp.zeros((tm, D), F32)
        for j in range(2):
            a = _dot(h, win_s[j])
            b = _dot(h, win_s[2 + j])
            u_ref[j] = a.astype(BF16)
            u_ref[2 + j] = b.astype(BF16)
            act = (a * _sigmoid(a) * b).astype(BF16)
            f = f + _dot(act, wout_s[pl.ds(j * FS, FS), :])
        f_ref[...] = f
        out_ref[...] = _post_fwd(xv, f, vec_ref, 0.5)

    row = pl.BlockSpec((tm, D), lambda i: (i, 0))
    return pl.pallas_call(
        body, name=name, grid=(S // tm,),
        in_specs=[row, pl.BlockSpec((8, D), lambda i: (0, 0)), ANY, ANY],
        out_specs=[row, row, pl.BlockSpec((N_CHIPS, tm, FS), lambda i: (0, i, 0)), row],
        out_shape=[jax.ShapeDtypeStruct((S, D), F32), jax.ShapeDtypeStruct((S, D), BF16),
                   jax.ShapeDtypeStruct((N_CHIPS, S, FS), BF16), jax.ShapeDtypeStruct((S, D), F32)],
        scratch_shapes=[pltpu.VMEM((N_CHIPS, D, FS), BF16), pltpu.VMEM((N_CHIPS * FR, D), BF16),
                        pltpu.SemaphoreType.DMA((2 * N_CHIPS,))],
        compiler_params=_cp(("arbitrary",)),
    )(x, vec, win_g, wout_g)


def ffn_bwd(dout, x, f, u, vec, win_g, wout_g, l, k, name, tm=256):
    S, D = x.shape
    FS = win_g.shape[-1]
    FR = wout_g.shape[-2]

    def body(dout_ref, x_ref, f_ref, u_ref, vec_ref, win_hbm, wout_hbm,
             dx_ref, df_ref, du_ref, act_ref, acc_ref, win_s, wout_s, sem):
        @pl.when(pl.program_id(0) == 0)
        def _():
            cps = [pltpu.make_async_copy(win_hbm.at[j, l, k], win_s.at[j], sem.at[j]) for j in range(N_CHIPS)]
            cps += [pltpu.make_async_copy(wout_hbm.at[j, l, k], wout_s.at[pl.ds(j * FR, FR), :], sem.at[N_CHIPS + j])
                    for j in range(N_CHIPS)]
            for cp in cps:
                cp.start()
            acc_ref[...] = jnp.zeros_like(acc_ref)
            for cp in cps:
                cp.wait()

        dout_v = dout_ref[...]
        df, dgate, dgpost = _post_bwd(dout_v, f_ref[...], vec_ref, 0.5)
        dfb = df.astype(BF16)
        df_ref[...] = dfb
        dh = jnp.zeros((tm, D), F32)
        for j in range(2):
            a = u_ref[j].astype(F32)
            b = u_ref[2 + j].astype(F32)
            sig = _sigmoid(a)
            sl = a * sig
            dact = _dot_nt(dfb, wout_s[pl.ds(j * FS, FS), :])
            da = (dact * b * (sig * (1.0 + a * (1.0 - sig)))).astype(BF16)
            db = (dact * sl).astype(BF16)
            du_ref[j] = da
            du_ref[2 + j] = db
            act_ref[:, j * FS:(j + 1) * FS] = (sl * b).astype(BF16)
            dh = dh + _dot_nt(da, win_s[j]) + _dot_nt(db, win_s[2 + j])
        dx, dshift, dscale, dgpre = _pre_bwd(x_ref[...], dh, vec_ref)
        dx_ref[...] = dout_v + dx
        _acc_add(acc_ref, A_SHIFT, dshift)
        _acc_add(acc_ref, A_SCALE, dscale)
        _acc_add(acc_ref, A_GATE, dgate)
        _acc_add(acc_ref, A_GPRE, dgpre)
        _acc_add(acc_ref, A_GPOST, dgpost)

    row = pl.BlockSpec((tm, D), lambda i: (i, 0))
    ublk = pl.BlockSpec((N_CHIPS, tm, FS), lambda i: (0, i, 0))
    const = pl.BlockSpec((8, D), lambda i: (0, 0))
    return pl.pallas_call(
        body, name=name, grid=(S // tm,),
        in_specs=[row, row, row, ublk, const, ANY, ANY],
        out_specs=[row, row, ublk, pl.BlockSpec((tm, 2 * FS), lambda i: (i, 0)), const],
        out_shape=[jax.ShapeDtypeStruct((S, D), F32), jax.ShapeDtypeStruct((S, D), BF16),
                   jax.ShapeDtypeStruct((N_CHIPS, S, FS), BF16), jax.ShapeDtypeStruct((S, 2 * FS), BF16),
                   jax.ShapeDtypeStruct((8, D), F32)],
        scratch_shapes=[pltpu.VMEM((N_CHIPS, D, FS), BF16), pltpu.VMEM((N_CHIPS * FR, D), BF16),
                        pltpu.SemaphoreType.DMA((2 * N_CHIPS,))],
        compiler_params=_cp(("arbitrary",)),
    )(dout, x, f, u, vec, win_g, wout_g)


def tn_matmul(a, b, tk, name):
    S, K = a.shape
    nb, _, tn = b.shape

    def body(a_ref, b_ref, o_ref):
        o_ref[...] = _dot_tn(a_ref[...], b_ref[...]).astype(BF16)

    return pl.pallas_call(
        body, name=name, grid=(K // tk, nb),
        in_specs=[pl.BlockSpec((S, tk), lambda i, j: (0, i)), pl.BlockSpec((None, S, tn), lambda i, j: (j, 0, 0))],
        out_specs=pl.BlockSpec((None, tk, tn), lambda i, j: (j, i, 0)),
        out_shape=jax.ShapeDtypeStruct((nb, K, tn), BF16),
        compiler_params=_cp(("arbitrary", "arbitrary")),
    )(a, b)


def tn_matmul_cols(a, b, tk, tn, name):
    S, K = a.shape
    N = b.shape[1]

    def body(a_ref, b_ref, o_ref):
        o_ref[...] = _dot_tn(a_ref[...], b_ref[...]).astype(BF16)

    return pl.pallas_call(
        body, name=name, grid=(K // tk, N // tn),
        in_specs=[pl.BlockSpec((S, tk), lambda i, j: (0, i)), pl.BlockSpec((S, tn), lambda i, j: (0, j))],
        out_specs=pl.BlockSpec((None, tk, tn), lambda i, j: (j, i, 0)),
        out_shape=jax.ShapeDtypeStruct((N // tn, K, tn), BF16),
        compiler_params=_cp(("arbitrary", "arbitrary")),
    )(a, b)


def pre_matmul(x, vec, w3, name, tm=256):
    S, D = x.shape
    nj, _, Nj = w3.shape

    def body(x_ref, vec_ref, w_ref, h_ref, z_ref):
        h = _pre_fwd(x_ref[...], vec_ref).astype(BF16)
        h_ref[...] = h
        for j in range(nj):
            z_ref[:, j * Nj:(j + 1) * Nj] = _dot(h, w_ref[j])

    row = pl.BlockSpec((tm, D), lambda i: (i, 0))
    return pl.pallas_call(
        body, name=name, grid=(S // tm,),
        in_specs=[row, pl.BlockSpec((8, D), lambda i: (0, 0)), pl.BlockSpec((nj, D, Nj), lambda i: (0, 0, 0))],
        out_specs=[row, pl.BlockSpec((tm, nj * Nj), lambda i: (i, 0))],
        out_shape=[jax.ShapeDtypeStruct((S, D), BF16), jax.ShapeDtypeStruct((S, nj * Nj), F32)],
        compiler_params=_cp(("arbitrary",)),
    )(x, vec, w3)


def matmul_pre_bwd(dz, w3, x, dres, vec, name, tm=256):
    S, D = x.shape
    nj, _, Nj = w3.shape

    def body(dz_ref, w_ref, x_ref, dres_ref, vec_ref, dx_ref, acc_ref):
        @pl.when(pl.program_id(0) == 0)
        def _():
            acc_ref[...] = jnp.zeros_like(acc_ref)

        dh = jnp.zeros((tm, D), F32)
        for j in range(nj):
            dh = dh + _dot_nt(dz_ref[:, j * Nj:(j + 1) * Nj], w_ref[j])
        dx, dshift, dscale, dgpre = _pre_bwd(x_ref[...], dh, vec_ref)
        dx_ref[...] = dres_ref[...] + dx
        _acc_add(acc_ref, A_SHIFT, dshift)
        _acc_add(acc_ref, A_SCALE, dscale)
        _acc_add(acc_ref, A_GPRE, dgpre)

    row = pl.BlockSpec((tm, D), lambda i: (i, 0))
    const = pl.BlockSpec((8, D), lambda i: (0, 0))
    return pl.pallas_call(
        body, name=name, grid=(S // tm,),
        in_specs=[pl.BlockSpec((tm, nj * Nj), lambda i: (i, 0)), pl.BlockSpec((nj, D, Nj), lambda i: (0, 0, 0)),
                  row, row, const],
        out_specs=[row, const],
        out_shape=[jax.ShapeDtypeStruct((S, D), F32), jax.ShapeDtypeStruct((8, D), F32)],
        compiler_params=_cp(("arbitrary",)),
    )(dz, w3, x, dres, vec)


def matmul_post(act, w, x, vec, name, tm=256):
    S, D = x.shape
    K = act.shape[1]

    def body(act_ref, w_ref, x_ref, vec_ref, out_ref, f_ref):
        f = _dot(act_ref[...], w_ref[...])
        f_ref[...] = f
        out_ref[...] = _post_fwd(x_ref[...], f, vec_ref, 1.0)

    row = pl.BlockSpec((tm, D), lambda i: (i, 0))
    return pl.pallas_call(
        body, name=name, grid=(S // tm,),
        in_specs=[pl.BlockSpec((tm, K), lambda i: (i, 0)), pl.BlockSpec((K, D), lambda i: (0, 0)), row,
                  pl.BlockSpec((8, D), lambda i: (0, 0))],
        out_specs=[row, row],
        out_shape=[jax.ShapeDtypeStruct((S, D), F32), jax.ShapeDtypeStruct((S, D), F32)],
        compiler_params=_cp(("arbitrary",)),
    )(act, w, x, vec)


def post_bwd_matmul(dout, f, vec, w, name, tm=256):
    S, D = dout.shape
    K = w.shape[0]

    def body(dout_ref, f_ref, vec_ref, w_ref, df_ref, dact_ref, acc_ref):
        @pl.when(pl.program_id(0) == 0)
        def _():
            acc_ref[...] = jnp.zeros_like(acc_ref)

        df, dgate, dgpost = _post_bwd(dout_ref[...], f_ref[...], vec_ref, 1.0)
        dfb = df.astype(BF16)
        df_ref[...] = dfb
        dact_ref[...] = _dot_nt(dfb, w_ref[...])
        _acc_add(acc_ref, A_GATE, dgate)
        _acc_add(acc_ref, A_GPOST, dgpost)

    row = pl.BlockSpec((tm, D), lambda i: (i, 0))
    const = pl.BlockSpec((8, D), lambda i: (0, 0))
    return pl.pallas_call(
        body, name=name, grid=(S // tm,),
        in_specs=[row, row, const, pl.BlockSpec((K, D), lambda i: (0, 0))],
        out_specs=[row, pl.BlockSpec((tm, K), lambda i: (i, 0)), const],
        out_shape=[jax.ShapeDtypeStruct((S, D), BF16), jax.ShapeDtypeStruct((S, K), F32),
                   jax.ShapeDtypeStruct((8, D), F32)],
        compiler_params=_cp(("arbitrary",)),
    )(dout, f, vec, w)


def _band(w, transposed, prev):
    r = lax.broadcasted_iota(jnp.int32, (CHUNK, CHUNK), 1 if transposed else 0)
    c = lax.broadcasted_iota(jnp.int32, (CHUNK, CHUNK), 0 if transposed else 1)
    d = r - c
    m = (d + CHUNK < w) if prev else ((d >= 0) & (d < w))
    return jnp.where(m, 1.0, 0.0).astype(BF16)


def _split_hi_lo(a):
    hi = a.astype(BF16)
    lo = (a - hi.astype(F32)).astype(BF16)
    return hi, lo


def _pool_diff(a, ap, g, denom):
    w = POOL_WINDOWS[g]
    a_hi, a_lo = _split_hi_lo(a)
    p_hi, p_lo = _split_hi_lo(ap)
    mc = _band(w, False, False)
    mp = _band(w, False, True)
    win = _dot(mc, a_hi) + _dot(mc, a_lo) + _dot(mp, p_hi) + _dot(mp, p_lo)
    return win / denom - a


def _sgu_norm(v, lg, lb):
    mu = _lanemean(v)
    xc = v - mu
    rstd = lax.rsqrt(_lanemean(xc * xc) + EPS)
    xhat = xc * rstd
    return xhat, rstd, xhat * lg + lb


def _tril_mask():
    r = lax.broadcasted_iota(jnp.int32, (CHUNK, CHUNK), 0)
    c = lax.broadcasted_iota(jnp.int32, (CHUNK, CHUNK), 1)
    return r >= c


def _positions(i, w):
    r = lax.broadcasted_iota(jnp.int32, (CHUNK, 128), 0)
    pos = (i * CHUNK + r + 1).astype(F32)
    return jnp.minimum(pos, float(w))


def mixa_core_fwd(z, pool_w, pool_scale, ln_g, ln_b, sgu_w, sgu_bexp, name):
    S = z.shape[0]
    W = z.shape[1] // 3
    G = len(POOL_WINDOWS)
    GD = W // G

    def body(zc_ref, zp_ref, pw_ref, ps_ref, lg_ref, lb_ref, sw_ref, sb_ref, y_ref):
        i = pl.program_id(0)
        has_prev = jnp.where(i > 0, 1.0, 0.0)
        for g in range(G):
            sl = slice(g * GD, (g + 1) * GD)
            a = zc_ref[:, sl]
            ap = zp_ref[:, sl] * has_prev
            d = _pool_diff(a, ap, g, _positions(i, POOL_WINDOWS[g])).astype(BF16)
            y_ref[:, sl] = (_dot(d, pw_ref[g]) * ps_ref[:, sl]).astype(BF16)
        tril = _tril_mask()
        for hh in range(G):
            u = _gelu(zc_ref[:, W + hh * GD:W + (hh + 1) * GD])
            v = _gelu(zc_ref[:, 2 * W + hh * GD:2 * W + (hh + 1) * GD])
            sl = slice(hh * GD, (hh + 1) * GD)
            _, _, vn = _sgu_norm(v, lg_ref[:, sl], lb_ref[:, sl])
            wm = jnp.where(tril, sw_ref[hh], 0.0).astype(BF16)
            s = _dot(wm, vn.astype(BF16)) + sb_ref[hh]
            y_ref[:, W + hh * GD:W + (hh + 1) * GD] = (u * s).astype(BF16)

    vecw = pl.BlockSpec((1, W), lambda i: (0, 0))
    mats = pl.BlockSpec((G, GD, GD), lambda i: (0, 0, 0))
    return pl.pallas_call(
        body, name=name, grid=(S // CHUNK,),
        in_specs=[pl.BlockSpec((CHUNK, 3 * W), lambda i: (i, 0)),
                  pl.BlockSpec((CHUNK, W), lambda i: (jnp.maximum(i - 1, 0), 0)),
                  mats, vecw, vecw, vecw, mats, mats],
        out_specs=pl.BlockSpec((CHUNK, 2 * W), lambda i: (i, 0)),
        out_shape=jax.ShapeDtypeStruct((S, 2 * W), BF16),
        compiler_params=_cp(("arbitrary",)),
    )(z, z, pool_w, pool_scale, ln_g, ln_b, sgu_w, sgu_bexp)


def mixa_core_bwd(z, dy, pool_w, pool_scale, ln_g, ln_b, sgu_w, sgu_bexp, name):
    S = z.shape[0]
    W = z.shape[1] // 3
    G = len(POOL_WINDOWS)
    GD = W // G
    n_tiles = S // CHUNK

    def body(zc_ref, zp_ref, dyc_ref, dyn_ref, pw_ref, ps_ref, lg_ref, lb_ref, sw_ref, sb_ref,
             dz_ref, dpw_ref, dvec_ref, dsw_ref, dsb_ref):
        i = pl.program_id(0)

        @pl.when(i == 0)
        def _():
            dpw_ref[...] = jnp.zeros_like(dpw_ref)
            dvec_ref[...] = jnp.zeros_like(dvec_ref)
            dsw_ref[...] = jnp.zeros_like(dsw_ref)
            dsb_ref[...] = jnp.zeros_like(dsb_ref)

        has_prev = jnp.where(i > 0, 1.0, 0.0)
        has_next = jnp.where(i < n_tiles - 1, 1.0, 0.0)
        for g in range(G):
            w = POOL_WINDOWS[g]
            sl = slice(g * GD, (g + 1) * GD)
            a = zc_ref[:, sl]
            ap = zp_ref[:, sl] * has_prev
            den_c = _positions(i, w)
            den_n = _positions(i + 1, w)
            d = _pool_diff(a, ap, g, den_c).astype(BF16)
            ps = ps_ref[:, sl]
            pw = pw_ref[g]
            dyc = dyc_ref[:, sl]
            dvec_ref[pl.ds(0, 1), sl] += _rowsum(dyc * _dot(d, pw))
            dyp_c = (dyc * ps).astype(BF16)
            dyp_n = (dyn_ref[:, sl] * (ps * has_next)).astype(BF16)
            dpw_ref[g] += _dot_tn(d, dyp_c)
            dd_c = _dot_nt(dyp_c, pw)
            dd_n = _dot_nt(dyp_n, pw)
            da = (_dot(_band(w, True, False), (dd_c / den_c).astype(BF16))
                  + _dot(_band(w, True, True), (dd_n / den_n).astype(BF16)) - dd_c)
            dz_ref[:, sl] = da.astype(BF16)
        tril = _tril_mask()
        for hh in range(G):
            sl = slice(hh * GD, (hh + 1) * GD)
            zu = zc_ref[:, W + hh * GD:W + (hh + 1) * GD]
            zv = zc_ref[:, 2 * W + hh * GD:2 * W + (hh + 1) * GD]
            u = _gelu(zu)
            v = _gelu(zv)
            lg = lg_ref[:, sl]
            xhat, rstd, vn = _sgu_norm(v, lg, lb_ref[:, sl])
            vnb = vn.astype(BF16)
            wm = jnp.where(tril, sw_ref[hh], 0.0).astype(BF16)
            s = _dot(wm, vnb) + sb_ref[hh]
            dyb = dyc_ref[:, W + hh * GD:W + (hh + 1) * GD]
            du = dyb * s
            ds = dyb * u
            dsb_ref[:, hh:hh + 1] += jnp.sum(ds, axis=1, keepdims=True)
            dsb16 = ds.astype(BF16)
            dsw_ref[hh] += jnp.where(tril, _dot_nt(dsb16, vnb), 0.0)
            dvn = _dot_tn(wm, dsb16)
            dvec_ref[pl.ds(1, 1), sl] += _rowsum(dvn * xhat)
            dvec_ref[pl.ds(2, 1), sl] += _rowsum(dvn)
            dxh = dvn * lg
            dv = rstd * (dxh - _lanemean(dxh) - xhat * _lanemean(dxh * xhat))
            dz_ref[:, W + hh * GD:W + (hh + 1) * GD] = (du * _gelu_grad(zu)).astype(BF16)
            dz_ref[:, 2 * W + hh * GD:2 * W + (hh + 1) * GD] = (dv * _gelu_grad(zv)).astype(BF16)

    vecw = pl.BlockSpec((1, W), lambda i: (0, 0))
    mats = pl.BlockSpec((G, GD, GD), lambda i: (0, 0, 0))
    return pl.pallas_call(
        body, name=name, grid=(n_tiles,),
        in_specs=[pl.BlockSpec((CHUNK, 3 * W), lambda i: (i, 0)),
                  pl.BlockSpec((CHUNK, W), lambda i: (jnp.maximum(i - 1, 0), 0)),
                  pl.BlockSpec((CHUNK, 2 * W), lambda i: (i, 0)),
                  pl.BlockSpec((CHUNK, W), lambda i: (jnp.minimum(i + 1, n_tiles - 1), 0)),
                  mats, vecw, vecw, vecw, mats, mats],
        out_specs=[pl.BlockSpec((CHUNK, 3 * W), lambda i: (i, 0)), mats,
                   pl.BlockSpec((8, W), lambda i: (0, 0)), mats, pl.BlockSpec((CHUNK, G), lambda i: (0, 0))],
        out_shape=[jax.ShapeDtypeStruct((S, 3 * W), BF16), jax.ShapeDtypeStruct((G, GD, GD), F32),
                   jax.ShapeDtypeStruct((8, W), F32), jax.ShapeDtypeStruct((G, CHUNK, CHUNK), F32),
                   jax.ShapeDtypeStruct((CHUNK, G), F32)],
        compiler_params=_cp(("arbitrary",)),
    )(z, z, dy, dy, pool_w, pool_scale, ln_g, ln_b, sgu_w, sgu_bexp)


def _cmul(ar, ai, br, bi):
    return ar * br - ai * bi, ar * bi + ai * br


def _cpow(ar, ai, n):
    rr, ri = None, None
    br, bi = ar, ai
    while n:
        if n & 1:
            rr, ri = (br, bi) if rr is None else _cmul(rr, ri, br, bi)
        n >>= 1
        if n:
            br, bi = _cmul(br, bi, br, bi)
    return rr, ri


def _seg_rows(k):
    return pl.ds(pl.multiple_of(k * SCAN_LANES, SCAN_LANES), SCAN_LANES)


def _scan_fwd(xre, xim, carry, ar, ai, K):
    P = xre.shape[1]

    def local(k, c):
        pr, pi = c
        rows = _seg_rows(k)
        nr = ar * pr - ai * pi + xre[rows, :]
        ni = ar * pi + ai * pr + xim[rows, :]
        xre[rows, :] = nr
        xim[rows, :] = ni
        return nr, ni

    er, ei = lax.fori_loop(1, K, local, (xre[pl.ds(0, SCAN_LANES), :], xim[pl.ds(0, SCAN_LANES), :]))
    akr, aki = _cpow(ar, ai, K)
    cr = jnp.zeros((1, P), F32)
    ci = jnp.zeros((1, P), F32)
    carry[pl.ds(0, 1), :] = cr
    carry[pl.ds(SCAN_LANES, 1), :] = ci
    for j in range(1, SCAN_LANES):
        tr, ti = _cmul(akr, aki, cr, ci)
        cr = er[j - 1:j, :] + tr
        ci = ei[j - 1:j, :] + ti
        carry[pl.ds(j, 1), :] = cr
        carry[pl.ds(SCAN_LANES + j, 1), :] = ci
    cmr = carry[pl.ds(0, SCAN_LANES), :]
    cmi = carry[pl.ds(SCAN_LANES, SCAN_LANES), :]

    def fix(k, c):
        pr, pi = c
        rows = _seg_rows(k)
        tr, ti = _cmul(pr, pi, cmr, cmi)
        xre[rows, :] += tr
        xim[rows, :] += ti
        return _cmul(pr, pi, ar, ai)

    lax.fori_loop(0, K, fix, (jnp.broadcast_to(ar, (SCAN_LANES, P)), jnp.broadcast_to(ai, (SCAN_LANES, P))))


def s5_core_fwd(u, ar, ai, bre, bim, cre_t, cim_t, dskip, name):
    S, D = u.shape
    nblk, UB, PB = bre.shape
    K = S // SCAN_LANES

    def body(u_ref, ar_ref, ai_ref, bre_ref, bim_ref, cre_ref, cim_ref, d_ref, y_ref, xre, xim, carry):
        uv = u_ref[...]
        ub = uv.astype(BF16)
        xre[...] = _dot(ub, bre_ref[...])
        xim[...] = _dot(ub, bim_ref[...])
        _scan_fwd(xre, xim, carry, ar_ref[...], ai_ref[...], K)
        y_ref[...] = (_dot(xre[...].astype(BF16), cre_ref[...]) - _dot(xim[...].astype(BF16), cim_ref[...])
                      + d_ref[...] * uv)

    ucol = pl.BlockSpec((S, UB), lambda i: (0, i))
    pvec = pl.BlockSpec((None, 1, PB), lambda i: (i, 0, 0))
    bmat = pl.BlockSpec((None, UB, PB), lambda i: (i, 0, 0))
    cmat = pl.BlockSpec((None, PB, UB), lambda i: (i, 0, 0))
    return pl.pallas_call(
        body, name=name, grid=(nblk,),
        in_specs=[ucol, pvec, pvec, bmat, bmat, cmat, cmat, pl.BlockSpec((1, UB), lambda i: (0, i))],
        out_specs=ucol,
        out_shape=jax.ShapeDtypeStruct((S, D), F32),
        scratch_shapes=[pltpu.VMEM((S, PB), F32), pltpu.VMEM((S, PB), F32), pltpu.VMEM((2 * SCAN_LANES, PB), F32)],
        compiler_params=_cp(("arbitrary",)),
    )(u, ar, ai, bre, bim, cre_t, cim_t, dskip)


def s5_core_bwd(u, dy, ar, ai, bre, bim, cre, cim, dskip, name):
    S, D = u.shape
    nblk, UB, PB = bre.shape
    K = S // SCAN_LANES

    def body(u_ref, dy_ref, ar_ref, ai_ref, bre_ref, bim_ref, cre_ref, cim_ref, d_ref,
             du_ref, dbre_ref, dbim_ref, dcre_ref, dcim_ref, dar_ref, dai_ref, dd_ref,
             xre, xim, gre, gim, carry, carry_b):
        ar = ar_ref[...]
        ai = ai_ref[...]
        uv = u_ref[...]
        ub = uv.astype(BF16)
        dyv = dy_ref[...]
        dyb = dyv.astype(BF16)
        xre[...] = _dot(ub, bre_ref[...])
        xim[...] = _dot(ub, bim_ref[...])
        _scan_fwd(xre, xim, carry, ar, ai, K)
        dcre_ref[...] = _dot_tn(xre[...].astype(BF16), dyb)
        dcim_ref[...] = -_dot_tn(xim[...].astype(BF16), dyb)
        gre[...] = _dot(dyb, cre_ref[...])
        gim[...] = -_dot(dyb, cim_ref[...])

        def local(s, c):
            k = K - 2 - s
            nr, ni = c
            rows = _seg_rows(k)
            tr = gre[rows, :] + ar * nr + ai * ni
            ti = gim[rows, :] + ar * ni - ai * nr
            gre[rows, :] = tr
            gim[rows, :] = ti
            return tr, ti

        last = _seg_rows(K - 1)
        fr, fi = lax.fori_loop(0, K - 1, local, (gre[last, :], gim[last, :]))
        akr, aki = _cpow(ar, -ai, K)
        cr = jnp.zeros((1, PB), F32)
        ci = jnp.zeros((1, PB), F32)
        carry_b[pl.ds(SCAN_LANES - 1, 1), :] = cr
        carry_b[pl.ds(2 * SCAN_LANES - 1, 1), :] = ci
        for j in range(SCAN_LANES - 2, -1, -1):
            tr, ti = _cmul(akr, aki, cr, ci)
            cr = fr[j + 1:j + 2, :] + tr
            ci = fi[j + 1:j + 2, :] + ti
            carry_b[pl.ds(j, 1), :] = cr
            carry_b[pl.ds(SCAN_LANES + j, 1), :] = ci
        cbr = carry_b[pl.ds(0, SCAN_LANES), :]
        cbi = carry_b[pl.ds(SCAN_LANES, SCAN_LANES), :]

        def fix(s, c):
            k = K - 1 - s
            pr, pi, dar, dai = c
            rows = _seg_rows(k)
            tr, ti = _cmul(pr, pi, cbr, cbi)
            g_r = gre[rows, :] + tr
            g_i = gim[rows, :] + ti
            gre[rows, :] = g_r
            gim[rows, :] = g_i
            prev = _seg_rows(jnp.maximum(k - 1, 0))
            first = k == 0
            xr = jnp.where(first, carry[pl.ds(0, SCAN_LANES), :], xre[prev, :])
            xi = jnp.where(first, carry[pl.ds(SCAN_LANES, SCAN_LANES), :], xim[prev, :])
            dar = dar + g_r * xr + g_i * xi
            dai = dai + g_i * xr - g_r * xi
            nr, ni = _cmul(pr, pi, ar, -ai)
            return nr, ni, dar, dai

        z8 = jnp.zeros((SCAN_LANES, PB), F32)
        _, _, dar, dai = lax.fori_loop(
            0, K, fix, (jnp.broadcast_to(ar, (SCAN_LANES, PB)), jnp.broadcast_to(-ai, (SCAN_LANES, PB)), z8, z8))
        dar_ref[...] = _rowsum(dar)
        dai_ref[...] = _rowsum(dai)
        grb = gre[...].astype(BF16)
        gib = gim[...].astype(BF16)
        dbre_ref[...] = _dot_tn(ub, grb)
        dbim_ref[...] = _dot_tn(ub, gib)
        du_ref[...] = _dot_nt(grb, bre_ref[...]) + _dot_nt(gib, bim_ref[...]) + d_ref[...] * dyv
        dd_ref[...] = _rowsum(dyv * uv)

    ucol = pl.BlockSpec((S, UB), lambda i: (0, i))
    pvec = pl.BlockSpec((None, 1, PB), lambda i: (i, 0, 0))
    bmat = pl.BlockSpec((None, UB, PB), lambda i: (i, 0, 0))
    cmat = pl.BlockSpec((None, PB, UB), lambda i: (i, 0, 0))
    dvec = pl.BlockSpec((1, UB), lambda i: (0, i))
    return pl.pallas_call(
        body, name=name, grid=(nblk,),
        in_specs=[ucol, ucol, pvec, pvec, bmat, bmat, bmat, bmat, dvec],
        out_specs=[ucol, bmat, bmat, cmat, cmat, pvec, pvec, dvec],
        out_shape=[jax.ShapeDtypeStruct((S, D), F32),
                   jax.ShapeDtypeStruct((nblk, UB, PB), F32), jax.ShapeDtypeStruct((nblk, UB, PB), F32),
                   jax.ShapeDtypeStruct((nblk, PB, UB), F32), jax.ShapeDtypeStruct((nblk, PB, UB), F32),
                   jax.ShapeDtypeStruct((nblk, 1, PB), F32), jax.ShapeDtypeStruct((nblk, 1, PB), F32),
                   jax.ShapeDtypeStruct((1, D), F32)],
        scratch_shapes=[pltpu.VMEM((S, PB), F32), pltpu.VMEM((S, PB), F32), pltpu.VMEM((S, PB), F32),
                        pltpu.VMEM((S, PB), F32), pltpu.VMEM((2 * SCAN_LANES, PB), F32),
                        pltpu.VMEM((2 * SCAN_LANES, PB), F32)],
        compiler_params=_cp(("arbitrary",)),
    )(u, dy, ar, ai, bre, bim, cre, cim, dskip)


def s5_glu_fwd(y, wglu, x, vec, name, tm=256):
    S, D = x.shape
    NJ = wglu.shape[-1]

    def body(y_ref, w_ref, x_ref, vec_ref, out_ref, f_ref):
        g = _gelu(y_ref[...]).astype(BF16)
        f = jnp.concatenate([_dot(g, w_ref[j]) * _sigmoid(_dot(g, w_ref[2 + j])) for j in range(2)], axis=1)
        f_ref[...] = f
        out_ref[...] = _post_fwd(x_ref[...], f, vec_ref, 1.0)

    row = pl.BlockSpec((tm, D), lambda i: (i, 0))
    return pl.pallas_call(
        body, name=name, grid=(S // tm,),
        in_specs=[row, pl.BlockSpec((N_CHIPS, D, NJ), lambda i: (0, 0, 0)), row, pl.BlockSpec((8, D), lambda i: (0, 0))],
        out_specs=[row, row],
        out_shape=[jax.ShapeDtypeStruct((S, D), F32), jax.ShapeDtypeStruct((S, D), F32)],
        compiler_params=_cp(("arbitrary",)),
    )(y, wglu, x, vec)


def s5_glu_bwd(dout, f, y, vec, wglu, name, tm=256):
    S, D = dout.shape
    NJ = wglu.shape[-1]

    def body(dout_ref, f_ref, y_ref, vec_ref, w_ref, dy_ref, dab_ref, g_ref, acc_ref):
        @pl.when(pl.program_id(0) == 0)
        def _():
            acc_ref[...] = jnp.zeros_like(acc_ref)

        df, dgate, dgpost = _post_bwd(dout_ref[...], f_ref[...], vec_ref, 1.0)
        yv = y_ref[...]
        g = _gelu(yv).astype(BF16)
        g_ref[...] = g
        dg = jnp.zeros((tm, D), F32)
        for j in range(2):
            a = _dot(g, w_ref[j])
            sig = _sigmoid(_dot(g, w_ref[2 + j]))
            dfj = df[:, j * NJ:(j + 1) * NJ]
            da = (dfj * sig).astype(BF16)
            db = (dfj * a * sig * (1.0 - sig)).astype(BF16)
            dab_ref[:, j * NJ:(j + 1) * NJ] = da
            dab_ref[:, (2 + j) * NJ:(3 + j) * NJ] = db
            dg = dg + _dot_nt(da, w_ref[j]) + _dot_nt(db, w_ref[2 + j])
        dy_ref[...] = dg * _gelu_grad(yv)
        _acc_add(acc_ref, A_GATE, dgate)
        _acc_add(acc_ref, A_GPOST, dgpost)

    row = pl.BlockSpec((tm, D), lambda i: (i, 0))
    const = pl.BlockSpec((8, D), lambda i: (0, 0))
    return pl.pallas_call(
        body, name=name, grid=(S // tm,),
        in_specs=[row, row, row, const, pl.BlockSpec((N_CHIPS, D, NJ), lambda i: (0, 0, 0))],
        out_specs=[row, pl.BlockSpec((tm, N_CHIPS * NJ), lambda i: (i, 0)), row, const],
        out_shape=[jax.ShapeDtypeStruct((S, D), F32), jax.ShapeDtypeStruct((S, N_CHIPS * NJ), BF16),
                   jax.ShapeDtypeStruct((S, D), BF16), jax.ShapeDtypeStruct((8, D), F32)],
        compiler_params=_cp(("arbitrary",)),
    )(dout, f, y, vec, wglu)


def loss_head(y, target, name, tm=256):
    S, D = y.shape

    def body(y_ref, t_ref, d_ref, l_ref):
        @pl.when(pl.program_id(0) == 0)
        def _():
            l_ref[...] = jnp.zeros_like(l_ref)

        err = y_ref[...] - t_ref[...]
        d_ref[...] = err * (1.0 / D)
        l_ref[...] += jnp.sum(_rowsum(err * err), axis=1, keepdims=True)

    row = pl.BlockSpec((tm, D), lambda i: (i, 0))
    return pl.pallas_call(
        body, name=name, grid=(S // tm,),
        in_specs=[row, row],
        out_specs=[row, pl.BlockSpec((1, 1), lambda i: (0, 0))],
        out_shape=[jax.ShapeDtypeStruct((S, D), F32), jax.ShapeDtypeStruct((1, 1), F32)],
        compiler_params=_cp(("arbitrary",)),
    )(y, target)


def ada_mod(c_all, ada_w, ada_b_shard, name, tn=768):
    B, D = c_all.shape
    L, _, NS = ada_w.shape

    def body(c_ref, w_ref, b_ref, o_ref):
        cv = c_ref[...]
        cond = (cv * _sigmoid(cv)).astype(BF16)
        o_ref[...] = _dot(cond, w_ref[...].astype(BF16)) + b_ref[...]

    return pl.pallas_call(
        body, name=name, grid=(L, NS // tn),
        in_specs=[pl.BlockSpec((B, D), lambda l, j: (0, 0)), pl.BlockSpec((None, D, tn), lambda l, j: (l, 0, j)),
                  pl.BlockSpec((None, 1, tn), lambda l, j: (l, 0, j))],
        out_specs=pl.BlockSpec((None, B, tn), lambda l, j: (l, 0, j)),
        out_shape=jax.ShapeDtypeStruct((L, B, NS), F32),
        compiler_params=_cp(("arbitrary", "arbitrary")),
    )(c_all, ada_w, ada_b_shard)


def ada_grad(c_all, dmod, name, tn=768):
    B, D = c_all.shape
    L, _, NS = dmod.shape

    def body(c_ref, d_ref, o_ref):
        cv = c_ref[...]
        cond = (cv * _sigmoid(cv)).astype(BF16)
        o_ref[...] = _dot_tn(cond, d_ref[...].astype(BF16))

    return pl.pallas_call(
        body, name=name, grid=(L, NS // tn),
        in_specs=[pl.BlockSpec((B, D), lambda l, j: (0, 0)), pl.BlockSpec((None, B, tn), lambda l, j: (l, 0, j))],
        out_specs=pl.BlockSpec((None, D, tn), lambda l, j: (l, 0, j)),
        out_shape=jax.ShapeDtypeStruct((L, D, NS), F32),
        compiler_params=_cp(("arbitrary", "arbitrary")),
    )(c_all, dmod)


def sum_leading(x, name):
    n, R, C = x.shape

    def body(x_ref, o_ref):
        acc = x_ref[0].astype(F32)
        for i in range(1, n):
            acc = acc + x_ref[i].astype(F32)
        o_ref[...] = acc

    tr = _row_tile(R, 64)
    return pl.pallas_call(
        body, name=name, grid=(R // tr,),
        in_specs=[pl.BlockSpec((n, tr, C), lambda i: (0, i, 0))],
        out_specs=pl.BlockSpec((tr, C), lambda i: (i, 0)),
        out_shape=jax.ShapeDtypeStruct((R, C), F32),
        compiler_params=_cp(("arbitrary",)),
    )(x)


def _row_tile(R, cap=512):
    if R <= cap:
        return R
    for cand in (512, 384, 352, 256, 128, 64, 32, 16, 8):
        if cand <= cap and R % cand == 0:
            return cand
    return R


def adamw(w, g, m, v, name):
    R, C = w.shape
    tr = _row_tile(R, 256 if C > 1024 else 512)
    bc1 = 1.0 - ADAM_B1 ** ADAM_STEP
    bc2 = 1.0 - ADAM_B2 ** ADAM_STEP

    def body(w_ref, g_ref, m_ref, v_ref, d_ref, nm_ref, nv_ref):
        gv = g_ref[...]
        nm = ADAM_B1 * m_ref[...] + (1.0 - ADAM_B1) * gv
        nv = ADAM_B2 * v_ref[...] + (1.0 - ADAM_B2) * (gv * gv)
        nm_ref[...] = nm
        nv_ref[...] = nv
        d_ref[...] = -ADAM_LR * ((nm / bc1) / (jnp.sqrt(nv / bc2) + ADAM_EPS) + ADAM_WD * w_ref[...])

    blk = pl.BlockSpec((tr, C), lambda i: (i, 0))
    sd = jax.ShapeDtypeStruct((R, C), F32)
    return pl.pallas_call(
        body, name=name, grid=(R // tr,),
        in_specs=[blk, blk, blk, blk], out_specs=[blk, blk, blk], out_shape=[sd, sd, sd],
        compiler_params=_cp(("arbitrary",)),
    )(w, g, m, v)


def _as2d(a):
    if a.ndim == 1:
        return a.reshape(1, -1)
    return a.reshape(-1, a.shape[-1])


def adamw_nd(w, g, m, v, name):
    outs = adamw(_as2d(w), _as2d(g.reshape(w.shape)), _as2d(m), _as2d(v), name)
    return tuple(o.reshape(w.shape) for o in outs)


def _place():
    x, y, c = lax.axis_index("x"), lax.axis_index("y"), lax.axis_index("c")
    chips = [(1 - x, y), (x, 1 - y), (1 - x, 1 - y)]
    return x, y, c, chips


def allgather_small(xs, name):
    m_per, n = xs.shape

    def body(x_ref, out_ref, send_sems, recv_sems, local_sem):
        x, y, c, chips = _place()
        me, sibling = (x, y, c), (x, y, 1 - c)

        def rows(px, py, pc):
            return out_ref.at[pl.ds((4 * px + 2 * py + pc) * m_per, m_per), :]

        def copy(k, block, to, src=None):
            return pltpu.make_async_remote_copy(
                src_ref=rows(*block) if src is None else src, dst_ref=rows(*block),
                send_sem=send_sems.at[k], recv_sem=recv_sems.at[k], device_id=to, device_id_type=MESH_T)

        mine = pltpu.make_async_copy(x_ref, rows(*me), local_sem)
        mine.start()
        first = [copy(0, me, sibling, src=x_ref)]
        first += [copy(1 + j, me, (*chip, c), src=x_ref) for j, chip in enumerate(chips)]
        for cp in first:
            cp.start()
        passed = [copy(4 + j, (*chip, c), sibling) for j, chip in enumerate(chips)]
        for j, chip in enumerate(chips):
            copy(1 + j, (*chip, c), me).wait_recv()
            passed[j].start()
        copy(0, sibling, me).wait_recv()
        for j, chip in enumerate(chips):
            copy(4 + j, (*chip, 1 - c), me).wait_recv()
        for cp in first + passed:
            cp.wait_send()
        mine.wait()

    return pl.pallas_call(
        body, name=name,
        out_shape=jax.ShapeDtypeStruct((N_DEV * m_per, n), xs.dtype),
        in_specs=[pl.BlockSpec(memory_space=pltpu.VMEM)],
        out_specs=pl.BlockSpec(memory_space=pltpu.VMEM),
        scratch_shapes=[pltpu.SemaphoreType.DMA((7,)), pltpu.SemaphoreType.DMA((7,)), pltpu.SemaphoreType.DMA],
        compiler_params=_cp(),
    )(xs)


def gather_weights(shards, name):
    n = len(shards)

    def body(*refs):
        ins, outs = refs[:n], refs[n:2 * n]
        send_sems, recv_sems, fsend_sems, frecv_sems, local_sems = refs[2 * n:]
        x, y, c, chips = _place()
        me_chip = 2 * x + y
        sibling = (x, y, 1 - c)
        locs = []
        for a in range(n):
            cp = pltpu.make_async_copy(ins[a], outs[a].at[me_chip], local_sems.at[a])
            cp.start()
            locs.append(cp)

        def ici(a, k, src_chip, to):
            return pltpu.make_async_remote_copy(
                src_ref=ins[a].at[c], dst_ref=outs[a].at[src_chip, c],
                send_sem=send_sems.at[a * 3 + k], recv_sem=recv_sems.at[a * 3 + k], device_id=to, device_id_type=MESH_T)

        def d2d(a, k, src_chip, half):
            return pltpu.make_async_remote_copy(
                src_ref=outs[a].at[src_chip, half], dst_ref=outs[a].at[src_chip, half],
                send_sem=fsend_sems.at[a * 3 + k], recv_sem=frecv_sems.at[a * 3 + k],
                device_id=sibling, device_id_type=MESH_T)

        firsts = []
        for k, (cx, cy) in enumerate(chips):
            for a in range(n):
                cp = ici(a, k, me_chip, (cx, cy, c))
                cp.start()
                firsts.append(cp)
        passed = []
        for k, (cx, cy) in enumerate(chips):
            for a in range(n):
                ici(a, k, 2 * cx + cy, (cx, cy, c)).wait_recv()
                cp = d2d(a, k, 2 * cx + cy, c)
                cp.start()
                passed.append(cp)
        for k, (cx, cy) in enumerate(chips):
            for a in range(n):
                d2d(a, k, 2 * cx + cy, 1 - c).wait_recv()
        for cp in firsts + passed:
            cp.wait_send()
        for cp in locs:
            cp.wait()

    return pl.pallas_call(
        body, name=name,
        out_shape=[jax.ShapeDtypeStruct((N_CHIPS,) + s.shape, s.dtype) for s in shards],
        in_specs=[ANY] * n, out_specs=[ANY] * n,
        scratch_shapes=[pltpu.SemaphoreType.DMA((3 * n,)), pltpu.SemaphoreType.DMA((3 * n,)),
                        pltpu.SemaphoreType.DMA((3 * n,)), pltpu.SemaphoreType.DMA((3 * n,)),
                        pltpu.SemaphoreType.DMA((n,))],
        compiler_params=_cp(),
    )(*shards)


def _half_rows(ref, half, rh):
    idx = (slice(None),) * (len(ref.shape) - 2) + (pl.ds(pl.multiple_of(half * rh, 16), rh), slice(None))
    return ref.at[idx]


def sibling_exchange_halves(grads, name):
    n = len(grads)

    def body(*refs):
        ins, outs = refs[:n], refs[n:2 * n]
        send_sems, recv_sems = refs[2 * n:]
        x, y, c, _ = _place()
        cps = []
        for a in range(n):
            rh = ins[a].shape[-2] // 2
            cp = pltpu.make_async_remote_copy(
                src_ref=_half_rows(ins[a], 1 - c, rh), dst_ref=outs[a],
                send_sem=send_sems.at[a], recv_sem=recv_sems.at[a], device_id=(x, y, 1 - c), device_id_type=MESH_T)
            cp.start()
            cps.append(cp)
        for cp in cps:
            cp.wait()

    return pl.pallas_call(
        body, name=name,
        out_shape=[jax.ShapeDtypeStruct(g.shape[:-2] + (g.shape[-2] // 2, g.shape[-1]), g.dtype) for g in grads],
        in_specs=[ANY] * n, out_specs=[ANY] * n,
        scratch_shapes=[pltpu.SemaphoreType.DMA((n,)), pltpu.SemaphoreType.DMA((n,))],
        compiler_params=_cp(),
    )(*grads)


def pair_sum(g, recv, cidx, name):
    L, NS, R, C = g.shape
    rh = R // 2
    tr = _row_tile(rh, 256 if C > 1024 else 512)
    nt = rh // tr

    def body(c_ref, g_ref, r_ref, o_ref):
        o_ref[...] = (g_ref[...].astype(F32) + r_ref[...].astype(F32)).astype(BF16)

    grid_spec = pltpu.PrefetchScalarGridSpec(
        num_scalar_prefetch=1, grid=(L, NS, nt),
        in_specs=[pl.BlockSpec((None, None, tr, C), lambda l, s, t, c: (l, s, c[0] * nt + t, 0)),
                  pl.BlockSpec((None, None, tr, C), lambda l, s, t, c: (l, s, t, 0))],
        out_specs=pl.BlockSpec((None, None, tr, C), lambda l, s, t, c: (l, s, t, 0)))
    return pl.pallas_call(
        body, name=name, grid_spec=grid_spec,
        out_shape=jax.ShapeDtypeStruct((L, NS, rh, C), BF16),
        compiler_params=_cp(("arbitrary", "arbitrary", "arbitrary")),
    )(cidx, g, recv)


def chip_exchange(parts, name):
    n = len(parts)

    def body(*refs):
        ins, outs = refs[:n], refs[n:2 * n]
        send_sems, recv_sems = refs[2 * n:]
        x, y, c, chips = _place()
        cps = []
        for k, (cx, cy) in enumerate(chips):
            for a in range(n):
                cp = pltpu.make_async_remote_copy(
                    src_ref=ins[a].at[:, 2 * cx + cy], dst_ref=outs[a].at[k],
                    send_sem=send_sems.at[a * 3 + k], recv_sem=recv_sems.at[a * 3 + k],
                    device_id=(cx, cy, c), device_id_type=MESH_T)
                cp.start()
                cps.append(cp)
        for cp in cps:
            cp.wait()

    return pl.pallas_call(
        body, name=name,
        out_shape=[jax.ShapeDtypeStruct((3, p.shape[0]) + p.shape[2:], p.dtype) for p in parts],
        in_specs=[ANY] * n, out_specs=[ANY] * n,
        scratch_shapes=[pltpu.SemaphoreType.DMA((3 * n,)), pltpu.SemaphoreType.DMA((3 * n,))],
        compiler_params=_cp(),
    )(*parts)


def chip_sum(part, recv, chip_c, name, dest=None, slot=0, n_slots=1):
    _, NS, RH, C = part.shape
    tr = _row_tile(RH, 256 if C > 1024 else 512)
    nt = RH // tr

    def body(cc_ref, p_ref, r_ref, *rest):
        o_ref = rest[-1]
        acc = p_ref[...].astype(F32)
        for k in range(3):
            acc = acc + r_ref[k].astype(F32)
        o_ref[...] = acc

    in_specs = [pl.BlockSpec((None, None, tr, C), lambda t, cc: (0, cc[0], t, 0)),
                pl.BlockSpec((3, None, tr, C), lambda t, cc: (0, 0, t, 0))]
    args = [chip_c, part, recv]
    aliases = {}
    if dest is not None:
        in_specs.append(ANY)
        args.append(dest)
        aliases = {3: 0}
    grid_spec = pltpu.PrefetchScalarGridSpec(
        num_scalar_prefetch=1, grid=(nt,), in_specs=in_specs,
        out_specs=pl.BlockSpec((None, tr, C), lambda t, cc: (slot, cc[1] * nt + t, 0)))
    return pl.pallas_call(
        body, name=name, grid_spec=grid_spec,
        out_shape=jax.ShapeDtypeStruct((n_slots, 2 * RH, C), F32),
        input_output_aliases=aliases,
        compiler_params=_cp(("arbitrary",)),
    )(*args)


def sibling_fill_halves(bufs, name):
    n = len(bufs)

    def body(*refs):
        outs = refs[n:2 * n]
        send_sems, recv_sems = refs[2 * n:]
        x, y, c, _ = _place()
        cps = []
        for a in range(n):
            rh = outs[a].shape[-2] // 2
            cp = pltpu.make_async_remote_copy(
                src_ref=_half_rows(outs[a], c, rh), dst_ref=_half_rows(outs[a], c, rh),
                send_sem=send_sems.at[a], recv_sem=recv_sems.at[a], device_id=(x, y, 1 - c), device_id_type=MESH_T)
            cp.start()
            cps.append(cp)
        for a, cp in enumerate(cps):
            cp.wait_send()
        for a in range(n):
            rh = outs[a].shape[-2] // 2
            pltpu.make_async_remote_copy(
                src_ref=_half_rows(outs[a], 1 - c, rh), dst_ref=_half_rows(outs[a], 1 - c, rh),
                send_sem=send_sems.at[a], recv_sem=recv_sems.at[a], device_id=(x, y, 1 - c),
                device_id_type=MESH_T).wait_recv()

    return pl.pallas_call(
        body, name=name,
        out_shape=[jax.ShapeDtypeStruct(b.shape, b.dtype) for b in bufs],
        in_specs=[ANY] * n, out_specs=[ANY] * n,
        input_output_aliases={a: a for a in range(n)},
        scratch_shapes=[pltpu.SemaphoreType.DMA((n,)), pltpu.SemaphoreType.DMA((n,))],
        compiler_params=_cp(),
    )(*bufs)


HBM_SPEC = pl.BlockSpec(memory_space=pltpu.HBM)
SEM_SPEC = pl.BlockSpec(memory_space=pltpu.SEMAPHORE)
DATAFLOW_EFFECT = pltpu.SideEffectType.DATAFLOW_SIDE_EFFECTING


def split_start(srcs, lands, copies_fn, n_copies, name):
    ns, nl = len(srcs), len(lands)

    def body(*refs):
        src_refs, land_refs = refs[:ns], refs[ns:ns + nl]
        send_sems, recv_sems = refs[ns + nl], refs[ns + nl + 1]
        token = refs[-1]
        for cp in copies_fn(src_refs, land_refs, send_sems, recv_sems):
            cp.start()
        token[...] = jnp.zeros_like(token)

    bufs = list(srcs) + list(lands)
    outs = pl.pallas_call(
        body, name=name,
        out_shape=(pltpu.SemaphoreType.DMA((n_copies,)), pltpu.SemaphoreType.DMA((n_copies,)),
                   *[pltpu.HBM(b.shape, b.dtype) for b in bufs], jax.ShapeDtypeStruct((8, 128), F32)),
        in_specs=[HBM_SPEC] * (ns + nl),
        out_specs=(SEM_SPEC, SEM_SPEC, *[HBM_SPEC] * (ns + nl), pl.BlockSpec(memory_space=pltpu.VMEM)),
        input_output_aliases={i: 2 + i for i in range(ns + nl)},
        compiler_params=pltpu.CompilerParams(has_side_effects=DATAFLOW_EFFECT),
    )(*[pltpu.with_memory_space_constraint(b, pltpu.HBM) for b in bufs])
    return outs[0], outs[1], list(outs[2:2 + ns]), list(outs[2 + ns:2 + ns + nl]), outs[-1]


def split_wait(send_sems, recv_sems, srcs, lands, after, copies_fn, name):
    ns, nl = len(srcs), len(lands)

    def body(*refs):
        src_refs, land_refs = refs[:ns], refs[ns:ns + nl]
        send_s, recv_s = refs[ns + nl], refs[ns + nl + 1]
        for cp in copies_fn(src_refs, land_refs, send_s, recv_s):
            cp.wait_send()
            cp.wait_recv()

    bufs = list(srcs) + list(lands)
    outs = pl.pallas_call(
        body, name=name,
        out_shape=tuple(pltpu.HBM(b.shape, b.dtype) for b in bufs),
        in_specs=[HBM_SPEC] * (ns + nl) + [SEM_SPEC, SEM_SPEC, ANY],
        out_specs=tuple([HBM_SPEC] * (ns + nl)),
        input_output_aliases={i: i for i in range(ns + nl)},
        compiler_params=pltpu.CompilerParams(has_side_effects=DATAFLOW_EFFECT),
    )(*bufs, send_sems, recv_sems, after)
    return list(outs[:ns]), list(outs[ns:])


def chip_exchange_copies(src_refs, land_refs, send_sems, recv_sems):
    x, y, c, chips = _place()
    cps = []
    for a in range(len(src_refs)):
        for k, (cx, cy) in enumerate(chips):
            cps.append(pltpu.make_async_remote_copy(
                src_ref=src_refs[a].at[:, 2 * cx + cy], dst_ref=land_refs[a].at[k],
                send_sem=send_sems.at[a * 3 + k], recv_sem=recv_sems.at[a * 3 + k],
                device_id=(cx, cy, c), device_id_type=MESH_T))
    return cps


def _s5_prepare(lam_re, lam_im, log_dt, b_re, b_im, c_re, c_im, groups_per_block):
    G, P = lam_re.shape
    N = b_re.shape[-1]
    gb = groups_per_block
    nblk = G // gb
    dt = jnp.exp(log_dt)[:, None]
    e = jnp.exp(lam_re * dt)
    a_re = e * jnp.cos(lam_im * dt)
    a_im = e * jnp.sin(lam_im * dt)
    n2 = lam_re * lam_re + lam_im * lam_im
    co_re = ((a_re - 1.0) * lam_re + a_im * lam_im) / n2
    co_im = (a_im * lam_re - (a_re - 1.0) * lam_im) / n2
    bb_re = co_re[..., None] * b_re - co_im[..., None] * b_im
    bb_im = co_re[..., None] * b_im + co_im[..., None] * b_re
    eye = jnp.eye(gb, dtype=F32)

    def blockdiag_np(m):
        m = m.reshape(nblk, gb, N, P)
        return jnp.einsum('bgnp,gh->bgnhp', m, eye).reshape(nblk, gb * N, gb * P)

    b_np_re = jnp.swapaxes(bb_re, 1, 2)
    b_np_im = jnp.swapaxes(bb_im, 1, 2)
    return (a_re.reshape(nblk, 1, gb * P), a_im.reshape(nblk, 1, gb * P),
            blockdiag_np(b_np_re), blockdiag_np(b_np_im), blockdiag_np(c_re), blockdiag_np(c_im))


def _to_scan_order(a):
    S, D = a.shape
    return a.reshape(SCAN_LANES, S // SCAN_LANES, D).transpose(1, 0, 2).reshape(S, D)


def _from_scan_order(a):
    S, D = a.shape
    return a.reshape(S // SCAN_LANES, SCAN_LANES, D).transpose(1, 0, 2).reshape(S, D)


def _pad_rows(a, mult=8):
    r = (-a.shape[0]) % mult
    if r:
        a = jnp.concatenate([a, jnp.zeros((r, a.shape[1]), a.dtype)], axis=0)
    return a


def _pack_rows(arrs, width):
    parts, offs, o = [], [], 0
    for a in arrs:
        flat = a.reshape(-1)
        r = (-flat.shape[0]) % (16 * width)
        if r:
            flat = jnp.concatenate([flat, jnp.zeros((r,), flat.dtype)])
        p = flat.reshape(-1, width)
        parts.append(p)
        offs.append((o, a.shape, a.size))
        o += p.shape[0]
    if o % 64:
        parts.append(jnp.zeros((64 - o % 64, width), parts[0].dtype))
    return jnp.concatenate(parts, axis=0), offs


def _unpack_rows(packed, offs):
    outs = []
    for o, shape, size in offs:
        rows = -(-size // packed.shape[1])
        outs.append(packed[o:o + rows].reshape(-1)[:size].reshape(shape))
    return outs


def kernel(x, c, ada_w, ada_b, norm_pre, norm_post, ffn_w_in, ffn_w_out, ab_w_in, pool_w, pool_scale, sgu_ln_g, sgu_ln_b, sgu_w, sgu_b, ab_w_out, ssm_w_in, ssm_lam_re, ssm_lam_im, ssm_b_re, ssm_b_im, ssm_c_re, ssm_c_im, ssm_d, ssm_log_dt, ssm_w_glu, loss_target, m_ada_w, m_ada_b, m_norm_pre, m_norm_post, m_ffn_w_in, m_ffn_w_out, m_ab_w_in, m_pool_w, m_pool_scale, m_sgu_ln_g, m_sgu_ln_b, m_sgu_w, m_sgu_b, m_ab_w_out, m_ssm_w_in, m_ssm_lam_re, m_ssm_lam_im, m_ssm_b_re, m_ssm_b_im, m_ssm_c_re, m_ssm_c_im, m_ssm_d, m_ssm_log_dt, m_ssm_w_glu, v_ada_w, v_ada_b, v_norm_pre, v_norm_post, v_ffn_w_in, v_ffn_w_out, v_ab_w_in, v_pool_w, v_pool_scale, v_sgu_ln_g, v_sgu_ln_b, v_sgu_w, v_sgu_b, v_ab_w_out, v_ssm_w_in, v_ssm_lam_re, v_ssm_lam_im, v_ssm_b_re, v_ssm_b_im, v_ssm_c_re, v_ssm_c_im, v_ssm_d, v_ssm_log_dt, v_ssm_w_glu):
    weights = dict(ada_w=ada_w, ada_b=ada_b, norm_pre=norm_pre, norm_post=norm_post, ffn_w_in=ffn_w_in,
                   ffn_w_out=ffn_w_out, ab_w_in=ab_w_in, pool_w=pool_w, pool_scale=pool_scale, sgu_ln_g=sgu_ln_g,
                   sgu_ln_b=sgu_ln_b, sgu_w=sgu_w, sgu_b=sgu_b, ab_w_out=ab_w_out, ssm_w_in=ssm_w_in,
                   ssm_lam_re=ssm_lam_re, ssm_lam_im=ssm_lam_im, ssm_b_re=ssm_b_re, ssm_b_im=ssm_b_im,
                   ssm_c_re=ssm_c_re, ssm_c_im=ssm_c_im, ssm_d=ssm_d, ssm_log_dt=ssm_log_dt, ssm_w_glu=ssm_w_glu)
    m_in = dict(ada_w=m_ada_w, ada_b=m_ada_b, norm_pre=m_norm_pre, norm_post=m_norm_post, ffn_w_in=m_ffn_w_in,
                ffn_w_out=m_ffn_w_out, ab_w_in=m_ab_w_in, pool_w=m_pool_w, pool_scale=m_pool_scale,
                sgu_ln_g=m_sgu_ln_g, sgu_ln_b=m_sgu_ln_b, sgu_w=m_sgu_w, sgu_b=m_sgu_b, ab_w_out=m_ab_w_out,
                ssm_w_in=m_ssm_w_in, ssm_lam_re=m_ssm_lam_re, ssm_lam_im=m_ssm_lam_im, ssm_b_re=m_ssm_b_re,
                ssm_b_im=m_ssm_b_im, ssm_c_re=m_ssm_c_re, ssm_c_im=m_ssm_c_im, ssm_d=m_ssm_d,
                ssm_log_dt=m_ssm_log_dt, ssm_w_glu=m_ssm_w_glu)
    v_in = dict(ada_w=v_ada_w, ada_b=v_ada_b, norm_pre=v_norm_pre, norm_post=v_norm_post, ffn_w_in=v_ffn_w_in,
                ffn_w_out=v_ffn_w_out, ab_w_in=v_ab_w_in, pool_w=v_pool_w, pool_scale=v_pool_scale,
                sgu_ln_g=v_sgu_ln_g, sgu_ln_b=v_sgu_ln_b, sgu_w=v_sgu_w, sgu_b=v_sgu_b, ab_w_out=v_ab_w_out,
                ssm_w_in=v_ssm_w_in, ssm_lam_re=v_ssm_lam_re, ssm_lam_im=v_ssm_lam_im, ssm_b_re=v_ssm_b_re,
                ssm_b_im=v_ssm_b_im, ssm_c_re=v_ssm_c_re, ssm_c_im=v_ssm_c_im, ssm_d=v_ssm_d,
                ssm_log_dt=v_ssm_log_dt, ssm_w_glu=v_ssm_w_glu)
    names = list(weights.keys())

    xi, yi, ci = lax.axis_index("x"), lax.axis_index("y"), lax.axis_index("c")
    chip = 2 * xi + yi
    me = 4 * xi + 2 * yi + ci
    S, D = x.shape[1], x.shape[2]
    L = ada_w.shape[0]
    NSUB = norm_pre.shape[1]
    DS = norm_pre.shape[2]
    FS = ffn_w_in.shape[-1]
    FR = ffn_w_out.shape[-2]
    x0 = x[0]
    target = loss_target[0]

    small_parts = [_pad_rows(p) for p in (c.reshape(D // DS, DS), norm_pre.reshape(L * NSUB, DS),
                                          norm_post.reshape(L * NSUB, DS), ssm_d.reshape(1, DS))]
    small_offs = [0]
    for p in small_parts:
        small_offs.append(small_offs[-1] + p.shape[0])
    small_all = allgather_small(jnp.concatenate(small_parts, axis=0), "ag_small").reshape(N_DEV, -1, DS)
    c_all = small_all[:, :D // DS].reshape(N_DEV, D)
    per_chip = small_all[0::2]
    o = small_offs[1]
    g_pre_full = jnp.moveaxis(per_chip[:, o:o + L * NSUB], 0, 1).reshape(L, NSUB, D)
    o = small_offs[2]
    g_post_full = jnp.moveaxis(per_chip[:, o:o + L * NSUB], 0, 1).reshape(L, NSUB, D)
    o = small_offs[3]
    d_full = jnp.moveaxis(per_chip[:, o:o + 1], 0, 1).reshape(1, D)

    NS = ada_w.shape[-1]
    ada_b_shard = lax.dynamic_slice_in_dim(ada_b, chip * NS, NS, axis=1).reshape(L, 1, NS)
    mod_part = ada_mod(c_all, ada_w, ada_b_shard, "ada_mod")
    mod_all = allgather_small(mod_part.reshape(L * N_DEV, NS), "ag_mod").reshape(N_DEV, L, N_DEV, NS)
    mod_mine = lax.dynamic_index_in_dim(mod_all[0::2], me, axis=2, keepdims=False)
    mod = jnp.moveaxis(mod_mine, 0, 1).reshape(L, NSUB, 3, D)

    def vec_of(l, s):
        return jnp.concatenate([mod[l, s], g_pre_full[l, s][None], g_post_full[l, s][None],
                                jnp.zeros((3, D), F32)], axis=0)

    def halves(a):
        a = a.astype(BF16)
        if a.shape[0] == 2:
            return a
        a = a[0]
        return a.reshape((2, a.shape[0] // 2) + a.shape[1:])

    gathered = gather_weights([halves(ffn_w_in), halves(ffn_w_out), halves(ab_w_in), halves(ab_w_out),
                               halves(ssm_w_in), halves(ssm_w_glu)], "gather_weights")
    win_g, wout_g = gathered[0], gathered[1]
    abin_g = gathered[2].reshape(N_CHIPS, D, -1)
    about_g = gathered[3].reshape(-1, D)
    sin_g = gathered[4].reshape(1, -1, D)
    glu_g = gathered[5].reshape(N_CHIPS, D, -1)

    GB = 8
    s5_args = (ssm_lam_re[0], ssm_lam_im[0], ssm_log_dt[0], ssm_b_re[0], ssm_b_im[0], ssm_c_re[0], ssm_c_im[0])
    (a_re, a_im, bblk_re, bblk_im, cblk_re, cblk_im), s5_vjp = jax.vjp(lambda *p: _s5_prepare(*p, GB), *s5_args)
    bre16, bim16 = bblk_re.astype(BF16), bblk_im.astype(BF16)
    cre16, cim16 = cblk_re.astype(BF16), cblk_im.astype(BF16)
    cre16_t, cim16_t = jnp.swapaxes(cre16, 1, 2), jnp.swapaxes(cim16, 1, 2)

    pool_w16 = pool_w[0].astype(BF16)
    sgu_bexp = jnp.broadcast_to(sgu_b[0][:, :, None], sgu_w[0].shape)

    saved = {}
    xcur = x0
    for l in range(L):
        v0 = vec_of(l, 0)
        out, h, u, f = ffn_fwd(xcur, v0, win_g, wout_g, l, 0, f"ffn_fwd_{l}_0")
        saved[(l, 0)] = (xcur, v0, h, u, f)
        xcur = out
        v1 = vec_of(l, 1)
        if l % 2 == 0:
            h, z = pre_matmul(xcur, v1, abin_g, f"mixa_in_{l}")
            ycat = mixa_core_fwd(z, pool_w16, pool_scale, sgu_ln_g, sgu_ln_b, sgu_w[0], sgu_bexp, f"mixa_core_{l}")
            out, f = matmul_post(ycat, about_g, xcur, v1, f"mixa_out_{l}")
            saved[(l, 1)] = (xcur, v1, h, z, ycat, f)
        else:
            h, uu = pre_matmul(xcur, v1, sin_g, f"s5_in_{l}")
            us = _to_scan_order(uu)
            ys = s5_core_fwd(us, a_re, a_im, bre16, bim16, cre16_t, cim16_t, d_full, f"s5_core_{l}")
            yy = _from_scan_order(ys)
            out, f = s5_glu_fwd(yy, glu_g, xcur, v1, f"s5_glu_{l}")
            saved[(l, 1)] = (xcur, v1, h, us, yy, f)
        xcur = out
        v2 = vec_of(l, 2)
        out, h, u, f = ffn_fwd(xcur, v2, win_g, wout_g, l, 1, f"ffn_fwd_{l}_1")
        saved[(l, 2)] = (xcur, v2, h, u, f)
        xcur = out

    dcur, sq = loss_head(xcur, target, "loss_head")
    loss = lax.psum(sq[0, 0], ("x", "y", "c")) * (0.5 / D)

    accs = {}
    small_g = {}
    cidx = ci.reshape(1).astype(jnp.int32)
    chip_c = jnp.stack([chip, ci]).astype(jnp.int32)
    rs_pending = []

    def rs_begin(tag, items):
        arrs = [it[0] for it in items]
        recv_a = sibling_exchange_halves(arrs, f"rs_pair_exchange_{tag}")
        parts = [pair_sum(a, r, cidx, f"rs_pair_sum_{tag}_{i}") for i, (a, r) in enumerate(zip(arrs, recv_a))]
        lands = [lax.empty((3, 1) + p.shape[2:], BF16) for p in parts]
        rs_pending.append((tag, items, split_start(parts, lands, chip_exchange_copies, 3 * len(parts),
                                                   f"rs_chip_start_{tag}")))

    def ffn_back(l, s, k, dcur):
        xin, vv, h, u, f = saved[(l, s)]
        dx, df, du, act, acc = ffn_bwd(dcur, xin, f, u, vv, win_g, wout_g, l, k, f"ffn_bwd_{l}_{k}")
        accs[(l, s)] = acc
        g_win = tn_matmul(h, du, 512, f"ffn_dwin_{l}_{k}")
        g_wout = tn_matmul_cols(act, df, FS, D, f"ffn_dwout_{l}_{k}")
        rs_begin(f"ffn_{l}_{k}", [(g_win[None], "ffn_w_in", 2 * l + k, 2 * L),
                                  (g_wout.reshape(1, N_CHIPS, FR, D), "ffn_w_out", 2 * l + k, 2 * L)])
        return dx

    for l in reversed(range(L)):
        dcur = ffn_back(l, 2, 1, dcur)
        if l % 2 == 0:
            xin, vv, h, z, ycat, f = saved[(l, 1)]
            df, dact, acc_post = post_bwd_matmul(dcur, f, vv, about_g, f"mixa_out_bwd_{l}")
            g_about = tn_matmul_cols(ycat, df, 512, D, f"mixa_dwout_{l}")
            dz, dpw, dvecw, dsw, dsb = mixa_core_bwd(z, dact, pool_w16, pool_scale, sgu_ln_g, sgu_ln_b, sgu_w[0],
                                                     sgu_bexp, f"mixa_core_bwd_{l}")
            g_abin = tn_matmul_cols(h, dz, 512, abin_g.shape[-1], f"mixa_dwin_{l}")
            dcur, acc_pre = matmul_pre_bwd(dz, abin_g, xin, dcur, vv, f"mixa_in_bwd_{l}")
            accs[(l, 1)] = acc_pre + acc_post
            rs_begin(f"mixa_{l}", [(g_abin[None], "ab_w_in", 0, 1),
                                   (g_about.reshape(1, N_CHIPS, -1, D), "ab_w_out", 0, 1)])
            small_g.update(pool_w=dpw[None], pool_scale=dvecw[0:1], sgu_ln_g=dvecw[1:2], sgu_ln_b=dvecw[2:3],
                           sgu_w=dsw[None], sgu_b=dsb.T[None])
        else:
            xin, vv, h, us, yy, f = saved[(l, 1)]
            dy, dab, gact, acc_post = s5_glu_bwd(dcur, f, yy, vv, glu_g, f"s5_glu_bwd_{l}")
            g_glu = tn_matmul_cols(gact, dab, 512, glu_g.shape[-1], f"s5_dwglu_{l}")
            dys = _to_scan_order(dy)
            dus, dbre, dbim, dcre_t, dcim_t, dar, dai, dd = s5_core_bwd(
                us, dys, a_re, a_im, bre16, bim16, cre16, cim16, d_full, f"s5_core_bwd_{l}")
            du = _from_scan_order(dus).astype(BF16)
            g_sin = tn_matmul_cols(h, du, 512, D, f"s5_dwin_{l}")
            dcur, acc_pre = matmul_pre_bwd(du, sin_g, xin, dcur, vv, f"s5_in_bwd_{l}")
            accs[(l, 1)] = acc_pre + acc_post
            rs_begin(f"s5_{l}", [(g_sin.reshape(1, N_CHIPS, -1, D), "ssm_w_in", 0, 1), (g_glu[None], "ssm_w_glu", 0, 1)])
            s5_grads = s5_vjp((dar, dai, dbre, dbim, jnp.swapaxes(dcre_t, 1, 2), jnp.swapaxes(dcim_t, 1, 2)))
            small_g.update(ssm_lam_re=s5_grads[0][None], ssm_lam_im=s5_grads[1][None], ssm_log_dt=s5_grads[2][None],
                           ssm_b_re=s5_grads[3][None], ssm_b_im=s5_grads[4][None], ssm_c_re=s5_grads[5][None],
                           ssm_c_im=s5_grads[6][None])
            dd_mine = dd
        dcur = ffn_back(l, 0, 0, dcur)
    grad_x = dcur[None]

    acc_all = jnp.stack([jnp.stack([accs[(l, s)] for s in range(NSUB)]) for l in range(L)])
    dmod_mine = acc_all[:, :, A_SHIFT:A_GATE + 1].reshape(L, NSUB * 3 * D)
    dgpre_mine = acc_all[:, :, A_GPRE]
    dgpost_mine = acc_all[:, :, A_GPOST]
    small_names = ["pool_w", "pool_scale", "sgu_ln_g", "sgu_ln_b", "sgu_w", "sgu_b", "ssm_lam_re", "ssm_lam_im",
                   "ssm_b_re", "ssm_b_im", "ssm_c_re", "ssm_c_im", "ssm_log_dt"]
    packed, offs = _pack_rows([dmod_mine, dgpre_mine, dgpost_mine, dd_mine] + [small_g[n] for n in small_names], D)
    rows = packed.shape[0]
    packed_all = allgather_small(packed.astype(BF16), "ag_grads").reshape(N_DEV, rows, D)
    summed = _unpack_rows(sum_leading(packed_all, "sum_small"), offs)
    grads = {n: g for n, g in zip(small_names, summed[4:])}
    grads["ada_b"] = summed[0]
    grads["norm_pre"] = lax.dynamic_slice_in_dim(summed[1], chip * DS, DS, axis=2)
    grads["norm_post"] = lax.dynamic_slice_in_dim(summed[2], chip * DS, DS, axis=2)
    grads["ssm_d"] = lax.dynamic_slice_in_dim(summed[3], chip * DS, DS, axis=1)
    dmod_all = jnp.moveaxis(packed_all[:, :L * NSUB * 3].reshape(N_DEV, L, NSUB * 3 * D), 0, 1)
    dmod_shard = lax.dynamic_slice_in_dim(dmod_all, chip * NS, NS, axis=2)
    grads["ada_w"] = ada_grad(c_all, dmod_shard, "ada_grad")

    fin = {}
    for tag, items, (ssem, rsem, srcs, lands, _) in rs_pending:
        parts, recv = split_wait(ssem, rsem, srcs, lands, dcur, chip_exchange_copies, f"rs_chip_wait_{tag}")
        for i, (it, p, r) in enumerate(zip(items, parts, recv)):
            fin[it[1]] = chip_sum(p, r, chip_c, f"rs_chip_sum_{tag}_{i}", dest=fin.get(it[1]), slot=it[2],
                                  n_slots=it[3])
    big_names = ["ffn_w_in", "ffn_w_out", "ab_w_in", "ab_w_out", "ssm_w_in", "ssm_w_glu"]
    filled = sibling_fill_halves([fin[n] for n in big_names], "rs_fill")
    for n, g in zip(big_names, filled):
        grads[n] = g

    deltas, new_m, new_v = {}, {}, {}
    for n in names:
        g = grads[n].reshape(weights[n].shape)
        grads[n] = g
        deltas[n], new_m[n], new_v[n] = adamw_nd(weights[n], g, m_in[n], v_in[n], f"adamw_{n}")

    return (loss, grad_x, *[grads[n] for n in names], *[deltas[n] for n in names],
            *[new_m[n] for n in names], *[new_v[n] for n in names])
```

```python
import math

import jax
import jax.numpy as jnp
from jax import lax
from jax.experimental import pallas as pl
from jax.experimental.pallas import tpu as pltpu

F32 = jnp.float32
BF16 = jnp.bfloat16
EPS = 1e-6
MESH_T = pl.DeviceIdType.MESH
VMEM_LIMIT_BYTES = 56 * 1024 * 1024
N_CHIPS = 4
N_DEV = 8
POOL_WINDOWS = (2, 4, 8, 16)
CHUNK = 128
SCAN_LANES = 8
ADAM_LR = 0.001
ADAM_B1 = 0.9
ADAM_B2 = 0.999
ADAM_EPS = 1e-08
ADAM_WD = 0.01
ADAM_STEP = 10
GELU_C = math.sqrt(2.0 / math.pi)
GELU_K = 0.044715

V_SHIFT, V_SCALE, V_GATE, V_GPRE, V_GPOST = 0, 1, 2, 3, 4
A_SHIFT, A_SCALE, A_GATE, A_GPRE, A_GPOST = 0, 1, 2, 3, 4

ANY = pl.BlockSpec(memory_space=pl.ANY)


def _cp(sem=None):
    if sem is None:
        return pltpu.CompilerParams(vmem_limit_bytes=VMEM_LIMIT_BYTES)
    return pltpu.CompilerParams(vmem_limit_bytes=VMEM_LIMIT_BYTES, dimension_semantics=sem)


def _dot(a, b):
    return jnp.dot(a, b, preferred_element_type=F32)


def _dot_nt(a, b):
    return lax.dot_general(a, b, (((1,), (1,)), ((), ())), preferred_element_type=F32)


def _dot_tn(a, b):
    return lax.dot_general(a, b, (((0,), (0,)), ((), ())), preferred_element_type=F32)


def _sigmoid(x):
    return 1.0 / (1.0 + jnp.exp(-x))


def _gelu(x):
    return 0.5 * x * (1.0 + jnp.tanh(GELU_C * (x + GELU_K * x * x * x)))


def _gelu_grad(x):
    t = jnp.tanh(GELU_C * (x + GELU_K * x * x * x))
    return 0.5 * (1.0 + t) + 0.5 * x * (1.0 - t * t) * GELU_C * (1.0 + 3.0 * GELU_K * x * x)


def _rowsum(v):
    return jnp.sum(v, axis=0, keepdims=True)


def _lanemean(v):
    return jnp.mean(v, axis=-1, keepdims=True)


def _row(ref, i):
    return ref[pl.ds(i, 1), :]


def _pre_fwd(x, vec_ref):
    r = lax.rsqrt(_lanemean(x * x) + EPS)
    return (x * r) * _row(vec_ref, V_GPRE) * (1.0 + _row(vec_ref, V_SCALE)) + _row(vec_ref, V_SHIFT)


def _pre_bwd(x, dh, vec_ref):
    g = _row(vec_ref, V_GPRE)
    sc = 1.0 + _row(vec_ref, V_SCALE)
    r = lax.rsqrt(_lanemean(x * x) + EPS)
    xn = x * r
    dhx = dh * xn
    t2 = dh * (g * sc)
    dx = r * (t2 - xn * _lanemean(t2 * xn))
    return dx, _rowsum(dh), _rowsum(dhx * g), _rowsum(dhx * sc)


def _post_fwd(x, f, vec_ref, rw):
    q = lax.rsqrt(_lanemean(f * f) + EPS)
    return x + (rw * _row(vec_ref, V_GATE)) * (f * q * _row(vec_ref, V_GPOST))


def _post_bwd(dout, f, vec_ref, rw):
    gp = _row(vec_ref, V_GPOST)
    gate = _row(vec_ref, V_GATE)
    q = lax.rsqrt(_lanemean(f * f) + EPS)
    fhat = f * q
    dgate = _rowsum(dout * (rw * fhat * gp))
    dy = dout * (rw * gate)
    dgpost = _rowsum(dy * fhat)
    t = dy * gp
    df = q * (t - fhat * _lanemean(t * fhat))
    return df, dgate, dgpost


def _acc_add(acc_ref, row, v):
    acc_ref[pl.ds(row, 1), :] += v


def ffn_fwd(x, vec, win_g, wout_g, name, tm=256):
    S, D = x.shape
    FS = win_g.shape[-1]
    FR = wout_g.shape[-2]

    def body(x_ref, vec_ref, win_hbm, wout_hbm, out_ref, h_ref, u_ref, f_ref, win_s, wout_s, sem):
        @pl.when(pl.program_id(0) == 0)
        def _():
            cps = [pltpu.make_async_copy(win_hbm.at[j], win_s.at[j], sem.at[j]) for j in range(N_CHIPS)]
            cps += [pltpu.make_async_copy(wout_hbm.at[j], wout_s.at[pl.ds(j * FR, FR), :], sem.at[N_CHIPS + j])
                    for j in range(N_CHIPS)]
            for cp in cps:
                cp.start()
            for cp in cps:
                cp.wait()

        xv = x_ref[...]
        h = _pre_fwd(xv, vec_ref).astype(BF16)
        h_ref[...] = h
        f = jnp.zeros((tm, D), F32)
        for j in range(2):
            a = _dot(h, win_s[j])
            b = _dot(h, win_s[2 + j])
            u_ref[j] = a.astype(BF16)
            u_ref[2 + j] = b.astype(BF16)
            act = (a * _sigmoid(a) * b).astype(BF16)
            f = f + _dot(act, wout_s[pl.ds(j * FS, FS), :])
        f_ref[...] = f
        out_ref[...] = _post_fwd(xv, f, vec_ref, 0.5)

    row = pl.BlockSpec((tm, D), lambda i: (i, 0))
    return pl.pallas_call(
        body, name=name, grid=(S // tm,),
        in_specs=[row, pl.BlockSpec((8, D), lambda i: (0, 0)), ANY, ANY],
        out_specs=[row, row, pl.BlockSpec((N_CHIPS, tm, FS), lambda i: (0, i, 0)), row],
        out_shape=[jax.ShapeDtypeStruct((S, D), F32), jax.ShapeDtypeStruct((S, D), BF16),
                   jax.ShapeDtypeStruct((N_CHIPS, S, FS), BF16), jax.ShapeDtypeStruct((S, D), F32)],
        scratch_shapes=[pltpu.VMEM((N_CHIPS, D, FS), BF16), pltpu.VMEM((N_CHIPS * FR, D), BF16),
                        pltpu.SemaphoreType.DMA((2 * N_CHIPS,))],
        compiler_params=_cp(("arbitrary",)),
    )(x, vec, win_g, wout_g)


def ffn_bwd(dout, x, f, u, vec, win_g, wout_g, name, tm=256):
    S, D = x.shape
    FS = win_g.shape[-1]
    FR = wout_g.shape[-2]

    def body(dout_ref, x_ref, f_ref, u_ref, vec_ref, win_hbm, wout_hbm,
             dx_ref, df_ref, du_ref, act_ref, acc_ref, win_s, wout_s, sem):
        @pl.when(pl.program_id(0) == 0)
        def _():
            cps = [pltpu.make_async_copy(win_hbm.at[j], win_s.at[j], sem.at[j]) for j in range(N_CHIPS)]
            cps += [pltpu.make_async_copy(wout_hbm.at[j], wout_s.at[pl.ds(j * FR, FR), :], sem.at[N_CHIPS + j])
                    for j in range(N_CHIPS)]
            for cp in cps:
                cp.start()
            acc_ref[...] = jnp.zeros_like(acc_ref)
            for cp in cps:
                cp.wait()

        dout_v = dout_ref[...]
        df, dgate, dgpost = _post_bwd(dout_v, f_ref[...], vec_ref, 0.5)
        dfb = df.astype(BF16)
        df_ref[...] = dfb
        dh = jnp.zeros((tm, D), F32)
        for j in range(2):
            a = u_ref[j].astype(F32)
            b = u_ref[2 + j].astype(F32)
            sig = _sigmoid(a)
            sl = a * sig
            dact = _dot_nt(dfb, wout_s[pl.ds(j * FS, FS), :])
            da = (dact * b * (sig * (1.0 + a * (1.0 - sig)))).astype(BF16)
            db = (dact * sl).astype(BF16)
            du_ref[j] = da
            du_ref[2 + j] = db
            act_ref[:, j * FS:(j + 1) * FS] = (sl * b).astype(BF16)
            dh = dh + _dot_nt(da, win_s[j]) + _dot_nt(db, win_s[2 + j])
        dx, dshift, dscale, dgpre = _pre_bwd(x_ref[...], dh, vec_ref)
        dx_ref[...] = dout_v + dx
        _acc_add(acc_ref, A_SHIFT, dshift)
        _acc_add(acc_ref, A_SCALE, dscale)
        _acc_add(acc_ref, A_GATE, dgate)
        _acc_add(acc_ref, A_GPRE, dgpre)
        _acc_add(acc_ref, A_GPOST, dgpost)

    row = pl.BlockSpec((tm, D), lambda i: (i, 0))
    ublk = pl.BlockSpec((N_CHIPS, tm, FS), lambda i: (0, i, 0))
    const = pl.BlockSpec((8, D), lambda i: (0, 0))
    return pl.pallas_call(
        body, name=name, grid=(S // tm,),
        in_specs=[row, row, row, ublk, const, ANY, ANY],
        out_specs=[row, row, ublk, pl.BlockSpec((tm, 2 * FS), lambda i: (i, 0)), const],
        out_shape=[jax.ShapeDtypeStruct((S, D), F32), jax.ShapeDtypeStruct((S, D), BF16),
                   jax.ShapeDtypeStruct((N_CHIPS, S, FS), BF16), jax.ShapeDtypeStruct((S, 2 * FS), BF16),
                   jax.ShapeDtypeStruct((8, D), F32)],
        scratch_shapes=[pltpu.VMEM((N_CHIPS, D, FS), BF16), pltpu.VMEM((N_CHIPS * FR, D), BF16),
                        pltpu.SemaphoreType.DMA((2 * N_CHIPS,))],
        compiler_params=_cp(("arbitrary",)),
    )(dout, x, f, u, vec, win_g, wout_g)


def tn_matmul(a, b, tk, name):
    S, K = a.shape
    nb, _, tn = b.shape

    def body(a_ref, b_ref, o_ref):
        o_ref[...] = _dot_tn(a_ref[...], b_ref[...]).astype(BF16)

    return pl.pallas_call(
        body, name=name, grid=(K // tk, nb),
        in_specs=[pl.BlockSpec((S, tk), lambda i, j: (0, i)), pl.BlockSpec((None, S, tn), lambda i, j: (j, 0, 0))],
        out_specs=pl.BlockSpec((None, tk, tn), lambda i, j: (j, i, 0)),
        out_shape=jax.ShapeDtypeStruct((nb, K, tn), BF16),
        compiler_params=_cp(("arbitrary", "arbitrary")),
    )(a, b)


def tn_matmul_cols(a, b, tk, tn, name):
    S, K = a.shape
    N = b.shape[1]

    def body(a_ref, b_ref, o_ref):
        o_ref[...] = _dot_tn(a_ref[...], b_ref[...]).astype(BF16)

    return pl.pallas_call(
        body, name=name, grid=(K // tk, N // tn),
        in_specs=[pl.BlockSpec((S, tk), lambda i, j: (0, i)), pl.BlockSpec((S, tn), lambda i, j: (0, j))],
        out_specs=pl.BlockSpec((None, tk, tn), lambda i, j: (j, i, 0)),
        out_shape=jax.ShapeDtypeStruct((N // tn, K, tn), BF16),
        compiler_params=_cp(("arbitrary", "arbitrary")),
    )(a, b)


def pre_matmul(x, vec, w3, name, tm=256):
    S, D = x.shape
    nj, _, Nj = w3.shape

    def body(x_ref, vec_ref, w_ref, h_ref, z_ref):
        h = _pre_fwd(x_ref[...], vec_ref).astype(BF16)
        h_ref[...] = h
        for j in range(nj):
            z_ref[:, j * Nj:(j + 1) * Nj] = _dot(h, w_ref[j])

    row = pl.BlockSpec((tm, D), lambda i: (i, 0))
    return pl.pallas_call(
        body, name=name, grid=(S // tm,),
        in_specs=[row, pl.BlockSpec((8, D), lambda i: (0, 0)), pl.BlockSpec((nj, D, Nj), lambda i: (0, 0, 0))],
        out_specs=[row, pl.BlockSpec((tm, nj * Nj), lambda i: (i, 0))],
        out_shape=[jax.ShapeDtypeStruct((S, D), BF16), jax.ShapeDtypeStruct((S, nj * Nj), F32)],
        compiler_params=_cp(("arbitrary",)),
    )(x, vec, w3)


def matmul_pre_bwd(dz, w3, x, dres, vec, name, tm=256):
    S, D = x.shape
    nj, _, Nj = w3.shape

    def body(dz_ref, w_ref, x_ref, dres_ref, vec_ref, dx_ref, acc_ref):
        @pl.when(pl.program_id(0) == 0)
        def _():
            acc_ref[...] = jnp.zeros_like(acc_ref)

        dh = jnp.zeros((tm, D), F32)
        for j in range(nj):
            dh = dh + _dot_nt(dz_ref[:, j * Nj:(j + 1) * Nj], w_ref[j])
        dx, dshift, dscale, dgpre = _pre_bwd(x_ref[...], dh, vec_ref)
        dx_ref[...] = dres_ref[...] + dx
        _acc_add(acc_ref, A_SHIFT, dshift)
        _acc_add(acc_ref, A_SCALE, dscale)
        _acc_add(acc_ref, A_GPRE, dgpre)

    row = pl.BlockSpec((tm, D), lambda i: (i, 0))
    const = pl.BlockSpec((8, D), lambda i: (0, 0))
    return pl.pallas_call(
        body, name=name, grid=(S // tm,),
        in_specs=[pl.BlockSpec((tm, nj * Nj), lambda i: (i, 0)), pl.BlockSpec((nj, D, Nj), lambda i: (0, 0, 0)),
                  row, row, const],
        out_specs=[row, const],
        out_shape=[jax.ShapeDtypeStruct((S, D), F32), jax.ShapeDtypeStruct((8, D), F32)],
        compiler_params=_cp(("arbitrary",)),
    )(dz, w3, x, dres, vec)


def matmul_post(act, w, x, vec, name, tm=256):
    S, D = x.shape
    K = act.shape[1]

    def body(act_ref, w_ref, x_ref, vec_ref, out_ref, f_ref):
        f = _dot(act_ref[...], w_ref[...])
        f_ref[...] = f
        out_ref[...] = _post_fwd(x_ref[...], f, vec_ref, 1.0)

    row = pl.BlockSpec((tm, D), lambda i: (i, 0))
    return pl.pallas_call(
        body, name=name, grid=(S // tm,),
        in_specs=[pl.BlockSpec((tm, K), lambda i: (i, 0)), pl.BlockSpec((K, D), lambda i: (0, 0)), row,
                  pl.BlockSpec((8, D), lambda i: (0, 0))],
        out_specs=[row, row],
        out_shape=[jax.ShapeDtypeStruct((S, D), F32), jax.ShapeDtypeStruct((S, D), F32)],
        compiler_params=_cp(("arbitrary",)),
    )(act, w, x, vec)


def post_bwd_matmul(dout, f, vec, w, name, tm=256):
    S, D = dout.shape
    K = w.shape[0]

    def body(dout_ref, f_ref, vec_ref, w_ref, df_ref, dact_ref, acc_ref):
        @pl.when(pl.program_id(0) == 0)
        def _():
            acc_ref[...] = jnp.zeros_like(acc_ref)

        df, dgate, dgpost = _post_bwd(dout_ref[...], f_ref[...], vec_ref, 1.0)
        dfb = df.astype(BF16)
        df_ref[...] = dfb
        dact_ref[...] = _dot_nt(dfb, w_ref[...])
        _acc_add(acc_ref, A_GATE, dgate)
        _acc_add(acc_ref, A_GPOST, dgpost)

    row = pl.BlockSpec((tm, D), lambda i: (i, 0))
    const = pl.BlockSpec((8, D), lambda i: (0, 0))
    return pl.pallas_call(
        body, name=name, grid=(S // tm,),
        in_specs=[row, row, const, pl.BlockSpec((K, D), lambda i: (0, 0))],
        out_specs=[row, pl.BlockSpec((tm, K), lambda i: (i, 0)), const],
        out_shape=[jax.ShapeDtypeStruct((S, D), BF16), jax.ShapeDtypeStruct((S, K), F32),
                   jax.ShapeDtypeStruct((8, D), F32)],
        compiler_params=_cp(("arbitrary",)),
    )(dout, f, vec, w)


def _band(w, transposed, prev):
    r = lax.broadcasted_iota(jnp.int32, (CHUNK, CHUNK), 1 if transposed else 0)
    c = lax.broadcasted_iota(jnp.int32, (CHUNK, CHUNK), 0 if transposed else 1)
    d = r - c
    m = (d + CHUNK < w) if prev else ((d >= 0) & (d < w))
    return jnp.where(m, 1.0, 0.0).astype(BF16)


def _split_hi_lo(a):
    hi = a.astype(BF16)
    lo = (a - hi.astype(F32)).astype(BF16)
    return hi, lo


def _pool_diff(a, ap, g, denom):
    w = POOL_WINDOWS[g]
    a_hi, a_lo = _split_hi_lo(a)
    p_hi, p_lo = _split_hi_lo(ap)
    mc = _band(w, False, False)
    mp = _band(w, False, True)
    win = _dot(mc, a_hi) + _dot(mc, a_lo) + _dot(mp, p_hi) + _dot(mp, p_lo)
    return win / denom - a


def _sgu_norm(v, lg, lb):
    mu = _lanemean(v)
    xc = v - mu
    rstd = lax.rsqrt(_lanemean(xc * xc) + EPS)
    xhat = xc * rstd
    return xhat, rstd, xhat * lg + lb


def _tril_mask():
    r = lax.broadcasted_iota(jnp.int32, (CHUNK, CHUNK), 0)
    c = lax.broadcasted_iota(jnp.int32, (CHUNK, CHUNK), 1)
    return r >= c


def _positions(i, w):
    r = lax.broadcasted_iota(jnp.int32, (CHUNK, 128), 0)
    pos = (i * CHUNK + r + 1).astype(F32)
    return jnp.minimum(pos, float(w))


def mixa_core_fwd(z, pool_w, pool_scale, ln_g, ln_b, sgu_w, sgu_bexp, name):
    S = z.shape[0]
    W = z.shape[1] // 3
    G = len(POOL_WINDOWS)
    GD = W // G

    def body(zc_ref, zp_ref, pw_ref, ps_ref, lg_ref, lb_ref, sw_ref, sb_ref, y_ref):
        i = pl.program_id(0)
        has_prev = jnp.where(i > 0, 1.0, 0.0)
        for g in range(G):
            sl = slice(g * GD, (g + 1) * GD)
            a = zc_ref[:, sl]
            ap = zp_ref[:, sl] * has_prev
            d = _pool_diff(a, ap, g, _positions(i, POOL_WINDOWS[g])).astype(BF16)
            y_ref[:, sl] = (_dot(d, pw_ref[g]) * ps_ref[:, sl]).astype(BF16)
        tril = _tril_mask()
        for hh in range(G):
            u = _gelu(zc_ref[:, W + hh * GD:W + (hh + 1) * GD])
            v = _gelu(zc_ref[:, 2 * W + hh * GD:2 * W + (hh + 1) * GD])
            sl = slice(hh * GD, (hh + 1) * GD)
            _, _, vn = _sgu_norm(v, lg_ref[:, sl], lb_ref[:, sl])
            wm = jnp.where(tril, sw_ref[hh], 0.0).astype(BF16)
            s = _dot(wm, vn.astype(BF16)) + sb_ref[hh]
            y_ref[:, W + hh * GD:W + (hh + 1) * GD] = (u * s).astype(BF16)

    vecw = pl.BlockSpec((1, W), lambda i: (0, 0))
    mats = pl.BlockSpec((G, GD, GD), lambda i: (0, 0, 0))
    return pl.pallas_call(
        body, name=name, grid=(S // CHUNK,),
        in_specs=[pl.BlockSpec((CHUNK, 3 * W), lambda i: (i, 0)),
                  pl.BlockSpec((CHUNK, W), lambda i: (jnp.maximum(i - 1, 0), 0)),
                  mats, vecw, vecw, vecw, mats, mats],
        out_specs=pl.BlockSpec((CHUNK, 2 * W), lambda i: (i, 0)),
        out_shape=jax.ShapeDtypeStruct((S, 2 * W), BF16),
        compiler_params=_cp(("arbitrary",)),
    )(z, z, pool_w, pool_scale, ln_g, ln_b, sgu_w, sgu_bexp)


def mixa_core_bwd(z, dy, pool_w, pool_scale, ln_g, ln_b, sgu_w, sgu_bexp, name):
    S = z.shape[0]
    W = z.shape[1] // 3
    G = len(POOL_WINDOWS)
    GD = W // G
    n_tiles = S // CHUNK

    def body(zc_ref, zp_ref, dyc_ref, dyn_ref, pw_ref, ps_ref, lg_ref, lb_ref, sw_ref, sb_ref,
             dz_ref, dpw_ref, dvec_ref, dsw_ref, dsb_ref):
        i = pl.program_id(0)

        @pl.when(i == 0)
        def _():
            dpw_ref[...] = jnp.zeros_like(dpw_ref)
            dvec_ref[...] = jnp.zeros_like(dvec_ref)
            dsw_ref[...] = jnp.zeros_like(dsw_ref)
            dsb_ref[...] = jnp.zeros_like(dsb_ref)

        has_prev = jnp.where(i > 0, 1.0, 0.0)
        has_next = jnp.where(i < n_tiles - 1, 1.0, 0.0)
        for g in range(G):
            w = POOL_WINDOWS[g]
            sl = slice(g * GD, (g + 1) * GD)
            a = zc_ref[:, sl]
            ap = zp_ref[:, sl] * has_prev
            den_c = _positions(i, w)
            den_n = _positions(i + 1, w)
            d = _pool_diff(a, ap, g, den_c).astype(BF16)
            ps = ps_ref[:, sl]
            pw = pw_ref[g]
            dyc = dyc_ref[:, sl]
            dvec_ref[pl.ds(0, 1), sl] += _rowsum(dyc * _dot(d, pw))
            dyp_c = (dyc * ps).astype(BF16)
            dyp_n = (dyn_ref[:, sl] * (ps * has_next)).astype(BF16)
            dpw_ref[g] += _dot_tn(d, dyp_c)
            dd_c = _dot_nt(dyp_c, pw)
            dd_n = _dot_nt(dyp_n, pw)
            da = (_dot(_band(w, True, False), (dd_c / den_c).astype(BF16))
                  + _dot(_band(w, True, True), (dd_n / den_n).astype(BF16)) - dd_c)
            dz_ref[:, sl] = da.astype(BF16)
        tril = _tril_mask()
        for hh in range(G):
            sl = slice(hh * GD, (hh + 1) * GD)
            zu = zc_ref[:, W + hh * GD:W + (hh + 1) * GD]
            zv = zc_ref[:, 2 * W + hh * GD:2 * W + (hh + 1) * GD]
            u = _gelu(zu)
            v = _gelu(zv)
            lg = lg_ref[:, sl]
            xhat, rstd, vn = _sgu_norm(v, lg, lb_ref[:, sl])
            vnb = vn.astype(BF16)
            wm = jnp.where(tril, sw_ref[hh], 0.0).astype(BF16)
            s = _dot(wm, vnb) + sb_ref[hh]
            dyb = dyc_ref[:, W + hh * GD:W + (hh + 1) * GD]
            du = dyb * s
            ds = dyb * u
            dsb_ref[:, hh:hh + 1] += jnp.sum(ds, axis=1, keepdims=True)
            dsb16 = ds.astype(BF16)
            dsw_ref[hh] += jnp.where(tril, _dot_nt(dsb16, vnb), 0.0)
            dvn = _dot_tn(wm, dsb16)
            dvec_ref[pl.ds(1, 1), sl] += _rowsum(dvn * xhat)
            dvec_ref[pl.ds(2, 1), sl] += _rowsum(dvn)
            dxh = dvn * lg
            dv = rstd * (dxh - _lanemean(dxh) - xhat * _lanemean(dxh * xhat))
            dz_ref[:, W + hh * GD:W + (hh + 1) * GD] = (du * _gelu_grad(zu)).astype(BF16)
            dz_ref[:, 2 * W + hh * GD:2 * W + (hh + 1) * GD] = (dv * _gelu_grad(zv)).astype(BF16)

    vecw = pl.BlockSpec((1, W), lambda i: (0, 0))
    mats = pl.BlockSpec((G, GD, GD), lambda i: (0, 0, 0))
    return pl.pallas_call(
        body, name=name, grid=(n_tiles,),
        in_specs=[pl.BlockSpec((CHUNK, 3 * W), lambda i: (i, 0)),
                  pl.BlockSpec((CHUNK, W), lambda i: (jnp.maximum(i - 1, 0), 0)),
                  pl.BlockSpec((CHUNK, 2 * W), lambda i: (i, 0)),
                  pl.BlockSpec((CHUNK, W), lambda i: (jnp.minimum(i + 1, n_tiles - 1), 0)),
                  mats, vecw, vecw, vecw, mats, mats],
        out_specs=[pl.BlockSpec((CHUNK, 3 * W), lambda i: (i, 0)), mats,
                   pl.BlockSpec((8, W), lambda i: (0, 0)), mats, pl.BlockSpec((CHUNK, G), lambda i: (0, 0))],
        out_shape=[jax.ShapeDtypeStruct((S, 3 * W), BF16), jax.ShapeDtypeStruct((G, GD, GD), F32),
                   jax.ShapeDtypeStruct((8, W), F32), jax.ShapeDtypeStruct((G, CHUNK, CHUNK), F32),
                   jax.ShapeDtypeStruct((CHUNK, G), F32)],
        compiler_params=_cp(("arbitrary",)),
    )(z, z, dy, dy, pool_w, pool_scale, ln_g, ln_b, sgu_w, sgu_bexp)


def _cmul(ar, ai, br, bi):
    return ar * br - ai * bi, ar * bi + ai * br


def _cpow(ar, ai, n):
    rr, ri = None, None
    br, bi = ar, ai
    while n:
        if n & 1:
            rr, ri = (br, bi) if rr is None else _cmul(rr, ri, br, bi)
        n >>= 1
        if n:
            br, bi = _cmul(br, bi, br, bi)
    return rr, ri


def _seg_rows(k):
    return pl.ds(pl.multiple_of(k * SCAN_LANES, SCAN_LANES), SCAN_LANES)


def _scan_fwd(xre, xim, carry, ar, ai, K):
    P = xre.shape[1]

    def local(k, c):
        pr, pi = c
        rows = _seg_rows(k)
        nr = ar * pr - ai * pi + xre[rows, :]
        ni = ar * pi + ai * pr + xim[rows, :]
        xre[rows, :] = nr
        xim[rows, :] = ni
        return nr, ni

    er, ei = lax.fori_loop(1, K, local, (xre[pl.ds(0, SCAN_LANES), :], xim[pl.ds(0, SCAN_LANES), :]))
    akr, aki = _cpow(ar, ai, K)
    cr = jnp.zeros((1, P), F32)
    ci = jnp.zeros((1, P), F32)
    carry[pl.ds(0, 1), :] = cr
    carry[pl.ds(SCAN_LANES, 1), :] = ci
    for j in range(1, SCAN_LANES):
        tr, ti = _cmul(akr, aki, cr, ci)
        cr = er[j - 1:j, :] + tr
        ci = ei[j - 1:j, :] + ti
        carry[pl.ds(j, 1), :] = cr
        carry[pl.ds(SCAN_LANES + j, 1), :] = ci
    cmr = carry[pl.ds(0, SCAN_LANES), :]
    cmi = carry[pl.ds(SCAN_LANES, SCAN_LANES), :]

    def fix(k, c):
        pr, pi = c
        rows = _seg_rows(k)
        tr, ti = _cmul(pr, pi, cmr, cmi)
        xre[rows, :] += tr
        xim[rows, :] += ti
        return _cmul(pr, pi, ar, ai)

    lax.fori_loop(0, K, fix, (jnp.broadcast_to(ar, (SCAN_LANES, P)), jnp.broadcast_to(ai, (SCAN_LANES, P))))


def s5_core_fwd(u, ar, ai, bre, bim, cre_t, cim_t, dskip, name):
    S, D = u.shape
    nblk, UB, PB = bre.shape
    K = S // SCAN_LANES

    def body(u_ref, ar_ref, ai_ref, bre_ref, bim_ref, cre_ref, cim_ref, d_ref, y_ref, xre, xim, carry):
        uv = u_ref[...]
        ub = uv.astype(BF16)
        xre[...] = _dot(ub, bre_ref[...])
        xim[...] = _dot(ub, bim_ref[...])
        _scan_fwd(xre, xim, carry, ar_ref[...], ai_ref[...], K)
        y_ref[...] = (_dot(xre[...].astype(BF16), cre_ref[...]) - _dot(xim[...].astype(BF16), cim_ref[...])
                      + d_ref[...] * uv)

    ucol = pl.BlockSpec((S, UB), lambda i: (0, i))
    pvec = pl.BlockSpec((None, 1, PB), lambda i: (i, 0, 0))
    bmat = pl.BlockSpec((None, UB, PB), lambda i: (i, 0, 0))
    cmat = pl.BlockSpec((None, PB, UB), lambda i: (i, 0, 0))
    return pl.pallas_call(
        body, name=name, grid=(nblk,),
        in_specs=[ucol, pvec, pvec, bmat, bmat, cmat, cmat, pl.BlockSpec((1, UB), lambda i: (0, i))],
        out_specs=ucol,
        out_shape=jax.ShapeDtypeStruct((S, D), F32),
        scratch_shapes=[pltpu.VMEM((S, PB), F32), pltpu.VMEM((S, PB), F32), pltpu.VMEM((2 * SCAN_LANES, PB), F32)],
        compiler_params=_cp(("arbitrary",)),
    )(u, ar, ai, bre, bim, cre_t, cim_t, dskip)


def s5_core_bwd(u, dy, ar, ai, bre, bim, cre, cim, dskip, name):
    S, D = u.shape
    nblk, UB, PB = bre.shape
    K = S // SCAN_LANES

    def body(u_ref, dy_ref, ar_ref, ai_ref, bre_ref, bim_ref, cre_ref, cim_ref, d_ref,
             du_ref, dbre_ref, dbim_ref, dcre_ref, dcim_ref, dar_ref, dai_ref, dd_ref,
             xre, xim, gre, gim, carry, carry_b):
        ar = ar_ref[...]
        ai = ai_ref[...]
        uv = u_ref[...]
        ub = uv.astype(BF16)
        dyv = dy_ref[...]
        dyb = dyv.astype(BF16)
        xre[...] = _dot(ub, bre_ref[...])
        xim[...] = _dot(ub, bim_ref[...])
        _scan_fwd(xre, xim, carry, ar, ai, K)
        dcre_ref[...] = _dot_tn(xre[...].astype(BF16), dyb)
        dcim_ref[...] = -_dot_tn(xim[...].astype(BF16), dyb)
        gre[...] = _dot(dyb, cre_ref[...])
        gim[...] = -_dot(dyb, cim_ref[...])

        def local(s, c):
            k = K - 2 - s
            nr, ni = c
            rows = _seg_rows(k)
            tr = gre[rows, :] + ar * nr + ai * ni
            ti = gim[rows, :] + ar * ni - ai * nr
            gre[rows, :] = tr
            gim[rows, :] = ti
            return tr, ti

        last = _seg_rows(K - 1)
        fr, fi = lax.fori_loop(0, K - 1, local, (gre[last, :], gim[last, :]))
        akr, aki = _cpow(ar, -ai, K)
        cr = jnp.zeros((1, PB), F32)
        ci = jnp.zeros((1, PB), F32)
        carry_b[pl.ds(SCAN_LANES - 1, 1), :] = cr
        carry_b[pl.ds(2 * SCAN_LANES - 1, 1), :] = ci
        for j in range(SCAN_LANES - 2, -1, -1):
            tr, ti = _cmul(akr, aki, cr, ci)
            cr = fr[j + 1:j + 2, :] + tr
            ci = fi[j + 1:j + 2, :] + ti
            carry_b[pl.ds(j, 1), :] = cr
            carry_b[pl.ds(SCAN_LANES + j, 1), :] = ci
        cbr = carry_b[pl.ds(0, SCAN_LANES), :]
        cbi = carry_b[pl.ds(SCAN_LANES, SCAN_LANES), :]

        def fix(s, c):
            k = K - 1 - s
            pr, pi, dar, dai = c
            rows = _seg_rows(k)
            tr, ti = _cmul(pr, pi, cbr, cbi)
            g_r = gre[rows, :] + tr
            g_i = gim[rows, :] + ti
            gre[rows, :] = g_r
            gim[rows, :] = g_i
            prev = _seg_rows(jnp.maximum(k - 1, 0))
            first = k == 0
            xr = jnp.where(first, carry[pl.ds(0, SCAN_LANES), :], xre[prev, :])
            xi = jnp.where(first, carry[pl.ds(SCAN_LANES, SCAN_LANES), :], xim[prev, :])
            dar = dar + g_r * xr + g_i * xi
            dai = dai + g_i * xr - g_r * xi
            nr, ni = _cmul(pr, pi, ar, -ai)
            return nr, ni, dar, dai

        z8 = jnp.zeros((SCAN_LANES, PB), F32)
        _, _, dar, dai = lax.fori_loop(
            0, K, fix, (jnp.broadcast_to(ar, (SCAN_LANES, PB)), jnp.broadcast_to(-ai, (SCAN_LANES, PB)), z8, z8))
        dar_ref[...] = _rowsum(dar)
        dai_ref[...] = _rowsum(dai)
        grb = gre[...].astype(BF16)
        gib = gim[...].astype(BF16)
        dbre_ref[...] = _dot_tn(ub, grb)
        dbim_ref[...] = _dot_tn(ub, gib)
        du_ref[...] = _dot_nt(grb, bre_ref[...]) + _dot_nt(gib, bim_ref[...]) + d_ref[...] * dyv
        dd_ref[...] = _rowsum(dyv * uv)

    ucol = pl.BlockSpec((S, UB), lambda i: (0, i))
    pvec = pl.BlockSpec((None, 1, PB), lambda i: (i, 0, 0))
    bmat = pl.BlockSpec((None, UB, PB), lambda i: (i, 0, 0))
    cmat = pl.BlockSpec((None, PB, UB), lambda i: (i, 0, 0))
    dvec = pl.BlockSpec((1, UB), lambda i: (0, i))
    return pl.pallas_call(
        body, name=name, grid=(nblk,),
        in_specs=[ucol, ucol, pvec, pvec, bmat, bmat, bmat, bmat, dvec],
        out_specs=[ucol, bmat, bmat, cmat, cmat, pvec, pvec, dvec],
        out_shape=[jax.ShapeDtypeStruct((S, D), F32),
                   jax.ShapeDtypeStruct((nblk, UB, PB), F32), jax.ShapeDtypeStruct((nblk, UB, PB), F32),
                   jax.ShapeDtypeStruct((nblk, PB, UB), F32), jax.ShapeDtypeStruct((nblk, PB, UB), F32),
                   jax.ShapeDtypeStruct((nblk, 1, PB), F32), jax.ShapeDtypeStruct((nblk, 1, PB), F32),
                   jax.ShapeDtypeStruct((1, D), F32)],
        scratch_shapes=[pltpu.VMEM((S, PB), F32), pltpu.VMEM((S, PB), F32), pltpu.VMEM((S, PB), F32),
                        pltpu.VMEM((S, PB), F32), pltpu.VMEM((2 * SCAN_LANES, PB), F32),
                        pltpu.VMEM((2 * SCAN_LANES, PB), F32)],
        compiler_params=_cp(("arbitrary",)),
    )(u, dy, ar, ai, bre, bim, cre, cim, dskip)


def s5_glu_fwd(y, wglu, x, vec, name, tm=256):
    S, D = x.shape
    NJ = wglu.shape[-1]

    def body(y_ref, w_ref, x_ref, vec_ref, out_ref, f_ref):
        g = _gelu(y_ref[...]).astype(BF16)
        f = jnp.concatenate([_dot(g, w_ref[j]) * _sigmoid(_dot(g, w_ref[2 + j])) for j in range(2)], axis=1)
        f_ref[...] = f
        out_ref[...] = _post_fwd(x_ref[...], f, vec_ref, 1.0)

    row = pl.BlockSpec((tm, D), lambda i: (i, 0))
    return pl.pallas_call(
        body, name=name, grid=(S // tm,),
        in_specs=[row, pl.BlockSpec((N_CHIPS, D, NJ), lambda i: (0, 0, 0)), row, pl.BlockSpec((8, D), lambda i: (0, 0))],
        out_specs=[row, row],
        out_shape=[jax.ShapeDtypeStruct((S, D), F32), jax.ShapeDtypeStruct((S, D), F32)],
        compiler_params=_cp(("arbitrary",)),
    )(y, wglu, x, vec)


def s5_glu_bwd(dout, f, y, vec, wglu, name, tm=256):
    S, D = dout.shape
    NJ = wglu.shape[-1]

    def body(dout_ref, f_ref, y_ref, vec_ref, w_ref, dy_ref, dab_ref, g_ref, acc_ref):
        @pl.when(pl.program_id(0) == 0)
        def _():
            acc_ref[...] = jnp.zeros_like(acc_ref)

        df, dgate, dgpost = _post_bwd(dout_ref[...], f_ref[...], vec_ref, 1.0)
        yv = y_ref[...]
        g = _gelu(yv).astype(BF16)
        g_ref[...] = g
        dg = jnp.zeros((tm, D), F32)
        for j in range(2):
            a = _dot(g, w_ref[j])
            sig = _sigmoid(_dot(g, w_ref[2 + j]))
            dfj = df[:, j * NJ:(j + 1) * NJ]
            da = (dfj * sig).astype(BF16)
            db = (dfj * a * sig * (1.0 - sig)).astype(BF16)
            dab_ref[:, j * NJ:(j + 1) * NJ] = da
            dab_ref[:, (2 + j) * NJ:(3 + j) * NJ] = db
            dg = dg + _dot_nt(da, w_ref[j]) + _dot_nt(db, w_ref[2 + j])
        dy_ref[...] = dg * _gelu_grad(yv)
        _acc_add(acc_ref, A_GATE, dgate)
        _acc_add(acc_ref, A_GPOST, dgpost)

    row = pl.BlockSpec((tm, D), lambda i: (i, 0))
    const = pl.BlockSpec((8, D), lambda i: (0, 0))
    return pl.pallas_call(
        body, name=name, grid=(S // tm,),
        in_specs=[row, row, row, const, pl.BlockSpec((N_CHIPS, D, NJ), lambda i: (0, 0, 0))],
        out_specs=[row, pl.BlockSpec((tm, N_CHIPS * NJ), lambda i: (i, 0)), row, const],
        out_shape=[jax.ShapeDtypeStruct((S, D), F32), jax.ShapeDtypeStruct((S, N_CHIPS * NJ), BF16),
                   jax.ShapeDtypeStruct((S, D), BF16), jax.ShapeDtypeStruct((8, D), F32)],
        compiler_params=_cp(("arbitrary",)),
    )(dout, f, y, vec, wglu)


def loss_head(y, target, name, tm=256):
    S, D = y.shape

    def body(y_ref, t_ref, d_ref, l_ref):
        @pl.when(pl.program_id(0) == 0)
        def _():
            l_ref[...] = jnp.zeros_like(l_ref)

        err = y_ref[...] - t_ref[...]
        d_ref[...] = err * (1.0 / D)
        l_ref[...] += jnp.sum(_rowsum(err * err), axis=1, keepdims=True)

    row = pl.BlockSpec((tm, D), lambda i: (i, 0))
    return pl.pallas_call(
        body, name=name, grid=(S // tm,),
        in_specs=[row, row],
        out_specs=[row, pl.BlockSpec((1, 1), lambda i: (0, 0))],
        out_shape=[jax.ShapeDtypeStruct((S, D), F32), jax.ShapeDtypeStruct((1, 1), F32)],
        compiler_params=_cp(("arbitrary",)),
    )(y, target)


def ada_mod(c_all, ada_w, ada_b_shard, name, tn=768):
    B, D = c_all.shape
    L, _, NS = ada_w.shape

    def body(c_ref, w_ref, b_ref, o_ref):
        cv = c_ref[...]
        cond = (cv * _sigmoid(cv)).astype(BF16)
        o_ref[...] = _dot(cond, w_ref[...].astype(BF16)) + b_ref[...]

    return pl.pallas_call(
        body, name=name, grid=(L, NS // tn),
        in_specs=[pl.BlockSpec((B, D), lambda l, j: (0, 0)), pl.BlockSpec((None, D, tn), lambda l, j: (l, 0, j)),
                  pl.BlockSpec((None, 1, tn), lambda l, j: (l, 0, j))],
        out_specs=pl.BlockSpec((None, B, tn), lambda l, j: (l, 0, j)),
        out_shape=jax.ShapeDtypeStruct((L, B, NS), F32),
        compiler_params=_cp(("arbitrary", "arbitrary")),
    )(c_all, ada_w, ada_b_shard)


def ada_grad(c_all, dmod, name, tn=768):
    B, D = c_all.shape
    L, _, NS = dmod.shape

    def body(c_ref, d_ref, o_ref):
        cv = c_ref[...]
        cond = (cv * _sigmoid(cv)).astype(BF16)
        o_ref[...] = _dot_tn(cond, d_ref[...].astype(BF16))

    return pl.pallas_call(
        body, name=name, grid=(L, NS // tn),
        in_specs=[pl.BlockSpec((B, D), lambda l, j: (0, 0)), pl.BlockSpec((None, B, tn), lambda l, j: (l, 0, j))],
        out_specs=pl.BlockSpec((None, D, tn), lambda l, j: (l, 0, j)),
        out_shape=jax.ShapeDtypeStruct((L, D, NS), F32),
        compiler_params=_cp(("arbitrary", "arbitrary")),
    )(c_all, dmod)


def sum_leading(x, name):
    n, R, C = x.shape

    def body(x_ref, o_ref):
        acc = x_ref[0].astype(F32)
        for i in range(1, n):
            acc = acc + x_ref[i].astype(F32)
        o_ref[...] = acc

    tr = _row_tile(R, 64)
    return pl.pallas_call(
        body, name=name, grid=(R // tr,),
        in_specs=[pl.BlockSpec((n, tr, C), lambda i: (0, i, 0))],
        out_specs=pl.BlockSpec((tr, C), lambda i: (i, 0)),
        out_shape=jax.ShapeDtypeStruct((R, C), F32),
        compiler_params=_cp(("arbitrary",)),
    )(x)


def _row_tile(R, cap=512):
    if R <= cap:
        return R
    for cand in (512, 384, 352, 256, 128, 64, 32, 16, 8):
        if cand <= cap and R % cand == 0:
            return cand
    return R


def adamw(w, g, m, v, name):
    R, C = w.shape
    tr = _row_tile(R, 256 if C > 1024 else 512)
    bc1 = 1.0 - ADAM_B1 ** ADAM_STEP
    bc2 = 1.0 - ADAM_B2 ** ADAM_STEP

    def body(w_ref, g_ref, m_ref, v_ref, d_ref, nm_ref, nv_ref):
        gv = g_ref[...]
        nm = ADAM_B1 * m_ref[...] + (1.0 - ADAM_B1) * gv
        nv = ADAM_B2 * v_ref[...] + (1.0 - ADAM_B2) * (gv * gv)
        nm_ref[...] = nm
        nv_ref[...] = nv
        d_ref[...] = -ADAM_LR * ((nm / bc1) / (jnp.sqrt(nv / bc2) + ADAM_EPS) + ADAM_WD * w_ref[...])

    blk = pl.BlockSpec((tr, C), lambda i: (i, 0))
    sd = jax.ShapeDtypeStruct((R, C), F32)
    return pl.pallas_call(
        body, name=name, grid=(R // tr,),
        in_specs=[blk, blk, blk, blk], out_specs=[blk, blk, blk], out_shape=[sd, sd, sd],
        compiler_params=_cp(("arbitrary",)),
    )(w, g, m, v)


def _as2d(a):
    if a.ndim == 1:
        return a.reshape(1, -1)
    return a.reshape(-1, a.shape[-1])


def adamw_nd(w, g, m, v, name):
    outs = adamw(_as2d(w), _as2d(g.reshape(w.shape)), _as2d(m), _as2d(v), name)
    return tuple(o.reshape(w.shape) for o in outs)


def _place():
    x, y, c = lax.axis_index("x"), lax.axis_index("y"), lax.axis_index("c")
    chips = [(1 - x, y), (x, 1 - y), (1 - x, 1 - y)]
    return x, y, c, chips


def allgather_small(xs, name):
    m_per, n = xs.shape

    def body(x_ref, out_ref, send_sems, recv_sems, local_sem):
        x, y, c, chips = _place()
        me, sibling = (x, y, c), (x, y, 1 - c)

        def rows(px, py, pc):
            return out_ref.at[pl.ds((4 * px + 2 * py + pc) * m_per, m_per), :]

        def copy(k, block, to, src=None):
            return pltpu.make_async_remote_copy(
                src_ref=rows(*block) if src is None else src, dst_ref=rows(*block),
                send_sem=send_sems.at[k], recv_sem=recv_sems.at[k], device_id=to, device_id_type=MESH_T)

        mine = pltpu.make_async_copy(x_ref, rows(*me), local_sem)
        mine.start()
        first = [copy(0, me, sibling, src=x_ref)]
        first += [copy(1 + j, me, (*chip, c), src=x_ref) for j, chip in enumerate(chips)]
        for cp in first:
            cp.start()
        passed = [copy(4 + j, (*chip, c), sibling) for j, chip in enumerate(chips)]
        for j, chip in enumerate(chips):
            copy(1 + j, (*chip, c), me).wait_recv()
            passed[j].start()
        copy(0, sibling, me).wait_recv()
        for j, chip in enumerate(chips):
            copy(4 + j, (*chip, 1 - c), me).wait_recv()
        for cp in first + passed:
            cp.wait_send()
        mine.wait()

    return pl.pallas_call(
        body, name=name,
        out_shape=jax.ShapeDtypeStruct((N_DEV * m_per, n), xs.dtype),
        in_specs=[pl.BlockSpec(memory_space=pltpu.VMEM)],
        out_specs=pl.BlockSpec(memory_space=pltpu.VMEM),
        scratch_shapes=[pltpu.SemaphoreType.DMA((7,)), pltpu.SemaphoreType.DMA((7,)), pltpu.SemaphoreType.DMA],
        compiler_params=_cp(),
    )(xs)


def gather_weights(shards, name):
    n = len(shards)

    def body(*refs):
        ins, outs = refs[:n], refs[n:2 * n]
        send_sems, recv_sems, fsend_sems, frecv_sems, local_sems = refs[2 * n:]
        x, y, c, chips = _place()
        me_chip = 2 * x + y
        sibling = (x, y, 1 - c)
        locs = []
        for a in range(n):
            cp = pltpu.make_async_copy(ins[a], outs[a].at[me_chip], local_sems.at[a])
            cp.start()
            locs.append(cp)

        def ici(a, k, src_chip, to):
            return pltpu.make_async_remote_copy(
                src_ref=ins[a].at[c], dst_ref=outs[a].at[src_chip, c],
                send_sem=send_sems.at[a * 3 + k], recv_sem=recv_sems.at[a * 3 + k], device_id=to, device_id_type=MESH_T)

        def d2d(a, k, src_chip, half):
            return pltpu.make_async_remote_copy(
                src_ref=outs[a].at[src_chip, half], dst_ref=outs[a].at[src_chip, half],
                send_sem=fsend_sems.at[a * 3 + k], recv_sem=frecv_sems.at[a * 3 + k],
                device_id=sibling, device_id_type=MESH_T)

        firsts = []
        for k, (cx, cy) in enumerate(chips):
            for a in range(n):
                cp = ici(a, k, me_chip, (cx, cy, c))
                cp.start()
                firsts.append(cp)
        passed = []
        for k, (cx, cy) in enumerate(chips):
            for a in range(n):
                ici(a, k, 2 * cx + cy, (cx, cy, c)).wait_recv()
                cp = d2d(a, k, 2 * cx + cy, c)
                cp.start()
                passed.append(cp)
        for k, (cx, cy) in enumerate(chips):
            for a in range(n):
                d2d(a, k, 2 * cx + cy, 1 - c).wait_recv()
        for cp in firsts + passed:
            cp.wait_send()
        for cp in locs:
            cp.wait()

    return pl.pallas_call(
        body, name=name,
        out_shape=[jax.ShapeDtypeStruct((N_CHIPS,) + s.shape, s.dtype) for s in shards],
        in_specs=[ANY] * n, out_specs=[ANY] * n,
        scratch_shapes=[pltpu.SemaphoreType.DMA((3 * n,)), pltpu.SemaphoreType.DMA((3 * n,)),
                        pltpu.SemaphoreType.DMA((3 * n,)), pltpu.SemaphoreType.DMA((3 * n,)),
                        pltpu.SemaphoreType.DMA((n,))],
        compiler_params=_cp(),
    )(*shards)


def _half_rows(ref, half, rh):
    idx = (slice(None),) * (len(ref.shape) - 2) + (pl.ds(pl.multiple_of(half * rh, 16), rh), slice(None))
    return ref.at[idx]


def sibling_exchange_halves(grads, name):
    n = len(grads)

    def body(*refs):
        ins, outs = refs[:n], refs[n:2 * n]
        send_sems, recv_sems = refs[2 * n:]
        x, y, c, _ = _place()
        cps = []
        for a in range(n):
            rh = ins[a].shape[-2] // 2
            cp = pltpu.make_async_remote_copy(
                src_ref=_half_rows(ins[a], 1 - c, rh), dst_ref=outs[a],
                send_sem=send_sems.at[a], recv_sem=recv_sems.at[a], device_id=(x, y, 1 - c), device_id_type=MESH_T)
            cp.start()
            cps.append(cp)
        for cp in cps:
            cp.wait()

    return pl.pallas_call(
        body, name=name,
        out_shape=[jax.ShapeDtypeStruct(g.shape[:-2] + (g.shape[-2] // 2, g.shape[-1]), g.dtype) for g in grads],
        in_specs=[ANY] * n, out_specs=[ANY] * n,
        scratch_shapes=[pltpu.SemaphoreType.DMA((n,)), pltpu.SemaphoreType.DMA((n,))],
        compiler_params=_cp(),
    )(*grads)


def pair_sum(g, recv, cidx, name):
    L, NS, R, C = g.shape
    rh = R // 2
    tr = _row_tile(rh, 256 if C > 1024 else 512)
    nt = rh // tr

    def body(c_ref, g_ref, r_ref, o_ref):
        o_ref[...] = (g_ref[...].astype(F32) + r_ref[...].astype(F32)).astype(BF16)

    grid_spec = pltpu.PrefetchScalarGridSpec(
        num_scalar_prefetch=1, grid=(L, NS, nt),
        in_specs=[pl.BlockSpec((None, None, tr, C), lambda l, s, t, c: (l, s, c[0] * nt + t, 0)),
                  pl.BlockSpec((None, None, tr, C), lambda l, s, t, c: (l, s, t, 0))],
        out_specs=pl.BlockSpec((None, None, tr, C), lambda l, s, t, c: (l, s, t, 0)))
    return pl.pallas_call(
        body, name=name, grid_spec=grid_spec,
        out_shape=jax.ShapeDtypeStruct((L, NS, rh, C), BF16),
        compiler_params=_cp(("arbitrary", "arbitrary", "arbitrary")),
    )(cidx, g, recv)


def chip_exchange(parts, name):
    n = len(parts)

    def body(*refs):
        ins, outs = refs[:n], refs[n:2 * n]
        send_sems, recv_sems = refs[2 * n:]
        x, y, c, chips = _place()
        cps = []
        for k, (cx, cy) in enumerate(chips):
            for a in range(n):
                cp = pltpu.make_async_remote_copy(
                    src_ref=ins[a].at[:, 2 * cx + cy], dst_ref=outs[a].at[k],
                    send_sem=send_sems.at[a * 3 + k], recv_sem=recv_sems.at[a * 3 + k],
                    device_id=(cx, cy, c), device_id_type=MESH_T)
                cp.start()
                cps.append(cp)
        for cp in cps:
            cp.wait()

    return pl.pallas_call(
        body, name=name,
        out_shape=[jax.ShapeDtypeStruct((3, p.shape[0]) + p.shape[2:], p.dtype) for p in parts],
        in_specs=[ANY] * n, out_specs=[ANY] * n,
        scratch_shapes=[pltpu.SemaphoreType.DMA((3 * n,)), pltpu.SemaphoreType.DMA((3 * n,))],
        compiler_params=_cp(),
    )(*parts)


def chip_sum(part, recv, chip_c, name, dest=None, slot=0, n_slots=1):
    _, NS, RH, C = part.shape
    tr = _row_tile(RH, 256 if C > 1024 else 512)
    nt = RH // tr

    def body(cc_ref, p_ref, r_ref, *rest):
        o_ref = rest[-1]
        acc = p_ref[...].astype(F32)
        for k in range(3):
            acc = acc + r_ref[k].astype(F32)
        o_ref[...] = acc

    in_specs = [pl.BlockSpec((None, None, tr, C), lambda t, cc: (0, cc[0], t, 0)),
                pl.BlockSpec((3, None, tr, C), lambda t, cc: (0, 0, t, 0))]
    args = [chip_c, part, recv]
    aliases = {}
    if dest is not None:
        in_specs.append(ANY)
        args.append(dest)
        aliases = {3: 0}
    grid_spec = pltpu.PrefetchScalarGridSpec(
        num_scalar_prefetch=1, grid=(nt,), in_specs=in_specs,
        out_specs=pl.BlockSpec((None, tr, C), lambda t, cc: (slot, cc[1] * nt + t, 0)))
    return pl.pallas_call(
        body, name=name, grid_spec=grid_spec,
        out_shape=jax.ShapeDtypeStruct((n_slots, 2 * RH, C), F32),
        input_output_aliases=aliases,
        compiler_params=_cp(("arbitrary",)),
    )(*args)


def sibling_fill_halves(bufs, name):
    n = len(bufs)

    def body(*refs):
        outs = refs[n:2 * n]
        send_sems, recv_sems = refs[2 * n:]
        x, y, c, _ = _place()
        cps = []
        for a in range(n):
            rh = outs[a].shape[-2] // 2
            cp = pltpu.make_async_remote_copy(
                src_ref=_half_rows(outs[a], c, rh), dst_ref=_half_rows(outs[a], c, rh),
                send_sem=send_sems.at[a], recv_sem=recv_sems.at[a], device_id=(x, y, 1 - c), device_id_type=MESH_T)
            cp.start()
            cps.append(cp)
        for a, cp in enumerate(cps):
            cp.wait_send()
        for a in range(n):
            rh = outs[a].shape[-2] // 2
            pltpu.make_async_remote_copy(
                src_ref=_half_rows(outs[a], 1 - c, rh), dst_ref=_half_rows(outs[a], 1 - c, rh),
                send_sem=send_sems.at[a], recv_sem=recv_sems.at[a], device_id=(x, y, 1 - c),
                device_id_type=MESH_T).wait_recv()

    return pl.pallas_call(
        body, name=name,
        out_shape=[jax.ShapeDtypeStruct(b.shape, b.dtype) for b in bufs],
        in_specs=[ANY] * n, out_specs=[ANY] * n,
        input_output_aliases={a: a for a in range(n)},
        scratch_shapes=[pltpu.SemaphoreType.DMA((n,)), pltpu.SemaphoreType.DMA((n,))],
        compiler_params=_cp(),
    )(*bufs)


HBM_SPEC = pl.BlockSpec(memory_space=pltpu.HBM)
SEM_SPEC = pl.BlockSpec(memory_space=pltpu.SEMAPHORE)
DATAFLOW_EFFECT = pltpu.SideEffectType.DATAFLOW_SIDE_EFFECTING


def split_start(bufs, copies_fn, n_copies, name):
    nb = len(bufs)

    def body(*refs):
        send_sems, recv_sems = refs[nb], refs[nb + 1]
        token = refs[-1]
        for cp in copies_fn(refs[:nb], send_sems, recv_sems):
            cp.start()
        token[...] = jnp.zeros_like(token)

    outs = pl.pallas_call(
        body, name=name,
        out_shape=(pltpu.SemaphoreType.DMA((n_copies,)), pltpu.SemaphoreType.DMA((n_copies,)),
                   *[pltpu.HBM(b.shape, b.dtype) for b in bufs], jax.ShapeDtypeStruct((8, 128), F32)),
        in_specs=[HBM_SPEC] * nb,
        out_specs=(SEM_SPEC, SEM_SPEC, *[HBM_SPEC] * nb, pl.BlockSpec(memory_space=pltpu.VMEM)),
        input_output_aliases={i: 2 + i for i in range(nb)},
        compiler_params=pltpu.CompilerParams(has_side_effects=DATAFLOW_EFFECT),
    )(*[pltpu.with_memory_space_constraint(b, pltpu.HBM) for b in bufs])
    return outs[0], outs[1], list(outs[2:2 + nb]), outs[-1]


def split_wait(send_sems, recv_sems, bufs, after, copies_fn, name):
    nb = len(bufs)

    def body(*refs):
        for cp in copies_fn(refs[:nb], refs[nb], refs[nb + 1]):
            cp.wait_send()
            cp.wait_recv()

    outs = pl.pallas_call(
        body, name=name,
        out_shape=tuple(pltpu.HBM(b.shape, b.dtype) for b in bufs),
        in_specs=[HBM_SPEC] * nb + [SEM_SPEC, SEM_SPEC, ANY],
        out_specs=tuple([HBM_SPEC] * nb),
        input_output_aliases={i: i for i in range(nb)},
        compiler_params=pltpu.CompilerParams(has_side_effects=DATAFLOW_EFFECT),
    )(*bufs, send_sems, recv_sems, after)
    return list(outs)


def chip_exchange_copies(refs, send_sems, recv_sems):
    n = len(refs) // 2
    x, y, c, chips = _place()
    cps = []
    for a in range(n):
        for k, (cx, cy) in enumerate(chips):
            cps.append(pltpu.make_async_remote_copy(
                src_ref=refs[a].at[:, 2 * cx + cy], dst_ref=refs[n + a].at[k],
                send_sem=send_sems.at[a * 3 + k], recv_sem=recv_sems.at[a * 3 + k],
                device_id=(cx, cy, c), device_id_type=MESH_T))
    return cps


def weight_ici_copies(refs, send_sems, recv_sems):
    n = len(refs) // 2
    x, y, c, chips = _place()
    cps = []
    for a in range(n):
        for k, (cx, cy) in enumerate(chips):
            cps.append(pltpu.make_async_remote_copy(
                src_ref=refs[a].at[c], dst_ref=refs[n + a].at[2 * x + y, c],
                send_sem=send_sems.at[a * 3 + k], recv_sem=recv_sems.at[a * 3 + k],
                device_id=(cx, cy, c), device_id_type=MESH_T))
    return cps


def weight_d2d_copies(refs, send_sems, recv_sems):
    x, y, c, chips = _place()
    cps = []
    for a in range(len(refs)):
        for k, (cx, cy) in enumerate(chips):
            cps.append(pltpu.make_async_remote_copy(
                src_ref=refs[a].at[2 * cx + cy, c], dst_ref=refs[a].at[2 * cx + cy, c],
                send_sem=send_sems.at[a * 3 + k], recv_sem=recv_sems.at[a * 3 + k],
                device_id=(x, y, 1 - c), device_id_type=MESH_T))
    return cps


def place_own_shards(shards, name):
    n = len(shards)

    def body(*refs):
        ins, outs, sems = refs[:n], refs[n:2 * n], refs[2 * n]
        x, y, _, _ = _place()
        cps = [pltpu.make_async_copy(ins[a], outs[a].at[2 * x + y], sems.at[a]) for a in range(n)]
        for cp in cps:
            cp.start()
        for cp in cps:
            cp.wait()

    return pl.pallas_call(
        body, name=name,
        out_shape=[jax.ShapeDtypeStruct((N_CHIPS,) + s.shape, s.dtype) for s in shards],
        in_specs=[ANY] * n, out_specs=[ANY] * n,
        scratch_shapes=[pltpu.SemaphoreType.DMA((n,))],
        compiler_params=_cp(),
    )(*shards)


def _s5_prepare(lam_re, lam_im, log_dt, b_re, b_im, c_re, c_im, groups_per_block):
    G, P = lam_re.shape
    N = b_re.shape[-1]
    gb = groups_per_block
    nblk = G // gb
    dt = jnp.exp(log_dt)[:, None]
    e = jnp.exp(lam_re * dt)
    a_re = e * jnp.cos(lam_im * dt)
    a_im = e * jnp.sin(lam_im * dt)
    n2 = lam_re * lam_re + lam_im * lam_im
    co_re = ((a_re - 1.0) * lam_re + a_im * lam_im) / n2
    co_im = (a_im * lam_re - (a_re - 1.0) * lam_im) / n2
    bb_re = co_re[..., None] * b_re - co_im[..., None] * b_im
    bb_im = co_re[..., None] * b_im + co_im[..., None] * b_re
    eye = jnp.eye(gb, dtype=F32)

    def blockdiag_np(m):
        m = m.reshape(nblk, gb, N, P)
        return jnp.einsum('bgnp,gh->bgnhp', m, eye).reshape(nblk, gb * N, gb * P)

    b_np_re = jnp.swapaxes(bb_re, 1, 2)
    b_np_im = jnp.swapaxes(bb_im, 1, 2)
    return (a_re.reshape(nblk, 1, gb * P), a_im.reshape(nblk, 1, gb * P),
            blockdiag_np(b_np_re), blockdiag_np(b_np_im), blockdiag_np(c_re), blockdiag_np(c_im))


def _to_scan_order(a):
    S, D = a.shape
    return a.reshape(SCAN_LANES, S // SCAN_LANES, D).transpose(1, 0, 2).reshape(S, D)


def _from_scan_order(a):
    S, D = a.shape
    return a.reshape(S // SCAN_LANES, SCAN_LANES, D).transpose(1, 0, 2).reshape(S, D)


def _pad_rows(a, mult=8):
    r = (-a.shape[0]) % mult
    if r:
        a = jnp.concatenate([a, jnp.zeros((r, a.shape[1]), a.dtype)], axis=0)
    return a


def _pack_rows(arrs, width):
    parts, offs, o = [], [], 0
    for a in arrs:
        flat = a.reshape(-1)
        r = (-flat.shape[0]) % (16 * width)
        if r:
            flat = jnp.concatenate([flat, jnp.zeros((r,), flat.dtype)])
        p = flat.reshape(-1, width)
        parts.append(p)
        offs.append((o, a.shape, a.size))
        o += p.shape[0]
    if o % 64:
        parts.append(jnp.zeros((64 - o % 64, width), parts[0].dtype))
    return jnp.concatenate(parts, axis=0), offs


def _unpack_rows(packed, offs):
    outs = []
    for o, shape, size in offs:
        rows = -(-size // packed.shape[1])
        outs.append(packed[o:o + rows].reshape(-1)[:size].reshape(shape))
    return outs


def kernel(x, c, ada_w, ada_b, norm_pre, norm_post, ffn_w_in, ffn_w_out, ab_w_in, pool_w, pool_scale, sgu_ln_g, sgu_ln_b, sgu_w, sgu_b, ab_w_out, ssm_w_in, ssm_lam_re, ssm_lam_im, ssm_b_re, ssm_b_im, ssm_c_re, ssm_c_im, ssm_d, ssm_log_dt, ssm_w_glu, loss_target, m_ada_w, m_ada_b, m_norm_pre, m_norm_post, m_ffn_w_in, m_ffn_w_out, m_ab_w_in, m_pool_w, m_pool_scale, m_sgu_ln_g, m_sgu_ln_b, m_sgu_w, m_sgu_b, m_ab_w_out, m_ssm_w_in, m_ssm_lam_re, m_ssm_lam_im, m_ssm_b_re, m_ssm_b_im, m_ssm_c_re, m_ssm_c_im, m_ssm_d, m_ssm_log_dt, m_ssm_w_glu, v_ada_w, v_ada_b, v_norm_pre, v_norm_post, v_ffn_w_in, v_ffn_w_out, v_ab_w_in, v_pool_w, v_pool_scale, v_sgu_ln_g, v_sgu_ln_b, v_sgu_w, v_sgu_b, v_ab_w_out, v_ssm_w_in, v_ssm_lam_re, v_ssm_lam_im, v_ssm_b_re, v_ssm_b_im, v_ssm_c_re, v_ssm_c_im, v_ssm_d, v_ssm_log_dt, v_ssm_w_glu):
    weights = dict(ada_w=ada_w, ada_b=ada_b, norm_pre=norm_pre, norm_post=norm_post, ffn_w_in=ffn_w_in,
                   ffn_w_out=ffn_w_out, ab_w_in=ab_w_in, pool_w=pool_w, pool_scale=pool_scale, sgu_ln_g=sgu_ln_g,
                   sgu_ln_b=sgu_ln_b, sgu_w=sgu_w, sgu_b=sgu_b, ab_w_out=ab_w_out, ssm_w_in=ssm_w_in,
                   ssm_lam_re=ssm_lam_re, ssm_lam_im=ssm_lam_im, ssm_b_re=ssm_b_re, ssm_b_im=ssm_b_im,
                   ssm_c_re=ssm_c_re, ssm_c_im=ssm_c_im, ssm_d=ssm_d, ssm_log_dt=ssm_log_dt, ssm_w_glu=ssm_w_glu)
    m_in = dict(ada_w=m_ada_w, ada_b=m_ada_b, norm_pre=m_norm_pre, norm_post=m_norm_post, ffn_w_in=m_ffn_w_in,
                ffn_w_out=m_ffn_w_out, ab_w_in=m_ab_w_in, pool_w=m_pool_w, pool_scale=m_pool_scale,
                sgu_ln_g=m_sgu_ln_g, sgu_ln_b=m_sgu_ln_b, sgu_w=m_sgu_w, sgu_b=m_sgu_b, ab_w_out=m_ab_w_out,
                ssm_w_in=m_ssm_w_in, ssm_lam_re=m_ssm_lam_re, ssm_lam_im=m_ssm_lam_im, ssm_b_re=m_ssm_b_re,
                ssm_b_im=m_ssm_b_im, ssm_c_re=m_ssm_c_re, ssm_c_im=m_ssm_c_im, ssm_d=m_ssm_d,
                ssm_log_dt=m_ssm_log_dt, ssm_w_glu=m_ssm_w_glu)
    v_in = dict(ada_w=v_ada_w, ada_b=v_ada_b, norm_pre=v_norm_pre, norm_post=v_norm_post, ffn_w_in=v_ffn_w_in,
                ffn_w_out=v_ffn_w_out, ab_w_in=v_ab_w_in, pool_w=v_pool_w, pool_scale=v_pool_scale,
                sgu_ln_g=v_sgu_ln_g, sgu_ln_b=v_sgu_ln_b, sgu_w=v_sgu_w, sgu_b=v_sgu_b, ab_w_out=v_ab_w_out,
                ssm_w_in=v_ssm_w_in, ssm_lam_re=v_ssm_lam_re, ssm_lam_im=v_ssm_lam_im, ssm_b_re=v_ssm_b_re,
                ssm_b_im=v_ssm_b_im, ssm_c_re=v_ssm_c_re, ssm_c_im=v_ssm_c_im, ssm_d=v_ssm_d,
                ssm_log_dt=v_ssm_log_dt, ssm_w_glu=v_ssm_w_glu)
    names = list(weights.keys())

    xi, yi, ci = lax.axis_index("x"), lax.axis_index("y"), lax.axis_index("c")
    chip = 2 * xi + yi
    me = 4 * xi + 2 * yi + ci
    S, D = x.shape[1], x.shape[2]
    L = ada_w.shape[0]
    NSUB = norm_pre.shape[1]
    DS = norm_pre.shape[2]
    FS = ffn_w_in.shape[-1]
    FR = ffn_w_out.shape[-2]
    x0 = x[0]
    target = loss_target[0]

    def halves(a):
        return a.astype(BF16).reshape((2, a.shape[0] // 2) + a.shape[1:])

    pieces = []
    for l in range(L):
        pieces.append((f"ffn_{l}_0", [halves(ffn_w_in[l, 0]), halves(ffn_w_out[l, 0])]))
        if l % 2 == 0:
            pieces.append((f"mix_{l}", [halves(ab_w_in[l // 2]), halves(ab_w_out[l // 2])]))
        else:
            pieces.append((f"mix_{l}", [halves(ssm_w_in[l // 2]), halves(ssm_w_glu[l // 2])]))
        pieces.append((f"ffn_{l}_1", [halves(ffn_w_in[l, 1]), halves(ffn_w_out[l, 1])]))
    all_shards = [sh for _, shs in pieces for sh in shs]
    placed = list(place_own_shards(all_shards, "wg_place"))
    wg_started = {}
    wg_token = None
    o = 0
    for tag, shs in pieces:
        n = len(shs)
        started = split_start(shs + placed[o:o + n], weight_ici_copies, 3 * n, f"wg_ici_start_{tag}")
        wg_started[tag] = started
        wg_token = started[3][0, 0] if wg_token is None else wg_token + started[3][0, 0]
        o += n

    def weights_of(tag, after):
        ssem, rsem, bufs, _ = wg_started[tag]
        n = len(bufs) // 2
        lands = split_wait(ssem, rsem, bufs, after, weight_ici_copies, f"wg_ici_wait_{tag}")[n:]
        ssem, rsem, lands, tok = split_start(lands, weight_d2d_copies, 3 * n, f"wg_d2d_start_{tag}")
        return split_wait(ssem, rsem, lands, tok, weight_d2d_copies, f"wg_d2d_wait_{tag}")

    small_parts = [_pad_rows(p) for p in (c.reshape(D // DS, DS), norm_pre.reshape(L * NSUB, DS),
                                          norm_post.reshape(L * NSUB, DS), ssm_d.reshape(1, DS))]
    small_offs = [0]
    for p in small_parts:
        small_offs.append(small_offs[-1] + p.shape[0])
    small_all = allgather_small(jnp.concatenate(small_parts, axis=0) + wg_token, "ag_small").reshape(N_DEV, -1, DS)
    c_all = small_all[:, :D // DS].reshape(N_DEV, D)
    per_chip = small_all[0::2]
    o = small_offs[1]
    g_pre_full = jnp.moveaxis(per_chip[:, o:o + L * NSUB], 0, 1).reshape(L, NSUB, D)
    o = small_offs[2]
    g_post_full = jnp.moveaxis(per_chip[:, o:o + L * NSUB], 0, 1).reshape(L, NSUB, D)
    o = small_offs[3]
    d_full = jnp.moveaxis(per_chip[:, o:o + 1], 0, 1).reshape(1, D)

    NS = ada_w.shape[-1]
    ada_b_shard = lax.dynamic_slice_in_dim(ada_b, chip * NS, NS, axis=1).reshape(L, 1, NS)
    mod_part = ada_mod(c_all, ada_w, ada_b_shard, "ada_mod")
    mod_all = allgather_small(mod_part.reshape(L * N_DEV, NS), "ag_mod").reshape(N_DEV, L, N_DEV, NS)
    mod_mine = lax.dynamic_index_in_dim(mod_all[0::2], me, axis=2, keepdims=False)
    mod = jnp.moveaxis(mod_mine, 0, 1).reshape(L, NSUB, 3, D)

    def vec_of(l, s):
        return jnp.concatenate([mod[l, s], g_pre_full[l, s][None], g_post_full[l, s][None],
                                jnp.zeros((3, D), F32)], axis=0)

    GB = 8
    s5_args = (ssm_lam_re[0], ssm_lam_im[0], ssm_log_dt[0], ssm_b_re[0], ssm_b_im[0], ssm_c_re[0], ssm_c_im[0])
    (a_re, a_im, bblk_re, bblk_im, cblk_re, cblk_im), s5_vjp = jax.vjp(lambda *p: _s5_prepare(*p, GB), *s5_args)
    bre16, bim16 = bblk_re.astype(BF16), bblk_im.astype(BF16)
    cre16, cim16 = cblk_re.astype(BF16), cblk_im.astype(BF16)
    cre16_t, cim16_t = jnp.swapaxes(cre16, 1, 2), jnp.swapaxes(cim16, 1, 2)

    pool_w16 = pool_w[0].astype(BF16)
    sgu_bexp = jnp.broadcast_to(sgu_b[0][:, :, None], sgu_w[0].shape)

    saved = {}
    ffn_w = {}
    xcur = x0

    def ffn_weights(l, k, after):
        g = weights_of(f"ffn_{l}_{k}", after)
        ffn_w[(l, k)] = (g[0].reshape(N_CHIPS, D, FS), g[1].reshape(N_CHIPS, FR, D))
        return ffn_w[(l, k)]

    for l in range(L):
        v0 = vec_of(l, 0)
        win_g, wout_g = ffn_weights(l, 0, xcur if l else v0)
        out, h, u, f = ffn_fwd(xcur, v0, win_g, wout_g, f"ffn_fwd_{l}_0")
        saved[(l, 0)] = (xcur, v0, h, u, f)
        xcur = out
        v1 = vec_of(l, 1)
        mix_g = weights_of(f"mix_{l}", xcur)
        if l % 2 == 0:
            abin_g = mix_g[0].reshape(N_CHIPS, D, -1)
            about_g = mix_g[1].reshape(-1, D)
            h, z = pre_matmul(xcur, v1, abin_g, f"mixa_in_{l}")
            ycat = mixa_core_fwd(z, pool_w16, pool_scale, sgu_ln_g, sgu_ln_b, sgu_w[0], sgu_bexp, f"mixa_core_{l}")
            out, f = matmul_post(ycat, about_g, xcur, v1, f"mixa_out_{l}")
            saved[(l, 1)] = (xcur, v1, h, z, ycat, f)
        else:
            sin_g = mix_g[0].reshape(1, -1, D)
            glu_g = mix_g[1].reshape(N_CHIPS, D, -1)
            h, uu = pre_matmul(xcur, v1, sin_g, f"s5_in_{l}")
            us = _to_scan_order(uu)
            ys = s5_core_fwd(us, a_re, a_im, bre16, bim16, cre16_t, cim16_t, d_full, f"s5_core_{l}")
            yy = _from_scan_order(ys)
            out, f = s5_glu_fwd(yy, glu_g, xcur, v1, f"s5_glu_{l}")
            saved[(l, 1)] = (xcur, v1, h, us, yy, f)
        xcur = out
        v2 = vec_of(l, 2)
        win_g, wout_g = ffn_weights(l, 1, xcur)
        out, h, u, f = ffn_fwd(xcur, v2, win_g, wout_g, f"ffn_fwd_{l}_1")
        saved[(l, 2)] = (xcur, v2, h, u, f)
        xcur = out

    dcur, sq = loss_head(xcur, target, "loss_head")
    loss = lax.psum(sq[0, 0], ("x", "y", "c")) * (0.5 / D)

    accs = {}
    small_g = {}
    cidx = ci.reshape(1).astype(jnp.int32)
    chip_c = jnp.stack([chip, ci]).astype(jnp.int32)
    rs_pending = []

    def rs_begin(tag, items):
        arrs = [it[0] for it in items]
        recv_a = sibling_exchange_halves(arrs, f"rs_pair_exchange_{tag}")
        parts = [pair_sum(a, r, cidx, f"rs_pair_sum_{tag}_{i}") for i, (a, r) in enumerate(zip(arrs, recv_a))]
        lands = [lax.empty((3, 1) + p.shape[2:], BF16) for p in parts]
        started = split_start(parts + lands, chip_exchange_copies, 3 * len(parts), f"rs_chip_start_{tag}")
        rs_pending.append((tag, items, started))
        pin[0] = started[3][0, 0]

    pin = [None]

    def pinned(v):
        return v if pin[0] is None else v + pin[0]

    def ffn_back(l, s, k, dcur):
        xin, vv, h, u, f = saved[(l, s)]
        vv = pinned(vv)
        dx, df, du, act, acc = ffn_bwd(dcur, xin, f, u, vv, *ffn_w[(l, k)], f"ffn_bwd_{l}_{k}")
        accs[(l, s)] = acc
        g_win = tn_matmul(h, du, 512, f"ffn_dwin_{l}_{k}")
        g_wout = tn_matmul_cols(act, df, FS, D, f"ffn_dwout_{l}_{k}")
        rs_begin(f"ffn_{l}_{k}", [(g_win[None], "ffn_w_in", 2 * l + k, 2 * L),
                                  (g_wout.reshape(1, N_CHIPS, FR, D), "ffn_w_out", 2 * l + k, 2 * L)])
        return dx

    for l in reversed(range(L)):
        dcur = ffn_back(l, 2, 1, dcur)
        if l % 2 == 0:
            xin, vv, h, z, ycat, f = saved[(l, 1)]
            vv = pinned(vv)
            df, dact, acc_post = post_bwd_matmul(dcur, f, vv, about_g, f"mixa_out_bwd_{l}")
            g_about = tn_matmul_cols(ycat, df, 512, D, f"mixa_dwout_{l}")
            dz, dpw, dvecw, dsw, dsb = mixa_core_bwd(z, dact, pool_w16, pool_scale, sgu_ln_g, sgu_ln_b, sgu_w[0],
                                                     sgu_bexp, f"mixa_core_bwd_{l}")
            g_abin = tn_matmul_cols(h, dz, 512, abin_g.shape[-1], f"mixa_dwin_{l}")
            dcur, acc_pre = matmul_pre_bwd(dz, abin_g, xin, dcur, vv, f"mixa_in_bwd_{l}")
            accs[(l, 1)] = acc_pre + acc_post
            rs_begin(f"mixa_{l}", [(g_abin[None], "ab_w_in", 0, 1),
                                   (g_about.reshape(1, N_CHIPS, -1, D), "ab_w_out", 0, 1)])
            small_g.update(pool_w=dpw[None], pool_scale=dvecw[0:1], sgu_ln_g=dvecw[1:2], sgu_ln_b=dvecw[2:3],
                           sgu_w=dsw[None], sgu_b=dsb.T[None])
        else:
            xin, vv, h, us, yy, f = saved[(l, 1)]
            vv = pinned(vv)
            dy, dab, gact, acc_post = s5_glu_bwd(dcur, f, yy, vv, glu_g, f"s5_glu_bwd_{l}")
            g_glu = tn_matmul_cols(gact, dab, 512, glu_g.shape[-1], f"s5_dwglu_{l}")
            dys = _to_scan_order(dy)
            dus, dbre, dbim, dcre_t, dcim_t, dar, dai, dd = s5_core_bwd(
                us, dys, a_re, a_im, bre16, bim16, cre16, cim16, d_full, f"s5_core_bwd_{l}")
            du = _from_scan_order(dus).astype(BF16)
            g_sin = tn_matmul_cols(h, du, 512, D, f"s5_dwin_{l}")
            dcur, acc_pre = matmul_pre_bwd(du, sin_g, xin, dcur, vv, f"s5_in_bwd_{l}")
            accs[(l, 1)] = acc_pre + acc_post
            rs_begin(f"s5_{l}", [(g_sin.reshape(1, N_CHIPS, -1, D), "ssm_w_in", 0, 1), (g_glu[None], "ssm_w_glu", 0, 1)])
            s5_grads = s5_vjp((dar, dai, dbre, dbim, jnp.swapaxes(dcre_t, 1, 2), jnp.swapaxes(dcim_t, 1, 2)))
            small_g.update(ssm_lam_re=s5_grads[0][None], ssm_lam_im=s5_grads[1][None], ssm_log_dt=s5_grads[2][None],
                           ssm_b_re=s5_grads[3][None], ssm_b_im=s5_grads[4][None], ssm_c_re=s5_grads[5][None],
                           ssm_c_im=s5_grads[6][None])
            dd_mine = dd
        dcur = ffn_back(l, 0, 0, dcur)
    grad_x = dcur[None]

    acc_all = jnp.stack([jnp.stack([accs[(l, s)] for s in range(NSUB)]) for l in range(L)])
    dmod_mine = acc_all[:, :, A_SHIFT:A_GATE + 1].reshape(L, NSUB * 3 * D)
    dgpre_mine = acc_all[:, :, A_GPRE]
    dgpost_mine = acc_all[:, :, A_GPOST]
    small_names = ["pool_w", "pool_scale", "sgu_ln_g", "sgu_ln_b", "sgu_w", "sgu_b", "ssm_lam_re", "ssm_lam_im",
                   "ssm_b_re", "ssm_b_im", "ssm_c_re", "ssm_c_im", "ssm_log_dt"]
    packed, offs = _pack_rows([dmod_mine, dgpre_mine, dgpost_mine, dd_mine] + [small_g[n] for n in small_names], D)
    rows = packed.shape[0]
    packed_all = allgather_small(pinned(packed).astype(BF16), "ag_grads").reshape(N_DEV, rows, D)
    summed = _unpack_rows(sum_leading(packed_all, "sum_small"), offs)
    grads = {n: g for n, g in zip(small_names, summed[4:])}
    grads["ada_b"] = summed[0]
    grads["norm_pre"] = lax.dynamic_slice_in_dim(summed[1], chip * DS, DS, axis=2)
    grads["norm_post"] = lax.dynamic_slice_in_dim(summed[2], chip * DS, DS, axis=2)
    grads["ssm_d"] = lax.dynamic_slice_in_dim(summed[3], chip * DS, DS, axis=1)
    dmod_all = jnp.moveaxis(packed_all[:, :L * NSUB * 3].reshape(N_DEV, L, NSUB * 3 * D), 0, 1)
    dmod_shard = lax.dynamic_slice_in_dim(dmod_all, chip * NS, NS, axis=2)
    grads["ada_w"] = ada_grad(c_all, dmod_shard, "ada_grad")

    big_names = ["ffn_w_in", "ffn_w_out", "ab_w_in", "ab_w_out", "ssm_w_in", "ssm_w_glu"]
    deltas, new_m, new_v = {}, {}, {}

    def update(n):
        grads[n] = grads[n].reshape(weights[n].shape)
        deltas[n], new_m[n], new_v[n] = adamw_nd(weights[n], grads[n], m_in[n], v_in[n], f"adamw_{n}")

    for n in names:
        if n not in big_names:
            update(n)

    fin = {}
    after = deltas["ada_w"]
    for tag, items, (ssem, rsem, bufs, _) in rs_pending:
        bufs = split_wait(ssem, rsem, bufs, after, chip_exchange_copies, f"rs_chip_wait_{tag}")
        n = len(items)
        for i, (it, p, r) in enumerate(zip(items, bufs[:n], bufs[n:])):
            fin[it[1]] = chip_sum(p, r, chip_c, f"rs_chip_sum_{tag}_{i}", dest=fin.get(it[1]), slot=it[2],
                                  n_slots=it[3])
    filled = sibling_fill_halves([fin[n] for n in big_names], "rs_fill")
    for n, g in zip(big_names, filled):
        grads[n] = g
        update(n)

    return (loss, grad_x, *[grads[n] for n in names], *[deltas[n] for n in names],
            *[new_m[n] for n in names], *[new_v[n] for n in names])
```

```python
import math

import jax
import jax.numpy as jnp
from jax import lax
from jax.experimental import pallas as pl
from jax.experimental.pallas import tpu as pltpu

F32 = jnp.float32
BF16 = jnp.bfloat16
EPS = 1e-6
MESH_T = pl.DeviceIdType.MESH
VMEM_LIMIT_BYTES = 56 * 1024 * 1024
N_CHIPS = 4
N_DEV = 8
POOL_WINDOWS = (2, 4, 8, 16)
CHUNK = 128
SCAN_LANES = 8
ADAM_LR = 0.001
ADAM_B1 = 0.9
ADAM_B2 = 0.999
ADAM_EPS = 1e-08
ADAM_WD = 0.01
ADAM_STEP = 10
GELU_C = math.sqrt(2.0 / math.pi)
GELU_K = 0.044715

V_SHIFT, V_SCALE, V_GATE, V_GPRE, V_GPOST = 0, 1, 2, 3, 4
A_SHIFT, A_SCALE, A_GATE, A_GPRE, A_GPOST = 0, 1, 2, 3, 4

ANY = pl.BlockSpec(memory_space=pl.ANY)


def _cp(sem=None):
    if sem is None:
        return pltpu.CompilerParams(vmem_limit_bytes=VMEM_LIMIT_BYTES)
    return pltpu.CompilerParams(vmem_limit_bytes=VMEM_LIMIT_BYTES, dimension_semantics=sem)


def _dot(a, b):
    return jnp.dot(a, b, preferred_element_type=F32)


def _dot_nt(a, b):
    return lax.dot_general(a, b, (((1,), (1,)), ((), ())), preferred_element_type=F32)


def _dot_tn(a, b):
    return lax.dot_general(a, b, (((0,), (0,)), ((), ())), preferred_element_type=F32)


def _sigmoid(x):
    return 1.0 / (1.0 + jnp.exp(-x))


def _gelu(x):
    return 0.5 * x * (1.0 + jnp.tanh(GELU_C * (x + GELU_K * x * x * x)))


def _gelu_grad(x):
    t = jnp.tanh(GELU_C * (x + GELU_K * x * x * x))
    return 0.5 * (1.0 + t) + 0.5 * x * (1.0 - t * t) * GELU_C * (1.0 + 3.0 * GELU_K * x * x)


def _rowsum(v):
    return jnp.sum(v, axis=0, keepdims=True)


def _lanemean(v):
    return jnp.mean(v, axis=-1, keepdims=True)


def _row(ref, i):
    return ref[pl.ds(i, 1), :]


def _pre_fwd(x, vec_ref):
    r = lax.rsqrt(_lanemean(x * x) + EPS)
    return (x * r) * _row(vec_ref, V_GPRE) * (1.0 + _row(vec_ref, V_SCALE)) + _row(vec_ref, V_SHIFT)


def _pre_bwd(x, dh, vec_ref):
    g = _row(vec_ref, V_GPRE)
    sc = 1.0 + _row(vec_ref, V_SCALE)
    r = lax.rsqrt(_lanemean(x * x) + EPS)
    xn = x * r
    dhx = dh * xn
    t2 = dh * (g * sc)
    dx = r * (t2 - xn * _lanemean(t2 * xn))
    return dx, _rowsum(dh), _rowsum(dhx * g), _rowsum(dhx * sc)


def _post_fwd(x, f, vec_ref, rw):
    q = lax.rsqrt(_lanemean(f * f) + EPS)
    return x + (rw * _row(vec_ref, V_GATE)) * (f * q * _row(vec_ref, V_GPOST))


def _post_bwd(dout, f, vec_ref, rw):
    gp = _row(vec_ref, V_GPOST)
    gate = _row(vec_ref, V_GATE)
    q = lax.rsqrt(_lanemean(f * f) + EPS)
    fhat = f * q
    dgate = _rowsum(dout * (rw * fhat * gp))
    dy = dout * (rw * gate)
    dgpost = _rowsum(dy * fhat)
    t = dy * gp
    df = q * (t - fhat * _lanemean(t * fhat))
    return df, dgate, dgpost


def _acc_add(acc_ref, row, v):
    acc_ref[pl.ds(row, 1), :] += v


def ffn_fwd(x, vec, win_g, wout_g, name, tm=256):
    S, D = x.shape
    FS = win_g.shape[-1]
    FR = wout_g.shape[-2]

    def body(x_ref, vec_ref, win_hbm, wout_hbm, out_ref, h_ref, u_ref, f_ref, win_s, wout_s, sem):
        @pl.when(pl.program_id(0) == 0)
        def _():
            cps = [pltpu.make_async_copy(win_hbm.at[j], win_s.at[j], sem.at[j]) for j in range(N_CHIPS)]
            cps += [pltpu.make_async_copy(wout_hbm.at[j], wout_s.at[pl.ds(j * FR, FR), :], sem.at[N_CHIPS + j])
                    for j in range(N_CHIPS)]
            for cp in cps:
                cp.start()
            for cp in cps:
                cp.wait()

        xv = x_ref[...]
        h = _pre_fwd(xv, vec_ref).astype(BF16)
        h_ref[...] = h
        f = jnp.zeros((tm, D), F32)
        for j in range(2):
            a = _dot(h, win_s[j])
            b = _dot(h, win_s[2 + j])
            u_ref[j] = a.astype(BF16)
            u_ref[2 + j] = b.astype(BF16)
            act = (a * _sigmoid(a) * b).astype(BF16)
            f = f + _dot(act, wout_s[pl.ds(j * FS, FS), :])
        f_ref[...] = f
        out_ref[...] = _post_fwd(xv, f, vec_ref, 0.5)

    row = pl.BlockSpec((tm, D), lambda i: (i, 0))
    return pl.pallas_call(
        body, name=name, grid=(S // tm,),
        in_specs=[row, pl.BlockSpec((8, D), lambda i: (0, 0)), ANY, ANY],
        out_specs=[row, row, pl.BlockSpec((N_CHIPS, tm, FS), lambda i: (0, i, 0)), row],
        out_shape=[jax.ShapeDtypeStruct((S, D), F32), jax.ShapeDtypeStruct((S, D), BF16),
                   jax.ShapeDtypeStruct((N_CHIPS, S, FS), BF16), jax.ShapeDtypeStruct((S, D), F32)],
        scratch_shapes=[pltpu.VMEM((N_CHIPS, D, FS), BF16), pltpu.VMEM((N_CHIPS * FR, D), BF16),
                        pltpu.SemaphoreType.DMA((2 * N_CHIPS,))],
        compiler_params=_cp(("arbitrary",)),
    )(x, vec, win_g, wout_g)


def ffn_bwd(dout, x, f, u, vec, win_g, wout_g, name, tm=256):
    S, D = x.shape
    FS = win_g.shape[-1]
    FR = wout_g.shape[-2]

    def body(dout_ref, x_ref, f_ref, u_ref, vec_ref, win_hbm, wout_hbm,
             dx_ref, df_ref, du_ref, act_ref, acc_ref, win_s, wout_s, sem):
        @pl.when(pl.program_id(0) == 0)
        def _():
            cps = [pltpu.make_async_copy(win_hbm.at[j], win_s.at[j], sem.at[j]) for j in range(N_CHIPS)]
            cps += [pltpu.make_async_copy(wout_hbm.at[j], wout_s.at[pl.ds(j * FR, FR), :], sem.at[N_CHIPS + j])
                    for j in range(N_CHIPS)]
            for cp in cps:
                cp.start()
            acc_ref[...] = jnp.zeros_like(acc_ref)
            for cp in cps:
                cp.wait()

        dout_v = dout_ref[...]
        df, dgate, dgpost = _post_bwd(dout_v, f_ref[...], vec_ref, 0.5)
        dfb = df.astype(BF16)
        df_ref[...] = dfb
        dh = jnp.zeros((tm, D), F32)
        for j in range(2):
            a = u_ref[j].astype(F32)
            b = u_ref[2 + j].astype(F32)
            sig = _sigmoid(a)
            sl = a * sig
            dact = _dot_nt(dfb, wout_s[pl.ds(j * FS, FS), :])
            da = (dact * b * (sig * (1.0 + a * (1.0 - sig)))).astype(BF16)
            db = (dact * sl).astype(BF16)
            du_ref[j] = da
            du_ref[2 + j] = db
            act_ref[:, j * FS:(j + 1) * FS] = (sl * b).astype(BF16)
            dh = dh + _dot_nt(da, win_s[j]) + _dot_nt(db, win_s[2 + j])
        dx, dshift, dscale, dgpre = _pre_bwd(x_ref[...], dh, vec_ref)
        dx_ref[...] = dout_v + dx
        _acc_add(acc_ref, A_SHIFT, dshift)
        _acc_add(acc_ref, A_SCALE, dscale)
        _acc_add(acc_ref, A_GATE, dgate)
        _acc_add(acc_ref, A_GPRE, dgpre)
        _acc_add(acc_ref, A_GPOST, dgpost)

    row = pl.BlockSpec((tm, D), lambda i: (i, 0))
    ublk = pl.BlockSpec((N_CHIPS, tm, FS), lambda i: (0, i, 0))
    const = pl.BlockSpec((8, D), lambda i: (0, 0))
    return pl.pallas_call(
        body, name=name, grid=(S // tm,),
        in_specs=[row, row, row, ublk, const, ANY, ANY],
        out_specs=[row, row, ublk, pl.BlockSpec((tm, 2 * FS), lambda i: (i, 0)), const],
        out_shape=[jax.ShapeDtypeStruct((S, D), F32), jax.ShapeDtypeStruct((S, D), BF16),
                   jax.ShapeDtypeStruct((N_CHIPS, S, FS), BF16), jax.ShapeDtypeStruct((S, 2 * FS), BF16),
                   jax.ShapeDtypeStruct((8, D), F32)],
        scratch_shapes=[pltpu.VMEM((N_CHIPS, D, FS), BF16), pltpu.VMEM((N_CHIPS * FR, D), BF16),
                        pltpu.SemaphoreType.DMA((2 * N_CHIPS,))],
        compiler_params=_cp(("arbitrary",)),
    )(dout, x, f, u, vec, win_g, wout_g)


def tn_matmul(a, b, tk, name, dep=None):
    S, K = a.shape
    nb, _, tn = b.shape
    extra = [] if dep is None else [dep]

    def body(a_ref, b_ref, *rest):
        rest[-1][...] = _dot_tn(a_ref[...], b_ref[...]).astype(BF16)

    return pl.pallas_call(
        body, name=name, grid=(K // tk, nb),
        in_specs=[pl.BlockSpec((S, tk), lambda i, j: (0, i)), pl.BlockSpec((None, S, tn), lambda i, j: (j, 0, 0))]
        + [ANY] * len(extra),
        out_specs=pl.BlockSpec((None, tk, tn), lambda i, j: (j, i, 0)),
        out_shape=jax.ShapeDtypeStruct((nb, K, tn), BF16),
        compiler_params=_cp(("arbitrary", "arbitrary")),
    )(a, b, *extra)


def tn_matmul_cols(a, b, tk, tn, name):
    S, K = a.shape
    N = b.shape[1]

    def body(a_ref, b_ref, o_ref):
        o_ref[...] = _dot_tn(a_ref[...], b_ref[...]).astype(BF16)

    return pl.pallas_call(
        body, name=name, grid=(K // tk, N // tn),
        in_specs=[pl.BlockSpec((S, tk), lambda i, j: (0, i)), pl.BlockSpec((S, tn), lambda i, j: (0, j))],
        out_specs=pl.BlockSpec((None, tk, tn), lambda i, j: (j, i, 0)),
        out_shape=jax.ShapeDtypeStruct((N // tn, K, tn), BF16),
        compiler_params=_cp(("arbitrary", "arbitrary")),
    )(a, b)


def pre_matmul(x, vec, w3, name, tm=256):
    S, D = x.shape
    nj, _, Nj = w3.shape

    def body(x_ref, vec_ref, w_ref, h_ref, z_ref):
        h = _pre_fwd(x_ref[...], vec_ref).astype(BF16)
        h_ref[...] = h
        for j in range(nj):
            z_ref[:, j * Nj:(j + 1) * Nj] = _dot(h, w_ref[j])

    row = pl.BlockSpec((tm, D), lambda i: (i, 0))
    return pl.pallas_call(
        body, name=name, grid=(S // tm,),
        in_specs=[row, pl.BlockSpec((8, D), lambda i: (0, 0)), pl.BlockSpec((nj, D, Nj), lambda i: (0, 0, 0))],
        out_specs=[row, pl.BlockSpec((tm, nj * Nj), lambda i: (i, 0))],
        out_shape=[jax.ShapeDtypeStruct((S, D), BF16), jax.ShapeDtypeStruct((S, nj * Nj), F32)],
        compiler_params=_cp(("arbitrary",)),
    )(x, vec, w3)


def matmul_pre_bwd(dz, w3, x, dres, vec, name, tm=256):
    S, D = x.shape
    nj, _, Nj = w3.shape

    def body(dz_ref, w_ref, x_ref, dres_ref, vec_ref, dx_ref, acc_ref):
        @pl.when(pl.program_id(0) == 0)
        def _():
            acc_ref[...] = jnp.zeros_like(acc_ref)

        dh = jnp.zeros((tm, D), F32)
        for j in range(nj):
            dh = dh + _dot_nt(dz_ref[:, j * Nj:(j + 1) * Nj], w_ref[j])
        dx, dshift, dscale, dgpre = _pre_bwd(x_ref[...], dh, vec_ref)
        dx_ref[...] = dres_ref[...] + dx
        _acc_add(acc_ref, A_SHIFT, dshift)
        _acc_add(acc_ref, A_SCALE, dscale)
        _acc_add(acc_ref, A_GPRE, dgpre)

    row = pl.BlockSpec((tm, D), lambda i: (i, 0))
    const = pl.BlockSpec((8, D), lambda i: (0, 0))
    return pl.pallas_call(
        body, name=name, grid=(S // tm,),
        in_specs=[pl.BlockSpec((tm, nj * Nj), lambda i: (i, 0)), pl.BlockSpec((nj, D, Nj), lambda i: (0, 0, 0)),
                  row, row, const],
        out_specs=[row, const],
        out_shape=[jax.ShapeDtypeStruct((S, D), F32), jax.ShapeDtypeStruct((8, D), F32)],
        compiler_params=_cp(("arbitrary",)),
    )(dz, w3, x, dres, vec)


def matmul_post(act, w, x, vec, name, tm=256):
    S, D = x.shape
    K = act.shape[1]

    def body(act_ref, w_ref, x_ref, vec_ref, out_ref, f_ref):
        f = _dot(act_ref[...], w_ref[...])
        f_ref[...] = f
        out_ref[...] = _post_fwd(x_ref[...], f, vec_ref, 1.0)

    row = pl.BlockSpec((tm, D), lambda i: (i, 0))
    return pl.pallas_call(
        body, name=name, grid=(S // tm,),
        in_specs=[pl.BlockSpec((tm, K), lambda i: (i, 0)), pl.BlockSpec((K, D), lambda i: (0, 0)), row,
                  pl.BlockSpec((8, D), lambda i: (0, 0))],
        out_specs=[row, row],
        out_shape=[jax.ShapeDtypeStruct((S, D), F32), jax.ShapeDtypeStruct((S, D), F32)],
        compiler_params=_cp(("arbitrary",)),
    )(act, w, x, vec)


def post_bwd_matmul(dout, f, vec, w, name, tm=256):
    S, D = dout.shape
    K = w.shape[0]

    def body(dout_ref, f_ref, vec_ref, w_ref, df_ref, dact_ref, acc_ref):
        @pl.when(pl.program_id(0) == 0)
        def _():
            acc_ref[...] = jnp.zeros_like(acc_ref)

        df, dgate, dgpost = _post_bwd(dout_ref[...], f_ref[...], vec_ref, 1.0)
        dfb = df.astype(BF16)
        df_ref[...] = dfb
        dact_ref[...] = _dot_nt(dfb, w_ref[...])
        _acc_add(acc_ref, A_GATE, dgate)
        _acc_add(acc_ref, A_GPOST, dgpost)

    row = pl.BlockSpec((tm, D), lambda i: (i, 0))
    const = pl.BlockSpec((8, D), lambda i: (0, 0))
    return pl.pallas_call(
        body, name=name, grid=(S // tm,),
        in_specs=[row, row, const, pl.BlockSpec((K, D), lambda i: (0, 0))],
        out_specs=[row, pl.BlockSpec((tm, K), lambda i: (i, 0)), const],
        out_shape=[jax.ShapeDtypeStruct((S, D), BF16), jax.ShapeDtypeStruct((S, K), F32),
                   jax.ShapeDtypeStruct((8, D), F32)],
        compiler_params=_cp(("arbitrary",)),
    )(dout, f, vec, w)


def _band(w, transposed, prev):
    r = lax.broadcasted_iota(jnp.int32, (CHUNK, CHUNK), 1 if transposed else 0)
    c = lax.broadcasted_iota(jnp.int32, (CHUNK, CHUNK), 0 if transposed else 1)
    d = r - c
    m = (d + CHUNK < w) if prev else ((d >= 0) & (d < w))
    return jnp.where(m, 1.0, 0.0).astype(BF16)


def _split_hi_lo(a):
    hi = a.astype(BF16)
    lo = (a - hi.astype(F32)).astype(BF16)
    return hi, lo


def _pool_diff(a, ap, g, denom):
    w = POOL_WINDOWS[g]
    a_hi, a_lo = _split_hi_lo(a)
    p_hi, p_lo = _split_hi_lo(ap)
    mc = _band(w, False, False)
    mp = _band(w, False, True)
    win = _dot(mc, a_hi) + _dot(mc, a_lo) + _dot(mp, p_hi) + _dot(mp, p_lo)
    return win / denom - a


def _sgu_norm(v, lg, lb):
    mu = _lanemean(v)
    xc = v - mu
    rstd = lax.rsqrt(_lanemean(xc * xc) + EPS)
    xhat = xc * rstd
    return xhat, rstd, xhat * lg + lb


def _tril_mask():
    r = lax.broadcasted_iota(jnp.int32, (CHUNK, CHUNK), 0)
    c = lax.broadcasted_iota(jnp.int32, (CHUNK, CHUNK), 1)
    return r >= c


def _positions(i, w):
    r = lax.broadcasted_iota(jnp.int32, (CHUNK, 128), 0)
    pos = (i * CHUNK + r + 1).astype(F32)
    return jnp.minimum(pos, float(w))


def mixa_core_fwd(z, pool_w, pool_scale, ln_g, ln_b, sgu_w, sgu_bexp, name):
    S = z.shape[0]
    W = z.shape[1] // 3
    G = len(POOL_WINDOWS)
    GD = W // G

    def body(zc_ref, zp_ref, pw_ref, ps_ref, lg_ref, lb_ref, sw_ref, sb_ref, y_ref):
        i = pl.program_id(0)
        has_prev = jnp.where(i > 0, 1.0, 0.0)
        for g in range(G):
            sl = slice(g * GD, (g + 1) * GD)
            a = zc_ref[:, sl]
            ap = zp_ref[:, sl] * has_prev
            d = _pool_diff(a, ap, g, _positions(i, POOL_WINDOWS[g])).astype(BF16)
            y_ref[:, sl] = (_dot(d, pw_ref[g]) * ps_ref[:, sl]).astype(BF16)
        tril = _tril_mask()
        for hh in range(G):
            u = _gelu(zc_ref[:, W + hh * GD:W + (hh + 1) * GD])
            v = _gelu(zc_ref[:, 2 * W + hh * GD:2 * W + (hh + 1) * GD])
            sl = slice(hh * GD, (hh + 1) * GD)
            _, _, vn = _sgu_norm(v, lg_ref[:, sl], lb_ref[:, sl])
            wm = jnp.where(tril, sw_ref[hh], 0.0).astype(BF16)
            s = _dot(wm, vn.astype(BF16)) + sb_ref[hh]
            y_ref[:, W + hh * GD:W + (hh + 1) * GD] = (u * s).astype(BF16)

    vecw = pl.BlockSpec((1, W), lambda i: (0, 0))
    mats = pl.BlockSpec((G, GD, GD), lambda i: (0, 0, 0))
    return pl.pallas_call(
        body, name=name, grid=(S // CHUNK,),
        in_specs=[pl.BlockSpec((CHUNK, 3 * W), lambda i: (i, 0)),
                  pl.BlockSpec((CHUNK, W), lambda i: (jnp.maximum(i - 1, 0), 0)),
                  mats, vecw, vecw, vecw, mats, mats],
        out_specs=pl.BlockSpec((CHUNK, 2 * W), lambda i: (i, 0)),
        out_shape=jax.ShapeDtypeStruct((S, 2 * W), BF16),
        compiler_params=_cp(("arbitrary",)),
    )(z, z, pool_w, pool_scale, ln_g, ln_b, sgu_w, sgu_bexp)


def mixa_core_bwd(z, dy, pool_w, pool_scale, ln_g, ln_b, sgu_w, sgu_bexp, name):
    S = z.shape[0]
    W = z.shape[1] // 3
    G = len(POOL_WINDOWS)
    GD = W // G
    n_tiles = S // CHUNK

    def body(zc_ref, zp_ref, dyc_ref, dyn_ref, pw_ref, ps_ref, lg_ref, lb_ref, sw_ref, sb_ref,
             dz_ref, dpw_ref, dvec_ref, dsw_ref, dsb_ref):
        i = pl.program_id(0)

        @pl.when(i == 0)
        def _():
            dpw_ref[...] = jnp.zeros_like(dpw_ref)
            dvec_ref[...] = jnp.zeros_like(dvec_ref)
            dsw_ref[...] = jnp.zeros_like(dsw_ref)
            dsb_ref[...] = jnp.zeros_like(dsb_ref)

        has_prev = jnp.where(i > 0, 1.0, 0.0)
        has_next = jnp.where(i < n_tiles - 1, 1.0, 0.0)
        for g in range(G):
            w = POOL_WINDOWS[g]
            sl = slice(g * GD, (g + 1) * GD)
            a = zc_ref[:, sl]
            ap = zp_ref[:, sl] * has_prev
            den_c = _positions(i, w)
            den_n = _positions(i + 1, w)
            d = _pool_diff(a, ap, g, den_c).astype(BF16)
            ps = ps_ref[:, sl]
            pw = pw_ref[g]
            dyc = dyc_ref[:, sl]
            dvec_ref[pl.ds(0, 1), sl] += _rowsum(dyc * _dot(d, pw))
            dyp_c = (dyc * ps).astype(BF16)
            dyp_n = (dyn_ref[:, sl] * (ps * has_next)).astype(BF16)
            dpw_ref[g] += _dot_tn(d, dyp_c)
            dd_c = _dot_nt(dyp_c, pw)
            dd_n = _dot_nt(dyp_n, pw)
            da = (_dot(_band(w, True, False), (dd_c / den_c).astype(BF16))
                  + _dot(_band(w, True, True), (dd_n / den_n).astype(BF16)) - dd_c)
            dz_ref[:, sl] = da.astype(BF16)
        tril = _tril_mask()
        for hh in range(G):
            sl = slice(hh * GD, (hh + 1) * GD)
            zu = zc_ref[:, W + hh * GD:W + (hh + 1) * GD]
            zv = zc_ref[:, 2 * W + hh * GD:2 * W + (hh + 1) * GD]
            u = _gelu(zu)
            v = _gelu(zv)
            lg = lg_ref[:, sl]
            xhat, rstd, vn = _sgu_norm(v, lg, lb_ref[:, sl])
            vnb = vn.astype(BF16)
            wm = jnp.where(tril, sw_ref[hh], 0.0).astype(BF16)
            s = _dot(wm, vnb) + sb_ref[hh]
            dyb = dyc_ref[:, W + hh * GD:W + (hh + 1) * GD]
            du = dyb * s
            ds = dyb * u
            dsb_ref[:, hh:hh + 1] += jnp.sum(ds, axis=1, keepdims=True)
            dsb16 = ds.astype(BF16)
            dsw_ref[hh] += jnp.where(tril, _dot_nt(dsb16, vnb), 0.0)
            dvn = _dot_tn(wm, dsb16)
            dvec_ref[pl.ds(1, 1), sl] += _rowsum(dvn * xhat)
            dvec_ref[pl.ds(2, 1), sl] += _rowsum(dvn)
            dxh = dvn * lg
            dv = rstd * (dxh - _lanemean(dxh) - xhat * _lanemean(dxh * xhat))
            dz_ref[:, W + hh * GD:W + (hh + 1) * GD] = (du * _gelu_grad(zu)).astype(BF16)
            dz_ref[:, 2 * W + hh * GD:2 * W + (hh + 1) * GD] = (dv * _gelu_grad(zv)).astype(BF16)

    vecw = pl.BlockSpec((1, W), lambda i: (0, 0))
    mats = pl.BlockSpec((G, GD, GD), lambda i: (0, 0, 0))
    return pl.pallas_call(
        body, name=name, grid=(n_tiles,),
        in_specs=[pl.BlockSpec((CHUNK, 3 * W), lambda i: (i, 0)),
                  pl.BlockSpec((CHUNK, W), lambda i: (jnp.maximum(i - 1, 0), 0)),
                  pl.BlockSpec((CHUNK, 2 * W), lambda i: (i, 0)),
                  pl.BlockSpec((CHUNK, W), lambda i: (jnp.minimum(i + 1, n_tiles - 1), 0)),
                  mats, vecw, vecw, vecw, mats, mats],
        out_specs=[pl.BlockSpec((CHUNK, 3 * W), lambda i: (i, 0)), mats,
                   pl.BlockSpec((8, W), lambda i: (0, 0)), mats, pl.BlockSpec((CHUNK, G), lambda i: (0, 0))],
        out_shape=[jax.ShapeDtypeStruct((S, 3 * W), BF16), jax.ShapeDtypeStruct((G, GD, GD), F32),
                   jax.ShapeDtypeStruct((8, W), F32), jax.ShapeDtypeStruct((G, CHUNK, CHUNK), F32),
                   jax.ShapeDtypeStruct((CHUNK, G), F32)],
        compiler_params=_cp(("arbitrary",)),
    )(z, z, dy, dy, pool_w, pool_scale, ln_g, ln_b, sgu_w, sgu_bexp)


def _cmul(ar, ai, br, bi):
    return ar * br - ai * bi, ar * bi + ai * br


def _cpow(ar, ai, n):
    rr, ri = None, None
    br, bi = ar, ai
    while n:
        if n & 1:
            rr, ri = (br, bi) if rr is None else _cmul(rr, ri, br, bi)
        n >>= 1
        if n:
            br, bi = _cmul(br, bi, br, bi)
    return rr, ri


def _seg_rows(k):
    return pl.ds(pl.multiple_of(k * SCAN_LANES, SCAN_LANES), SCAN_LANES)


def _scan_fwd(xre, xim, carry, ar, ai, K):
    P = xre.shape[1]

    def local(k, c):
        pr, pi = c
        rows = _seg_rows(k)
        nr = ar * pr - ai * pi + xre[rows, :]
        ni = ar * pi + ai * pr + xim[rows, :]
        xre[rows, :] = nr
        xim[rows, :] = ni
        return nr, ni

    er, ei = lax.fori_loop(1, K, local, (xre[pl.ds(0, SCAN_LANES), :], xim[pl.ds(0, SCAN_LANES), :]))
    akr, aki = _cpow(ar, ai, K)
    cr = jnp.zeros((1, P), F32)
    ci = jnp.zeros((1, P), F32)
    carry[pl.ds(0, 1), :] = cr
    carry[pl.ds(SCAN_LANES, 1), :] = ci
    for j in range(1, SCAN_LANES):
        tr, ti = _cmul(akr, aki, cr, ci)
        cr = er[j - 1:j, :] + tr
        ci = ei[j - 1:j, :] + ti
        carry[pl.ds(j, 1), :] = cr
        carry[pl.ds(SCAN_LANES + j, 1), :] = ci
    cmr = carry[pl.ds(0, SCAN_LANES), :]
    cmi = carry[pl.ds(SCAN_LANES, SCAN_LANES), :]

    def fix(k, c):
        pr, pi = c
        rows = _seg_rows(k)
        tr, ti = _cmul(pr, pi, cmr, cmi)
        xre[rows, :] += tr
        xim[rows, :] += ti
        return _cmul(pr, pi, ar, ai)

    lax.fori_loop(0, K, fix, (jnp.broadcast_to(ar, (SCAN_LANES, P)), jnp.broadcast_to(ai, (SCAN_LANES, P))))


def s5_core_fwd(u, ar, ai, bre, bim, cre_t, cim_t, dskip, name):
    S, D = u.shape
    nblk, UB, PB = bre.shape
    K = S // SCAN_LANES

    def body(u_ref, ar_ref, ai_ref, bre_ref, bim_ref, cre_ref, cim_ref, d_ref, y_ref, xre, xim, carry):
        uv = u_ref[...]
        ub = uv.astype(BF16)
        xre[...] = _dot(ub, bre_ref[...])
        xim[...] = _dot(ub, bim_ref[...])
        _scan_fwd(xre, xim, carry, ar_ref[...], ai_ref[...], K)
        y_ref[...] = (_dot(xre[...].astype(BF16), cre_ref[...]) - _dot(xim[...].astype(BF16), cim_ref[...])
                      + d_ref[...] * uv)

    ucol = pl.BlockSpec((S, UB), lambda i: (0, i))
    pvec = pl.BlockSpec((None, 1, PB), lambda i: (i, 0, 0))
    bmat = pl.BlockSpec((None, UB, PB), lambda i: (i, 0, 0))
    cmat = pl.BlockSpec((None, PB, UB), lambda i: (i, 0, 0))
    return pl.pallas_call(
        body, name=name, grid=(nblk,),
        in_specs=[ucol, pvec, pvec, bmat, bmat, cmat, cmat, pl.BlockSpec((1, UB), lambda i: (0, i))],
        out_specs=ucol,
        out_shape=jax.ShapeDtypeStruct((S, D), F32),
        scratch_shapes=[pltpu.VMEM((S, PB), F32), pltpu.VMEM((S, PB), F32), pltpu.VMEM((2 * SCAN_LANES, PB), F32)],
        compiler_params=_cp(("arbitrary",)),
    )(u, ar, ai, bre, bim, cre_t, cim_t, dskip)


def s5_core_bwd(u, dy, ar, ai, bre, bim, cre, cim, dskip, name):
    S, D = u.shape
    nblk, UB, PB = bre.shape
    K = S // SCAN_LANES

    def body(u_ref, dy_ref, ar_ref, ai_ref, bre_ref, bim_ref, cre_ref, cim_ref, d_ref,
             du_ref, dbre_ref, dbim_ref, dcre_ref, dcim_ref, dar_ref, dai_ref, dd_ref,
             xre, xim, gre, gim, carry, carry_b):
        ar = ar_ref[...]
        ai = ai_ref[...]
        uv = u_ref[...]
        ub = uv.astype(BF16)
        dyv = dy_ref[...]
        dyb = dyv.astype(BF16)
        xre[...] = _dot(ub, bre_ref[...])
        xim[...] = _dot(ub, bim_ref[...])
        _scan_fwd(xre, xim, carry, ar, ai, K)
        dcre_ref[...] = _dot_tn(xre[...].astype(BF16), dyb)
        dcim_ref[...] = -_dot_tn(xim[...].astype(BF16), dyb)
        gre[...] = _dot(dyb, cre_ref[...])
        gim[...] = -_dot(dyb, cim_ref[...])

        def local(s, c):
            k = K - 2 - s
            nr, ni = c
            rows = _seg_rows(k)
            tr = gre[rows, :] + ar * nr + ai * ni
            ti = gim[rows, :] + ar * ni - ai * nr
            gre[rows, :] = tr
            gim[rows, :] = ti
            return tr, ti

        last = _seg_rows(K - 1)
        fr, fi = lax.fori_loop(0, K - 1, local, (gre[last, :], gim[last, :]))
        akr, aki = _cpow(ar, -ai, K)
        cr = jnp.zeros((1, PB), F32)
        ci = jnp.zeros((1, PB), F32)
        carry_b[pl.ds(SCAN_LANES - 1, 1), :] = cr
        carry_b[pl.ds(2 * SCAN_LANES - 1, 1), :] = ci
        for j in range(SCAN_LANES - 2, -1, -1):
            tr, ti = _cmul(akr, aki, cr, ci)
            cr = fr[j + 1:j + 2, :] + tr
            ci = fi[j + 1:j + 2, :] + ti
            carry_b[pl.ds(j, 1), :] = cr
            carry_b[pl.ds(SCAN_LANES + j, 1), :] = ci
        cbr = carry_b[pl.ds(0, SCAN_LANES), :]
        cbi = carry_b[pl.ds(SCAN_LANES, SCAN_LANES), :]

        def fix(s, c):
            k = K - 1 - s
            pr, pi, dar, dai = c
            rows = _seg_rows(k)
            tr, ti = _cmul(pr, pi, cbr, cbi)
            g_r = gre[rows, :] + tr
            g_i = gim[rows, :] + ti
            gre[rows, :] = g_r
            gim[rows, :] = g_i
            prev = _seg_rows(jnp.maximum(k - 1, 0))
            first = k == 0
            xr = jnp.where(first, carry[pl.ds(0, SCAN_LANES), :], xre[prev, :])
            xi = jnp.where(first, carry[pl.ds(SCAN_LANES, SCAN_LANES), :], xim[prev, :])
            dar = dar + g_r * xr + g_i * xi
            dai = dai + g_i * xr - g_r * xi
            nr, ni = _cmul(pr, pi, ar, -ai)
            return nr, ni, dar, dai

        z8 = jnp.zeros((SCAN_LANES, PB), F32)
        _, _, dar, dai = lax.fori_loop(
            0, K, fix, (jnp.broadcast_to(ar, (SCAN_LANES, PB)), jnp.broadcast_to(-ai, (SCAN_LANES, PB)), z8, z8))
        dar_ref[...] = _rowsum(dar)
        dai_ref[...] = _rowsum(dai)
        grb = gre[...].astype(BF16)
        gib = gim[...].astype(BF16)
        dbre_ref[...] = _dot_tn(ub, grb)
        dbim_ref[...] = _dot_tn(ub, gib)
        du_ref[...] = _dot_nt(grb, bre_ref[...]) + _dot_nt(gib, bim_ref[...]) + d_ref[...] * dyv
        dd_ref[...] = _rowsum(dyv * uv)

    ucol = pl.BlockSpec((S, UB), lambda i: (0, i))
    pvec = pl.BlockSpec((None, 1, PB), lambda i: (i, 0, 0))
    bmat = pl.BlockSpec((None, UB, PB), lambda i: (i, 0, 0))
    cmat = pl.BlockSpec((None, PB, UB), lambda i: (i, 0, 0))
    dvec = pl.BlockSpec((1, UB), lambda i: (0, i))
    return pl.pallas_call(
        body, name=name, grid=(nblk,),
        in_specs=[ucol, ucol, pvec, pvec, bmat, bmat, bmat, bmat, dvec],
        out_specs=[ucol, bmat, bmat, cmat, cmat, pvec, pvec, dvec],
        out_shape=[jax.ShapeDtypeStruct((S, D), F32),
                   jax.ShapeDtypeStruct((nblk, UB, PB), F32), jax.ShapeDtypeStruct((nblk, UB, PB), F32),
                   jax.ShapeDtypeStruct((nblk, PB, UB), F32), jax.ShapeDtypeStruct((nblk, PB, UB), F32),
                   jax.ShapeDtypeStruct((nblk, 1, PB), F32), jax.ShapeDtypeStruct((nblk, 1, PB), F32),
                   jax.ShapeDtypeStruct((1, D), F32)],
        scratch_shapes=[pltpu.VMEM((S, PB), F32), pltpu.VMEM((S, PB), F32), pltpu.VMEM((S, PB), F32),
                        pltpu.VMEM((S, PB), F32), pltpu.VMEM((2 * SCAN_LANES, PB), F32),
                        pltpu.VMEM((2 * SCAN_LANES, PB), F32)],
        compiler_params=_cp(("arbitrary",)),
    )(u, dy, ar, ai, bre, bim, cre, cim, dskip)


def s5_glu_fwd(y, wglu, x, vec, name, tm=256):
    S, D = x.shape
    NJ = wglu.shape[-1]

    def body(y_ref, w_ref, x_ref, vec_ref, out_ref, f_ref):
        g = _gelu(y_ref[...]).astype(BF16)
        f = jnp.concatenate([_dot(g, w_ref[j]) * _sigmoid(_dot(g, w_ref[2 + j])) for j in range(2)], axis=1)
        f_ref[...] = f
        out_ref[...] = _post_fwd(x_ref[...], f, vec_ref, 1.0)

    row = pl.BlockSpec((tm, D), lambda i: (i, 0))
    return pl.pallas_call(
        body, name=name, grid=(S // tm,),
        in_specs=[row, pl.BlockSpec((N_CHIPS, D, NJ), lambda i: (0, 0, 0)), row, pl.BlockSpec((8, D), lambda i: (0, 0))],
        out_specs=[row, row],
        out_shape=[jax.ShapeDtypeStruct((S, D), F32), jax.ShapeDtypeStruct((S, D), F32)],
        compiler_params=_cp(("arbitrary",)),
    )(y, wglu, x, vec)


def s5_glu_bwd(dout, f, y, vec, wglu, name, tm=256):
    S, D = dout.shape
    NJ = wglu.shape[-1]

    def body(dout_ref, f_ref, y_ref, vec_ref, w_ref, dy_ref, dab_ref, g_ref, acc_ref):
        @pl.when(pl.program_id(0) == 0)
        def _():
            acc_ref[...] = jnp.zeros_like(acc_ref)

        df, dgate, dgpost = _post_bwd(dout_ref[...], f_ref[...], vec_ref, 1.0)
        yv = y_ref[...]
        g = _gelu(yv).astype(BF16)
        g_ref[...] = g
        dg = jnp.zeros((tm, D), F32)
        for j in range(2):
            a = _dot(g, w_ref[j])
            sig = _sigmoid(_dot(g, w_ref[2 + j]))
            dfj = df[:, j * NJ:(j + 1) * NJ]
            da = (dfj * sig).astype(BF16)
            db = (dfj * a * sig * (1.0 - sig)).astype(BF16)
            dab_ref[:, j * NJ:(j + 1) * NJ] = da
            dab_ref[:, (2 + j) * NJ:(3 + j) * NJ] = db
            dg = dg + _dot_nt(da, w_ref[j]) + _dot_nt(db, w_ref[2 + j])
        dy_ref[...] = dg * _gelu_grad(yv)
        _acc_add(acc_ref, A_GATE, dgate)
        _acc_add(acc_ref, A_GPOST, dgpost)

    row = pl.BlockSpec((tm, D), lambda i: (i, 0))
    const = pl.BlockSpec((8, D), lambda i: (0, 0))
    return pl.pallas_call(
        body, name=name, grid=(S // tm,),
        in_specs=[row, row, row, const, pl.BlockSpec((N_CHIPS, D, NJ), lambda i: (0, 0, 0))],
        out_specs=[row, pl.BlockSpec((tm, N_CHIPS * NJ), lambda i: (i, 0)), row, const],
        out_shape=[jax.ShapeDtypeStruct((S, D), F32), jax.ShapeDtypeStruct((S, N_CHIPS * NJ), BF16),
                   jax.ShapeDtypeStruct((S, D), BF16), jax.ShapeDtypeStruct((8, D), F32)],
        compiler_params=_cp(("arbitrary",)),
    )(dout, f, y, vec, wglu)


def loss_head(y, target, name, tm=256):
    S, D = y.shape

    def body(y_ref, t_ref, d_ref, l_ref):
        @pl.when(pl.program_id(0) == 0)
        def _():
            l_ref[...] = jnp.zeros_like(l_ref)

        err = y_ref[...] - t_ref[...]
        d_ref[...] = err * (1.0 / D)
        l_ref[...] += jnp.sum(_rowsum(err * err), axis=1, keepdims=True)

    row = pl.BlockSpec((tm, D), lambda i: (i, 0))
    return pl.pallas_call(
        body, name=name, grid=(S // tm,),
        in_specs=[row, row],
        out_specs=[row, pl.BlockSpec((1, 1), lambda i: (0, 0))],
        out_shape=[jax.ShapeDtypeStruct((S, D), F32), jax.ShapeDtypeStruct((1, 1), F32)],
        compiler_params=_cp(("arbitrary",)),
    )(y, target)


def ada_mod(c_all, ada_w, ada_b_shard, name, tn=768):
    B, D = c_all.shape
    L, _, NS = ada_w.shape

    def body(c_ref, w_ref, b_ref, o_ref):
        cv = c_ref[...]
        cond = (cv * _sigmoid(cv)).astype(BF16)
        o_ref[...] = _dot(cond, w_ref[...].astype(BF16)) + b_ref[...]

    return pl.pallas_call(
        body, name=name, grid=(L, NS // tn),
        in_specs=[pl.BlockSpec((B, D), lambda l, j: (0, 0)), pl.BlockSpec((None, D, tn), lambda l, j: (l, 0, j)),
                  pl.BlockSpec((None, 1, tn), lambda l, j: (l, 0, j))],
        out_specs=pl.BlockSpec((None, B, tn), lambda l, j: (l, 0, j)),
        out_shape=jax.ShapeDtypeStruct((L, B, NS), F32),
        compiler_params=_cp(("arbitrary", "arbitrary")),
    )(c_all, ada_w, ada_b_shard)


def ada_grad(c_all, dmod, name, tn=768):
    B, D = c_all.shape
    L, _, NS = dmod.shape

    def body(c_ref, d_ref, o_ref):
        cv = c_ref[...]
        cond = (cv * _sigmoid(cv)).astype(BF16)
        o_ref[...] = _dot_tn(cond, d_ref[...].astype(BF16))

    return pl.pallas_call(
        body, name=name, grid=(L, NS // tn),
        in_specs=[pl.BlockSpec((B, D), lambda l, j: (0, 0)), pl.BlockSpec((None, B, tn), lambda l, j: (l, 0, j))],
        out_specs=pl.BlockSpec((None, D, tn), lambda l, j: (l, 0, j)),
        out_shape=jax.ShapeDtypeStruct((L, D, NS), F32),
        compiler_params=_cp(("arbitrary", "arbitrary")),
    )(c_all, dmod)


def sum_leading(x, name):
    n, R, C = x.shape

    def body(x_ref, o_ref):
        acc = x_ref[0].astype(F32)
        for i in range(1, n):
            acc = acc + x_ref[i].astype(F32)
        o_ref[...] = acc

    tr = _row_tile(R, 64)
    return pl.pallas_call(
        body, name=name, grid=(R // tr,),
        in_specs=[pl.BlockSpec((n, tr, C), lambda i: (0, i, 0))],
        out_specs=pl.BlockSpec((tr, C), lambda i: (i, 0)),
        out_shape=jax.ShapeDtypeStruct((R, C), F32),
        compiler_params=_cp(("arbitrary",)),
    )(x)


def _row_tile(R, cap=512):
    if R <= cap:
        return R
    for cand in (512, 384, 352, 256, 128, 64, 32, 16, 8):
        if cand <= cap and R % cand == 0:
            return cand
    return R


def adamw(w, g, m, v, name):
    R, C = w.shape
    tr = _row_tile(R, 256 if C > 1024 else 512)
    bc1 = 1.0 - ADAM_B1 ** ADAM_STEP
    bc2 = 1.0 - ADAM_B2 ** ADAM_STEP

    def body(w_ref, g_ref, m_ref, v_ref, d_ref, nm_ref, nv_ref):
        gv = g_ref[...]
        nm = ADAM_B1 * m_ref[...] + (1.0 - ADAM_B1) * gv
        nv = ADAM_B2 * v_ref[...] + (1.0 - ADAM_B2) * (gv * gv)
        nm_ref[...] = nm
        nv_ref[...] = nv
        d_ref[...] = -ADAM_LR * ((nm / bc1) / (jnp.sqrt(nv / bc2) + ADAM_EPS) + ADAM_WD * w_ref[...])

    blk = pl.BlockSpec((tr, C), lambda i: (i, 0))
    sd = jax.ShapeDtypeStruct((R, C), F32)
    return pl.pallas_call(
        body, name=name, grid=(R // tr,),
        in_specs=[blk, blk, blk, blk], out_specs=[blk, blk, blk], out_shape=[sd, sd, sd],
        compiler_params=_cp(("arbitrary",)),
    )(w, g, m, v)


def _as2d(a):
    if a.ndim == 1:
        return a.reshape(1, -1)
    return a.reshape(-1, a.shape[-1])


def adamw_nd(w, g, m, v, name):
    outs = adamw(_as2d(w), _as2d(g.reshape(w.shape)), _as2d(m), _as2d(v), name)
    return tuple(o.reshape(w.shape) for o in outs)


def _place():
    x, y, c = lax.axis_index("x"), lax.axis_index("y"), lax.axis_index("c")
    chips = [(1 - x, y), (x, 1 - y), (1 - x, 1 - y)]
    return x, y, c, chips


def allgather_small(xs, name):
    m_per, n = xs.shape

    def body(x_ref, out_ref, send_sems, recv_sems, local_sem):
        x, y, c, chips = _place()
        me, sibling = (x, y, c), (x, y, 1 - c)

        def rows(px, py, pc):
            return out_ref.at[pl.ds((4 * px + 2 * py + pc) * m_per, m_per), :]

        def copy(k, block, to, src=None):
            return pltpu.make_async_remote_copy(
                src_ref=rows(*block) if src is None else src, dst_ref=rows(*block),
                send_sem=send_sems.at[k], recv_sem=recv_sems.at[k], device_id=to, device_id_type=MESH_T)

        mine = pltpu.make_async_copy(x_ref, rows(*me), local_sem)
        mine.start()
        first = [copy(0, me, sibling, src=x_ref)]
        first += [copy(1 + j, me, (*chip, c), src=x_ref) for j, chip in enumerate(chips)]
        for cp in first:
            cp.start()
        passed = [copy(4 + j, (*chip, c), sibling) for j, chip in enumerate(chips)]
        for j, chip in enumerate(chips):
            copy(1 + j, (*chip, c), me).wait_recv()
            passed[j].start()
        copy(0, sibling, me).wait_recv()
        for j, chip in enumerate(chips):
            copy(4 + j, (*chip, 1 - c), me).wait_recv()
        for cp in first + passed:
            cp.wait_send()
        mine.wait()

    return pl.pallas_call(
        body, name=name,
        out_shape=jax.ShapeDtypeStruct((N_DEV * m_per, n), xs.dtype),
        in_specs=[pl.BlockSpec(memory_space=pltpu.VMEM)],
        out_specs=pl.BlockSpec(memory_space=pltpu.VMEM),
        scratch_shapes=[pltpu.SemaphoreType.DMA((7,)), pltpu.SemaphoreType.DMA((7,)), pltpu.SemaphoreType.DMA],
        compiler_params=_cp(),
    )(xs)


def gather_weights(shards, name):
    n = len(shards)

    def body(*refs):
        ins, outs = refs[:n], refs[n:2 * n]
        send_sems, recv_sems, fsend_sems, frecv_sems, local_sems = refs[2 * n:]
        x, y, c, chips = _place()
        me_chip = 2 * x + y
        sibling = (x, y, 1 - c)
        locs = []
        for a in range(n):
            cp = pltpu.make_async_copy(ins[a], outs[a].at[me_chip], local_sems.at[a])
            cp.start()
            locs.append(cp)

        def ici(a, k, src_chip, to):
            return pltpu.make_async_remote_copy(
                src_ref=ins[a].at[c], dst_ref=outs[a].at[src_chip, c],
                send_sem=send_sems.at[a * 3 + k], recv_sem=recv_sems.at[a * 3 + k], device_id=to, device_id_type=MESH_T)

        def d2d(a, k, src_chip, half):
            return pltpu.make_async_remote_copy(
                src_ref=outs[a].at[src_chip, half], dst_ref=outs[a].at[src_chip, half],
                send_sem=fsend_sems.at[a * 3 + k], recv_sem=frecv_sems.at[a * 3 + k],
                device_id=sibling, device_id_type=MESH_T)

        firsts = []
        for k, (cx, cy) in enumerate(chips):
            for a in range(n):
                cp = ici(a, k, me_chip, (cx, cy, c))
                cp.start()
                firsts.append(cp)
        passed = []
        for k, (cx, cy) in enumerate(chips):
            for a in range(n):
                ici(a, k, 2 * cx + cy, (cx, cy, c)).wait_recv()
                cp = d2d(a, k, 2 * cx + cy, c)
                cp.start()
                passed.append(cp)
        for k, (cx, cy) in enumerate(chips):
            for a in range(n):
                d2d(a, k, 2 * cx + cy, 1 - c).wait_recv()
        for cp in firsts + passed:
            cp.wait_send()
        for cp in locs:
            cp.wait()

    return pl.pallas_call(
        body, name=name,
        out_shape=[jax.ShapeDtypeStruct((N_CHIPS,) + s.shape, s.dtype) for s in shards],
        in_specs=[ANY] * n, out_specs=[ANY] * n,
        scratch_shapes=[pltpu.SemaphoreType.DMA((3 * n,)), pltpu.SemaphoreType.DMA((3 * n,)),
                        pltpu.SemaphoreType.DMA((3 * n,)), pltpu.SemaphoreType.DMA((3 * n,)),
                        pltpu.SemaphoreType.DMA((n,))],
        compiler_params=_cp(),
    )(*shards)


def _half_rows(ref, half, rh):
    idx = (slice(None),) * (len(ref.shape) - 2) + (pl.ds(pl.multiple_of(half * rh, 16), rh), slice(None))
    return ref.at[idx]


def sibling_exchange_halves(grads, name):
    n = len(grads)

    def body(*refs):
        ins, outs = refs[:n], refs[n:2 * n]
        send_sems, recv_sems = refs[2 * n:]
        x, y, c, _ = _place()
        cps = []
        for a in range(n):
            rh = ins[a].shape[-2] // 2
            cp = pltpu.make_async_remote_copy(
                src_ref=_half_rows(ins[a], 1 - c, rh), dst_ref=outs[a],
                send_sem=send_sems.at[a], recv_sem=recv_sems.at[a], device_id=(x, y, 1 - c), device_id_type=MESH_T)
            cp.start()
            cps.append(cp)
        for cp in cps:
            cp.wait()

    return pl.pallas_call(
        body, name=name,
        out_shape=[jax.ShapeDtypeStruct(g.shape[:-2] + (g.shape[-2] // 2, g.shape[-1]), g.dtype) for g in grads],
        in_specs=[ANY] * n, out_specs=[ANY] * n,
        scratch_shapes=[pltpu.SemaphoreType.DMA((n,)), pltpu.SemaphoreType.DMA((n,))],
        compiler_params=_cp(),
    )(*grads)


def pair_sum(g, recv, cidx, name):
    L, NS, R, C = g.shape
    rh = R // 2
    tr = _row_tile(rh, 256 if C > 1024 else 512)
    nt = rh // tr

    def body(c_ref, g_ref, r_ref, o_ref):
        o_ref[...] = (g_ref[...].astype(F32) + r_ref[...].astype(F32)).astype(BF16)

    grid_spec = pltpu.PrefetchScalarGridSpec(
        num_scalar_prefetch=1, grid=(L, NS, nt),
        in_specs=[pl.BlockSpec((None, None, tr, C), lambda l, s, t, c: (l, s, c[0] * nt + t, 0)),
                  pl.BlockSpec((None, None, tr, C), lambda l, s, t, c: (l, s, t, 0))],
        out_specs=pl.BlockSpec((None, None, tr, C), lambda l, s, t, c: (l, s, t, 0)))
    return pl.pallas_call(
        body, name=name, grid_spec=grid_spec,
        out_shape=jax.ShapeDtypeStruct((L, NS, rh, C), BF16),
        compiler_params=_cp(("arbitrary", "arbitrary", "arbitrary")),
    )(cidx, g, recv)


def chip_exchange(parts, name):
    n = len(parts)

    def body(*refs):
        ins, outs = refs[:n], refs[n:2 * n]
        send_sems, recv_sems = refs[2 * n:]
        x, y, c, chips = _place()
        cps = []
        for k, (cx, cy) in enumerate(chips):
            for a in range(n):
                cp = pltpu.make_async_remote_copy(
                    src_ref=ins[a].at[:, 2 * cx + cy], dst_ref=outs[a].at[k],
                    send_sem=send_sems.at[a * 3 + k], recv_sem=recv_sems.at[a * 3 + k],
                    device_id=(cx, cy, c), device_id_type=MESH_T)
                cp.start()
                cps.append(cp)
        for cp in cps:
            cp.wait()

    return pl.pallas_call(
        body, name=name,
        out_shape=[jax.ShapeDtypeStruct((3, p.shape[0]) + p.shape[2:], p.dtype) for p in parts],
        in_specs=[ANY] * n, out_specs=[ANY] * n,
        scratch_shapes=[pltpu.SemaphoreType.DMA((3 * n,)), pltpu.SemaphoreType.DMA((3 * n,))],
        compiler_params=_cp(),
    )(*parts)


def chip_sum(part, recv, chip_c, name, dest=None, slot=0, n_slots=1):
    _, NS, RH, C = part.shape
    tr = _row_tile(RH, 256 if C > 1024 else 512)
    nt = RH // tr

    def body(cc_ref, p_ref, r_ref, *rest):
        o_ref = rest[-1]
        acc = p_ref[...].astype(F32)
        for k in range(3):
            acc = acc + r_ref[k].astype(F32)
        o_ref[...] = acc

    in_specs = [pl.BlockSpec((None, None, tr, C), lambda t, cc: (0, cc[0], t, 0)),
                pl.BlockSpec((3, None, tr, C), lambda t, cc: (0, 0, t, 0))]
    args = [chip_c, part, recv]
    aliases = {}
    if dest is not None:
        in_specs.append(ANY)
        args.append(dest)
        aliases = {3: 0}
    grid_spec = pltpu.PrefetchScalarGridSpec(
        num_scalar_prefetch=1, grid=(nt,), in_specs=in_specs,
        out_specs=pl.BlockSpec((None, tr, C), lambda t, cc: (slot, cc[1] * nt + t, 0)))
    return pl.pallas_call(
        body, name=name, grid_spec=grid_spec,
        out_shape=jax.ShapeDtypeStruct((n_slots, 2 * RH, C), F32),
        input_output_aliases=aliases,
        compiler_params=_cp(("arbitrary",)),
    )(*args)


def sibling_fill_halves(bufs, name):
    n = len(bufs)

    def body(*refs):
        outs = refs[n:2 * n]
        send_sems, recv_sems = refs[2 * n:]
        x, y, c, _ = _place()
        cps = []
        for a in range(n):
            rh = outs[a].shape[-2] // 2
            cp = pltpu.make_async_remote_copy(
                src_ref=_half_rows(outs[a], c, rh), dst_ref=_half_rows(outs[a], c, rh),
                send_sem=send_sems.at[a], recv_sem=recv_sems.at[a], device_id=(x, y, 1 - c), device_id_type=MESH_T)
            cp.start()
            cps.append(cp)
        for a, cp in enumerate(cps):
            cp.wait_send()
        for a in range(n):
            rh = outs[a].shape[-2] // 2
            pltpu.make_async_remote_copy(
                src_ref=_half_rows(outs[a], 1 - c, rh), dst_ref=_half_rows(outs[a], 1 - c, rh),
                send_sem=send_sems.at[a], recv_sem=recv_sems.at[a], device_id=(x, y, 1 - c),
                device_id_type=MESH_T).wait_recv()

    return pl.pallas_call(
        body, name=name,
        out_shape=[jax.ShapeDtypeStruct(b.shape, b.dtype) for b in bufs],
        in_specs=[ANY] * n, out_specs=[ANY] * n,
        input_output_aliases={a: a for a in range(n)},
        scratch_shapes=[pltpu.SemaphoreType.DMA((n,)), pltpu.SemaphoreType.DMA((n,))],
        compiler_params=_cp(),
    )(*bufs)


HBM_SPEC = pl.BlockSpec(memory_space=pltpu.HBM)
SEM_SPEC = pl.BlockSpec(memory_space=pltpu.SEMAPHORE)
DATAFLOW_EFFECT = pltpu.SideEffectType.DATAFLOW_SIDE_EFFECTING


def split_start(bufs, copies_fn, n_copies, name, dep=None):
    nb = len(bufs)
    extra = [] if dep is None else [dep]

    def body(*refs):
        ne = len(extra)
        send_sems, recv_sems = refs[nb + ne], refs[nb + ne + 1]
        token = refs[-1]
        for cp in copies_fn(refs[:nb], send_sems, recv_sems):
            cp.start()
        token[...] = jnp.zeros_like(token)

    outs = pl.pallas_call(
        body, name=name,
        out_shape=(pltpu.SemaphoreType.DMA((n_copies,)), pltpu.SemaphoreType.DMA((n_copies,)),
                   *[pltpu.HBM(b.shape, b.dtype) for b in bufs], jax.ShapeDtypeStruct((8, 128), F32)),
        in_specs=[HBM_SPEC] * nb + [ANY] * len(extra),
        out_specs=(SEM_SPEC, SEM_SPEC, *[HBM_SPEC] * nb, pl.BlockSpec(memory_space=pltpu.VMEM)),
        input_output_aliases={i: 2 + i for i in range(nb)},
        compiler_params=pltpu.CompilerParams(has_side_effects=DATAFLOW_EFFECT),
    )(*[pltpu.with_memory_space_constraint(b, pltpu.HBM) for b in bufs], *extra)
    return outs[0], outs[1], list(outs[2:2 + nb]), outs[-1]


def split_wait(send_sems, recv_sems, bufs, after, copies_fn, name):
    nb = len(bufs)

    def body(*refs):
        for cp in copies_fn(refs[:nb], refs[nb], refs[nb + 1]):
            cp.wait_send()
            cp.wait_recv()

    outs = pl.pallas_call(
        body, name=name,
        out_shape=tuple(pltpu.HBM(b.shape, b.dtype) for b in bufs),
        in_specs=[HBM_SPEC] * nb + [SEM_SPEC, SEM_SPEC, ANY],
        out_specs=tuple([HBM_SPEC] * nb),
        input_output_aliases={i: i for i in range(nb)},
        compiler_params=pltpu.CompilerParams(has_side_effects=DATAFLOW_EFFECT),
    )(*bufs, send_sems, recv_sems, after)
    return list(outs)


def chip_exchange_copies(refs, send_sems, recv_sems):
    n = len(refs) // 2
    x, y, c, chips = _place()
    cps = []
    for a in range(n):
        for k, (cx, cy) in enumerate(chips):
            cps.append(pltpu.make_async_remote_copy(
                src_ref=refs[a].at[:, 2 * cx + cy], dst_ref=refs[n + a].at[k],
                send_sem=send_sems.at[a * 3 + k], recv_sem=recv_sems.at[a * 3 + k],
                device_id=(cx, cy, c), device_id_type=MESH_T))
    return cps


def weight_ici_copies(refs, send_sems, recv_sems):
    x, y, c, chips = _place()
    cps = []
    for a in range(len(refs)):
        for k, (cx, cy) in enumerate(chips):
            cps.append(pltpu.make_async_remote_copy(
                src_ref=refs[a].at[2 * x + y, c], dst_ref=refs[a].at[2 * x + y, c],
                send_sem=send_sems.at[a * 3 + k], recv_sem=recv_sems.at[a * 3 + k],
                device_id=(cx, cy, c), device_id_type=MESH_T))
    return cps


def weight_d2d_copies(refs, send_sems, recv_sems):
    x, y, c, chips = _place()
    cps = []
    for a in range(len(refs)):
        for k, (cx, cy) in enumerate(chips):
            cps.append(pltpu.make_async_remote_copy(
                src_ref=refs[a].at[2 * cx + cy, c], dst_ref=refs[a].at[2 * cx + cy, c],
                send_sem=send_sems.at[a * 3 + k], recv_sem=recv_sems.at[a * 3 + k],
                device_id=(x, y, 1 - c), device_id_type=MESH_T))
    return cps


def cast_place(w, chip_arr, name, dep=None):
    R, C = w.shape
    tr = _row_tile(R, 256 if C > 1024 else 512)

    def body(chip_ref, w_ref, *rest):
        rest[-1][...] = w_ref[...].astype(BF16)

    in_specs = [pl.BlockSpec((tr, C), lambda t, ch: (t, 0))]
    args = [chip_arr, w]
    if dep is not None:
        in_specs.append(ANY)
        args.append(dep)
    grid_spec = pltpu.PrefetchScalarGridSpec(
        num_scalar_prefetch=1, grid=(R // tr,), in_specs=in_specs,
        out_specs=pl.BlockSpec((None, tr, C), lambda t, ch: (ch[0], t, 0)))
    return pl.pallas_call(
        body, name=name, grid_spec=grid_spec,
        out_shape=jax.ShapeDtypeStruct((N_CHIPS, R, C), BF16),
        compiler_params=_cp(("arbitrary",)),
    )(*args)


def _s5_prepare(lam_re, lam_im, log_dt, b_re, b_im, c_re, c_im, groups_per_block):
    G, P = lam_re.shape
    N = b_re.shape[-1]
    gb = groups_per_block
    nblk = G // gb
    dt = jnp.exp(log_dt)[:, None]
    e = jnp.exp(lam_re * dt)
    a_re = e * jnp.cos(lam_im * dt)
    a_im = e * jnp.sin(lam_im * dt)
    n2 = lam_re * lam_re + lam_im * lam_im
    co_re = ((a_re - 1.0) * lam_re + a_im * lam_im) / n2
    co_im = (a_im * lam_re - (a_re - 1.0) * lam_im) / n2
    bb_re = co_re[..., None] * b_re - co_im[..., None] * b_im
    bb_im = co_re[..., None] * b_im + co_im[..., None] * b_re
    eye = jnp.eye(gb, dtype=F32)

    def blockdiag_np(m):
        m = m.reshape(nblk, gb, N, P)
        return jnp.einsum('bgnp,gh->bgnhp', m, eye).reshape(nblk, gb * N, gb * P)

    b_np_re = jnp.swapaxes(bb_re, 1, 2)
    b_np_im = jnp.swapaxes(bb_im, 1, 2)
    return (a_re.reshape(nblk, 1, gb * P), a_im.reshape(nblk, 1, gb * P),
            blockdiag_np(b_np_re), blockdiag_np(b_np_im), blockdiag_np(c_re), blockdiag_np(c_im))


def _to_scan_order(a):
    S, D = a.shape
    return a.reshape(SCAN_LANES, S // SCAN_LANES, D).transpose(1, 0, 2).reshape(S, D)


def _from_scan_order(a):
    S, D = a.shape
    return a.reshape(S // SCAN_LANES, SCAN_LANES, D).transpose(1, 0, 2).reshape(S, D)


def _pad_rows(a, mult=8):
    r = (-a.shape[0]) % mult
    if r:
        a = jnp.concatenate([a, jnp.zeros((r, a.shape[1]), a.dtype)], axis=0)
    return a


def _pack_rows(arrs, width):
    parts, offs, o = [], [], 0
    for a in arrs:
        flat = a.reshape(-1)
        r = (-flat.shape[0]) % (16 * width)
        if r:
            flat = jnp.concatenate([flat, jnp.zeros((r,), flat.dtype)])
        p = flat.reshape(-1, width)
        parts.append(p)
        offs.append((o, a.shape, a.size))
        o += p.shape[0]
    if o % 64:
        parts.append(jnp.zeros((64 - o % 64, width), parts[0].dtype))
    return jnp.concatenate(parts, axis=0), offs


def _unpack_rows(packed, offs):
    outs = []
    for o, shape, size in offs:
        rows = -(-size // packed.shape[1])
        outs.append(packed[o:o + rows].reshape(-1)[:size].reshape(shape))
    return outs


def kernel(x, c, ada_w, ada_b, norm_pre, norm_post, ffn_w_in, ffn_w_out, ab_w_in, pool_w, pool_scale, sgu_ln_g, sgu_ln_b, sgu_w, sgu_b, ab_w_out, ssm_w_in, ssm_lam_re, ssm_lam_im, ssm_b_re, ssm_b_im, ssm_c_re, ssm_c_im, ssm_d, ssm_log_dt, ssm_w_glu, loss_target, m_ada_w, m_ada_b, m_norm_pre, m_norm_post, m_ffn_w_in, m_ffn_w_out, m_ab_w_in, m_pool_w, m_pool_scale, m_sgu_ln_g, m_sgu_ln_b, m_sgu_w, m_sgu_b, m_ab_w_out, m_ssm_w_in, m_ssm_lam_re, m_ssm_lam_im, m_ssm_b_re, m_ssm_b_im, m_ssm_c_re, m_ssm_c_im, m_ssm_d, m_ssm_log_dt, m_ssm_w_glu, v_ada_w, v_ada_b, v_norm_pre, v_norm_post, v_ffn_w_in, v_ffn_w_out, v_ab_w_in, v_pool_w, v_pool_scale, v_sgu_ln_g, v_sgu_ln_b, v_sgu_w, v_sgu_b, v_ab_w_out, v_ssm_w_in, v_ssm_lam_re, v_ssm_lam_im, v_ssm_b_re, v_ssm_b_im, v_ssm_c_re, v_ssm_c_im, v_ssm_d, v_ssm_log_dt, v_ssm_w_glu):
    weights = dict(ada_w=ada_w, ada_b=ada_b, norm_pre=norm_pre, norm_post=norm_post, ffn_w_in=ffn_w_in,
                   ffn_w_out=ffn_w_out, ab_w_in=ab_w_in, pool_w=pool_w, pool_scale=pool_scale, sgu_ln_g=sgu_ln_g,
                   sgu_ln_b=sgu_ln_b, sgu_w=sgu_w, sgu_b=sgu_b, ab_w_out=ab_w_out, ssm_w_in=ssm_w_in,
                   ssm_lam_re=ssm_lam_re, ssm_lam_im=ssm_lam_im, ssm_b_re=ssm_b_re, ssm_b_im=ssm_b_im,
                   ssm_c_re=ssm_c_re, ssm_c_im=ssm_c_im, ssm_d=ssm_d, ssm_log_dt=ssm_log_dt, ssm_w_glu=ssm_w_glu)
    m_in = dict(ada_w=m_ada_w, ada_b=m_ada_b, norm_pre=m_norm_pre, norm_post=m_norm_post, ffn_w_in=m_ffn_w_in,
                ffn_w_out=m_ffn_w_out, ab_w_in=m_ab_w_in, pool_w=m_pool_w, pool_scale=m_pool_scale,
                sgu_ln_g=m_sgu_ln_g, sgu_ln_b=m_sgu_ln_b, sgu_w=m_sgu_w, sgu_b=m_sgu_b, ab_w_out=m_ab_w_out,
                ssm_w_in=m_ssm_w_in, ssm_lam_re=m_ssm_lam_re, ssm_lam_im=m_ssm_lam_im, ssm_b_re=m_ssm_b_re,
                ssm_b_im=m_ssm_b_im, ssm_c_re=m_ssm_c_re, ssm_c_im=m_ssm_c_im, ssm_d=m_ssm_d,
                ssm_log_dt=m_ssm_log_dt, ssm_w_glu=m_ssm_w_glu)
    v_in = dict(ada_w=v_ada_w, ada_b=v_ada_b, norm_pre=v_norm_pre, norm_post=v_norm_post, ffn_w_in=v_ffn_w_in,
                ffn_w_out=v_ffn_w_out, ab_w_in=v_ab_w_in, pool_w=v_pool_w, pool_scale=v_pool_scale,
                sgu_ln_g=v_sgu_ln_g, sgu_ln_b=v_sgu_ln_b, sgu_w=v_sgu_w, sgu_b=v_sgu_b, ab_w_out=v_ab_w_out,
                ssm_w_in=v_ssm_w_in, ssm_lam_re=v_ssm_lam_re, ssm_lam_im=v_ssm_lam_im, ssm_b_re=v_ssm_b_re,
                ssm_b_im=v_ssm_b_im, ssm_c_re=v_ssm_c_re, ssm_c_im=v_ssm_c_im, ssm_d=v_ssm_d,
                ssm_log_dt=v_ssm_log_dt, ssm_w_glu=v_ssm_w_glu)
    names = list(weights.keys())

    xi, yi, ci = lax.axis_index("x"), lax.axis_index("y"), lax.axis_index("c")
    chip = 2 * xi + yi
    me = 4 * xi + 2 * yi + ci
    S, D = x.shape[1], x.shape[2]
    L = ada_w.shape[0]
    NSUB = norm_pre.shape[1]
    DS = norm_pre.shape[2]
    FS = ffn_w_in.shape[-1]
    FR = ffn_w_out.shape[-2]
    x0 = x[0]
    target = loss_target[0]

    small_parts = [_pad_rows(p) for p in (c.reshape(D // DS, DS), norm_pre.reshape(L * NSUB, DS),
                                          norm_post.reshape(L * NSUB, DS), ssm_d.reshape(1, DS))]
    small_offs = [0]
    for p in small_parts:
        small_offs.append(small_offs[-1] + p.shape[0])
    small_all = allgather_small(jnp.concatenate(small_parts, axis=0), "ag_small").reshape(N_DEV, -1, DS)
    c_all = small_all[:, :D // DS].reshape(N_DEV, D)
    per_chip = small_all[0::2]
    o = small_offs[1]
    g_pre_full = jnp.moveaxis(per_chip[:, o:o + L * NSUB], 0, 1).reshape(L, NSUB, D)
    o = small_offs[2]
    g_post_full = jnp.moveaxis(per_chip[:, o:o + L * NSUB], 0, 1).reshape(L, NSUB, D)
    o = small_offs[3]
    d_full = jnp.moveaxis(per_chip[:, o:o + 1], 0, 1).reshape(1, D)

    pieces = []
    for l in range(L):
        pieces.append((f"ffn_{l}_0", [ffn_w_in[l, 0], ffn_w_out[l, 0]]))
        if l % 2 == 0:
            pieces.append((f"mix_{l}", [ab_w_in[l // 2], ab_w_out[l // 2]]))
        else:
            pieces.append((f"mix_{l}", [ssm_w_in[l // 2], ssm_w_glu[l // 2]]))
        pieces.append((f"ffn_{l}_1", [ffn_w_in[l, 1], ffn_w_out[l, 1]]))
    chip_arr = chip.reshape(1).astype(jnp.int32)
    wg_started = {}

    def start_piece(tag, ws, dep):
        bufs = [cast_place(w, chip_arr, f"wg_cast_{tag}_{i}", dep=dep).reshape(N_CHIPS, 2, w.shape[0] // 2, w.shape[1])
                for i, w in enumerate(ws)]
        wg_started[tag] = split_start(bufs, weight_ici_copies, 3 * len(bufs), f"wg_ici_start_{tag}", dep=dep)
        return wg_started[tag][3][0, 0]

    def weights_of(tag, after):
        ssem, rsem, bufs, _ = wg_started[tag]
        lands = split_wait(ssem, rsem, bufs, after, weight_ici_copies, f"wg_ici_wait_{tag}")
        ssem, rsem, lands, tok = split_start(lands, weight_d2d_copies, 3 * len(lands), f"wg_d2d_start_{tag}")
        return split_wait(ssem, rsem, lands, tok, weight_d2d_copies, f"wg_d2d_wait_{tag}")

    wg_token = start_piece(*pieces[0], small_all)

    NS = ada_w.shape[-1]
    ada_b_shard = lax.dynamic_slice_in_dim(ada_b, chip * NS, NS, axis=1).reshape(L, 1, NS)
    mod_part = ada_mod(c_all + wg_token, ada_w, ada_b_shard, "ada_mod")
    mod_all = allgather_small(mod_part.reshape(L * N_DEV, NS), "ag_mod").reshape(N_DEV, L, N_DEV, NS)
    mod_mine = lax.dynamic_index_in_dim(mod_all[0::2], me, axis=2, keepdims=False)
    mod = jnp.moveaxis(mod_mine, 0, 1).reshape(L, NSUB, 3, D)
    for tag, ws in pieces[1:]:
        wg_token = wg_token + start_piece(tag, ws, mod_all)

    def vec_of(l, s):
        return jnp.concatenate([mod[l, s], g_pre_full[l, s][None], g_post_full[l, s][None],
                                jnp.zeros((3, D), F32)], axis=0)

    GB = 8
    s5_args = (ssm_lam_re[0], ssm_lam_im[0], ssm_log_dt[0], ssm_b_re[0], ssm_b_im[0], ssm_c_re[0], ssm_c_im[0])
    (a_re, a_im, bblk_re, bblk_im, cblk_re, cblk_im), s5_vjp = jax.vjp(lambda *p: _s5_prepare(*p, GB), *s5_args)
    bre16, bim16 = bblk_re.astype(BF16), bblk_im.astype(BF16)
    cre16, cim16 = cblk_re.astype(BF16), cblk_im.astype(BF16)
    cre16_t, cim16_t = jnp.swapaxes(cre16, 1, 2), jnp.swapaxes(cim16, 1, 2)

    pool_w16 = pool_w[0].astype(BF16)
    sgu_bexp = jnp.broadcast_to(sgu_b[0][:, :, None], sgu_w[0].shape)

    saved = {}
    ffn_w = {}
    xcur = x0

    def ffn_weights(l, k, after):
        g = weights_of(f"ffn_{l}_{k}", after)
        ffn_w[(l, k)] = (g[0].reshape(N_CHIPS, D, FS), g[1].reshape(N_CHIPS, FR, D))
        return ffn_w[(l, k)]

    for l in range(L):
        v0 = vec_of(l, 0)
        win_g, wout_g = ffn_weights(l, 0, xcur if l else v0 + wg_token)
        out, h, u, f = ffn_fwd(xcur, v0, win_g, wout_g, f"ffn_fwd_{l}_0")
        saved[(l, 0)] = (xcur, v0, h, u, f)
        xcur = out
        v1 = vec_of(l, 1)
        mix_g = weights_of(f"mix_{l}", xcur)
        if l % 2 == 0:
            abin_g = mix_g[0].reshape(N_CHIPS, D, -1)
            about_g = mix_g[1].reshape(-1, D)
            h, z = pre_matmul(xcur, v1, abin_g, f"mixa_in_{l}")
            ycat = mixa_core_fwd(z, pool_w16, pool_scale, sgu_ln_g, sgu_ln_b, sgu_w[0], sgu_bexp, f"mixa_core_{l}")
            out, f = matmul_post(ycat, about_g, xcur, v1, f"mixa_out_{l}")
            saved[(l, 1)] = (xcur, v1, h, z, ycat, f)
        else:
            sin_g = mix_g[0].reshape(1, -1, D)
            glu_g = mix_g[1].reshape(N_CHIPS, D, -1)
            h, uu = pre_matmul(xcur, v1, sin_g, f"s5_in_{l}")
            us = _to_scan_order(uu)
            ys = s5_core_fwd(us, a_re, a_im, bre16, bim16, cre16_t, cim16_t, d_full, f"s5_core_{l}")
            yy = _from_scan_order(ys)
            out, f = s5_glu_fwd(yy, glu_g, xcur, v1, f"s5_glu_{l}")
            saved[(l, 1)] = (xcur, v1, h, us, yy, f)
        xcur = out
        v2 = vec_of(l, 2)
        win_g, wout_g = ffn_weights(l, 1, xcur)
        out, h, u, f = ffn_fwd(xcur, v2, win_g, wout_g, f"ffn_fwd_{l}_1")
        saved[(l, 2)] = (xcur, v2, h, u, f)
        xcur = out

    dcur, sq = loss_head(xcur, target, "loss_head")
    loss = lax.psum(sq[0, 0], ("x", "y", "c")) * (0.5 / D)

    accs = {}
    small_g = {}
    cidx = ci.reshape(1).astype(jnp.int32)
    chip_c = jnp.stack([chip, ci]).astype(jnp.int32)
    rs_pending = []

    def rs_begin(tag, items):
        arrs = [it[0] for it in items]
        recv_a = sibling_exchange_halves(arrs, f"rs_pair_exchange_{tag}")
        parts = [pair_sum(a, r, cidx, f"rs_pair_sum_{tag}_{i}") for i, (a, r) in enumerate(zip(arrs, recv_a))]
        lands = [lax.empty((3, 1) + p.shape[2:], BF16) for p in parts]
        started = split_start(parts + lands, chip_exchange_copies, 3 * len(parts), f"rs_chip_start_{tag}")
        rs_pending.append((tag, items, started))
        pin[0] = started[3][0, 0]

    pin = [None]

    def pinned(v):
        return v if pin[0] is None else v + pin[0]

    def ffn_back(l, s, k, dcur, before_wgrads=None):
        xin, vv, h, u, f = saved[(l, s)]
        vv = pinned(vv)
        dx, df, du, act, acc = ffn_bwd(dcur, xin, f, u, vv, *ffn_w[(l, k)], f"ffn_bwd_{l}_{k}")
        accs[(l, s)] = acc
        dep = None if before_wgrads is None else before_wgrads()
        g_win = tn_matmul(h, du, 512, f"ffn_dwin_{l}_{k}", dep=dep)
        g_wout = tn_matmul_cols(act, df, FS, D, f"ffn_dwout_{l}_{k}")
        rs_begin(f"ffn_{l}_{k}", [(g_win[None], "ffn_w_in", 2 * l + k, 2 * L),
                                  (g_wout.reshape(1, N_CHIPS, FR, D), "ffn_w_out", 2 * l + k, 2 * L)])
        return dx

    grads = {}

    def small_grads():
        acc_all = jnp.stack([jnp.stack([accs[(l, s)] for s in range(NSUB)]) for l in range(L)])
        dmod_mine = acc_all[:, :, A_SHIFT:A_GATE + 1].reshape(L, NSUB * 3 * D)
        dgpre_mine = acc_all[:, :, A_GPRE]
        dgpost_mine = acc_all[:, :, A_GPOST]
        small_names = ["pool_w", "pool_scale", "sgu_ln_g", "sgu_ln_b", "sgu_w", "sgu_b", "ssm_lam_re", "ssm_lam_im",
                       "ssm_b_re", "ssm_b_im", "ssm_c_re", "ssm_c_im", "ssm_log_dt"]
        packed, offs = _pack_rows([dmod_mine, dgpre_mine, dgpost_mine, dd_mine] + [small_g[n] for n in small_names], D)
        rows = packed.shape[0]
        packed_all = allgather_small(pinned(packed).astype(BF16), "ag_grads").reshape(N_DEV, rows, D)
        summed = _unpack_rows(sum_leading(packed_all, "sum_small"), offs)
        grads.update({n: g for n, g in zip(small_names, summed[4:])})
        grads["ada_b"] = summed[0]
        grads["norm_pre"] = lax.dynamic_slice_in_dim(summed[1], chip * DS, DS, axis=2)
        grads["norm_post"] = lax.dynamic_slice_in_dim(summed[2], chip * DS, DS, axis=2)
        grads["ssm_d"] = lax.dynamic_slice_in_dim(summed[3], chip * DS, DS, axis=1)
        dmod_all = jnp.moveaxis(packed_all[:, :L * NSUB * 3].reshape(N_DEV, L, NSUB * 3 * D), 0, 1)
        dmod_shard = lax.dynamic_slice_in_dim(dmod_all, chip * NS, NS, axis=2)
        grads["ada_w"] = ada_grad(c_all, dmod_shard, "ada_grad")
        return packed_all

    for l in reversed(range(L)):
        dcur = ffn_back(l, 2, 1, dcur)
        if l % 2 == 0:
            xin, vv, h, z, ycat, f = saved[(l, 1)]
            vv = pinned(vv)
            df, dact, acc_post = post_bwd_matmul(dcur, f, vv, about_g, f"mixa_out_bwd_{l}")
            g_about = tn_matmul_cols(ycat, df, 512, D, f"mixa_dwout_{l}")
            dz, dpw, dvecw, dsw, dsb = mixa_core_bwd(z, dact, pool_w16, pool_scale, sgu_ln_g, sgu_ln_b, sgu_w[0],
                                                     sgu_bexp, f"mixa_core_bwd_{l}")
            g_abin = tn_matmul_cols(h, dz, 512, abin_g.shape[-1], f"mixa_dwin_{l}")
            dcur, acc_pre = matmul_pre_bwd(dz, abin_g, xin, dcur, vv, f"mixa_in_bwd_{l}")
            accs[(l, 1)] = acc_pre + acc_post
            rs_begin(f"mixa_{l}", [(g_abin[None], "ab_w_in", 0, 1),
                                   (g_about.reshape(1, N_CHIPS, -1, D), "ab_w_out", 0, 1)])
            small_g.update(pool_w=dpw[None], pool_scale=dvecw[0:1], sgu_ln_g=dvecw[1:2], sgu_ln_b=dvecw[2:3],
                           sgu_w=dsw[None], sgu_b=dsb.T[None])
        else:
            xin, vv, h, us, yy, f = saved[(l, 1)]
            vv = pinned(vv)
            dy, dab, gact, acc_post = s5_glu_bwd(dcur, f, yy, vv, glu_g, f"s5_glu_bwd_{l}")
            g_glu = tn_matmul_cols(gact, dab, 512, glu_g.shape[-1], f"s5_dwglu_{l}")
            dys = _to_scan_order(dy)
            dus, dbre, dbim, dcre_t, dcim_t, dar, dai, dd = s5_core_bwd(
                us, dys, a_re, a_im, bre16, bim16, cre16, cim16, d_full, f"s5_core_bwd_{l}")
            du = _from_scan_order(dus).astype(BF16)
            g_sin = tn_matmul_cols(h, du, 512, D, f"s5_dwin_{l}")
            dcur, acc_pre = matmul_pre_bwd(du, sin_g, xin, dcur, vv, f"s5_in_bwd_{l}")
            accs[(l, 1)] = acc_pre + acc_post
            rs_begin(f"s5_{l}", [(g_sin.reshape(1, N_CHIPS, -1, D), "ssm_w_in", 0, 1), (g_glu[None], "ssm_w_glu", 0, 1)])
            s5_grads = s5_vjp((dar, dai, dbre, dbim, jnp.swapaxes(dcre_t, 1, 2), jnp.swapaxes(dcim_t, 1, 2)))
            small_g.update(ssm_lam_re=s5_grads[0][None], ssm_lam_im=s5_grads[1][None], ssm_log_dt=s5_grads[2][None],
                           ssm_b_re=s5_grads[3][None], ssm_b_im=s5_grads[4][None], ssm_c_re=s5_grads[5][None],
                           ssm_c_im=s5_grads[6][None])
            dd_mine = dd
        dcur = ffn_back(l, 0, 0, dcur, before_wgrads=small_grads if l == 0 else None)
    grad_x = dcur[None]

    big_names = ["ffn_w_in", "ffn_w_out", "ab_w_in", "ab_w_out", "ssm_w_in", "ssm_w_glu"]
    deltas, new_m, new_v = {}, {}, {}

    def update(n):
        grads[n] = grads[n].reshape(weights[n].shape)
        deltas[n], new_m[n], new_v[n] = adamw_nd(weights[n], grads[n], m_in[n], v_in[n], f"adamw_{n}")

    for n in names:
        if n not in big_names:
            update(n)

    fin = {}
    after = pinned(deltas["ssm_log_dt"]) + deltas["ada_w"][0, 0, 0]
    for tag, items, (ssem, rsem, bufs, _) in rs_pending:
        bufs = split_wait(ssem, rsem, bufs, after, chip_exchange_copies, f"rs_chip_wait_{tag}")
        n = len(items)
        for i, (it, p, r) in enumerate(zip(items, bufs[:n], bufs[n:])):
            fin[it[1]] = chip_sum(p, r, chip_c, f"rs_chip_sum_{tag}_{i}", dest=fin.get(it[1]), slot=it[2],
                                  n_slots=it[3])
    filled = sibling_fill_halves([fin[n] for n in big_names], "rs_fill")
    for n, g in zip(big_names, filled):
        grads[n] = g
        update(n)

    return (loss, grad_x, *[grads[n] for n in names], *[deltas[n] for n in names],
            *[new_m[n] for n in names], *[new_v[n] for n in names])
```

```python
import math

import jax
import jax.numpy as jnp
from jax import lax
from jax.experimental import pallas as pl
from jax.experimental.pallas import tpu as pltpu

F32 = jnp.float32
BF16 = jnp.bfloat16
EPS = 1e-6
MESH_T = pl.DeviceIdType.MESH
VMEM_LIMIT_BYTES = 56 * 1024 * 1024
N_CHIPS = 4
N_DEV = 8
POOL_WINDOWS = (2, 4, 8, 16)
CHUNK = 128
SCAN_LANES = 8
SCAN_UNROLL = 4
ADAM_LR = 0.001
ADAM_B1 = 0.9
ADAM_B2 = 0.999
ADAM_EPS = 1e-08
ADAM_WD = 0.01
ADAM_STEP = 10
GELU_C = math.sqrt(2.0 / math.pi)
GELU_K = 0.044715

V_SHIFT, V_SCALE, V_GATE, V_GPRE, V_GPOST = 0, 1, 2, 3, 4
A_SHIFT, A_SCALE, A_GATE, A_GPRE, A_GPOST = 0, 1, 2, 3, 4

ANY = pl.BlockSpec(memory_space=pl.ANY)


def _cp(sem=None):
    if sem is None:
        return pltpu.CompilerParams(vmem_limit_bytes=VMEM_LIMIT_BYTES)
    return pltpu.CompilerParams(vmem_limit_bytes=VMEM_LIMIT_BYTES, dimension_semantics=sem)


def _dot(a, b):
    return jnp.dot(a, b, preferred_element_type=F32)


def _dot_nt(a, b):
    return lax.dot_general(a, b, (((1,), (1,)), ((), ())), preferred_element_type=F32)


def _dot_tn(a, b):
    return lax.dot_general(a, b, (((0,), (0,)), ((), ())), preferred_element_type=F32)


def _sigmoid(x):
    return 1.0 / (1.0 + jnp.exp(-x))


def _gelu(x):
    return 0.5 * x * (1.0 + jnp.tanh(GELU_C * (x + GELU_K * x * x * x)))


def _gelu_grad(x):
    t = jnp.tanh(GELU_C * (x + GELU_K * x * x * x))
    return 0.5 * (1.0 + t) + 0.5 * x * (1.0 - t * t) * GELU_C * (1.0 + 3.0 * GELU_K * x * x)


def _rowsum(v):
    return jnp.sum(v, axis=0, keepdims=True)


def _lanemean(v):
    return jnp.mean(v, axis=-1, keepdims=True)


def _row(ref, i):
    return ref[pl.ds(i, 1), :]


def _pre_fwd(x, vec_ref):
    r = lax.rsqrt(_lanemean(x * x) + EPS)
    return (x * r) * _row(vec_ref, V_GPRE) * (1.0 + _row(vec_ref, V_SCALE)) + _row(vec_ref, V_SHIFT)


def _pre_bwd(x, dh, vec_ref):
    g = _row(vec_ref, V_GPRE)
    sc = 1.0 + _row(vec_ref, V_SCALE)
    r = lax.rsqrt(_lanemean(x * x) + EPS)
    xn = x * r
    dhx = dh * xn
    t2 = dh * (g * sc)
    dx = r * (t2 - xn * _lanemean(t2 * xn))
    return dx, _rowsum(dh), _rowsum(dhx * g), _rowsum(dhx * sc)


def _post_fwd(x, f, vec_ref, rw):
    q = lax.rsqrt(_lanemean(f * f) + EPS)
    return x + (rw * _row(vec_ref, V_GATE)) * (f * q * _row(vec_ref, V_GPOST))


def _post_bwd(dout, f, vec_ref, rw):
    gp = _row(vec_ref, V_GPOST)
    gate = _row(vec_ref, V_GATE)
    q = lax.rsqrt(_lanemean(f * f) + EPS)
    fhat = f * q
    dgate = _rowsum(dout * (rw * fhat * gp))
    dy = dout * (rw * gate)
    dgpost = _rowsum(dy * fhat)
    t = dy * gp
    df = q * (t - fhat * _lanemean(t * fhat))
    return df, dgate, dgpost


def _acc_add(acc_ref, row, v):
    acc_ref[pl.ds(row, 1), :] += v


def ffn_fwd(x, vec, win_g, wout_g, name, tm=256):
    S, D = x.shape
    FS = win_g.shape[-1]
    FR = wout_g.shape[-2]

    def body(x_ref, vec_ref, win_hbm, wout_hbm, out_ref, h_ref, u_ref, f_ref, win_s, wout_s, sem):
        @pl.when(pl.program_id(0) == 0)
        def _():
            cps = [pltpu.make_async_copy(win_hbm.at[j], win_s.at[j], sem.at[j]) for j in range(N_CHIPS)]
            cps += [pltpu.make_async_copy(wout_hbm.at[j], wout_s.at[pl.ds(j * FR, FR), :], sem.at[N_CHIPS + j])
                    for j in range(N_CHIPS)]
            for cp in cps:
                cp.start()
            for cp in cps:
                cp.wait()

        xv = x_ref[...]
        h = _pre_fwd(xv, vec_ref).astype(BF16)
        h_ref[...] = h
        f = jnp.zeros((tm, D), F32)
        for j in range(2):
            a = _dot(h, win_s[j])
            b = _dot(h, win_s[2 + j])
            u_ref[j] = a.astype(BF16)
            u_ref[2 + j] = b.astype(BF16)
            act = (a * _sigmoid(a) * b).astype(BF16)
            f = f + _dot(act, wout_s[pl.ds(j * FS, FS), :])
        f_ref[...] = f
        out_ref[...] = _post_fwd(xv, f, vec_ref, 0.5)

    row = pl.BlockSpec((tm, D), lambda i: (i, 0))
    return pl.pallas_call(
        body, name=name, grid=(S // tm,),
        in_specs=[row, pl.BlockSpec((8, D), lambda i: (0, 0)), ANY, ANY],
        out_specs=[row, row, pl.BlockSpec((N_CHIPS, tm, FS), lambda i: (0, i, 0)), row],
        out_shape=[jax.ShapeDtypeStruct((S, D), F32), jax.ShapeDtypeStruct((S, D), BF16),
                   jax.ShapeDtypeStruct((N_CHIPS, S, FS), BF16), jax.ShapeDtypeStruct((S, D), F32)],
        scratch_shapes=[pltpu.VMEM((N_CHIPS, D, FS), BF16), pltpu.VMEM((N_CHIPS * FR, D), BF16),
                        pltpu.SemaphoreType.DMA((2 * N_CHIPS,))],
        compiler_params=_cp(("arbitrary",)),
    )(x, vec, win_g, wout_g)


def ffn_bwd(dout, x, f, u, vec, win_g, wout_g, name, tm=256):
    S, D = x.shape
    FS = win_g.shape[-1]
    FR = wout_g.shape[-2]

    def body(dout_ref, x_ref, f_ref, u_ref, vec_ref, win_hbm, wout_hbm,
             dx_ref, df_ref, du_ref, act_ref, acc_ref, win_s, wout_s, sem):
        @pl.when(pl.program_id(0) == 0)
        def _():
            cps = [pltpu.make_async_copy(win_hbm.at[j], win_s.at[j], sem.at[j]) for j in range(N_CHIPS)]
            cps += [pltpu.make_async_copy(wout_hbm.at[j], wout_s.at[pl.ds(j * FR, FR), :], sem.at[N_CHIPS + j])
                    for j in range(N_CHIPS)]
            for cp in cps:
                cp.start()
            acc_ref[...] = jnp.zeros_like(acc_ref)
            for cp in cps:
                cp.wait()

        dout_v = dout_ref[...]
        df, dgate, dgpost = _post_bwd(dout_v, f_ref[...], vec_ref, 0.5)
        dfb = df.astype(BF16)
        df_ref[...] = dfb
        dh = jnp.zeros((tm, D), F32)
        for j in range(2):
            a = u_ref[j].astype(F32)
            b = u_ref[2 + j].astype(F32)
            sig = _sigmoid(a)
            sl = a * sig
            dact = _dot_nt(dfb, wout_s[pl.ds(j * FS, FS), :])
            da = (dact * b * (sig * (1.0 + a * (1.0 - sig)))).astype(BF16)
            db = (dact * sl).astype(BF16)
            du_ref[j] = da
            du_ref[2 + j] = db
            act_ref[:, j * FS:(j + 1) * FS] = (sl * b).astype(BF16)
            dh = dh + _dot_nt(da, win_s[j]) + _dot_nt(db, win_s[2 + j])
        dx, dshift, dscale, dgpre = _pre_bwd(x_ref[...], dh, vec_ref)
        dx_ref[...] = dout_v + dx
        _acc_add(acc_ref, A_SHIFT, dshift)
        _acc_add(acc_ref, A_SCALE, dscale)
        _acc_add(acc_ref, A_GATE, dgate)
        _acc_add(acc_ref, A_GPRE, dgpre)
        _acc_add(acc_ref, A_GPOST, dgpost)

    row = pl.BlockSpec((tm, D), lambda i: (i, 0))
    ublk = pl.BlockSpec((N_CHIPS, tm, FS), lambda i: (0, i, 0))
    const = pl.BlockSpec((8, D), lambda i: (0, 0))
    return pl.pallas_call(
        body, name=name, grid=(S // tm,),
        in_specs=[row, row, row, ublk, const, ANY, ANY],
        out_specs=[row, row, ublk, pl.BlockSpec((tm, 2 * FS), lambda i: (i, 0)), const],
        out_shape=[jax.ShapeDtypeStruct((S, D), F32), jax.ShapeDtypeStruct((S, D), BF16),
                   jax.ShapeDtypeStruct((N_CHIPS, S, FS), BF16), jax.ShapeDtypeStruct((S, 2 * FS), BF16),
                   jax.ShapeDtypeStruct((8, D), F32)],
        scratch_shapes=[pltpu.VMEM((N_CHIPS, D, FS), BF16), pltpu.VMEM((N_CHIPS * FR, D), BF16),
                        pltpu.SemaphoreType.DMA((2 * N_CHIPS,))],
        compiler_params=_cp(("arbitrary",)),
    )(dout, x, f, u, vec, win_g, wout_g)


def tn_matmul(a, b, tk, name, dep=None):
    S, K = a.shape
    nb, _, tn = b.shape
    extra = [] if dep is None else [dep]

    def body(a_ref, b_ref, *rest):
        rest[-1][...] = _dot_tn(a_ref[...], b_ref[...]).astype(BF16)

    return pl.pallas_call(
        body, name=name, grid=(K // tk, nb),
        in_specs=[pl.BlockSpec((S, tk), lambda i, j: (0, i)), pl.BlockSpec((None, S, tn), lambda i, j: (j, 0, 0))]
        + [ANY] * len(extra),
        out_specs=pl.BlockSpec((None, tk, tn), lambda i, j: (j, i, 0)),
        out_shape=jax.ShapeDtypeStruct((nb, K, tn), BF16),
        compiler_params=_cp(("arbitrary", "arbitrary")),
    )(a, b, *extra)


def tn_matmul_cols(a, b, tk, tn, name):
    S, K = a.shape
    N = b.shape[1]

    def body(a_ref, b_ref, o_ref):
        o_ref[...] = _dot_tn(a_ref[...], b_ref[...]).astype(BF16)

    return pl.pallas_call(
        body, name=name, grid=(K // tk, N // tn),
        in_specs=[pl.BlockSpec((S, tk), lambda i, j: (0, i)), pl.BlockSpec((S, tn), lambda i, j: (0, j))],
        out_specs=pl.BlockSpec((None, tk, tn), lambda i, j: (j, i, 0)),
        out_shape=jax.ShapeDtypeStruct((N // tn, K, tn), BF16),
        compiler_params=_cp(("arbitrary", "arbitrary")),
    )(a, b)


def pre_matmul(x, vec, w3, name, tm=256):
    S, D = x.shape
    nj, _, Nj = w3.shape

    def body(x_ref, vec_ref, w_ref, h_ref, z_ref):
        h = _pre_fwd(x_ref[...], vec_ref).astype(BF16)
        h_ref[...] = h
        for j in range(nj):
            z_ref[:, j * Nj:(j + 1) * Nj] = _dot(h, w_ref[j])

    row = pl.BlockSpec((tm, D), lambda i: (i, 0))
    return pl.pallas_call(
        body, name=name, grid=(S // tm,),
        in_specs=[row, pl.BlockSpec((8, D), lambda i: (0, 0)), pl.BlockSpec((nj, D, Nj), lambda i: (0, 0, 0))],
        out_specs=[row, pl.BlockSpec((tm, nj * Nj), lambda i: (i, 0))],
        out_shape=[jax.ShapeDtypeStruct((S, D), BF16), jax.ShapeDtypeStruct((S, nj * Nj), F32)],
        compiler_params=_cp(("arbitrary",)),
    )(x, vec, w3)


def matmul_pre_bwd(dz, w3, x, dres, vec, name, tm=256):
    S, D = x.shape
    nj, _, Nj = w3.shape

    def body(dz_ref, w_ref, x_ref, dres_ref, vec_ref, dx_ref, acc_ref):
        @pl.when(pl.program_id(0) == 0)
        def _():
            acc_ref[...] = jnp.zeros_like(acc_ref)

        dh = jnp.zeros((tm, D), F32)
        for j in range(nj):
            dh = dh + _dot_nt(dz_ref[:, j * Nj:(j + 1) * Nj], w_ref[j])
        dx, dshift, dscale, dgpre = _pre_bwd(x_ref[...], dh, vec_ref)
        dx_ref[...] = dres_ref[...] + dx
        _acc_add(acc_ref, A_SHIFT, dshift)
        _acc_add(acc_ref, A_SCALE, dscale)
        _acc_add(acc_ref, A_GPRE, dgpre)

    row = pl.BlockSpec((tm, D), lambda i: (i, 0))
    const = pl.BlockSpec((8, D), lambda i: (0, 0))
    return pl.pallas_call(
        body, name=name, grid=(S // tm,),
        in_specs=[pl.BlockSpec((tm, nj * Nj), lambda i: (i, 0)), pl.BlockSpec((nj, D, Nj), lambda i: (0, 0, 0)),
                  row, row, const],
        out_specs=[row, const],
        out_shape=[jax.ShapeDtypeStruct((S, D), F32), jax.ShapeDtypeStruct((8, D), F32)],
        compiler_params=_cp(("arbitrary",)),
    )(dz, w3, x, dres, vec)


def matmul_post(act, w, x, vec, name, tm=256):
    S, D = x.shape
    K = act.shape[1]

    def body(act_ref, w_ref, x_ref, vec_ref, out_ref, f_ref):
        f = _dot(act_ref[...], w_ref[...])
        f_ref[...] = f
        out_ref[...] = _post_fwd(x_ref[...], f, vec_ref, 1.0)

    row = pl.BlockSpec((tm, D), lambda i: (i, 0))
    return pl.pallas_call(
        body, name=name, grid=(S // tm,),
        in_specs=[pl.BlockSpec((tm, K), lambda i: (i, 0)), pl.BlockSpec((K, D), lambda i: (0, 0)), row,
                  pl.BlockSpec((8, D), lambda i: (0, 0))],
        out_specs=[row, row],
        out_shape=[jax.ShapeDtypeStruct((S, D), F32), jax.ShapeDtypeStruct((S, D), F32)],
        compiler_params=_cp(("arbitrary",)),
    )(act, w, x, vec)


def post_bwd_matmul(dout, f, vec, w, name, tm=256):
    S, D = dout.shape
    K = w.shape[0]

    def body(dout_ref, f_ref, vec_ref, w_ref, df_ref, dact_ref, acc_ref):
        @pl.when(pl.program_id(0) == 0)
        def _():
            acc_ref[...] = jnp.zeros_like(acc_ref)

        df, dgate, dgpost = _post_bwd(dout_ref[...], f_ref[...], vec_ref, 1.0)
        dfb = df.astype(BF16)
        df_ref[...] = dfb
        dact_ref[...] = _dot_nt(dfb, w_ref[...])
        _acc_add(acc_ref, A_GATE, dgate)
        _acc_add(acc_ref, A_GPOST, dgpost)

    row = pl.BlockSpec((tm, D), lambda i: (i, 0))
    const = pl.BlockSpec((8, D), lambda i: (0, 0))
    return pl.pallas_call(
        body, name=name, grid=(S // tm,),
        in_specs=[row, row, const, pl.BlockSpec((K, D), lambda i: (0, 0))],
        out_specs=[row, pl.BlockSpec((tm, K), lambda i: (i, 0)), const],
        out_shape=[jax.ShapeDtypeStruct((S, D), BF16), jax.ShapeDtypeStruct((S, K), F32),
                   jax.ShapeDtypeStruct((8, D), F32)],
        compiler_params=_cp(("arbitrary",)),
    )(dout, f, vec, w)


def _band(w, transposed, prev):
    r = lax.broadcasted_iota(jnp.int32, (CHUNK, CHUNK), 1 if transposed else 0)
    c = lax.broadcasted_iota(jnp.int32, (CHUNK, CHUNK), 0 if transposed else 1)
    d = r - c
    m = (d + CHUNK < w) if prev else ((d >= 0) & (d < w))
    return jnp.where(m, 1.0, 0.0).astype(BF16)


def _split_hi_lo(a):
    hi = a.astype(BF16)
    lo = (a - hi.astype(F32)).astype(BF16)
    return hi, lo


def _pool_diff(a, ap, g, denom):
    w = POOL_WINDOWS[g]
    a_hi, a_lo = _split_hi_lo(a)
    p_hi, p_lo = _split_hi_lo(ap)
    mc = _band(w, False, False)
    mp = _band(w, False, True)
    win = _dot(mc, a_hi) + _dot(mc, a_lo) + _dot(mp, p_hi) + _dot(mp, p_lo)
    return win / denom - a


def _sgu_norm(v, lg, lb):
    mu = _lanemean(v)
    xc = v - mu
    rstd = lax.rsqrt(_lanemean(xc * xc) + EPS)
    xhat = xc * rstd
    return xhat, rstd, xhat * lg + lb


def _tril_mask():
    r = lax.broadcasted_iota(jnp.int32, (CHUNK, CHUNK), 0)
    c = lax.broadcasted_iota(jnp.int32, (CHUNK, CHUNK), 1)
    return r >= c


def _positions(i, w):
    r = lax.broadcasted_iota(jnp.int32, (CHUNK, 128), 0)
    pos = (i * CHUNK + r + 1).astype(F32)
    return jnp.minimum(pos, float(w))


def mixa_core_fwd(z, pool_w, pool_scale, ln_g, ln_b, sgu_w, sgu_bexp, name):
    S = z.shape[0]
    W = z.shape[1] // 3
    G = len(POOL_WINDOWS)
    GD = W // G

    def body(zc_ref, zp_ref, pw_ref, ps_ref, lg_ref, lb_ref, sw_ref, sb_ref, y_ref):
        i = pl.program_id(0)
        has_prev = jnp.where(i > 0, 1.0, 0.0)
        for g in range(G):
            sl = slice(g * GD, (g + 1) * GD)
            a = zc_ref[:, sl]
            ap = zp_ref[:, sl] * has_prev
            d = _pool_diff(a, ap, g, _positions(i, POOL_WINDOWS[g])).astype(BF16)
            y_ref[:, sl] = (_dot(d, pw_ref[g]) * ps_ref[:, sl]).astype(BF16)
        tril = _tril_mask()
        for hh in range(G):
            u = _gelu(zc_ref[:, W + hh * GD:W + (hh + 1) * GD])
            v = _gelu(zc_ref[:, 2 * W + hh * GD:2 * W + (hh + 1) * GD])
            sl = slice(hh * GD, (hh + 1) * GD)
            _, _, vn = _sgu_norm(v, lg_ref[:, sl], lb_ref[:, sl])
            wm = jnp.where(tril, sw_ref[hh], 0.0).astype(BF16)
            s = _dot(wm, vn.astype(BF16)) + sb_ref[hh]
            y_ref[:, W + hh * GD:W + (hh + 1) * GD] = (u * s).astype(BF16)

    vecw = pl.BlockSpec((1, W), lambda i: (0, 0))
    mats = pl.BlockSpec((G, GD, GD), lambda i: (0, 0, 0))
    return pl.pallas_call(
        body, name=name, grid=(S // CHUNK,),
        in_specs=[pl.BlockSpec((CHUNK, 3 * W), lambda i: (i, 0)),
                  pl.BlockSpec((CHUNK, W), lambda i: (jnp.maximum(i - 1, 0), 0)),
                  mats, vecw, vecw, vecw, mats, mats],
        out_specs=pl.BlockSpec((CHUNK, 2 * W), lambda i: (i, 0)),
        out_shape=jax.ShapeDtypeStruct((S, 2 * W), BF16),
        compiler_params=_cp(("arbitrary",)),
    )(z, z, pool_w, pool_scale, ln_g, ln_b, sgu_w, sgu_bexp)


def mixa_core_bwd(z, dy, pool_w, pool_scale, ln_g, ln_b, sgu_w, sgu_bexp, name):
    S = z.shape[0]
    W = z.shape[1] // 3
    G = len(POOL_WINDOWS)
    GD = W // G
    n_tiles = S // CHUNK

    def body(zc_ref, zp_ref, dyc_ref, dyn_ref, pw_ref, ps_ref, lg_ref, lb_ref, sw_ref, sb_ref,
             dz_ref, dpw_ref, dvec_ref, dsw_ref, dsb_ref):
        i = pl.program_id(0)

        @pl.when(i == 0)
        def _():
            dpw_ref[...] = jnp.zeros_like(dpw_ref)
            dvec_ref[...] = jnp.zeros_like(dvec_ref)
            dsw_ref[...] = jnp.zeros_like(dsw_ref)
            dsb_ref[...] = jnp.zeros_like(dsb_ref)

        has_prev = jnp.where(i > 0, 1.0, 0.0)
        has_next = jnp.where(i < n_tiles - 1, 1.0, 0.0)
        for g in range(G):
            w = POOL_WINDOWS[g]
            sl = slice(g * GD, (g + 1) * GD)
            a = zc_ref[:, sl]
            ap = zp_ref[:, sl] * has_prev
            den_c = _positions(i, w)
            den_n = _positions(i + 1, w)
            d = _pool_diff(a, ap, g, den_c).astype(BF16)
            ps = ps_ref[:, sl]
            pw = pw_ref[g]
            dyc = dyc_ref[:, sl]
            dvec_ref[pl.ds(0, 1), sl] += _rowsum(dyc * _dot(d, pw))
            dyp_c = (dyc * ps).astype(BF16)
            dyp_n = (dyn_ref[:, sl] * (ps * has_next)).astype(BF16)
            dpw_ref[g] += _dot_tn(d, dyp_c)
            dd_c = _dot_nt(dyp_c, pw)
            dd_n = _dot_nt(dyp_n, pw)
            da = (_dot(_band(w, True, False), (dd_c / den_c).astype(BF16))
                  + _dot(_band(w, True, True), (dd_n / den_n).astype(BF16)) - dd_c)
            dz_ref[:, sl] = da.astype(BF16)
        tril = _tril_mask()
        for hh in range(G):
            sl = slice(hh * GD, (hh + 1) * GD)
            zu = zc_ref[:, W + hh * GD:W + (hh + 1) * GD]
            zv = zc_ref[:, 2 * W + hh * GD:2 * W + (hh + 1) * GD]
            u = _gelu(zu)
            v = _gelu(zv)
            lg = lg_ref[:, sl]
            xhat, rstd, vn = _sgu_norm(v, lg, lb_ref[:, sl])
            vnb = vn.astype(BF16)
            wm = jnp.where(tril, sw_ref[hh], 0.0).astype(BF16)
            s = _dot(wm, vnb) + sb_ref[hh]
            dyb = dyc_ref[:, W + hh * GD:W + (hh + 1) * GD]
            du = dyb * s
            ds = dyb * u
            dsb_ref[:, hh:hh + 1] += jnp.sum(ds, axis=1, keepdims=True)
            dsb16 = ds.astype(BF16)
            dsw_ref[hh] += jnp.where(tril, _dot_nt(dsb16, vnb), 0.0)
            dvn = _dot_tn(wm, dsb16)
            dvec_ref[pl.ds(1, 1), sl] += _rowsum(dvn * xhat)
            dvec_ref[pl.ds(2, 1), sl] += _rowsum(dvn)
            dxh = dvn * lg
            dv = rstd * (dxh - _lanemean(dxh) - xhat * _lanemean(dxh * xhat))
            dz_ref[:, W + hh * GD:W + (hh + 1) * GD] = (du * _gelu_grad(zu)).astype(BF16)
            dz_ref[:, 2 * W + hh * GD:2 * W + (hh + 1) * GD] = (dv * _gelu_grad(zv)).astype(BF16)

    vecw = pl.BlockSpec((1, W), lambda i: (0, 0))
    mats = pl.BlockSpec((G, GD, GD), lambda i: (0, 0, 0))
    return pl.pallas_call(
        body, name=name, grid=(n_tiles,),
        in_specs=[pl.BlockSpec((CHUNK, 3 * W), lambda i: (i, 0)),
                  pl.BlockSpec((CHUNK, W), lambda i: (jnp.maximum(i - 1, 0), 0)),
                  pl.BlockSpec((CHUNK, 2 * W), lambda i: (i, 0)),
                  pl.BlockSpec((CHUNK, W), lambda i: (jnp.minimum(i + 1, n_tiles - 1), 0)),
                  mats, vecw, vecw, vecw, mats, mats],
        out_specs=[pl.BlockSpec((CHUNK, 3 * W), lambda i: (i, 0)), mats,
                   pl.BlockSpec((8, W), lambda i: (0, 0)), mats, pl.BlockSpec((CHUNK, G), lambda i: (0, 0))],
        out_shape=[jax.ShapeDtypeStruct((S, 3 * W), BF16), jax.ShapeDtypeStruct((G, GD, GD), F32),
                   jax.ShapeDtypeStruct((8, W), F32), jax.ShapeDtypeStruct((G, CHUNK, CHUNK), F32),
                   jax.ShapeDtypeStruct((CHUNK, G), F32)],
        compiler_params=_cp(("arbitrary",)),
    )(z, z, dy, dy, pool_w, pool_scale, ln_g, ln_b, sgu_w, sgu_bexp)


def _cmul(ar, ai, br, bi):
    return ar * br - ai * bi, ar * bi + ai * br


def _cpow(ar, ai, n):
    rr, ri = None, None
    br, bi = ar, ai
    while n:
        if n & 1:
            rr, ri = (br, bi) if rr is None else _cmul(rr, ri, br, bi)
        n >>= 1
        if n:
            br, bi = _cmul(br, bi, br, bi)
    return rr, ri


def _seg_rows(k):
    return pl.ds(pl.multiple_of(k * SCAN_LANES, SCAN_LANES), SCAN_LANES)


def _scan_fwd(xre, xim, carry, ar, ai, K):
    P = xre.shape[1]
    a8r = jnp.broadcast_to(ar, (SCAN_LANES, P))
    a8i = jnp.broadcast_to(ai, (SCAN_LANES, P))

    def local(k, c):
        pr, pi = c
        rows = _seg_rows(k)
        nr = a8r * pr - a8i * pi + xre[rows, :]
        ni = a8r * pi + a8i * pr + xim[rows, :]
        xre[rows, :] = nr
        xim[rows, :] = ni
        return nr, ni

    er, ei = lax.fori_loop(1, K, local, (xre[pl.ds(0, SCAN_LANES), :], xim[pl.ds(0, SCAN_LANES), :]),
                           unroll=SCAN_UNROLL)
    akr, aki = _cpow(ar, ai, K)
    cr = jnp.zeros((1, P), F32)
    ci = jnp.zeros((1, P), F32)
    carry[pl.ds(0, 1), :] = cr
    carry[pl.ds(SCAN_LANES, 1), :] = ci
    for j in range(1, SCAN_LANES):
        tr, ti = _cmul(akr, aki, cr, ci)
        cr = er[j - 1:j, :] + tr
        ci = ei[j - 1:j, :] + ti
        carry[pl.ds(j, 1), :] = cr
        carry[pl.ds(SCAN_LANES + j, 1), :] = ci
    cmr = carry[pl.ds(0, SCAN_LANES), :]
    cmi = carry[pl.ds(SCAN_LANES, SCAN_LANES), :]

    def fix(k, c):
        pr, pi = c
        rows = _seg_rows(k)
        tr, ti = _cmul(pr, pi, cmr, cmi)
        xre[rows, :] += tr
        xim[rows, :] += ti
        return _cmul(pr, pi, a8r, a8i)

    lax.fori_loop(0, K, fix, (a8r, a8i), unroll=SCAN_UNROLL)


def s5_core_fwd(u, ar, ai, bre, bim, cre_t, cim_t, dskip, name):
    S, D = u.shape
    nblk, UB, PB = bre.shape
    K = S // SCAN_LANES

    def body(u_ref, ar_ref, ai_ref, bre_ref, bim_ref, cre_ref, cim_ref, d_ref, y_ref, xre, xim, carry):
        uv = u_ref[...]
        ub = uv.astype(BF16)
        xre[...] = _dot(ub, bre_ref[...])
        xim[...] = _dot(ub, bim_ref[...])
        _scan_fwd(xre, xim, carry, ar_ref[...], ai_ref[...], K)
        y_ref[...] = (_dot(xre[...].astype(BF16), cre_ref[...]) - _dot(xim[...].astype(BF16), cim_ref[...])
                      + d_ref[...] * uv)

    ucol = pl.BlockSpec((S, UB), lambda i: (0, i))
    pvec = pl.BlockSpec((None, 1, PB), lambda i: (i, 0, 0))
    bmat = pl.BlockSpec((None, UB, PB), lambda i: (i, 0, 0))
    cmat = pl.BlockSpec((None, PB, UB), lambda i: (i, 0, 0))
    return pl.pallas_call(
        body, name=name, grid=(nblk,),
        in_specs=[ucol, pvec, pvec, bmat, bmat, cmat, cmat, pl.BlockSpec((1, UB), lambda i: (0, i))],
        out_specs=ucol,
        out_shape=jax.ShapeDtypeStruct((S, D), F32),
        scratch_shapes=[pltpu.VMEM((S, PB), F32), pltpu.VMEM((S, PB), F32), pltpu.VMEM((2 * SCAN_LANES, PB), F32)],
        compiler_params=_cp(("arbitrary",)),
    )(u, ar, ai, bre, bim, cre_t, cim_t, dskip)


def s5_core_bwd(u, dy, ar, ai, bre, bim, cre, cim, dskip, name):
    S, D = u.shape
    nblk, UB, PB = bre.shape
    K = S // SCAN_LANES

    def body(u_ref, dy_ref, ar_ref, ai_ref, bre_ref, bim_ref, cre_ref, cim_ref, d_ref,
             du_ref, dbre_ref, dbim_ref, dcre_ref, dcim_ref, dar_ref, dai_ref, dd_ref,
             xre, xim, gre, gim, carry, carry_b):
        ar = ar_ref[...]
        ai = ai_ref[...]
        uv = u_ref[...]
        ub = uv.astype(BF16)
        dyv = dy_ref[...]
        dyb = dyv.astype(BF16)
        xre[...] = _dot(ub, bre_ref[...])
        xim[...] = _dot(ub, bim_ref[...])
        _scan_fwd(xre, xim, carry, ar, ai, K)
        dcre_ref[...] = _dot_tn(xre[...].astype(BF16), dyb)
        dcim_ref[...] = -_dot_tn(xim[...].astype(BF16), dyb)
        gre[...] = _dot(dyb, cre_ref[...])
        gim[...] = -_dot(dyb, cim_ref[...])

        a8r = jnp.broadcast_to(ar, (SCAN_LANES, PB))
        a8i = jnp.broadcast_to(ai, (SCAN_LANES, PB))
        na8i = -a8i

        def local(s, c):
            k = K - 2 - s
            nr, ni = c
            rows = _seg_rows(k)
            tr = gre[rows, :] + a8r * nr + a8i * ni
            ti = gim[rows, :] + a8r * ni - a8i * nr
            gre[rows, :] = tr
            gim[rows, :] = ti
            return tr, ti

        last = _seg_rows(K - 1)
        fr, fi = lax.fori_loop(0, K - 1, local, (gre[last, :], gim[last, :]), unroll=SCAN_UNROLL)
        akr, aki = _cpow(ar, -ai, K)
        cr = jnp.zeros((1, PB), F32)
        ci = jnp.zeros((1, PB), F32)
        carry_b[pl.ds(SCAN_LANES - 1, 1), :] = cr
        carry_b[pl.ds(2 * SCAN_LANES - 1, 1), :] = ci
        for j in range(SCAN_LANES - 2, -1, -1):
            tr, ti = _cmul(akr, aki, cr, ci)
            cr = fr[j + 1:j + 2, :] + tr
            ci = fi[j + 1:j + 2, :] + ti
            carry_b[pl.ds(j, 1), :] = cr
            carry_b[pl.ds(SCAN_LANES + j, 1), :] = ci
        cbr = carry_b[pl.ds(0, SCAN_LANES), :]
        cbi = carry_b[pl.ds(SCAN_LANES, SCAN_LANES), :]

        def fix_rows(rows, c, xr, xi):
            pr, pi, dar, dai = c
            tr, ti = _cmul(pr, pi, cbr, cbi)
            g_r = gre[rows, :] + tr
            g_i = gim[rows, :] + ti
            gre[rows, :] = g_r
            gim[rows, :] = g_i
            dar = dar + g_r * xr + g_i * xi
            dai = dai + g_i * xr - g_r * xi
            nr, ni = _cmul(pr, pi, a8r, na8i)
            return nr, ni, dar, dai

        def fix(s, c):
            k = K - 1 - s
            prev = _seg_rows(k - 1)
            return fix_rows(_seg_rows(k), c, xre[prev, :], xim[prev, :])

        z8 = jnp.zeros((SCAN_LANES, PB), F32)
        c = lax.fori_loop(0, K - 1, fix, (a8r, na8i, z8, z8), unroll=SCAN_UNROLL)
        _, _, dar, dai = fix_rows(_seg_rows(0), c, carry[pl.ds(0, SCAN_LANES), :], carry[pl.ds(SCAN_LANES, SCAN_LANES), :])
        dar_ref[...] = _rowsum(dar)
        dai_ref[...] = _rowsum(dai)
        grb = gre[...].astype(BF16)
        gib = gim[...].astype(BF16)
        dbre_ref[...] = _dot_tn(ub, grb)
        dbim_ref[...] = _dot_tn(ub, gib)
        du_ref[...] = _dot_nt(grb, bre_ref[...]) + _dot_nt(gib, bim_ref[...]) + d_ref[...] * dyv
        dd_ref[...] = _rowsum(dyv * uv)

    ucol = pl.BlockSpec((S, UB), lambda i: (0, i))
    pvec = pl.BlockSpec((None, 1, PB), lambda i: (i, 0, 0))
    bmat = pl.BlockSpec((None, UB, PB), lambda i: (i, 0, 0))
    cmat = pl.BlockSpec((None, PB, UB), lambda i: (i, 0, 0))
    dvec = pl.BlockSpec((1, UB), lambda i: (0, i))
    return pl.pallas_call(
        body, name=name, grid=(nblk,),
        in_specs=[ucol, ucol, pvec, pvec, bmat, bmat, bmat, bmat, dvec],
        out_specs=[ucol, bmat, bmat, cmat, cmat, pvec, pvec, dvec],
        out_shape=[jax.ShapeDtypeStruct((S, D), F32),
                   jax.ShapeDtypeStruct((nblk, UB, PB), F32), jax.ShapeDtypeStruct((nblk, UB, PB), F32),
                   jax.ShapeDtypeStruct((nblk, PB, UB), F32), jax.ShapeDtypeStruct((nblk, PB, UB), F32),
                   jax.ShapeDtypeStruct((nblk, 1, PB), F32), jax.ShapeDtypeStruct((nblk, 1, PB), F32),
                   jax.ShapeDtypeStruct((1, D), F32)],
        scratch_shapes=[pltpu.VMEM((S, PB), F32), pltpu.VMEM((S, PB), F32), pltpu.VMEM((S, PB), F32),
                        pltpu.VMEM((S, PB), F32), pltpu.VMEM((2 * SCAN_LANES, PB), F32),
                        pltpu.VMEM((2 * SCAN_LANES, PB), F32)],
        compiler_params=_cp(("arbitrary",)),
    )(u, dy, ar, ai, bre, bim, cre, cim, dskip)


def s5_glu_fwd(y, wglu, x, vec, name, tm=256):
    S, D = x.shape
    NJ = wglu.shape[-1]

    def body(y_ref, w_ref, x_ref, vec_ref, out_ref, f_ref):
        g = _gelu(y_ref[...]).astype(BF16)
        f = jnp.concatenate([_dot(g, w_ref[j]) * _sigmoid(_dot(g, w_ref[2 + j])) for j in range(2)], axis=1)
        f_ref[...] = f
        out_ref[...] = _post_fwd(x_ref[...], f, vec_ref, 1.0)

    row = pl.BlockSpec((tm, D), lambda i: (i, 0))
    return pl.pallas_call(
        body, name=name, grid=(S // tm,),
        in_specs=[row, pl.BlockSpec((N_CHIPS, D, NJ), lambda i: (0, 0, 0)), row, pl.BlockSpec((8, D), lambda i: (0, 0))],
        out_specs=[row, row],
        out_shape=[jax.ShapeDtypeStruct((S, D), F32), jax.ShapeDtypeStruct((S, D), F32)],
        compiler_params=_cp(("arbitrary",)),
    )(y, wglu, x, vec)


def s5_glu_bwd(dout, f, y, vec, wglu, name, tm=256):
    S, D = dout.shape
    NJ = wglu.shape[-1]

    def body(dout_ref, f_ref, y_ref, vec_ref, w_ref, dy_ref, dab_ref, g_ref, acc_ref):
        @pl.when(pl.program_id(0) == 0)
        def _():
            acc_ref[...] = jnp.zeros_like(acc_ref)

        df, dgate, dgpost = _post_bwd(dout_ref[...], f_ref[...], vec_ref, 1.0)
        yv = y_ref[...]
        g = _gelu(yv).astype(BF16)
        g_ref[...] = g
        dg = jnp.zeros((tm, D), F32)
        for j in range(2):
            a = _dot(g, w_ref[j])
            sig = _sigmoid(_dot(g, w_ref[2 + j]))
            dfj = df[:, j * NJ:(j + 1) * NJ]
            da = (dfj * sig).astype(BF16)
            db = (dfj * a * sig * (1.0 - sig)).astype(BF16)
            dab_ref[:, j * NJ:(j + 1) * NJ] = da
            dab_ref[:, (2 + j) * NJ:(3 + j) * NJ] = db
            dg = dg + _dot_nt(da, w_ref[j]) + _dot_nt(db, w_ref[2 + j])
        dy_ref[...] = dg * _gelu_grad(yv)
        _acc_add(acc_ref, A_GATE, dgate)
        _acc_add(acc_ref, A_GPOST, dgpost)

    row = pl.BlockSpec((tm, D), lambda i: (i, 0))
    const = pl.BlockSpec((8, D), lambda i: (0, 0))
    return pl.pallas_call(
        body, name=name, grid=(S // tm,),
        in_specs=[row, row, row, const, pl.BlockSpec((N_CHIPS, D, NJ), lambda i: (0, 0, 0))],
        out_specs=[row, pl.BlockSpec((tm, N_CHIPS * NJ), lambda i: (i, 0)), row, const],
        out_shape=[jax.ShapeDtypeStruct((S, D), F32), jax.ShapeDtypeStruct((S, N_CHIPS * NJ), BF16),
                   jax.ShapeDtypeStruct((S, D), BF16), jax.ShapeDtypeStruct((8, D), F32)],
        compiler_params=_cp(("arbitrary",)),
    )(dout, f, y, vec, wglu)


def loss_head(y, target, name, tm=256):
    S, D = y.shape

    def body(y_ref, t_ref, d_ref, l_ref):
        @pl.when(pl.program_id(0) == 0)
        def _():
            l_ref[...] = jnp.zeros_like(l_ref)

        err = y_ref[...] - t_ref[...]
        d_ref[...] = err * (1.0 / D)
        l_ref[...] += jnp.sum(_rowsum(err * err), axis=1, keepdims=True)

    row = pl.BlockSpec((tm, D), lambda i: (i, 0))
    return pl.pallas_call(
        body, name=name, grid=(S // tm,),
        in_specs=[row, row],
        out_specs=[row, pl.BlockSpec((1, 1), lambda i: (0, 0))],
        out_shape=[jax.ShapeDtypeStruct((S, D), F32), jax.ShapeDtypeStruct((1, 1), F32)],
        compiler_params=_cp(("arbitrary",)),
    )(y, target)


def zero_after(dep, name):
    def body(dep_ref, o_ref):
        o_ref[...] = jnp.zeros_like(o_ref)

    return pl.pallas_call(body, name=name, in_specs=[ANY], out_specs=pl.BlockSpec(memory_space=pltpu.VMEM),
                          out_shape=jax.ShapeDtypeStruct((8, 128), F32), compiler_params=_cp())(dep)


def ada_mod(c_all, ada_w, ada_b_shard, name, tn=768):
    B, D = c_all.shape
    L, _, NS = ada_w.shape

    def body(c_ref, w_ref, b_ref, o_ref):
        cv = c_ref[...]
        cond = (cv * _sigmoid(cv)).astype(BF16)
        o_ref[...] = _dot(cond, w_ref[...].astype(BF16)) + b_ref[...]

    return pl.pallas_call(
        body, name=name, grid=(L, NS // tn),
        in_specs=[pl.BlockSpec((B, D), lambda l, j: (0, 0)), pl.BlockSpec((None, D, tn), lambda l, j: (l, 0, j)),
                  pl.BlockSpec((None, 1, tn), lambda l, j: (l, 0, j))],
        out_specs=pl.BlockSpec((None, B, tn), lambda l, j: (l, 0, j)),
        out_shape=jax.ShapeDtypeStruct((L, B, NS), F32),
        compiler_params=_cp(("arbitrary", "arbitrary")),
    )(c_all, ada_w, ada_b_shard)


def ada_grad(c_all, dmod, name, tn=768):
    B, D = c_all.shape
    L, _, NS = dmod.shape

    def body(c_ref, d_ref, o_ref):
        cv = c_ref[...]
        cond = (cv * _sigmoid(cv)).astype(BF16)
        o_ref[...] = _dot_tn(cond, d_ref[...].astype(BF16))

    return pl.pallas_call(
        body, name=name, grid=(L, NS // tn),
        in_specs=[pl.BlockSpec((B, D), lambda l, j: (0, 0)), pl.BlockSpec((None, B, tn), lambda l, j: (l, 0, j))],
        out_specs=pl.BlockSpec((None, D, tn), lambda l, j: (l, 0, j)),
        out_shape=jax.ShapeDtypeStruct((L, D, NS), F32),
        compiler_params=_cp(("arbitrary", "arbitrary")),
    )(c_all, dmod)


def sum_leading(x, name):
    n, R, C = x.shape

    def body(x_ref, o_ref):
        acc = x_ref[0].astype(F32)
        for i in range(1, n):
            acc = acc + x_ref[i].astype(F32)
        o_ref[...] = acc

    tr = _row_tile(R, 64)
    return pl.pallas_call(
        body, name=name, grid=(R // tr,),
        in_specs=[pl.BlockSpec((n, tr, C), lambda i: (0, i, 0))],
        out_specs=pl.BlockSpec((tr, C), lambda i: (i, 0)),
        out_shape=jax.ShapeDtypeStruct((R, C), F32),
        compiler_params=_cp(("arbitrary",)),
    )(x)


def _row_tile(R, cap=512):
    if R <= cap:
        return R
    for cand in (512, 384, 352, 256, 128, 64, 32, 16, 8):
        if cand <= cap and R % cand == 0:
            return cand
    return R


def adamw(w, g, m, v, name):
    R, C = w.shape
    tr = _row_tile(R, 256 if C > 1024 else 512)
    bc1 = 1.0 - ADAM_B1 ** ADAM_STEP
    bc2 = 1.0 - ADAM_B2 ** ADAM_STEP

    def body(w_ref, g_ref, m_ref, v_ref, d_ref, nm_ref, nv_ref):
        gv = g_ref[...]
        nm = ADAM_B1 * m_ref[...] + (1.0 - ADAM_B1) * gv
        nv = ADAM_B2 * v_ref[...] + (1.0 - ADAM_B2) * (gv * gv)
        nm_ref[...] = nm
        nv_ref[...] = nv
        d_ref[...] = -ADAM_LR * ((nm / bc1) / (jnp.sqrt(nv / bc2) + ADAM_EPS) + ADAM_WD * w_ref[...])

    blk = pl.BlockSpec((tr, C), lambda i: (i, 0))
    sd = jax.ShapeDtypeStruct((R, C), F32)
    return pl.pallas_call(
        body, name=name, grid=(R // tr,),
        in_specs=[blk, blk, blk, blk], out_specs=[blk, blk, blk], out_shape=[sd, sd, sd],
        compiler_params=_cp(("arbitrary",)),
    )(w, g, m, v)


def _as2d(a):
    if a.ndim == 1:
        return a.reshape(1, -1)
    return a.reshape(-1, a.shape[-1])


def adamw_nd(w, g, m, v, name):
    outs = adamw(_as2d(w), _as2d(g.reshape(w.shape)), _as2d(m), _as2d(v), name)
    return tuple(o.reshape(w.shape) for o in outs)


def _place():
    x, y, c = lax.axis_index("x"), lax.axis_index("y"), lax.axis_index("c")
    chips = [(1 - x, y), (x, 1 - y), (1 - x, 1 - y)]
    return x, y, c, chips


def allgather_small(xs, name):
    m_per, n = xs.shape

    def body(x_ref, out_ref, send_sems, recv_sems, local_sem):
        x, y, c, chips = _place()
        me, sibling = (x, y, c), (x, y, 1 - c)

        def rows(px, py, pc):
            return out_ref.at[pl.ds((4 * px + 2 * py + pc) * m_per, m_per), :]

        def copy(k, block, to, src=None):
            return pltpu.make_async_remote_copy(
                src_ref=rows(*block) if src is None else src, dst_ref=rows(*block),
                send_sem=send_sems.at[k], recv_sem=recv_sems.at[k], device_id=to, device_id_type=MESH_T)

        mine = pltpu.make_async_copy(x_ref, rows(*me), local_sem)
        mine.start()
        first = [copy(0, me, sibling, src=x_ref)]
        first += [copy(1 + j, me, (*chip, c), src=x_ref) for j, chip in enumerate(chips)]
        for cp in first:
            cp.start()
        passed = [copy(4 + j, (*chip, c), sibling) for j, chip in enumerate(chips)]
        for j, chip in enumerate(chips):
            copy(1 + j, (*chip, c), me).wait_recv()
            passed[j].start()
        copy(0, sibling, me).wait_recv()
        for j, chip in enumerate(chips):
            copy(4 + j, (*chip, 1 - c), me).wait_recv()
        for cp in first + passed:
            cp.wait_send()
        mine.wait()

    return pl.pallas_call(
        body, name=name,
        out_shape=jax.ShapeDtypeStruct((N_DEV * m_per, n), xs.dtype),
        in_specs=[pl.BlockSpec(memory_space=pltpu.VMEM)],
        out_specs=pl.BlockSpec(memory_space=pltpu.VMEM),
        scratch_shapes=[pltpu.SemaphoreType.DMA((7,)), pltpu.SemaphoreType.DMA((7,)), pltpu.SemaphoreType.DMA],
        compiler_params=_cp(),
    )(xs)


def gather_weights(shards, name):
    n = len(shards)

    def body(*refs):
        ins, outs = refs[:n], refs[n:2 * n]
        send_sems, recv_sems, fsend_sems, frecv_sems, local_sems = refs[2 * n:]
        x, y, c, chips = _place()
        me_chip = 2 * x + y
        sibling = (x, y, 1 - c)
        locs = []
        for a in range(n):
            cp = pltpu.make_async_copy(ins[a], outs[a].at[me_chip], local_sems.at[a])
            cp.start()
            locs.append(cp)

        def ici(a, k, src_chip, to):
            return pltpu.make_async_remote_copy(
                src_ref=ins[a].at[c], dst_ref=outs[a].at[src_chip, c],
                send_sem=send_sems.at[a * 3 + k], recv_sem=recv_sems.at[a * 3 + k], device_id=to, device_id_type=MESH_T)

        def d2d(a, k, src_chip, half):
            return pltpu.make_async_remote_copy(
                src_ref=outs[a].at[src_chip, half], dst_ref=outs[a].at[src_chip, half],
                send_sem=fsend_sems.at[a * 3 + k], recv_sem=frecv_sems.at[a * 3 + k],
                device_id=sibling, device_id_type=MESH_T)

        firsts = []
        for k, (cx, cy) in enumerate(chips):
            for a in range(n):
                cp = ici(a, k, me_chip, (cx, cy, c))
                cp.start()
                firsts.append(cp)
        passed = []
        for k, (cx, cy) in enumerate(chips):
            for a in range(n):
                ici(a, k, 2 * cx + cy, (cx, cy, c)).wait_recv()
                cp = d2d(a, k, 2 * cx + cy, c)
                cp.start()
                passed.append(cp)
        for k, (cx, cy) in enumerate(chips):
            for a in range(n):
                d2d(a, k, 2 * cx + cy, 1 - c).wait_recv()
        for cp in firsts + passed:
            cp.wait_send()
        for cp in locs:
            cp.wait()

    return pl.pallas_call(
        body, name=name,
        out_shape=[jax.ShapeDtypeStruct((N_CHIPS,) + s.shape, s.dtype) for s in shards],
        in_specs=[ANY] * n, out_specs=[ANY] * n,
        scratch_shapes=[pltpu.SemaphoreType.DMA((3 * n,)), pltpu.SemaphoreType.DMA((3 * n,)),
                        pltpu.SemaphoreType.DMA((3 * n,)), pltpu.SemaphoreType.DMA((3 * n,)),
                        pltpu.SemaphoreType.DMA((n,))],
        compiler_params=_cp(),
    )(*shards)


def _half_rows(ref, half, rh):
    idx = (slice(None),) * (len(ref.shape) - 2) + (pl.ds(pl.multiple_of(half * rh, 16), rh), slice(None))
    return ref.at[idx]


def sibling_exchange_halves(grads, name):
    n = len(grads)

    def body(*refs):
        ins, outs = refs[:n], refs[n:2 * n]
        send_sems, recv_sems = refs[2 * n:]
        x, y, c, _ = _place()
        cps = []
        for a in range(n):
            rh = ins[a].shape[-2] // 2
            cp = pltpu.make_async_remote_copy(
                src_ref=_half_rows(ins[a], 1 - c, rh), dst_ref=outs[a],
                send_sem=send_sems.at[a], recv_sem=recv_sems.at[a], device_id=(x, y, 1 - c), device_id_type=MESH_T)
            cp.start()
            cps.append(cp)
        for cp in cps:
            cp.wait()

    return pl.pallas_call(
        body, name=name,
        out_shape=[jax.ShapeDtypeStruct(g.shape[:-2] + (g.shape[-2] // 2, g.shape[-1]), g.dtype) for g in grads],
        in_specs=[ANY] * n, out_specs=[ANY] * n,
        scratch_shapes=[pltpu.SemaphoreType.DMA((n,)), pltpu.SemaphoreType.DMA((n,))],
        compiler_params=_cp(),
    )(*grads)


def pair_sum(g, recv, cidx, name):
    L, NS, R, C = g.shape
    rh = R // 2
    tr = _row_tile(rh, 256 if C > 1024 else 512)
    nt = rh // tr

    def body(c_ref, g_ref, r_ref, o_ref):
        o_ref[...] = (g_ref[...].astype(F32) + r_ref[...].astype(F32)).astype(BF16)

    grid_spec = pltpu.PrefetchScalarGridSpec(
        num_scalar_prefetch=1, grid=(L, NS, nt),
        in_specs=[pl.BlockSpec((None, None, tr, C), lambda l, s, t, c: (l, s, c[0] * nt + t, 0)),
                  pl.BlockSpec((None, None, tr, C), lambda l, s, t, c: (l, s, t, 0))],
        out_specs=pl.BlockSpec((None, None, tr, C), lambda l, s, t, c: (l, s, t, 0)))
    return pl.pallas_call(
        body, name=name, grid_spec=grid_spec,
        out_shape=jax.ShapeDtypeStruct((L, NS, rh, C), BF16),
        compiler_params=_cp(("arbitrary", "arbitrary", "arbitrary")),
    )(cidx, g, recv)


def chip_exchange(parts, name):
    n = len(parts)

    def body(*refs):
        ins, outs = refs[:n], refs[n:2 * n]
        send_sems, recv_sems = refs[2 * n:]
        x, y, c, chips = _place()
        cps = []
        for k, (cx, cy) in enumerate(chips):
            for a in range(n):
                cp = pltpu.make_async_remote_copy(
                    src_ref=ins[a].at[:, 2 * cx + cy], dst_ref=outs[a].at[k],
                    send_sem=send_sems.at[a * 3 + k], recv_sem=recv_sems.at[a * 3 + k],
                    device_id=(cx, cy, c), device_id_type=MESH_T)
                cp.start()
                cps.append(cp)
        for cp in cps:
            cp.wait()

    return pl.pallas_call(
        body, name=name,
        out_shape=[jax.ShapeDtypeStruct((3, p.shape[0]) + p.shape[2:], p.dtype) for p in parts],
        in_specs=[ANY] * n, out_specs=[ANY] * n,
        scratch_shapes=[pltpu.SemaphoreType.DMA((3 * n,)), pltpu.SemaphoreType.DMA((3 * n,))],
        compiler_params=_cp(),
    )(*parts)


def chip_sum(part, recv, chip_c, name, dest=None, slot=0, n_slots=1):
    _, NS, RH, C = part.shape
    tr = _row_tile(RH, 256 if C > 1024 else 512)
    nt = RH // tr

    def body(cc_ref, p_ref, r_ref, *rest):
        o_ref = rest[-1]
        acc = p_ref[...].astype(F32)
        for k in range(3):
            acc = acc + r_ref[k].astype(F32)
        o_ref[...] = acc

    in_specs = [pl.BlockSpec((None, None, tr, C), lambda t, cc: (0, cc[0], t, 0)),
                pl.BlockSpec((3, None, tr, C), lambda t, cc: (0, 0, t, 0))]
    args = [chip_c, part, recv]
    aliases = {}
    if dest is not None:
        in_specs.append(ANY)
        args.append(dest)
        aliases = {3: 0}
    grid_spec = pltpu.PrefetchScalarGridSpec(
        num_scalar_prefetch=1, grid=(nt,), in_specs=in_specs,
        out_specs=pl.BlockSpec((None, tr, C), lambda t, cc: (slot, cc[1] * nt + t, 0)))
    return pl.pallas_call(
        body, name=name, grid_spec=grid_spec,
        out_shape=jax.ShapeDtypeStruct((n_slots, 2 * RH, C), F32),
        input_output_aliases=aliases,
        compiler_params=_cp(("arbitrary",)),
    )(*args)


def sibling_fill_halves(bufs, name):
    n = len(bufs)

    def body(*refs):
        outs = refs[n:2 * n]
        send_sems, recv_sems = refs[2 * n:]
        x, y, c, _ = _place()
        cps = []
        for a in range(n):
            rh = outs[a].shape[-2] // 2
            cp = pltpu.make_async_remote_copy(
                src_ref=_half_rows(outs[a], c, rh), dst_ref=_half_rows(outs[a], c, rh),
                send_sem=send_sems.at[a], recv_sem=recv_sems.at[a], device_id=(x, y, 1 - c), device_id_type=MESH_T)
            cp.start()
            cps.append(cp)
        for a, cp in enumerate(cps):
            cp.wait_send()
        for a in range(n):
            rh = outs[a].shape[-2] // 2
            pltpu.make_async_remote_copy(
                src_ref=_half_rows(outs[a], 1 - c, rh), dst_ref=_half_rows(outs[a], 1 - c, rh),
                send_sem=send_sems.at[a], recv_sem=recv_sems.at[a], device_id=(x, y, 1 - c),
                device_id_type=MESH_T).wait_recv()

    return pl.pallas_call(
        body, name=name,
        out_shape=[jax.ShapeDtypeStruct(b.shape, b.dtype) for b in bufs],
        in_specs=[ANY] * n, out_specs=[ANY] * n,
        input_output_aliases={a: a for a in range(n)},
        scratch_shapes=[pltpu.SemaphoreType.DMA((n,)), pltpu.SemaphoreType.DMA((n,))],
        compiler_params=_cp(),
    )(*bufs)


HBM_SPEC = pl.BlockSpec(memory_space=pltpu.HBM)
SEM_SPEC = pl.BlockSpec(memory_space=pltpu.SEMAPHORE)
DATAFLOW_EFFECT = pltpu.SideEffectType.DATAFLOW_SIDE_EFFECTING


def split_start(bufs, copies_fn, n_copies, name, dep=None):
    nb = len(bufs)
    extra = [] if dep is None else [dep]

    def body(*refs):
        ne = len(extra)
        send_sems, recv_sems = refs[nb + ne], refs[nb + ne + 1]
        token = refs[-1]
        for cp in copies_fn(refs[:nb], send_sems, recv_sems):
            cp.start()
        token[...] = jnp.zeros_like(token)

    outs = pl.pallas_call(
        body, name=name,
        out_shape=(pltpu.SemaphoreType.DMA((n_copies,)), pltpu.SemaphoreType.DMA((n_copies,)),
                   *[pltpu.HBM(b.shape, b.dtype) for b in bufs], jax.ShapeDtypeStruct((8, 128), F32)),
        in_specs=[HBM_SPEC] * nb + [ANY] * len(extra),
        out_specs=(SEM_SPEC, SEM_SPEC, *[HBM_SPEC] * nb, pl.BlockSpec(memory_space=pltpu.VMEM)),
        input_output_aliases={i: 2 + i for i in range(nb)},
        compiler_params=pltpu.CompilerParams(has_side_effects=DATAFLOW_EFFECT),
    )(*[pltpu.with_memory_space_constraint(b, pltpu.HBM) for b in bufs], *extra)
    return outs[0], outs[1], list(outs[2:2 + nb]), outs[-1]


def split_wait(send_sems, recv_sems, bufs, after, copies_fn, name):
    nb = len(bufs)

    def body(*refs):
        for cp in copies_fn(refs[:nb], refs[nb], refs[nb + 1]):
            cp.wait_send()
            cp.wait_recv()

    outs = pl.pallas_call(
        body, name=name,
        out_shape=tuple(pltpu.HBM(b.shape, b.dtype) for b in bufs),
        in_specs=[HBM_SPEC] * nb + [SEM_SPEC, SEM_SPEC, ANY],
        out_specs=tuple([HBM_SPEC] * nb),
        input_output_aliases={i: i for i in range(nb)},
        compiler_params=pltpu.CompilerParams(has_side_effects=DATAFLOW_EFFECT),
    )(*bufs, send_sems, recv_sems, after)
    return list(outs)


def chip_exchange_copies(refs, send_sems, recv_sems):
    n = len(refs) // 2
    x, y, c, chips = _place()
    cps = []
    for a in range(n):
        for k, (cx, cy) in enumerate(chips):
            cps.append(pltpu.make_async_remote_copy(
                src_ref=refs[a].at[:, 2 * cx + cy], dst_ref=refs[n + a].at[k],
                send_sem=send_sems.at[a * 3 + k], recv_sem=recv_sems.at[a * 3 + k],
                device_id=(cx, cy, c), device_id_type=MESH_T))
    return cps


def weight_ici_copies(refs, send_sems, recv_sems):
    x, y, c, chips = _place()
    cps = []
    for a in range(len(refs)):
        for k, (cx, cy) in enumerate(chips):
            cps.append(pltpu.make_async_remote_copy(
                src_ref=refs[a].at[2 * x + y, c], dst_ref=refs[a].at[2 * x + y, c],
                send_sem=send_sems.at[a * 3 + k], recv_sem=recv_sems.at[a * 3 + k],
                device_id=(cx, cy, c), device_id_type=MESH_T))
    return cps


def weight_d2d_copies(refs, send_sems, recv_sems):
    x, y, c, chips = _place()
    cps = []
    for a in range(len(refs)):
        for k, (cx, cy) in enumerate(chips):
            cps.append(pltpu.make_async_remote_copy(
                src_ref=refs[a].at[2 * cx + cy, c], dst_ref=refs[a].at[2 * cx + cy, c],
                send_sem=send_sems.at[a * 3 + k], recv_sem=recv_sems.at[a * 3 + k],
                device_id=(x, y, 1 - c), device_id_type=MESH_T))
    return cps


def cast_place(w, lead, chip_arr, name, dep=None):
    R, C = w.shape[-2:]
    tr = _row_tile(R, 256 if C > 1024 else 512)

    def body(chip_ref, w_ref, *rest):
        rest[-1][...] = w_ref[...].astype(BF16)

    in_specs = [pl.BlockSpec((None,) * len(lead) + (tr, C), lambda t, ch: (*lead, t, 0))]
    args = [chip_arr, w]
    if dep is not None:
        in_specs.append(ANY)
        args.append(dep)
    grid_spec = pltpu.PrefetchScalarGridSpec(
        num_scalar_prefetch=1, grid=(R // tr,), in_specs=in_specs,
        out_specs=pl.BlockSpec((None, tr, C), lambda t, ch: (ch[0], t, 0)))
    return pl.pallas_call(
        body, name=name, grid_spec=grid_spec,
        out_shape=jax.ShapeDtypeStruct((N_CHIPS, R, C), BF16),
        compiler_params=_cp(("arbitrary",)),
    )(*args)


def _s5_prepare(lam_re, lam_im, log_dt, b_re, b_im, c_re, c_im, groups_per_block):
    G, P = lam_re.shape
    N = b_re.shape[-1]
    gb = groups_per_block
    nblk = G // gb
    dt = jnp.exp(log_dt)[:, None]
    e = jnp.exp(lam_re * dt)
    a_re = e * jnp.cos(lam_im * dt)
    a_im = e * jnp.sin(lam_im * dt)
    n2 = lam_re * lam_re + lam_im * lam_im
    co_re = ((a_re - 1.0) * lam_re + a_im * lam_im) / n2
    co_im = (a_im * lam_re - (a_re - 1.0) * lam_im) / n2
    bb_re = co_re[..., None] * b_re - co_im[..., None] * b_im
    bb_im = co_re[..., None] * b_im + co_im[..., None] * b_re
    eye = jnp.eye(gb, dtype=F32)

    def blockdiag_np(m):
        m = m.reshape(nblk, gb, N, P)
        return jnp.einsum('bgnp,gh->bgnhp', m, eye).reshape(nblk, gb * N, gb * P)

    b_np_re = jnp.swapaxes(bb_re, 1, 2)
    b_np_im = jnp.swapaxes(bb_im, 1, 2)
    return (a_re.reshape(nblk, 1, gb * P), a_im.reshape(nblk, 1, gb * P),
            blockdiag_np(b_np_re), blockdiag_np(b_np_im), blockdiag_np(c_re), blockdiag_np(c_im))


def _to_scan_order(a):
    S, D = a.shape
    return a.reshape(SCAN_LANES, S // SCAN_LANES, D).transpose(1, 0, 2).reshape(S, D)


def _from_scan_order(a):
    S, D = a.shape
    return a.reshape(S // SCAN_LANES, SCAN_LANES, D).transpose(1, 0, 2).reshape(S, D)


def _pad_rows(a, mult=8):
    r = (-a.shape[0]) % mult
    if r:
        a = jnp.concatenate([a, jnp.zeros((r, a.shape[1]), a.dtype)], axis=0)
    return a


def _pack_rows(arrs, width):
    parts, offs, o = [], [], 0
    for a in arrs:
        flat = a.reshape(-1)
        r = (-flat.shape[0]) % (16 * width)
        if r:
            flat = jnp.concatenate([flat, jnp.zeros((r,), flat.dtype)])
        p = flat.reshape(-1, width)
        parts.append(p)
        offs.append((o, a.shape, a.size))
        o += p.shape[0]
    if o % 64:
        parts.append(jnp.zeros((64 - o % 64, width), parts[0].dtype))
    return jnp.concatenate(parts, axis=0), offs


def _unpack_rows(packed, offs):
    outs = []
    for o, shape, size in offs:
        rows = -(-size // packed.shape[1])
        outs.append(packed[o:o + rows].reshape(-1)[:size].reshape(shape))
    return outs


def kernel(x, c, ada_w, ada_b, norm_pre, norm_post, ffn_w_in, ffn_w_out, ab_w_in, pool_w, pool_scale, sgu_ln_g, sgu_ln_b, sgu_w, sgu_b, ab_w_out, ssm_w_in, ssm_lam_re, ssm_lam_im, ssm_b_re, ssm_b_im, ssm_c_re, ssm_c_im, ssm_d, ssm_log_dt, ssm_w_glu, loss_target, m_ada_w, m_ada_b, m_norm_pre, m_norm_post, m_ffn_w_in, m_ffn_w_out, m_ab_w_in, m_pool_w, m_pool_scale, m_sgu_ln_g, m_sgu_ln_b, m_sgu_w, m_sgu_b, m_ab_w_out, m_ssm_w_in, m_ssm_lam_re, m_ssm_lam_im, m_ssm_b_re, m_ssm_b_im, m_ssm_c_re, m_ssm_c_im, m_ssm_d, m_ssm_log_dt, m_ssm_w_glu, v_ada_w, v_ada_b, v_norm_pre, v_norm_post, v_ffn_w_in, v_ffn_w_out, v_ab_w_in, v_pool_w, v_pool_scale, v_sgu_ln_g, v_sgu_ln_b, v_sgu_w, v_sgu_b, v_ab_w_out, v_ssm_w_in, v_ssm_lam_re, v_ssm_lam_im, v_ssm_b_re, v_ssm_b_im, v_ssm_c_re, v_ssm_c_im, v_ssm_d, v_ssm_log_dt, v_ssm_w_glu):
    weights = dict(ada_w=ada_w, ada_b=ada_b, norm_pre=norm_pre, norm_post=norm_post, ffn_w_in=ffn_w_in,
                   ffn_w_out=ffn_w_out, ab_w_in=ab_w_in, pool_w=pool_w, pool_scale=pool_scale, sgu_ln_g=sgu_ln_g,
                   sgu_ln_b=sgu_ln_b, sgu_w=sgu_w, sgu_b=sgu_b, ab_w_out=ab_w_out, ssm_w_in=ssm_w_in,
                   ssm_lam_re=ssm_lam_re, ssm_lam_im=ssm_lam_im, ssm_b_re=ssm_b_re, ssm_b_im=ssm_b_im,
                   ssm_c_re=ssm_c_re, ssm_c_im=ssm_c_im, ssm_d=ssm_d, ssm_log_dt=ssm_log_dt, ssm_w_glu=ssm_w_glu)
    m_in = dict(ada_w=m_ada_w, ada_b=m_ada_b, norm_pre=m_norm_pre, norm_post=m_norm_post, ffn_w_in=m_ffn_w_in,
                ffn_w_out=m_ffn_w_out, ab_w_in=m_ab_w_in, pool_w=m_pool_w, pool_scale=m_pool_scale,
                sgu_ln_g=m_sgu_ln_g, sgu_ln_b=m_sgu_ln_b, sgu_w=m_sgu_w, sgu_b=m_sgu_b, ab_w_out=m_ab_w_out,
                ssm_w_in=m_ssm_w_in, ssm_lam_re=m_ssm_lam_re, ssm_lam_im=m_ssm_lam_im, ssm_b_re=m_ssm_b_re,
                ssm_b_im=m_ssm_b_im, ssm_c_re=m_ssm_c_re, ssm_c_im=m_ssm_c_im, ssm_d=m_ssm_d,
                ssm_log_dt=m_ssm_log_dt, ssm_w_glu=m_ssm_w_glu)
    v_in = dict(ada_w=v_ada_w, ada_b=v_ada_b, norm_pre=v_norm_pre, norm_post=v_norm_post, ffn_w_in=v_ffn_w_in,
                ffn_w_out=v_ffn_w_out, ab_w_in=v_ab_w_in, pool_w=v_pool_w, pool_scale=v_pool_scale,
                sgu_ln_g=v_sgu_ln_g, sgu_ln_b=v_sgu_ln_b, sgu_w=v_sgu_w, sgu_b=v_sgu_b, ab_w_out=v_ab_w_out,
                ssm_w_in=v_ssm_w_in, ssm_lam_re=v_ssm_lam_re, ssm_lam_im=v_ssm_lam_im, ssm_b_re=v_ssm_b_re,
                ssm_b_im=v_ssm_b_im, ssm_c_re=v_ssm_c_re, ssm_c_im=v_ssm_c_im, ssm_d=v_ssm_d,
                ssm_log_dt=v_ssm_log_dt, ssm_w_glu=v_ssm_w_glu)
    names = list(weights.keys())

    xi, yi, ci = lax.axis_index("x"), lax.axis_index("y"), lax.axis_index("c")
    chip = 2 * xi + yi
    me = 4 * xi + 2 * yi + ci
    S, D = x.shape[1], x.shape[2]
    L = ada_w.shape[0]
    NSUB = norm_pre.shape[1]
    DS = norm_pre.shape[2]
    FS = ffn_w_in.shape[-1]
    FR = ffn_w_out.shape[-2]
    x0 = x[0]
    target = loss_target[0]

    small_parts = [_pad_rows(p) for p in (c.reshape(D // DS, DS), norm_pre.reshape(L * NSUB, DS),
                                          norm_post.reshape(L * NSUB, DS), ssm_d.reshape(1, DS))]
    small_offs = [0]
    for p in small_parts:
        small_offs.append(small_offs[-1] + p.shape[0])
    small_all = allgather_small(jnp.concatenate(small_parts, axis=0), "ag_small").reshape(N_DEV, -1, DS)
    c_all = small_all[:, :D // DS].reshape(N_DEV, D)
    per_chip = small_all[0::2]
    o = small_offs[1]
    g_pre_full = jnp.moveaxis(per_chip[:, o:o + L * NSUB], 0, 1).reshape(L, NSUB, D)
    o = small_offs[2]
    g_post_full = jnp.moveaxis(per_chip[:, o:o + L * NSUB], 0, 1).reshape(L, NSUB, D)
    o = small_offs[3]
    d_full = jnp.moveaxis(per_chip[:, o:o + 1], 0, 1).reshape(1, D)

    pieces = []
    for l in range(L):
        pieces.append((f"ffn_{l}_0", [(ffn_w_in, (l, 0)), (ffn_w_out, (l, 0))]))
        if l % 2 == 0:
            pieces.append((f"mix_{l}", [(ab_w_in, (l // 2,)), (ab_w_out, (l // 2,))]))
        else:
            pieces.append((f"mix_{l}", [(ssm_w_in, (l // 2,)), (ssm_w_glu, (l // 2,))]))
        pieces.append((f"ffn_{l}_1", [(ffn_w_in, (l, 1)), (ffn_w_out, (l, 1))]))
    tags = [tag for tag, _ in pieces]
    chip_arr = chip.reshape(1).astype(jnp.int32)
    wg_started, wg_passing = {}, {}

    def start_piece(tag, ws, dep, cast_dep):
        bufs = [cast_place(w, lead, chip_arr, f"wg_cast_{tag}_{i}", dep=cast_dep).reshape(
                    N_CHIPS, 2, w.shape[-2] // 2, w.shape[-1]) for i, (w, lead) in enumerate(ws)]
        wg_started[tag] = split_start(bufs, weight_ici_copies, 3 * len(bufs), f"wg_ici_start_{tag}", dep=dep)
        return wg_started[tag][3][0, 0]

    def weights_arrived(i, after):
        ssem, rsem, bufs, _ = wg_started[tags[i]]
        lands = split_wait(ssem, rsem, bufs, after, weight_ici_copies, f"wg_ici_wait_{tags[i]}")
        wg_passing[i] = split_start(lands, weight_d2d_copies, 3 * len(lands), f"wg_d2d_start_{tags[i]}")
        return wg_passing[i][3][0, 0]

    def weights_of(i, after):
        ssem, rsem, lands, _ = wg_passing[i]
        return split_wait(ssem, rsem, lands, after, weight_d2d_copies, f"wg_d2d_wait_{tags[i]}")

    NS = ada_w.shape[-1]
    ada_b_shard = lax.dynamic_slice_in_dim(ada_b, chip * NS, NS, axis=1).reshape(L, 1, NS)
    mod_part = ada_mod(c_all, ada_w, ada_b_shard, "ada_mod")
    mod_all = allgather_small(mod_part.reshape(L * N_DEV, NS), "ag_mod").reshape(N_DEV, L, N_DEV, NS)
    mod_mine = lax.dynamic_index_in_dim(mod_all[0::2], me, axis=2, keepdims=False)
    mod = jnp.moveaxis(mod_mine, 0, 1).reshape(L, NSUB, 3, D)
    wg_token = 0.0
    for tag, ws in pieces:
        wg_token = wg_token + start_piece(tag, ws, mod_all, wg_started[tags[0]][3] if wg_started else None)

    def vec_of(l, s):
        return jnp.concatenate([mod[l, s], g_pre_full[l, s][None], g_post_full[l, s][None],
                                jnp.zeros((3, D), F32)], axis=0)

    GB = 8
    s5_args = (ssm_lam_re[0], ssm_lam_im[0], ssm_log_dt[0], ssm_b_re[0], ssm_b_im[0], ssm_c_re[0], ssm_c_im[0])
    (a_re, a_im, bblk_re, bblk_im, cblk_re, cblk_im), s5_vjp = jax.vjp(lambda *p: _s5_prepare(*p, GB), *s5_args)
    bre16, bim16 = bblk_re.astype(BF16), bblk_im.astype(BF16)
    cre16, cim16 = cblk_re.astype(BF16), cblk_im.astype(BF16)
    cre16_t, cim16_t = jnp.swapaxes(cre16, 1, 2), jnp.swapaxes(cim16, 1, 2)

    pool_w16 = pool_w[0].astype(BF16)
    sgu_bexp = jnp.broadcast_to(sgu_b[0][:, :, None], sgu_w[0].shape)

    saved = {}
    ffn_w = {}
    xcur = x0
    stage = [0]
    first_pass = weights_arrived(0, vec_of(0, 0) + wg_token)

    def next_weights(after, vec):
        i = stage[0]
        stage[0] += 1
        g = weights_of(i, after)
        if i + 1 < len(tags):
            vec = vec + weights_arrived(i + 1, after)
        return g, vec

    for l in range(L):
        v0 = vec_of(l, 0)
        g, v0p = next_weights(xcur if l else v0 + first_pass, v0)
        ffn_w[(l, 0)] = (g[0].reshape(N_CHIPS, D, FS), g[1].reshape(N_CHIPS, FR, D))
        out, h, u, f = ffn_fwd(xcur, v0p, *ffn_w[(l, 0)], f"ffn_fwd_{l}_0")
        saved[(l, 0)] = (xcur, v0, h, u, f)
        xcur = out
        v1 = vec_of(l, 1)
        mix_g, v1p = next_weights(xcur, v1)
        if l % 2 == 0:
            abin_g = mix_g[0].reshape(N_CHIPS, D, -1)
            about_g = mix_g[1].reshape(-1, D)
            h, z = pre_matmul(xcur, v1p, abin_g, f"mixa_in_{l}")
            ycat = mixa_core_fwd(z, pool_w16, pool_scale, sgu_ln_g, sgu_ln_b, sgu_w[0], sgu_bexp, f"mixa_core_{l}")
            out, f = matmul_post(ycat, about_g, xcur, v1, f"mixa_out_{l}")
            saved[(l, 1)] = (xcur, v1, h, z, ycat, f)
        else:
            sin_g = mix_g[0].reshape(1, -1, D)
            glu_g = mix_g[1].reshape(N_CHIPS, D, -1)
            h, uu = pre_matmul(xcur, v1p, sin_g, f"s5_in_{l}")
            us = _to_scan_order(uu)
            ys = s5_core_fwd(us, a_re, a_im, bre16, bim16, cre16_t, cim16_t, d_full, f"s5_core_{l}")
            yy = _from_scan_order(ys)
            out, f = s5_glu_fwd(yy, glu_g, xcur, v1, f"s5_glu_{l}")
            saved[(l, 1)] = (xcur, v1, h, us, yy, f)
        xcur = out
        v2 = vec_of(l, 2)
        g, v2p = next_weights(xcur, v2)
        ffn_w[(l, 1)] = (g[0].reshape(N_CHIPS, D, FS), g[1].reshape(N_CHIPS, FR, D))
        out, h, u, f = ffn_fwd(xcur, v2p, *ffn_w[(l, 1)], f"ffn_fwd_{l}_1")
        saved[(l, 2)] = (xcur, v2, h, u, f)
        xcur = out

    dcur, sq = loss_head(xcur, target, "loss_head")
    loss = lax.psum(sq[0, 0], ("x", "y", "c")) * (0.5 / D)

    accs = {}
    small_g = {}
    cidx = ci.reshape(1).astype(jnp.int32)
    chip_c = jnp.stack([chip, ci]).astype(jnp.int32)
    rs_pending = []

    def rs_begin(tag, items):
        arrs = [it[0] for it in items]
        recv_a = sibling_exchange_halves(arrs, f"rs_pair_exchange_{tag}")
        parts = [pair_sum(a, r, cidx, f"rs_pair_sum_{tag}_{i}") for i, (a, r) in enumerate(zip(arrs, recv_a))]
        lands = [lax.empty((3, 1) + p.shape[2:], BF16) for p in parts]
        started = split_start(parts + lands, chip_exchange_copies, 3 * len(parts), f"rs_chip_start_{tag}")
        rs_pending.append((tag, items, started))
        pin[0] = started[3][0, 0]

    pin = [zero_after(loss.reshape(1, 1), "after_loss")[0, 0]]

    def pinned(v):
        return v if pin[0] is None else v + pin[0]

    def ffn_back(l, s, k, dcur, before_wgrads=None):
        xin, vv, h, u, f = saved[(l, s)]
        vv = pinned(vv)
        dx, df, du, act, acc = ffn_bwd(dcur, xin, f, u, vv, *ffn_w[(l, k)], f"ffn_bwd_{l}_{k}")
        accs[(l, s)] = acc
        dep = None if before_wgrads is None else before_wgrads()
        g_win = tn_matmul(h, du, 512, f"ffn_dwin_{l}_{k}", dep=dep)
        g_wout = tn_matmul_cols(act, df, FS, D, f"ffn_dwout_{l}_{k}")
        rs_begin(f"ffn_{l}_{k}", [(g_win[None], "ffn_w_in", 2 * l + k, 2 * L),
                                  (g_wout.reshape(1, N_CHIPS, FR, D), "ffn_w_out", 2 * l + k, 2 * L)])
        return dx

    grads = {}

    def small_grads():
        acc_all = jnp.stack([jnp.stack([accs[(l, s)] for s in range(NSUB)]) for l in range(L)])
        dmod_mine = acc_all[:, :, A_SHIFT:A_GATE + 1].reshape(L, NSUB * 3 * D)
        dgpre_mine = acc_all[:, :, A_GPRE]
        dgpost_mine = acc_all[:, :, A_GPOST]
        small_names = ["pool_w", "pool_scale", "sgu_ln_g", "sgu_ln_b", "sgu_w", "sgu_b", "ssm_lam_re", "ssm_lam_im",
                       "ssm_b_re", "ssm_b_im", "ssm_c_re", "ssm_c_im", "ssm_log_dt"]
        packed, offs = _pack_rows([dmod_mine, dgpre_mine, dgpost_mine, dd_mine] + [small_g[n] for n in small_names], D)
        rows = packed.shape[0]
        packed_all = allgather_small(pinned(packed).astype(BF16), "ag_grads").reshape(N_DEV, rows, D)
        summed = _unpack_rows(sum_leading(packed_all, "sum_small"), offs)
        grads.update({n: g for n, g in zip(small_names, summed[4:])})
        grads["ada_b"] = summed[0]
        grads["norm_pre"] = lax.dynamic_slice_in_dim(summed[1], chip * DS, DS, axis=2)
        grads["norm_post"] = lax.dynamic_slice_in_dim(summed[2], chip * DS, DS, axis=2)
        grads["ssm_d"] = lax.dynamic_slice_in_dim(summed[3], chip * DS, DS, axis=1)
        dmod_all = jnp.moveaxis(packed_all[:, :L * NSUB * 3].reshape(N_DEV, L, NSUB * 3 * D), 0, 1)
        dmod_shard = lax.dynamic_slice_in_dim(dmod_all, chip * NS, NS, axis=2)
        grads["ada_w"] = ada_grad(c_all, dmod_shard, "ada_grad")
        return packed_all

    for l in reversed(range(L)):
        dcur = ffn_back(l, 2, 1, dcur)
        if l % 2 == 0:
            xin, vv, h, z, ycat, f = saved[(l, 1)]
            vv = pinned(vv)
            df, dact, acc_post = post_bwd_matmul(dcur, f, vv, about_g, f"mixa_out_bwd_{l}")
            g_about = tn_matmul_cols(ycat, df, 512, D, f"mixa_dwout_{l}")
            dz, dpw, dvecw, dsw, dsb = mixa_core_bwd(z, dact, pool_w16, pool_scale, sgu_ln_g, sgu_ln_b, sgu_w[0],
                                                     sgu_bexp, f"mixa_core_bwd_{l}")
            g_abin = tn_matmul_cols(h, dz, 512, abin_g.shape[-1], f"mixa_dwin_{l}")
            dcur, acc_pre = matmul_pre_bwd(dz, abin_g, xin, dcur, vv, f"mixa_in_bwd_{l}")
            accs[(l, 1)] = acc_pre + acc_post
            rs_begin(f"mixa_{l}", [(g_abin[None], "ab_w_in", 0, 1),
                                   (g_about.reshape(1, N_CHIPS, -1, D), "ab_w_out", 0, 1)])
            small_g.update(pool_w=dpw[None], pool_scale=dvecw[0:1], sgu_ln_g=dvecw[1:2], sgu_ln_b=dvecw[2:3],
                           sgu_w=dsw[None], sgu_b=dsb.T[None])
        else:
            xin, vv, h, us, yy, f = saved[(l, 1)]
            vv = pinned(vv)
            dy, dab, gact, acc_post = s5_glu_bwd(dcur, f, yy, vv, glu_g, f"s5_glu_bwd_{l}")
            g_glu = tn_matmul_cols(gact, dab, 512, glu_g.shape[-1], f"s5_dwglu_{l}")
            dys = _to_scan_order(dy)
            dus, dbre, dbim, dcre_t, dcim_t, dar, dai, dd = s5_core_bwd(
                us, dys, a_re, a_im, bre16, bim16, cre16, cim16, d_full, f"s5_core_bwd_{l}")
            du = _from_scan_order(dus).astype(BF16)
            g_sin = tn_matmul_cols(h, du, 512, D, f"s5_dwin_{l}")
            dcur, acc_pre = matmul_pre_bwd(du, sin_g, xin, dcur, vv, f"s5_in_bwd_{l}")
            accs[(l, 1)] = acc_pre + acc_post
            rs_begin(f"s5_{l}", [(g_sin.reshape(1, N_CHIPS, -1, D), "ssm_w_in", 0, 1), (g_glu[None], "ssm_w_glu", 0, 1)])
            s5_grads = s5_vjp((dar, dai, dbre, dbim, jnp.swapaxes(dcre_t, 1, 2), jnp.swapaxes(dcim_t, 1, 2)))
            small_g.update(ssm_lam_re=s5_grads[0][None], ssm_lam_im=s5_grads[1][None], ssm_log_dt=s5_grads[2][None],
                           ssm_b_re=s5_grads[3][None], ssm_b_im=s5_grads[4][None], ssm_c_re=s5_grads[5][None],
                           ssm_c_im=s5_grads[6][None])
            dd_mine = dd
        dcur = ffn_back(l, 0, 0, dcur, before_wgrads=small_grads if l == 0 else None)
    grad_x = dcur[None]

    big_names = ["ffn_w_in", "ffn_w_out", "ab_w_in", "ab_w_out", "ssm_w_in", "ssm_w_glu"]
    deltas, new_m, new_v = {}, {}, {}

    def update(n):
        grads[n] = grads[n].reshape(weights[n].shape)
        deltas[n], new_m[n], new_v[n] = adamw_nd(weights[n], grads[n], m_in[n], v_in[n], f"adamw_{n}")

    for n in names:
        if n not in big_names:
            update(n)

    fin = {}
    after = pinned(deltas["ssm_log_dt"]) + deltas["ada_w"][0, 0, 0]
    for tag, items, (ssem, rsem, bufs, _) in rs_pending:
        bufs = split_wait(ssem, rsem, bufs, after, chip_exchange_copies, f"rs_chip_wait_{tag}")
        n = len(items)
        for i, (it, p, r) in enumerate(zip(items, bufs[:n], bufs[n:])):
            fin[it[1]] = chip_sum(p, r, chip_c, f"rs_chip_sum_{tag}_{i}", dest=fin.get(it[1]), slot=it[2],
                                  n_slots=it[3])
    filled = sibling_fill_halves([fin[n] for n in big_names], "rs_fill")
    for n, g in zip(big_names, filled):
        grads[n] = g
        update(n)

    return (loss, grad_x, *[grads[n] for n in names], *[deltas[n] for n in names],
            *[new_m[n] for n in names], *[new_v[n] for n in names])
```

```python
import math

import jax
import jax.numpy as jnp
from jax import lax
from jax.experimental import pallas as pl
from jax.experimental.pallas import tpu as pltpu

F32 = jnp.float32
BF16 = jnp.bfloat16
EPS = 1e-6
MESH_T = pl.DeviceIdType.MESH
VMEM_LIMIT_BYTES = 56 * 1024 * 1024
N_CHIPS = 4
N_DEV = 8
POOL_WINDOWS = (2, 4, 8, 16)
CHUNK = 128
SCAN_LANES = 8
SCAN_UNROLL = 4
ADAM_LR = 0.001
ADAM_B1 = 0.9
ADAM_B2 = 0.999
ADAM_EPS = 1e-08
ADAM_WD = 0.01
ADAM_STEP = 10
GELU_C = math.sqrt(2.0 / math.pi)
GELU_K = 0.044715

V_SHIFT, V_SCALE, V_GATE, V_GPRE, V_GPOST = 0, 1, 2, 3, 4
A_SHIFT, A_SCALE, A_GATE, A_GPRE, A_GPOST = 0, 1, 2, 3, 4

ANY = pl.BlockSpec(memory_space=pl.ANY)


def _cp(sem=None):
    if sem is None:
        return pltpu.CompilerParams(vmem_limit_bytes=VMEM_LIMIT_BYTES)
    return pltpu.CompilerParams(vmem_limit_bytes=VMEM_LIMIT_BYTES, dimension_semantics=sem)


def _dot(a, b):
    return jnp.dot(a, b, preferred_element_type=F32)


def _dot_nt(a, b):
    return lax.dot_general(a, b, (((1,), (1,)), ((), ())), preferred_element_type=F32)


def _dot_tn(a, b):
    return lax.dot_general(a, b, (((0,), (0,)), ((), ())), preferred_element_type=F32)


def _sigmoid(x):
    return 1.0 / (1.0 + jnp.exp(-x))


def _gelu(x):
    return 0.5 * x * (1.0 + jnp.tanh(GELU_C * (x + GELU_K * x * x * x)))


def _gelu_grad(x):
    t = jnp.tanh(GELU_C * (x + GELU_K * x * x * x))
    return 0.5 * (1.0 + t) + 0.5 * x * (1.0 - t * t) * GELU_C * (1.0 + 3.0 * GELU_K * x * x)


def _rowsum(v):
    return jnp.sum(v, axis=0, keepdims=True)


def _lanemean(v):
    return jnp.mean(v, axis=-1, keepdims=True)


def _row(ref, i):
    return ref[pl.ds(i, 1), :]


def _pre_fwd(x, vec_ref):
    r = lax.rsqrt(_lanemean(x * x) + EPS)
    return (x * r) * _row(vec_ref, V_GPRE) * (1.0 + _row(vec_ref, V_SCALE)) + _row(vec_ref, V_SHIFT)


def _pre_bwd(x, dh, vec_ref):
    g = _row(vec_ref, V_GPRE)
    sc = 1.0 + _row(vec_ref, V_SCALE)
    r = lax.rsqrt(_lanemean(x * x) + EPS)
    xn = x * r
    dhx = dh * xn
    t2 = dh * (g * sc)
    dx = r * (t2 - xn * _lanemean(t2 * xn))
    return dx, _rowsum(dh), _rowsum(dhx * g), _rowsum(dhx * sc)


def _post_fwd(x, f, vec_ref, rw):
    q = lax.rsqrt(_lanemean(f * f) + EPS)
    return x + (rw * _row(vec_ref, V_GATE)) * (f * q * _row(vec_ref, V_GPOST))


def _post_bwd(dout, f, vec_ref, rw):
    gp = _row(vec_ref, V_GPOST)
    gate = _row(vec_ref, V_GATE)
    q = lax.rsqrt(_lanemean(f * f) + EPS)
    fhat = f * q
    dgate = _rowsum(dout * (rw * fhat * gp))
    dy = dout * (rw * gate)
    dgpost = _rowsum(dy * fhat)
    t = dy * gp
    df = q * (t - fhat * _lanemean(t * fhat))
    return df, dgate, dgpost


def _acc_add(acc_ref, row, v):
    acc_ref[pl.ds(row, 1), :] += v


def ffn_fwd(x, vec, win_g, wout_g, name, tm=512):
    S, D = x.shape
    FS = win_g.shape[-1]
    FR = wout_g.shape[-2]

    def body(x_ref, vec_ref, win_hbm, wout_hbm, out_ref, h_ref, u_ref, f_ref, win_s, wout_s, sem):
        @pl.when(pl.program_id(0) == 0)
        def _():
            cps = [pltpu.make_async_copy(win_hbm.at[j], win_s.at[j], sem.at[j]) for j in range(N_CHIPS)]
            cps += [pltpu.make_async_copy(wout_hbm.at[j], wout_s.at[pl.ds(j * FR, FR), :], sem.at[N_CHIPS + j])
                    for j in range(N_CHIPS)]
            for cp in cps:
                cp.start()
            for cp in cps:
                cp.wait()

        xv = x_ref[...]
        h = _pre_fwd(xv, vec_ref).astype(BF16)
        h_ref[...] = h
        f = jnp.zeros((tm, D), F32)
        for j in range(2):
            a = _dot(h, win_s[j])
            b = _dot(h, win_s[2 + j])
            u_ref[j] = a.astype(BF16)
            u_ref[2 + j] = b.astype(BF16)
            act = (a * _sigmoid(a) * b).astype(BF16)
            f = f + _dot(act, wout_s[pl.ds(j * FS, FS), :])
        f_ref[...] = f
        out_ref[...] = _post_fwd(xv, f, vec_ref, 0.5)

    row = pl.BlockSpec((tm, D), lambda i: (i, 0))
    return pl.pallas_call(
        body, name=name, grid=(S // tm,),
        in_specs=[row, pl.BlockSpec((8, D), lambda i: (0, 0)), ANY, ANY],
        out_specs=[row, row, pl.BlockSpec((N_CHIPS, tm, FS), lambda i: (0, i, 0)), row],
        out_shape=[jax.ShapeDtypeStruct((S, D), F32), jax.ShapeDtypeStruct((S, D), BF16),
                   jax.ShapeDtypeStruct((N_CHIPS, S, FS), BF16), jax.ShapeDtypeStruct((S, D), F32)],
        scratch_shapes=[pltpu.VMEM((N_CHIPS, D, FS), BF16), pltpu.VMEM((N_CHIPS * FR, D), BF16),
                        pltpu.SemaphoreType.DMA((2 * N_CHIPS,))],
        compiler_params=_cp(("arbitrary",)),
    )(x, vec, win_g, wout_g)


def ffn_bwd(dout, x, f, u, vec, win_g, wout_g, name, tm=256):
    S, D = x.shape
    FS = win_g.shape[-1]
    FR = wout_g.shape[-2]

    def body(dout_ref, x_ref, f_ref, u_ref, vec_ref, win_hbm, wout_hbm,
             dx_ref, df_ref, du_ref, act_ref, acc_ref, win_s, wout_s, sem):
        @pl.when(pl.program_id(0) == 0)
        def _():
            cps = [pltpu.make_async_copy(win_hbm.at[j], win_s.at[j], sem.at[j]) for j in range(N_CHIPS)]
            cps += [pltpu.make_async_copy(wout_hbm.at[j], wout_s.at[pl.ds(j * FR, FR), :], sem.at[N_CHIPS + j])
                    for j in range(N_CHIPS)]
            for cp in cps:
                cp.start()
            acc_ref[...] = jnp.zeros_like(acc_ref)
            for cp in cps:
                cp.wait()

        dout_v = dout_ref[...]
        df, dgate, dgpost = _post_bwd(dout_v, f_ref[...], vec_ref, 0.5)
        dfb = df.astype(BF16)
        df_ref[...] = dfb
        dh = jnp.zeros((tm, D), F32)
        for j in range(2):
            a = u_ref[j].astype(F32)
            b = u_ref[2 + j].astype(F32)
            sig = _sigmoid(a)
            sl = a * sig
            dact = _dot_nt(dfb, wout_s[pl.ds(j * FS, FS), :])
            da = (dact * b * (sig * (1.0 + a * (1.0 - sig)))).astype(BF16)
            db = (dact * sl).astype(BF16)
            du_ref[j] = da
            du_ref[2 + j] = db
            act_ref[:, j * FS:(j + 1) * FS] = (sl * b).astype(BF16)
            dh = dh + _dot_nt(da, win_s[j]) + _dot_nt(db, win_s[2 + j])
        dx, dshift, dscale, dgpre = _pre_bwd(x_ref[...], dh, vec_ref)
        dx_ref[...] = dout_v + dx
        _acc_add(acc_ref, A_SHIFT, dshift)
        _acc_add(acc_ref, A_SCALE, dscale)
        _acc_add(acc_ref, A_GATE, dgate)
        _acc_add(acc_ref, A_GPRE, dgpre)
        _acc_add(acc_ref, A_GPOST, dgpost)

    row = pl.BlockSpec((tm, D), lambda i: (i, 0))
    ublk = pl.BlockSpec((N_CHIPS, tm, FS), lambda i: (0, i, 0))
    const = pl.BlockSpec((8, D), lambda i: (0, 0))
    return pl.pallas_call(
        body, name=name, grid=(S // tm,),
        in_specs=[row, row, row, ublk, const, ANY, ANY],
        out_specs=[row, row, ublk, pl.BlockSpec((tm, 2 * FS), lambda i: (i, 0)), const],
        out_shape=[jax.ShapeDtypeStruct((S, D), F32), jax.ShapeDtypeStruct((S, D), BF16),
                   jax.ShapeDtypeStruct((N_CHIPS, S, FS), BF16), jax.ShapeDtypeStruct((S, 2 * FS), BF16),
                   jax.ShapeDtypeStruct((8, D), F32)],
        scratch_shapes=[pltpu.VMEM((N_CHIPS, D, FS), BF16), pltpu.VMEM((N_CHIPS * FR, D), BF16),
                        pltpu.SemaphoreType.DMA((2 * N_CHIPS,))],
        compiler_params=_cp(("arbitrary",)),
    )(dout, x, f, u, vec, win_g, wout_g)


def tn_matmul(a, b, tk, name, dep=None):
    S, K = a.shape
    nb, _, tn = b.shape
    extra = [] if dep is None else [dep]

    def body(a_ref, b_ref, *rest):
        rest[-1][...] = _dot_tn(a_ref[...], b_ref[...]).astype(BF16)

    return pl.pallas_call(
        body, name=name, grid=(K // tk, nb),
        in_specs=[pl.BlockSpec((S, tk), lambda i, j: (0, i)), pl.BlockSpec((None, S, tn), lambda i, j: (j, 0, 0))]
        + [ANY] * len(extra),
        out_specs=pl.BlockSpec((None, tk, tn), lambda i, j: (j, i, 0)),
        out_shape=jax.ShapeDtypeStruct((nb, K, tn), BF16),
        compiler_params=_cp(("arbitrary", "arbitrary")),
    )(a, b, *extra)


def tn_matmul_cols(a, b, tk, tn, name):
    S, K = a.shape
    N = b.shape[1]

    def body(a_ref, b_ref, o_ref):
        o_ref[...] = _dot_tn(a_ref[...], b_ref[...]).astype(BF16)

    return pl.pallas_call(
        body, name=name, grid=(K // tk, N // tn),
        in_specs=[pl.BlockSpec((S, tk), lambda i, j: (0, i)), pl.BlockSpec((S, tn), lambda i, j: (0, j))],
        out_specs=pl.BlockSpec((None, tk, tn), lambda i, j: (j, i, 0)),
        out_shape=jax.ShapeDtypeStruct((N // tn, K, tn), BF16),
        compiler_params=_cp(("arbitrary", "arbitrary")),
    )(a, b)


def pre_matmul(x, vec, w3, name, tm=256):
    S, D = x.shape
    nj, _, Nj = w3.shape

    def body(x_ref, vec_ref, w_ref, h_ref, z_ref):
        h = _pre_fwd(x_ref[...], vec_ref).astype(BF16)
        h_ref[...] = h
        for j in range(nj):
            z_ref[:, j * Nj:(j + 1) * Nj] = _dot(h, w_ref[j])

    row = pl.BlockSpec((tm, D), lambda i: (i, 0))
    return pl.pallas_call(
        body, name=name, grid=(S // tm,),
        in_specs=[row, pl.BlockSpec((8, D), lambda i: (0, 0)), pl.BlockSpec((nj, D, Nj), lambda i: (0, 0, 0))],
        out_specs=[row, pl.BlockSpec((tm, nj * Nj), lambda i: (i, 0))],
        out_shape=[jax.ShapeDtypeStruct((S, D), BF16), jax.ShapeDtypeStruct((S, nj * Nj), F32)],
        compiler_params=_cp(("arbitrary",)),
    )(x, vec, w3)


def matmul_pre_bwd(dz, w3, x, dres, vec, name, tm=256):
    S, D = x.shape
    nj, _, Nj = w3.shape

    def body(dz_ref, w_ref, x_ref, dres_ref, vec_ref, dx_ref, acc_ref):
        @pl.when(pl.program_id(0) == 0)
        def _():
            acc_ref[...] = jnp.zeros_like(acc_ref)

        dh = jnp.zeros((tm, D), F32)
        for j in range(nj):
            dh = dh + _dot_nt(dz_ref[:, j * Nj:(j + 1) * Nj], w_ref[j])
        dx, dshift, dscale, dgpre = _pre_bwd(x_ref[...], dh, vec_ref)
        dx_ref[...] = dres_ref[...] + dx
        _acc_add(acc_ref, A_SHIFT, dshift)
        _acc_add(acc_ref, A_SCALE, dscale)
        _acc_add(acc_ref, A_GPRE, dgpre)

    row = pl.BlockSpec((tm, D), lambda i: (i, 0))
    const = pl.BlockSpec((8, D), lambda i: (0, 0))
    return pl.pallas_call(
        body, name=name, grid=(S // tm,),
        in_specs=[pl.BlockSpec((tm, nj * Nj), lambda i: (i, 0)), pl.BlockSpec((nj, D, Nj), lambda i: (0, 0, 0)),
                  row, row, const],
        out_specs=[row, const],
        out_shape=[jax.ShapeDtypeStruct((S, D), F32), jax.ShapeDtypeStruct((8, D), F32)],
        compiler_params=_cp(("arbitrary",)),
    )(dz, w3, x, dres, vec)


def matmul_post(act, w, x, vec, name, tm=256):
    S, D = x.shape
    K = act.shape[1]

    def body(act_ref, w_ref, x_ref, vec_ref, out_ref, f_ref):
        f = _dot(act_ref[...], w_ref[...])
        f_ref[...] = f
        out_ref[...] = _post_fwd(x_ref[...], f, vec_ref, 1.0)

    row = pl.BlockSpec((tm, D), lambda i: (i, 0))
    return pl.pallas_call(
        body, name=name, grid=(S // tm,),
        in_specs=[pl.BlockSpec((tm, K), lambda i: (i, 0)), pl.BlockSpec((K, D), lambda i: (0, 0)), row,
                  pl.BlockSpec((8, D), lambda i: (0, 0))],
        out_specs=[row, row],
        out_shape=[jax.ShapeDtypeStruct((S, D), F32), jax.ShapeDtypeStruct((S, D), F32)],
        compiler_params=_cp(("arbitrary",)),
    )(act, w, x, vec)


def post_bwd_matmul(dout, f, vec, w, name, tm=256):
    S, D = dout.shape
    K = w.shape[0]

    def body(dout_ref, f_ref, vec_ref, w_ref, df_ref, dact_ref, acc_ref):
        @pl.when(pl.program_id(0) == 0)
        def _():
            acc_ref[...] = jnp.zeros_like(acc_ref)

        df, dgate, dgpost = _post_bwd(dout_ref[...], f_ref[...], vec_ref, 1.0)
        dfb = df.astype(BF16)
        df_ref[...] = dfb
        dact_ref[...] = _dot_nt(dfb, w_ref[...])
        _acc_add(acc_ref, A_GATE, dgate)
        _acc_add(acc_ref, A_GPOST, dgpost)

    row = pl.BlockSpec((tm, D), lambda i: (i, 0))
    const = pl.BlockSpec((8, D), lambda i: (0, 0))
    return pl.pallas_call(
        body, name=name, grid=(S // tm,),
        in_specs=[row, row, const, pl.BlockSpec((K, D), lambda i: (0, 0))],
        out_specs=[row, pl.BlockSpec((tm, K), lambda i: (i, 0)), const],
        out_shape=[jax.ShapeDtypeStruct((S, D), BF16), jax.ShapeDtypeStruct((S, K), F32),
                   jax.ShapeDtypeStruct((8, D), F32)],
        compiler_params=_cp(("arbitrary",)),
    )(dout, f, vec, w)


def _band(w, transposed, prev):
    r = lax.broadcasted_iota(jnp.int32, (CHUNK, CHUNK), 1 if transposed else 0)
    c = lax.broadcasted_iota(jnp.int32, (CHUNK, CHUNK), 0 if transposed else 1)
    d = r - c
    m = (d + CHUNK < w) if prev else ((d >= 0) & (d < w))
    return jnp.where(m, 1.0, 0.0).astype(BF16)


def _split_hi_lo(a):
    hi = a.astype(BF16)
    lo = (a - hi.astype(F32)).astype(BF16)
    return hi, lo


def _pool_diff(a, ap, g, denom):
    w = POOL_WINDOWS[g]
    a_hi, a_lo = _split_hi_lo(a)
    p_hi, p_lo = _split_hi_lo(ap)
    mc = _band(w, False, False)
    mp = _band(w, False, True)
    win = _dot(mc, a_hi) + _dot(mc, a_lo) + _dot(mp, p_hi) + _dot(mp, p_lo)
    return win / denom - a


def _sgu_norm(v, lg, lb):
    mu = _lanemean(v)
    xc = v - mu
    rstd = lax.rsqrt(_lanemean(xc * xc) + EPS)
    xhat = xc * rstd
    return xhat, rstd, xhat * lg + lb


def _tril_mask():
    r = lax.broadcasted_iota(jnp.int32, (CHUNK, CHUNK), 0)
    c = lax.broadcasted_iota(jnp.int32, (CHUNK, CHUNK), 1)
    return r >= c


def _positions(i, w):
    r = lax.broadcasted_iota(jnp.int32, (CHUNK, 128), 0)
    pos = (i * CHUNK + r + 1).astype(F32)
    return jnp.minimum(pos, float(w))


def mixa_core_fwd(z, pool_w, pool_scale, ln_g, ln_b, sgu_w, sgu_bexp, name):
    S = z.shape[0]
    W = z.shape[1] // 3
    G = len(POOL_WINDOWS)
    GD = W // G

    def body(zc_ref, zp_ref, pw_ref, ps_ref, lg_ref, lb_ref, sw_ref, sb_ref, y_ref):
        i = pl.program_id(0)
        has_prev = jnp.where(i > 0, 1.0, 0.0)
        for g in range(G):
            sl = slice(g * GD, (g + 1) * GD)
            a = zc_ref[:, sl]
            ap = zp_ref[:, sl] * has_prev
            d = _pool_diff(a, ap, g, _positions(i, POOL_WINDOWS[g])).astype(BF16)
            y_ref[:, sl] = (_dot(d, pw_ref[g]) * ps_ref[:, sl]).astype(BF16)
        tril = _tril_mask()
        for hh in range(G):
            u = _gelu(zc_ref[:, W + hh * GD:W + (hh + 1) * GD])
            v = _gelu(zc_ref[:, 2 * W + hh * GD:2 * W + (hh + 1) * GD])
            sl = slice(hh * GD, (hh + 1) * GD)
            _, _, vn = _sgu_norm(v, lg_ref[:, sl], lb_ref[:, sl])
            wm = jnp.where(tril, sw_ref[hh], 0.0).astype(BF16)
            s = _dot(wm, vn.astype(BF16)) + sb_ref[hh]
            y_ref[:, W + hh * GD:W + (hh + 1) * GD] = (u * s).astype(BF16)

    vecw = pl.BlockSpec((1, W), lambda i: (0, 0))
    mats = pl.BlockSpec((G, GD, GD), lambda i: (0, 0, 0))
    return pl.pallas_call(
        body, name=name, grid=(S // CHUNK,),
        in_specs=[pl.BlockSpec((CHUNK, 3 * W), lambda i: (i, 0)),
                  pl.BlockSpec((CHUNK, W), lambda i: (jnp.maximum(i - 1, 0), 0)),
                  mats, vecw, vecw, vecw, mats, mats],
        out_specs=pl.BlockSpec((CHUNK, 2 * W), lambda i: (i, 0)),
        out_shape=jax.ShapeDtypeStruct((S, 2 * W), BF16),
        compiler_params=_cp(("arbitrary",)),
    )(z, z, pool_w, pool_scale, ln_g, ln_b, sgu_w, sgu_bexp)


def mixa_core_bwd(z, dy, pool_w, pool_scale, ln_g, ln_b, sgu_w, sgu_bexp, name):
    S = z.shape[0]
    W = z.shape[1] // 3
    G = len(POOL_WINDOWS)
    GD = W // G
    n_tiles = S // CHUNK

    def body(zc_ref, zp_ref, dyc_ref, dyn_ref, pw_ref, ps_ref, lg_ref, lb_ref, sw_ref, sb_ref,
             dz_ref, dpw_ref, dvec_ref, dsw_ref, dsb_ref):
        i = pl.program_id(0)

        @pl.when(i == 0)
        def _():
            dpw_ref[...] = jnp.zeros_like(dpw_ref)
            dvec_ref[...] = jnp.zeros_like(dvec_ref)
            dsw_ref[...] = jnp.zeros_like(dsw_ref)
            dsb_ref[...] = jnp.zeros_like(dsb_ref)

        has_prev = jnp.where(i > 0, 1.0, 0.0)
        has_next = jnp.where(i < n_tiles - 1, 1.0, 0.0)
        for g in range(G):
            w = POOL_WINDOWS[g]
            sl = slice(g * GD, (g + 1) * GD)
            a = zc_ref[:, sl]
            ap = zp_ref[:, sl] * has_prev
            den_c = _positions(i, w)
            den_n = _positions(i + 1, w)
            d = _pool_diff(a, ap, g, den_c).astype(BF16)
            ps = ps_ref[:, sl]
            pw = pw_ref[g]
            dyc = dyc_ref[:, sl]
            dvec_ref[pl.ds(0, 1), sl] += _rowsum(dyc * _dot(d, pw))
            dyp_c = (dyc * ps).astype(BF16)
            dyp_n = (dyn_ref[:, sl] * (ps * has_next)).astype(BF16)
            dpw_ref[g] += _dot_tn(d, dyp_c)
            dd_c = _dot_nt(dyp_c, pw)
            dd_n = _dot_nt(dyp_n, pw)
            da = (_dot(_band(w, True, False), (dd_c / den_c).astype(BF16))
                  + _dot(_band(w, True, True), (dd_n / den_n).astype(BF16)) - dd_c)
            dz_ref[:, sl] = da.astype(BF16)
        tril = _tril_mask()
        for hh in range(G):
            sl = slice(hh * GD, (hh + 1) * GD)
            zu = zc_ref[:, W + hh * GD:W + (hh + 1) * GD]
            zv = zc_ref[:, 2 * W + hh * GD:2 * W + (hh + 1) * GD]
            u = _gelu(zu)
            v = _gelu(zv)
            lg = lg_ref[:, sl]
            xhat, rstd, vn = _sgu_norm(v, lg, lb_ref[:, sl])
            vnb = vn.astype(BF16)
            wm = jnp.where(tril, sw_ref[hh], 0.0).astype(BF16)
            s = _dot(wm, vnb) + sb_ref[hh]
            dyb = dyc_ref[:, W + hh * GD:W + (hh + 1) * GD]
            du = dyb * s
            ds = dyb * u
            dsb_ref[:, hh:hh + 1] += jnp.sum(ds, axis=1, keepdims=True)
            dsb16 = ds.astype(BF16)
            dsw_ref[hh] += jnp.where(tril, _dot_nt(dsb16, vnb), 0.0)
            dvn = _dot_tn(wm, dsb16)
            dvec_ref[pl.ds(1, 1), sl] += _rowsum(dvn * xhat)
            dvec_ref[pl.ds(2, 1), sl] += _rowsum(dvn)
            dxh = dvn * lg
            dv = rstd * (dxh - _lanemean(dxh) - xhat * _lanemean(dxh * xhat))
            dz_ref[:, W + hh * GD:W + (hh + 1) * GD] = (du * _gelu_grad(zu)).astype(BF16)
            dz_ref[:, 2 * W + hh * GD:2 * W + (hh + 1) * GD] = (dv * _gelu_grad(zv)).astype(BF16)

    vecw = pl.BlockSpec((1, W), lambda i: (0, 0))
    mats = pl.BlockSpec((G, GD, GD), lambda i: (0, 0, 0))
    return pl.pallas_call(
        body, name=name, grid=(n_tiles,),
        in_specs=[pl.BlockSpec((CHUNK, 3 * W), lambda i: (i, 0)),
                  pl.BlockSpec((CHUNK, W), lambda i: (jnp.maximum(i - 1, 0), 0)),
                  pl.BlockSpec((CHUNK, 2 * W), lambda i: (i, 0)),
                  pl.BlockSpec((CHUNK, W), lambda i: (jnp.minimum(i + 1, n_tiles - 1), 0)),
                  mats, vecw, vecw, vecw, mats, mats],
        out_specs=[pl.BlockSpec((CHUNK, 3 * W), lambda i: (i, 0)), mats,
                   pl.BlockSpec((8, W), lambda i: (0, 0)), mats, pl.BlockSpec((CHUNK, G), lambda i: (0, 0))],
        out_shape=[jax.ShapeDtypeStruct((S, 3 * W), BF16), jax.ShapeDtypeStruct((G, GD, GD), F32),
                   jax.ShapeDtypeStruct((8, W), F32), jax.ShapeDtypeStruct((G, CHUNK, CHUNK), F32),
                   jax.ShapeDtypeStruct((CHUNK, G), F32)],
        compiler_params=_cp(("arbitrary",)),
    )(z, z, dy, dy, pool_w, pool_scale, ln_g, ln_b, sgu_w, sgu_bexp)


def _cmul(ar, ai, br, bi):
    return ar * br - ai * bi, ar * bi + ai * br


def _cpow(ar, ai, n):
    rr, ri = None, None
    br, bi = ar, ai
    while n:
        if n & 1:
            rr, ri = (br, bi) if rr is None else _cmul(rr, ri, br, bi)
        n >>= 1
        if n:
            br, bi = _cmul(br, bi, br, bi)
    return rr, ri


def _seg_rows(k):
    return pl.ds(pl.multiple_of(k * SCAN_LANES, SCAN_LANES), SCAN_LANES)


def _scan_fwd(xre, xim, carry, ar, ai, K):
    P = xre.shape[1]
    a8r = jnp.broadcast_to(ar, (SCAN_LANES, P))
    a8i = jnp.broadcast_to(ai, (SCAN_LANES, P))

    def local(k, c):
        pr, pi = c
        rows = _seg_rows(k)
        nr = a8r * pr - a8i * pi + xre[rows, :]
        ni = a8r * pi + a8i * pr + xim[rows, :]
        xre[rows, :] = nr
        xim[rows, :] = ni
        return nr, ni

    er, ei = lax.fori_loop(1, K, local, (xre[pl.ds(0, SCAN_LANES), :], xim[pl.ds(0, SCAN_LANES), :]),
                           unroll=SCAN_UNROLL)
    akr, aki = _cpow(ar, ai, K)
    cr = jnp.zeros((1, P), F32)
    ci = jnp.zeros((1, P), F32)
    carry[pl.ds(0, 1), :] = cr
    carry[pl.ds(SCAN_LANES, 1), :] = ci
    for j in range(1, SCAN_LANES):
        tr, ti = _cmul(akr, aki, cr, ci)
        cr = er[j - 1:j, :] + tr
        ci = ei[j - 1:j, :] + ti
        carry[pl.ds(j, 1), :] = cr
        carry[pl.ds(SCAN_LANES + j, 1), :] = ci
    cmr = carry[pl.ds(0, SCAN_LANES), :]
    cmi = carry[pl.ds(SCAN_LANES, SCAN_LANES), :]

    def fix(k, c):
        pr, pi = c
        rows = _seg_rows(k)
        tr, ti = _cmul(pr, pi, cmr, cmi)
        xre[rows, :] += tr
        xim[rows, :] += ti
        return _cmul(pr, pi, a8r, a8i)

    lax.fori_loop(0, K, fix, (a8r, a8i), unroll=SCAN_UNROLL)


def s5_core_fwd(u, ar, ai, bre, bim, cre_t, cim_t, dskip, name):
    S, D = u.shape
    nblk, UB, PB = bre.shape
    K = S // SCAN_LANES

    def body(u_ref, ar_ref, ai_ref, bre_ref, bim_ref, cre_ref, cim_ref, d_ref, y_ref, xre, xim, carry):
        uv = u_ref[...]
        ub = uv.astype(BF16)
        xre[...] = _dot(ub, bre_ref[...])
        xim[...] = _dot(ub, bim_ref[...])
        _scan_fwd(xre, xim, carry, ar_ref[...], ai_ref[...], K)
        y_ref[...] = (_dot(xre[...].astype(BF16), cre_ref[...]) - _dot(xim[...].astype(BF16), cim_ref[...])
                      + d_ref[...] * uv)

    ucol = pl.BlockSpec((S, UB), lambda i: (0, i))
    pvec = pl.BlockSpec((None, 1, PB), lambda i: (i, 0, 0))
    bmat = pl.BlockSpec((None, UB, PB), lambda i: (i, 0, 0))
    cmat = pl.BlockSpec((None, PB, UB), lambda i: (i, 0, 0))
    return pl.pallas_call(
        body, name=name, grid=(nblk,),
        in_specs=[ucol, pvec, pvec, bmat, bmat, cmat, cmat, pl.BlockSpec((1, UB), lambda i: (0, i))],
        out_specs=ucol,
        out_shape=jax.ShapeDtypeStruct((S, D), F32),
        scratch_shapes=[pltpu.VMEM((S, PB), F32), pltpu.VMEM((S, PB), F32), pltpu.VMEM((2 * SCAN_LANES, PB), F32)],
        compiler_params=_cp(("arbitrary",)),
    )(u, ar, ai, bre, bim, cre_t, cim_t, dskip)


def s5_core_bwd(u, dy, ar, ai, bre, bim, cre, cim, dskip, name):
    S, D = u.shape
    nblk, UB, PB = bre.shape
    K = S // SCAN_LANES

    def body(u_ref, dy_ref, ar_ref, ai_ref, bre_ref, bim_ref, cre_ref, cim_ref, d_ref,
             du_ref, dbre_ref, dbim_ref, dcre_ref, dcim_ref, dar_ref, dai_ref, dd_ref,
             xre, xim, gre, gim, carry, carry_b):
        ar = ar_ref[...]
        ai = ai_ref[...]
        uv = u_ref[...]
        ub = uv.astype(BF16)
        dyv = dy_ref[...]
        dyb = dyv.astype(BF16)
        xre[...] = _dot(ub, bre_ref[...])
        xim[...] = _dot(ub, bim_ref[...])
        _scan_fwd(xre, xim, carry, ar, ai, K)
        dcre_ref[...] = _dot_tn(xre[...].astype(BF16), dyb)
        dcim_ref[...] = -_dot_tn(xim[...].astype(BF16), dyb)
        gre[...] = _dot(dyb, cre_ref[...])
        gim[...] = -_dot(dyb, cim_ref[...])

        a8r = jnp.broadcast_to(ar, (SCAN_LANES, PB))
        a8i = jnp.broadcast_to(ai, (SCAN_LANES, PB))
        na8i = -a8i

        def local(s, c):
            k = K - 2 - s
            nr, ni = c
            rows = _seg_rows(k)
            tr = gre[rows, :] + a8r * nr + a8i * ni
            ti = gim[rows, :] + a8r * ni - a8i * nr
            gre[rows, :] = tr
            gim[rows, :] = ti
            return tr, ti

        last = _seg_rows(K - 1)
        fr, fi = lax.fori_loop(0, K - 1, local, (gre[last, :], gim[last, :]), unroll=SCAN_UNROLL)
        akr, aki = _cpow(ar, -ai, K)
        cr = jnp.zeros((1, PB), F32)
        ci = jnp.zeros((1, PB), F32)
        carry_b[pl.ds(SCAN_LANES - 1, 1), :] = cr
        carry_b[pl.ds(2 * SCAN_LANES - 1, 1), :] = ci
        for j in range(SCAN_LANES - 2, -1, -1):
            tr, ti = _cmul(akr, aki, cr, ci)
            cr = fr[j + 1:j + 2, :] + tr
            ci = fi[j + 1:j + 2, :] + ti
            carry_b[pl.ds(j, 1), :] = cr
            carry_b[pl.ds(SCAN_LANES + j, 1), :] = ci
        cbr = carry_b[pl.ds(0, SCAN_LANES), :]
        cbi = carry_b[pl.ds(SCAN_LANES, SCAN_LANES), :]

        def fix_rows(rows, c, xr, xi):
            pr, pi, dar, dai = c
            tr, ti = _cmul(pr, pi, cbr, cbi)
            g_r = gre[rows, :] + tr
            g_i = gim[rows, :] + ti
            gre[rows, :] = g_r
            gim[rows, :] = g_i
            dar = dar + g_r * xr + g_i * xi
            dai = dai + g_i * xr - g_r * xi
            nr, ni = _cmul(pr, pi, a8r, na8i)
            return nr, ni, dar, dai

        def fix(s, c):
            k = K - 1 - s
            prev = _seg_rows(k - 1)
            return fix_rows(_seg_rows(k), c, xre[prev, :], xim[prev, :])

        z8 = jnp.zeros((SCAN_LANES, PB), F32)
        c = lax.fori_loop(0, K - 1, fix, (a8r, na8i, z8, z8), unroll=SCAN_UNROLL)
        _, _, dar, dai = fix_rows(_seg_rows(0), c, carry[pl.ds(0, SCAN_LANES), :], carry[pl.ds(SCAN_LANES, SCAN_LANES), :])
        dar_ref[...] = _rowsum(dar)
        dai_ref[...] = _rowsum(dai)
        grb = gre[...].astype(BF16)
        gib = gim[...].astype(BF16)
        dbre_ref[...] = _dot_tn(ub, grb)
        dbim_ref[...] = _dot_tn(ub, gib)
        du_ref[...] = _dot_nt(grb, bre_ref[...]) + _dot_nt(gib, bim_ref[...]) + d_ref[...] * dyv
        dd_ref[...] = _rowsum(dyv * uv)

    ucol = pl.BlockSpec((S, UB), lambda i: (0, i))
    pvec = pl.BlockSpec((None, 1, PB), lambda i: (i, 0, 0))
    bmat = pl.BlockSpec((None, UB, PB), lambda i: (i, 0, 0))
    cmat = pl.BlockSpec((None, PB, UB), lambda i: (i, 0, 0))
    dvec = pl.BlockSpec((1, UB), lambda i: (0, i))
    return pl.pallas_call(
        body, name=name, grid=(nblk,),
        in_specs=[ucol, ucol, pvec, pvec, bmat, bmat, bmat, bmat, dvec],
        out_specs=[ucol, bmat, bmat, cmat, cmat, pvec, pvec, dvec],
        out_shape=[jax.ShapeDtypeStruct((S, D), F32),
                   jax.ShapeDtypeStruct((nblk, UB, PB), F32), jax.ShapeDtypeStruct((nblk, UB, PB), F32),
                   jax.ShapeDtypeStruct((nblk, PB, UB), F32), jax.ShapeDtypeStruct((nblk, PB, UB), F32),
                   jax.ShapeDtypeStruct((nblk, 1, PB), F32), jax.ShapeDtypeStruct((nblk, 1, PB), F32),
                   jax.ShapeDtypeStruct((1, D), F32)],
        scratch_shapes=[pltpu.VMEM((S, PB), F32), pltpu.VMEM((S, PB), F32), pltpu.VMEM((S, PB), F32),
                        pltpu.VMEM((S, PB), F32), pltpu.VMEM((2 * SCAN_LANES, PB), F32),
                        pltpu.VMEM((2 * SCAN_LANES, PB), F32)],
        compiler_params=_cp(("arbitrary",)),
    )(u, dy, ar, ai, bre, bim, cre, cim, dskip)


def s5_glu_fwd(y, wglu, x, vec, name, tm=256):
    S, D = x.shape
    NJ = wglu.shape[-1]

    def body(y_ref, w_ref, x_ref, vec_ref, out_ref, f_ref):
        g = _gelu(y_ref[...]).astype(BF16)
        f = jnp.concatenate([_dot(g, w_ref[j]) * _sigmoid(_dot(g, w_ref[2 + j])) for j in range(2)], axis=1)
        f_ref[...] = f
        out_ref[...] = _post_fwd(x_ref[...], f, vec_ref, 1.0)

    row = pl.BlockSpec((tm, D), lambda i: (i, 0))
    return pl.pallas_call(
        body, name=name, grid=(S // tm,),
        in_specs=[row, pl.BlockSpec((N_CHIPS, D, NJ), lambda i: (0, 0, 0)), row, pl.BlockSpec((8, D), lambda i: (0, 0))],
        out_specs=[row, row],
        out_shape=[jax.ShapeDtypeStruct((S, D), F32), jax.ShapeDtypeStruct((S, D), F32)],
        compiler_params=_cp(("arbitrary",)),
    )(y, wglu, x, vec)


def s5_glu_bwd(dout, f, y, vec, wglu, name, tm=256):
    S, D = dout.shape
    NJ = wglu.shape[-1]

    def body(dout_ref, f_ref, y_ref, vec_ref, w_ref, dy_ref, dab_ref, g_ref, acc_ref):
        @pl.when(pl.program_id(0) == 0)
        def _():
            acc_ref[...] = jnp.zeros_like(acc_ref)

        df, dgate, dgpost = _post_bwd(dout_ref[...], f_ref[...], vec_ref, 1.0)
        yv = y_ref[...]
        g = _gelu(yv).astype(BF16)
        g_ref[...] = g
        dg = jnp.zeros((tm, D), F32)
        for j in range(2):
            a = _dot(g, w_ref[j])
            sig = _sigmoid(_dot(g, w_ref[2 + j]))
            dfj = df[:, j * NJ:(j + 1) * NJ]
            da = (dfj * sig).astype(BF16)
            db = (dfj * a * sig * (1.0 - sig)).astype(BF16)
            dab_ref[:, j * NJ:(j + 1) * NJ] = da
            dab_ref[:, (2 + j) * NJ:(3 + j) * NJ] = db
            dg = dg + _dot_nt(da, w_ref[j]) + _dot_nt(db, w_ref[2 + j])
        dy_ref[...] = dg * _gelu_grad(yv)
        _acc_add(acc_ref, A_GATE, dgate)
        _acc_add(acc_ref, A_GPOST, dgpost)

    row = pl.BlockSpec((tm, D), lambda i: (i, 0))
    const = pl.BlockSpec((8, D), lambda i: (0, 0))
    return pl.pallas_call(
        body, name=name, grid=(S // tm,),
        in_specs=[row, row, row, const, pl.BlockSpec((N_CHIPS, D, NJ), lambda i: (0, 0, 0))],
        out_specs=[row, pl.BlockSpec((tm, N_CHIPS * NJ), lambda i: (i, 0)), row, const],
        out_shape=[jax.ShapeDtypeStruct((S, D), F32), jax.ShapeDtypeStruct((S, N_CHIPS * NJ), BF16),
                   jax.ShapeDtypeStruct((S, D), BF16), jax.ShapeDtypeStruct((8, D), F32)],
        compiler_params=_cp(("arbitrary",)),
    )(dout, f, y, vec, wglu)


def loss_head(y, target, name, tm=256):
    S, D = y.shape

    def body(y_ref, t_ref, d_ref, l_ref):
        @pl.when(pl.program_id(0) == 0)
        def _():
            l_ref[...] = jnp.zeros_like(l_ref)

        err = y_ref[...] - t_ref[...]
        d_ref[...] = err * (1.0 / D)
        l_ref[...] += jnp.sum(_rowsum(err * err), axis=1, keepdims=True)

    row = pl.BlockSpec((tm, D), lambda i: (i, 0))
    return pl.pallas_call(
        body, name=name, grid=(S // tm,),
        in_specs=[row, row],
        out_specs=[row, pl.BlockSpec((1, 1), lambda i: (0, 0))],
        out_shape=[jax.ShapeDtypeStruct((S, D), F32), jax.ShapeDtypeStruct((1, 1), F32)],
        compiler_params=_cp(("arbitrary",)),
    )(y, target)


def zero_after(dep, name):
    def body(dep_ref, o_ref):
        o_ref[...] = jnp.zeros_like(o_ref)

    return pl.pallas_call(body, name=name, in_specs=[ANY], out_specs=pl.BlockSpec(memory_space=pltpu.VMEM),
                          out_shape=jax.ShapeDtypeStruct((8, 128), F32), compiler_params=_cp())(dep)


def ada_mod(c_all, ada_w, ada_b_shard, name, tn=768, dep=None):
    B, D = c_all.shape
    L, _, NS = ada_w.shape
    extra = [] if dep is None else [dep]

    def body(c_ref, w_ref, b_ref, *rest):
        cv = c_ref[...]
        cond = (cv * _sigmoid(cv)).astype(BF16)
        rest[-1][...] = _dot(cond, w_ref[...].astype(BF16)) + b_ref[...]

    return pl.pallas_call(
        body, name=name, grid=(L, NS // tn),
        in_specs=[pl.BlockSpec((B, D), lambda l, j: (0, 0)), pl.BlockSpec((None, D, tn), lambda l, j: (l, 0, j)),
                  pl.BlockSpec((None, 1, tn), lambda l, j: (l, 0, j))] + [ANY] * len(extra),
        out_specs=pl.BlockSpec((None, B, tn), lambda l, j: (l, 0, j)),
        out_shape=jax.ShapeDtypeStruct((L, B, NS), F32),
        compiler_params=_cp(("arbitrary", "arbitrary")),
    )(c_all, ada_w, ada_b_shard, *extra)


def ada_grad(c_all, dmod, name, tn=768):
    B, D = c_all.shape
    L, _, NS = dmod.shape

    def body(c_ref, d_ref, o_ref):
        cv = c_ref[...]
        cond = (cv * _sigmoid(cv)).astype(BF16)
        o_ref[...] = _dot_tn(cond, d_ref[...].astype(BF16))

    return pl.pallas_call(
        body, name=name, grid=(L, NS // tn),
        in_specs=[pl.BlockSpec((B, D), lambda l, j: (0, 0)), pl.BlockSpec((None, B, tn), lambda l, j: (l, 0, j))],
        out_specs=pl.BlockSpec((None, D, tn), lambda l, j: (l, 0, j)),
        out_shape=jax.ShapeDtypeStruct((L, D, NS), F32),
        compiler_params=_cp(("arbitrary", "arbitrary")),
    )(c_all, dmod)


def sum_leading(x, name):
    n, R, C = x.shape

    def body(x_ref, o_ref):
        acc = x_ref[0].astype(F32)
        for i in range(1, n):
            acc = acc + x_ref[i].astype(F32)
        o_ref[...] = acc

    tr = _row_tile(R, 64)
    return pl.pallas_call(
        body, name=name, grid=(R // tr,),
        in_specs=[pl.BlockSpec((n, tr, C), lambda i: (0, i, 0))],
        out_specs=pl.BlockSpec((tr, C), lambda i: (i, 0)),
        out_shape=jax.ShapeDtypeStruct((R, C), F32),
        compiler_params=_cp(("arbitrary",)),
    )(x)


def _row_tile(R, cap=512):
    if R <= cap:
        return R
    for cand in (512, 384, 352, 256, 128, 64, 32, 16, 8):
        if cand <= cap and R % cand == 0:
            return cand
    return R


def adamw(w, g, m, v, name):
    R, C = w.shape
    tr = _row_tile(R, 256 if C > 1024 else 512)
    bc1 = 1.0 - ADAM_B1 ** ADAM_STEP
    bc2 = 1.0 - ADAM_B2 ** ADAM_STEP

    def body(w_ref, g_ref, m_ref, v_ref, d_ref, nm_ref, nv_ref):
        gv = g_ref[...]
        nm = ADAM_B1 * m_ref[...] + (1.0 - ADAM_B1) * gv
        nv = ADAM_B2 * v_ref[...] + (1.0 - ADAM_B2) * (gv * gv)
        nm_ref[...] = nm
        nv_ref[...] = nv
        d_ref[...] = -ADAM_LR * ((nm / bc1) / (jnp.sqrt(nv / bc2) + ADAM_EPS) + ADAM_WD * w_ref[...])

    blk = pl.BlockSpec((tr, C), lambda i: (i, 0))
    sd = jax.ShapeDtypeStruct((R, C), F32)
    return pl.pallas_call(
        body, name=name, grid=(R // tr,),
        in_specs=[blk, blk, blk, blk], out_specs=[blk, blk, blk], out_shape=[sd, sd, sd],
        compiler_params=_cp(("arbitrary",)),
    )(w, g, m, v)


def _as2d(a):
    if a.ndim == 1:
        return a.reshape(1, -1)
    return a.reshape(-1, a.shape[-1])


def adamw_nd(w, g, m, v, name):
    outs = adamw(_as2d(w), _as2d(g.reshape(w.shape)), _as2d(m), _as2d(v), name)
    return tuple(o.reshape(w.shape) for o in outs)


def _place():
    x, y, c = lax.axis_index("x"), lax.axis_index("y"), lax.axis_index("c")
    chips = [(1 - x, y), (x, 1 - y), (1 - x, 1 - y)]
    return x, y, c, chips


def allgather_small(xs, name):
    m_per, n = xs.shape

    def body(x_ref, out_ref, send_sems, recv_sems, local_sem):
        x, y, c, chips = _place()
        me, sibling = (x, y, c), (x, y, 1 - c)

        def rows(px, py, pc):
            return out_ref.at[pl.ds((4 * px + 2 * py + pc) * m_per, m_per), :]

        def copy(k, block, to, src=None):
            return pltpu.make_async_remote_copy(
                src_ref=rows(*block) if src is None else src, dst_ref=rows(*block),
                send_sem=send_sems.at[k], recv_sem=recv_sems.at[k], device_id=to, device_id_type=MESH_T)

        mine = pltpu.make_async_copy(x_ref, rows(*me), local_sem)
        mine.start()
        first = [copy(0, me, sibling, src=x_ref)]
        first += [copy(1 + j, me, (*chip, c), src=x_ref) for j, chip in enumerate(chips)]
        for cp in first:
            cp.start()
        passed = [copy(4 + j, (*chip, c), sibling) for j, chip in enumerate(chips)]
        for j, chip in enumerate(chips):
            copy(1 + j, (*chip, c), me).wait_recv()
            passed[j].start()
        copy(0, sibling, me).wait_recv()
        for j, chip in enumerate(chips):
            copy(4 + j, (*chip, 1 - c), me).wait_recv()
        for cp in first + passed:
            cp.wait_send()
        mine.wait()

    return pl.pallas_call(
        body, name=name,
        out_shape=jax.ShapeDtypeStruct((N_DEV * m_per, n), xs.dtype),
        in_specs=[pl.BlockSpec(memory_space=pltpu.VMEM)],
        out_specs=pl.BlockSpec(memory_space=pltpu.VMEM),
        scratch_shapes=[pltpu.SemaphoreType.DMA((7,)), pltpu.SemaphoreType.DMA((7,)), pltpu.SemaphoreType.DMA],
        compiler_params=_cp(),
    )(xs)


def gather_weights(shards, name):
    n = len(shards)

    def body(*refs):
        ins, outs = refs[:n], refs[n:2 * n]
        send_sems, recv_sems, fsend_sems, frecv_sems, local_sems = refs[2 * n:]
        x, y, c, chips = _place()
        me_chip = 2 * x + y
        sibling = (x, y, 1 - c)
        locs = []
        for a in range(n):
            cp = pltpu.make_async_copy(ins[a], outs[a].at[me_chip], local_sems.at[a])
            cp.start()
            locs.append(cp)

        def ici(a, k, src_chip, to):
            return pltpu.make_async_remote_copy(
                src_ref=ins[a].at[c], dst_ref=outs[a].at[src_chip, c],
                send_sem=send_sems.at[a * 3 + k], recv_sem=recv_sems.at[a * 3 + k], device_id=to, device_id_type=MESH_T)

        def d2d(a, k, src_chip, half):
            return pltpu.make_async_remote_copy(
                src_ref=outs[a].at[src_chip, half], dst_ref=outs[a].at[src_chip, half],
                send_sem=fsend_sems.at[a * 3 + k], recv_sem=frecv_sems.at[a * 3 + k],
                device_id=sibling, device_id_type=MESH_T)

        firsts = []
        for k, (cx, cy) in enumerate(chips):
            for a in range(n):
                cp = ici(a, k, me_chip, (cx, cy, c))
                cp.start()
                firsts.append(cp)
        passed = []
        for k, (cx, cy) in enumerate(chips):
            for a in range(n):
                ici(a, k, 2 * cx + cy, (cx, cy, c)).wait_recv()
                cp = d2d(a, k, 2 * cx + cy, c)
                cp.start()
                passed.append(cp)
        for k, (cx, cy) in enumerate(chips):
            for a in range(n):
                d2d(a, k, 2 * cx + cy, 1 - c).wait_recv()
        for cp in firsts + passed:
            cp.wait_send()
        for cp in locs:
            cp.wait()

    return pl.pallas_call(
        body, name=name,
        out_shape=[jax.ShapeDtypeStruct((N_CHIPS,) + s.shape, s.dtype) for s in shards],
        in_specs=[ANY] * n, out_specs=[ANY] * n,
        scratch_shapes=[pltpu.SemaphoreType.DMA((3 * n,)), pltpu.SemaphoreType.DMA((3 * n,)),
                        pltpu.SemaphoreType.DMA((3 * n,)), pltpu.SemaphoreType.DMA((3 * n,)),
                        pltpu.SemaphoreType.DMA((n,))],
        compiler_params=_cp(),
    )(*shards)


def _half_rows(ref, half, rh):
    idx = (slice(None),) * (len(ref.shape) - 2) + (pl.ds(pl.multiple_of(half * rh, 16), rh), slice(None))
    return ref.at[idx]


def sibling_exchange_halves(grads, name):
    n = len(grads)

    def body(*refs):
        ins, outs = refs[:n], refs[n:2 * n]
        send_sems, recv_sems = refs[2 * n:]
        x, y, c, _ = _place()
        cps = []
        for a in range(n):
            rh = ins[a].shape[-2] // 2
            cp = pltpu.make_async_remote_copy(
                src_ref=_half_rows(ins[a], 1 - c, rh), dst_ref=outs[a],
                send_sem=send_sems.at[a], recv_sem=recv_sems.at[a], device_id=(x, y, 1 - c), device_id_type=MESH_T)
            cp.start()
            cps.append(cp)
        for cp in cps:
            cp.wait()

    return pl.pallas_call(
        body, name=name,
        out_shape=[jax.ShapeDtypeStruct(g.shape[:-2] + (g.shape[-2] // 2, g.shape[-1]), g.dtype) for g in grads],
        in_specs=[ANY] * n, out_specs=[ANY] * n,
        scratch_shapes=[pltpu.SemaphoreType.DMA((n,)), pltpu.SemaphoreType.DMA((n,))],
        compiler_params=_cp(),
    )(*grads)


def pair_sum(g, recv, cidx, name):
    L, NS, R, C = g.shape
    rh = R // 2
    tr = _row_tile(rh, 256 if C > 1024 else 512)
    nt = rh // tr

    def body(c_ref, g_ref, r_ref, o_ref):
        o_ref[...] = (g_ref[...].astype(F32) + r_ref[...].astype(F32)).astype(BF16)

    grid_spec = pltpu.PrefetchScalarGridSpec(
        num_scalar_prefetch=1, grid=(L, NS, nt),
        in_specs=[pl.BlockSpec((None, None, tr, C), lambda l, s, t, c: (l, s, c[0] * nt + t, 0)),
                  pl.BlockSpec((None, None, tr, C), lambda l, s, t, c: (l, s, t, 0))],
        out_specs=pl.BlockSpec((None, None, tr, C), lambda l, s, t, c: (l, s, t, 0)))
    return pl.pallas_call(
        body, name=name, grid_spec=grid_spec,
        out_shape=jax.ShapeDtypeStruct((L, NS, rh, C), BF16),
        compiler_params=_cp(("arbitrary", "arbitrary", "arbitrary")),
    )(cidx, g, recv)


def chip_exchange(parts, name):
    n = len(parts)

    def body(*refs):
        ins, outs = refs[:n], refs[n:2 * n]
        send_sems, recv_sems = refs[2 * n:]
        x, y, c, chips = _place()
        cps = []
        for k, (cx, cy) in enumerate(chips):
            for a in range(n):
                cp = pltpu.make_async_remote_copy(
                    src_ref=ins[a].at[:, 2 * cx + cy], dst_ref=outs[a].at[k],
                    send_sem=send_sems.at[a * 3 + k], recv_sem=recv_sems.at[a * 3 + k],
                    device_id=(cx, cy, c), device_id_type=MESH_T)
                cp.start()
                cps.append(cp)
        for cp in cps:
            cp.wait()

    return pl.pallas_call(
        body, name=name,
        out_shape=[jax.ShapeDtypeStruct((3, p.shape[0]) + p.shape[2:], p.dtype) for p in parts],
        in_specs=[ANY] * n, out_specs=[ANY] * n,
        scratch_shapes=[pltpu.SemaphoreType.DMA((3 * n,)), pltpu.SemaphoreType.DMA((3 * n,))],
        compiler_params=_cp(),
    )(*parts)


def chip_sum(part, recv, chip_c, name, dest=None, slot=0, n_slots=1):
    _, NS, RH, C = part.shape
    tr = _row_tile(RH, 256 if C > 1024 else 512)
    nt = RH // tr

    def body(cc_ref, p_ref, r_ref, *rest):
        o_ref = rest[-1]
        acc = p_ref[...].astype(F32)
        for k in range(3):
            acc = acc + r_ref[k].astype(F32)
        o_ref[...] = acc

    in_specs = [pl.BlockSpec((None, None, tr, C), lambda t, cc: (0, cc[0], t, 0)),
                pl.BlockSpec((3, None, tr, C), lambda t, cc: (0, 0, t, 0))]
    args = [chip_c, part, recv]
    aliases = {}
    if dest is not None:
        in_specs.append(ANY)
        args.append(dest)
        aliases = {3: 0}
    grid_spec = pltpu.PrefetchScalarGridSpec(
        num_scalar_prefetch=1, grid=(nt,), in_specs=in_specs,
        out_specs=pl.BlockSpec((None, tr, C), lambda t, cc: (slot, cc[1] * nt + t, 0)))
    return pl.pallas_call(
        body, name=name, grid_spec=grid_spec,
        out_shape=jax.ShapeDtypeStruct((n_slots, 2 * RH, C), F32),
        input_output_aliases=aliases,
        compiler_params=_cp(("arbitrary",)),
    )(*args)


def sibling_fill_halves(bufs, name):
    n = len(bufs)

    def body(*refs):
        outs = refs[n:2 * n]
        send_sems, recv_sems = refs[2 * n:]
        x, y, c, _ = _place()
        cps = []
        for a in range(n):
            rh = outs[a].shape[-2] // 2
            cp = pltpu.make_async_remote_copy(
                src_ref=_half_rows(outs[a], c, rh), dst_ref=_half_rows(outs[a], c, rh),
                send_sem=send_sems.at[a], recv_sem=recv_sems.at[a], device_id=(x, y, 1 - c), device_id_type=MESH_T)
            cp.start()
            cps.append(cp)
        for a, cp in enumerate(cps):
            cp.wait_send()
        for a in range(n):
            rh = outs[a].shape[-2] // 2
            pltpu.make_async_remote_copy(
                src_ref=_half_rows(outs[a], 1 - c, rh), dst_ref=_half_rows(outs[a], 1 - c, rh),
                send_sem=send_sems.at[a], recv_sem=recv_sems.at[a], device_id=(x, y, 1 - c),
                device_id_type=MESH_T).wait_recv()

    return pl.pallas_call(
        body, name=name,
        out_shape=[jax.ShapeDtypeStruct(b.shape, b.dtype) for b in bufs],
        in_specs=[ANY] * n, out_specs=[ANY] * n,
        input_output_aliases={a: a for a in range(n)},
        scratch_shapes=[pltpu.SemaphoreType.DMA((n,)), pltpu.SemaphoreType.DMA((n,))],
        compiler_params=_cp(),
    )(*bufs)


HBM_SPEC = pl.BlockSpec(memory_space=pltpu.HBM)
SEM_SPEC = pl.BlockSpec(memory_space=pltpu.SEMAPHORE)
DATAFLOW_EFFECT = pltpu.SideEffectType.DATAFLOW_SIDE_EFFECTING


def split_start(bufs, copies_fn, n_copies, name, dep=None):
    nb = len(bufs)
    extra = [] if dep is None else [dep]

    def body(*refs):
        ne = len(extra)
        send_sems, recv_sems = refs[nb + ne], refs[nb + ne + 1]
        token = refs[-1]
        for cp in copies_fn(refs[:nb], send_sems, recv_sems):
            cp.start()
        token[...] = jnp.zeros_like(token)

    outs = pl.pallas_call(
        body, name=name,
        out_shape=(pltpu.SemaphoreType.DMA((n_copies,)), pltpu.SemaphoreType.DMA((n_copies,)),
                   *[pltpu.HBM(b.shape, b.dtype) for b in bufs], jax.ShapeDtypeStruct((8, 128), F32)),
        in_specs=[HBM_SPEC] * nb + [ANY] * len(extra),
        out_specs=(SEM_SPEC, SEM_SPEC, *[HBM_SPEC] * nb, pl.BlockSpec(memory_space=pltpu.VMEM)),
        input_output_aliases={i: 2 + i for i in range(nb)},
        compiler_params=pltpu.CompilerParams(has_side_effects=DATAFLOW_EFFECT),
    )(*[pltpu.with_memory_space_constraint(b, pltpu.HBM) for b in bufs], *extra)
    return outs[0], outs[1], list(outs[2:2 + nb]), outs[-1]


def split_wait(send_sems, recv_sems, bufs, after, copies_fn, name):
    nb = len(bufs)

    def body(*refs):
        for cp in copies_fn(refs[:nb], refs[nb], refs[nb + 1]):
            cp.wait_send()
            cp.wait_recv()

    outs = pl.pallas_call(
        body, name=name,
        out_shape=tuple(pltpu.HBM(b.shape, b.dtype) for b in bufs),
        in_specs=[HBM_SPEC] * nb + [SEM_SPEC, SEM_SPEC, ANY],
        out_specs=tuple([HBM_SPEC] * nb),
        input_output_aliases={i: i for i in range(nb)},
        compiler_params=pltpu.CompilerParams(has_side_effects=DATAFLOW_EFFECT),
    )(*bufs, send_sems, recv_sems, after)
    return list(outs)


def chip_exchange_copies(refs, send_sems, recv_sems):
    n = len(refs) // 2
    x, y, c, chips = _place()
    cps = []
    for a in range(n):
        for k, (cx, cy) in enumerate(chips):
            cps.append(pltpu.make_async_remote_copy(
                src_ref=refs[a].at[:, 2 * cx + cy], dst_ref=refs[n + a].at[k],
                send_sem=send_sems.at[a * 3 + k], recv_sem=recv_sems.at[a * 3 + k],
                device_id=(cx, cy, c), device_id_type=MESH_T))
    return cps


def pair_exchange_copies(refs, send_sems, recv_sems):
    n = len(refs) // 2
    x, y, c, _ = _place()
    return [pltpu.make_async_remote_copy(
        src_ref=_half_rows(refs[a], 1 - c, refs[a].shape[-2] // 2), dst_ref=refs[n + a],
        send_sem=send_sems.at[a], recv_sem=recv_sems.at[a], device_id=(x, y, 1 - c), device_id_type=MESH_T)
        for a in range(n)]


def weight_ici_copies(refs, send_sems, recv_sems):
    x, y, c, chips = _place()
    cps = []
    for a in range(len(refs)):
        for k, (cx, cy) in enumerate(chips):
            cps.append(pltpu.make_async_remote_copy(
                src_ref=refs[a].at[2 * x + y, c], dst_ref=refs[a].at[2 * x + y, c],
                send_sem=send_sems.at[a * 3 + k], recv_sem=recv_sems.at[a * 3 + k],
                device_id=(cx, cy, c), device_id_type=MESH_T))
    return cps


def weight_d2d_copies(refs, send_sems, recv_sems):
    x, y, c, chips = _place()
    cps = []
    for a in range(len(refs)):
        for k, (cx, cy) in enumerate(chips):
            cps.append(pltpu.make_async_remote_copy(
                src_ref=refs[a].at[2 * cx + cy, c], dst_ref=refs[a].at[2 * cx + cy, c],
                send_sem=send_sems.at[a * 3 + k], recv_sem=recv_sems.at[a * 3 + k],
                device_id=(x, y, 1 - c), device_id_type=MESH_T))
    return cps


def cast_place(w, lead, chip_arr, name, dep=None):
    R, C = w.shape[-2:]
    tr = _row_tile(R, 256 if C > 1024 else 512)

    def body(chip_ref, w_ref, *rest):
        rest[-1][...] = w_ref[...].astype(BF16)

    in_specs = [pl.BlockSpec((None,) * len(lead) + (tr, C), lambda t, ch: (*lead, t, 0))]
    args = [chip_arr, w]
    if dep is not None:
        in_specs.append(ANY)
        args.append(dep)
    grid_spec = pltpu.PrefetchScalarGridSpec(
        num_scalar_prefetch=1, grid=(R // tr,), in_specs=in_specs,
        out_specs=pl.BlockSpec((None, tr, C), lambda t, ch: (ch[0], t, 0)))
    return pl.pallas_call(
        body, name=name, grid_spec=grid_spec,
        out_shape=jax.ShapeDtypeStruct((N_CHIPS, R, C), BF16),
        compiler_params=_cp(("arbitrary",)),
    )(*args)


def _s5_prepare(lam_re, lam_im, log_dt, b_re, b_im, c_re, c_im, groups_per_block):
    G, P = lam_re.shape
    N = b_re.shape[-1]
    gb = groups_per_block
    nblk = G // gb
    dt = jnp.exp(log_dt)[:, None]
    e = jnp.exp(lam_re * dt)
    a_re = e * jnp.cos(lam_im * dt)
    a_im = e * jnp.sin(lam_im * dt)
    n2 = lam_re * lam_re + lam_im * lam_im
    co_re = ((a_re - 1.0) * lam_re + a_im * lam_im) / n2
    co_im = (a_im * lam_re - (a_re - 1.0) * lam_im) / n2
    bb_re = co_re[..., None] * b_re - co_im[..., None] * b_im
    bb_im = co_re[..., None] * b_im + co_im[..., None] * b_re
    eye = jnp.eye(gb, dtype=F32)

    def blockdiag_np(m):
        m = m.reshape(nblk, gb, N, P)
        return jnp.einsum('bgnp,gh->bgnhp', m, eye).reshape(nblk, gb * N, gb * P)

    b_np_re = jnp.swapaxes(bb_re, 1, 2)
    b_np_im = jnp.swapaxes(bb_im, 1, 2)
    return (a_re.reshape(nblk, 1, gb * P), a_im.reshape(nblk, 1, gb * P),
            blockdiag_np(b_np_re), blockdiag_np(b_np_im), blockdiag_np(c_re), blockdiag_np(c_im))


def _to_scan_order(a):
    S, D = a.shape
    return a.reshape(SCAN_LANES, S // SCAN_LANES, D).transpose(1, 0, 2).reshape(S, D)


def _from_scan_order(a):
    S, D = a.shape
    return a.reshape(S // SCAN_LANES, SCAN_LANES, D).transpose(1, 0, 2).reshape(S, D)


def _pad_rows(a, mult=8):
    r = (-a.shape[0]) % mult
    if r:
        a = jnp.concatenate([a, jnp.zeros((r, a.shape[1]), a.dtype)], axis=0)
    return a


def _pack_rows(arrs, width):
    parts, offs, o = [], [], 0
    for a in arrs:
        flat = a.reshape(-1)
        r = (-flat.shape[0]) % (16 * width)
        if r:
            flat = jnp.concatenate([flat, jnp.zeros((r,), flat.dtype)])
        p = flat.reshape(-1, width)
        parts.append(p)
        offs.append((o, a.shape, a.size))
        o += p.shape[0]
    if o % 64:
        parts.append(jnp.zeros((64 - o % 64, width), parts[0].dtype))
    return jnp.concatenate(parts, axis=0), offs


def _unpack_rows(packed, offs):
    outs = []
    for o, shape, size in offs:
        rows = -(-size // packed.shape[1])
        outs.append(packed[o:o + rows].reshape(-1)[:size].reshape(shape))
    return outs


def kernel(x, c, ada_w, ada_b, norm_pre, norm_post, ffn_w_in, ffn_w_out, ab_w_in, pool_w, pool_scale, sgu_ln_g, sgu_ln_b, sgu_w, sgu_b, ab_w_out, ssm_w_in, ssm_lam_re, ssm_lam_im, ssm_b_re, ssm_b_im, ssm_c_re, ssm_c_im, ssm_d, ssm_log_dt, ssm_w_glu, loss_target, m_ada_w, m_ada_b, m_norm_pre, m_norm_post, m_ffn_w_in, m_ffn_w_out, m_ab_w_in, m_pool_w, m_pool_scale, m_sgu_ln_g, m_sgu_ln_b, m_sgu_w, m_sgu_b, m_ab_w_out, m_ssm_w_in, m_ssm_lam_re, m_ssm_lam_im, m_ssm_b_re, m_ssm_b_im, m_ssm_c_re, m_ssm_c_im, m_ssm_d, m_ssm_log_dt, m_ssm_w_glu, v_ada_w, v_ada_b, v_norm_pre, v_norm_post, v_ffn_w_in, v_ffn_w_out, v_ab_w_in, v_pool_w, v_pool_scale, v_sgu_ln_g, v_sgu_ln_b, v_sgu_w, v_sgu_b, v_ab_w_out, v_ssm_w_in, v_ssm_lam_re, v_ssm_lam_im, v_ssm_b_re, v_ssm_b_im, v_ssm_c_re, v_ssm_c_im, v_ssm_d, v_ssm_log_dt, v_ssm_w_glu):
    weights = dict(ada_w=ada_w, ada_b=ada_b, norm_pre=norm_pre, norm_post=norm_post, ffn_w_in=ffn_w_in,
                   ffn_w_out=ffn_w_out, ab_w_in=ab_w_in, pool_w=pool_w, pool_scale=pool_scale, sgu_ln_g=sgu_ln_g,
                   sgu_ln_b=sgu_ln_b, sgu_w=sgu_w, sgu_b=sgu_b, ab_w_out=ab_w_out, ssm_w_in=ssm_w_in,
                   ssm_lam_re=ssm_lam_re, ssm_lam_im=ssm_lam_im, ssm_b_re=ssm_b_re, ssm_b_im=ssm_b_im,
                   ssm_c_re=ssm_c_re, ssm_c_im=ssm_c_im, ssm_d=ssm_d, ssm_log_dt=ssm_log_dt, ssm_w_glu=ssm_w_glu)
    m_in = dict(ada_w=m_ada_w, ada_b=m_ada_b, norm_pre=m_norm_pre, norm_post=m_norm_post, ffn_w_in=m_ffn_w_in,
                ffn_w_out=m_ffn_w_out, ab_w_in=m_ab_w_in, pool_w=m_pool_w, pool_scale=m_pool_scale,
                sgu_ln_g=m_sgu_ln_g, sgu_ln_b=m_sgu_ln_b, sgu_w=m_sgu_w, sgu_b=m_sgu_b, ab_w_out=m_ab_w_out,
                ssm_w_in=m_ssm_w_in, ssm_lam_re=m_ssm_lam_re, ssm_lam_im=m_ssm_lam_im, ssm_b_re=m_ssm_b_re,
                ssm_b_im=m_ssm_b_im, ssm_c_re=m_ssm_c_re, ssm_c_im=m_ssm_c_im, ssm_d=m_ssm_d,
                ssm_log_dt=m_ssm_log_dt, ssm_w_glu=m_ssm_w_glu)
    v_in = dict(ada_w=v_ada_w, ada_b=v_ada_b, norm_pre=v_norm_pre, norm_post=v_norm_post, ffn_w_in=v_ffn_w_in,
                ffn_w_out=v_ffn_w_out, ab_w_in=v_ab_w_in, pool_w=v_pool_w, pool_scale=v_pool_scale,
                sgu_ln_g=v_sgu_ln_g, sgu_ln_b=v_sgu_ln_b, sgu_w=v_sgu_w, sgu_b=v_sgu_b, ab_w_out=v_ab_w_out,
                ssm_w_in=v_ssm_w_in, ssm_lam_re=v_ssm_lam_re, ssm_lam_im=v_ssm_lam_im, ssm_b_re=v_ssm_b_re,
                ssm_b_im=v_ssm_b_im, ssm_c_re=v_ssm_c_re, ssm_c_im=v_ssm_c_im, ssm_d=v_ssm_d,
                ssm_log_dt=v_ssm_log_dt, ssm_w_glu=v_ssm_w_glu)
    names = list(weights.keys())

    xi, yi, ci = lax.axis_index("x"), lax.axis_index("y"), lax.axis_index("c")
    chip = 2 * xi + yi
    me = 4 * xi + 2 * yi + ci
    S, D = x.shape[1], x.shape[2]
    L = ada_w.shape[0]
    NSUB = norm_pre.shape[1]
    DS = norm_pre.shape[2]
    FS = ffn_w_in.shape[-1]
    FR = ffn_w_out.shape[-2]
    x0 = x[0]
    target = loss_target[0]

    small_parts = [_pad_rows(p) for p in (c.reshape(D // DS, DS), norm_pre.reshape(L * NSUB, DS),
                                          norm_post.reshape(L * NSUB, DS), ssm_d.reshape(1, DS))]
    small_offs = [0]
    for p in small_parts:
        small_offs.append(small_offs[-1] + p.shape[0])
    small_all = allgather_small(jnp.concatenate(small_parts, axis=0), "ag_small").reshape(N_DEV, -1, DS)
    c_all = small_all[:, :D // DS].reshape(N_DEV, D)
    per_chip = small_all[0::2]
    o = small_offs[1]
    g_pre_full = jnp.moveaxis(per_chip[:, o:o + L * NSUB], 0, 1).reshape(L, NSUB, D)
    o = small_offs[2]
    g_post_full = jnp.moveaxis(per_chip[:, o:o + L * NSUB], 0, 1).reshape(L, NSUB, D)
    o = small_offs[3]
    d_full = jnp.moveaxis(per_chip[:, o:o + 1], 0, 1).reshape(1, D)

    pieces = []
    for l in range(L):
        pieces.append((f"ffn_{l}_0", [(ffn_w_in, (l, 0)), (ffn_w_out, (l, 0))]))
        if l % 2 == 0:
            pieces.append((f"mix_{l}", [(ab_w_in, (l // 2,)), (ab_w_out, (l // 2,))]))
        else:
            pieces.append((f"mix_{l}", [(ssm_w_in, (l // 2,)), (ssm_w_glu, (l // 2,))]))
        pieces.append((f"ffn_{l}_1", [(ffn_w_in, (l, 1)), (ffn_w_out, (l, 1))]))
    tags = [tag for tag, _ in pieces]
    chip_arr = chip.reshape(1).astype(jnp.int32)
    wg_started, wg_passing = {}, {}

    def cast_piece(tag, ws, dep):
        wg_started[tag] = [cast_place(w, lead, chip_arr, f"wg_cast_{tag}_{i}", dep=dep).reshape(
            N_CHIPS, 2, w.shape[-2] // 2, w.shape[-1]) for i, (w, lead) in enumerate(ws)]

    def start_piece(tag, dep):
        bufs = wg_started[tag]
        wg_started[tag] = split_start(bufs, weight_ici_copies, 3 * len(bufs), f"wg_ici_start_{tag}", dep=dep)
        return wg_started[tag][3]

    def weights_arrived(i, after):
        ssem, rsem, bufs, _ = wg_started[tags[i]]
        lands = split_wait(ssem, rsem, bufs, after, weight_ici_copies, f"wg_ici_wait_{tags[i]}")
        wg_passing[i] = split_start(lands, weight_d2d_copies, 3 * len(lands), f"wg_d2d_start_{tags[i]}")
        return wg_passing[i][3][0, 0]

    def weights_of(i, after):
        ssem, rsem, lands, _ = wg_passing[i]
        return split_wait(ssem, rsem, lands, after, weight_d2d_copies, f"wg_d2d_wait_{tags[i]}")

    NS = ada_w.shape[-1]
    ada_b_shard = lax.dynamic_slice_in_dim(ada_b, chip * NS, NS, axis=1).reshape(L, 1, NS)
    cast_piece(*pieces[0], small_all)
    first_token = start_piece(tags[0], small_all)
    for tag, ws in pieces[1:]:
        cast_piece(tag, ws, first_token)
    mod_part = ada_mod(c_all, ada_w, ada_b_shard, "ada_mod", dep=wg_started[tags[-1]][-1])
    mod_all = allgather_small(mod_part.reshape(L * N_DEV, NS), "ag_mod").reshape(N_DEV, L, N_DEV, NS)
    mod_mine = lax.dynamic_index_in_dim(mod_all[0::2], me, axis=2, keepdims=False)
    mod = jnp.moveaxis(mod_mine, 0, 1).reshape(L, NSUB, 3, D)
    wg_token = first_token[0, 0]
    for tag in tags[1:]:
        wg_token = wg_token + start_piece(tag, mod_all)[0, 0]

    def vec_of(l, s):
        return jnp.concatenate([mod[l, s], g_pre_full[l, s][None], g_post_full[l, s][None],
                                jnp.zeros((3, D), F32)], axis=0)

    GB = 8
    s5_args = (ssm_lam_re[0], ssm_lam_im[0], ssm_log_dt[0], ssm_b_re[0], ssm_b_im[0], ssm_c_re[0], ssm_c_im[0])
    (a_re, a_im, bblk_re, bblk_im, cblk_re, cblk_im), s5_vjp = jax.vjp(lambda *p: _s5_prepare(*p, GB), *s5_args)
    bre16, bim16 = bblk_re.astype(BF16), bblk_im.astype(BF16)
    cre16, cim16 = cblk_re.astype(BF16), cblk_im.astype(BF16)
    cre16_t, cim16_t = jnp.swapaxes(cre16, 1, 2), jnp.swapaxes(cim16, 1, 2)

    pool_w16 = pool_w[0].astype(BF16)
    sgu_bexp = jnp.broadcast_to(sgu_b[0][:, :, None], sgu_w[0].shape)

    saved = {}
    ffn_w = {}
    xcur = x0
    stage = [0]
    first_pass = weights_arrived(0, vec_of(0, 0) + wg_token)

    def next_weights(after, vec):
        i = stage[0]
        stage[0] += 1
        g = weights_of(i, after)
        if i + 1 < len(tags):
            vec = vec + weights_arrived(i + 1, after)
        return g, vec

    for l in range(L):
        v0 = vec_of(l, 0)
        g, v0p = next_weights(xcur if l else v0 + first_pass, v0)
        ffn_w[(l, 0)] = (g[0].reshape(N_CHIPS, D, FS), g[1].reshape(N_CHIPS, FR, D))
        out, h, u, f = ffn_fwd(xcur, v0p, *ffn_w[(l, 0)], f"ffn_fwd_{l}_0")
        saved[(l, 0)] = (xcur, v0, h, u, f)
        xcur = out
        v1 = vec_of(l, 1)
        mix_g, v1p = next_weights(xcur, v1)
        if l % 2 == 0:
            abin_g = mix_g[0].reshape(N_CHIPS, D, -1)
            about_g = mix_g[1].reshape(-1, D)
            h, z = pre_matmul(xcur, v1p, abin_g, f"mixa_in_{l}")
            ycat = mixa_core_fwd(z, pool_w16, pool_scale, sgu_ln_g, sgu_ln_b, sgu_w[0], sgu_bexp, f"mixa_core_{l}")
            out, f = matmul_post(ycat, about_g, xcur, v1, f"mixa_out_{l}")
            saved[(l, 1)] = (xcur, v1, h, z, ycat, f)
        else:
            sin_g = mix_g[0].reshape(1, -1, D)
            glu_g = mix_g[1].reshape(N_CHIPS, D, -1)
            h, uu = pre_matmul(xcur, v1p, sin_g, f"s5_in_{l}")
            us = _to_scan_order(uu)
            ys = s5_core_fwd(us, a_re, a_im, bre16, bim16, cre16_t, cim16_t, d_full, f"s5_core_{l}")
            yy = _from_scan_order(ys)
            out, f = s5_glu_fwd(yy, glu_g, xcur, v1, f"s5_glu_{l}")
            saved[(l, 1)] = (xcur, v1, h, us, yy, f)
        xcur = out
        v2 = vec_of(l, 2)
        g, v2p = next_weights(xcur, v2)
        ffn_w[(l, 1)] = (g[0].reshape(N_CHIPS, D, FS), g[1].reshape(N_CHIPS, FR, D))
        out, h, u, f = ffn_fwd(xcur, v2p, *ffn_w[(l, 1)], f"ffn_fwd_{l}_1")
        saved[(l, 2)] = (xcur, v2, h, u, f)
        xcur = out

    dcur, sq = loss_head(xcur, target, "loss_head")
    loss = lax.psum(sq[0, 0], ("x", "y", "c")) * (0.5 / D)

    accs = {}
    small_g = {}
    cidx = ci.reshape(1).astype(jnp.int32)
    chip_c = jnp.stack([chip, ci]).astype(jnp.int32)
    rs_open, rs_pending = [], []

    pin = [zero_after(loss.reshape(1, 1), "after_loss")[0, 0]]

    def pinned(v):
        return v + pin[0]

    def rs_begin(tag, items):
        arrs = [it[0] for it in items]
        lands = [lax.empty(a.shape[:-2] + (a.shape[-2] // 2, a.shape[-1]), BF16) for a in arrs]
        started = split_start(arrs + lands, pair_exchange_copies, len(arrs), f"rs_pair_start_{tag}")
        rs_open.append((tag, items, started))
        pin[0] = pin[0] + started[3][0, 0]

    def rs_advance(after):
        while rs_open:
            tag, items, (ssem, rsem, bufs, _) = rs_open.pop(0)
            n = len(items)
            bufs = split_wait(ssem, rsem, bufs, after, pair_exchange_copies, f"rs_pair_wait_{tag}")
            parts = [pair_sum(a, r, cidx, f"rs_pair_sum_{tag}_{i}") for i, (a, r) in enumerate(zip(bufs[:n], bufs[n:]))]
            lands = [lax.empty((3, 1) + p.shape[2:], BF16) for p in parts]
            started = split_start(parts + lands, chip_exchange_copies, 3 * n, f"rs_chip_start_{tag}")
            rs_pending.append((tag, items, started))
            pin[0] = pin[0] + started[3][0, 0]

    def ffn_back(l, s, k, dcur, before_wgrads=None):
        xin, vv, h, u, f = saved[(l, s)]
        vv = pinned(vv)
        dx, df, du, act, acc = ffn_bwd(dcur, xin, f, u, vv, *ffn_w[(l, k)], f"ffn_bwd_{l}_{k}")
        accs[(l, s)] = acc
        rs_advance(dx)
        dep = None if before_wgrads is None else before_wgrads()
        g_win = tn_matmul(h, du, 1024, f"ffn_dwin_{l}_{k}", dep=dep)
        g_wout = tn_matmul_cols(act, df, FS, D, f"ffn_dwout_{l}_{k}")
        rs_begin(f"ffn_{l}_{k}", [(g_win[None], "ffn_w_in", 2 * l + k, 2 * L),
                                  (g_wout.reshape(1, N_CHIPS, FR, D), "ffn_w_out", 2 * l + k, 2 * L)])
        return dx

    grads = {}

    def small_grads():
        acc_all = jnp.stack([jnp.stack([accs[(l, s)] for s in range(NSUB)]) for l in range(L)])
        dmod_mine = acc_all[:, :, A_SHIFT:A_GATE + 1].reshape(L, NSUB * 3 * D)
        dgpre_mine = acc_all[:, :, A_GPRE]
        dgpost_mine = acc_all[:, :, A_GPOST]
        small_names = ["pool_w", "pool_scale", "sgu_ln_g", "sgu_ln_b", "sgu_w", "sgu_b", "ssm_lam_re", "ssm_lam_im",
                       "ssm_b_re", "ssm_b_im", "ssm_c_re", "ssm_c_im", "ssm_log_dt"]
        packed, offs = _pack_rows([dmod_mine, dgpre_mine, dgpost_mine, dd_mine] + [small_g[n] for n in small_names], D)
        rows = packed.shape[0]
        packed_all = allgather_small(pinned(packed).astype(BF16), "ag_grads").reshape(N_DEV, rows, D)
        summed = _unpack_rows(sum_leading(packed_all, "sum_small"), offs)
        grads.update({n: g for n, g in zip(small_names, summed[4:])})
        grads["ada_b"] = summed[0]
        grads["norm_pre"] = lax.dynamic_slice_in_dim(summed[1], chip * DS, DS, axis=2)
        grads["norm_post"] = lax.dynamic_slice_in_dim(summed[2], chip * DS, DS, axis=2)
        grads["ssm_d"] = lax.dynamic_slice_in_dim(summed[3], chip * DS, DS, axis=1)
        dmod_all = jnp.moveaxis(packed_all[:, :L * NSUB * 3].reshape(N_DEV, L, NSUB * 3 * D), 0, 1)
        dmod_shard = lax.dynamic_slice_in_dim(dmod_all, chip * NS, NS, axis=2)
        grads["ada_w"] = ada_grad(c_all, dmod_shard, "ada_grad")
        return packed_all

    for l in reversed(range(L)):
        dcur = ffn_back(l, 2, 1, dcur)
        if l % 2 == 0:
            xin, vv, h, z, ycat, f = saved[(l, 1)]
            vv = pinned(vv)
            df, dact, acc_post = post_bwd_matmul(dcur, f, vv, about_g, f"mixa_out_bwd_{l}")
            rs_advance(dact)
            g_about = tn_matmul_cols(ycat, df, 512, D, f"mixa_dwout_{l}")
            dz, dpw, dvecw, dsw, dsb = mixa_core_bwd(z, dact, pool_w16, pool_scale, sgu_ln_g, sgu_ln_b, sgu_w[0],
                                                     sgu_bexp, f"mixa_core_bwd_{l}")
            g_abin = tn_matmul_cols(h, dz, 512, abin_g.shape[-1], f"mixa_dwin_{l}")
            dcur, acc_pre = matmul_pre_bwd(dz, abin_g, xin, dcur, vv, f"mixa_in_bwd_{l}")
            accs[(l, 1)] = acc_pre + acc_post
            rs_begin(f"mixa_{l}", [(g_abin[None], "ab_w_in", 0, 1),
                                   (g_about.reshape(1, N_CHIPS, -1, D), "ab_w_out", 0, 1)])
            small_g.update(pool_w=dpw[None], pool_scale=dvecw[0:1], sgu_ln_g=dvecw[1:2], sgu_ln_b=dvecw[2:3],
                           sgu_w=dsw[None], sgu_b=dsb.T[None])
        else:
            xin, vv, h, us, yy, f = saved[(l, 1)]
            vv = pinned(vv)
            dy, dab, gact, acc_post = s5_glu_bwd(dcur, f, yy, vv, glu_g, f"s5_glu_bwd_{l}")
            rs_advance(dy)
            g_glu = tn_matmul_cols(gact, dab, 512, glu_g.shape[-1], f"s5_dwglu_{l}")
            dys = _to_scan_order(dy)
            dus, dbre, dbim, dcre_t, dcim_t, dar, dai, dd = s5_core_bwd(
                us, dys, a_re, a_im, bre16, bim16, cre16, cim16, d_full, f"s5_core_bwd_{l}")
            du = _from_scan_order(dus).astype(BF16)
            g_sin = tn_matmul_cols(h, du, 512, D, f"s5_dwin_{l}")
            dcur, acc_pre = matmul_pre_bwd(du, sin_g, xin, dcur, vv, f"s5_in_bwd_{l}")
            accs[(l, 1)] = acc_pre + acc_post
            rs_begin(f"s5_{l}", [(g_sin.reshape(1, N_CHIPS, -1, D), "ssm_w_in", 0, 1), (g_glu[None], "ssm_w_glu", 0, 1)])
            s5_grads = s5_vjp((dar, dai, dbre, dbim, jnp.swapaxes(dcre_t, 1, 2), jnp.swapaxes(dcim_t, 1, 2)))
            small_g.update(ssm_lam_re=s5_grads[0][None], ssm_lam_im=s5_grads[1][None], ssm_log_dt=s5_grads[2][None],
                           ssm_b_re=s5_grads[3][None], ssm_b_im=s5_grads[4][None], ssm_c_re=s5_grads[5][None],
                           ssm_c_im=s5_grads[6][None])
            dd_mine = dd
        dcur = ffn_back(l, 0, 0, dcur, before_wgrads=small_grads if l == 0 else None)
    grad_x = dcur[None]
    rs_advance(pin[0].reshape(1, 1))

    big_names = ["ffn_w_in", "ffn_w_out", "ab_w_in", "ab_w_out", "ssm_w_in", "ssm_w_glu"]
    deltas, new_m, new_v = {}, {}, {}

    def update(n):
        grads[n] = grads[n].reshape(weights[n].shape)
        deltas[n], new_m[n], new_v[n] = adamw_nd(weights[n], grads[n], m_in[n], v_in[n], f"adamw_{n}")

    for n in names:
        if n not in big_names:
            update(n)

    fin = {}
    after = pinned(deltas["ssm_log_dt"]) + deltas["ada_w"][0, 0, 0]
    for tag, items, (ssem, rsem, bufs, _) in rs_pending:
        bufs = split_wait(ssem, rsem, bufs, after, chip_exchange_copies, f"rs_chip_wait_{tag}")
        n = len(items)
        for i, (it, p, r) in enumerate(zip(items, bufs[:n], bufs[n:])):
            fin[it[1]] = chip_sum(p, r, chip_c, f"rs_chip_sum_{tag}_{i}", dest=fin.get(it[1]), slot=it[2],
                                  n_slots=it[3])
    filled = sibling_fill_halves([fin[n] for n in big_names], "rs_fill")
    for n, g in zip(big_names, filled):
        grads[n] = g
        update(n)

    return (loss, grad_x, *[grads[n] for n in names], *[deltas[n] for n in names],
            *[new_m[n] for n in names], *[new_v[n] for n in names])
```

```python
import math

import jax
import jax.numpy as jnp
from jax import lax
from jax.experimental import pallas as pl
from jax.experimental.pallas import tpu as pltpu

F32 = jnp.float32
BF16 = jnp.bfloat16
EPS = 1e-6
MESH_T = pl.DeviceIdType.MESH
VMEM_LIMIT_BYTES = 56 * 1024 * 1024
N_CHIPS = 4
N_DEV = 8
POOL_WINDOWS = (2, 4, 8, 16)
CHUNK = 128
SCAN_LANES = 8
SCAN_UNROLL = 4
ADAM_LR = 0.001
ADAM_B1 = 0.9
ADAM_B2 = 0.999
ADAM_EPS = 1e-08
ADAM_WD = 0.01
ADAM_STEP = 10
GELU_C = math.sqrt(2.0 / math.pi)
GELU_K = 0.044715

V_SHIFT, V_SCALE, V_GATE, V_GPRE, V_GPOST = 0, 1, 2, 3, 4
A_SHIFT, A_SCALE, A_GATE, A_GPRE, A_GPOST = 0, 1, 2, 3, 4

ANY = pl.BlockSpec(memory_space=pl.ANY)


def _cp(sem=None):
    if sem is None:
        return pltpu.CompilerParams(vmem_limit_bytes=VMEM_LIMIT_BYTES)
    return pltpu.CompilerParams(vmem_limit_bytes=VMEM_LIMIT_BYTES, dimension_semantics=sem)


def _dot(a, b):
    return jnp.dot(a, b, preferred_element_type=F32)


def _dot_nt(a, b):
    return lax.dot_general(a, b, (((1,), (1,)), ((), ())), preferred_element_type=F32)


def _dot_tn(a, b):
    return lax.dot_general(a, b, (((0,), (0,)), ((), ())), preferred_element_type=F32)


def _sigmoid(x):
    return 1.0 / (1.0 + jnp.exp(-x))


def _gelu(x):
    return 0.5 * x * (1.0 + jnp.tanh(GELU_C * (x + GELU_K * x * x * x)))


def _gelu_grad(x):
    t = jnp.tanh(GELU_C * (x + GELU_K * x * x * x))
    return 0.5 * (1.0 + t) + 0.5 * x * (1.0 - t * t) * GELU_C * (1.0 + 3.0 * GELU_K * x * x)


def _rowsum(v):
    return jnp.sum(v, axis=0, keepdims=True)


def _lanemean(v):
    return jnp.mean(v, axis=-1, keepdims=True)


def _row(ref, i):
    return ref[pl.ds(i, 1), :]


def _pre_fwd(x, vec_ref):
    r = lax.rsqrt(_lanemean(x * x) + EPS)
    return (x * r) * _row(vec_ref, V_GPRE) * (1.0 + _row(vec_ref, V_SCALE)) + _row(vec_ref, V_SHIFT)


def _pre_bwd(x, dh, vec_ref):
    g = _row(vec_ref, V_GPRE)
    sc = 1.0 + _row(vec_ref, V_SCALE)
    r = lax.rsqrt(_lanemean(x * x) + EPS)
    xn = x * r
    dhx = dh * xn
    t2 = dh * (g * sc)
    dx = r * (t2 - xn * _lanemean(t2 * xn))
    return dx, _rowsum(dh), _rowsum(dhx * g), _rowsum(dhx * sc)


def _post_fwd(x, f, vec_ref, rw):
    q = lax.rsqrt(_lanemean(f * f) + EPS)
    return x + (rw * _row(vec_ref, V_GATE)) * (f * q * _row(vec_ref, V_GPOST))


def _post_bwd(dout, f, vec_ref, rw):
    gp = _row(vec_ref, V_GPOST)
    gate = _row(vec_ref, V_GATE)
    q = lax.rsqrt(_lanemean(f * f) + EPS)
    fhat = f * q
    dgate = _rowsum(dout * (rw * fhat * gp))
    dy = dout * (rw * gate)
    dgpost = _rowsum(dy * fhat)
    t = dy * gp
    df = q * (t - fhat * _lanemean(t * fhat))
    return df, dgate, dgpost


def _acc_add(acc_ref, row, v):
    acc_ref[pl.ds(row, 1), :] += v


def ffn_fwd(x, vec, win_g, wout_g, name, tm=512):
    S, D = x.shape
    FS = win_g.shape[-1]
    FR = wout_g.shape[-2]

    def body(x_ref, vec_ref, win_hbm, wout_hbm, out_ref, h_ref, u_ref, f_ref, win_s, wout_s, sem):
        @pl.when(pl.program_id(0) == 0)
        def _():
            cps = [pltpu.make_async_copy(win_hbm.at[j], win_s.at[j], sem.at[j]) for j in range(N_CHIPS)]
            cps += [pltpu.make_async_copy(wout_hbm.at[j], wout_s.at[pl.ds(j * FR, FR), :], sem.at[N_CHIPS + j])
                    for j in range(N_CHIPS)]
            for cp in cps:
                cp.start()
            for cp in cps:
                cp.wait()

        xv = x_ref[...]
        h = _pre_fwd(xv, vec_ref).astype(BF16)
        h_ref[...] = h
        f = jnp.zeros((tm, D), F32)
        for j in range(2):
            a = _dot(h, win_s[j])
            b = _dot(h, win_s[2 + j])
            u_ref[j] = a.astype(BF16)
            u_ref[2 + j] = b.astype(BF16)
            act = (a * _sigmoid(a) * b).astype(BF16)
            f = f + _dot(act, wout_s[pl.ds(j * FS, FS), :])
        f_ref[...] = f
        out_ref[...] = _post_fwd(xv, f, vec_ref, 0.5)

    row = pl.BlockSpec((tm, D), lambda i: (i, 0))
    return pl.pallas_call(
        body, name=name, grid=(S // tm,),
        in_specs=[row, pl.BlockSpec((8, D), lambda i: (0, 0)), ANY, ANY],
        out_specs=[row, row, pl.BlockSpec((N_CHIPS, tm, FS), lambda i: (0, i, 0)), row],
        out_shape=[jax.ShapeDtypeStruct((S, D), F32), jax.ShapeDtypeStruct((S, D), BF16),
                   jax.ShapeDtypeStruct((N_CHIPS, S, FS), BF16), jax.ShapeDtypeStruct((S, D), F32)],
        scratch_shapes=[pltpu.VMEM((N_CHIPS, D, FS), BF16), pltpu.VMEM((N_CHIPS * FR, D), BF16),
                        pltpu.SemaphoreType.DMA((2 * N_CHIPS,))],
        compiler_params=_cp(("arbitrary",)),
    )(x, vec, win_g, wout_g)


def ffn_bwd(dout, x, f, u, vec, win_g, wout_g, name, tm=256):
    S, D = x.shape
    FS = win_g.shape[-1]
    FR = wout_g.shape[-2]

    def body(dout_ref, x_ref, f_ref, u_ref, vec_ref, win_hbm, wout_hbm,
             dx_ref, df_ref, du_ref, act_ref, acc_ref, win_s, wout_s, sem):
        @pl.when(pl.program_id(0) == 0)
        def _():
            cps = [pltpu.make_async_copy(win_hbm.at[j], win_s.at[j], sem.at[j]) for j in range(N_CHIPS)]
            cps += [pltpu.make_async_copy(wout_hbm.at[j], wout_s.at[pl.ds(j * FR, FR), :], sem.at[N_CHIPS + j])
                    for j in range(N_CHIPS)]
            for cp in cps:
                cp.start()
            acc_ref[...] = jnp.zeros_like(acc_ref)
            for cp in cps:
                cp.wait()

        dout_v = dout_ref[...]
        df, dgate, dgpost = _post_bwd(dout_v, f_ref[...], vec_ref, 0.5)
        dfb = df.astype(BF16)
        df_ref[...] = dfb
        dh = jnp.zeros((tm, D), F32)
        for j in range(2):
            a = u_ref[j].astype(F32)
            b = u_ref[2 + j].astype(F32)
            sig = _sigmoid(a)
            sl = a * sig
            dact = _dot_nt(dfb, wout_s[pl.ds(j * FS, FS), :])
            da = (dact * b * (sig * (1.0 + a * (1.0 - sig)))).astype(BF16)
            db = (dact * sl).astype(BF16)
            du_ref[j] = da
            du_ref[2 + j] = db
            act_ref[:, j * FS:(j + 1) * FS] = (sl * b).astype(BF16)
            dh = dh + _dot_nt(da, win_s[j]) + _dot_nt(db, win_s[2 + j])
        dx, dshift, dscale, dgpre = _pre_bwd(x_ref[...], dh, vec_ref)
        dx_ref[...] = dout_v + dx
        _acc_add(acc_ref, A_SHIFT, dshift)
        _acc_add(acc_ref, A_SCALE, dscale)
        _acc_add(acc_ref, A_GATE, dgate)
        _acc_add(acc_ref, A_GPRE, dgpre)
        _acc_add(acc_ref, A_GPOST, dgpost)

    row = pl.BlockSpec((tm, D), lambda i: (i, 0))
    ublk = pl.BlockSpec((N_CHIPS, tm, FS), lambda i: (0, i, 0))
    const = pl.BlockSpec((8, D), lambda i: (0, 0))
    return pl.pallas_call(
        body, name=name, grid=(S // tm,),
        in_specs=[row, row, row, ublk, const, ANY, ANY],
        out_specs=[row, row, ublk, pl.BlockSpec((tm, 2 * FS), lambda i: (i, 0)), const],
        out_shape=[jax.ShapeDtypeStruct((S, D), F32), jax.ShapeDtypeStruct((S, D), BF16),
                   jax.ShapeDtypeStruct((N_CHIPS, S, FS), BF16), jax.ShapeDtypeStruct((S, 2 * FS), BF16),
                   jax.ShapeDtypeStruct((8, D), F32)],
        scratch_shapes=[pltpu.VMEM((N_CHIPS, D, FS), BF16), pltpu.VMEM((N_CHIPS * FR, D), BF16),
                        pltpu.SemaphoreType.DMA((2 * N_CHIPS,))],
        compiler_params=_cp(("arbitrary",)),
    )(dout, x, f, u, vec, win_g, wout_g)


def tn_matmul(a, b, tk, name, dep=None):
    S, K = a.shape
    nb, _, tn = b.shape
    extra = [] if dep is None else [dep]

    def body(a_ref, b_ref, *rest):
        rest[-1][...] = _dot_tn(a_ref[...], b_ref[...]).astype(BF16)

    return pl.pallas_call(
        body, name=name, grid=(K // tk, nb),
        in_specs=[pl.BlockSpec((S, tk), lambda i, j: (0, i)), pl.BlockSpec((None, S, tn), lambda i, j: (j, 0, 0))]
        + [ANY] * len(extra),
        out_specs=pl.BlockSpec((None, tk, tn), lambda i, j: (j, i, 0)),
        out_shape=jax.ShapeDtypeStruct((nb, K, tn), BF16),
        compiler_params=_cp(("arbitrary", "arbitrary")),
    )(a, b, *extra)


def tn_matmul_cols(a, b, tk, tn, name):
    S, K = a.shape
    N = b.shape[1]

    def body(a_ref, b_ref, o_ref):
        o_ref[...] = _dot_tn(a_ref[...], b_ref[...]).astype(BF16)

    return pl.pallas_call(
        body, name=name, grid=(K // tk, N // tn),
        in_specs=[pl.BlockSpec((S, tk), lambda i, j: (0, i)), pl.BlockSpec((S, tn), lambda i, j: (0, j))],
        out_specs=pl.BlockSpec((None, tk, tn), lambda i, j: (j, i, 0)),
        out_shape=jax.ShapeDtypeStruct((N // tn, K, tn), BF16),
        compiler_params=_cp(("arbitrary", "arbitrary")),
    )(a, b)


def pre_matmul(x, vec, w3, name, tm=256):
    S, D = x.shape
    nj, _, Nj = w3.shape

    def body(x_ref, vec_ref, w_ref, h_ref, z_ref):
        h = _pre_fwd(x_ref[...], vec_ref).astype(BF16)
        h_ref[...] = h
        for j in range(nj):
            z_ref[:, j * Nj:(j + 1) * Nj] = _dot(h, w_ref[j])

    row = pl.BlockSpec((tm, D), lambda i: (i, 0))
    return pl.pallas_call(
        body, name=name, grid=(S // tm,),
        in_specs=[row, pl.BlockSpec((8, D), lambda i: (0, 0)), pl.BlockSpec((nj, D, Nj), lambda i: (0, 0, 0))],
        out_specs=[row, pl.BlockSpec((tm, nj * Nj), lambda i: (i, 0))],
        out_shape=[jax.ShapeDtypeStruct((S, D), BF16), jax.ShapeDtypeStruct((S, nj * Nj), F32)],
        compiler_params=_cp(("arbitrary",)),
    )(x, vec, w3)


def matmul_pre_bwd(dz, w3, x, dres, vec, name, tm=256):
    S, D = x.shape
    nj, _, Nj = w3.shape

    def body(dz_ref, w_ref, x_ref, dres_ref, vec_ref, dx_ref, acc_ref):
        @pl.when(pl.program_id(0) == 0)
        def _():
            acc_ref[...] = jnp.zeros_like(acc_ref)

        dh = jnp.zeros((tm, D), F32)
        for j in range(nj):
            dh = dh + _dot_nt(dz_ref[:, j * Nj:(j + 1) * Nj], w_ref[j])
        dx, dshift, dscale, dgpre = _pre_bwd(x_ref[...], dh, vec_ref)
        dx_ref[...] = dres_ref[...] + dx
        _acc_add(acc_ref, A_SHIFT, dshift)
        _acc_add(acc_ref, A_SCALE, dscale)
        _acc_add(acc_ref, A_GPRE, dgpre)

    row = pl.BlockSpec((tm, D), lambda i: (i, 0))
    const = pl.BlockSpec((8, D), lambda i: (0, 0))
    return pl.pallas_call(
        body, name=name, grid=(S // tm,),
        in_specs=[pl.BlockSpec((tm, nj * Nj), lambda i: (i, 0)), pl.BlockSpec((nj, D, Nj), lambda i: (0, 0, 0)),
                  row, row, const],
        out_specs=[row, const],
        out_shape=[jax.ShapeDtypeStruct((S, D), F32), jax.ShapeDtypeStruct((8, D), F32)],
        compiler_params=_cp(("arbitrary",)),
    )(dz, w3, x, dres, vec)


def matmul_post(act, w, x, vec, name, tm=256):
    S, D = x.shape
    K = act.shape[1]

    def body(act_ref, w_ref, x_ref, vec_ref, out_ref, f_ref):
        f = _dot(act_ref[...], w_ref[...])
        f_ref[...] = f
        out_ref[...] = _post_fwd(x_ref[...], f, vec_ref, 1.0)

    row = pl.BlockSpec((tm, D), lambda i: (i, 0))
    return pl.pallas_call(
        body, name=name, grid=(S // tm,),
        in_specs=[pl.BlockSpec((tm, K), lambda i: (i, 0)), pl.BlockSpec((K, D), lambda i: (0, 0)), row,
                  pl.BlockSpec((8, D), lambda i: (0, 0))],
        out_specs=[row, row],
        out_shape=[jax.ShapeDtypeStruct((S, D), F32), jax.ShapeDtypeStruct((S, D), F32)],
        compiler_params=_cp(("arbitrary",)),
    )(act, w, x, vec)


def post_bwd_matmul(dout, f, vec, w, name, tm=256):
    S, D = dout.shape
    K = w.shape[0]

    def body(dout_ref, f_ref, vec_ref, w_ref, df_ref, dact_ref, acc_ref):
        @pl.when(pl.program_id(0) == 0)
        def _():
            acc_ref[...] = jnp.zeros_like(acc_ref)

        df, dgate, dgpost = _post_bwd(dout_ref[...], f_ref[...], vec_ref, 1.0)
        dfb = df.astype(BF16)
        df_ref[...] = dfb
        dact_ref[...] = _dot_nt(dfb, w_ref[...])
        _acc_add(acc_ref, A_GATE, dgate)
        _acc_add(acc_ref, A_GPOST, dgpost)

    row = pl.BlockSpec((tm, D), lambda i: (i, 0))
    const = pl.BlockSpec((8, D), lambda i: (0, 0))
    return pl.pallas_call(
        body, name=name, grid=(S // tm,),
        in_specs=[row, row, const, pl.BlockSpec((K, D), lambda i: (0, 0))],
        out_specs=[row, pl.BlockSpec((tm, K), lambda i: (i, 0)), const],
        out_shape=[jax.ShapeDtypeStruct((S, D), BF16), jax.ShapeDtypeStruct((S, K), F32),
                   jax.ShapeDtypeStruct((8, D), F32)],
        compiler_params=_cp(("arbitrary",)),
    )(dout, f, vec, w)


def _band(w, transposed, prev):
    r = lax.broadcasted_iota(jnp.int32, (CHUNK, CHUNK), 1 if transposed else 0)
    c = lax.broadcasted_iota(jnp.int32, (CHUNK, CHUNK), 0 if transposed else 1)
    d = r - c
    m = (d + CHUNK < w) if prev else ((d >= 0) & (d < w))
    return jnp.where(m, 1.0, 0.0).astype(BF16)


def _split_hi_lo(a):
    hi = a.astype(BF16)
    lo = (a - hi.astype(F32)).astype(BF16)
    return hi, lo


def _pool_diff(a, ap, g, denom):
    w = POOL_WINDOWS[g]
    a_hi, a_lo = _split_hi_lo(a)
    p_hi, p_lo = _split_hi_lo(ap)
    mc = _band(w, False, False)
    mp = _band(w, False, True)
    win = _dot(mc, a_hi) + _dot(mc, a_lo) + _dot(mp, p_hi) + _dot(mp, p_lo)
    return win / denom - a


def _sgu_norm(v, lg, lb):
    mu = _lanemean(v)
    xc = v - mu
    rstd = lax.rsqrt(_lanemean(xc * xc) + EPS)
    xhat = xc * rstd
    return xhat, rstd, xhat * lg + lb


def _tril_mask():
    r = lax.broadcasted_iota(jnp.int32, (CHUNK, CHUNK), 0)
    c = lax.broadcasted_iota(jnp.int32, (CHUNK, CHUNK), 1)
    return r >= c


def _positions(i, w):
    r = lax.broadcasted_iota(jnp.int32, (CHUNK, 128), 0)
    pos = (i * CHUNK + r + 1).astype(F32)
    return jnp.minimum(pos, float(w))


def mixa_core_fwd(z, pool_w, pool_scale, ln_g, ln_b, sgu_w, sgu_bexp, name):
    S = z.shape[0]
    W = z.shape[1] // 3
    G = len(POOL_WINDOWS)
    GD = W // G

    def body(zc_ref, zp_ref, pw_ref, ps_ref, lg_ref, lb_ref, sw_ref, sb_ref, y_ref):
        i = pl.program_id(0)
        has_prev = jnp.where(i > 0, 1.0, 0.0)
        for g in range(G):
            sl = slice(g * GD, (g + 1) * GD)
            a = zc_ref[:, sl]
            ap = zp_ref[:, sl] * has_prev
            d = _pool_diff(a, ap, g, _positions(i, POOL_WINDOWS[g])).astype(BF16)
            y_ref[:, sl] = (_dot(d, pw_ref[g]) * ps_ref[:, sl]).astype(BF16)
        tril = _tril_mask()
        for hh in range(G):
            u = _gelu(zc_ref[:, W + hh * GD:W + (hh + 1) * GD])
            v = _gelu(zc_ref[:, 2 * W + hh * GD:2 * W + (hh + 1) * GD])
            sl = slice(hh * GD, (hh + 1) * GD)
            _, _, vn = _sgu_norm(v, lg_ref[:, sl], lb_ref[:, sl])
            wm = jnp.where(tril, sw_ref[hh], 0.0).astype(BF16)
            s = _dot(wm, vn.astype(BF16)) + sb_ref[hh]
            y_ref[:, W + hh * GD:W + (hh + 1) * GD] = (u * s).astype(BF16)

    vecw = pl.BlockSpec((1, W), lambda i: (0, 0))
    mats = pl.BlockSpec((G, GD, GD), lambda i: (0, 0, 0))
    return pl.pallas_call(
        body, name=name, grid=(S // CHUNK,),
        in_specs=[pl.BlockSpec((CHUNK, 3 * W), lambda i: (i, 0)),
                  pl.BlockSpec((CHUNK, W), lambda i: (jnp.maximum(i - 1, 0), 0)),
                  mats, vecw, vecw, vecw, mats, mats],
        out_specs=pl.BlockSpec((CHUNK, 2 * W), lambda i: (i, 0)),
        out_shape=jax.ShapeDtypeStruct((S, 2 * W), BF16),
        compiler_params=_cp(("arbitrary",)),
    )(z, z, pool_w, pool_scale, ln_g, ln_b, sgu_w, sgu_bexp)


def mixa_core_bwd(z, dy, pool_w, pool_scale, ln_g, ln_b, sgu_w, sgu_bexp, name):
    S = z.shape[0]
    W = z.shape[1] // 3
    G = len(POOL_WINDOWS)
    GD = W // G
    n_tiles = S // CHUNK

    def body(zc_ref, zp_ref, dyc_ref, dyn_ref, pw_ref, ps_ref, lg_ref, lb_ref, sw_ref, sb_ref,
             dz_ref, dpw_ref, dvec_ref, dsw_ref, dsb_ref):
        i = pl.program_id(0)

        @pl.when(i == 0)
        def _():
            dpw_ref[...] = jnp.zeros_like(dpw_ref)
            dvec_ref[...] = jnp.zeros_like(dvec_ref)
            dsw_ref[...] = jnp.zeros_like(dsw_ref)
            dsb_ref[...] = jnp.zeros_like(dsb_ref)

        has_prev = jnp.where(i > 0, 1.0, 0.0)
        has_next = jnp.where(i < n_tiles - 1, 1.0, 0.0)
        for g in range(G):
            w = POOL_WINDOWS[g]
            sl = slice(g * GD, (g + 1) * GD)
            a = zc_ref[:, sl]
            ap = zp_ref[:, sl] * has_prev
            den_c = _positions(i, w)
            den_n = _positions(i + 1, w)
            d = _pool_diff(a, ap, g, den_c).astype(BF16)
            ps = ps_ref[:, sl]
            pw = pw_ref[g]
            dyc = dyc_ref[:, sl]
            dvec_ref[pl.ds(0, 1), sl] += _rowsum(dyc * _dot(d, pw))
            dyp_c = (dyc * ps).astype(BF16)
            dyp_n = (dyn_ref[:, sl] * (ps * has_next)).astype(BF16)
            dpw_ref[g] += _dot_tn(d, dyp_c)
            dd_c = _dot_nt(dyp_c, pw)
            dd_n = _dot_nt(dyp_n, pw)
            da = (_dot(_band(w, True, False), (dd_c / den_c).astype(BF16))
                  + _dot(_band(w, True, True), (dd_n / den_n).astype(BF16)) - dd_c)
            dz_ref[:, sl] = da.astype(BF16)
        tril = _tril_mask()
        for hh in range(G):
            sl = slice(hh * GD, (hh + 1) * GD)
            zu = zc_ref[:, W + hh * GD:W + (hh + 1) * GD]
            zv = zc_ref[:, 2 * W + hh * GD:2 * W + (hh + 1) * GD]
            u = _gelu(zu)
            v = _gelu(zv)
            lg = lg_ref[:, sl]
            xhat, rstd, vn = _sgu_norm(v, lg, lb_ref[:, sl])
            vnb = vn.astype(BF16)
            wm = jnp.where(tril, sw_ref[hh], 0.0).astype(BF16)
            s = _dot(wm, vnb) + sb_ref[hh]
            dyb = dyc_ref[:, W + hh * GD:W + (hh + 1) * GD]
            du = dyb * s
            ds = dyb * u
            dsb_ref[:, hh:hh + 1] += jnp.sum(ds, axis=1, keepdims=True)
            dsb16 = ds.astype(BF16)
            dsw_ref[hh] += jnp.where(tril, _dot_nt(dsb16, vnb), 0.0)
            dvn = _dot_tn(wm, dsb16)
            dvec_ref[pl.ds(1, 1), sl] += _rowsum(dvn * xhat)
            dvec_ref[pl.ds(2, 1), sl] += _rowsum(dvn)
            dxh = dvn * lg
            dv = rstd * (dxh - _lanemean(dxh) - xhat * _lanemean(dxh * xhat))
            dz_ref[:, W + hh * GD:W + (hh + 1) * GD] = (du * _gelu_grad(zu)).astype(BF16)
            dz_ref[:, 2 * W + hh * GD:2 * W + (hh + 1) * GD] = (dv * _gelu_grad(zv)).astype(BF16)

    vecw = pl.BlockSpec((1, W), lambda i: (0, 0))
    mats = pl.BlockSpec((G, GD, GD), lambda i: (0, 0, 0))
    return pl.pallas_call(
        body, name=name, grid=(n_tiles,),
        in_specs=[pl.BlockSpec((CHUNK, 3 * W), lambda i: (i, 0)),
                  pl.BlockSpec((CHUNK, W), lambda i: (jnp.maximum(i - 1, 0), 0)),
                  pl.BlockSpec((CHUNK, 2 * W), lambda i: (i, 0)),
                  pl.BlockSpec((CHUNK, W), lambda i: (jnp.minimum(i + 1, n_tiles - 1), 0)),
                  mats, vecw, vecw, vecw, mats, mats],
        out_specs=[pl.BlockSpec((CHUNK, 3 * W), lambda i: (i, 0)), mats,
                   pl.BlockSpec((8, W), lambda i: (0, 0)), mats, pl.BlockSpec((CHUNK, G), lambda i: (0, 0))],
        out_shape=[jax.ShapeDtypeStruct((S, 3 * W), BF16), jax.ShapeDtypeStruct((G, GD, GD), F32),
                   jax.ShapeDtypeStruct((8, W), F32), jax.ShapeDtypeStruct((G, CHUNK, CHUNK), F32),
                   jax.ShapeDtypeStruct((CHUNK, G), F32)],
        compiler_params=_cp(("arbitrary",)),
    )(z, z, dy, dy, pool_w, pool_scale, ln_g, ln_b, sgu_w, sgu_bexp)


def _cmul(ar, ai, br, bi):
    return ar * br - ai * bi, ar * bi + ai * br


def _cpow(ar, ai, n):
    rr, ri = None, None
    br, bi = ar, ai
    while n:
        if n & 1:
            rr, ri = (br, bi) if rr is None else _cmul(rr, ri, br, bi)
        n >>= 1
        if n:
            br, bi = _cmul(br, bi, br, bi)
    return rr, ri


def _seg_rows(k):
    return pl.ds(pl.multiple_of(k * SCAN_LANES, SCAN_LANES), SCAN_LANES)


def _scan_fwd(xre, xim, carry, ar, ai, K):
    P = xre.shape[1]
    a8r = jnp.broadcast_to(ar, (SCAN_LANES, P))
    a8i = jnp.broadcast_to(ai, (SCAN_LANES, P))

    def local(k, c):
        pr, pi = c
        rows = _seg_rows(k)
        nr = a8r * pr - a8i * pi + xre[rows, :]
        ni = a8r * pi + a8i * pr + xim[rows, :]
        xre[rows, :] = nr
        xim[rows, :] = ni
        return nr, ni

    er, ei = lax.fori_loop(1, K, local, (xre[pl.ds(0, SCAN_LANES), :], xim[pl.ds(0, SCAN_LANES), :]),
                           unroll=SCAN_UNROLL)
    akr, aki = _cpow(ar, ai, K)
    cr = jnp.zeros((1, P), F32)
    ci = jnp.zeros((1, P), F32)
    carry[pl.ds(0, 1), :] = cr
    carry[pl.ds(SCAN_LANES, 1), :] = ci
    for j in range(1, SCAN_LANES):
        tr, ti = _cmul(akr, aki, cr, ci)
        cr = er[j - 1:j, :] + tr
        ci = ei[j - 1:j, :] + ti
        carry[pl.ds(j, 1), :] = cr
        carry[pl.ds(SCAN_LANES + j, 1), :] = ci
    cmr = carry[pl.ds(0, SCAN_LANES), :]
    cmi = carry[pl.ds(SCAN_LANES, SCAN_LANES), :]

    def fix(k, c):
        pr, pi = c
        rows = _seg_rows(k)
        tr, ti = _cmul(pr, pi, cmr, cmi)
        xre[rows, :] += tr
        xim[rows, :] += ti
        return _cmul(pr, pi, a8r, a8i)

    lax.fori_loop(0, K, fix, (a8r, a8i), unroll=SCAN_UNROLL)


def s5_core_fwd(u, ar, ai, bre, bim, cre_t, cim_t, dskip, name):
    S, D = u.shape
    nblk, UB, PB = bre.shape
    K = S // SCAN_LANES

    def body(u_ref, ar_ref, ai_ref, bre_ref, bim_ref, cre_ref, cim_ref, d_ref, y_ref, xre, xim, carry):
        uv = u_ref[...]
        ub = uv.astype(BF16)
        xre[...] = _dot(ub, bre_ref[...])
        xim[...] = _dot(ub, bim_ref[...])
        _scan_fwd(xre, xim, carry, ar_ref[...], ai_ref[...], K)
        y_ref[...] = (_dot(xre[...].astype(BF16), cre_ref[...]) - _dot(xim[...].astype(BF16), cim_ref[...])
                      + d_ref[...] * uv)

    ucol = pl.BlockSpec((S, UB), lambda i: (0, i))
    pvec = pl.BlockSpec((None, 1, PB), lambda i: (i, 0, 0))
    bmat = pl.BlockSpec((None, UB, PB), lambda i: (i, 0, 0))
    cmat = pl.BlockSpec((None, PB, UB), lambda i: (i, 0, 0))
    return pl.pallas_call(
        body, name=name, grid=(nblk,),
        in_specs=[ucol, pvec, pvec, bmat, bmat, cmat, cmat, pl.BlockSpec((1, UB), lambda i: (0, i))],
        out_specs=ucol,
        out_shape=jax.ShapeDtypeStruct((S, D), F32),
        scratch_shapes=[pltpu.VMEM((S, PB), F32), pltpu.VMEM((S, PB), F32), pltpu.VMEM((2 * SCAN_LANES, PB), F32)],
        compiler_params=_cp(("arbitrary",)),
    )(u, ar, ai, bre, bim, cre_t, cim_t, dskip)


def s5_core_bwd(u, dy, ar, ai, bre, bim, cre, cim, dskip, name):
    S, D = u.shape
    nblk, UB, PB = bre.shape
    K = S // SCAN_LANES

    def body(u_ref, dy_ref, ar_ref, ai_ref, bre_ref, bim_ref, cre_ref, cim_ref, d_ref,
             du_ref, dbre_ref, dbim_ref, dcre_ref, dcim_ref, dar_ref, dai_ref, dd_ref,
             xre, xim, gre, gim, carry, carry_b):
        ar = ar_ref[...]
        ai = ai_ref[...]
        uv = u_ref[...]
        ub = uv.astype(BF16)
        dyv = dy_ref[...]
        dyb = dyv.astype(BF16)
        xre[...] = _dot(ub, bre_ref[...])
        xim[...] = _dot(ub, bim_ref[...])
        _scan_fwd(xre, xim, carry, ar, ai, K)
        dcre_ref[...] = _dot_tn(xre[...].astype(BF16), dyb)
        dcim_ref[...] = -_dot_tn(xim[...].astype(BF16), dyb)
        gre[...] = _dot(dyb, cre_ref[...])
        gim[...] = -_dot(dyb, cim_ref[...])

        a8r = jnp.broadcast_to(ar, (SCAN_LANES, PB))
        a8i = jnp.broadcast_to(ai, (SCAN_LANES, PB))
        na8i = -a8i

        def local(s, c):
            k = K - 2 - s
            nr, ni = c
            rows = _seg_rows(k)
            tr = gre[rows, :] + a8r * nr + a8i * ni
            ti = gim[rows, :] + a8r * ni - a8i * nr
            gre[rows, :] = tr
            gim[rows, :] = ti
            return tr, ti

        last = _seg_rows(K - 1)
        fr, fi = lax.fori_loop(0, K - 1, local, (gre[last, :], gim[last, :]), unroll=SCAN_UNROLL)
        akr, aki = _cpow(ar, -ai, K)
        cr = jnp.zeros((1, PB), F32)
        ci = jnp.zeros((1, PB), F32)
        carry_b[pl.ds(SCAN_LANES - 1, 1), :] = cr
        carry_b[pl.ds(2 * SCAN_LANES - 1, 1), :] = ci
        for j in range(SCAN_LANES - 2, -1, -1):
            tr, ti = _cmul(akr, aki, cr, ci)
            cr = fr[j + 1:j + 2, :] + tr
            ci = fi[j + 1:j + 2, :] + ti
            carry_b[pl.ds(j, 1), :] = cr
            carry_b[pl.ds(SCAN_LANES + j, 1), :] = ci
        cbr = carry_b[pl.ds(0, SCAN_LANES), :]
        cbi = carry_b[pl.ds(SCAN_LANES, SCAN_LANES), :]

        def fix_rows(rows, c, xr, xi):
            pr, pi, dar, dai = c
            tr, ti = _cmul(pr, pi, cbr, cbi)
            g_r = gre[rows, :] + tr
            g_i = gim[rows, :] + ti
            gre[rows, :] = g_r
            gim[rows, :] = g_i
            dar = dar + g_r * xr + g_i * xi
            dai = dai + g_i * xr - g_r * xi
            nr, ni = _cmul(pr, pi, a8r, na8i)
            return nr, ni, dar, dai

        def fix(s, c):
            k = K - 1 - s
            prev = _seg_rows(k - 1)
            return fix_rows(_seg_rows(k), c, xre[prev, :], xim[prev, :])

        z8 = jnp.zeros((SCAN_LANES, PB), F32)
        c = lax.fori_loop(0, K - 1, fix, (a8r, na8i, z8, z8), unroll=SCAN_UNROLL)
        _, _, dar, dai = fix_rows(_seg_rows(0), c, carry[pl.ds(0, SCAN_LANES), :], carry[pl.ds(SCAN_LANES, SCAN_LANES), :])
        dar_ref[...] = _rowsum(dar)
        dai_ref[...] = _rowsum(dai)
        grb = gre[...].astype(BF16)
        gib = gim[...].astype(BF16)
        dbre_ref[...] = _dot_tn(ub, grb)
        dbim_ref[...] = _dot_tn(ub, gib)
        du_ref[...] = _dot_nt(grb, bre_ref[...]) + _dot_nt(gib, bim_ref[...]) + d_ref[...] * dyv
        dd_ref[...] = _rowsum(dyv * uv)

    ucol = pl.BlockSpec((S, UB), lambda i: (0, i))
    pvec = pl.BlockSpec((None, 1, PB), lambda i: (i, 0, 0))
    bmat = pl.BlockSpec((None, UB, PB), lambda i: (i, 0, 0))
    cmat = pl.BlockSpec((None, PB, UB), lambda i: (i, 0, 0))
    dvec = pl.BlockSpec((1, UB), lambda i: (0, i))
    return pl.pallas_call(
        body, name=name, grid=(nblk,),
        in_specs=[ucol, ucol, pvec, pvec, bmat, bmat, bmat, bmat, dvec],
        out_specs=[ucol, bmat, bmat, cmat, cmat, pvec, pvec, dvec],
        out_shape=[jax.ShapeDtypeStruct((S, D), F32),
                   jax.ShapeDtypeStruct((nblk, UB, PB), F32), jax.ShapeDtypeStruct((nblk, UB, PB), F32),
                   jax.ShapeDtypeStruct((nblk, PB, UB), F32), jax.ShapeDtypeStruct((nblk, PB, UB), F32),
                   jax.ShapeDtypeStruct((nblk, 1, PB), F32), jax.ShapeDtypeStruct((nblk, 1, PB), F32),
                   jax.ShapeDtypeStruct((1, D), F32)],
        scratch_shapes=[pltpu.VMEM((S, PB), F32), pltpu.VMEM((S, PB), F32), pltpu.VMEM((S, PB), F32),
                        pltpu.VMEM((S, PB), F32), pltpu.VMEM((2 * SCAN_LANES, PB), F32),
                        pltpu.VMEM((2 * SCAN_LANES, PB), F32)],
        compiler_params=_cp(("arbitrary",)),
    )(u, dy, ar, ai, bre, bim, cre, cim, dskip)


def s5_glu_fwd(y, wglu, x, vec, name, tm=256):
    S, D = x.shape
    NJ = wglu.shape[-1]

    def body(y_ref, w_ref, x_ref, vec_ref, out_ref, f_ref):
        g = _gelu(y_ref[...]).astype(BF16)
        f = jnp.concatenate([_dot(g, w_ref[j]) * _sigmoid(_dot(g, w_ref[2 + j])) for j in range(2)], axis=1)
        f_ref[...] = f
        out_ref[...] = _post_fwd(x_ref[...], f, vec_ref, 1.0)

    row = pl.BlockSpec((tm, D), lambda i: (i, 0))
    return pl.pallas_call(
        body, name=name, grid=(S // tm,),
        in_specs=[row, pl.BlockSpec((N_CHIPS, D, NJ), lambda i: (0, 0, 0)), row, pl.BlockSpec((8, D), lambda i: (0, 0))],
        out_specs=[row, row],
        out_shape=[jax.ShapeDtypeStruct((S, D), F32), jax.ShapeDtypeStruct((S, D), F32)],
        compiler_params=_cp(("arbitrary",)),
    )(y, wglu, x, vec)


def s5_glu_bwd(dout, f, y, vec, wglu, name, tm=256):
    S, D = dout.shape
    NJ = wglu.shape[-1]

    def body(dout_ref, f_ref, y_ref, vec_ref, w_ref, dy_ref, dab_ref, g_ref, acc_ref):
        @pl.when(pl.program_id(0) == 0)
        def _():
            acc_ref[...] = jnp.zeros_like(acc_ref)

        df, dgate, dgpost = _post_bwd(dout_ref[...], f_ref[...], vec_ref, 1.0)
        yv = y_ref[...]
        g = _gelu(yv).astype(BF16)
        g_ref[...] = g
        dg = jnp.zeros((tm, D), F32)
        for j in range(2):
            a = _dot(g, w_ref[j])
            sig = _sigmoid(_dot(g, w_ref[2 + j]))
            dfj = df[:, j * NJ:(j + 1) * NJ]
            da = (dfj * sig).astype(BF16)
            db = (dfj * a * sig * (1.0 - sig)).astype(BF16)
            dab_ref[:, j * NJ:(j + 1) * NJ] = da
            dab_ref[:, (2 + j) * NJ:(3 + j) * NJ] = db
            dg = dg + _dot_nt(da, w_ref[j]) + _dot_nt(db, w_ref[2 + j])
        dy_ref[...] = dg * _gelu_grad(yv)
        _acc_add(acc_ref, A_GATE, dgate)
        _acc_add(acc_ref, A_GPOST, dgpost)

    row = pl.BlockSpec((tm, D), lambda i: (i, 0))
    const = pl.BlockSpec((8, D), lambda i: (0, 0))
    return pl.pallas_call(
        body, name=name, grid=(S // tm,),
        in_specs=[row, row, row, const, pl.BlockSpec((N_CHIPS, D, NJ), lambda i: (0, 0, 0))],
        out_specs=[row, pl.BlockSpec((tm, N_CHIPS * NJ), lambda i: (i, 0)), row, const],
        out_shape=[jax.ShapeDtypeStruct((S, D), F32), jax.ShapeDtypeStruct((S, N_CHIPS * NJ), BF16),
                   jax.ShapeDtypeStruct((S, D), BF16), jax.ShapeDtypeStruct((8, D), F32)],
        compiler_params=_cp(("arbitrary",)),
    )(dout, f, y, vec, wglu)


def loss_head(y, target, name, tm=256):
    S, D = y.shape

    def body(y_ref, t_ref, d_ref, l_ref):
        @pl.when(pl.program_id(0) == 0)
        def _():
            l_ref[...] = jnp.zeros_like(l_ref)

        err = y_ref[...] - t_ref[...]
        d_ref[...] = err * (1.0 / D)
        l_ref[...] += jnp.sum(_rowsum(err * err), axis=1, keepdims=True)

    row = pl.BlockSpec((tm, D), lambda i: (i, 0))
    return pl.pallas_call(
        body, name=name, grid=(S // tm,),
        in_specs=[row, row],
        out_specs=[row, pl.BlockSpec((1, 1), lambda i: (0, 0))],
        out_shape=[jax.ShapeDtypeStruct((S, D), F32), jax.ShapeDtypeStruct((1, 1), F32)],
        compiler_params=_cp(("arbitrary",)),
    )(y, target)


def zero_after(dep, name):
    def body(dep_ref, o_ref):
        o_ref[...] = jnp.zeros_like(o_ref)

    return pl.pallas_call(body, name=name, in_specs=[ANY], out_specs=pl.BlockSpec(memory_space=pltpu.VMEM),
                          out_shape=jax.ShapeDtypeStruct((8, 128), F32), compiler_params=_cp())(dep)


def ada_mod(c_all, ada_w, ada_b_shard, name, tn=768, dep=None):
    B, D = c_all.shape
    L, _, NS = ada_w.shape
    extra = [] if dep is None else [dep]

    def body(c_ref, w_ref, b_ref, *rest):
        cv = c_ref[...]
        cond = (cv * _sigmoid(cv)).astype(BF16)
        rest[-1][...] = _dot(cond, w_ref[...].astype(BF16)) + b_ref[...]

    return pl.pallas_call(
        body, name=name, grid=(L, NS // tn),
        in_specs=[pl.BlockSpec((B, D), lambda l, j: (0, 0)), pl.BlockSpec((None, D, tn), lambda l, j: (l, 0, j)),
                  pl.BlockSpec((None, 1, tn), lambda l, j: (l, 0, j))] + [ANY] * len(extra),
        out_specs=pl.BlockSpec((None, B, tn), lambda l, j: (l, 0, j)),
        out_shape=jax.ShapeDtypeStruct((L, B, NS), F32),
        compiler_params=_cp(("arbitrary", "arbitrary")),
    )(c_all, ada_w, ada_b_shard, *extra)


def ada_grad(c_all, dmod, name, tn=768):
    B, D = c_all.shape
    L, _, NS = dmod.shape

    def body(c_ref, d_ref, o_ref):
        cv = c_ref[...]
        cond = (cv * _sigmoid(cv)).astype(BF16)
        o_ref[...] = _dot_tn(cond, d_ref[...].astype(BF16))

    return pl.pallas_call(
        body, name=name, grid=(L, NS // tn),
        in_specs=[pl.BlockSpec((B, D), lambda l, j: (0, 0)), pl.BlockSpec((None, B, tn), lambda l, j: (l, 0, j))],
        out_specs=pl.BlockSpec((None, D, tn), lambda l, j: (l, 0, j)),
        out_shape=jax.ShapeDtypeStruct((L, D, NS), F32),
        compiler_params=_cp(("arbitrary", "arbitrary")),
    )(c_all, dmod)


def sum_leading(x, name):
    n, R, C = x.shape

    def body(x_ref, o_ref):
        acc = x_ref[0].astype(F32)
        for i in range(1, n):
            acc = acc + x_ref[i].astype(F32)
        o_ref[...] = acc

    tr = _row_tile(R, 64)
    return pl.pallas_call(
        body, name=name, grid=(R // tr,),
        in_specs=[pl.BlockSpec((n, tr, C), lambda i: (0, i, 0))],
        out_specs=pl.BlockSpec((tr, C), lambda i: (i, 0)),
        out_shape=jax.ShapeDtypeStruct((R, C), F32),
        compiler_params=_cp(("arbitrary",)),
    )(x)


def sum_gathered(land, own, me_arr, name):
    n, R, C = land.shape

    def body(me_ref, land_ref, own_ref, o_ref):
        acc = jnp.zeros(o_ref.shape, F32)
        for i in range(n):
            acc = acc + jnp.where(me_ref[0] == i, own_ref[...], land_ref[i]).astype(F32)
        o_ref[...] = acc

    tr = _row_tile(R, 64)
    grid_spec = pltpu.PrefetchScalarGridSpec(
        num_scalar_prefetch=1, grid=(R // tr,),
        in_specs=[pl.BlockSpec((n, tr, C), lambda i, me: (0, i, 0)), pl.BlockSpec((tr, C), lambda i, me: (i, 0))],
        out_specs=pl.BlockSpec((tr, C), lambda i, me: (i, 0)))
    return pl.pallas_call(
        body, name=name, grid_spec=grid_spec,
        out_shape=jax.ShapeDtypeStruct((R, C), F32),
        compiler_params=_cp(("arbitrary",)),
    )(me_arr, land, own)


def _row_tile(R, cap=512):
    if R <= cap:
        return R
    for cand in (512, 384, 352, 256, 128, 64, 32, 16, 8):
        if cand <= cap and R % cand == 0:
            return cand
    return R


def adamw(w, g, m, v, name):
    R, C = w.shape
    tr = _row_tile(R, 256 if C > 1024 else 512)
    bc1 = 1.0 - ADAM_B1 ** ADAM_STEP
    bc2 = 1.0 - ADAM_B2 ** ADAM_STEP

    def body(w_ref, g_ref, m_ref, v_ref, d_ref, nm_ref, nv_ref):
        gv = g_ref[...]
        nm = ADAM_B1 * m_ref[...] + (1.0 - ADAM_B1) * gv
        nv = ADAM_B2 * v_ref[...] + (1.0 - ADAM_B2) * (gv * gv)
        nm_ref[...] = nm
        nv_ref[...] = nv
        d_ref[...] = -ADAM_LR * ((nm / bc1) / (jnp.sqrt(nv / bc2) + ADAM_EPS) + ADAM_WD * w_ref[...])

    blk = pl.BlockSpec((tr, C), lambda i: (i, 0))
    sd = jax.ShapeDtypeStruct((R, C), F32)
    return pl.pallas_call(
        body, name=name, grid=(R // tr,),
        in_specs=[blk, blk, blk, blk], out_specs=[blk, blk, blk], out_shape=[sd, sd, sd],
        compiler_params=_cp(("arbitrary",)),
    )(w, g, m, v)


def _as2d(a):
    if a.ndim == 1:
        return a.reshape(1, -1)
    return a.reshape(-1, a.shape[-1])


def adamw_nd(w, g, m, v, name):
    outs = adamw(_as2d(w), _as2d(g.reshape(w.shape)), _as2d(m), _as2d(v), name)
    return tuple(o.reshape(w.shape) for o in outs)


def _place():
    x, y, c = lax.axis_index("x"), lax.axis_index("y"), lax.axis_index("c")
    chips = [(1 - x, y), (x, 1 - y), (1 - x, 1 - y)]
    return x, y, c, chips


def allgather_small(xs, name):
    m_per, n = xs.shape

    def body(x_ref, out_ref, send_sems, recv_sems, local_sem):
        x, y, c, chips = _place()
        me, sibling = (x, y, c), (x, y, 1 - c)

        def rows(px, py, pc):
            return out_ref.at[pl.ds((4 * px + 2 * py + pc) * m_per, m_per), :]

        def copy(k, block, to, src=None):
            return pltpu.make_async_remote_copy(
                src_ref=rows(*block) if src is None else src, dst_ref=rows(*block),
                send_sem=send_sems.at[k], recv_sem=recv_sems.at[k], device_id=to, device_id_type=MESH_T)

        mine = pltpu.make_async_copy(x_ref, rows(*me), local_sem)
        mine.start()
        first = [copy(0, me, sibling, src=x_ref)]
        first += [copy(1 + j, me, (*chip, c), src=x_ref) for j, chip in enumerate(chips)]
        for cp in first:
            cp.start()
        passed = [copy(4 + j, (*chip, c), sibling) for j, chip in enumerate(chips)]
        for j, chip in enumerate(chips):
            copy(1 + j, (*chip, c), me).wait_recv()
            passed[j].start()
        copy(0, sibling, me).wait_recv()
        for j, chip in enumerate(chips):
            copy(4 + j, (*chip, 1 - c), me).wait_recv()
        for cp in first + passed:
            cp.wait_send()
        mine.wait()

    return pl.pallas_call(
        body, name=name,
        out_shape=jax.ShapeDtypeStruct((N_DEV * m_per, n), xs.dtype),
        in_specs=[pl.BlockSpec(memory_space=pltpu.VMEM)],
        out_specs=pl.BlockSpec(memory_space=pltpu.VMEM),
        scratch_shapes=[pltpu.SemaphoreType.DMA((7,)), pltpu.SemaphoreType.DMA((7,)), pltpu.SemaphoreType.DMA],
        compiler_params=_cp(),
    )(xs)


def gather_weights(shards, name):
    n = len(shards)

    def body(*refs):
        ins, outs = refs[:n], refs[n:2 * n]
        send_sems, recv_sems, fsend_sems, frecv_sems, local_sems = refs[2 * n:]
        x, y, c, chips = _place()
        me_chip = 2 * x + y
        sibling = (x, y, 1 - c)
        locs = []
        for a in range(n):
            cp = pltpu.make_async_copy(ins[a], outs[a].at[me_chip], local_sems.at[a])
            cp.start()
            locs.append(cp)

        def ici(a, k, src_chip, to):
            return pltpu.make_async_remote_copy(
                src_ref=ins[a].at[c], dst_ref=outs[a].at[src_chip, c],
                send_sem=send_sems.at[a * 3 + k], recv_sem=recv_sems.at[a * 3 + k], device_id=to, device_id_type=MESH_T)

        def d2d(a, k, src_chip, half):
            return pltpu.make_async_remote_copy(
                src_ref=outs[a].at[src_chip, half], dst_ref=outs[a].at[src_chip, half],
                send_sem=fsend_sems.at[a * 3 + k], recv_sem=frecv_sems.at[a * 3 + k],
                device_id=sibling, device_id_type=MESH_T)

        firsts = []
        for k, (cx, cy) in enumerate(chips):
            for a in range(n):
                cp = ici(a, k, me_chip, (cx, cy, c))
                cp.start()
                firsts.append(cp)
        passed = []
        for k, (cx, cy) in enumerate(chips):
            for a in range(n):
                ici(a, k, 2 * cx + cy, (cx, cy, c)).wait_recv()
                cp = d2d(a, k, 2 * cx + cy, c)
                cp.start()
                passed.append(cp)
        for k, (cx, cy) in enumerate(chips):
            for a in range(n):
                d2d(a, k, 2 * cx + cy, 1 - c).wait_recv()
        for cp in firsts + passed:
            cp.wait_send()
        for cp in locs:
            cp.wait()

    return pl.pallas_call(
        body, name=name,
        out_shape=[jax.ShapeDtypeStruct((N_CHIPS,) + s.shape, s.dtype) for s in shards],
        in_specs=[ANY] * n, out_specs=[ANY] * n,
        scratch_shapes=[pltpu.SemaphoreType.DMA((3 * n,)), pltpu.SemaphoreType.DMA((3 * n,)),
                        pltpu.SemaphoreType.DMA((3 * n,)), pltpu.SemaphoreType.DMA((3 * n,)),
                        pltpu.SemaphoreType.DMA((n,))],
        compiler_params=_cp(),
    )(*shards)


def _half_rows(ref, half, rh):
    idx = (slice(None),) * (len(ref.shape) - 2) + (pl.ds(pl.multiple_of(half * rh, 16), rh), slice(None))
    return ref.at[idx]


def sibling_exchange_halves(grads, name):
    n = len(grads)

    def body(*refs):
        ins, outs = refs[:n], refs[n:2 * n]
        send_sems, recv_sems = refs[2 * n:]
        x, y, c, _ = _place()
        cps = []
        for a in range(n):
            rh = ins[a].shape[-2] // 2
            cp = pltpu.make_async_remote_copy(
                src_ref=_half_rows(ins[a], 1 - c, rh), dst_ref=outs[a],
                send_sem=send_sems.at[a], recv_sem=recv_sems.at[a], device_id=(x, y, 1 - c), device_id_type=MESH_T)
            cp.start()
            cps.append(cp)
        for cp in cps:
            cp.wait()

    return pl.pallas_call(
        body, name=name,
        out_shape=[jax.ShapeDtypeStruct(g.shape[:-2] + (g.shape[-2] // 2, g.shape[-1]), g.dtype) for g in grads],
        in_specs=[ANY] * n, out_specs=[ANY] * n,
        scratch_shapes=[pltpu.SemaphoreType.DMA((n,)), pltpu.SemaphoreType.DMA((n,))],
        compiler_params=_cp(),
    )(*grads)


def pair_sum(g, recv, cidx, name):
    L, NS, R, C = g.shape
    rh = R // 2
    tr = _row_tile(rh, 256 if C > 1024 else 512)
    nt = rh // tr

    def body(c_ref, g_ref, r_ref, o_ref):
        o_ref[...] = (g_ref[...].astype(F32) + r_ref[...].astype(F32)).astype(BF16)

    grid_spec = pltpu.PrefetchScalarGridSpec(
        num_scalar_prefetch=1, grid=(L, NS, nt),
        in_specs=[pl.BlockSpec((None, None, tr, C), lambda l, s, t, c: (l, s, c[0] * nt + t, 0)),
                  pl.BlockSpec((None, None, tr, C), lambda l, s, t, c: (l, s, t, 0))],
        out_specs=pl.BlockSpec((None, None, tr, C), lambda l, s, t, c: (l, s, t, 0)))
    return pl.pallas_call(
        body, name=name, grid_spec=grid_spec,
        out_shape=jax.ShapeDtypeStruct((L, NS, rh, C), BF16),
        compiler_params=_cp(("arbitrary", "arbitrary", "arbitrary")),
    )(cidx, g, recv)


def chip_exchange(parts, name):
    n = len(parts)

    def body(*refs):
        ins, outs = refs[:n], refs[n:2 * n]
        send_sems, recv_sems = refs[2 * n:]
        x, y, c, chips = _place()
        cps = []
        for k, (cx, cy) in enumerate(chips):
            for a in range(n):
                cp = pltpu.make_async_remote_copy(
                    src_ref=ins[a].at[:, 2 * cx + cy], dst_ref=outs[a].at[k],
                    send_sem=send_sems.at[a * 3 + k], recv_sem=recv_sems.at[a * 3 + k],
                    device_id=(cx, cy, c), device_id_type=MESH_T)
                cp.start()
                cps.append(cp)
        for cp in cps:
            cp.wait()

    return pl.pallas_call(
        body, name=name,
        out_shape=[jax.ShapeDtypeStruct((3, p.shape[0]) + p.shape[2:], p.dtype) for p in parts],
        in_specs=[ANY] * n, out_specs=[ANY] * n,
        scratch_shapes=[pltpu.SemaphoreType.DMA((3 * n,)), pltpu.SemaphoreType.DMA((3 * n,))],
        compiler_params=_cp(),
    )(*parts)


def chip_sum(part, recv, chip_c, name, dest=None, slot=0, n_slots=1):
    _, NS, RH, C = part.shape
    tr = _row_tile(RH, 256 if C > 1024 else 512)
    nt = RH // tr

    def body(cc_ref, p_ref, r_ref, *rest):
        o_ref = rest[-1]
        acc = p_ref[...].astype(F32)
        for k in range(3):
            acc = acc + r_ref[k].astype(F32)
        o_ref[...] = acc

    in_specs = [pl.BlockSpec((None, None, tr, C), lambda t, cc: (0, cc[0], t, 0)),
                pl.BlockSpec((3, None, tr, C), lambda t, cc: (0, 0, t, 0))]
    args = [chip_c, part, recv]
    aliases = {}
    if dest is not None:
        in_specs.append(ANY)
        args.append(dest)
        aliases = {3: 0}
    grid_spec = pltpu.PrefetchScalarGridSpec(
        num_scalar_prefetch=1, grid=(nt,), in_specs=in_specs,
        out_specs=pl.BlockSpec((None, tr, C), lambda t, cc: (slot, cc[1] * nt + t, 0)))
    return pl.pallas_call(
        body, name=name, grid_spec=grid_spec,
        out_shape=jax.ShapeDtypeStruct((n_slots, 2 * RH, C), F32),
        input_output_aliases=aliases,
        compiler_params=_cp(("arbitrary",)),
    )(*args)


def sibling_fill_halves(bufs, name):
    n = len(bufs)

    def body(*refs):
        outs = refs[n:2 * n]
        send_sems, recv_sems = refs[2 * n:]
        x, y, c, _ = _place()
        cps = []
        for a in range(n):
            rh = outs[a].shape[-2] // 2
            cp = pltpu.make_async_remote_copy(
                src_ref=_half_rows(outs[a], c, rh), dst_ref=_half_rows(outs[a], c, rh),
                send_sem=send_sems.at[a], recv_sem=recv_sems.at[a], device_id=(x, y, 1 - c), device_id_type=MESH_T)
            cp.start()
            cps.append(cp)
        for a, cp in enumerate(cps):
            cp.wait_send()
        for a in range(n):
            rh = outs[a].shape[-2] // 2
            pltpu.make_async_remote_copy(
                src_ref=_half_rows(outs[a], 1 - c, rh), dst_ref=_half_rows(outs[a], 1 - c, rh),
                send_sem=send_sems.at[a], recv_sem=recv_sems.at[a], device_id=(x, y, 1 - c),
                device_id_type=MESH_T).wait_recv()

    return pl.pallas_call(
        body, name=name,
        out_shape=[jax.ShapeDtypeStruct(b.shape, b.dtype) for b in bufs],
        in_specs=[ANY] * n, out_specs=[ANY] * n,
        input_output_aliases={a: a for a in range(n)},
        scratch_shapes=[pltpu.SemaphoreType.DMA((n,)), pltpu.SemaphoreType.DMA((n,))],
        compiler_params=_cp(),
    )(*bufs)


HBM_SPEC = pl.BlockSpec(memory_space=pltpu.HBM)
SEM_SPEC = pl.BlockSpec(memory_space=pltpu.SEMAPHORE)
DATAFLOW_EFFECT = pltpu.SideEffectType.DATAFLOW_SIDE_EFFECTING


def split_start(bufs, copies_fn, n_copies, name, dep=None):
    nb = len(bufs)
    extra = [] if dep is None else [dep]

    def body(*refs):
        ne = len(extra)
        send_sems, recv_sems = refs[nb + ne], refs[nb + ne + 1]
        token = refs[-1]
        for cp in copies_fn(refs[:nb], send_sems, recv_sems):
            cp.start()
        token[...] = jnp.zeros_like(token)

    outs = pl.pallas_call(
        body, name=name,
        out_shape=(pltpu.SemaphoreType.DMA((n_copies,)), pltpu.SemaphoreType.DMA((n_copies,)),
                   *[pltpu.HBM(b.shape, b.dtype) for b in bufs], jax.ShapeDtypeStruct((8, 128), F32)),
        in_specs=[HBM_SPEC] * nb + [ANY] * len(extra),
        out_specs=(SEM_SPEC, SEM_SPEC, *[HBM_SPEC] * nb, pl.BlockSpec(memory_space=pltpu.VMEM)),
        input_output_aliases={i: 2 + i for i in range(nb)},
        compiler_params=pltpu.CompilerParams(has_side_effects=DATAFLOW_EFFECT),
    )(*[pltpu.with_memory_space_constraint(b, pltpu.HBM) for b in bufs], *extra)
    return outs[0], outs[1], list(outs[2:2 + nb]), outs[-1]


def split_wait(send_sems, recv_sems, bufs, after, copies_fn, name):
    nb = len(bufs)

    def body(*refs):
        for cp in copies_fn(refs[:nb], refs[nb], refs[nb + 1]):
            cp.wait_send()
            cp.wait_recv()

    outs = pl.pallas_call(
        body, name=name,
        out_shape=tuple(pltpu.HBM(b.shape, b.dtype) for b in bufs),
        in_specs=[HBM_SPEC] * nb + [SEM_SPEC, SEM_SPEC, ANY],
        out_specs=tuple([HBM_SPEC] * nb),
        input_output_aliases={i: i for i in range(nb)},
        compiler_params=pltpu.CompilerParams(has_side_effects=DATAFLOW_EFFECT),
    )(*bufs, send_sems, recv_sems, after)
    return list(outs)


def chip_exchange_copies(refs, send_sems, recv_sems):
    n = len(refs) // 2
    x, y, c, chips = _place()
    cps = []
    for a in range(n):
        for k, (cx, cy) in enumerate(chips):
            cps.append(pltpu.make_async_remote_copy(
                src_ref=refs[a].at[:, 2 * cx + cy], dst_ref=refs[n + a].at[k],
                send_sem=send_sems.at[a * 3 + k], recv_sem=recv_sems.at[a * 3 + k],
                device_id=(cx, cy, c), device_id_type=MESH_T))
    return cps


def allgather_direct_copies(refs, send_sems, recv_sems):
    x, y, c, _ = _place()
    me = 4 * x + 2 * y + c
    cps = []
    for r in range(1, N_DEV):
        fx, fy, fc = (r >> 2) & 1, (r >> 1) & 1, r & 1
        peer = (1 - x if fx else x, 1 - y if fy else y, 1 - c if fc else c)
        cps.append(pltpu.make_async_remote_copy(
            src_ref=refs[0], dst_ref=refs[1].at[me], send_sem=send_sems.at[r - 1], recv_sem=recv_sems.at[r - 1],
            device_id=peer, device_id_type=MESH_T))
    return cps


def pair_exchange_copies(refs, send_sems, recv_sems):
    n = len(refs) // 2
    x, y, c, _ = _place()
    return [pltpu.make_async_remote_copy(
        src_ref=_half_rows(refs[a], 1 - c, refs[a].shape[-2] // 2), dst_ref=refs[n + a],
        send_sem=send_sems.at[a], recv_sem=recv_sems.at[a], device_id=(x, y, 1 - c), device_id_type=MESH_T)
        for a in range(n)]


def weight_ici_copies(refs, send_sems, recv_sems):
    x, y, c, chips = _place()
    cps = []
    for a in range(len(refs)):
        for k, (cx, cy) in enumerate(chips):
            cps.append(pltpu.make_async_remote_copy(
                src_ref=refs[a].at[2 * x + y, c], dst_ref=refs[a].at[2 * x + y, c],
                send_sem=send_sems.at[a * 3 + k], recv_sem=recv_sems.at[a * 3 + k],
                device_id=(cx, cy, c), device_id_type=MESH_T))
    return cps


def weight_d2d_copies(refs, send_sems, recv_sems):
    x, y, c, chips = _place()
    cps = []
    for a in range(len(refs)):
        for k, (cx, cy) in enumerate(chips):
            cps.append(pltpu.make_async_remote_copy(
                src_ref=refs[a].at[2 * cx + cy, c], dst_ref=refs[a].at[2 * cx + cy, c],
                send_sem=send_sems.at[a * 3 + k], recv_sem=recv_sems.at[a * 3 + k],
                device_id=(x, y, 1 - c), device_id_type=MESH_T))
    return cps


def cast_place(w, lead, chip_arr, name, dep=None):
    R, C = w.shape[-2:]
    tr = _row_tile(R, 256 if C > 1024 else 512)

    def body(chip_ref, w_ref, *rest):
        rest[-1][...] = w_ref[...].astype(BF16)

    in_specs = [pl.BlockSpec((None,) * len(lead) + (tr, C), lambda t, ch: (*lead, t, 0))]
    args = [chip_arr, w]
    if dep is not None:
        in_specs.append(ANY)
        args.append(dep)
    grid_spec = pltpu.PrefetchScalarGridSpec(
        num_scalar_prefetch=1, grid=(R // tr,), in_specs=in_specs,
        out_specs=pl.BlockSpec((None, tr, C), lambda t, ch: (ch[0], t, 0)))
    return pl.pallas_call(
        body, name=name, grid_spec=grid_spec,
        out_shape=jax.ShapeDtypeStruct((N_CHIPS, R, C), BF16),
        compiler_params=_cp(("arbitrary",)),
    )(*args)


def _s5_prepare(lam_re, lam_im, log_dt, b_re, b_im, c_re, c_im, groups_per_block):
    G, P = lam_re.shape
    N = b_re.shape[-1]
    gb = groups_per_block
    nblk = G // gb
    dt = jnp.exp(log_dt)[:, None]
    e = jnp.exp(lam_re * dt)
    a_re = e * jnp.cos(lam_im * dt)
    a_im = e * jnp.sin(lam_im * dt)
    n2 = lam_re * lam_re + lam_im * lam_im
    co_re = ((a_re - 1.0) * lam_re + a_im * lam_im) / n2
    co_im = (a_im * lam_re - (a_re - 1.0) * lam_im) / n2
    bb_re = co_re[..., None] * b_re - co_im[..., None] * b_im
    bb_im = co_re[..., None] * b_im + co_im[..., None] * b_re
    eye = jnp.eye(gb, dtype=F32)

    def blockdiag_np(m):
        m = m.reshape(nblk, gb, N, P)
        return jnp.einsum('bgnp,gh->bgnhp', m, eye).reshape(nblk, gb * N, gb * P)

    b_np_re = jnp.swapaxes(bb_re, 1, 2)
    b_np_im = jnp.swapaxes(bb_im, 1, 2)
    return (a_re.reshape(nblk, 1, gb * P), a_im.reshape(nblk, 1, gb * P),
            blockdiag_np(b_np_re), blockdiag_np(b_np_im), blockdiag_np(c_re), blockdiag_np(c_im))


def _to_scan_order(a):
    S, D = a.shape
    return a.reshape(SCAN_LANES, S // SCAN_LANES, D).transpose(1, 0, 2).reshape(S, D)


def _from_scan_order(a):
    S, D = a.shape
    return a.reshape(S // SCAN_LANES, SCAN_LANES, D).transpose(1, 0, 2).reshape(S, D)


def _pad_rows(a, mult=8):
    r = (-a.shape[0]) % mult
    if r:
        a = jnp.concatenate([a, jnp.zeros((r, a.shape[1]), a.dtype)], axis=0)
    return a


def _pack_rows(arrs, width):
    parts, offs, o = [], [], 0
    for a in arrs:
        flat = a.reshape(-1)
        r = (-flat.shape[0]) % (16 * width)
        if r:
            flat = jnp.concatenate([flat, jnp.zeros((r,), flat.dtype)])
        p = flat.reshape(-1, width)
        parts.append(p)
        offs.append((o, a.shape, a.size))
        o += p.shape[0]
    if o % 64:
        parts.append(jnp.zeros((64 - o % 64, width), parts[0].dtype))
    return jnp.concatenate(parts, axis=0), offs


def _unpack_rows(packed, offs):
    outs = []
    for o, shape, size in offs:
        rows = -(-size // packed.shape[1])
        outs.append(packed[o:o + rows].reshape(-1)[:size].reshape(shape))
    return outs


def kernel(x, c, ada_w, ada_b, norm_pre, norm_post, ffn_w_in, ffn_w_out, ab_w_in, pool_w, pool_scale, sgu_ln_g, sgu_ln_b, sgu_w, sgu_b, ab_w_out, ssm_w_in, ssm_lam_re, ssm_lam_im, ssm_b_re, ssm_b_im, ssm_c_re, ssm_c_im, ssm_d, ssm_log_dt, ssm_w_glu, loss_target, m_ada_w, m_ada_b, m_norm_pre, m_norm_post, m_ffn_w_in, m_ffn_w_out, m_ab_w_in, m_pool_w, m_pool_scale, m_sgu_ln_g, m_sgu_ln_b, m_sgu_w, m_sgu_b, m_ab_w_out, m_ssm_w_in, m_ssm_lam_re, m_ssm_lam_im, m_ssm_b_re, m_ssm_b_im, m_ssm_c_re, m_ssm_c_im, m_ssm_d, m_ssm_log_dt, m_ssm_w_glu, v_ada_w, v_ada_b, v_norm_pre, v_norm_post, v_ffn_w_in, v_ffn_w_out, v_ab_w_in, v_pool_w, v_pool_scale, v_sgu_ln_g, v_sgu_ln_b, v_sgu_w, v_sgu_b, v_ab_w_out, v_ssm_w_in, v_ssm_lam_re, v_ssm_lam_im, v_ssm_b_re, v_ssm_b_im, v_ssm_c_re, v_ssm_c_im, v_ssm_d, v_ssm_log_dt, v_ssm_w_glu):
    weights = dict(ada_w=ada_w, ada_b=ada_b, norm_pre=norm_pre, norm_post=norm_post, ffn_w_in=ffn_w_in,
                   ffn_w_out=ffn_w_out, ab_w_in=ab_w_in, pool_w=pool_w, pool_scale=pool_scale, sgu_ln_g=sgu_ln_g,
                   sgu_ln_b=sgu_ln_b, sgu_w=sgu_w, sgu_b=sgu_b, ab_w_out=ab_w_out, ssm_w_in=ssm_w_in,
                   ssm_lam_re=ssm_lam_re, ssm_lam_im=ssm_lam_im, ssm_b_re=ssm_b_re, ssm_b_im=ssm_b_im,
                   ssm_c_re=ssm_c_re, ssm_c_im=ssm_c_im, ssm_d=ssm_d, ssm_log_dt=ssm_log_dt, ssm_w_glu=ssm_w_glu)
    m_in = dict(ada_w=m_ada_w, ada_b=m_ada_b, norm_pre=m_norm_pre, norm_post=m_norm_post, ffn_w_in=m_ffn_w_in,
                ffn_w_out=m_ffn_w_out, ab_w_in=m_ab_w_in, pool_w=m_pool_w, pool_scale=m_pool_scale,
                sgu_ln_g=m_sgu_ln_g, sgu_ln_b=m_sgu_ln_b, sgu_w=m_sgu_w, sgu_b=m_sgu_b, ab_w_out=m_ab_w_out,
                ssm_w_in=m_ssm_w_in, ssm_lam_re=m_ssm_lam_re, ssm_lam_im=m_ssm_lam_im, ssm_b_re=m_ssm_b_re,
                ssm_b_im=m_ssm_b_im, ssm_c_re=m_ssm_c_re, ssm_c_im=m_ssm_c_im, ssm_d=m_ssm_d,
                ssm_log_dt=m_ssm_log_dt, ssm_w_glu=m_ssm_w_glu)
    v_in = dict(ada_w=v_ada_w, ada_b=v_ada_b, norm_pre=v_norm_pre, norm_post=v_norm_post, ffn_w_in=v_ffn_w_in,
                ffn_w_out=v_ffn_w_out, ab_w_in=v_ab_w_in, pool_w=v_pool_w, pool_scale=v_pool_scale,
                sgu_ln_g=v_sgu_ln_g, sgu_ln_b=v_sgu_ln_b, sgu_w=v_sgu_w, sgu_b=v_sgu_b, ab_w_out=v_ab_w_out,
                ssm_w_in=v_ssm_w_in, ssm_lam_re=v_ssm_lam_re, ssm_lam_im=v_ssm_lam_im, ssm_b_re=v_ssm_b_re,
                ssm_b_im=v_ssm_b_im, ssm_c_re=v_ssm_c_re, ssm_c_im=v_ssm_c_im, ssm_d=v_ssm_d,
                ssm_log_dt=v_ssm_log_dt, ssm_w_glu=v_ssm_w_glu)
    names = list(weights.keys())

    xi, yi, ci = lax.axis_index("x"), lax.axis_index("y"), lax.axis_index("c")
    chip = 2 * xi + yi
    me = 4 * xi + 2 * yi + ci
    S, D = x.shape[1], x.shape[2]
    L = ada_w.shape[0]
    NSUB = norm_pre.shape[1]
    DS = norm_pre.shape[2]
    FS = ffn_w_in.shape[-1]
    FR = ffn_w_out.shape[-2]
    x0 = x[0]
    target = loss_target[0]

    small_parts = [_pad_rows(p) for p in (c.reshape(D // DS, DS), norm_pre.reshape(L * NSUB, DS),
                                          norm_post.reshape(L * NSUB, DS), ssm_d.reshape(1, DS))]
    small_offs = [0]
    for p in small_parts:
        small_offs.append(small_offs[-1] + p.shape[0])
    small_all = allgather_small(jnp.concatenate(small_parts, axis=0), "ag_small").reshape(N_DEV, -1, DS)
    c_all = small_all[:, :D // DS].reshape(N_DEV, D)
    per_chip = small_all[0::2]
    o = small_offs[1]
    g_pre_full = jnp.moveaxis(per_chip[:, o:o + L * NSUB], 0, 1).reshape(L, NSUB, D)
    o = small_offs[2]
    g_post_full = jnp.moveaxis(per_chip[:, o:o + L * NSUB], 0, 1).reshape(L, NSUB, D)
    o = small_offs[3]
    d_full = jnp.moveaxis(per_chip[:, o:o + 1], 0, 1).reshape(1, D)

    pieces = []
    for l in range(L):
        pieces.append((f"ffn_{l}_0", [(ffn_w_in, (l, 0)), (ffn_w_out, (l, 0))]))
        if l % 2 == 0:
            pieces.append((f"mix_{l}", [(ab_w_in, (l // 2,)), (ab_w_out, (l // 2,))]))
        else:
            pieces.append((f"mix_{l}", [(ssm_w_in, (l // 2,)), (ssm_w_glu, (l // 2,))]))
        pieces.append((f"ffn_{l}_1", [(ffn_w_in, (l, 1)), (ffn_w_out, (l, 1))]))
    tags = [tag for tag, _ in pieces]
    chip_arr = chip.reshape(1).astype(jnp.int32)
    wg_started, wg_passing = {}, {}

    def cast_piece(tag, ws, dep):
        wg_started[tag] = [cast_place(w, lead, chip_arr, f"wg_cast_{tag}_{i}", dep=dep).reshape(
            N_CHIPS, 2, w.shape[-2] // 2, w.shape[-1]) for i, (w, lead) in enumerate(ws)]

    def start_piece(tag, dep):
        bufs = wg_started[tag]
        wg_started[tag] = split_start(bufs, weight_ici_copies, 3 * len(bufs), f"wg_ici_start_{tag}", dep=dep)
        return wg_started[tag][3]

    def weights_arrived(i, after):
        ssem, rsem, bufs, _ = wg_started[tags[i]]
        lands = split_wait(ssem, rsem, bufs, after, weight_ici_copies, f"wg_ici_wait_{tags[i]}")
        wg_passing[i] = split_start(lands, weight_d2d_copies, 3 * len(lands), f"wg_d2d_start_{tags[i]}")
        return wg_passing[i][3][0, 0]

    def weights_of(i, after):
        ssem, rsem, lands, _ = wg_passing[i]
        return split_wait(ssem, rsem, lands, after, weight_d2d_copies, f"wg_d2d_wait_{tags[i]}")

    NS = ada_w.shape[-1]
    ada_b_shard = lax.dynamic_slice_in_dim(ada_b, chip * NS, NS, axis=1).reshape(L, 1, NS)
    cast_piece(*pieces[0], small_all)
    first_token = start_piece(tags[0], small_all)
    for tag, ws in pieces[1:]:
        cast_piece(tag, ws, first_token)
    mod_part = ada_mod(c_all, ada_w, ada_b_shard, "ada_mod", dep=wg_started[tags[-1]][-1])
    mod_all = allgather_small(mod_part.reshape(L * N_DEV, NS), "ag_mod").reshape(N_DEV, L, N_DEV, NS)
    mod_mine = lax.dynamic_index_in_dim(mod_all[0::2], me, axis=2, keepdims=False)
    mod = jnp.moveaxis(mod_mine, 0, 1).reshape(L, NSUB, 3, D)
    wg_token = first_token[0, 0]
    for tag in tags[1:]:
        wg_token = wg_token + start_piece(tag, mod_all)[0, 0]

    def vec_of(l, s):
        return jnp.concatenate([mod[l, s], g_pre_full[l, s][None], g_post_full[l, s][None],
                                jnp.zeros((3, D), F32)], axis=0)

    GB = 8
    s5_args = (ssm_lam_re[0], ssm_lam_im[0], ssm_log_dt[0], ssm_b_re[0], ssm_b_im[0], ssm_c_re[0], ssm_c_im[0])
    (a_re, a_im, bblk_re, bblk_im, cblk_re, cblk_im), s5_vjp = jax.vjp(lambda *p: _s5_prepare(*p, GB), *s5_args)
    bre16, bim16 = bblk_re.astype(BF16), bblk_im.astype(BF16)
    cre16, cim16 = cblk_re.astype(BF16), cblk_im.astype(BF16)
    cre16_t, cim16_t = jnp.swapaxes(cre16, 1, 2), jnp.swapaxes(cim16, 1, 2)

    pool_w16 = pool_w[0].astype(BF16)
    sgu_bexp = jnp.broadcast_to(sgu_b[0][:, :, None], sgu_w[0].shape)

    saved = {}
    ffn_w = {}
    xcur = x0
    stage = [0]
    first_pass = weights_arrived(0, vec_of(0, 0) + wg_token)

    def next_weights(after, vec):
        i = stage[0]
        stage[0] += 1
        g = weights_of(i, after)
        if i + 1 < len(tags):
            vec = vec + weights_arrived(i + 1, after)
        return g, vec

    for l in range(L):
        v0 = vec_of(l, 0)
        g, v0p = next_weights(xcur if l else v0 + first_pass, v0)
        ffn_w[(l, 0)] = (g[0].reshape(N_CHIPS, D, FS), g[1].reshape(N_CHIPS, FR, D))
        out, h, u, f = ffn_fwd(xcur, v0p, *ffn_w[(l, 0)], f"ffn_fwd_{l}_0")
        saved[(l, 0)] = (xcur, v0, h, u, f)
        xcur = out
        v1 = vec_of(l, 1)
        mix_g, v1p = next_weights(xcur, v1)
        if l % 2 == 0:
            abin_g = mix_g[0].reshape(N_CHIPS, D, -1)
            about_g = mix_g[1].reshape(-1, D)
            h, z = pre_matmul(xcur, v1p, abin_g, f"mixa_in_{l}")
            ycat = mixa_core_fwd(z, pool_w16, pool_scale, sgu_ln_g, sgu_ln_b, sgu_w[0], sgu_bexp, f"mixa_core_{l}")
            out, f = matmul_post(ycat, about_g, xcur, v1, f"mixa_out_{l}")
            saved[(l, 1)] = (xcur, v1, h, z, ycat, f)
        else:
            sin_g = mix_g[0].reshape(1, -1, D)
            glu_g = mix_g[1].reshape(N_CHIPS, D, -1)
            h, uu = pre_matmul(xcur, v1p, sin_g, f"s5_in_{l}")
            us = _to_scan_order(uu)
            ys = s5_core_fwd(us, a_re, a_im, bre16, bim16, cre16_t, cim16_t, d_full, f"s5_core_{l}")
            yy = _from_scan_order(ys)
            out, f = s5_glu_fwd(yy, glu_g, xcur, v1, f"s5_glu_{l}")
            saved[(l, 1)] = (xcur, v1, h, us, yy, f)
        xcur = out
        v2 = vec_of(l, 2)
        g, v2p = next_weights(xcur, v2)
        ffn_w[(l, 1)] = (g[0].reshape(N_CHIPS, D, FS), g[1].reshape(N_CHIPS, FR, D))
        out, h, u, f = ffn_fwd(xcur, v2p, *ffn_w[(l, 1)], f"ffn_fwd_{l}_1")
        saved[(l, 2)] = (xcur, v2, h, u, f)
        xcur = out

    dcur, sq = loss_head(xcur, target, "loss_head")
    loss = lax.psum(sq[0, 0], ("x", "y", "c")) * (0.5 / D)

    accs = {}
    small_g = {}
    cidx = ci.reshape(1).astype(jnp.int32)
    chip_c = jnp.stack([chip, ci]).astype(jnp.int32)
    rs_open, rs_pending = [], []

    pin = [zero_after(loss.reshape(1, 1), "after_loss")[0, 0]]

    def pinned(v):
        return v + pin[0]

    def rs_begin(tag, items):
        arrs = [it[0] for it in items]
        lands = [lax.empty(a.shape[:-2] + (a.shape[-2] // 2, a.shape[-1]), BF16) for a in arrs]
        started = split_start(arrs + lands, pair_exchange_copies, len(arrs), f"rs_pair_start_{tag}")
        rs_open.append((tag, items, started))
        pin[0] = pin[0] + started[3][0, 0]

    def rs_advance(after):
        while rs_open:
            tag, items, (ssem, rsem, bufs, _) = rs_open.pop(0)
            n = len(items)
            bufs = split_wait(ssem, rsem, bufs, after, pair_exchange_copies, f"rs_pair_wait_{tag}")
            parts = [pair_sum(a, r, cidx, f"rs_pair_sum_{tag}_{i}") for i, (a, r) in enumerate(zip(bufs[:n], bufs[n:]))]
            lands = [lax.empty((3, 1) + p.shape[2:], BF16) for p in parts]
            started = split_start(parts + lands, chip_exchange_copies, 3 * n, f"rs_chip_start_{tag}")
            rs_pending.append((tag, items, started))
            pin[0] = pin[0] + started[3][0, 0]

    def ffn_back(l, s, k, dcur, before_wgrads=None):
        xin, vv, h, u, f = saved[(l, s)]
        vv = pinned(vv)
        dx, df, du, act, acc = ffn_bwd(dcur, xin, f, u, vv, *ffn_w[(l, k)], f"ffn_bwd_{l}_{k}")
        accs[(l, s)] = acc
        rs_advance(dx)
        dep = None if before_wgrads is None else before_wgrads()
        g_win = tn_matmul(h, du, 1024, f"ffn_dwin_{l}_{k}", dep=dep)
        g_wout = tn_matmul_cols(act, df, FS, D, f"ffn_dwout_{l}_{k}")
        rs_begin(f"ffn_{l}_{k}", [(g_win[None], "ffn_w_in", 2 * l + k, 2 * L),
                                  (g_wout.reshape(1, N_CHIPS, FR, D), "ffn_w_out", 2 * l + k, 2 * L)])
        return dx

    grads = {}

    small_names = ["pool_w", "pool_scale", "sgu_ln_g", "sgu_ln_b", "sgu_w", "sgu_b", "ssm_lam_re", "ssm_lam_im",
                   "ssm_b_re", "ssm_b_im", "ssm_c_re", "ssm_c_im", "ssm_log_dt"]
    small_open = []

    def small_grads_start():
        acc_all = jnp.stack([jnp.stack([accs[(l, s)] for s in range(NSUB)]) for l in range(L)])
        dmod_mine = acc_all[:, :, A_SHIFT:A_GATE + 1].reshape(L, NSUB * 3 * D)
        packed, offs = _pack_rows([dmod_mine, acc_all[:, :, A_GPRE], acc_all[:, :, A_GPOST], dd_mine]
                                  + [small_g[n] for n in small_names], D)
        packed = pinned(packed).astype(BF16)
        started = split_start([packed, lax.empty((N_DEV,) + packed.shape, BF16)], allgather_direct_copies,
                              N_DEV - 1, "ag_grads_start")
        small_open.append((started, offs))
        pin[0] = pin[0] + started[3][0, 0]
        return started[3]

    def small_grads_finish(after):
        (ssem, rsem, bufs, _), offs = small_open.pop()
        own, land = split_wait(ssem, rsem, bufs, after, allgather_direct_copies, "ag_grads_wait")
        me_arr = me.reshape(1).astype(jnp.int32)
        summed = _unpack_rows(sum_gathered(land, own, me_arr, "sum_small"), offs)
        grads.update({n: g for n, g in zip(small_names, summed[4:])})
        grads["ada_b"] = summed[0]
        grads["norm_pre"] = lax.dynamic_slice_in_dim(summed[1], chip * DS, DS, axis=2)
        grads["norm_post"] = lax.dynamic_slice_in_dim(summed[2], chip * DS, DS, axis=2)
        grads["ssm_d"] = lax.dynamic_slice_in_dim(summed[3], chip * DS, DS, axis=1)
        nm = L * NSUB * 3
        is_me = (jnp.arange(N_DEV) == me)[:, None, None]
        dmod_all = jnp.where(is_me, own[None, :nm], land[:, :nm]).reshape(N_DEV, L, NSUB * 3 * D)
        dmod_shard = lax.dynamic_slice_in_dim(jnp.moveaxis(dmod_all, 0, 1), chip * NS, NS, axis=2)
        grads["ada_w"] = ada_grad(c_all, dmod_shard, "ada_grad")

    for l in reversed(range(L)):
        dcur = ffn_back(l, 2, 1, dcur)
        if l % 2 == 0:
            xin, vv, h, z, ycat, f = saved[(l, 1)]
            vv = pinned(vv)
            df, dact, acc_post = post_bwd_matmul(dcur, f, vv, about_g, f"mixa_out_bwd_{l}")
            rs_advance(dact)
            g_about = tn_matmul_cols(ycat, df, 512, D, f"mixa_dwout_{l}")
            dz, dpw, dvecw, dsw, dsb = mixa_core_bwd(z, dact, pool_w16, pool_scale, sgu_ln_g, sgu_ln_b, sgu_w[0],
                                                     sgu_bexp, f"mixa_core_bwd_{l}")
            g_abin = tn_matmul_cols(h, dz, 512, abin_g.shape[-1], f"mixa_dwin_{l}")
            dcur, acc_pre = matmul_pre_bwd(dz, abin_g, xin, dcur, vv, f"mixa_in_bwd_{l}")
            accs[(l, 1)] = acc_pre + acc_post
            rs_begin(f"mixa_{l}", [(g_abin[None], "ab_w_in", 0, 1),
                                   (g_about.reshape(1, N_CHIPS, -1, D), "ab_w_out", 0, 1)])
            small_g.update(pool_w=dpw[None], pool_scale=dvecw[0:1], sgu_ln_g=dvecw[1:2], sgu_ln_b=dvecw[2:3],
                           sgu_w=dsw[None], sgu_b=dsb.T[None])
        else:
            xin, vv, h, us, yy, f = saved[(l, 1)]
            vv = pinned(vv)
            dy, dab, gact, acc_post = s5_glu_bwd(dcur, f, yy, vv, glu_g, f"s5_glu_bwd_{l}")
            rs_advance(dy)
            g_glu = tn_matmul_cols(gact, dab, 512, glu_g.shape[-1], f"s5_dwglu_{l}")
            dys = _to_scan_order(dy)
            dus, dbre, dbim, dcre_t, dcim_t, dar, dai, dd = s5_core_bwd(
                us, dys, a_re, a_im, bre16, bim16, cre16, cim16, d_full, f"s5_core_bwd_{l}")
            du = _from_scan_order(dus).astype(BF16)
            g_sin = tn_matmul_cols(h, du, 512, D, f"s5_dwin_{l}")
            dcur, acc_pre = matmul_pre_bwd(du, sin_g, xin, dcur, vv, f"s5_in_bwd_{l}")
            accs[(l, 1)] = acc_pre + acc_post
            rs_begin(f"s5_{l}", [(g_sin.reshape(1, N_CHIPS, -1, D), "ssm_w_in", 0, 1), (g_glu[None], "ssm_w_glu", 0, 1)])
            s5_grads = s5_vjp((dar, dai, dbre, dbim, jnp.swapaxes(dcre_t, 1, 2), jnp.swapaxes(dcim_t, 1, 2)))
            small_g.update(ssm_lam_re=s5_grads[0][None], ssm_lam_im=s5_grads[1][None], ssm_log_dt=s5_grads[2][None],
                           ssm_b_re=s5_grads[3][None], ssm_b_im=s5_grads[4][None], ssm_c_re=s5_grads[5][None],
                           ssm_c_im=s5_grads[6][None])
            dd_mine = dd
        dcur = ffn_back(l, 0, 0, dcur, before_wgrads=small_grads_start if l == 0 else None)
    grad_x = dcur[None]
    rs_advance(pin[0].reshape(1, 1))
    small_grads_finish(pin[0].reshape(1, 1))

    big_names = ["ffn_w_in", "ffn_w_out", "ab_w_in", "ab_w_out", "ssm_w_in", "ssm_w_glu"]
    deltas, new_m, new_v = {}, {}, {}

    def update(n):
        grads[n] = grads[n].reshape(weights[n].shape)
        deltas[n], new_m[n], new_v[n] = adamw_nd(weights[n], grads[n], m_in[n], v_in[n], f"adamw_{n}")

    for n in names:
        if n not in big_names:
            update(n)

    fin = {}
    after = pinned(deltas["ssm_log_dt"]) + deltas["ada_w"][0, 0, 0]
    for tag, items, (ssem, rsem, bufs, _) in rs_pending:
        bufs = split_wait(ssem, rsem, bufs, after, chip_exchange_copies, f"rs_chip_wait_{tag}")
        n = len(items)
        for i, (it, p, r) in enumerate(zip(items, bufs[:n], bufs[n:])):
            fin[it[1]] = chip_sum(p, r, chip_c, f"rs_chip_sum_{tag}_{i}", dest=fin.get(it[1]), slot=it[2],
                                  n_slots=it[3])
    filled = sibling_fill_halves([fin[n] for n in big_names], "rs_fill")
    for n, g in zip(big_names, filled):
        grads[n] = g
        update(n)

    return (loss, grad_x, *[grads[n] for n in names], *[deltas[n] for n in names],
            *[new_m[n] for n in names], *[new_v[n] for n in names])
```

```python
import math

import jax
import jax.numpy as jnp
from jax import lax
from jax.experimental import pallas as pl
from jax.experimental.pallas import tpu as pltpu

F32 = jnp.float32
BF16 = jnp.bfloat16
EPS = 1e-6
MESH_T = pl.DeviceIdType.MESH
VMEM_LIMIT_BYTES = 56 * 1024 * 1024
N_CHIPS = 4
N_DEV = 8
POOL_WINDOWS = (2, 4, 8, 16)
CHUNK = 128
SCAN_LANES = 8
SCAN_UNROLL = 4
ADAM_LR = 0.001
ADAM_B1 = 0.9
ADAM_B2 = 0.999
ADAM_EPS = 1e-08
ADAM_WD = 0.01
ADAM_STEP = 10
GELU_C = math.sqrt(2.0 / math.pi)
GELU_K = 0.044715

V_SHIFT, V_SCALE, V_GATE, V_GPRE, V_GPOST = 0, 1, 2, 3, 4
A_SHIFT, A_SCALE, A_GATE, A_GPRE, A_GPOST = 0, 1, 2, 3, 4

ANY = pl.BlockSpec(memory_space=pl.ANY)


def _cp(sem=None):
    if sem is None:
        return pltpu.CompilerParams(vmem_limit_bytes=VMEM_LIMIT_BYTES)
    return pltpu.CompilerParams(vmem_limit_bytes=VMEM_LIMIT_BYTES, dimension_semantics=sem)


def _dot(a, b):
    return jnp.dot(a, b, preferred_element_type=F32)


def _dot_nt(a, b):
    return lax.dot_general(a, b, (((1,), (1,)), ((), ())), preferred_element_type=F32)


def _dot_tn(a, b):
    return lax.dot_general(a, b, (((0,), (0,)), ((), ())), preferred_element_type=F32)


def _sigmoid(x):
    return 1.0 / (1.0 + jnp.exp(-x))


def _gelu(x):
    return 0.5 * x * (1.0 + jnp.tanh(GELU_C * (x + GELU_K * x * x * x)))


def _gelu_grad(x):
    t = jnp.tanh(GELU_C * (x + GELU_K * x * x * x))
    return 0.5 * (1.0 + t) + 0.5 * x * (1.0 - t * t) * GELU_C * (1.0 + 3.0 * GELU_K * x * x)


def _rowsum(v):
    return jnp.sum(v, axis=0, keepdims=True)


def _lanemean(v):
    return jnp.mean(v, axis=-1, keepdims=True)


def _row(ref, i):
    return ref[pl.ds(i, 1), :]


def _pre_fwd(x, vec_ref):
    r = lax.rsqrt(_lanemean(x * x) + EPS)
    return (x * r) * _row(vec_ref, V_GPRE) * (1.0 + _row(vec_ref, V_SCALE)) + _row(vec_ref, V_SHIFT)


def _pre_bwd(x, dh, vec_ref):
    g = _row(vec_ref, V_GPRE)
    sc = 1.0 + _row(vec_ref, V_SCALE)
    r = lax.rsqrt(_lanemean(x * x) + EPS)
    xn = x * r
    dhx = dh * xn
    t2 = dh * (g * sc)
    dx = r * (t2 - xn * _lanemean(t2 * xn))
    return dx, _rowsum(dh), _rowsum(dhx * g), _rowsum(dhx * sc)


def _post_fwd(x, f, vec_ref, rw):
    q = lax.rsqrt(_lanemean(f * f) + EPS)
    return x + (rw * _row(vec_ref, V_GATE)) * (f * q * _row(vec_ref, V_GPOST))


def _post_bwd(dout, f, vec_ref, rw):
    gp = _row(vec_ref, V_GPOST)
    gate = _row(vec_ref, V_GATE)
    q = lax.rsqrt(_lanemean(f * f) + EPS)
    fhat = f * q
    dgate = _rowsum(dout * (rw * fhat * gp))
    dy = dout * (rw * gate)
    dgpost = _rowsum(dy * fhat)
    t = dy * gp
    df = q * (t - fhat * _lanemean(t * fhat))
    return df, dgate, dgpost


def _acc_add(acc_ref, row, v):
    acc_ref[pl.ds(row, 1), :] += v


def ffn_fwd(x, vec, win_g, wout_g, name, tm=512):
    S, D = x.shape
    FS = win_g.shape[-1]
    FR = wout_g.shape[-2]

    def body(x_ref, vec_ref, win_hbm, wout_hbm, out_ref, h_ref, u_ref, f_ref, win_s, wout_s, sem):
        @pl.when(pl.program_id(0) == 0)
        def _():
            cps = [pltpu.make_async_copy(win_hbm.at[j], win_s.at[j], sem.at[j]) for j in range(N_CHIPS)]
            cps += [pltpu.make_async_copy(wout_hbm.at[j], wout_s.at[pl.ds(j * FR, FR), :], sem.at[N_CHIPS + j])
                    for j in range(N_CHIPS)]
            for cp in cps:
                cp.start()
            for cp in cps:
                cp.wait()

        xv = x_ref[...]
        h = _pre_fwd(xv, vec_ref).astype(BF16)
        h_ref[...] = h
        f = jnp.zeros((tm, D), F32)
        for j in range(2):
            a = _dot(h, win_s[j])
            b = _dot(h, win_s[2 + j])
            u_ref[j] = a.astype(BF16)
            u_ref[2 + j] = b.astype(BF16)
            act = (a * _sigmoid(a) * b).astype(BF16)
            f = f + _dot(act, wout_s[pl.ds(j * FS, FS), :])
        f_ref[...] = f
        out_ref[...] = _post_fwd(xv, f, vec_ref, 0.5)

    row = pl.BlockSpec((tm, D), lambda i: (i, 0))
    return pl.pallas_call(
        body, name=name, grid=(S // tm,),
        in_specs=[row, pl.BlockSpec((8, D), lambda i: (0, 0)), ANY, ANY],
        out_specs=[row, row, pl.BlockSpec((N_CHIPS, tm, FS), lambda i: (0, i, 0)), row],
        out_shape=[jax.ShapeDtypeStruct((S, D), F32), jax.ShapeDtypeStruct((S, D), BF16),
                   jax.ShapeDtypeStruct((N_CHIPS, S, FS), BF16), jax.ShapeDtypeStruct((S, D), F32)],
        scratch_shapes=[pltpu.VMEM((N_CHIPS, D, FS), BF16), pltpu.VMEM((N_CHIPS * FR, D), BF16),
                        pltpu.SemaphoreType.DMA((2 * N_CHIPS,))],
        compiler_params=_cp(("arbitrary",)),
    )(x, vec, win_g, wout_g)


def ffn_bwd(dout, x, f, u, vec, win_g, wout_g, name, tm=256):
    S, D = x.shape
    FS = win_g.shape[-1]
    FR = wout_g.shape[-2]

    def body(dout_ref, x_ref, f_ref, u_ref, vec_ref, win_hbm, wout_hbm,
             dx_ref, df_ref, du_ref, act_ref, acc_ref, win_s, wout_s, sem):
        @pl.when(pl.program_id(0) == 0)
        def _():
            cps = [pltpu.make_async_copy(win_hbm.at[j], win_s.at[j], sem.at[j]) for j in range(N_CHIPS)]
            cps += [pltpu.make_async_copy(wout_hbm.at[j], wout_s.at[pl.ds(j * FR, FR), :], sem.at[N_CHIPS + j])
                    for j in range(N_CHIPS)]
            for cp in cps:
                cp.start()
            acc_ref[...] = jnp.zeros_like(acc_ref)
            for cp in cps:
                cp.wait()

        dout_v = dout_ref[...]
        df, dgate, dgpost = _post_bwd(dout_v, f_ref[...], vec_ref, 0.5)
        dfb = df.astype(BF16)
        df_ref[...] = dfb
        dh = jnp.zeros((tm, D), F32)
        for j in range(2):
            a = u_ref[j].astype(F32)
            b = u_ref[2 + j].astype(F32)
            sig = _sigmoid(a)
            sl = a * sig
            dact = _dot_nt(dfb, wout_s[pl.ds(j * FS, FS), :])
            da = (dact * b * (sig * (1.0 + a * (1.0 - sig)))).astype(BF16)
            db = (dact * sl).astype(BF16)
            du_ref[j] = da
            du_ref[2 + j] = db
            act_ref[:, j * FS:(j + 1) * FS] = (sl * b).astype(BF16)
            dh = dh + _dot_nt(da, win_s[j]) + _dot_nt(db, win_s[2 + j])
        dx, dshift, dscale, dgpre = _pre_bwd(x_ref[...], dh, vec_ref)
        dx_ref[...] = dout_v + dx
        _acc_add(acc_ref, A_SHIFT, dshift)
        _acc_add(acc_ref, A_SCALE, dscale)
        _acc_add(acc_ref, A_GATE, dgate)
        _acc_add(acc_ref, A_GPRE, dgpre)
        _acc_add(acc_ref, A_GPOST, dgpost)

    row = pl.BlockSpec((tm, D), lambda i: (i, 0))
    ublk = pl.BlockSpec((N_CHIPS, tm, FS), lambda i: (0, i, 0))
    const = pl.BlockSpec((8, D), lambda i: (0, 0))
    return pl.pallas_call(
        body, name=name, grid=(S // tm,),
        in_specs=[row, row, row, ublk, const, ANY, ANY],
        out_specs=[row, row, ublk, pl.BlockSpec((tm, 2 * FS), lambda i: (i, 0)), const],
        out_shape=[jax.ShapeDtypeStruct((S, D), F32), jax.ShapeDtypeStruct((S, D), BF16),
                   jax.ShapeDtypeStruct((N_CHIPS, S, FS), BF16), jax.ShapeDtypeStruct((S, 2 * FS), BF16),
                   jax.ShapeDtypeStruct((8, D), F32)],
        scratch_shapes=[pltpu.VMEM((N_CHIPS, D, FS), BF16), pltpu.VMEM((N_CHIPS * FR, D), BF16),
                        pltpu.SemaphoreType.DMA((2 * N_CHIPS,))],
        compiler_params=_cp(("arbitrary",)),
    )(dout, x, f, u, vec, win_g, wout_g)


def tn_matmul(a, b, tk, name, dep=None):
    S, K = a.shape
    nb, _, tn = b.shape
    extra = [] if dep is None else [dep]

    def body(a_ref, b_ref, *rest):
        rest[-1][...] = _dot_tn(a_ref[...], b_ref[...]).astype(BF16)

    return pl.pallas_call(
        body, name=name, grid=(K // tk, nb),
        in_specs=[pl.BlockSpec((S, tk), lambda i, j: (0, i)), pl.BlockSpec((None, S, tn), lambda i, j: (j, 0, 0))]
        + [ANY] * len(extra),
        out_specs=pl.BlockSpec((None, tk, tn), lambda i, j: (j, i, 0)),
        out_shape=jax.ShapeDtypeStruct((nb, K, tn), BF16),
        compiler_params=_cp(("arbitrary", "arbitrary")),
    )(a, b, *extra)


def tn_matmul_cols(a, b, tk, tn, name):
    S, K = a.shape
    N = b.shape[1]

    def body(a_ref, b_ref, o_ref):
        o_ref[...] = _dot_tn(a_ref[...], b_ref[...]).astype(BF16)

    return pl.pallas_call(
        body, name=name, grid=(K // tk, N // tn),
        in_specs=[pl.BlockSpec((S, tk), lambda i, j: (0, i)), pl.BlockSpec((S, tn), lambda i, j: (0, j))],
        out_specs=pl.BlockSpec((None, tk, tn), lambda i, j: (j, i, 0)),
        out_shape=jax.ShapeDtypeStruct((N // tn, K, tn), BF16),
        compiler_params=_cp(("arbitrary", "arbitrary")),
    )(a, b)


def pre_matmul(x, vec, w3, name, tm=256):
    S, D = x.shape
    nj, _, Nj = w3.shape

    def body(x_ref, vec_ref, w_ref, h_ref, z_ref):
        h = _pre_fwd(x_ref[...], vec_ref).astype(BF16)
        h_ref[...] = h
        for j in range(nj):
            z_ref[:, j * Nj:(j + 1) * Nj] = _dot(h, w_ref[j])

    row = pl.BlockSpec((tm, D), lambda i: (i, 0))
    return pl.pallas_call(
        body, name=name, grid=(S // tm,),
        in_specs=[row, pl.BlockSpec((8, D), lambda i: (0, 0)), pl.BlockSpec((nj, D, Nj), lambda i: (0, 0, 0))],
        out_specs=[row, pl.BlockSpec((tm, nj * Nj), lambda i: (i, 0))],
        out_shape=[jax.ShapeDtypeStruct((S, D), BF16), jax.ShapeDtypeStruct((S, nj * Nj), F32)],
        compiler_params=_cp(("arbitrary",)),
    )(x, vec, w3)


def matmul_pre_bwd(dz, w3, x, dres, vec, name, tm=256):
    S, D = x.shape
    nj, _, Nj = w3.shape

    def body(dz_ref, w_ref, x_ref, dres_ref, vec_ref, dx_ref, acc_ref):
        @pl.when(pl.program_id(0) == 0)
        def _():
            acc_ref[...] = jnp.zeros_like(acc_ref)

        dh = jnp.zeros((tm, D), F32)
        for j in range(nj):
            dh = dh + _dot_nt(dz_ref[:, j * Nj:(j + 1) * Nj], w_ref[j])
        dx, dshift, dscale, dgpre = _pre_bwd(x_ref[...], dh, vec_ref)
        dx_ref[...] = dres_ref[...] + dx
        _acc_add(acc_ref, A_SHIFT, dshift)
        _acc_add(acc_ref, A_SCALE, dscale)
        _acc_add(acc_ref, A_GPRE, dgpre)

    row = pl.BlockSpec((tm, D), lambda i: (i, 0))
    const = pl.BlockSpec((8, D), lambda i: (0, 0))
    return pl.pallas_call(
        body, name=name, grid=(S // tm,),
        in_specs=[pl.BlockSpec((tm, nj * Nj), lambda i: (i, 0)), pl.BlockSpec((nj, D, Nj), lambda i: (0, 0, 0)),
                  row, row, const],
        out_specs=[row, const],
        out_shape=[jax.ShapeDtypeStruct((S, D), F32), jax.ShapeDtypeStruct((8, D), F32)],
        compiler_params=_cp(("arbitrary",)),
    )(dz, w3, x, dres, vec)


def matmul_post(act, w, x, vec, name, tm=256):
    S, D = x.shape
    K = act.shape[1]

    def body(act_ref, w_ref, x_ref, vec_ref, out_ref, f_ref):
        f = _dot(act_ref[...], w_ref[...])
        f_ref[...] = f
        out_ref[...] = _post_fwd(x_ref[...], f, vec_ref, 1.0)

    row = pl.BlockSpec((tm, D), lambda i: (i, 0))
    return pl.pallas_call(
        body, name=name, grid=(S // tm,),
        in_specs=[pl.BlockSpec((tm, K), lambda i: (i, 0)), pl.BlockSpec((K, D), lambda i: (0, 0)), row,
                  pl.BlockSpec((8, D), lambda i: (0, 0))],
        out_specs=[row, row],
        out_shape=[jax.ShapeDtypeStruct((S, D), F32), jax.ShapeDtypeStruct((S, D), F32)],
        compiler_params=_cp(("arbitrary",)),
    )(act, w, x, vec)


def post_bwd_matmul(dout, f, vec, w, name, tm=256):
    S, D = dout.shape
    K = w.shape[0]

    def body(dout_ref, f_ref, vec_ref, w_ref, df_ref, dact_ref, acc_ref):
        @pl.when(pl.program_id(0) == 0)
        def _():
            acc_ref[...] = jnp.zeros_like(acc_ref)

        df, dgate, dgpost = _post_bwd(dout_ref[...], f_ref[...], vec_ref, 1.0)
        dfb = df.astype(BF16)
        df_ref[...] = dfb
        dact_ref[...] = _dot_nt(dfb, w_ref[...])
        _acc_add(acc_ref, A_GATE, dgate)
        _acc_add(acc_ref, A_GPOST, dgpost)

    row = pl.BlockSpec((tm, D), lambda i: (i, 0))
    const = pl.BlockSpec((8, D), lambda i: (0, 0))
    return pl.pallas_call(
        body, name=name, grid=(S // tm,),
        in_specs=[row, row, const, pl.BlockSpec((K, D), lambda i: (0, 0))],
        out_specs=[row, pl.BlockSpec((tm, K), lambda i: (i, 0)), const],
        out_shape=[jax.ShapeDtypeStruct((S, D), BF16), jax.ShapeDtypeStruct((S, K), F32),
                   jax.ShapeDtypeStruct((8, D), F32)],
        compiler_params=_cp(("arbitrary",)),
    )(dout, f, vec, w)


def _band(w, transposed, prev):
    r = lax.broadcasted_iota(jnp.int32, (CHUNK, CHUNK), 1 if transposed else 0)
    c = lax.broadcasted_iota(jnp.int32, (CHUNK, CHUNK), 0 if transposed else 1)
    d = r - c
    m = (d + CHUNK < w) if prev else ((d >= 0) & (d < w))
    return jnp.where(m, 1.0, 0.0).astype(BF16)


def _split_hi_lo(a):
    hi = a.astype(BF16)
    lo = (a - hi.astype(F32)).astype(BF16)
    return hi, lo


def _pool_diff(a, ap, g, denom):
    w = POOL_WINDOWS[g]
    a_hi, a_lo = _split_hi_lo(a)
    p_hi, p_lo = _split_hi_lo(ap)
    mc = _band(w, False, False)
    mp = _band(w, False, True)
    win = _dot(mc, a_hi) + _dot(mc, a_lo) + _dot(mp, p_hi) + _dot(mp, p_lo)
    return win / denom - a


def _sgu_norm(v, lg, lb):
    mu = _lanemean(v)
    xc = v - mu
    rstd = lax.rsqrt(_lanemean(xc * xc) + EPS)
    xhat = xc * rstd
    return xhat, rstd, xhat * lg + lb


def _tril_mask():
    r = lax.broadcasted_iota(jnp.int32, (CHUNK, CHUNK), 0)
    c = lax.broadcasted_iota(jnp.int32, (CHUNK, CHUNK), 1)
    return r >= c


def _positions(i, w):
    r = lax.broadcasted_iota(jnp.int32, (CHUNK, 128), 0)
    pos = (i * CHUNK + r + 1).astype(F32)
    return jnp.minimum(pos, float(w))


def mixa_core_fwd(z, pool_w, pool_scale, ln_g, ln_b, sgu_w, sgu_bexp, name):
    S = z.shape[0]
    W = z.shape[1] // 3
    G = len(POOL_WINDOWS)
    GD = W // G

    def body(zc_ref, zp_ref, pw_ref, ps_ref, lg_ref, lb_ref, sw_ref, sb_ref, y_ref):
        i = pl.program_id(0)
        has_prev = jnp.where(i > 0, 1.0, 0.0)
        for g in range(G):
            sl = slice(g * GD, (g + 1) * GD)
            a = zc_ref[:, sl]
            ap = zp_ref[:, sl] * has_prev
            d = _pool_diff(a, ap, g, _positions(i, POOL_WINDOWS[g])).astype(BF16)
            y_ref[:, sl] = (_dot(d, pw_ref[g]) * ps_ref[:, sl]).astype(BF16)
        tril = _tril_mask()
        for hh in range(G):
            u = _gelu(zc_ref[:, W + hh * GD:W + (hh + 1) * GD])
            v = _gelu(zc_ref[:, 2 * W + hh * GD:2 * W + (hh + 1) * GD])
            sl = slice(hh * GD, (hh + 1) * GD)
            _, _, vn = _sgu_norm(v, lg_ref[:, sl], lb_ref[:, sl])
            wm = jnp.where(tril, sw_ref[hh], 0.0).astype(BF16)
            s = _dot(wm, vn.astype(BF16)) + sb_ref[hh]
            y_ref[:, W + hh * GD:W + (hh + 1) * GD] = (u * s).astype(BF16)

    vecw = pl.BlockSpec((1, W), lambda i: (0, 0))
    mats = pl.BlockSpec((G, GD, GD), lambda i: (0, 0, 0))
    return pl.pallas_call(
        body, name=name, grid=(S // CHUNK,),
        in_specs=[pl.BlockSpec((CHUNK, 3 * W), lambda i: (i, 0)),
                  pl.BlockSpec((CHUNK, W), lambda i: (jnp.maximum(i - 1, 0), 0)),
                  mats, vecw, vecw, vecw, mats, mats],
        out_specs=pl.BlockSpec((CHUNK, 2 * W), lambda i: (i, 0)),
        out_shape=jax.ShapeDtypeStruct((S, 2 * W), BF16),
        compiler_params=_cp(("arbitrary",)),
    )(z, z, pool_w, pool_scale, ln_g, ln_b, sgu_w, sgu_bexp)


def mixa_core_bwd(z, dy, pool_w, pool_scale, ln_g, ln_b, sgu_w, sgu_bexp, name):
    S = z.shape[0]
    W = z.shape[1] // 3
    G = len(POOL_WINDOWS)
    GD = W // G
    n_tiles = S // CHUNK

    def body(zc_ref, zp_ref, dyc_ref, dyn_ref, pw_ref, ps_ref, lg_ref, lb_ref, sw_ref, sb_ref,
             dz_ref, dpw_ref, dvec_ref, dsw_ref, dsb_ref):
        i = pl.program_id(0)

        @pl.when(i == 0)
        def _():
            dpw_ref[...] = jnp.zeros_like(dpw_ref)
            dvec_ref[...] = jnp.zeros_like(dvec_ref)
            dsw_ref[...] = jnp.zeros_like(dsw_ref)
            dsb_ref[...] = jnp.zeros_like(dsb_ref)

        has_prev = jnp.where(i > 0, 1.0, 0.0)
        has_next = jnp.where(i < n_tiles - 1, 1.0, 0.0)
        for g in range(G):
            w = POOL_WINDOWS[g]
            sl = slice(g * GD, (g + 1) * GD)
            a = zc_ref[:, sl]
            ap = zp_ref[:, sl] * has_prev
            den_c = _positions(i, w)
            den_n = _positions(i + 1, w)
            d = _pool_diff(a, ap, g, den_c).astype(BF16)
            ps = ps_ref[:, sl]
            pw = pw_ref[g]
            dyc = dyc_ref[:, sl]
            dvec_ref[pl.ds(0, 1), sl] += _rowsum(dyc * _dot(d, pw))
            dyp_c = (dyc * ps).astype(BF16)
            dyp_n = (dyn_ref[:, sl] * (ps * has_next)).astype(BF16)
            dpw_ref[g] += _dot_tn(d, dyp_c)
            dd_c = _dot_nt(dyp_c, pw)
            dd_n = _dot_nt(dyp_n, pw)
            da = (_dot(_band(w, True, False), (dd_c / den_c).astype(BF16))
                  + _dot(_band(w, True, True), (dd_n / den_n).astype(BF16)) - dd_c)
            dz_ref[:, sl] = da.astype(BF16)
        tril = _tril_mask()
        for hh in range(G):
            sl = slice(hh * GD, (hh + 1) * GD)
            zu = zc_ref[:, W + hh * GD:W + (hh + 1) * GD]
            zv = zc_ref[:, 2 * W + hh * GD:2 * W + (hh + 1) * GD]
            u = _gelu(zu)
            v = _gelu(zv)
            lg = lg_ref[:, sl]
            xhat, rstd, vn = _sgu_norm(v, lg, lb_ref[:, sl])
            vnb = vn.astype(BF16)
            wm = jnp.where(tril, sw_ref[hh], 0.0).astype(BF16)
            s = _dot(wm, vnb) + sb_ref[hh]
            dyb = dyc_ref[:, W + hh * GD:W + (hh + 1) * GD]
            du = dyb * s
            ds = dyb * u
            dsb_ref[:, hh:hh + 1] += jnp.sum(ds, axis=1, keepdims=True)
            dsb16 = ds.astype(BF16)
            dsw_ref[hh] += jnp.where(tril, _dot_nt(dsb16, vnb), 0.0)
            dvn = _dot_tn(wm, dsb16)
            dvec_ref[pl.ds(1, 1), sl] += _rowsum(dvn * xhat)
            dvec_ref[pl.ds(2, 1), sl] += _rowsum(dvn)
            dxh = dvn * lg
            dv = rstd * (dxh - _lanemean(dxh) - xhat * _lanemean(dxh * xhat))
            dz_ref[:, W + hh * GD:W + (hh + 1) * GD] = (du * _gelu_grad(zu)).astype(BF16)
            dz_ref[:, 2 * W + hh * GD:2 * W + (hh + 1) * GD] = (dv * _gelu_grad(zv)).astype(BF16)

    vecw = pl.BlockSpec((1, W), lambda i: (0, 0))
    mats = pl.BlockSpec((G, GD, GD), lambda i: (0, 0, 0))
    return pl.pallas_call(
        body, name=name, grid=(n_tiles,),
        in_specs=[pl.BlockSpec((CHUNK, 3 * W), lambda i: (i, 0)),
                  pl.BlockSpec((CHUNK, W), lambda i: (jnp.maximum(i - 1, 0), 0)),
                  pl.BlockSpec((CHUNK, 2 * W), lambda i: (i, 0)),
                  pl.BlockSpec((CHUNK, W), lambda i: (jnp.minimum(i + 1, n_tiles - 1), 0)),
                  mats, vecw, vecw, vecw, mats, mats],
        out_specs=[pl.BlockSpec((CHUNK, 3 * W), lambda i: (i, 0)), mats,
                   pl.BlockSpec((8, W), lambda i: (0, 0)), mats, pl.BlockSpec((CHUNK, G), lambda i: (0, 0))],
        out_shape=[jax.ShapeDtypeStruct((S, 3 * W), BF16), jax.ShapeDtypeStruct((G, GD, GD), F32),
                   jax.ShapeDtypeStruct((8, W), F32), jax.ShapeDtypeStruct((G, CHUNK, CHUNK), F32),
                   jax.ShapeDtypeStruct((CHUNK, G), F32)],
        compiler_params=_cp(("arbitrary",)),
    )(z, z, dy, dy, pool_w, pool_scale, ln_g, ln_b, sgu_w, sgu_bexp)


def _cmul(ar, ai, br, bi):
    return ar * br - ai * bi, ar * bi + ai * br


def _cpow(ar, ai, n):
    rr, ri = None, None
    br, bi = ar, ai
    while n:
        if n & 1:
            rr, ri = (br, bi) if rr is None else _cmul(rr, ri, br, bi)
        n >>= 1
        if n:
            br, bi = _cmul(br, bi, br, bi)
    return rr, ri


def _seg_rows(k):
    return pl.ds(pl.multiple_of(k * SCAN_LANES, SCAN_LANES), SCAN_LANES)


def _scan_fwd(xre, xim, carry, ar, ai, K):
    P = xre.shape[1]
    a8r = jnp.broadcast_to(ar, (SCAN_LANES, P))
    a8i = jnp.broadcast_to(ai, (SCAN_LANES, P))

    def local(k, c):
        pr, pi = c
        rows = _seg_rows(k)
        nr = a8r * pr - a8i * pi + xre[rows, :]
        ni = a8r * pi + a8i * pr + xim[rows, :]
        xre[rows, :] = nr
        xim[rows, :] = ni
        return nr, ni

    er, ei = lax.fori_loop(1, K, local, (xre[pl.ds(0, SCAN_LANES), :], xim[pl.ds(0, SCAN_LANES), :]),
                           unroll=SCAN_UNROLL)
    akr, aki = _cpow(ar, ai, K)
    cr = jnp.zeros((1, P), F32)
    ci = jnp.zeros((1, P), F32)
    carry[pl.ds(0, 1), :] = cr
    carry[pl.ds(SCAN_LANES, 1), :] = ci
    for j in range(1, SCAN_LANES):
        tr, ti = _cmul(akr, aki, cr, ci)
        cr = er[j - 1:j, :] + tr
        ci = ei[j - 1:j, :] + ti
        carry[pl.ds(j, 1), :] = cr
        carry[pl.ds(SCAN_LANES + j, 1), :] = ci
    cmr = carry[pl.ds(0, SCAN_LANES), :]
    cmi = carry[pl.ds(SCAN_LANES, SCAN_LANES), :]

    def fix(k, c):
        pr, pi = c
        rows = _seg_rows(k)
        tr, ti = _cmul(pr, pi, cmr, cmi)
        xre[rows, :] += tr
        xim[rows, :] += ti
        return _cmul(pr, pi, a8r, a8i)

    lax.fori_loop(0, K, fix, (a8r, a8i), unroll=SCAN_UNROLL)


def s5_core_fwd(u, ar, ai, bre, bim, cre_t, cim_t, dskip, name):
    S, D = u.shape
    nblk, UB, PB = bre.shape
    K = S // SCAN_LANES

    def body(u_ref, ar_ref, ai_ref, bre_ref, bim_ref, cre_ref, cim_ref, d_ref, y_ref, xre, xim, carry):
        uv = u_ref[...]
        ub = uv.astype(BF16)
        xre[...] = _dot(ub, bre_ref[...])
        xim[...] = _dot(ub, bim_ref[...])
        _scan_fwd(xre, xim, carry, ar_ref[...], ai_ref[...], K)
        y_ref[...] = (_dot(xre[...].astype(BF16), cre_ref[...]) - _dot(xim[...].astype(BF16), cim_ref[...])
                      + d_ref[...] * uv)

    ucol = pl.BlockSpec((S, UB), lambda i: (0, i))
    pvec = pl.BlockSpec((None, 1, PB), lambda i: (i, 0, 0))
    bmat = pl.BlockSpec((None, UB, PB), lambda i: (i, 0, 0))
    cmat = pl.BlockSpec((None, PB, UB), lambda i: (i, 0, 0))
    return pl.pallas_call(
        body, name=name, grid=(nblk,),
        in_specs=[ucol, pvec, pvec, bmat, bmat, cmat, cmat, pl.BlockSpec((1, UB), lambda i: (0, i))],
        out_specs=ucol,
        out_shape=jax.ShapeDtypeStruct((S, D), F32),
        scratch_shapes=[pltpu.VMEM((S, PB), F32), pltpu.VMEM((S, PB), F32), pltpu.VMEM((2 * SCAN_LANES, PB), F32)],
        compiler_params=_cp(("arbitrary",)),
    )(u, ar, ai, bre, bim, cre_t, cim_t, dskip)


def s5_core_bwd(u, dy, ar, ai, bre, bim, cre, cim, dskip, name):
    S, D = u.shape
    nblk, UB, PB = bre.shape
    K = S // SCAN_LANES

    def body(u_ref, dy_ref, ar_ref, ai_ref, bre_ref, bim_ref, cre_ref, cim_ref, d_ref,
             du_ref, dbre_ref, dbim_ref, dcre_ref, dcim_ref, dar_ref, dai_ref, dd_ref,
             xre, xim, gre, gim, carry, carry_b):
        ar = ar_ref[...]
        ai = ai_ref[...]
        uv = u_ref[...]
        ub = uv.astype(BF16)
        dyv = dy_ref[...]
        dyb = dyv.astype(BF16)
        xre[...] = _dot(ub, bre_ref[...])
        xim[...] = _dot(ub, bim_ref[...])
        _scan_fwd(xre, xim, carry, ar, ai, K)
        dcre_ref[...] = _dot_tn(xre[...].astype(BF16), dyb)
        dcim_ref[...] = -_dot_tn(xim[...].astype(BF16), dyb)
        gre[...] = _dot(dyb, cre_ref[...])
        gim[...] = -_dot(dyb, cim_ref[...])

        a8r = jnp.broadcast_to(ar, (SCAN_LANES, PB))
        a8i = jnp.broadcast_to(ai, (SCAN_LANES, PB))
        na8i = -a8i

        def local(s, c):
            k = K - 2 - s
            nr, ni = c
            rows = _seg_rows(k)
            tr = gre[rows, :] + a8r * nr + a8i * ni
            ti = gim[rows, :] + a8r * ni - a8i * nr
            gre[rows, :] = tr
            gim[rows, :] = ti
            return tr, ti

        last = _seg_rows(K - 1)
        fr, fi = lax.fori_loop(0, K - 1, local, (gre[last, :], gim[last, :]), unroll=SCAN_UNROLL)
        akr, aki = _cpow(ar, -ai, K)
        cr = jnp.zeros((1, PB), F32)
        ci = jnp.zeros((1, PB), F32)
        carry_b[pl.ds(SCAN_LANES - 1, 1), :] = cr
        carry_b[pl.ds(2 * SCAN_LANES - 1, 1), :] = ci
        for j in range(SCAN_LANES - 2, -1, -1):
            tr, ti = _cmul(akr, aki, cr, ci)
            cr = fr[j + 1:j + 2, :] + tr
            ci = fi[j + 1:j + 2, :] + ti
            carry_b[pl.ds(j, 1), :] = cr
            carry_b[pl.ds(SCAN_LANES + j, 1), :] = ci
        cbr = carry_b[pl.ds(0, SCAN_LANES), :]
        cbi = carry_b[pl.ds(SCAN_LANES, SCAN_LANES), :]

        def fix_rows(rows, c, xr, xi):
            pr, pi, dar, dai = c
            tr, ti = _cmul(pr, pi, cbr, cbi)
            g_r = gre[rows, :] + tr
            g_i = gim[rows, :] + ti
            gre[rows, :] = g_r
            gim[rows, :] = g_i
            dar = dar + g_r * xr + g_i * xi
            dai = dai + g_i * xr - g_r * xi
            nr, ni = _cmul(pr, pi, a8r, na8i)
            return nr, ni, dar, dai

        def fix(s, c):
            k = K - 1 - s
            prev = _seg_rows(k - 1)
            return fix_rows(_seg_rows(k), c, xre[prev, :], xim[prev, :])

        z8 = jnp.zeros((SCAN_LANES, PB), F32)
        c = lax.fori_loop(0, K - 1, fix, (a8r, na8i, z8, z8), unroll=SCAN_UNROLL)
        _, _, dar, dai = fix_rows(_seg_rows(0), c, carry[pl.ds(0, SCAN_LANES), :], carry[pl.ds(SCAN_LANES, SCAN_LANES), :])
        dar_ref[...] = _rowsum(dar)
        dai_ref[...] = _rowsum(dai)
        grb = gre[...].astype(BF16)
        gib = gim[...].astype(BF16)
        dbre_ref[...] = _dot_tn(ub, grb)
        dbim_ref[...] = _dot_tn(ub, gib)
        du_ref[...] = _dot_nt(grb, bre_ref[...]) + _dot_nt(gib, bim_ref[...]) + d_ref[...] * dyv
        dd_ref[...] = _rowsum(dyv * uv)

    ucol = pl.BlockSpec((S, UB), lambda i: (0, i))
    pvec = pl.BlockSpec((None, 1, PB), lambda i: (i, 0, 0))
    bmat = pl.BlockSpec((None, UB, PB), lambda i: (i, 0, 0))
    cmat = pl.BlockSpec((None, PB, UB), lambda i: (i, 0, 0))
    dvec = pl.BlockSpec((1, UB), lambda i: (0, i))
    return pl.pallas_call(
        body, name=name, grid=(nblk,),
        in_specs=[ucol, ucol, pvec, pvec, bmat, bmat, bmat, bmat, dvec],
        out_specs=[ucol, bmat, bmat, cmat, cmat, pvec, pvec, dvec],
        out_shape=[jax.ShapeDtypeStruct((S, D), F32),
                   jax.ShapeDtypeStruct((nblk, UB, PB), F32), jax.ShapeDtypeStruct((nblk, UB, PB), F32),
                   jax.ShapeDtypeStruct((nblk, PB, UB), F32), jax.ShapeDtypeStruct((nblk, PB, UB), F32),
                   jax.ShapeDtypeStruct((nblk, 1, PB), F32), jax.ShapeDtypeStruct((nblk, 1, PB), F32),
                   jax.ShapeDtypeStruct((1, D), F32)],
        scratch_shapes=[pltpu.VMEM((S, PB), F32), pltpu.VMEM((S, PB), F32), pltpu.VMEM((S, PB), F32),
                        pltpu.VMEM((S, PB), F32), pltpu.VMEM((2 * SCAN_LANES, PB), F32),
                        pltpu.VMEM((2 * SCAN_LANES, PB), F32)],
        compiler_params=_cp(("arbitrary",)),
    )(u, dy, ar, ai, bre, bim, cre, cim, dskip)


def s5_glu_fwd(y, wglu, x, vec, name, tm=256):
    S, D = x.shape
    NJ = wglu.shape[-1]

    def body(y_ref, w_ref, x_ref, vec_ref, out_ref, f_ref):
        g = _gelu(y_ref[...]).astype(BF16)
        f = jnp.concatenate([_dot(g, w_ref[j]) * _sigmoid(_dot(g, w_ref[2 + j])) for j in range(2)], axis=1)
        f_ref[...] = f
        out_ref[...] = _post_fwd(x_ref[...], f, vec_ref, 1.0)

    row = pl.BlockSpec((tm, D), lambda i: (i, 0))
    return pl.pallas_call(
        body, name=name, grid=(S // tm,),
        in_specs=[row, pl.BlockSpec((N_CHIPS, D, NJ), lambda i: (0, 0, 0)), row, pl.BlockSpec((8, D), lambda i: (0, 0))],
        out_specs=[row, row],
        out_shape=[jax.ShapeDtypeStruct((S, D), F32), jax.ShapeDtypeStruct((S, D), F32)],
        compiler_params=_cp(("arbitrary",)),
    )(y, wglu, x, vec)


def s5_glu_bwd(dout, f, y, vec, wglu, name, tm=256):
    S, D = dout.shape
    NJ = wglu.shape[-1]

    def body(dout_ref, f_ref, y_ref, vec_ref, w_ref, dy_ref, dab_ref, g_ref, acc_ref):
        @pl.when(pl.program_id(0) == 0)
        def _():
            acc_ref[...] = jnp.zeros_like(acc_ref)

        df, dgate, dgpost = _post_bwd(dout_ref[...], f_ref[...], vec_ref, 1.0)
        yv = y_ref[...]
        g = _gelu(yv).astype(BF16)
        g_ref[...] = g
        dg = jnp.zeros((tm, D), F32)
        for j in range(2):
            a = _dot(g, w_ref[j])
            sig = _sigmoid(_dot(g, w_ref[2 + j]))
            dfj = df[:, j * NJ:(j + 1) * NJ]
            da = (dfj * sig).astype(BF16)
            db = (dfj * a * sig * (1.0 - sig)).astype(BF16)
            dab_ref[:, j * NJ:(j + 1) * NJ] = da
            dab_ref[:, (2 + j) * NJ:(3 + j) * NJ] = db
            dg = dg + _dot_nt(da, w_ref[j]) + _dot_nt(db, w_ref[2 + j])
        dy_ref[...] = dg * _gelu_grad(yv)
        _acc_add(acc_ref, A_GATE, dgate)
        _acc_add(acc_ref, A_GPOST, dgpost)

    row = pl.BlockSpec((tm, D), lambda i: (i, 0))
    const = pl.BlockSpec((8, D), lambda i: (0, 0))
    return pl.pallas_call(
        body, name=name, grid=(S // tm,),
        in_specs=[row, row, row, const, pl.BlockSpec((N_CHIPS, D, NJ), lambda i: (0, 0, 0))],
        out_specs=[row, pl.BlockSpec((tm, N_CHIPS * NJ), lambda i: (i, 0)), row, const],
        out_shape=[jax.ShapeDtypeStruct((S, D), F32), jax.ShapeDtypeStruct((S, N_CHIPS * NJ), BF16),
                   jax.ShapeDtypeStruct((S, D), BF16), jax.ShapeDtypeStruct((8, D), F32)],
        compiler_params=_cp(("arbitrary",)),
    )(dout, f, y, vec, wglu)


def loss_head(y, target, name, tm=256):
    S, D = y.shape

    def body(y_ref, t_ref, d_ref, l_ref):
        @pl.when(pl.program_id(0) == 0)
        def _():
            l_ref[...] = jnp.zeros_like(l_ref)

        err = y_ref[...] - t_ref[...]
        d_ref[...] = err * (1.0 / D)
        l_ref[...] += jnp.sum(_rowsum(err * err), axis=1, keepdims=True)

    row = pl.BlockSpec((tm, D), lambda i: (i, 0))
    return pl.pallas_call(
        body, name=name, grid=(S // tm,),
        in_specs=[row, row],
        out_specs=[row, pl.BlockSpec((1, 1), lambda i: (0, 0))],
        out_shape=[jax.ShapeDtypeStruct((S, D), F32), jax.ShapeDtypeStruct((1, 1), F32)],
        compiler_params=_cp(("arbitrary",)),
    )(y, target)


def zero_after(dep, name):
    def body(dep_ref, o_ref):
        o_ref[...] = jnp.zeros_like(o_ref)

    return pl.pallas_call(body, name=name, in_specs=[ANY], out_specs=pl.BlockSpec(memory_space=pltpu.VMEM),
                          out_shape=jax.ShapeDtypeStruct((8, 128), F32), compiler_params=_cp())(dep)


def ada_mod(c_all, ada_w, ada_b_shard, name, tn=768, dep=None):
    B, D = c_all.shape
    L, _, NS = ada_w.shape
    extra = [] if dep is None else [dep]

    def body(c_ref, w_ref, b_ref, *rest):
        cv = c_ref[...]
        cond = (cv * _sigmoid(cv)).astype(BF16)
        rest[-1][...] = _dot(cond, w_ref[...].astype(BF16)) + b_ref[...]

    return pl.pallas_call(
        body, name=name, grid=(L, NS // tn),
        in_specs=[pl.BlockSpec((B, D), lambda l, j: (0, 0)), pl.BlockSpec((None, D, tn), lambda l, j: (l, 0, j)),
                  pl.BlockSpec((None, 1, tn), lambda l, j: (l, 0, j))] + [ANY] * len(extra),
        out_specs=pl.BlockSpec((None, B, tn), lambda l, j: (l, 0, j)),
        out_shape=jax.ShapeDtypeStruct((L, B, NS), F32),
        compiler_params=_cp(("arbitrary", "arbitrary")),
    )(c_all, ada_w, ada_b_shard, *extra)


def ada_grad(c_all, dmod, name, tn=768):
    B, D = c_all.shape
    L, _, NS = dmod.shape

    def body(c_ref, d_ref, o_ref):
        cv = c_ref[...]
        cond = (cv * _sigmoid(cv)).astype(BF16)
        o_ref[...] = _dot_tn(cond, d_ref[...].astype(BF16))

    return pl.pallas_call(
        body, name=name, grid=(L, NS // tn),
        in_specs=[pl.BlockSpec((B, D), lambda l, j: (0, 0)), pl.BlockSpec((None, B, tn), lambda l, j: (l, 0, j))],
        out_specs=pl.BlockSpec((None, D, tn), lambda l, j: (l, 0, j)),
        out_shape=jax.ShapeDtypeStruct((L, D, NS), F32),
        compiler_params=_cp(("arbitrary", "arbitrary")),
    )(c_all, dmod)


def sum_leading(x, name):
    n, R, C = x.shape

    def body(x_ref, o_ref):
        acc = x_ref[0].astype(F32)
        for i in range(1, n):
            acc = acc + x_ref[i].astype(F32)
        o_ref[...] = acc

    tr = _row_tile(R, 64)
    return pl.pallas_call(
        body, name=name, grid=(R // tr,),
        in_specs=[pl.BlockSpec((n, tr, C), lambda i: (0, i, 0))],
        out_specs=pl.BlockSpec((tr, C), lambda i: (i, 0)),
        out_shape=jax.ShapeDtypeStruct((R, C), F32),
        compiler_params=_cp(("arbitrary",)),
    )(x)


def sum_gathered(land, own, me_arr, name):
    n, R, C = land.shape

    def body(me_ref, land_ref, own_ref, o_ref):
        acc = jnp.zeros(o_ref.shape, F32)
        for i in range(n):
            acc = acc + jnp.where(me_ref[0] == i, own_ref[...], land_ref[i]).astype(F32)
        o_ref[...] = acc

    tr = _row_tile(R, 64)
    grid_spec = pltpu.PrefetchScalarGridSpec(
        num_scalar_prefetch=1, grid=(R // tr,),
        in_specs=[pl.BlockSpec((n, tr, C), lambda i, me: (0, i, 0)), pl.BlockSpec((tr, C), lambda i, me: (i, 0))],
        out_specs=pl.BlockSpec((tr, C), lambda i, me: (i, 0)))
    return pl.pallas_call(
        body, name=name, grid_spec=grid_spec,
        out_shape=jax.ShapeDtypeStruct((R, C), F32),
        compiler_params=_cp(("arbitrary",)),
    )(me_arr, land, own)


def _row_tile(R, cap=512):
    if R <= cap:
        return R
    for cand in (512, 384, 352, 256, 128, 64, 32, 16, 8):
        if cand <= cap and R % cand == 0:
            return cand
    return R


def adamw(w, g, m, v, name):
    R, C = w.shape
    tr = _row_tile(R, 256 if C > 1024 else 512)
    bc1 = 1.0 - ADAM_B1 ** ADAM_STEP
    bc2 = 1.0 - ADAM_B2 ** ADAM_STEP

    def body(w_ref, g_ref, m_ref, v_ref, d_ref, nm_ref, nv_ref):
        gv = g_ref[...]
        nm = ADAM_B1 * m_ref[...] + (1.0 - ADAM_B1) * gv
        nv = ADAM_B2 * v_ref[...] + (1.0 - ADAM_B2) * (gv * gv)
        nm_ref[...] = nm
        nv_ref[...] = nv
        d_ref[...] = -ADAM_LR * ((nm / bc1) / (jnp.sqrt(nv / bc2) + ADAM_EPS) + ADAM_WD * w_ref[...])

    blk = pl.BlockSpec((tr, C), lambda i: (i, 0))
    sd = jax.ShapeDtypeStruct((R, C), F32)
    return pl.pallas_call(
        body, name=name, grid=(R // tr,),
        in_specs=[blk, blk, blk, blk], out_specs=[blk, blk, blk], out_shape=[sd, sd, sd],
        compiler_params=_cp(("arbitrary",)),
    )(w, g, m, v)


def _as2d(a):
    if a.ndim == 1:
        return a.reshape(1, -1)
    return a.reshape(-1, a.shape[-1])


SMALL_PARAM_ELEMS = 1 << 18


def adamw_nd(w, g, m, v, name):
    if w.size <= SMALL_PARAM_ELEMS:
        bc1 = 1.0 - ADAM_B1 ** ADAM_STEP
        bc2 = 1.0 - ADAM_B2 ** ADAM_STEP

        def body(w_ref, g_ref, m_ref, v_ref, d_ref, nm_ref, nv_ref):
            gv = g_ref[...]
            nm = ADAM_B1 * m_ref[...] + (1.0 - ADAM_B1) * gv
            nv = ADAM_B2 * v_ref[...] + (1.0 - ADAM_B2) * (gv * gv)
            nm_ref[...] = nm
            nv_ref[...] = nv
            d_ref[...] = -ADAM_LR * ((nm / bc1) / (jnp.sqrt(nv / bc2) + ADAM_EPS) + ADAM_WD * w_ref[...])

        sd = jax.ShapeDtypeStruct(w.shape, F32)
        whole = pl.BlockSpec(memory_space=pltpu.VMEM)
        return tuple(pl.pallas_call(body, name=name, in_specs=[whole] * 4, out_specs=[whole] * 3,
                                    out_shape=[sd, sd, sd], compiler_params=_cp())(w, g.reshape(w.shape), m, v))
    outs = adamw(_as2d(w), _as2d(g.reshape(w.shape)), _as2d(m), _as2d(v), name)
    return tuple(o.reshape(w.shape) for o in outs)


def _place():
    x, y, c = lax.axis_index("x"), lax.axis_index("y"), lax.axis_index("c")
    chips = [(1 - x, y), (x, 1 - y), (1 - x, 1 - y)]
    return x, y, c, chips


def allgather_small(xs, name):
    m_per, n = xs.shape

    def body(x_ref, out_ref, send_sems, recv_sems, local_sem):
        x, y, c, chips = _place()
        me, sibling = (x, y, c), (x, y, 1 - c)

        def rows(px, py, pc):
            return out_ref.at[pl.ds((4 * px + 2 * py + pc) * m_per, m_per), :]

        def copy(k, block, to, src=None):
            return pltpu.make_async_remote_copy(
                src_ref=rows(*block) if src is None else src, dst_ref=rows(*block),
                send_sem=send_sems.at[k], recv_sem=recv_sems.at[k], device_id=to, device_id_type=MESH_T)

        mine = pltpu.make_async_copy(x_ref, rows(*me), local_sem)
        mine.start()
        first = [copy(0, me, sibling, src=x_ref)]
        first += [copy(1 + j, me, (*chip, c), src=x_ref) for j, chip in enumerate(chips)]
        for cp in first:
            cp.start()
        passed = [copy(4 + j, (*chip, c), sibling) for j, chip in enumerate(chips)]
        for j, chip in enumerate(chips):
            copy(1 + j, (*chip, c), me).wait_recv()
            passed[j].start()
        copy(0, sibling, me).wait_recv()
        for j, chip in enumerate(chips):
            copy(4 + j, (*chip, 1 - c), me).wait_recv()
        for cp in first + passed:
            cp.wait_send()
        mine.wait()

    return pl.pallas_call(
        body, name=name,
        out_shape=jax.ShapeDtypeStruct((N_DEV * m_per, n), xs.dtype),
        in_specs=[pl.BlockSpec(memory_space=pltpu.VMEM)],
        out_specs=pl.BlockSpec(memory_space=pltpu.VMEM),
        scratch_shapes=[pltpu.SemaphoreType.DMA((7,)), pltpu.SemaphoreType.DMA((7,)), pltpu.SemaphoreType.DMA],
        compiler_params=_cp(),
    )(xs)


def gather_weights(shards, name):
    n = len(shards)

    def body(*refs):
        ins, outs = refs[:n], refs[n:2 * n]
        send_sems, recv_sems, fsend_sems, frecv_sems, local_sems = refs[2 * n:]
        x, y, c, chips = _place()
        me_chip = 2 * x + y
        sibling = (x, y, 1 - c)
        locs = []
        for a in range(n):
            cp = pltpu.make_async_copy(ins[a], outs[a].at[me_chip], local_sems.at[a])
            cp.start()
            locs.append(cp)

        def ici(a, k, src_chip, to):
            return pltpu.make_async_remote_copy(
                src_ref=ins[a].at[c], dst_ref=outs[a].at[src_chip, c],
                send_sem=send_sems.at[a * 3 + k], recv_sem=recv_sems.at[a * 3 + k], device_id=to, device_id_type=MESH_T)

        def d2d(a, k, src_chip, half):
            return pltpu.make_async_remote_copy(
                src_ref=outs[a].at[src_chip, half], dst_ref=outs[a].at[src_chip, half],
                send_sem=fsend_sems.at[a * 3 + k], recv_sem=frecv_sems.at[a * 3 + k],
                device_id=sibling, device_id_type=MESH_T)

        firsts = []
        for k, (cx, cy) in enumerate(chips):
            for a in range(n):
                cp = ici(a, k, me_chip, (cx, cy, c))
                cp.start()
                firsts.append(cp)
        passed = []
        for k, (cx, cy) in enumerate(chips):
            for a in range(n):
                ici(a, k, 2 * cx + cy, (cx, cy, c)).wait_recv()
                cp = d2d(a, k, 2 * cx + cy, c)
                cp.start()
                passed.append(cp)
        for k, (cx, cy) in enumerate(chips):
            for a in range(n):
                d2d(a, k, 2 * cx + cy, 1 - c).wait_recv()
        for cp in firsts + passed:
            cp.wait_send()
        for cp in locs:
            cp.wait()

    return pl.pallas_call(
        body, name=name,
        out_shape=[jax.ShapeDtypeStruct((N_CHIPS,) + s.shape, s.dtype) for s in shards],
        in_specs=[ANY] * n, out_specs=[ANY] * n,
        scratch_shapes=[pltpu.SemaphoreType.DMA((3 * n,)), pltpu.SemaphoreType.DMA((3 * n,)),
                        pltpu.SemaphoreType.DMA((3 * n,)), pltpu.SemaphoreType.DMA((3 * n,)),
                        pltpu.SemaphoreType.DMA((n,))],
        compiler_params=_cp(),
    )(*shards)


def _half_rows(ref, half, rh):
    idx = (slice(None),) * (len(ref.shape) - 2) + (pl.ds(pl.multiple_of(half * rh, 16), rh), slice(None))
    return ref.at[idx]


def sibling_exchange_halves(grads, name):
    n = len(grads)

    def body(*refs):
        ins, outs = refs[:n], refs[n:2 * n]
        send_sems, recv_sems = refs[2 * n:]
        x, y, c, _ = _place()
        cps = []
        for a in range(n):
            rh = ins[a].shape[-2] // 2
            cp = pltpu.make_async_remote_copy(
                src_ref=_half_rows(ins[a], 1 - c, rh), dst_ref=outs[a],
                send_sem=send_sems.at[a], recv_sem=recv_sems.at[a], device_id=(x, y, 1 - c), device_id_type=MESH_T)
            cp.start()
            cps.append(cp)
        for cp in cps:
            cp.wait()

    return pl.pallas_call(
        body, name=name,
        out_shape=[jax.ShapeDtypeStruct(g.shape[:-2] + (g.shape[-2] // 2, g.shape[-1]), g.dtype) for g in grads],
        in_specs=[ANY] * n, out_specs=[ANY] * n,
        scratch_shapes=[pltpu.SemaphoreType.DMA((n,)), pltpu.SemaphoreType.DMA((n,))],
        compiler_params=_cp(),
    )(*grads)


def pair_sum(g, recv, cidx, name):
    L, NS, R, C = g.shape
    rh = R // 2
    tr = _row_tile(rh, 256 if C > 1024 else 512)
    nt = rh // tr

    def body(c_ref, g_ref, r_ref, o_ref):
        o_ref[...] = (g_ref[...].astype(F32) + r_ref[...].astype(F32)).astype(BF16)

    grid_spec = pltpu.PrefetchScalarGridSpec(
        num_scalar_prefetch=1, grid=(L, NS, nt),
        in_specs=[pl.BlockSpec((None, None, tr, C), lambda l, s, t, c: (l, s, c[0] * nt + t, 0)),
                  pl.BlockSpec((None, None, tr, C), lambda l, s, t, c: (l, s, t, 0))],
        out_specs=pl.BlockSpec((None, None, tr, C), lambda l, s, t, c: (l, s, t, 0)))
    return pl.pallas_call(
        body, name=name, grid_spec=grid_spec,
        out_shape=jax.ShapeDtypeStruct((L, NS, rh, C), BF16),
        compiler_params=_cp(("arbitrary", "arbitrary", "arbitrary")),
    )(cidx, g, recv)


def chip_exchange(parts, name):
    n = len(parts)

    def body(*refs):
        ins, outs = refs[:n], refs[n:2 * n]
        send_sems, recv_sems = refs[2 * n:]
        x, y, c, chips = _place()
        cps = []
        for k, (cx, cy) in enumerate(chips):
            for a in range(n):
                cp = pltpu.make_async_remote_copy(
                    src_ref=ins[a].at[:, 2 * cx + cy], dst_ref=outs[a].at[k],
                    send_sem=send_sems.at[a * 3 + k], recv_sem=recv_sems.at[a * 3 + k],
                    device_id=(cx, cy, c), device_id_type=MESH_T)
                cp.start()
                cps.append(cp)
        for cp in cps:
            cp.wait()

    return pl.pallas_call(
        body, name=name,
        out_shape=[jax.ShapeDtypeStruct((3, p.shape[0]) + p.shape[2:], p.dtype) for p in parts],
        in_specs=[ANY] * n, out_specs=[ANY] * n,
        scratch_shapes=[pltpu.SemaphoreType.DMA((3 * n,)), pltpu.SemaphoreType.DMA((3 * n,))],
        compiler_params=_cp(),
    )(*parts)


def chip_sum(part, recv, chip_c, name, dest=None, slot=0, n_slots=1):
    _, NS, RH, C = part.shape
    tr = _row_tile(RH, 256 if C > 1024 else 512)
    nt = RH // tr

    def body(cc_ref, p_ref, r_ref, *rest):
        o_ref = rest[-1]
        acc = p_ref[...].astype(F32)
        for k in range(3):
            acc = acc + r_ref[k].astype(F32)
        o_ref[...] = acc

    in_specs = [pl.BlockSpec((None, None, tr, C), lambda t, cc: (0, cc[0], t, 0)),
                pl.BlockSpec((3, None, tr, C), lambda t, cc: (0, 0, t, 0))]
    args = [chip_c, part, recv]
    aliases = {}
    if dest is not None:
        in_specs.append(ANY)
        args.append(dest)
        aliases = {3: 0}
    grid_spec = pltpu.PrefetchScalarGridSpec(
        num_scalar_prefetch=1, grid=(nt,), in_specs=in_specs,
        out_specs=pl.BlockSpec((None, tr, C), lambda t, cc: (slot, cc[1] * nt + t, 0)))
    return pl.pallas_call(
        body, name=name, grid_spec=grid_spec,
        out_shape=jax.ShapeDtypeStruct((n_slots, 2 * RH, C), F32),
        input_output_aliases=aliases,
        compiler_params=_cp(("arbitrary",)),
    )(*args)


def sibling_fill_halves(bufs, name):
    n = len(bufs)

    def body(*refs):
        outs = refs[n:2 * n]
        send_sems, recv_sems = refs[2 * n:]
        x, y, c, _ = _place()
        cps = []
        for a in range(n):
            rh = outs[a].shape[-2] // 2
            cp = pltpu.make_async_remote_copy(
                src_ref=_half_rows(outs[a], c, rh), dst_ref=_half_rows(outs[a], c, rh),
                send_sem=send_sems.at[a], recv_sem=recv_sems.at[a], device_id=(x, y, 1 - c), device_id_type=MESH_T)
            cp.start()
            cps.append(cp)
        for a, cp in enumerate(cps):
            cp.wait_send()
        for a in range(n):
            rh = outs[a].shape[-2] // 2
            pltpu.make_async_remote_copy(
                src_ref=_half_rows(outs[a], 1 - c, rh), dst_ref=_half_rows(outs[a], 1 - c, rh),
                send_sem=send_sems.at[a], recv_sem=recv_sems.at[a], device_id=(x, y, 1 - c),
                device_id_type=MESH_T).wait_recv()

    return pl.pallas_call(
        body, name=name,
        out_shape=[jax.ShapeDtypeStruct(b.shape, b.dtype) for b in bufs],
        in_specs=[ANY] * n, out_specs=[ANY] * n,
        input_output_aliases={a: a for a in range(n)},
        scratch_shapes=[pltpu.SemaphoreType.DMA((n,)), pltpu.SemaphoreType.DMA((n,))],
        compiler_params=_cp(),
    )(*bufs)


HBM_SPEC = pl.BlockSpec(memory_space=pltpu.HBM)
SEM_SPEC = pl.BlockSpec(memory_space=pltpu.SEMAPHORE)
DATAFLOW_EFFECT = pltpu.SideEffectType.DATAFLOW_SIDE_EFFECTING


def split_start(bufs, copies_fn, n_copies, name, dep=None):
    nb = len(bufs)
    extra = [] if dep is None else [dep]

    def body(*refs):
        ne = len(extra)
        send_sems, recv_sems = refs[nb + ne], refs[nb + ne + 1]
        token = refs[-1]
        for cp in copies_fn(refs[:nb], send_sems, recv_sems):
            cp.start()
        token[...] = jnp.zeros_like(token)

    outs = pl.pallas_call(
        body, name=name,
        out_shape=(pltpu.SemaphoreType.DMA((n_copies,)), pltpu.SemaphoreType.DMA((n_copies,)),
                   *[pltpu.HBM(b.shape, b.dtype) for b in bufs], jax.ShapeDtypeStruct((8, 128), F32)),
        in_specs=[HBM_SPEC] * nb + [ANY] * len(extra),
        out_specs=(SEM_SPEC, SEM_SPEC, *[HBM_SPEC] * nb, pl.BlockSpec(memory_space=pltpu.VMEM)),
        input_output_aliases={i: 2 + i for i in range(nb)},
        compiler_params=pltpu.CompilerParams(has_side_effects=DATAFLOW_EFFECT),
    )(*[pltpu.with_memory_space_constraint(b, pltpu.HBM) for b in bufs], *extra)
    return outs[0], outs[1], list(outs[2:2 + nb]), outs[-1]


def split_wait(send_sems, recv_sems, bufs, after, copies_fn, name):
    nb = len(bufs)

    def body(*refs):
        for cp in copies_fn(refs[:nb], refs[nb], refs[nb + 1]):
            cp.wait_send()
            cp.wait_recv()

    outs = pl.pallas_call(
        body, name=name,
        out_shape=tuple(pltpu.HBM(b.shape, b.dtype) for b in bufs),
        in_specs=[HBM_SPEC] * nb + [SEM_SPEC, SEM_SPEC, ANY],
        out_specs=tuple([HBM_SPEC] * nb),
        input_output_aliases={i: i for i in range(nb)},
        compiler_params=pltpu.CompilerParams(has_side_effects=DATAFLOW_EFFECT),
    )(*bufs, send_sems, recv_sems, after)
    return list(outs)


def chip_exchange_copies(refs, send_sems, recv_sems):
    n = len(refs) // 2
    x, y, c, chips = _place()
    cps = []
    for a in range(n):
        for k, (cx, cy) in enumerate(chips):
            cps.append(pltpu.make_async_remote_copy(
                src_ref=refs[a].at[:, 2 * cx + cy], dst_ref=refs[n + a].at[k],
                send_sem=send_sems.at[a * 3 + k], recv_sem=recv_sems.at[a * 3 + k],
                device_id=(cx, cy, c), device_id_type=MESH_T))
    return cps


def allgather_direct_copies(refs, send_sems, recv_sems):
    x, y, c, _ = _place()
    me = 4 * x + 2 * y + c
    cps = []
    for r in range(1, N_DEV):
        fx, fy, fc = (r >> 2) & 1, (r >> 1) & 1, r & 1
        peer = (1 - x if fx else x, 1 - y if fy else y, 1 - c if fc else c)
        cps.append(pltpu.make_async_remote_copy(
            src_ref=refs[0], dst_ref=refs[1].at[me], send_sem=send_sems.at[r - 1], recv_sem=recv_sems.at[r - 1],
            device_id=peer, device_id_type=MESH_T))
    return cps


def pair_exchange_copies(refs, send_sems, recv_sems):
    n = len(refs) // 2
    x, y, c, _ = _place()
    return [pltpu.make_async_remote_copy(
        src_ref=_half_rows(refs[a], 1 - c, refs[a].shape[-2] // 2), dst_ref=refs[n + a],
        send_sem=send_sems.at[a], recv_sem=recv_sems.at[a], device_id=(x, y, 1 - c), device_id_type=MESH_T)
        for a in range(n)]


def weight_ici_copies(refs, send_sems, recv_sems):
    x, y, c, chips = _place()
    cps = []
    for a in range(len(refs)):
        for k, (cx, cy) in enumerate(chips):
            cps.append(pltpu.make_async_remote_copy(
                src_ref=refs[a].at[2 * x + y, c], dst_ref=refs[a].at[2 * x + y, c],
                send_sem=send_sems.at[a * 3 + k], recv_sem=recv_sems.at[a * 3 + k],
                device_id=(cx, cy, c), device_id_type=MESH_T))
    return cps


def weight_d2d_copies(refs, send_sems, recv_sems):
    x, y, c, chips = _place()
    cps = []
    for a in range(len(refs)):
        for k, (cx, cy) in enumerate(chips):
            cps.append(pltpu.make_async_remote_copy(
                src_ref=refs[a].at[2 * cx + cy, c], dst_ref=refs[a].at[2 * cx + cy, c],
                send_sem=send_sems.at[a * 3 + k], recv_sem=recv_sems.at[a * 3 + k],
                device_id=(x, y, 1 - c), device_id_type=MESH_T))
    return cps


def cast_place(w, lead, chip_arr, name, dep=None):
    R, C = w.shape[-2:]
    tr = _row_tile(R, 256 if C > 1024 else 512)

    def body(chip_ref, w_ref, *rest):
        rest[-1][...] = w_ref[...].astype(BF16)

    in_specs = [pl.BlockSpec((None,) * len(lead) + (tr, C), lambda t, ch: (*lead, t, 0))]
    args = [chip_arr, w]
    if dep is not None:
        in_specs.append(ANY)
        args.append(dep)
    grid_spec = pltpu.PrefetchScalarGridSpec(
        num_scalar_prefetch=1, grid=(R // tr,), in_specs=in_specs,
        out_specs=pl.BlockSpec((None, tr, C), lambda t, ch: (ch[0], t, 0)))
    return pl.pallas_call(
        body, name=name, grid_spec=grid_spec,
        out_shape=jax.ShapeDtypeStruct((N_CHIPS, R, C), BF16),
        compiler_params=_cp(("arbitrary",)),
    )(*args)


def _s5_prepare(lam_re, lam_im, log_dt, b_re, b_im, c_re, c_im, groups_per_block):
    G, P = lam_re.shape
    N = b_re.shape[-1]
    gb = groups_per_block
    nblk = G // gb
    dt = jnp.exp(log_dt)[:, None]
    e = jnp.exp(lam_re * dt)
    a_re = e * jnp.cos(lam_im * dt)
    a_im = e * jnp.sin(lam_im * dt)
    n2 = lam_re * lam_re + lam_im * lam_im
    co_re = ((a_re - 1.0) * lam_re + a_im * lam_im) / n2
    co_im = (a_im * lam_re - (a_re - 1.0) * lam_im) / n2
    bb_re = co_re[..., None] * b_re - co_im[..., None] * b_im
    bb_im = co_re[..., None] * b_im + co_im[..., None] * b_re
    eye = jnp.eye(gb, dtype=F32)

    def blockdiag_np(m):
        m = m.reshape(nblk, gb, N, P)
        return jnp.einsum('bgnp,gh->bgnhp', m, eye).reshape(nblk, gb * N, gb * P)

    b_np_re = jnp.swapaxes(bb_re, 1, 2)
    b_np_im = jnp.swapaxes(bb_im, 1, 2)
    return (a_re.reshape(nblk, 1, gb * P), a_im.reshape(nblk, 1, gb * P),
            blockdiag_np(b_np_re), blockdiag_np(b_np_im), blockdiag_np(c_re), blockdiag_np(c_im))


def _to_scan_order(a):
    S, D = a.shape
    return a.reshape(SCAN_LANES, S // SCAN_LANES, D).transpose(1, 0, 2).reshape(S, D)


def _from_scan_order(a):
    S, D = a.shape
    return a.reshape(S // SCAN_LANES, SCAN_LANES, D).transpose(1, 0, 2).reshape(S, D)


def _pad_rows(a, mult=8):
    r = (-a.shape[0]) % mult
    if r:
        a = jnp.concatenate([a, jnp.zeros((r, a.shape[1]), a.dtype)], axis=0)
    return a


def _pack_rows(arrs, width):
    parts, offs, o = [], [], 0
    for a in arrs:
        flat = a.reshape(-1)
        r = (-flat.shape[0]) % (16 * width)
        if r:
            flat = jnp.concatenate([flat, jnp.zeros((r,), flat.dtype)])
        p = flat.reshape(-1, width)
        parts.append(p)
        offs.append((o, a.shape, a.size))
        o += p.shape[0]
    if o % 64:
        parts.append(jnp.zeros((64 - o % 64, width), parts[0].dtype))
    return jnp.concatenate(parts, axis=0), offs


def _unpack_rows(packed, offs):
    outs = []
    for o, shape, size in offs:
        rows = -(-size // packed.shape[1])
        outs.append(packed[o:o + rows].reshape(-1)[:size].reshape(shape))
    return outs


def kernel(x, c, ada_w, ada_b, norm_pre, norm_post, ffn_w_in, ffn_w_out, ab_w_in, pool_w, pool_scale, sgu_ln_g, sgu_ln_b, sgu_w, sgu_b, ab_w_out, ssm_w_in, ssm_lam_re, ssm_lam_im, ssm_b_re, ssm_b_im, ssm_c_re, ssm_c_im, ssm_d, ssm_log_dt, ssm_w_glu, loss_target, m_ada_w, m_ada_b, m_norm_pre, m_norm_post, m_ffn_w_in, m_ffn_w_out, m_ab_w_in, m_pool_w, m_pool_scale, m_sgu_ln_g, m_sgu_ln_b, m_sgu_w, m_sgu_b, m_ab_w_out, m_ssm_w_in, m_ssm_lam_re, m_ssm_lam_im, m_ssm_b_re, m_ssm_b_im, m_ssm_c_re, m_ssm_c_im, m_ssm_d, m_ssm_log_dt, m_ssm_w_glu, v_ada_w, v_ada_b, v_norm_pre, v_norm_post, v_ffn_w_in, v_ffn_w_out, v_ab_w_in, v_pool_w, v_pool_scale, v_sgu_ln_g, v_sgu_ln_b, v_sgu_w, v_sgu_b, v_ab_w_out, v_ssm_w_in, v_ssm_lam_re, v_ssm_lam_im, v_ssm_b_re, v_ssm_b_im, v_ssm_c_re, v_ssm_c_im, v_ssm_d, v_ssm_log_dt, v_ssm_w_glu):
    weights = dict(ada_w=ada_w, ada_b=ada_b, norm_pre=norm_pre, norm_post=norm_post, ffn_w_in=ffn_w_in,
                   ffn_w_out=ffn_w_out, ab_w_in=ab_w_in, pool_w=pool_w, pool_scale=pool_scale, sgu_ln_g=sgu_ln_g,
                   sgu_ln_b=sgu_ln_b, sgu_w=sgu_w, sgu_b=sgu_b, ab_w_out=ab_w_out, ssm_w_in=ssm_w_in,
                   ssm_lam_re=ssm_lam_re, ssm_lam_im=ssm_lam_im, ssm_b_re=ssm_b_re, ssm_b_im=ssm_b_im,
                   ssm_c_re=ssm_c_re, ssm_c_im=ssm_c_im, ssm_d=ssm_d, ssm_log_dt=ssm_log_dt, ssm_w_glu=ssm_w_glu)
    m_in = dict(ada_w=m_ada_w, ada_b=m_ada_b, norm_pre=m_norm_pre, norm_post=m_norm_post, ffn_w_in=m_ffn_w_in,
                ffn_w_out=m_ffn_w_out, ab_w_in=m_ab_w_in, pool_w=m_pool_w, pool_scale=m_pool_scale,
                sgu_ln_g=m_sgu_ln_g, sgu_ln_b=m_sgu_ln_b, sgu_w=m_sgu_w, sgu_b=m_sgu_b, ab_w_out=m_ab_w_out,
                ssm_w_in=m_ssm_w_in, ssm_lam_re=m_ssm_lam_re, ssm_lam_im=m_ssm_lam_im, ssm_b_re=m_ssm_b_re,
                ssm_b_im=m_ssm_b_im, ssm_c_re=m_ssm_c_re, ssm_c_im=m_ssm_c_im, ssm_d=m_ssm_d,
                ssm_log_dt=m_ssm_log_dt, ssm_w_glu=m_ssm_w_glu)
    v_in = dict(ada_w=v_ada_w, ada_b=v_ada_b, norm_pre=v_norm_pre, norm_post=v_norm_post, ffn_w_in=v_ffn_w_in,
                ffn_w_out=v_ffn_w_out, ab_w_in=v_ab_w_in, pool_w=v_pool_w, pool_scale=v_pool_scale,
                sgu_ln_g=v_sgu_ln_g, sgu_ln_b=v_sgu_ln_b, sgu_w=v_sgu_w, sgu_b=v_sgu_b, ab_w_out=v_ab_w_out,
                ssm_w_in=v_ssm_w_in, ssm_lam_re=v_ssm_lam_re, ssm_lam_im=v_ssm_lam_im, ssm_b_re=v_ssm_b_re,
                ssm_b_im=v_ssm_b_im, ssm_c_re=v_ssm_c_re, ssm_c_im=v_ssm_c_im, ssm_d=v_ssm_d,
                ssm_log_dt=v_ssm_log_dt, ssm_w_glu=v_ssm_w_glu)
    names = list(weights.keys())

    xi, yi, ci = lax.axis_index("x"), lax.axis_index("y"), lax.axis_index("c")
    chip = 2 * xi + yi
    me = 4 * xi + 2 * yi + ci
    S, D = x.shape[1], x.shape[2]
    L = ada_w.shape[0]
    NSUB = norm_pre.shape[1]
    DS = norm_pre.shape[2]
    FS = ffn_w_in.shape[-1]
    FR = ffn_w_out.shape[-2]
    x0 = x[0]
    target = loss_target[0]

    small_parts = [_pad_rows(p) for p in (c.reshape(D // DS, DS), norm_pre.reshape(L * NSUB, DS),
                                          norm_post.reshape(L * NSUB, DS), ssm_d.reshape(1, DS))]
    small_offs = [0]
    for p in small_parts:
        small_offs.append(small_offs[-1] + p.shape[0])
    small_all = allgather_small(jnp.concatenate(small_parts, axis=0), "ag_small").reshape(N_DEV, -1, DS)
    c_all = small_all[:, :D // DS].reshape(N_DEV, D)
    per_chip = small_all[0::2]
    o = small_offs[1]
    g_pre_full = jnp.moveaxis(per_chip[:, o:o + L * NSUB], 0, 1).reshape(L, NSUB, D)
    o = small_offs[2]
    g_post_full = jnp.moveaxis(per_chip[:, o:o + L * NSUB], 0, 1).reshape(L, NSUB, D)
    o = small_offs[3]
    d_full = jnp.moveaxis(per_chip[:, o:o + 1], 0, 1).reshape(1, D)

    pieces = []
    for l in range(L):
        pieces.append((f"ffn_{l}_0", [(ffn_w_in, (l, 0)), (ffn_w_out, (l, 0))]))
        if l % 2 == 0:
            pieces.append((f"mix_{l}", [(ab_w_in, (l // 2,)), (ab_w_out, (l // 2,))]))
        else:
            pieces.append((f"mix_{l}", [(ssm_w_in, (l // 2,)), (ssm_w_glu, (l // 2,))]))
        pieces.append((f"ffn_{l}_1", [(ffn_w_in, (l, 1)), (ffn_w_out, (l, 1))]))
    tags = [tag for tag, _ in pieces]
    chip_arr = chip.reshape(1).astype(jnp.int32)
    wg_started, wg_passing = {}, {}

    def cast_piece(tag, ws, dep):
        wg_started[tag] = [cast_place(w, lead, chip_arr, f"wg_cast_{tag}_{i}", dep=dep).reshape(
            N_CHIPS, 2, w.shape[-2] // 2, w.shape[-1]) for i, (w, lead) in enumerate(ws)]

    def start_piece(tag, dep):
        bufs = wg_started[tag]
        wg_started[tag] = split_start(bufs, weight_ici_copies, 3 * len(bufs), f"wg_ici_start_{tag}", dep=dep)
        return wg_started[tag][3]

    def weights_arrived(i, after):
        ssem, rsem, bufs, _ = wg_started[tags[i]]
        lands = split_wait(ssem, rsem, bufs, after, weight_ici_copies, f"wg_ici_wait_{tags[i]}")
        wg_passing[i] = split_start(lands, weight_d2d_copies, 3 * len(lands), f"wg_d2d_start_{tags[i]}")
        return wg_passing[i][3][0, 0]

    def weights_of(i, after):
        ssem, rsem, lands, _ = wg_passing[i]
        return split_wait(ssem, rsem, lands, after, weight_d2d_copies, f"wg_d2d_wait_{tags[i]}")

    NS = ada_w.shape[-1]
    ada_b_shard = lax.dynamic_slice_in_dim(ada_b, chip * NS, NS, axis=1).reshape(L, 1, NS)
    mod_part = ada_mod(c_all, ada_w, ada_b_shard, "ada_mod")
    mod_all = allgather_small(mod_part.reshape(L * N_DEV, NS), "ag_mod").reshape(N_DEV, L, N_DEV, NS)
    mod_mine = lax.dynamic_index_in_dim(mod_all[0::2], me, axis=2, keepdims=False)
    mod = jnp.moveaxis(mod_mine, 0, 1).reshape(L, NSUB, 3, D)
    cast_piece(*pieces[0], mod_all)
    first_token = start_piece(tags[0], mod_all)
    wg_tokens = [first_token]
    for tag, ws in pieces[1:]:
        cast_piece(tag, ws, first_token)
        wg_tokens.append(start_piece(tag, first_token))

    def vec_of(l, s):
        return jnp.concatenate([mod[l, s], g_pre_full[l, s][None], g_post_full[l, s][None],
                                jnp.zeros((3, D), F32)], axis=0)

    GB = 8
    s5_args = (ssm_lam_re[0], ssm_lam_im[0], ssm_log_dt[0], ssm_b_re[0], ssm_b_im[0], ssm_c_re[0], ssm_c_im[0])
    (a_re, a_im, bblk_re, bblk_im, cblk_re, cblk_im), s5_vjp = jax.vjp(lambda *p: _s5_prepare(*p, GB), *s5_args)
    bre16, bim16 = bblk_re.astype(BF16), bblk_im.astype(BF16)
    cre16, cim16 = cblk_re.astype(BF16), cblk_im.astype(BF16)
    cre16_t, cim16_t = jnp.swapaxes(cre16, 1, 2), jnp.swapaxes(cim16, 1, 2)

    pool_w16 = pool_w[0].astype(BF16)
    sgu_bexp = jnp.broadcast_to(sgu_b[0][:, :, None], sgu_w[0].shape)

    saved = {}
    ffn_w = {}
    xcur = x0
    stage = [0]

    def next_weights(after, vec):
        i = stage[0]
        stage[0] += 1
        if i not in wg_passing:
            weights_arrived(i, after)
        g = weights_of(i, after)
        if 1 <= i < len(tags) - 1:
            vec = vec + weights_arrived(i + 1, after)
        return g, vec

    for l in range(L):
        v0 = vec_of(l, 0)
        g, v0p = next_weights(xcur if l else v0 + sum(t[0, 0] for t in wg_tokens), v0)
        ffn_w[(l, 0)] = (g[0].reshape(N_CHIPS, D, FS), g[1].reshape(N_CHIPS, FR, D))
        out, h, u, f = ffn_fwd(xcur, v0p, *ffn_w[(l, 0)], f"ffn_fwd_{l}_0")
        saved[(l, 0)] = (xcur, v0, h, u, f)
        xcur = out
        v1 = vec_of(l, 1)
        mix_g, v1p = next_weights(xcur, v1)
        if l % 2 == 0:
            abin_g = mix_g[0].reshape(N_CHIPS, D, -1)
            about_g = mix_g[1].reshape(-1, D)
            h, z = pre_matmul(xcur, v1p, abin_g, f"mixa_in_{l}")
            ycat = mixa_core_fwd(z, pool_w16, pool_scale, sgu_ln_g, sgu_ln_b, sgu_w[0], sgu_bexp, f"mixa_core_{l}")
            out, f = matmul_post(ycat, about_g, xcur, v1, f"mixa_out_{l}")
            saved[(l, 1)] = (xcur, v1, h, z, ycat, f)
        else:
            sin_g = mix_g[0].reshape(1, -1, D)
            glu_g = mix_g[1].reshape(N_CHIPS, D, -1)
            h, uu = pre_matmul(xcur, v1p, sin_g, f"s5_in_{l}")
            us = _to_scan_order(uu)
            ys = s5_core_fwd(us, a_re, a_im, bre16, bim16, cre16_t, cim16_t, d_full, f"s5_core_{l}")
            yy = _from_scan_order(ys)
            out, f = s5_glu_fwd(yy, glu_g, xcur, v1, f"s5_glu_{l}")
            saved[(l, 1)] = (xcur, v1, h, us, yy, f)
        xcur = out
        v2 = vec_of(l, 2)
        g, v2p = next_weights(xcur, v2)
        ffn_w[(l, 1)] = (g[0].reshape(N_CHIPS, D, FS), g[1].reshape(N_CHIPS, FR, D))
        out, h, u, f = ffn_fwd(xcur, v2p, *ffn_w[(l, 1)], f"ffn_fwd_{l}_1")
        saved[(l, 2)] = (xcur, v2, h, u, f)
        xcur = out

    dcur, sq = loss_head(xcur, target, "loss_head")
    loss = lax.psum(sq[0, 0], ("x", "y", "c")) * (0.5 / D)

    accs = {}
    small_g = {}
    cidx = ci.reshape(1).astype(jnp.int32)
    chip_c = jnp.stack([chip, ci]).astype(jnp.int32)
    rs_open, rs_pending = [], []

    pin = [zero_after(loss.reshape(1, 1), "after_loss")]
    last_start = [pin[0]]

    def pin_behind(token):
        pin.append(token)
        last_start[0] = token

    def pinned(v):
        for token in pin:
            v = v + token[0, 0]
        pin.clear()
        return v

    def rs_begin(tag, items):
        arrs = [it[0] for it in items]
        lands = [lax.empty(a.shape[:-2] + (a.shape[-2] // 2, a.shape[-1]), BF16) for a in arrs]
        started = split_start(arrs + lands, pair_exchange_copies, len(arrs), f"rs_pair_start_{tag}")
        rs_open.append((tag, items, started))
        pin_behind(started[3])

    def rs_advance(after):
        while rs_open:
            tag, items, (ssem, rsem, bufs, _) = rs_open.pop(0)
            n = len(items)
            bufs = split_wait(ssem, rsem, bufs, after, pair_exchange_copies, f"rs_pair_wait_{tag}")
            parts = [pair_sum(a, r, cidx, f"rs_pair_sum_{tag}_{i}") for i, (a, r) in enumerate(zip(bufs[:n], bufs[n:]))]
            lands = [lax.empty((3, 1) + p.shape[2:], BF16) for p in parts]
            started = split_start(parts + lands, chip_exchange_copies, 3 * n, f"rs_chip_start_{tag}")
            rs_pending.append((tag, items, started))
            pin_behind(started[3])

    def ffn_back(l, s, k, dcur, before_wgrads=None):
        xin, vv, h, u, f = saved[(l, s)]
        vv = pinned(vv)
        dx, df, du, act, acc = ffn_bwd(dcur, xin, f, u, vv, *ffn_w[(l, k)], f"ffn_bwd_{l}_{k}")
        accs[(l, s)] = acc
        rs_advance(dx)
        dep = None if before_wgrads is None else before_wgrads()
        g_win = tn_matmul(h, du, 1024, f"ffn_dwin_{l}_{k}", dep=dep)
        g_wout = tn_matmul_cols(act, df, FS, D, f"ffn_dwout_{l}_{k}")
        rs_begin(f"ffn_{l}_{k}", [(g_win[None], "ffn_w_in", 2 * l + k, 2 * L),
                                  (g_wout.reshape(1, N_CHIPS, FR, D), "ffn_w_out", 2 * l + k, 2 * L)])
        return dx

    grads = {}

    small_names = ["pool_w", "pool_scale", "sgu_ln_g", "sgu_ln_b", "sgu_w", "sgu_b", "ssm_lam_re", "ssm_lam_im",
                   "ssm_b_re", "ssm_b_im", "ssm_c_re", "ssm_c_im", "ssm_log_dt"]
    small_open = []

    def small_grads_start():
        acc_all = jnp.stack([jnp.stack([accs[(l, s)] for s in range(NSUB)]) for l in range(L)])
        dmod_mine = acc_all[:, :, A_SHIFT:A_GATE + 1].reshape(L, NSUB * 3 * D)
        packed, offs = _pack_rows([dmod_mine, acc_all[:, :, A_GPRE], acc_all[:, :, A_GPOST], dd_mine]
                                  + [small_g[n] for n in small_names], D)
        packed = pinned(packed).astype(BF16)
        started = split_start([packed, lax.empty((N_DEV,) + packed.shape, BF16)], allgather_direct_copies,
                              N_DEV - 1, "ag_grads_start")
        small_open.append((started, offs))
        pin_behind(started[3])
        return started[3]

    def small_grads_finish(after):
        (ssem, rsem, bufs, _), offs = small_open.pop()
        own, land = split_wait(ssem, rsem, bufs, after, allgather_direct_copies, "ag_grads_wait")
        me_arr = me.reshape(1).astype(jnp.int32)
        summed = _unpack_rows(sum_gathered(land, own, me_arr, "sum_small"), offs)
        grads.update({n: g for n, g in zip(small_names, summed[4:])})
        grads["ada_b"] = summed[0]
        grads["norm_pre"] = lax.dynamic_slice_in_dim(summed[1], chip * DS, DS, axis=2)
        grads["norm_post"] = lax.dynamic_slice_in_dim(summed[2], chip * DS, DS, axis=2)
        grads["ssm_d"] = lax.dynamic_slice_in_dim(summed[3], chip * DS, DS, axis=1)
        nm = L * NSUB * 3
        is_me = (jnp.arange(N_DEV) == me)[:, None, None]
        dmod_all = jnp.where(is_me, own[None, :nm], land[:, :nm]).reshape(N_DEV, L, NSUB * 3 * D)
        dmod_shard = lax.dynamic_slice_in_dim(jnp.moveaxis(dmod_all, 0, 1), chip * NS, NS, axis=2)
        grads["ada_w"] = ada_grad(c_all, dmod_shard, "ada_grad")

    for l in reversed(range(L)):
        dcur = ffn_back(l, 2, 1, dcur)
        if l % 2 == 0:
            xin, vv, h, z, ycat, f = saved[(l, 1)]
            vv = pinned(vv)
            df, dact, acc_post = post_bwd_matmul(dcur, f, vv, about_g, f"mixa_out_bwd_{l}")
            rs_advance(dact)
            g_about = tn_matmul_cols(ycat, df, 512, D, f"mixa_dwout_{l}")
            dz, dpw, dvecw, dsw, dsb = mixa_core_bwd(z, dact, pool_w16, pool_scale, sgu_ln_g, sgu_ln_b, sgu_w[0],
                                                     sgu_bexp, f"mixa_core_bwd_{l}")
            g_abin = tn_matmul_cols(h, dz, 512, abin_g.shape[-1], f"mixa_dwin_{l}")
            dcur, acc_pre = matmul_pre_bwd(dz, abin_g, xin, dcur, vv, f"mixa_in_bwd_{l}")
            accs[(l, 1)] = acc_pre + acc_post
            rs_begin(f"mixa_{l}", [(g_abin[None], "ab_w_in", 0, 1),
                                   (g_about.reshape(1, N_CHIPS, -1, D), "ab_w_out", 0, 1)])
            small_g.update(pool_w=dpw[None], pool_scale=dvecw[0:1], sgu_ln_g=dvecw[1:2], sgu_ln_b=dvecw[2:3],
                           sgu_w=dsw[None], sgu_b=dsb.T[None])
        else:
            xin, vv, h, us, yy, f = saved[(l, 1)]
            vv = pinned(vv)
            dy, dab, gact, acc_post = s5_glu_bwd(dcur, f, yy, vv, glu_g, f"s5_glu_bwd_{l}")
            rs_advance(dy)
            g_glu = tn_matmul_cols(gact, dab, 512, glu_g.shape[-1], f"s5_dwglu_{l}")
            dys = _to_scan_order(dy)
            dus, dbre, dbim, dcre_t, dcim_t, dar, dai, dd = s5_core_bwd(
                us, dys, a_re, a_im, bre16, bim16, cre16, cim16, d_full, f"s5_core_bwd_{l}")
            du = _from_scan_order(dus).astype(BF16)
            g_sin = tn_matmul_cols(h, du, 512, D, f"s5_dwin_{l}")
            dcur, acc_pre = matmul_pre_bwd(du, sin_g, xin, dcur, vv, f"s5_in_bwd_{l}")
            accs[(l, 1)] = acc_pre + acc_post
            rs_begin(f"s5_{l}", [(g_sin.reshape(1, N_CHIPS, -1, D), "ssm_w_in", 0, 1), (g_glu[None], "ssm_w_glu", 0, 1)])
            s5_grads = s5_vjp((dar, dai, dbre, dbim, jnp.swapaxes(dcre_t, 1, 2), jnp.swapaxes(dcim_t, 1, 2)))
            small_g.update(ssm_lam_re=s5_grads[0][None], ssm_lam_im=s5_grads[1][None], ssm_log_dt=s5_grads[2][None],
                           ssm_b_re=s5_grads[3][None], ssm_b_im=s5_grads[4][None], ssm_c_re=s5_grads[5][None],
                           ssm_c_im=s5_grads[6][None])
            dd_mine = dd
        dcur = ffn_back(l, 0, 0, dcur, before_wgrads=small_grads_start if l == 0 else None)
    grad_x = dcur[None]
    rs_advance(last_start[0])
    small_grads_finish(last_start[0])

    big_names = ["ffn_w_in", "ffn_w_out", "ab_w_in", "ab_w_out", "ssm_w_in", "ssm_w_glu"]
    deltas, new_m, new_v = {}, {}, {}

    def update(n):
        grads[n] = grads[n].reshape(weights[n].shape)
        deltas[n], new_m[n], new_v[n] = adamw_nd(weights[n], grads[n], m_in[n], v_in[n], f"adamw_{n}")

    for n in names:
        if n not in big_names:
            update(n)

    fin = {}
    after = pinned(deltas["ssm_log_dt"]) + deltas["ada_w"][0, 0, 0]
    for tag, items, (ssem, rsem, bufs, _) in rs_pending:
        bufs = split_wait(ssem, rsem, bufs, after, chip_exchange_copies, f"rs_chip_wait_{tag}")
        n = len(items)
        for i, (it, p, r) in enumerate(zip(items, bufs[:n], bufs[n:])):
            fin[it[1]] = chip_sum(p, r, chip_c, f"rs_chip_sum_{tag}_{i}", dest=fin.get(it[1]), slot=it[2],
                                  n_slots=it[3])
    filled = sibling_fill_halves([fin[n] for n in big_names], "rs_fill")
    for n, g in zip(big_names, filled):
        grads[n] = g
        update(n)

    return (loss, grad_x, *[grads[n] for n in names], *[deltas[n] for n in names],
            *[new_m[n] for n in names], *[new_v[n] for n in names])
```

```python
import math

import jax
import jax.numpy as jnp
from jax import lax
from jax.experimental import pallas as pl
from jax.experimental.pallas import tpu as pltpu

F32 = jnp.float32
BF16 = jnp.bfloat16
EPS = 1e-6
MESH_T = pl.DeviceIdType.MESH
VMEM_LIMIT_BYTES = 56 * 1024 * 1024
N_CHIPS = 4
N_DEV = 8
POOL_WINDOWS = (2, 4, 8, 16)
CHUNK = 128
SCAN_LANES = 8
SCAN_UNROLL = 4
FFN_BWD_CHUNK = 2048
ADAM_LR = 0.001
ADAM_B1 = 0.9
ADAM_B2 = 0.999
ADAM_EPS = 1e-08
ADAM_WD = 0.01
ADAM_STEP = 10
GELU_C = math.sqrt(2.0 / math.pi)
GELU_K = 0.044715

V_SHIFT, V_SCALE, V_GATE, V_GPRE, V_GPOST = 0, 1, 2, 3, 4
A_SHIFT, A_SCALE, A_GATE, A_GPRE, A_GPOST = 0, 1, 2, 3, 4

ANY = pl.BlockSpec(memory_space=pl.ANY)


def _cp(sem=None):
    if sem is None:
        return pltpu.CompilerParams(vmem_limit_bytes=VMEM_LIMIT_BYTES)
    return pltpu.CompilerParams(vmem_limit_bytes=VMEM_LIMIT_BYTES, dimension_semantics=sem)


def _dot(a, b):
    return jnp.dot(a, b, preferred_element_type=F32)


def _dot_nt(a, b):
    return lax.dot_general(a, b, (((1,), (1,)), ((), ())), preferred_element_type=F32)


def _dot_tn(a, b):
    return lax.dot_general(a, b, (((0,), (0,)), ((), ())), preferred_element_type=F32)


def _sigmoid(x):
    return 1.0 / (1.0 + jnp.exp(-x))


def _gelu(x):
    return 0.5 * x * (1.0 + jnp.tanh(GELU_C * (x + GELU_K * x * x * x)))


def _gelu_grad(x):
    t = jnp.tanh(GELU_C * (x + GELU_K * x * x * x))
    return 0.5 * (1.0 + t) + 0.5 * x * (1.0 - t * t) * GELU_C * (1.0 + 3.0 * GELU_K * x * x)


def _rowsum(v):
    return jnp.sum(v, axis=0, keepdims=True)


def _lanemean(v):
    return jnp.mean(v, axis=-1, keepdims=True)


def _row(ref, i):
    return ref[pl.ds(i, 1), :]


def _pre_fwd(x, vec_ref):
    r = lax.rsqrt(_lanemean(x * x) + EPS)
    return (x * r) * _row(vec_ref, V_GPRE) * (1.0 + _row(vec_ref, V_SCALE)) + _row(vec_ref, V_SHIFT)


def _pre_bwd(x, dh, vec_ref):
    g = _row(vec_ref, V_GPRE)
    sc = 1.0 + _row(vec_ref, V_SCALE)
    r = lax.rsqrt(_lanemean(x * x) + EPS)
    xn = x * r
    dhx = dh * xn
    t2 = dh * (g * sc)
    dx = r * (t2 - xn * _lanemean(t2 * xn))
    return dx, _rowsum(dh), _rowsum(dhx * g), _rowsum(dhx * sc)


def _post_fwd(x, f, vec_ref, rw):
    q = lax.rsqrt(_lanemean(f * f) + EPS)
    return x + (rw * _row(vec_ref, V_GATE)) * (f * q * _row(vec_ref, V_GPOST))


def _post_bwd(dout, f, vec_ref, rw):
    gp = _row(vec_ref, V_GPOST)
    gate = _row(vec_ref, V_GATE)
    q = lax.rsqrt(_lanemean(f * f) + EPS)
    fhat = f * q
    dgate = _rowsum(dout * (rw * fhat * gp))
    dy = dout * (rw * gate)
    dgpost = _rowsum(dy * fhat)
    t = dy * gp
    df = q * (t - fhat * _lanemean(t * fhat))
    return df, dgate, dgpost


def _acc_add(acc_ref, row, v):
    acc_ref[pl.ds(row, 1), :] += v


def ffn_fwd(x, vec, win_g, wout_g, name, tm=512):
    S, D = x.shape
    FS = win_g.shape[-1]
    FR = wout_g.shape[-2]

    def body(x_ref, vec_ref, win_hbm, wout_hbm, out_ref, h_ref, u_ref, f_ref, win_s, wout_s, sem):
        @pl.when(pl.program_id(0) == 0)
        def _():
            cps = [pltpu.make_async_copy(win_hbm.at[j], win_s.at[j], sem.at[j]) for j in range(N_CHIPS)]
            cps += [pltpu.make_async_copy(wout_hbm.at[j], wout_s.at[pl.ds(j * FR, FR), :], sem.at[N_CHIPS + j])
                    for j in range(N_CHIPS)]
            for cp in cps:
                cp.start()
            for cp in cps:
                cp.wait()

        xv = x_ref[...]
        h = _pre_fwd(xv, vec_ref).astype(BF16)
        h_ref[...] = h
        f = jnp.zeros((tm, D), F32)
        for j in range(2):
            a = _dot(h, win_s[j])
            b = _dot(h, win_s[2 + j])
            u_ref[j] = a.astype(BF16)
            u_ref[2 + j] = b.astype(BF16)
            act = (a * _sigmoid(a) * b).astype(BF16)
            f = f + _dot(act, wout_s[pl.ds(j * FS, FS), :])
        f_ref[...] = f
        out_ref[...] = _post_fwd(xv, f, vec_ref, 0.5)

    row = pl.BlockSpec((tm, D), lambda i: (i, 0))
    return pl.pallas_call(
        body, name=name, grid=(S // tm,),
        in_specs=[row, pl.BlockSpec((8, D), lambda i: (0, 0)), ANY, ANY],
        out_specs=[row, row, pl.BlockSpec((N_CHIPS, tm, FS), lambda i: (0, i, 0)), row],
        out_shape=[jax.ShapeDtypeStruct((S, D), F32), jax.ShapeDtypeStruct((S, D), BF16),
                   jax.ShapeDtypeStruct((N_CHIPS, S, FS), BF16), jax.ShapeDtypeStruct((S, D), F32)],
        scratch_shapes=[pltpu.VMEM((N_CHIPS, D, FS), BF16), pltpu.VMEM((N_CHIPS * FR, D), BF16),
                        pltpu.SemaphoreType.DMA((2 * N_CHIPS,))],
        compiler_params=_cp(("arbitrary",)),
    )(x, vec, win_g, wout_g)


def ffn_bwd(dout, x, f, u, vec, win_g, wout_g, name, tm=256):
    S, D = x.shape
    FS = win_g.shape[-1]
    FR = wout_g.shape[-2]

    def body(dout_ref, x_ref, f_ref, u_ref, vec_ref, win_hbm, wout_hbm,
             dx_ref, df_ref, du_ref, act_ref, acc_ref, win_s, wout_s, sem):
        @pl.when(pl.program_id(0) == 0)
        def _():
            cps = [pltpu.make_async_copy(win_hbm.at[j], win_s.at[j], sem.at[j]) for j in range(N_CHIPS)]
            cps += [pltpu.make_async_copy(wout_hbm.at[j], wout_s.at[pl.ds(j * FR, FR), :], sem.at[N_CHIPS + j])
                    for j in range(N_CHIPS)]
            for cp in cps:
                cp.start()
            acc_ref[...] = jnp.zeros_like(acc_ref)
            for cp in cps:
                cp.wait()

        dout_v = dout_ref[...]
        df, dgate, dgpost = _post_bwd(dout_v, f_ref[...], vec_ref, 0.5)
        dfb = df.astype(BF16)
        df_ref[...] = dfb
        dh = jnp.zeros((tm, D), F32)
        chunks = [(c0, min(FFN_BWD_CHUNK, FS - c0)) for c0 in range(0, FS, FFN_BWD_CHUNK)]
        for j in range(2):
            for c0, cw in chunks:
                a = u_ref[j, :, c0:c0 + cw].astype(F32)
                b = u_ref[2 + j, :, c0:c0 + cw].astype(F32)
                sig = _sigmoid(a)
                sl = a * sig
                dact = _dot_nt(dfb, wout_s[pl.ds(j * FS + c0, cw), :])
                da = (dact * b * (sig * (1.0 + a * (1.0 - sig)))).astype(BF16)
                db = (dact * sl).astype(BF16)
                du_ref[j, :, c0:c0 + cw] = da
                du_ref[2 + j, :, c0:c0 + cw] = db
                act_ref[:, j * FS + c0:j * FS + c0 + cw] = (sl * b).astype(BF16)
                dh = dh + _dot_nt(da, win_s[j, :, c0:c0 + cw]) + _dot_nt(db, win_s[2 + j, :, c0:c0 + cw])
        dx, dshift, dscale, dgpre = _pre_bwd(x_ref[...], dh, vec_ref)
        dx_ref[...] = dout_v + dx
        _acc_add(acc_ref, A_SHIFT, dshift)
        _acc_add(acc_ref, A_SCALE, dscale)
        _acc_add(acc_ref, A_GATE, dgate)
        _acc_add(acc_ref, A_GPRE, dgpre)
        _acc_add(acc_ref, A_GPOST, dgpost)

    row = pl.BlockSpec((tm, D), lambda i: (i, 0))
    ublk = pl.BlockSpec((N_CHIPS, tm, FS), lambda i: (0, i, 0))
    const = pl.BlockSpec((8, D), lambda i: (0, 0))
    return pl.pallas_call(
        body, name=name, grid=(S // tm,),
        in_specs=[row, row, row, ublk, const, ANY, ANY],
        out_specs=[row, row, ublk, pl.BlockSpec((tm, 2 * FS), lambda i: (i, 0)), const],
        out_shape=[jax.ShapeDtypeStruct((S, D), F32), jax.ShapeDtypeStruct((S, D), BF16),
                   jax.ShapeDtypeStruct((N_CHIPS, S, FS), BF16), jax.ShapeDtypeStruct((S, 2 * FS), BF16),
                   jax.ShapeDtypeStruct((8, D), F32)],
        scratch_shapes=[pltpu.VMEM((N_CHIPS, D, FS), BF16), pltpu.VMEM((N_CHIPS * FR, D), BF16),
                        pltpu.SemaphoreType.DMA((2 * N_CHIPS,))],
        compiler_params=_cp(("arbitrary",)),
    )(dout, x, f, u, vec, win_g, wout_g)


def tn_matmul(a, b, tk, name, dep=None):
    S, K = a.shape
    nb, _, tn = b.shape
    extra = [] if dep is None else [dep]

    def body(a_ref, b_ref, *rest):
        rest[-1][...] = _dot_tn(a_ref[...], b_ref[...]).astype(BF16)

    return pl.pallas_call(
        body, name=name, grid=(K // tk, nb),
        in_specs=[pl.BlockSpec((S, tk), lambda i, j: (0, i)), pl.BlockSpec((None, S, tn), lambda i, j: (j, 0, 0))]
        + [ANY] * len(extra),
        out_specs=pl.BlockSpec((None, tk, tn), lambda i, j: (j, i, 0)),
        out_shape=jax.ShapeDtypeStruct((nb, K, tn), BF16),
        compiler_params=_cp(("arbitrary", "arbitrary")),
    )(a, b, *extra)


def tn_matmul_cols(a, b, tk, tn, name):
    S, K = a.shape
    N = b.shape[1]

    def body(a_ref, b_ref, o_ref):
        o_ref[...] = _dot_tn(a_ref[...], b_ref[...]).astype(BF16)

    return pl.pallas_call(
        body, name=name, grid=(K // tk, N // tn),
        in_specs=[pl.BlockSpec((S, tk), lambda i, j: (0, i)), pl.BlockSpec((S, tn), lambda i, j: (0, j))],
        out_specs=pl.BlockSpec((None, tk, tn), lambda i, j: (j, i, 0)),
        out_shape=jax.ShapeDtypeStruct((N // tn, K, tn), BF16),
        compiler_params=_cp(("arbitrary", "arbitrary")),
    )(a, b)


def pre_matmul(x, vec, w3, name, tm=256):
    S, D = x.shape
    nj, _, Nj = w3.shape

    def body(x_ref, vec_ref, w_ref, h_ref, z_ref):
        h = _pre_fwd(x_ref[...], vec_ref).astype(BF16)
        h_ref[...] = h
        for j in range(nj):
            z_ref[:, j * Nj:(j + 1) * Nj] = _dot(h, w_ref[j])

    row = pl.BlockSpec((tm, D), lambda i: (i, 0))
    return pl.pallas_call(
        body, name=name, grid=(S // tm,),
        in_specs=[row, pl.BlockSpec((8, D), lambda i: (0, 0)), pl.BlockSpec((nj, D, Nj), lambda i: (0, 0, 0))],
        out_specs=[row, pl.BlockSpec((tm, nj * Nj), lambda i: (i, 0))],
        out_shape=[jax.ShapeDtypeStruct((S, D), BF16), jax.ShapeDtypeStruct((S, nj * Nj), F32)],
        compiler_params=_cp(("arbitrary",)),
    )(x, vec, w3)


def matmul_pre_bwd(dz, w3, x, dres, vec, name, tm=256):
    S, D = x.shape
    nj, _, Nj = w3.shape

    def body(dz_ref, w_ref, x_ref, dres_ref, vec_ref, dx_ref, acc_ref):
        @pl.when(pl.program_id(0) == 0)
        def _():
            acc_ref[...] = jnp.zeros_like(acc_ref)

        dh = jnp.zeros((tm, D), F32)
        for j in range(nj):
            dh = dh + _dot_nt(dz_ref[:, j * Nj:(j + 1) * Nj], w_ref[j])
        dx, dshift, dscale, dgpre = _pre_bwd(x_ref[...], dh, vec_ref)
        dx_ref[...] = dres_ref[...] + dx
        _acc_add(acc_ref, A_SHIFT, dshift)
        _acc_add(acc_ref, A_SCALE, dscale)
        _acc_add(acc_ref, A_GPRE, dgpre)

    row = pl.BlockSpec((tm, D), lambda i: (i, 0))
    const = pl.BlockSpec((8, D), lambda i: (0, 0))
    return pl.pallas_call(
        body, name=name, grid=(S // tm,),
        in_specs=[pl.BlockSpec((tm, nj * Nj), lambda i: (i, 0)), pl.BlockSpec((nj, D, Nj), lambda i: (0, 0, 0)),
                  row, row, const],
        out_specs=[row, const],
        out_shape=[jax.ShapeDtypeStruct((S, D), F32), jax.ShapeDtypeStruct((8, D), F32)],
        compiler_params=_cp(("arbitrary",)),
    )(dz, w3, x, dres, vec)


def matmul_post(act, w, x, vec, name, tm=256):
    S, D = x.shape
    K = act.shape[1]

    def body(act_ref, w_ref, x_ref, vec_ref, out_ref, f_ref):
        f = _dot(act_ref[...], w_ref[...])
        f_ref[...] = f
        out_ref[...] = _post_fwd(x_ref[...], f, vec_ref, 1.0)

    row = pl.BlockSpec((tm, D), lambda i: (i, 0))
    return pl.pallas_call(
        body, name=name, grid=(S // tm,),
        in_specs=[pl.BlockSpec((tm, K), lambda i: (i, 0)), pl.BlockSpec((K, D), lambda i: (0, 0)), row,
                  pl.BlockSpec((8, D), lambda i: (0, 0))],
        out_specs=[row, row],
        out_shape=[jax.ShapeDtypeStruct((S, D), F32), jax.ShapeDtypeStruct((S, D), F32)],
        compiler_params=_cp(("arbitrary",)),
    )(act, w, x, vec)


def post_bwd_matmul(dout, f, vec, w, name, tm=256):
    S, D = dout.shape
    K = w.shape[0]

    def body(dout_ref, f_ref, vec_ref, w_ref, df_ref, dact_ref, acc_ref):
        @pl.when(pl.program_id(0) == 0)
        def _():
            acc_ref[...] = jnp.zeros_like(acc_ref)

        df, dgate, dgpost = _post_bwd(dout_ref[...], f_ref[...], vec_ref, 1.0)
        dfb = df.astype(BF16)
        df_ref[...] = dfb
        dact_ref[...] = _dot_nt(dfb, w_ref[...])
        _acc_add(acc_ref, A_GATE, dgate)
        _acc_add(acc_ref, A_GPOST, dgpost)

    row = pl.BlockSpec((tm, D), lambda i: (i, 0))
    const = pl.BlockSpec((8, D), lambda i: (0, 0))
    return pl.pallas_call(
        body, name=name, grid=(S // tm,),
        in_specs=[row, row, const, pl.BlockSpec((K, D), lambda i: (0, 0))],
        out_specs=[row, pl.BlockSpec((tm, K), lambda i: (i, 0)), const],
        out_shape=[jax.ShapeDtypeStruct((S, D), BF16), jax.ShapeDtypeStruct((S, K), F32),
                   jax.ShapeDtypeStruct((8, D), F32)],
        compiler_params=_cp(("arbitrary",)),
    )(dout, f, vec, w)


def _band(w, transposed, prev):
    r = lax.broadcasted_iota(jnp.int32, (CHUNK, CHUNK), 1 if transposed else 0)
    c = lax.broadcasted_iota(jnp.int32, (CHUNK, CHUNK), 0 if transposed else 1)
    d = r - c
    m = (d + CHUNK < w) if prev else ((d >= 0) & (d < w))
    return jnp.where(m, 1.0, 0.0).astype(BF16)


def _split_hi_lo(a):
    hi = a.astype(BF16)
    lo = (a - hi.astype(F32)).astype(BF16)
    return hi, lo


def _pool_diff(a, ap, g, denom):
    w = POOL_WINDOWS[g]
    a_hi, a_lo = _split_hi_lo(a)
    p_hi, p_lo = _split_hi_lo(ap)
    mc = _band(w, False, False)
    mp = _band(w, False, True)
    win = _dot(mc, a_hi) + _dot(mc, a_lo) + _dot(mp, p_hi) + _dot(mp, p_lo)
    return win / denom - a


def _sgu_norm(v, lg, lb):
    mu = _lanemean(v)
    xc = v - mu
    rstd = lax.rsqrt(_lanemean(xc * xc) + EPS)
    xhat = xc * rstd
    return xhat, rstd, xhat * lg + lb


def _tril_mask():
    r = lax.broadcasted_iota(jnp.int32, (CHUNK, CHUNK), 0)
    c = lax.broadcasted_iota(jnp.int32, (CHUNK, CHUNK), 1)
    return r >= c


def _positions(i, w):
    r = lax.broadcasted_iota(jnp.int32, (CHUNK, 128), 0)
    pos = (i * CHUNK + r + 1).astype(F32)
    return jnp.minimum(pos, float(w))


def mixa_core_fwd(z, pool_w, pool_scale, ln_g, ln_b, sgu_w, sgu_bexp, name):
    S = z.shape[0]
    W = z.shape[1] // 3
    G = len(POOL_WINDOWS)
    GD = W // G

    def body(zc_ref, zp_ref, pw_ref, ps_ref, lg_ref, lb_ref, sw_ref, sb_ref, y_ref):
        i = pl.program_id(0)
        has_prev = jnp.where(i > 0, 1.0, 0.0)
        for g in range(G):
            sl = slice(g * GD, (g + 1) * GD)
            a = zc_ref[:, sl]
            ap = zp_ref[:, sl] * has_prev
            d = _pool_diff(a, ap, g, _positions(i, POOL_WINDOWS[g])).astype(BF16)
            y_ref[:, sl] = (_dot(d, pw_ref[g]) * ps_ref[:, sl]).astype(BF16)
        tril = _tril_mask()
        for hh in range(G):
            u = _gelu(zc_ref[:, W + hh * GD:W + (hh + 1) * GD])
            v = _gelu(zc_ref[:, 2 * W + hh * GD:2 * W + (hh + 1) * GD])
            sl = slice(hh * GD, (hh + 1) * GD)
            _, _, vn = _sgu_norm(v, lg_ref[:, sl], lb_ref[:, sl])
            wm = jnp.where(tril, sw_ref[hh], 0.0).astype(BF16)
            s = _dot(wm, vn.astype(BF16)) + sb_ref[hh]
            y_ref[:, W + hh * GD:W + (hh + 1) * GD] = (u * s).astype(BF16)

    vecw = pl.BlockSpec((1, W), lambda i: (0, 0))
    mats = pl.BlockSpec((G, GD, GD), lambda i: (0, 0, 0))
    return pl.pallas_call(
        body, name=name, grid=(S // CHUNK,),
        in_specs=[pl.BlockSpec((CHUNK, 3 * W), lambda i: (i, 0)),
                  pl.BlockSpec((CHUNK, W), lambda i: (jnp.maximum(i - 1, 0), 0)),
                  mats, vecw, vecw, vecw, mats, mats],
        out_specs=pl.BlockSpec((CHUNK, 2 * W), lambda i: (i, 0)),
        out_shape=jax.ShapeDtypeStruct((S, 2 * W), BF16),
        compiler_params=_cp(("arbitrary",)),
    )(z, z, pool_w, pool_scale, ln_g, ln_b, sgu_w, sgu_bexp)


def mixa_core_bwd(z, dy, pool_w, pool_scale, ln_g, ln_b, sgu_w, sgu_bexp, name):
    S = z.shape[0]
    W = z.shape[1] // 3
    G = len(POOL_WINDOWS)
    GD = W // G
    n_tiles = S // CHUNK

    def body(zc_ref, zp_ref, dyc_ref, dyn_ref, pw_ref, ps_ref, lg_ref, lb_ref, sw_ref, sb_ref,
             dz_ref, dpw_ref, dvec_ref, dsw_ref, dsb_ref):
        i = pl.program_id(0)

        @pl.when(i == 0)
        def _():
            dpw_ref[...] = jnp.zeros_like(dpw_ref)
            dvec_ref[...] = jnp.zeros_like(dvec_ref)
            dsw_ref[...] = jnp.zeros_like(dsw_ref)
            dsb_ref[...] = jnp.zeros_like(dsb_ref)

        has_prev = jnp.where(i > 0, 1.0, 0.0)
        has_next = jnp.where(i < n_tiles - 1, 1.0, 0.0)
        for g in range(G):
            w = POOL_WINDOWS[g]
            sl = slice(g * GD, (g + 1) * GD)
            a = zc_ref[:, sl]
            ap = zp_ref[:, sl] * has_prev
            den_c = _positions(i, w)
            den_n = _positions(i + 1, w)
            d = _pool_diff(a, ap, g, den_c).astype(BF16)
            ps = ps_ref[:, sl]
            pw = pw_ref[g]
            dyc = dyc_ref[:, sl]
            dvec_ref[pl.ds(0, 1), sl] += _rowsum(dyc * _dot(d, pw))
            dyp_c = (dyc * ps).astype(BF16)
            dyp_n = (dyn_ref[:, sl] * (ps * has_next)).astype(BF16)
            dpw_ref[g] += _dot_tn(d, dyp_c)
            dd_c = _dot_nt(dyp_c, pw)
            dd_n = _dot_nt(dyp_n, pw)
            da = (_dot(_band(w, True, False), (dd_c / den_c).astype(BF16))
                  + _dot(_band(w, True, True), (dd_n / den_n).astype(BF16)) - dd_c)
            dz_ref[:, sl] = da.astype(BF16)
        tril = _tril_mask()
        for hh in range(G):
            sl = slice(hh * GD, (hh + 1) * GD)
            zu = zc_ref[:, W + hh * GD:W + (hh + 1) * GD]
            zv = zc_ref[:, 2 * W + hh * GD:2 * W + (hh + 1) * GD]
            u = _gelu(zu)
            v = _gelu(zv)
            lg = lg_ref[:, sl]
            xhat, rstd, vn = _sgu_norm(v, lg, lb_ref[:, sl])
            vnb = vn.astype(BF16)
            wm = jnp.where(tril, sw_ref[hh], 0.0).astype(BF16)
            s = _dot(wm, vnb) + sb_ref[hh]
            dyb = dyc_ref[:, W + hh * GD:W + (hh + 1) * GD]
            du = dyb * s
            ds = dyb * u
            dsb_ref[:, hh:hh + 1] += jnp.sum(ds, axis=1, keepdims=True)
            dsb16 = ds.astype(BF16)
            dsw_ref[hh] += jnp.where(tril, _dot_nt(dsb16, vnb), 0.0)
            dvn = _dot_tn(wm, dsb16)
            dvec_ref[pl.ds(1, 1), sl] += _rowsum(dvn * xhat)
            dvec_ref[pl.ds(2, 1), sl] += _rowsum(dvn)
            dxh = dvn * lg
            dv = rstd * (dxh - _lanemean(dxh) - xhat * _lanemean(dxh * xhat))
            dz_ref[:, W + hh * GD:W + (hh + 1) * GD] = (du * _gelu_grad(zu)).astype(BF16)
            dz_ref[:, 2 * W + hh * GD:2 * W + (hh + 1) * GD] = (dv * _gelu_grad(zv)).astype(BF16)

    vecw = pl.BlockSpec((1, W), lambda i: (0, 0))
    mats = pl.BlockSpec((G, GD, GD), lambda i: (0, 0, 0))
    return pl.pallas_call(
        body, name=name, grid=(n_tiles,),
        in_specs=[pl.BlockSpec((CHUNK, 3 * W), lambda i: (i, 0)),
                  pl.BlockSpec((CHUNK, W), lambda i: (jnp.maximum(i - 1, 0), 0)),
                  pl.BlockSpec((CHUNK, 2 * W), lambda i: (i, 0)),
                  pl.BlockSpec((CHUNK, W), lambda i: (jnp.minimum(i + 1, n_tiles - 1), 0)),
                  mats, vecw, vecw, vecw, mats, mats],
        out_specs=[pl.BlockSpec((CHUNK, 3 * W), lambda i: (i, 0)), mats,
                   pl.BlockSpec((8, W), lambda i: (0, 0)), mats, pl.BlockSpec((CHUNK, G), lambda i: (0, 0))],
        out_shape=[jax.ShapeDtypeStruct((S, 3 * W), BF16), jax.ShapeDtypeStruct((G, GD, GD), F32),
                   jax.ShapeDtypeStruct((8, W), F32), jax.ShapeDtypeStruct((G, CHUNK, CHUNK), F32),
                   jax.ShapeDtypeStruct((CHUNK, G), F32)],
        compiler_params=_cp(("arbitrary",)),
    )(z, z, dy, dy, pool_w, pool_scale, ln_g, ln_b, sgu_w, sgu_bexp)


def _cmul(ar, ai, br, bi):
    return ar * br - ai * bi, ar * bi + ai * br


def _cpow(ar, ai, n):
    rr, ri = None, None
    br, bi = ar, ai
    while n:
        if n & 1:
            rr, ri = (br, bi) if rr is None else _cmul(rr, ri, br, bi)
        n >>= 1
        if n:
            br, bi = _cmul(br, bi, br, bi)
    return rr, ri


def _seg_rows(k):
    return pl.ds(pl.multiple_of(k * SCAN_LANES, SCAN_LANES), SCAN_LANES)


def _scan_fwd(xre, xim, carry, ar, ai, K):
    P = xre.shape[1]
    a8r = jnp.broadcast_to(ar, (SCAN_LANES, P))
    a8i = jnp.broadcast_to(ai, (SCAN_LANES, P))

    def local(k, c):
        pr, pi = c
        rows = _seg_rows(k)
        nr = a8r * pr - a8i * pi + xre[rows, :]
        ni = a8r * pi + a8i * pr + xim[rows, :]
        xre[rows, :] = nr
        xim[rows, :] = ni
        return nr, ni

    er, ei = lax.fori_loop(1, K, local, (xre[pl.ds(0, SCAN_LANES), :], xim[pl.ds(0, SCAN_LANES), :]),
                           unroll=SCAN_UNROLL)
    akr, aki = _cpow(ar, ai, K)
    cr = jnp.zeros((1, P), F32)
    ci = jnp.zeros((1, P), F32)
    carry[pl.ds(0, 1), :] = cr
    carry[pl.ds(SCAN_LANES, 1), :] = ci
    for j in range(1, SCAN_LANES):
        tr, ti = _cmul(akr, aki, cr, ci)
        cr = er[j - 1:j, :] + tr
        ci = ei[j - 1:j, :] + ti
        carry[pl.ds(j, 1), :] = cr
        carry[pl.ds(SCAN_LANES + j, 1), :] = ci
    cmr = carry[pl.ds(0, SCAN_LANES), :]
    cmi = carry[pl.ds(SCAN_LANES, SCAN_LANES), :]

    def fix(k, c):
        pr, pi = c
        rows = _seg_rows(k)
        tr, ti = _cmul(pr, pi, cmr, cmi)
        xre[rows, :] += tr
        xim[rows, :] += ti
        return _cmul(pr, pi, a8r, a8i)

    lax.fori_loop(0, K, fix, (a8r, a8i), unroll=SCAN_UNROLL)


def s5_core_fwd(u, ar, ai, bre, bim, cre_t, cim_t, dskip, name):
    S, D = u.shape
    nblk, UB, PB = bre.shape
    K = S // SCAN_LANES

    def body(u_ref, ar_ref, ai_ref, bre_ref, bim_ref, cre_ref, cim_ref, d_ref, y_ref, xre, xim, carry):
        uv = u_ref[...]
        ub = uv.astype(BF16)
        xre[...] = _dot(ub, bre_ref[...])
        xim[...] = _dot(ub, bim_ref[...])
        _scan_fwd(xre, xim, carry, ar_ref[...], ai_ref[...], K)
        y_ref[...] = (_dot(xre[...].astype(BF16), cre_ref[...]) - _dot(xim[...].astype(BF16), cim_ref[...])
                      + d_ref[...] * uv)

    ucol = pl.BlockSpec((S, UB), lambda i: (0, i))
    pvec = pl.BlockSpec((None, 1, PB), lambda i: (i, 0, 0))
    bmat = pl.BlockSpec((None, UB, PB), lambda i: (i, 0, 0))
    cmat = pl.BlockSpec((None, PB, UB), lambda i: (i, 0, 0))
    return pl.pallas_call(
        body, name=name, grid=(nblk,),
        in_specs=[ucol, pvec, pvec, bmat, bmat, cmat, cmat, pl.BlockSpec((1, UB), lambda i: (0, i))],
        out_specs=ucol,
        out_shape=jax.ShapeDtypeStruct((S, D), F32),
        scratch_shapes=[pltpu.VMEM((S, PB), F32), pltpu.VMEM((S, PB), F32), pltpu.VMEM((2 * SCAN_LANES, PB), F32)],
        compiler_params=_cp(("arbitrary",)),
    )(u, ar, ai, bre, bim, cre_t, cim_t, dskip)


def s5_core_bwd(u, dy, ar, ai, bre, bim, cre, cim, dskip, name):
    S, D = u.shape
    nblk, UB, PB = bre.shape
    K = S // SCAN_LANES

    def body(u_ref, dy_ref, ar_ref, ai_ref, bre_ref, bim_ref, cre_ref, cim_ref, d_ref,
             du_ref, dbre_ref, dbim_ref, dcre_ref, dcim_ref, dar_ref, dai_ref, dd_ref,
             xre, xim, gre, gim, carry, carry_b):
        ar = ar_ref[...]
        ai = ai_ref[...]
        uv = u_ref[...]
        ub = uv.astype(BF16)
        dyv = dy_ref[...]
        dyb = dyv.astype(BF16)
        xre[...] = _dot(ub, bre_ref[...])
        xim[...] = _dot(ub, bim_ref[...])
        _scan_fwd(xre, xim, carry, ar, ai, K)
        dcre_ref[...] = _dot_tn(xre[...].astype(BF16), dyb)
        dcim_ref[...] = -_dot_tn(xim[...].astype(BF16), dyb)
        gre[...] = _dot(dyb, cre_ref[...])
        gim[...] = -_dot(dyb, cim_ref[...])

        a8r = jnp.broadcast_to(ar, (SCAN_LANES, PB))
        a8i = jnp.broadcast_to(ai, (SCAN_LANES, PB))
        na8i = -a8i

        def local(s, c):
            k = K - 2 - s
            nr, ni = c
            rows = _seg_rows(k)
            tr = gre[rows, :] + a8r * nr + a8i * ni
            ti = gim[rows, :] + a8r * ni - a8i * nr
            gre[rows, :] = tr
            gim[rows, :] = ti
            return tr, ti

        last = _seg_rows(K - 1)
        fr, fi = lax.fori_loop(0, K - 1, local, (gre[last, :], gim[last, :]), unroll=SCAN_UNROLL)
        akr, aki = _cpow(ar, -ai, K)
        cr = jnp.zeros((1, PB), F32)
        ci = jnp.zeros((1, PB), F32)
        carry_b[pl.ds(SCAN_LANES - 1, 1), :] = cr
        carry_b[pl.ds(2 * SCAN_LANES - 1, 1), :] = ci
        for j in range(SCAN_LANES - 2, -1, -1):
            tr, ti = _cmul(akr, aki, cr, ci)
            cr = fr[j + 1:j + 2, :] + tr
            ci = fi[j + 1:j + 2, :] + ti
            carry_b[pl.ds(j, 1), :] = cr
            carry_b[pl.ds(SCAN_LANES + j, 1), :] = ci
        cbr = carry_b[pl.ds(0, SCAN_LANES), :]
        cbi = carry_b[pl.ds(SCAN_LANES, SCAN_LANES), :]

        def fix_rows(rows, c, xr, xi):
            pr, pi, dar, dai = c
            tr, ti = _cmul(pr, pi, cbr, cbi)
            g_r = gre[rows, :] + tr
            g_i = gim[rows, :] + ti
            gre[rows, :] = g_r
            gim[rows, :] = g_i
            dar = dar + g_r * xr + g_i * xi
            dai = dai + g_i * xr - g_r * xi
            nr, ni = _cmul(pr, pi, a8r, na8i)
            return nr, ni, dar, dai

        def fix(s, c):
            k = K - 1 - s
            prev = _seg_rows(k - 1)
            return fix_rows(_seg_rows(k), c, xre[prev, :], xim[prev, :])

        z8 = jnp.zeros((SCAN_LANES, PB), F32)
        c = lax.fori_loop(0, K - 1, fix, (a8r, na8i, z8, z8), unroll=SCAN_UNROLL)
        _, _, dar, dai = fix_rows(_seg_rows(0), c, carry[pl.ds(0, SCAN_LANES), :], carry[pl.ds(SCAN_LANES, SCAN_LANES), :])
        dar_ref[...] = _rowsum(dar)
        dai_ref[...] = _rowsum(dai)
        grb = gre[...].astype(BF16)
        gib = gim[...].astype(BF16)
        dbre_ref[...] = _dot_tn(ub, grb)
        dbim_ref[...] = _dot_tn(ub, gib)
        du_ref[...] = _dot_nt(grb, bre_ref[...]) + _dot_nt(gib, bim_ref[...]) + d_ref[...] * dyv
        dd_ref[...] = _rowsum(dyv * uv)

    ucol = pl.BlockSpec((S, UB), lambda i: (0, i))
    pvec = pl.BlockSpec((None, 1, PB), lambda i: (i, 0, 0))
    bmat = pl.BlockSpec((None, UB, PB), lambda i: (i, 0, 0))
    cmat = pl.BlockSpec((None, PB, UB), lambda i: (i, 0, 0))
    dvec = pl.BlockSpec((1, UB), lambda i: (0, i))
    return pl.pallas_call(
        body, name=name, grid=(nblk,),
        in_specs=[ucol, ucol, pvec, pvec, bmat, bmat, bmat, bmat, dvec],
        out_specs=[ucol, bmat, bmat, cmat, cmat, pvec, pvec, dvec],
        out_shape=[jax.ShapeDtypeStruct((S, D), F32),
                   jax.ShapeDtypeStruct((nblk, UB, PB), F32), jax.ShapeDtypeStruct((nblk, UB, PB), F32),
                   jax.ShapeDtypeStruct((nblk, PB, UB), F32), jax.ShapeDtypeStruct((nblk, PB, UB), F32),
                   jax.ShapeDtypeStruct((nblk, 1, PB), F32), jax.ShapeDtypeStruct((nblk, 1, PB), F32),
                   jax.ShapeDtypeStruct((1, D), F32)],
        scratch_shapes=[pltpu.VMEM((S, PB), F32), pltpu.VMEM((S, PB), F32), pltpu.VMEM((S, PB), F32),
                        pltpu.VMEM((S, PB), F32), pltpu.VMEM((2 * SCAN_LANES, PB), F32),
                        pltpu.VMEM((2 * SCAN_LANES, PB), F32)],
        compiler_params=_cp(("arbitrary",)),
    )(u, dy, ar, ai, bre, bim, cre, cim, dskip)


def s5_glu_fwd(y, wglu, x, vec, name, tm=256):
    S, D = x.shape
    NJ = wglu.shape[-1]

    def body(y_ref, w_ref, x_ref, vec_ref, out_ref, f_ref):
        g = _gelu(y_ref[...]).astype(BF16)
        f = jnp.concatenate([_dot(g, w_ref[j]) * _sigmoid(_dot(g, w_ref[2 + j])) for j in range(2)], axis=1)
        f_ref[...] = f
        out_ref[...] = _post_fwd(x_ref[...], f, vec_ref, 1.0)

    row = pl.BlockSpec((tm, D), lambda i: (i, 0))
    return pl.pallas_call(
        body, name=name, grid=(S // tm,),
        in_specs=[row, pl.BlockSpec((N_CHIPS, D, NJ), lambda i: (0, 0, 0)), row, pl.BlockSpec((8, D), lambda i: (0, 0))],
        out_specs=[row, row],
        out_shape=[jax.ShapeDtypeStruct((S, D), F32), jax.ShapeDtypeStruct((S, D), F32)],
        compiler_params=_cp(("arbitrary",)),
    )(y, wglu, x, vec)


def s5_glu_bwd(dout, f, y, vec, wglu, name, tm=256):
    S, D = dout.shape
    NJ = wglu.shape[-1]

    def body(dout_ref, f_ref, y_ref, vec_ref, w_ref, dy_ref, dab_ref, g_ref, acc_ref):
        @pl.when(pl.program_id(0) == 0)
        def _():
            acc_ref[...] = jnp.zeros_like(acc_ref)

        df, dgate, dgpost = _post_bwd(dout_ref[...], f_ref[...], vec_ref, 1.0)
        yv = y_ref[...]
        g = _gelu(yv).astype(BF16)
        g_ref[...] = g
        dg = jnp.zeros((tm, D), F32)
        for j in range(2):
            a = _dot(g, w_ref[j])
            sig = _sigmoid(_dot(g, w_ref[2 + j]))
            dfj = df[:, j * NJ:(j + 1) * NJ]
            da = (dfj * sig).astype(BF16)
            db = (dfj * a * sig * (1.0 - sig)).astype(BF16)
            dab_ref[:, j * NJ:(j + 1) * NJ] = da
            dab_ref[:, (2 + j) * NJ:(3 + j) * NJ] = db
            dg = dg + _dot_nt(da, w_ref[j]) + _dot_nt(db, w_ref[2 + j])
        dy_ref[...] = dg * _gelu_grad(yv)
        _acc_add(acc_ref, A_GATE, dgate)
        _acc_add(acc_ref, A_GPOST, dgpost)

    row = pl.BlockSpec((tm, D), lambda i: (i, 0))
    const = pl.BlockSpec((8, D), lambda i: (0, 0))
    return pl.pallas_call(
        body, name=name, grid=(S // tm,),
        in_specs=[row, row, row, const, pl.BlockSpec((N_CHIPS, D, NJ), lambda i: (0, 0, 0))],
        out_specs=[row, pl.BlockSpec((tm, N_CHIPS * NJ), lambda i: (i, 0)), row, const],
        out_shape=[jax.ShapeDtypeStruct((S, D), F32), jax.ShapeDtypeStruct((S, N_CHIPS * NJ), BF16),
                   jax.ShapeDtypeStruct((S, D), BF16), jax.ShapeDtypeStruct((8, D), F32)],
        compiler_params=_cp(("arbitrary",)),
    )(dout, f, y, vec, wglu)


def loss_head(y, target, name, tm=256):
    S, D = y.shape

    def body(y_ref, t_ref, d_ref, l_ref):
        @pl.when(pl.program_id(0) == 0)
        def _():
            l_ref[...] = jnp.zeros_like(l_ref)

        err = y_ref[...] - t_ref[...]
        d_ref[...] = err * (1.0 / D)
        l_ref[...] += jnp.sum(_rowsum(err * err), axis=1, keepdims=True)

    row = pl.BlockSpec((tm, D), lambda i: (i, 0))
    return pl.pallas_call(
        body, name=name, grid=(S // tm,),
        in_specs=[row, row],
        out_specs=[row, pl.BlockSpec((1, 1), lambda i: (0, 0))],
        out_shape=[jax.ShapeDtypeStruct((S, D), F32), jax.ShapeDtypeStruct((1, 1), F32)],
        compiler_params=_cp(("arbitrary",)),
    )(y, target)


def zero_after(dep, name, shape=(8, 128)):
    def body(dep_ref, o_ref):
        o_ref[...] = jnp.zeros_like(o_ref)

    return pl.pallas_call(body, name=name, in_specs=[ANY], out_specs=pl.BlockSpec(memory_space=pltpu.VMEM),
                          out_shape=jax.ShapeDtypeStruct(shape, F32), compiler_params=_cp())(dep)


def ada_mod(c_all, ada_w, ada_b_shard, name, tn=768, dep=None):
    B, D = c_all.shape
    L, _, NS = ada_w.shape
    extra = [] if dep is None else [dep]

    def body(c_ref, w_ref, b_ref, *rest):
        cv = c_ref[...]
        cond = (cv * _sigmoid(cv)).astype(BF16)
        rest[-1][...] = _dot(cond, w_ref[...].astype(BF16)) + b_ref[...]

    return pl.pallas_call(
        body, name=name, grid=(L, NS // tn),
        in_specs=[pl.BlockSpec((B, D), lambda l, j: (0, 0)), pl.BlockSpec((None, D, tn), lambda l, j: (l, 0, j)),
                  pl.BlockSpec((None, 1, tn), lambda l, j: (l, 0, j))] + [ANY] * len(extra),
        out_specs=pl.BlockSpec((None, B, tn), lambda l, j: (l, 0, j)),
        out_shape=jax.ShapeDtypeStruct((L, B, NS), F32),
        compiler_params=_cp(("arbitrary", "arbitrary")),
    )(c_all, ada_w, ada_b_shard, *extra)


def ada_grad(c_all, dmod, name, tn=768):
    B, D = c_all.shape
    L, _, NS = dmod.shape

    def body(c_ref, d_ref, o_ref):
        cv = c_ref[...]
        cond = (cv * _sigmoid(cv)).astype(BF16)
        o_ref[...] = _dot_tn(cond, d_ref[...].astype(BF16))

    return pl.pallas_call(
        body, name=name, grid=(L, NS // tn),
        in_specs=[pl.BlockSpec((B, D), lambda l, j: (0, 0)), pl.BlockSpec((None, B, tn), lambda l, j: (l, 0, j))],
        out_specs=pl.BlockSpec((None, D, tn), lambda l, j: (l, 0, j)),
        out_shape=jax.ShapeDtypeStruct((L, D, NS), F32),
        compiler_params=_cp(("arbitrary", "arbitrary")),
    )(c_all, dmod)


def sum_leading(x, name):
    n, R, C = x.shape

    def body(x_ref, o_ref):
        acc = x_ref[0].astype(F32)
        for i in range(1, n):
            acc = acc + x_ref[i].astype(F32)
        o_ref[...] = acc

    tr = _row_tile(R, 64)
    return pl.pallas_call(
        body, name=name, grid=(R // tr,),
        in_specs=[pl.BlockSpec((n, tr, C), lambda i: (0, i, 0))],
        out_specs=pl.BlockSpec((tr, C), lambda i: (i, 0)),
        out_shape=jax.ShapeDtypeStruct((R, C), F32),
        compiler_params=_cp(("arbitrary",)),
    )(x)


def sum_gathered(land, own, me_arr, name):
    n, R, C = land.shape

    def body(me_ref, land_ref, own_ref, o_ref):
        acc = jnp.zeros(o_ref.shape, F32)
        for i in range(n):
            acc = acc + jnp.where(me_ref[0] == i, own_ref[...], land_ref[i]).astype(F32)
        o_ref[...] = acc

    tr = _row_tile(R, 64)
    grid_spec = pltpu.PrefetchScalarGridSpec(
        num_scalar_prefetch=1, grid=(R // tr,),
        in_specs=[pl.BlockSpec((n, tr, C), lambda i, me: (0, i, 0)), pl.BlockSpec((tr, C), lambda i, me: (i, 0))],
        out_specs=pl.BlockSpec((tr, C), lambda i, me: (i, 0)))
    return pl.pallas_call(
        body, name=name, grid_spec=grid_spec,
        out_shape=jax.ShapeDtypeStruct((R, C), F32),
        compiler_params=_cp(("arbitrary",)),
    )(me_arr, land, own)


def _row_tile(R, cap=512):
    if R <= cap:
        return R
    for cand in (512, 384, 352, 256, 128, 64, 32, 16, 8):
        if cand <= cap and R % cand == 0:
            return cand
    return R


def adamw(w, g, m, v, name):
    R, C = w.shape
    tr = _row_tile(R, 256 if C > 1024 else 512)
    bc1 = 1.0 - ADAM_B1 ** ADAM_STEP
    bc2 = 1.0 - ADAM_B2 ** ADAM_STEP

    def body(w_ref, g_ref, m_ref, v_ref, d_ref, nm_ref, nv_ref):
        gv = g_ref[...]
        nm = ADAM_B1 * m_ref[...] + (1.0 - ADAM_B1) * gv
        nv = ADAM_B2 * v_ref[...] + (1.0 - ADAM_B2) * (gv * gv)
        nm_ref[...] = nm
        nv_ref[...] = nv
        d_ref[...] = -ADAM_LR * ((nm / bc1) / (jnp.sqrt(nv / bc2) + ADAM_EPS) + ADAM_WD * w_ref[...])

    blk = pl.BlockSpec((tr, C), lambda i: (i, 0))
    sd = jax.ShapeDtypeStruct((R, C), F32)
    return pl.pallas_call(
        body, name=name, grid=(R // tr,),
        in_specs=[blk, blk, blk, blk], out_specs=[blk, blk, blk], out_shape=[sd, sd, sd],
        compiler_params=_cp(("arbitrary",)),
    )(w, g, m, v)


def _as2d(a):
    if a.ndim == 1:
        return a.reshape(1, -1)
    return a.reshape(-1, a.shape[-1])


SMALL_PARAM_ELEMS = 1 << 18


def adamw_nd(w, g, m, v, name):
    if w.size <= SMALL_PARAM_ELEMS:
        bc1 = 1.0 - ADAM_B1 ** ADAM_STEP
        bc2 = 1.0 - ADAM_B2 ** ADAM_STEP

        def body(w_ref, g_ref, m_ref, v_ref, d_ref, nm_ref, nv_ref):
            gv = g_ref[...]
            nm = ADAM_B1 * m_ref[...] + (1.0 - ADAM_B1) * gv
            nv = ADAM_B2 * v_ref[...] + (1.0 - ADAM_B2) * (gv * gv)
            nm_ref[...] = nm
            nv_ref[...] = nv
            d_ref[...] = -ADAM_LR * ((nm / bc1) / (jnp.sqrt(nv / bc2) + ADAM_EPS) + ADAM_WD * w_ref[...])

        sd = jax.ShapeDtypeStruct(w.shape, F32)
        whole = pl.BlockSpec(memory_space=pltpu.VMEM)
        return tuple(pl.pallas_call(body, name=name, in_specs=[whole] * 4, out_specs=[whole] * 3,
                                    out_shape=[sd, sd, sd], compiler_params=_cp())(w, g.reshape(w.shape), m, v))
    outs = adamw(_as2d(w), _as2d(g.reshape(w.shape)), _as2d(m), _as2d(v), name)
    return tuple(o.reshape(w.shape) for o in outs)


def _place():
    x, y, c = lax.axis_index("x"), lax.axis_index("y"), lax.axis_index("c")
    chips = [(1 - x, y), (x, 1 - y), (1 - x, 1 - y)]
    return x, y, c, chips


def allgather_small(xs, name):
    m_per, n = xs.shape

    def body(x_ref, out_ref, send_sems, recv_sems, local_sem):
        x, y, c, chips = _place()
        me, sibling = (x, y, c), (x, y, 1 - c)

        def rows(px, py, pc):
            return out_ref.at[pl.ds((4 * px + 2 * py + pc) * m_per, m_per), :]

        def copy(k, block, to, src=None):
            return pltpu.make_async_remote_copy(
                src_ref=rows(*block) if src is None else src, dst_ref=rows(*block),
                send_sem=send_sems.at[k], recv_sem=recv_sems.at[k], device_id=to, device_id_type=MESH_T)

        mine = pltpu.make_async_copy(x_ref, rows(*me), local_sem)
        mine.start()
        first = [copy(0, me, sibling, src=x_ref)]
        first += [copy(1 + j, me, (*chip, c), src=x_ref) for j, chip in enumerate(chips)]
        for cp in first:
            cp.start()
        passed = [copy(4 + j, (*chip, c), sibling) for j, chip in enumerate(chips)]
        for j, chip in enumerate(chips):
            copy(1 + j, (*chip, c), me).wait_recv()
            passed[j].start()
        copy(0, sibling, me).wait_recv()
        for j, chip in enumerate(chips):
            copy(4 + j, (*chip, 1 - c), me).wait_recv()
        for cp in first + passed:
            cp.wait_send()
        mine.wait()

    return pl.pallas_call(
        body, name=name,
        out_shape=jax.ShapeDtypeStruct((N_DEV * m_per, n), xs.dtype),
        in_specs=[pl.BlockSpec(memory_space=pltpu.VMEM)],
        out_specs=pl.BlockSpec(memory_space=pltpu.VMEM),
        scratch_shapes=[pltpu.SemaphoreType.DMA((7,)), pltpu.SemaphoreType.DMA((7,)), pltpu.SemaphoreType.DMA],
        compiler_params=_cp(),
    )(xs)


def gather_weights(shards, name):
    n = len(shards)

    def body(*refs):
        ins, outs = refs[:n], refs[n:2 * n]
        send_sems, recv_sems, fsend_sems, frecv_sems, local_sems = refs[2 * n:]
        x, y, c, chips = _place()
        me_chip = 2 * x + y
        sibling = (x, y, 1 - c)
        locs = []
        for a in range(n):
            cp = pltpu.make_async_copy(ins[a], outs[a].at[me_chip], local_sems.at[a])
            cp.start()
            locs.append(cp)

        def ici(a, k, src_chip, to):
            return pltpu.make_async_remote_copy(
                src_ref=ins[a].at[c], dst_ref=outs[a].at[src_chip, c],
                send_sem=send_sems.at[a * 3 + k], recv_sem=recv_sems.at[a * 3 + k], device_id=to, device_id_type=MESH_T)

        def d2d(a, k, src_chip, half):
            return pltpu.make_async_remote_copy(
                src_ref=outs[a].at[src_chip, half], dst_ref=outs[a].at[src_chip, half],
                send_sem=fsend_sems.at[a * 3 + k], recv_sem=frecv_sems.at[a * 3 + k],
                device_id=sibling, device_id_type=MESH_T)

        firsts = []
        for k, (cx, cy) in enumerate(chips):
            for a in range(n):
                cp = ici(a, k, me_chip, (cx, cy, c))
                cp.start()
                firsts.append(cp)
        passed = []
        for k, (cx, cy) in enumerate(chips):
            for a in range(n):
                ici(a, k, 2 * cx + cy, (cx, cy, c)).wait_recv()
                cp = d2d(a, k, 2 * cx + cy, c)
                cp.start()
                passed.append(cp)
        for k, (cx, cy) in enumerate(chips):
            for a in range(n):
                d2d(a, k, 2 * cx + cy, 1 - c).wait_recv()
        for cp in firsts + passed:
            cp.wait_send()
        for cp in locs:
            cp.wait()

    return pl.pallas_call(
        body, name=name,
        out_shape=[jax.ShapeDtypeStruct((N_CHIPS,) + s.shape, s.dtype) for s in shards],
        in_specs=[ANY] * n, out_specs=[ANY] * n,
        scratch_shapes=[pltpu.SemaphoreType.DMA((3 * n,)), pltpu.SemaphoreType.DMA((3 * n,)),
                        pltpu.SemaphoreType.DMA((3 * n,)), pltpu.SemaphoreType.DMA((3 * n,)),
                        pltpu.SemaphoreType.DMA((n,))],
        compiler_params=_cp(),
    )(*shards)


def _half_rows(ref, half, rh):
    idx = (slice(None),) * (len(ref.shape) - 2) + (pl.ds(pl.multiple_of(half * rh, 16), rh), slice(None))
    return ref.at[idx]


def sibling_exchange_halves(grads, name):
    n = len(grads)

    def body(*refs):
        ins, outs = refs[:n], refs[n:2 * n]
        send_sems, recv_sems = refs[2 * n:]
        x, y, c, _ = _place()
        cps = []
        for a in range(n):
            rh = ins[a].shape[-2] // 2
            cp = pltpu.make_async_remote_copy(
                src_ref=_half_rows(ins[a], 1 - c, rh), dst_ref=outs[a],
                send_sem=send_sems.at[a], recv_sem=recv_sems.at[a], device_id=(x, y, 1 - c), device_id_type=MESH_T)
            cp.start()
            cps.append(cp)
        for cp in cps:
            cp.wait()

    return pl.pallas_call(
        body, name=name,
        out_shape=[jax.ShapeDtypeStruct(g.shape[:-2] + (g.shape[-2] // 2, g.shape[-1]), g.dtype) for g in grads],
        in_specs=[ANY] * n, out_specs=[ANY] * n,
        scratch_shapes=[pltpu.SemaphoreType.DMA((n,)), pltpu.SemaphoreType.DMA((n,))],
        compiler_params=_cp(),
    )(*grads)


def pair_sum(g, recv, cidx, name):
    L, NS, R, C = g.shape
    rh = R // 2
    tr = _row_tile(rh, 256 if C > 1024 else 512)
    nt = rh // tr

    def body(c_ref, g_ref, r_ref, o_ref):
        o_ref[...] = (g_ref[...].astype(F32) + r_ref[...].astype(F32)).astype(BF16)

    grid_spec = pltpu.PrefetchScalarGridSpec(
        num_scalar_prefetch=1, grid=(L, NS, nt),
        in_specs=[pl.BlockSpec((None, None, tr, C), lambda l, s, t, c: (l, s, c[0] * nt + t, 0)),
                  pl.BlockSpec((None, None, tr, C), lambda l, s, t, c: (l, s, t, 0))],
        out_specs=pl.BlockSpec((None, None, tr, C), lambda l, s, t, c: (l, s, t, 0)))
    return pl.pallas_call(
        body, name=name, grid_spec=grid_spec,
        out_shape=jax.ShapeDtypeStruct((L, NS, rh, C), BF16),
        compiler_params=_cp(("arbitrary", "arbitrary", "arbitrary")),
    )(cidx, g, recv)


def chip_exchange(parts, name):
    n = len(parts)

    def body(*refs):
        ins, outs = refs[:n], refs[n:2 * n]
        send_sems, recv_sems = refs[2 * n:]
        x, y, c, chips = _place()
        cps = []
        for k, (cx, cy) in enumerate(chips):
            for a in range(n):
                cp = pltpu.make_async_remote_copy(
                    src_ref=ins[a].at[:, 2 * cx + cy], dst_ref=outs[a].at[k],
                    send_sem=send_sems.at[a * 3 + k], recv_sem=recv_sems.at[a * 3 + k],
                    device_id=(cx, cy, c), device_id_type=MESH_T)
                cp.start()
                cps.append(cp)
        for cp in cps:
            cp.wait()

    return pl.pallas_call(
        body, name=name,
        out_shape=[jax.ShapeDtypeStruct((3, p.shape[0]) + p.shape[2:], p.dtype) for p in parts],
        in_specs=[ANY] * n, out_specs=[ANY] * n,
        scratch_shapes=[pltpu.SemaphoreType.DMA((3 * n,)), pltpu.SemaphoreType.DMA((3 * n,))],
        compiler_params=_cp(),
    )(*parts)


def chip_sum(part, recv, chip_c, name, dest=None, slot=0, n_slots=1):
    _, NS, RH, C = part.shape
    tr = _row_tile(RH, 256 if C > 1024 else 512)
    nt = RH // tr

    def body(cc_ref, p_ref, r_ref, *rest):
        o_ref = rest[-1]
        acc = p_ref[...].astype(F32)
        for k in range(3):
            acc = acc + r_ref[k].astype(F32)
        o_ref[...] = acc

    in_specs = [pl.BlockSpec((None, None, tr, C), lambda t, cc: (0, cc[0], t, 0)),
                pl.BlockSpec((3, None, tr, C), lambda t, cc: (0, 0, t, 0))]
    args = [chip_c, part, recv]
    aliases = {}
    if dest is not None:
        in_specs.append(ANY)
        args.append(dest)
        aliases = {3: 0}
    grid_spec = pltpu.PrefetchScalarGridSpec(
        num_scalar_prefetch=1, grid=(nt,), in_specs=in_specs,
        out_specs=pl.BlockSpec((None, tr, C), lambda t, cc: (slot, cc[1] * nt + t, 0)))
    return pl.pallas_call(
        body, name=name, grid_spec=grid_spec,
        out_shape=jax.ShapeDtypeStruct((n_slots, 2 * RH, C), F32),
        input_output_aliases=aliases,
        compiler_params=_cp(("arbitrary",)),
    )(*args)


def sibling_fill_halves(bufs, name):
    n = len(bufs)

    def body(*refs):
        outs = refs[n:2 * n]
        send_sems, recv_sems = refs[2 * n:]
        x, y, c, _ = _place()
        cps = []
        for a in range(n):
            rh = outs[a].shape[-2] // 2
            cp = pltpu.make_async_remote_copy(
                src_ref=_half_rows(outs[a], c, rh), dst_ref=_half_rows(outs[a], c, rh),
                send_sem=send_sems.at[a], recv_sem=recv_sems.at[a], device_id=(x, y, 1 - c), device_id_type=MESH_T)
            cp.start()
            cps.append(cp)
        for a, cp in enumerate(cps):
            cp.wait_send()
        for a in range(n):
            rh = outs[a].shape[-2] // 2
            pltpu.make_async_remote_copy(
                src_ref=_half_rows(outs[a], 1 - c, rh), dst_ref=_half_rows(outs[a], 1 - c, rh),
                send_sem=send_sems.at[a], recv_sem=recv_sems.at[a], device_id=(x, y, 1 - c),
                device_id_type=MESH_T).wait_recv()

    return pl.pallas_call(
        body, name=name,
        out_shape=[jax.ShapeDtypeStruct(b.shape, b.dtype) for b in bufs],
        in_specs=[ANY] * n, out_specs=[ANY] * n,
        input_output_aliases={a: a for a in range(n)},
        scratch_shapes=[pltpu.SemaphoreType.DMA((n,)), pltpu.SemaphoreType.DMA((n,))],
        compiler_params=_cp(),
    )(*bufs)


HBM_SPEC = pl.BlockSpec(memory_space=pltpu.HBM)
SEM_SPEC = pl.BlockSpec(memory_space=pltpu.SEMAPHORE)
DATAFLOW_EFFECT = pltpu.SideEffectType.DATAFLOW_SIDE_EFFECTING


def split_start(bufs, copies_fn, n_copies, name, dep=None, token_shape=(8, 128)):
    nb = len(bufs)
    extra = [] if dep is None else [dep]

    def body(*refs):
        ne = len(extra)
        send_sems, recv_sems = refs[nb + ne], refs[nb + ne + 1]
        token = refs[-1]
        for cp in copies_fn(refs[:nb], send_sems, recv_sems):
            cp.start()
        token[...] = jnp.zeros_like(token)

    outs = pl.pallas_call(
        body, name=name,
        out_shape=(pltpu.SemaphoreType.DMA((n_copies,)), pltpu.SemaphoreType.DMA((n_copies,)),
                   *[pltpu.HBM(b.shape, b.dtype) for b in bufs], jax.ShapeDtypeStruct(token_shape, F32)),
        in_specs=[HBM_SPEC] * nb + [ANY] * len(extra),
        out_specs=(SEM_SPEC, SEM_SPEC, *[HBM_SPEC] * nb, pl.BlockSpec(memory_space=pltpu.VMEM)),
        input_output_aliases={i: 2 + i for i in range(nb)},
        compiler_params=pltpu.CompilerParams(has_side_effects=DATAFLOW_EFFECT),
    )(*[pltpu.with_memory_space_constraint(b, pltpu.HBM) for b in bufs], *extra)
    return outs[0], outs[1], list(outs[2:2 + nb]), outs[-1]


def split_wait(send_sems, recv_sems, bufs, after, copies_fn, name):
    nb = len(bufs)

    def body(*refs):
        for cp in copies_fn(refs[:nb], refs[nb], refs[nb + 1]):
            cp.wait_send()
            cp.wait_recv()

    outs = pl.pallas_call(
        body, name=name,
        out_shape=tuple(pltpu.HBM(b.shape, b.dtype) for b in bufs),
        in_specs=[HBM_SPEC] * nb + [SEM_SPEC, SEM_SPEC, ANY],
        out_specs=tuple([HBM_SPEC] * nb),
        input_output_aliases={i: i for i in range(nb)},
        compiler_params=pltpu.CompilerParams(has_side_effects=DATAFLOW_EFFECT),
    )(*bufs, send_sems, recv_sems, after)
    return list(outs)


def chip_exchange_copies(refs, send_sems, recv_sems):
    n = len(refs) // 2
    x, y, c, chips = _place()
    cps = []
    for a in range(n):
        for k, (cx, cy) in enumerate(chips):
            cps.append(pltpu.make_async_remote_copy(
                src_ref=refs[a].at[:, 2 * cx + cy], dst_ref=refs[n + a].at[k],
                send_sem=send_sems.at[a * 3 + k], recv_sem=recv_sems.at[a * 3 + k],
                device_id=(cx, cy, c), device_id_type=MESH_T))
    return cps


def allgather_direct_copies(refs, send_sems, recv_sems):
    x, y, c, _ = _place()
    me = 4 * x + 2 * y + c
    cps = []
    for r in range(1, N_DEV):
        fx, fy, fc = (r >> 2) & 1, (r >> 1) & 1, r & 1
        peer = (1 - x if fx else x, 1 - y if fy else y, 1 - c if fc else c)
        cps.append(pltpu.make_async_remote_copy(
            src_ref=refs[0], dst_ref=refs[1].at[me], send_sem=send_sems.at[r - 1], recv_sem=recv_sems.at[r - 1],
            device_id=peer, device_id_type=MESH_T))
    return cps


def pair_exchange_copies(refs, send_sems, recv_sems):
    n = len(refs) // 2
    x, y, c, _ = _place()
    return [pltpu.make_async_remote_copy(
        src_ref=_half_rows(refs[a], 1 - c, refs[a].shape[-2] // 2), dst_ref=refs[n + a],
        send_sem=send_sems.at[a], recv_sem=recv_sems.at[a], device_id=(x, y, 1 - c), device_id_type=MESH_T)
        for a in range(n)]


def weight_ici_copies(refs, send_sems, recv_sems):
    x, y, c, chips = _place()
    cps = []
    for a in range(len(refs)):
        for k, (cx, cy) in enumerate(chips):
            cps.append(pltpu.make_async_remote_copy(
                src_ref=refs[a].at[2 * x + y, c], dst_ref=refs[a].at[2 * x + y, c],
                send_sem=send_sems.at[a * 3 + k], recv_sem=recv_sems.at[a * 3 + k],
                device_id=(cx, cy, c), device_id_type=MESH_T))
    return cps


def weight_d2d_copies(refs, send_sems, recv_sems):
    x, y, c, chips = _place()
    cps = []
    for a in range(len(refs)):
        for k, (cx, cy) in enumerate(chips):
            cps.append(pltpu.make_async_remote_copy(
                src_ref=refs[a].at[2 * cx + cy, c], dst_ref=refs[a].at[2 * cx + cy, c],
                send_sem=send_sems.at[a * 3 + k], recv_sem=recv_sems.at[a * 3 + k],
                device_id=(x, y, 1 - c), device_id_type=MESH_T))
    return cps


def cast_place(w, lead, chip_arr, name, dep=None):
    R, C = w.shape[-2:]
    tr = _row_tile(R, 256 if C > 1024 else 512)

    def body(chip_ref, w_ref, *rest):
        rest[-1][...] = w_ref[...].astype(BF16)

    in_specs = [pl.BlockSpec((None,) * len(lead) + (tr, C), lambda t, ch: (*lead, t, 0))]
    args = [chip_arr, w]
    if dep is not None:
        in_specs.append(ANY)
        args.append(dep)
    grid_spec = pltpu.PrefetchScalarGridSpec(
        num_scalar_prefetch=1, grid=(R // tr,), in_specs=in_specs,
        out_specs=pl.BlockSpec((None, tr, C), lambda t, ch: (ch[0], t, 0)))
    return pl.pallas_call(
        body, name=name, grid_spec=grid_spec,
        out_shape=jax.ShapeDtypeStruct((N_CHIPS, R, C), BF16),
        compiler_params=_cp(("arbitrary",)),
    )(*args)


def _s5_prepare(lam_re, lam_im, log_dt, b_re, b_im, c_re, c_im, groups_per_block):
    G, P = lam_re.shape
    N = b_re.shape[-1]
    gb = groups_per_block
    nblk = G // gb
    dt = jnp.exp(log_dt)[:, None]
    e = jnp.exp(lam_re * dt)
    a_re = e * jnp.cos(lam_im * dt)
    a_im = e * jnp.sin(lam_im * dt)
    n2 = lam_re * lam_re + lam_im * lam_im
    co_re = ((a_re - 1.0) * lam_re + a_im * lam_im) / n2
    co_im = (a_im * lam_re - (a_re - 1.0) * lam_im) / n2
    bb_re = co_re[..., None] * b_re - co_im[..., None] * b_im
    bb_im = co_re[..., None] * b_im + co_im[..., None] * b_re
    eye = jnp.eye(gb, dtype=F32)

    def blockdiag_np(m):
        m = m.reshape(nblk, gb, N, P)
        return jnp.einsum('bgnp,gh->bgnhp', m, eye).reshape(nblk, gb * N, gb * P)

    b_np_re = jnp.swapaxes(bb_re, 1, 2)
    b_np_im = jnp.swapaxes(bb_im, 1, 2)
    return (a_re.reshape(nblk, 1, gb * P), a_im.reshape(nblk, 1, gb * P),
            blockdiag_np(b_np_re), blockdiag_np(b_np_im), blockdiag_np(c_re), blockdiag_np(c_im))


def _to_scan_order(a):
    S, D = a.shape
    return a.reshape(SCAN_LANES, S // SCAN_LANES, D).transpose(1, 0, 2).reshape(S, D)


def _from_scan_order(a):
    S, D = a.shape
    return a.reshape(S // SCAN_LANES, SCAN_LANES, D).transpose(1, 0, 2).reshape(S, D)


def _pad_rows(a, mult=8):
    r = (-a.shape[0]) % mult
    if r:
        a = jnp.concatenate([a, jnp.zeros((r, a.shape[1]), a.dtype)], axis=0)
    return a


def _pack_rows(arrs, width):
    parts, offs, o = [], [], 0
    for a in arrs:
        flat = a.reshape(-1)
        r = (-flat.shape[0]) % (16 * width)
        if r:
            flat = jnp.concatenate([flat, jnp.zeros((r,), flat.dtype)])
        p = flat.reshape(-1, width)
        parts.append(p)
        offs.append((o, a.shape, a.size))
        o += p.shape[0]
    if o % 64:
        parts.append(jnp.zeros((64 - o % 64, width), parts[0].dtype))
    return jnp.concatenate(parts, axis=0), offs


def _unpack_rows(packed, offs):
    outs = []
    for o, shape, size in offs:
        rows = -(-size // packed.shape[1])
        outs.append(packed[o:o + rows].reshape(-1)[:size].reshape(shape))
    return outs


def kernel(x, c, ada_w, ada_b, norm_pre, norm_post, ffn_w_in, ffn_w_out, ab_w_in, pool_w, pool_scale, sgu_ln_g, sgu_ln_b, sgu_w, sgu_b, ab_w_out, ssm_w_in, ssm_lam_re, ssm_lam_im, ssm_b_re, ssm_b_im, ssm_c_re, ssm_c_im, ssm_d, ssm_log_dt, ssm_w_glu, loss_target, m_ada_w, m_ada_b, m_norm_pre, m_norm_post, m_ffn_w_in, m_ffn_w_out, m_ab_w_in, m_pool_w, m_pool_scale, m_sgu_ln_g, m_sgu_ln_b, m_sgu_w, m_sgu_b, m_ab_w_out, m_ssm_w_in, m_ssm_lam_re, m_ssm_lam_im, m_ssm_b_re, m_ssm_b_im, m_ssm_c_re, m_ssm_c_im, m_ssm_d, m_ssm_log_dt, m_ssm_w_glu, v_ada_w, v_ada_b, v_norm_pre, v_norm_post, v_ffn_w_in, v_ffn_w_out, v_ab_w_in, v_pool_w, v_pool_scale, v_sgu_ln_g, v_sgu_ln_b, v_sgu_w, v_sgu_b, v_ab_w_out, v_ssm_w_in, v_ssm_lam_re, v_ssm_lam_im, v_ssm_b_re, v_ssm_b_im, v_ssm_c_re, v_ssm_c_im, v_ssm_d, v_ssm_log_dt, v_ssm_w_glu):
    weights = dict(ada_w=ada_w, ada_b=ada_b, norm_pre=norm_pre, norm_post=norm_post, ffn_w_in=ffn_w_in,
                   ffn_w_out=ffn_w_out, ab_w_in=ab_w_in, pool_w=pool_w, pool_scale=pool_scale, sgu_ln_g=sgu_ln_g,
                   sgu_ln_b=sgu_ln_b, sgu_w=sgu_w, sgu_b=sgu_b, ab_w_out=ab_w_out, ssm_w_in=ssm_w_in,
                   ssm_lam_re=ssm_lam_re, ssm_lam_im=ssm_lam_im, ssm_b_re=ssm_b_re, ssm_b_im=ssm_b_im,
                   ssm_c_re=ssm_c_re, ssm_c_im=ssm_c_im, ssm_d=ssm_d, ssm_log_dt=ssm_log_dt, ssm_w_glu=ssm_w_glu)
    m_in = dict(ada_w=m_ada_w, ada_b=m_ada_b, norm_pre=m_norm_pre, norm_post=m_norm_post, ffn_w_in=m_ffn_w_in,
                ffn_w_out=m_ffn_w_out, ab_w_in=m_ab_w_in, pool_w=m_pool_w, pool_scale=m_pool_scale,
                sgu_ln_g=m_sgu_ln_g, sgu_ln_b=m_sgu_ln_b, sgu_w=m_sgu_w, sgu_b=m_sgu_b, ab_w_out=m_ab_w_out,
                ssm_w_in=m_ssm_w_in, ssm_lam_re=m_ssm_lam_re, ssm_lam_im=m_ssm_lam_im, ssm_b_re=m_ssm_b_re,
                ssm_b_im=m_ssm_b_im, ssm_c_re=m_ssm_c_re, ssm_c_im=m_ssm_c_im, ssm_d=m_ssm_d,
                ssm_log_dt=m_ssm_log_dt, ssm_w_glu=m_ssm_w_glu)
    v_in = dict(ada_w=v_ada_w, ada_b=v_ada_b, norm_pre=v_norm_pre, norm_post=v_norm_post, ffn_w_in=v_ffn_w_in,
                ffn_w_out=v_ffn_w_out, ab_w_in=v_ab_w_in, pool_w=v_pool_w, pool_scale=v_pool_scale,
                sgu_ln_g=v_sgu_ln_g, sgu_ln_b=v_sgu_ln_b, sgu_w=v_sgu_w, sgu_b=v_sgu_b, ab_w_out=v_ab_w_out,
                ssm_w_in=v_ssm_w_in, ssm_lam_re=v_ssm_lam_re, ssm_lam_im=v_ssm_lam_im, ssm_b_re=v_ssm_b_re,
                ssm_b_im=v_ssm_b_im, ssm_c_re=v_ssm_c_re, ssm_c_im=v_ssm_c_im, ssm_d=v_ssm_d,
                ssm_log_dt=v_ssm_log_dt, ssm_w_glu=v_ssm_w_glu)
    names = list(weights.keys())

    xi, yi, ci = lax.axis_index("x"), lax.axis_index("y"), lax.axis_index("c")
    chip = 2 * xi + yi
    me = 4 * xi + 2 * yi + ci
    S, D = x.shape[1], x.shape[2]
    L = ada_w.shape[0]
    NSUB = norm_pre.shape[1]
    DS = norm_pre.shape[2]
    FS = ffn_w_in.shape[-1]
    FR = ffn_w_out.shape[-2]
    x0 = x[0]
    target = loss_target[0]

    small_parts = [_pad_rows(p) for p in (c.reshape(D // DS, DS), norm_pre.reshape(L * NSUB, DS),
                                          norm_post.reshape(L * NSUB, DS), ssm_d.reshape(1, DS))]
    small_offs = [0]
    for p in small_parts:
        small_offs.append(small_offs[-1] + p.shape[0])
    small_all = allgather_small(jnp.concatenate(small_parts, axis=0), "ag_small").reshape(N_DEV, -1, DS)
    c_all = small_all[:, :D // DS].reshape(N_DEV, D)
    per_chip = small_all[0::2]
    o = small_offs[1]
    g_pre_full = jnp.moveaxis(per_chip[:, o:o + L * NSUB], 0, 1).reshape(L, NSUB, D)
    o = small_offs[2]
    g_post_full = jnp.moveaxis(per_chip[:, o:o + L * NSUB], 0, 1).reshape(L, NSUB, D)
    o = small_offs[3]
    d_full = jnp.moveaxis(per_chip[:, o:o + 1], 0, 1).reshape(1, D)

    pieces = []
    for l in range(L):
        pieces.append((f"ffn_{l}_0", [(ffn_w_in, (l, 0)), (ffn_w_out, (l, 0))]))
        if l % 2 == 0:
            pieces.append((f"mix_{l}", [(ab_w_in, (l // 2,)), (ab_w_out, (l // 2,))]))
        else:
            pieces.append((f"mix_{l}", [(ssm_w_in, (l // 2,)), (ssm_w_glu, (l // 2,))]))
        pieces.append((f"ffn_{l}_1", [(ffn_w_in, (l, 1)), (ffn_w_out, (l, 1))]))
    tags = [tag for tag, _ in pieces]
    chip_arr = chip.reshape(1).astype(jnp.int32)
    wg_started, wg_passing = {}, {}

    def cast_piece(tag, ws, dep):
        wg_started[tag] = [cast_place(w, lead, chip_arr, f"wg_cast_{tag}_{i}", dep=dep).reshape(
            N_CHIPS, 2, w.shape[-2] // 2, w.shape[-1]) for i, (w, lead) in enumerate(ws)]

    def start_piece(tag, dep):
        bufs = wg_started[tag]
        wg_started[tag] = split_start(bufs, weight_ici_copies, 3 * len(bufs), f"wg_ici_start_{tag}", dep=dep,
                                      token_shape=(8, D))
        return wg_started[tag][3]

    def weights_arrived(i, after):
        ssem, rsem, bufs, _ = wg_started[tags[i]]
        lands = split_wait(ssem, rsem, bufs, after, weight_ici_copies, f"wg_ici_wait_{tags[i]}")
        wg_passing[i] = split_start(lands, weight_d2d_copies, 3 * len(lands), f"wg_d2d_start_{tags[i]}")
        return wg_passing[i][3][0, 0]

    def weights_of(i, after):
        ssem, rsem, lands, _ = wg_passing[i]
        return split_wait(ssem, rsem, lands, after, weight_d2d_copies, f"wg_d2d_wait_{tags[i]}")

    NS = ada_w.shape[-1]
    ada_b_shard = lax.dynamic_slice_in_dim(ada_b, chip * NS, NS, axis=1).reshape(L, 1, NS)
    mod_part = ada_mod(c_all, ada_w, ada_b_shard, "ada_mod")
    mod_all = allgather_small(mod_part.reshape(L * N_DEV, NS), "ag_mod").reshape(N_DEV, L, N_DEV, NS)
    mod_mine = lax.dynamic_index_in_dim(mod_all[0::2], me, axis=2, keepdims=False)
    mod = jnp.moveaxis(mod_mine, 0, 1).reshape(L, NSUB, 3, D)
    cast_piece(*pieces[0], mod_all)
    first_token = start_piece(tags[0], mod_all)
    wg_tokens = [first_token]
    for tag, ws in pieces[1:]:
        cast_piece(tag, ws, first_token)
        wg_tokens.append(start_piece(tag, first_token))

    def vec_of(l, s):
        return jnp.concatenate([mod[l, s], g_pre_full[l, s][None], g_post_full[l, s][None],
                                jnp.zeros((3, D), F32)], axis=0)

    GB = 8
    s5_args = (ssm_lam_re[0], ssm_lam_im[0], ssm_log_dt[0], ssm_b_re[0], ssm_b_im[0], ssm_c_re[0], ssm_c_im[0])
    (a_re, a_im, bblk_re, bblk_im, cblk_re, cblk_im), s5_vjp = jax.vjp(lambda *p: _s5_prepare(*p, GB), *s5_args)
    bre16, bim16 = bblk_re.astype(BF16), bblk_im.astype(BF16)
    cre16, cim16 = cblk_re.astype(BF16), cblk_im.astype(BF16)
    cre16_t, cim16_t = jnp.swapaxes(cre16, 1, 2), jnp.swapaxes(cim16, 1, 2)

    pool_w16 = pool_w[0].astype(BF16)
    sgu_bexp = jnp.broadcast_to(sgu_b[0][:, :, None], sgu_w[0].shape)

    saved = {}
    ffn_w = {}
    xcur = x0
    stage = [0]

    def next_weights(after, vec):
        i = stage[0]
        stage[0] += 1
        weights_arrived(i, after)
        return weights_of(i, after), vec

    for l in range(L):
        v0 = vec_of(l, 0)
        g, v0p = next_weights(xcur if l else sum(wg_tokens, v0), v0)
        ffn_w[(l, 0)] = (g[0].reshape(N_CHIPS, D, FS), g[1].reshape(N_CHIPS, FR, D))
        out, h, u, f = ffn_fwd(xcur, v0p, *ffn_w[(l, 0)], f"ffn_fwd_{l}_0")
        saved[(l, 0)] = (xcur, v0, h, u, f)
        xcur = out
        v1 = vec_of(l, 1)
        mix_g, v1p = next_weights(xcur, v1)
        if l % 2 == 0:
            abin_g = mix_g[0].reshape(N_CHIPS, D, -1)
            about_g = mix_g[1].reshape(-1, D)
            h, z = pre_matmul(xcur, v1p, abin_g, f"mixa_in_{l}")
            ycat = mixa_core_fwd(z, pool_w16, pool_scale, sgu_ln_g, sgu_ln_b, sgu_w[0], sgu_bexp, f"mixa_core_{l}")
            out, f = matmul_post(ycat, about_g, xcur, v1, f"mixa_out_{l}")
            saved[(l, 1)] = (xcur, v1, h, z, ycat, f)
        else:
            sin_g = mix_g[0].reshape(1, -1, D)
            glu_g = mix_g[1].reshape(N_CHIPS, D, -1)
            h, uu = pre_matmul(xcur, v1p, sin_g, f"s5_in_{l}")
            us = _to_scan_order(uu)
            ys = s5_core_fwd(us, a_re, a_im, bre16, bim16, cre16_t, cim16_t, d_full, f"s5_core_{l}")
            yy = _from_scan_order(ys)
            out, f = s5_glu_fwd(yy, glu_g, xcur, v1, f"s5_glu_{l}")
            saved[(l, 1)] = (xcur, v1, h, us, yy, f)
        xcur = out
        v2 = vec_of(l, 2)
        g, v2p = next_weights(xcur, v2)
        ffn_w[(l, 1)] = (g[0].reshape(N_CHIPS, D, FS), g[1].reshape(N_CHIPS, FR, D))
        out, h, u, f = ffn_fwd(xcur, v2p, *ffn_w[(l, 1)], f"ffn_fwd_{l}_1")
        saved[(l, 2)] = (xcur, v2, h, u, f)
        xcur = out

    dcur, sq = loss_head(xcur, target, "loss_head")
    loss = lax.psum(sq[0, 0], ("x", "y", "c")) * (0.5 / D)

    accs = {}
    small_g = {}
    cidx = ci.reshape(1).astype(jnp.int32)
    chip_c = jnp.stack([chip, ci]).astype(jnp.int32)
    rs_open, rs_pending = [], []

    pin = [zero_after(loss.reshape(1, 1), "after_loss", (8, D))]
    last_start = [pin[0]]

    def pin_behind(token):
        pin.append(token)
        last_start[0] = token

    def pinned(v):
        for token in pin:
            v = v + (token if token.shape == v.shape else token[0, 0])
        pin.clear()
        return v

    def rs_begin(tag, items):
        arrs = [it[0] for it in items]
        lands = [lax.empty(a.shape[:-2] + (a.shape[-2] // 2, a.shape[-1]), BF16) for a in arrs]
        started = split_start(arrs + lands, pair_exchange_copies, len(arrs), f"rs_pair_start_{tag}",
                              token_shape=(8, D))
        rs_open.append((tag, items, started))
        pin_behind(started[3])

    def rs_advance(after):
        while rs_open:
            tag, items, (ssem, rsem, bufs, _) = rs_open.pop(0)
            n = len(items)
            bufs = split_wait(ssem, rsem, bufs, after, pair_exchange_copies, f"rs_pair_wait_{tag}")
            parts = [pair_sum(a, r, cidx, f"rs_pair_sum_{tag}_{i}") for i, (a, r) in enumerate(zip(bufs[:n], bufs[n:]))]
            lands = [lax.empty((3, 1) + p.shape[2:], BF16) for p in parts]
            started = split_start(parts + lands, chip_exchange_copies, 3 * n, f"rs_chip_start_{tag}",
                                  token_shape=(8, D))
            rs_pending.append((tag, items, started))
            pin_behind(started[3])

    def ffn_back(l, s, k, dcur, before_wgrads=None):
        xin, vv, h, u, f = saved[(l, s)]
        vv = pinned(vv)
        dx, df, du, act, acc = ffn_bwd(dcur, xin, f, u, vv, *ffn_w[(l, k)], f"ffn_bwd_{l}_{k}")
        accs[(l, s)] = acc
        rs_advance(dx)
        dep = None if before_wgrads is None else before_wgrads()
        g_win = tn_matmul(h, du, 1024, f"ffn_dwin_{l}_{k}", dep=dep)
        g_wout = tn_matmul_cols(act, df, FS, D, f"ffn_dwout_{l}_{k}")
        rs_begin(f"ffn_{l}_{k}", [(g_win[None], "ffn_w_in", 2 * l + k, 2 * L),
                                  (g_wout.reshape(1, N_CHIPS, FR, D), "ffn_w_out", 2 * l + k, 2 * L)])
        return dx

    grads = {}

    small_names = ["pool_w", "pool_scale", "sgu_ln_g", "sgu_ln_b", "sgu_w", "sgu_b", "ssm_lam_re", "ssm_lam_im",
                   "ssm_b_re", "ssm_b_im", "ssm_c_re", "ssm_c_im", "ssm_log_dt"]
    small_open = []

    def small_grads_start():
        acc_all = jnp.stack([jnp.stack([accs[(l, s)] for s in range(NSUB)]) for l in range(L)])
        dmod_mine = acc_all[:, :, A_SHIFT:A_GATE + 1].reshape(L, NSUB * 3 * D)
        packed, offs = _pack_rows([dmod_mine, acc_all[:, :, A_GPRE], acc_all[:, :, A_GPOST], dd_mine]
                                  + [small_g[n] for n in small_names], D)
        packed = pinned(packed).astype(BF16)
        started = split_start([packed, lax.empty((N_DEV,) + packed.shape, BF16)], allgather_direct_copies,
                              N_DEV - 1, "ag_grads_start")
        small_open.append((started, offs))
        pin_behind(started[3])
        return started[3]

    def small_grads_finish(after):
        (ssem, rsem, bufs, _), offs = small_open.pop()
        own, land = split_wait(ssem, rsem, bufs, after, allgather_direct_copies, "ag_grads_wait")
        me_arr = me.reshape(1).astype(jnp.int32)
        summed = _unpack_rows(sum_gathered(land, own, me_arr, "sum_small"), offs)
        grads.update({n: g for n, g in zip(small_names, summed[4:])})
        grads["ada_b"] = summed[0]
        grads["norm_pre"] = lax.dynamic_slice_in_dim(summed[1], chip * DS, DS, axis=2)
        grads["norm_post"] = lax.dynamic_slice_in_dim(summed[2], chip * DS, DS, axis=2)
        grads["ssm_d"] = lax.dynamic_slice_in_dim(summed[3], chip * DS, DS, axis=1)
        nm = L * NSUB * 3
        is_me = (jnp.arange(N_DEV) == me)[:, None, None]
        dmod_all = jnp.where(is_me, own[None, :nm], land[:, :nm]).reshape(N_DEV, L, NSUB * 3 * D)
        dmod_shard = lax.dynamic_slice_in_dim(jnp.moveaxis(dmod_all, 0, 1), chip * NS, NS, axis=2)
        grads["ada_w"] = ada_grad(c_all, dmod_shard, "ada_grad")

    for l in reversed(range(L)):
        dcur = ffn_back(l, 2, 1, dcur)
        if l % 2 == 0:
            xin, vv, h, z, ycat, f = saved[(l, 1)]
            vv = pinned(vv)
            df, dact, acc_post = post_bwd_matmul(dcur, f, vv, about_g, f"mixa_out_bwd_{l}")
            rs_advance(dact)
            g_about = tn_matmul_cols(ycat, df, 512, D, f"mixa_dwout_{l}")
            dz, dpw, dvecw, dsw, dsb = mixa_core_bwd(z, dact, pool_w16, pool_scale, sgu_ln_g, sgu_ln_b, sgu_w[0],
                                                     sgu_bexp, f"mixa_core_bwd_{l}")
            g_abin = tn_matmul_cols(h, dz, 512, abin_g.shape[-1], f"mixa_dwin_{l}")
            dcur, acc_pre = matmul_pre_bwd(dz, abin_g, xin, dcur, vv, f"mixa_in_bwd_{l}")
            accs[(l, 1)] = acc_pre + acc_post
            rs_begin(f"mixa_{l}", [(g_abin[None], "ab_w_in", 0, 1),
                                   (g_about.reshape(1, N_CHIPS, -1, D), "ab_w_out", 0, 1)])
            small_g.update(pool_w=dpw[None], pool_scale=dvecw[0:1], sgu_ln_g=dvecw[1:2], sgu_ln_b=dvecw[2:3],
                           sgu_w=dsw[None], sgu_b=dsb.T[None])
        else:
            xin, vv, h, us, yy, f = saved[(l, 1)]
            vv = pinned(vv)
            dy, dab, gact, acc_post = s5_glu_bwd(dcur, f, yy, vv, glu_g, f"s5_glu_bwd_{l}")
            rs_advance(dy)
            g_glu = tn_matmul_cols(gact, dab, 512, glu_g.shape[-1], f"s5_dwglu_{l}")
            dys = _to_scan_order(dy)
            dus, dbre, dbim, dcre_t, dcim_t, dar, dai, dd = s5_core_bwd(
                us, dys, a_re, a_im, bre16, bim16, cre16, cim16, d_full, f"s5_core_bwd_{l}")
            du = _from_scan_order(dus).astype(BF16)
            g_sin = tn_matmul_cols(h, du, 512, D, f"s5_dwin_{l}")
            dcur, acc_pre = matmul_pre_bwd(du, sin_g, xin, dcur, vv, f"s5_in_bwd_{l}")
            accs[(l, 1)] = acc_pre + acc_post
            rs_begin(f"s5_{l}", [(g_sin.reshape(1, N_CHIPS, -1, D), "ssm_w_in", 0, 1), (g_glu[None], "ssm_w_glu", 0, 1)])
            s5_grads = s5_vjp((dar, dai, dbre, dbim, jnp.swapaxes(dcre_t, 1, 2), jnp.swapaxes(dcim_t, 1, 2)))
            small_g.update(ssm_lam_re=s5_grads[0][None], ssm_lam_im=s5_grads[1][None], ssm_log_dt=s5_grads[2][None],
                           ssm_b_re=s5_grads[3][None], ssm_b_im=s5_grads[4][None], ssm_c_re=s5_grads[5][None],
                           ssm_c_im=s5_grads[6][None])
            dd_mine = dd
        dcur = ffn_back(l, 0, 0, dcur, before_wgrads=small_grads_start if l == 0 else None)
    grad_x = dcur[None]
    rs_advance(last_start[0])
    small_grads_finish(last_start[0])

    big_names = ["ffn_w_in", "ffn_w_out", "ab_w_in", "ab_w_out", "ssm_w_in", "ssm_w_glu"]
    deltas, new_m, new_v = {}, {}, {}

    def update(n):
        grads[n] = grads[n].reshape(weights[n].shape)
        deltas[n], new_m[n], new_v[n] = adamw_nd(weights[n], grads[n], m_in[n], v_in[n], f"adamw_{n}")

    for n in names:
        if n not in big_names:
            update(n)

    fin = {}
    after = pinned(deltas["ssm_log_dt"]) + deltas["ada_w"][0, 0, 0]
    for tag, items, (ssem, rsem, bufs, _) in rs_pending:
        bufs = split_wait(ssem, rsem, bufs, after, chip_exchange_copies, f"rs_chip_wait_{tag}")
        n = len(items)
        for i, (it, p, r) in enumerate(zip(items, bufs[:n], bufs[n:])):
            fin[it[1]] = chip_sum(p, r, chip_c, f"rs_chip_sum_{tag}_{i}", dest=fin.get(it[1]), slot=it[2],
                                  n_slots=it[3])
    filled = sibling_fill_halves([fin[n] for n in big_names], "rs_fill")
    for n, g in zip(big_names, filled):
        grads[n] = g
        update(n)

    return (loss, grad_x, *[grads[n] for n in names], *[deltas[n] for n in names],
            *[new_m[n] for n in names], *[new_v[n] for n in names])
```

```python
import math

import jax
import jax.numpy as jnp
from jax import lax
from jax.experimental import pallas as pl
from jax.experimental.pallas import tpu as pltpu

F32 = jnp.float32
BF16 = jnp.bfloat16
EPS = 1e-6
MESH_T = pl.DeviceIdType.MESH
VMEM_LIMIT_BYTES = 56 * 1024 * 1024
N_CHIPS = 4
N_DEV = 8
POOL_WINDOWS = (2, 4, 8, 16)
CHUNK = 128
SCAN_LANES = 8
SCAN_UNROLL = 4
WG_AHEAD = 2
FFN_BWD_CHUNK = 2048
ADAM_LR = 0.001
ADAM_B1 = 0.9
ADAM_B2 = 0.999
ADAM_EPS = 1e-08
ADAM_WD = 0.01
ADAM_STEP = 10
GELU_C = math.sqrt(2.0 / math.pi)
GELU_K = 0.044715

V_SHIFT, V_SCALE, V_GATE, V_GPRE, V_GPOST = 0, 1, 2, 3, 4
A_SHIFT, A_SCALE, A_GATE, A_GPRE, A_GPOST = 0, 1, 2, 3, 4

ANY = pl.BlockSpec(memory_space=pl.ANY)


def _cp(sem=None):
    if sem is None:
        return pltpu.CompilerParams(vmem_limit_bytes=VMEM_LIMIT_BYTES)
    return pltpu.CompilerParams(vmem_limit_bytes=VMEM_LIMIT_BYTES, dimension_semantics=sem)


def _dot(a, b):
    return jnp.dot(a, b, preferred_element_type=F32)


def _dot_nt(a, b):
    return lax.dot_general(a, b, (((1,), (1,)), ((), ())), preferred_element_type=F32)


def _dot_tn(a, b):
    return lax.dot_general(a, b, (((0,), (0,)), ((), ())), preferred_element_type=F32)


def _sigmoid(x):
    return 1.0 / (1.0 + jnp.exp(-x))


def _gelu(x):
    return 0.5 * x * (1.0 + jnp.tanh(GELU_C * (x + GELU_K * x * x * x)))


def _gelu_grad(x):
    t = jnp.tanh(GELU_C * (x + GELU_K * x * x * x))
    return 0.5 * (1.0 + t) + 0.5 * x * (1.0 - t * t) * GELU_C * (1.0 + 3.0 * GELU_K * x * x)


def _rowsum(v):
    return jnp.sum(v, axis=0, keepdims=True)


def _lanemean(v):
    return jnp.mean(v, axis=-1, keepdims=True)


def _row(ref, i):
    return ref[pl.ds(i, 1), :]


def _pre_fwd(x, vec_ref):
    r = lax.rsqrt(_lanemean(x * x) + EPS)
    return (x * r) * _row(vec_ref, V_GPRE) * (1.0 + _row(vec_ref, V_SCALE)) + _row(vec_ref, V_SHIFT)


def _pre_bwd(x, dh, vec_ref):
    g = _row(vec_ref, V_GPRE)
    sc = 1.0 + _row(vec_ref, V_SCALE)
    r = lax.rsqrt(_lanemean(x * x) + EPS)
    xn = x * r
    dhx = dh * xn
    t2 = dh * (g * sc)
    dx = r * (t2 - xn * _lanemean(t2 * xn))
    return dx, _rowsum(dh), _rowsum(dhx * g), _rowsum(dhx * sc)


def _post_fwd(x, f, vec_ref, rw):
    q = lax.rsqrt(_lanemean(f * f) + EPS)
    return x + (rw * _row(vec_ref, V_GATE)) * (f * q * _row(vec_ref, V_GPOST))


def _post_bwd(dout, f, vec_ref, rw):
    gp = _row(vec_ref, V_GPOST)
    gate = _row(vec_ref, V_GATE)
    q = lax.rsqrt(_lanemean(f * f) + EPS)
    fhat = f * q
    dgate = _rowsum(dout * (rw * fhat * gp))
    dy = dout * (rw * gate)
    dgpost = _rowsum(dy * fhat)
    t = dy * gp
    df = q * (t - fhat * _lanemean(t * fhat))
    return df, dgate, dgpost


def _acc_add(acc_ref, row, v):
    acc_ref[pl.ds(row, 1), :] += v


def ffn_fwd(x, vec, win_g, wout_g, name, tm=512):
    S, D = x.shape
    FS = win_g.shape[-1]
    FR = wout_g.shape[-2]

    def body(x_ref, vec_ref, win_hbm, wout_hbm, out_ref, h_ref, u_ref, f_ref, win_s, wout_s, sem):
        @pl.when(pl.program_id(0) == 0)
        def _():
            cps = [pltpu.make_async_copy(win_hbm.at[j], win_s.at[j], sem.at[j]) for j in range(N_CHIPS)]
            cps += [pltpu.make_async_copy(wout_hbm.at[j], wout_s.at[pl.ds(j * FR, FR), :], sem.at[N_CHIPS + j])
                    for j in range(N_CHIPS)]
            for cp in cps:
                cp.start()
            for cp in cps:
                cp.wait()

        xv = x_ref[...]
        h = _pre_fwd(xv, vec_ref).astype(BF16)
        h_ref[...] = h
        f = jnp.zeros((tm, D), F32)
        for j in range(2):
            a = _dot(h, win_s[j])
            b = _dot(h, win_s[2 + j])
            u_ref[j] = a.astype(BF16)
            u_ref[2 + j] = b.astype(BF16)
            act = (a * _sigmoid(a) * b).astype(BF16)
            f = f + _dot(act, wout_s[pl.ds(j * FS, FS), :])
        f_ref[...] = f
        out_ref[...] = _post_fwd(xv, f, vec_ref, 0.5)

    row = pl.BlockSpec((tm, D), lambda i: (i, 0))
    return pl.pallas_call(
        body, name=name, grid=(S // tm,),
        in_specs=[row, pl.BlockSpec((8, D), lambda i: (0, 0)), ANY, ANY],
        out_specs=[row, row, pl.BlockSpec((N_CHIPS, tm, FS), lambda i: (0, i, 0)), row],
        out_shape=[jax.ShapeDtypeStruct((S, D), F32), jax.ShapeDtypeStruct((S, D), BF16),
                   jax.ShapeDtypeStruct((N_CHIPS, S, FS), BF16), jax.ShapeDtypeStruct((S, D), F32)],
        scratch_shapes=[pltpu.VMEM((N_CHIPS, D, FS), BF16), pltpu.VMEM((N_CHIPS * FR, D), BF16),
                        pltpu.SemaphoreType.DMA((2 * N_CHIPS,))],
        compiler_params=_cp(("arbitrary",)),
    )(x, vec, win_g, wout_g)


def ffn_bwd(dout, x, f, u, vec, win_g, wout_g, name, tm=256):
    S, D = x.shape
    FS = win_g.shape[-1]
    FR = wout_g.shape[-2]

    def body(dout_ref, x_ref, f_ref, u_ref, vec_ref, win_hbm, wout_hbm,
             dx_ref, df_ref, du_ref, act_ref, acc_ref, win_s, wout_s, sem):
        @pl.when(pl.program_id(0) == 0)
        def _():
            cps = [pltpu.make_async_copy(win_hbm.at[j], win_s.at[j], sem.at[j]) for j in range(N_CHIPS)]
            cps += [pltpu.make_async_copy(wout_hbm.at[j], wout_s.at[pl.ds(j * FR, FR), :], sem.at[N_CHIPS + j])
                    for j in range(N_CHIPS)]
            for cp in cps:
                cp.start()
            acc_ref[...] = jnp.zeros_like(acc_ref)
            for cp in cps:
                cp.wait()

        dout_v = dout_ref[...]
        df, dgate, dgpost = _post_bwd(dout_v, f_ref[...], vec_ref, 0.5)
        dfb = df.astype(BF16)
        df_ref[...] = dfb
        dh = jnp.zeros((tm, D), F32)
        chunks = [(c0, min(FFN_BWD_CHUNK, FS - c0)) for c0 in range(0, FS, FFN_BWD_CHUNK)]
        for j in range(2):
            for c0, cw in chunks:
                a = u_ref[j, :, c0:c0 + cw].astype(F32)
                b = u_ref[2 + j, :, c0:c0 + cw].astype(F32)
                sig = _sigmoid(a)
                sl = a * sig
                dact = _dot_nt(dfb, wout_s[pl.ds(j * FS + c0, cw), :])
                da = (dact * b * (sig * (1.0 + a * (1.0 - sig)))).astype(BF16)
                db = (dact * sl).astype(BF16)
                du_ref[j, :, c0:c0 + cw] = da
                du_ref[2 + j, :, c0:c0 + cw] = db
                act_ref[:, j * FS + c0:j * FS + c0 + cw] = (sl * b).astype(BF16)
                dh = dh + _dot_nt(da, win_s[j, :, c0:c0 + cw]) + _dot_nt(db, win_s[2 + j, :, c0:c0 + cw])
        dx, dshift, dscale, dgpre = _pre_bwd(x_ref[...], dh, vec_ref)
        dx_ref[...] = dout_v + dx
        _acc_add(acc_ref, A_SHIFT, dshift)
        _acc_add(acc_ref, A_SCALE, dscale)
        _acc_add(acc_ref, A_GATE, dgate)
        _acc_add(acc_ref, A_GPRE, dgpre)
        _acc_add(acc_ref, A_GPOST, dgpost)

    row = pl.BlockSpec((tm, D), lambda i: (i, 0))
    ublk = pl.BlockSpec((N_CHIPS, tm, FS), lambda i: (0, i, 0))
    const = pl.BlockSpec((8, D), lambda i: (0, 0))
    return pl.pallas_call(
        body, name=name, grid=(S // tm,),
        in_specs=[row, row, row, ublk, const, ANY, ANY],
        out_specs=[row, row, ublk, pl.BlockSpec((tm, 2 * FS), lambda i: (i, 0)), const],
        out_shape=[jax.ShapeDtypeStruct((S, D), F32), jax.ShapeDtypeStruct((S, D), BF16),
                   jax.ShapeDtypeStruct((N_CHIPS, S, FS), BF16), jax.ShapeDtypeStruct((S, 2 * FS), BF16),
                   jax.ShapeDtypeStruct((8, D), F32)],
        scratch_shapes=[pltpu.VMEM((N_CHIPS, D, FS), BF16), pltpu.VMEM((N_CHIPS * FR, D), BF16),
                        pltpu.SemaphoreType.DMA((2 * N_CHIPS,))],
        compiler_params=_cp(("arbitrary",)),
    )(dout, x, f, u, vec, win_g, wout_g)


def tn_matmul(a, b, tk, name, dep=None):
    S, K = a.shape
    nb, _, tn = b.shape
    extra = [] if dep is None else [dep]

    def body(a_ref, b_ref, *rest):
        rest[-1][...] = _dot_tn(a_ref[...], b_ref[...]).astype(BF16)

    return pl.pallas_call(
        body, name=name, grid=(K // tk, nb),
        in_specs=[pl.BlockSpec((S, tk), lambda i, j: (0, i)), pl.BlockSpec((None, S, tn), lambda i, j: (j, 0, 0))]
        + [ANY] * len(extra),
        out_specs=pl.BlockSpec((None, tk, tn), lambda i, j: (j, i, 0)),
        out_shape=jax.ShapeDtypeStruct((nb, K, tn), BF16),
        compiler_params=_cp(("arbitrary", "arbitrary")),
    )(a, b, *extra)


def tn_matmul_cols(a, b, tk, tn, name):
    S, K = a.shape
    N = b.shape[1]

    def body(a_ref, b_ref, o_ref):
        o_ref[...] = _dot_tn(a_ref[...], b_ref[...]).astype(BF16)

    return pl.pallas_call(
        body, name=name, grid=(K // tk, N // tn),
        in_specs=[pl.BlockSpec((S, tk), lambda i, j: (0, i)), pl.BlockSpec((S, tn), lambda i, j: (0, j))],
        out_specs=pl.BlockSpec((None, tk, tn), lambda i, j: (j, i, 0)),
        out_shape=jax.ShapeDtypeStruct((N // tn, K, tn), BF16),
        compiler_params=_cp(("arbitrary", "arbitrary")),
    )(a, b)


def pre_matmul(x, vec, w3, name, tm=256):
    S, D = x.shape
    nj, _, Nj = w3.shape

    def body(x_ref, vec_ref, w_ref, h_ref, z_ref):
        h = _pre_fwd(x_ref[...], vec_ref).astype(BF16)
        h_ref[...] = h
        for j in range(nj):
            z_ref[:, j * Nj:(j + 1) * Nj] = _dot(h, w_ref[j])

    row = pl.BlockSpec((tm, D), lambda i: (i, 0))
    return pl.pallas_call(
        body, name=name, grid=(S // tm,),
        in_specs=[row, pl.BlockSpec((8, D), lambda i: (0, 0)), pl.BlockSpec((nj, D, Nj), lambda i: (0, 0, 0))],
        out_specs=[row, pl.BlockSpec((tm, nj * Nj), lambda i: (i, 0))],
        out_shape=[jax.ShapeDtypeStruct((S, D), BF16), jax.ShapeDtypeStruct((S, nj * Nj), F32)],
        compiler_params=_cp(("arbitrary",)),
    )(x, vec, w3)


def matmul_pre_bwd(dz, w3, x, dres, vec, name, tm=256):
    S, D = x.shape
    nj, _, Nj = w3.shape

    def body(dz_ref, w_ref, x_ref, dres_ref, vec_ref, dx_ref, acc_ref):
        @pl.when(pl.program_id(0) == 0)
        def _():
            acc_ref[...] = jnp.zeros_like(acc_ref)

        dh = jnp.zeros((tm, D), F32)
        for j in range(nj):
            dh = dh + _dot_nt(dz_ref[:, j * Nj:(j + 1) * Nj], w_ref[j])
        dx, dshift, dscale, dgpre = _pre_bwd(x_ref[...], dh, vec_ref)
        dx_ref[...] = dres_ref[...] + dx
        _acc_add(acc_ref, A_SHIFT, dshift)
        _acc_add(acc_ref, A_SCALE, dscale)
        _acc_add(acc_ref, A_GPRE, dgpre)

    row = pl.BlockSpec((tm, D), lambda i: (i, 0))
    const = pl.BlockSpec((8, D), lambda i: (0, 0))
    return pl.pallas_call(
        body, name=name, grid=(S // tm,),
        in_specs=[pl.BlockSpec((tm, nj * Nj), lambda i: (i, 0)), pl.BlockSpec((nj, D, Nj), lambda i: (0, 0, 0)),
                  row, row, const],
        out_specs=[row, const],
        out_shape=[jax.ShapeDtypeStruct((S, D), F32), jax.ShapeDtypeStruct((8, D), F32)],
        compiler_params=_cp(("arbitrary",)),
    )(dz, w3, x, dres, vec)


def matmul_post(act, w, x, vec, name, tm=256):
    S, D = x.shape
    K = act.shape[1]

    def body(act_ref, w_ref, x_ref, vec_ref, out_ref, f_ref):
        f = _dot(act_ref[...], w_ref[...])
        f_ref[...] = f
        out_ref[...] = _post_fwd(x_ref[...], f, vec_ref, 1.0)

    row = pl.BlockSpec((tm, D), lambda i: (i, 0))
    return pl.pallas_call(
        body, name=name, grid=(S // tm,),
        in_specs=[pl.BlockSpec((tm, K), lambda i: (i, 0)), pl.BlockSpec((K, D), lambda i: (0, 0)), row,
                  pl.BlockSpec((8, D), lambda i: (0, 0))],
        out_specs=[row, row],
        out_shape=[jax.ShapeDtypeStruct((S, D), F32), jax.ShapeDtypeStruct((S, D), F32)],
        compiler_params=_cp(("arbitrary",)),
    )(act, w, x, vec)


def post_bwd_matmul(dout, f, vec, w, name, tm=256):
    S, D = dout.shape
    K = w.shape[0]

    def body(dout_ref, f_ref, vec_ref, w_ref, df_ref, dact_ref, acc_ref):
        @pl.when(pl.program_id(0) == 0)
        def _():
            acc_ref[...] = jnp.zeros_like(acc_ref)

        df, dgate, dgpost = _post_bwd(dout_ref[...], f_ref[...], vec_ref, 1.0)
        dfb = df.astype(BF16)
        df_ref[...] = dfb
        dact_ref[...] = _dot_nt(dfb, w_ref[...])
        _acc_add(acc_ref, A_GATE, dgate)
        _acc_add(acc_ref, A_GPOST, dgpost)

    row = pl.BlockSpec((tm, D), lambda i: (i, 0))
    const = pl.BlockSpec((8, D), lambda i: (0, 0))
    return pl.pallas_call(
        body, name=name, grid=(S // tm,),
        in_specs=[row, row, const, pl.BlockSpec((K, D), lambda i: (0, 0))],
        out_specs=[row, pl.BlockSpec((tm, K), lambda i: (i, 0)), const],
        out_shape=[jax.ShapeDtypeStruct((S, D), BF16), jax.ShapeDtypeStruct((S, K), F32),
                   jax.ShapeDtypeStruct((8, D), F32)],
        compiler_params=_cp(("arbitrary",)),
    )(dout, f, vec, w)


def _band(w, transposed, prev):
    r = lax.broadcasted_iota(jnp.int32, (CHUNK, CHUNK), 1 if transposed else 0)
    c = lax.broadcasted_iota(jnp.int32, (CHUNK, CHUNK), 0 if transposed else 1)
    d = r - c
    m = (d + CHUNK < w) if prev else ((d >= 0) & (d < w))
    return jnp.where(m, 1.0, 0.0).astype(BF16)


def _split_hi_lo(a):
    hi = a.astype(BF16)
    lo = (a - hi.astype(F32)).astype(BF16)
    return hi, lo


def _pool_diff(a, ap, g, denom):
    w = POOL_WINDOWS[g]
    a_hi, a_lo = _split_hi_lo(a)
    p_hi, p_lo = _split_hi_lo(ap)
    mc = _band(w, False, False)
    mp = _band(w, False, True)
    win = _dot(mc, a_hi) + _dot(mc, a_lo) + _dot(mp, p_hi) + _dot(mp, p_lo)
    return win / denom - a


def _sgu_norm(v, lg, lb):
    mu = _lanemean(v)
    xc = v - mu
    rstd = lax.rsqrt(_lanemean(xc * xc) + EPS)
    xhat = xc * rstd
    return xhat, rstd, xhat * lg + lb


def _tril_mask():
    r = lax.broadcasted_iota(jnp.int32, (CHUNK, CHUNK), 0)
    c = lax.broadcasted_iota(jnp.int32, (CHUNK, CHUNK), 1)
    return r >= c


def _positions(i, w):
    r = lax.broadcasted_iota(jnp.int32, (CHUNK, 128), 0)
    pos = (i * CHUNK + r + 1).astype(F32)
    return jnp.minimum(pos, float(w))


def mixa_core_fwd(z, pool_w, pool_scale, ln_g, ln_b, sgu_w, sgu_bexp, name):
    S = z.shape[0]
    W = z.shape[1] // 3
    G = len(POOL_WINDOWS)
    GD = W // G

    def body(zc_ref, zp_ref, pw_ref, ps_ref, lg_ref, lb_ref, sw_ref, sb_ref, y_ref):
        i = pl.program_id(0)
        has_prev = jnp.where(i > 0, 1.0, 0.0)
        for g in range(G):
            sl = slice(g * GD, (g + 1) * GD)
            a = zc_ref[:, sl]
            ap = zp_ref[:, sl] * has_prev
            d = _pool_diff(a, ap, g, _positions(i, POOL_WINDOWS[g])).astype(BF16)
            y_ref[:, sl] = (_dot(d, pw_ref[g]) * ps_ref[:, sl]).astype(BF16)
        tril = _tril_mask()
        for hh in range(G):
            u = _gelu(zc_ref[:, W + hh * GD:W + (hh + 1) * GD])
            v = _gelu(zc_ref[:, 2 * W + hh * GD:2 * W + (hh + 1) * GD])
            sl = slice(hh * GD, (hh + 1) * GD)
            _, _, vn = _sgu_norm(v, lg_ref[:, sl], lb_ref[:, sl])
            wm = jnp.where(tril, sw_ref[hh], 0.0).astype(BF16)
            s = _dot(wm, vn.astype(BF16)) + sb_ref[hh]
            y_ref[:, W + hh * GD:W + (hh + 1) * GD] = (u * s).astype(BF16)

    vecw = pl.BlockSpec((1, W), lambda i: (0, 0))
    mats = pl.BlockSpec((G, GD, GD), lambda i: (0, 0, 0))
    return pl.pallas_call(
        body, name=name, grid=(S // CHUNK,),
        in_specs=[pl.BlockSpec((CHUNK, 3 * W), lambda i: (i, 0)),
                  pl.BlockSpec((CHUNK, W), lambda i: (jnp.maximum(i - 1, 0), 0)),
                  mats, vecw, vecw, vecw, mats, mats],
        out_specs=pl.BlockSpec((CHUNK, 2 * W), lambda i: (i, 0)),
        out_shape=jax.ShapeDtypeStruct((S, 2 * W), BF16),
        compiler_params=_cp(("arbitrary",)),
    )(z, z, pool_w, pool_scale, ln_g, ln_b, sgu_w, sgu_bexp)


def mixa_core_bwd(z, dy, pool_w, pool_scale, ln_g, ln_b, sgu_w, sgu_bexp, name):
    S = z.shape[0]
    W = z.shape[1] // 3
    G = len(POOL_WINDOWS)
    GD = W // G
    n_tiles = S // CHUNK

    def body(zc_ref, zp_ref, dyc_ref, dyn_ref, pw_ref, ps_ref, lg_ref, lb_ref, sw_ref, sb_ref,
             dz_ref, dpw_ref, dvec_ref, dsw_ref, dsb_ref):
        i = pl.program_id(0)

        @pl.when(i == 0)
        def _():
            dpw_ref[...] = jnp.zeros_like(dpw_ref)
            dvec_ref[...] = jnp.zeros_like(dvec_ref)
            dsw_ref[...] = jnp.zeros_like(dsw_ref)
            dsb_ref[...] = jnp.zeros_like(dsb_ref)

        has_prev = jnp.where(i > 0, 1.0, 0.0)
        has_next = jnp.where(i < n_tiles - 1, 1.0, 0.0)
        for g in range(G):
            w = POOL_WINDOWS[g]
            sl = slice(g * GD, (g + 1) * GD)
            a = zc_ref[:, sl]
            ap = zp_ref[:, sl] * has_prev
            den_c = _positions(i, w)
            den_n = _positions(i + 1, w)
            d = _pool_diff(a, ap, g, den_c).astype(BF16)
            ps = ps_ref[:, sl]
            pw = pw_ref[g]
            dyc = dyc_ref[:, sl]
            dvec_ref[pl.ds(0, 1), sl] += _rowsum(dyc * _dot(d, pw))
            dyp_c = (dyc * ps).astype(BF16)
            dyp_n = (dyn_ref[:, sl] * (ps * has_next)).astype(BF16)
            dpw_ref[g] += _dot_tn(d, dyp_c)
            dd_c = _dot_nt(dyp_c, pw)
            dd_n = _dot_nt(dyp_n, pw)
            da = (_dot(_band(w, True, False), (dd_c / den_c).astype(BF16))
                  + _dot(_band(w, True, True), (dd_n / den_n).astype(BF16)) - dd_c)
            dz_ref[:, sl] = da.astype(BF16)
        tril = _tril_mask()
        for hh in range(G):
            sl = slice(hh * GD, (hh + 1) * GD)
            zu = zc_ref[:, W + hh * GD:W + (hh + 1) * GD]
            zv = zc_ref[:, 2 * W + hh * GD:2 * W + (hh + 1) * GD]
            u = _gelu(zu)
            v = _gelu(zv)
            lg = lg_ref[:, sl]
            xhat, rstd, vn = _sgu_norm(v, lg, lb_ref[:, sl])
            vnb = vn.astype(BF16)
            wm = jnp.where(tril, sw_ref[hh], 0.0).astype(BF16)
            s = _dot(wm, vnb) + sb_ref[hh]
            dyb = dyc_ref[:, W + hh * GD:W + (hh + 1) * GD]
            du = dyb * s
            ds = dyb * u
            dsb_ref[:, hh:hh + 1] += jnp.sum(ds, axis=1, keepdims=True)
            dsb16 = ds.astype(BF16)
            dsw_ref[hh] += jnp.where(tril, _dot_nt(dsb16, vnb), 0.0)
            dvn = _dot_tn(wm, dsb16)
            dvec_ref[pl.ds(1, 1), sl] += _rowsum(dvn * xhat)
            dvec_ref[pl.ds(2, 1), sl] += _rowsum(dvn)
            dxh = dvn * lg
            dv = rstd * (dxh - _lanemean(dxh) - xhat * _lanemean(dxh * xhat))
            dz_ref[:, W + hh * GD:W + (hh + 1) * GD] = (du * _gelu_grad(zu)).astype(BF16)
            dz_ref[:, 2 * W + hh * GD:2 * W + (hh + 1) * GD] = (dv * _gelu_grad(zv)).astype(BF16)

    vecw = pl.BlockSpec((1, W), lambda i: (0, 0))
    mats = pl.BlockSpec((G, GD, GD), lambda i: (0, 0, 0))
    return pl.pallas_call(
        body, name=name, grid=(n_tiles,),
        in_specs=[pl.BlockSpec((CHUNK, 3 * W), lambda i: (i, 0)),
                  pl.BlockSpec((CHUNK, W), lambda i: (jnp.maximum(i - 1, 0), 0)),
                  pl.BlockSpec((CHUNK, 2 * W), lambda i: (i, 0)),
                  pl.BlockSpec((CHUNK, W), lambda i: (jnp.minimum(i + 1, n_tiles - 1), 0)),
                  mats, vecw, vecw, vecw, mats, mats],
        out_specs=[pl.BlockSpec((CHUNK, 3 * W), lambda i: (i, 0)), mats,
                   pl.BlockSpec((8, W), lambda i: (0, 0)), mats, pl.BlockSpec((CHUNK, G), lambda i: (0, 0))],
        out_shape=[jax.ShapeDtypeStruct((S, 3 * W), BF16), jax.ShapeDtypeStruct((G, GD, GD), F32),
                   jax.ShapeDtypeStruct((8, W), F32), jax.ShapeDtypeStruct((G, CHUNK, CHUNK), F32),
                   jax.ShapeDtypeStruct((CHUNK, G), F32)],
        compiler_params=_cp(("arbitrary",)),
    )(z, z, dy, dy, pool_w, pool_scale, ln_g, ln_b, sgu_w, sgu_bexp)


def _cmul(ar, ai, br, bi):
    return ar * br - ai * bi, ar * bi + ai * br


def _cpow(ar, ai, n):
    rr, ri = None, None
    br, bi = ar, ai
    while n:
        if n & 1:
            rr, ri = (br, bi) if rr is None else _cmul(rr, ri, br, bi)
        n >>= 1
        if n:
            br, bi = _cmul(br, bi, br, bi)
    return rr, ri


def _seg_rows(k):
    return pl.ds(pl.multiple_of(k * SCAN_LANES, SCAN_LANES), SCAN_LANES)


def _scan_fwd(xre, xim, carry, ar, ai, K):
    P = xre.shape[1]
    a8r = jnp.broadcast_to(ar, (SCAN_LANES, P))
    a8i = jnp.broadcast_to(ai, (SCAN_LANES, P))

    def local(k, c):
        pr, pi = c
        rows = _seg_rows(k)
        nr = a8r * pr - a8i * pi + xre[rows, :]
        ni = a8r * pi + a8i * pr + xim[rows, :]
        xre[rows, :] = nr
        xim[rows, :] = ni
        return nr, ni

    er, ei = lax.fori_loop(1, K, local, (xre[pl.ds(0, SCAN_LANES), :], xim[pl.ds(0, SCAN_LANES), :]),
                           unroll=SCAN_UNROLL)
    akr, aki = _cpow(ar, ai, K)
    cr = jnp.zeros((1, P), F32)
    ci = jnp.zeros((1, P), F32)
    carry[pl.ds(0, 1), :] = cr
    carry[pl.ds(SCAN_LANES, 1), :] = ci
    for j in range(1, SCAN_LANES):
        tr, ti = _cmul(akr, aki, cr, ci)
        cr = er[j - 1:j, :] + tr
        ci = ei[j - 1:j, :] + ti
        carry[pl.ds(j, 1), :] = cr
        carry[pl.ds(SCAN_LANES + j, 1), :] = ci
    cmr = carry[pl.ds(0, SCAN_LANES), :]
    cmi = carry[pl.ds(SCAN_LANES, SCAN_LANES), :]

    def fix(k, c):
        pr, pi = c
        rows = _seg_rows(k)
        tr, ti = _cmul(pr, pi, cmr, cmi)
        xre[rows, :] += tr
        xim[rows, :] += ti
        return _cmul(pr, pi, a8r, a8i)

    lax.fori_loop(0, K, fix, (a8r, a8i), unroll=SCAN_UNROLL)


def s5_core_fwd(u, ar, ai, bre, bim, cre_t, cim_t, dskip, name):
    S, D = u.shape
    nblk, UB, PB = bre.shape
    K = S // SCAN_LANES

    def body(u_ref, ar_ref, ai_ref, bre_ref, bim_ref, cre_ref, cim_ref, d_ref, y_ref, xre, xim, carry):
        uv = u_ref[...]
        ub = uv.astype(BF16)
        xre[...] = _dot(ub, bre_ref[...])
        xim[...] = _dot(ub, bim_ref[...])
        _scan_fwd(xre, xim, carry, ar_ref[...], ai_ref[...], K)
        y_ref[...] = (_dot(xre[...].astype(BF16), cre_ref[...]) - _dot(xim[...].astype(BF16), cim_ref[...])
                      + d_ref[...] * uv)

    ucol = pl.BlockSpec((S, UB), lambda i: (0, i))
    pvec = pl.BlockSpec((None, 1, PB), lambda i: (i, 0, 0))
    bmat = pl.BlockSpec((None, UB, PB), lambda i: (i, 0, 0))
    cmat = pl.BlockSpec((None, PB, UB), lambda i: (i, 0, 0))
    return pl.pallas_call(
        body, name=name, grid=(nblk,),
        in_specs=[ucol, pvec, pvec, bmat, bmat, cmat, cmat, pl.BlockSpec((1, UB), lambda i: (0, i))],
        out_specs=ucol,
        out_shape=jax.ShapeDtypeStruct((S, D), F32),
        scratch_shapes=[pltpu.VMEM((S, PB), F32), pltpu.VMEM((S, PB), F32), pltpu.VMEM((2 * SCAN_LANES, PB), F32)],
        compiler_params=_cp(("arbitrary",)),
    )(u, ar, ai, bre, bim, cre_t, cim_t, dskip)


def s5_core_bwd(u, dy, ar, ai, bre, bim, cre, cim, dskip, name):
    S, D = u.shape
    nblk, UB, PB = bre.shape
    K = S // SCAN_LANES

    def body(u_ref, dy_ref, ar_ref, ai_ref, bre_ref, bim_ref, cre_ref, cim_ref, d_ref,
             du_ref, dbre_ref, dbim_ref, dcre_ref, dcim_ref, dar_ref, dai_ref, dd_ref,
             xre, xim, gre, gim, carry, carry_b):
        ar = ar_ref[...]
        ai = ai_ref[...]
        uv = u_ref[...]
        ub = uv.astype(BF16)
        dyv = dy_ref[...]
        dyb = dyv.astype(BF16)
        xre[...] = _dot(ub, bre_ref[...])
        xim[...] = _dot(ub, bim_ref[...])
        _scan_fwd(xre, xim, carry, ar, ai, K)
        dcre_ref[...] = _dot_tn(xre[...].astype(BF16), dyb)
        dcim_ref[...] = -_dot_tn(xim[...].astype(BF16), dyb)
        gre[...] = _dot(dyb, cre_ref[...])
        gim[...] = -_dot(dyb, cim_ref[...])

        a8r = jnp.broadcast_to(ar, (SCAN_LANES, PB))
        a8i = jnp.broadcast_to(ai, (SCAN_LANES, PB))
        na8i = -a8i

        def local(s, c):
            k = K - 2 - s
            nr, ni = c
            rows = _seg_rows(k)
            tr = gre[rows, :] + a8r * nr + a8i * ni
            ti = gim[rows, :] + a8r * ni - a8i * nr
            gre[rows, :] = tr
            gim[rows, :] = ti
            return tr, ti

        last = _seg_rows(K - 1)
        fr, fi = lax.fori_loop(0, K - 1, local, (gre[last, :], gim[last, :]), unroll=SCAN_UNROLL)
        akr, aki = _cpow(ar, -ai, K)
        cr = jnp.zeros((1, PB), F32)
        ci = jnp.zeros((1, PB), F32)
        carry_b[pl.ds(SCAN_LANES - 1, 1), :] = cr
        carry_b[pl.ds(2 * SCAN_LANES - 1, 1), :] = ci
        for j in range(SCAN_LANES - 2, -1, -1):
            tr, ti = _cmul(akr, aki, cr, ci)
            cr = fr[j + 1:j + 2, :] + tr
            ci = fi[j + 1:j + 2, :] + ti
            carry_b[pl.ds(j, 1), :] = cr
            carry_b[pl.ds(SCAN_LANES + j, 1), :] = ci
        cbr = carry_b[pl.ds(0, SCAN_LANES), :]
        cbi = carry_b[pl.ds(SCAN_LANES, SCAN_LANES), :]

        def fix_rows(rows, c, xr, xi):
            pr, pi, dar, dai = c
            tr, ti = _cmul(pr, pi, cbr, cbi)
            g_r = gre[rows, :] + tr
            g_i = gim[rows, :] + ti
            gre[rows, :] = g_r
            gim[rows, :] = g_i
            dar = dar + g_r * xr + g_i * xi
            dai = dai + g_i * xr - g_r * xi
            nr, ni = _cmul(pr, pi, a8r, na8i)
            return nr, ni, dar, dai

        def fix(s, c):
            k = K - 1 - s
            prev = _seg_rows(k - 1)
            return fix_rows(_seg_rows(k), c, xre[prev, :], xim[prev, :])

        z8 = jnp.zeros((SCAN_LANES, PB), F32)
        c = lax.fori_loop(0, K - 1, fix, (a8r, na8i, z8, z8), unroll=SCAN_UNROLL)
        _, _, dar, dai = fix_rows(_seg_rows(0), c, carry[pl.ds(0, SCAN_LANES), :], carry[pl.ds(SCAN_LANES, SCAN_LANES), :])
        dar_ref[...] = _rowsum(dar)
        dai_ref[...] = _rowsum(dai)
        grb = gre[...].astype(BF16)
        gib = gim[...].astype(BF16)
        dbre_ref[...] = _dot_tn(ub, grb)
        dbim_ref[...] = _dot_tn(ub, gib)
        du_ref[...] = _dot_nt(grb, bre_ref[...]) + _dot_nt(gib, bim_ref[...]) + d_ref[...] * dyv
        dd_ref[...] = _rowsum(dyv * uv)

    ucol = pl.BlockSpec((S, UB), lambda i: (0, i))
    pvec = pl.BlockSpec((None, 1, PB), lambda i: (i, 0, 0))
    bmat = pl.BlockSpec((None, UB, PB), lambda i: (i, 0, 0))
    cmat = pl.BlockSpec((None, PB, UB), lambda i: (i, 0, 0))
    dvec = pl.BlockSpec((1, UB), lambda i: (0, i))
    return pl.pallas_call(
        body, name=name, grid=(nblk,),
        in_specs=[ucol, ucol, pvec, pvec, bmat, bmat, bmat, bmat, dvec],
        out_specs=[ucol, bmat, bmat, cmat, cmat, pvec, pvec, dvec],
        out_shape=[jax.ShapeDtypeStruct((S, D), F32),
                   jax.ShapeDtypeStruct((nblk, UB, PB), F32), jax.ShapeDtypeStruct((nblk, UB, PB), F32),
                   jax.ShapeDtypeStruct((nblk, PB, UB), F32), jax.ShapeDtypeStruct((nblk, PB, UB), F32),
                   jax.ShapeDtypeStruct((nblk, 1, PB), F32), jax.ShapeDtypeStruct((nblk, 1, PB), F32),
                   jax.ShapeDtypeStruct((1, D), F32)],
        scratch_shapes=[pltpu.VMEM((S, PB), F32), pltpu.VMEM((S, PB), F32), pltpu.VMEM((S, PB), F32),
                        pltpu.VMEM((S, PB), F32), pltpu.VMEM((2 * SCAN_LANES, PB), F32),
                        pltpu.VMEM((2 * SCAN_LANES, PB), F32)],
        compiler_params=_cp(("arbitrary",)),
    )(u, dy, ar, ai, bre, bim, cre, cim, dskip)


def s5_glu_fwd(y, wglu, x, vec, name, tm=256):
    S, D = x.shape
    NJ = wglu.shape[-1]

    def body(y_ref, w_ref, x_ref, vec_ref, out_ref, f_ref):
        g = _gelu(y_ref[...]).astype(BF16)
        f = jnp.concatenate([_dot(g, w_ref[j]) * _sigmoid(_dot(g, w_ref[2 + j])) for j in range(2)], axis=1)
        f_ref[...] = f
        out_ref[...] = _post_fwd(x_ref[...], f, vec_ref, 1.0)

    row = pl.BlockSpec((tm, D), lambda i: (i, 0))
    return pl.pallas_call(
        body, name=name, grid=(S // tm,),
        in_specs=[row, pl.BlockSpec((N_CHIPS, D, NJ), lambda i: (0, 0, 0)), row, pl.BlockSpec((8, D), lambda i: (0, 0))],
        out_specs=[row, row],
        out_shape=[jax.ShapeDtypeStruct((S, D), F32), jax.ShapeDtypeStruct((S, D), F32)],
        compiler_params=_cp(("arbitrary",)),
    )(y, wglu, x, vec)


def s5_glu_bwd(dout, f, y, vec, wglu, name, tm=256):
    S, D = dout.shape
    NJ = wglu.shape[-1]

    def body(dout_ref, f_ref, y_ref, vec_ref, w_ref, dy_ref, dab_ref, g_ref, acc_ref):
        @pl.when(pl.program_id(0) == 0)
        def _():
            acc_ref[...] = jnp.zeros_like(acc_ref)

        df, dgate, dgpost = _post_bwd(dout_ref[...], f_ref[...], vec_ref, 1.0)
        yv = y_ref[...]
        g = _gelu(yv).astype(BF16)
        g_ref[...] = g
        dg = jnp.zeros((tm, D), F32)
        for j in range(2):
            a = _dot(g, w_ref[j])
            sig = _sigmoid(_dot(g, w_ref[2 + j]))
            dfj = df[:, j * NJ:(j + 1) * NJ]
            da = (dfj * sig).astype(BF16)
            db = (dfj * a * sig * (1.0 - sig)).astype(BF16)
            dab_ref[:, j * NJ:(j + 1) * NJ] = da
            dab_ref[:, (2 + j) * NJ:(3 + j) * NJ] = db
            dg = dg + _dot_nt(da, w_ref[j]) + _dot_nt(db, w_ref[2 + j])
        dy_ref[...] = dg * _gelu_grad(yv)
        _acc_add(acc_ref, A_GATE, dgate)
        _acc_add(acc_ref, A_GPOST, dgpost)

    row = pl.BlockSpec((tm, D), lambda i: (i, 0))
    const = pl.BlockSpec((8, D), lambda i: (0, 0))
    return pl.pallas_call(
        body, name=name, grid=(S // tm,),
        in_specs=[row, row, row, const, pl.BlockSpec((N_CHIPS, D, NJ), lambda i: (0, 0, 0))],
        out_specs=[row, pl.BlockSpec((tm, N_CHIPS * NJ), lambda i: (i, 0)), row, const],
        out_shape=[jax.ShapeDtypeStruct((S, D), F32), jax.ShapeDtypeStruct((S, N_CHIPS * NJ), BF16),
                   jax.ShapeDtypeStruct((S, D), BF16), jax.ShapeDtypeStruct((8, D), F32)],
        compiler_params=_cp(("arbitrary",)),
    )(dout, f, y, vec, wglu)


def loss_head(y, target, name, tm=256):
    S, D = y.shape

    def body(y_ref, t_ref, d_ref, l_ref):
        @pl.when(pl.program_id(0) == 0)
        def _():
            l_ref[...] = jnp.zeros_like(l_ref)

        err = y_ref[...] - t_ref[...]
        d_ref[...] = err * (1.0 / D)
        l_ref[...] += jnp.sum(_rowsum(err * err), axis=1, keepdims=True)

    row = pl.BlockSpec((tm, D), lambda i: (i, 0))
    return pl.pallas_call(
        body, name=name, grid=(S // tm,),
        in_specs=[row, row],
        out_specs=[row, pl.BlockSpec((1, 1), lambda i: (0, 0))],
        out_shape=[jax.ShapeDtypeStruct((S, D), F32), jax.ShapeDtypeStruct((1, 1), F32)],
        compiler_params=_cp(("arbitrary",)),
    )(y, target)


def zero_after(dep, name, shape=(8, 128)):
    def body(dep_ref, o_ref):
        o_ref[...] = jnp.zeros_like(o_ref)

    return pl.pallas_call(body, name=name, in_specs=[ANY], out_specs=pl.BlockSpec(memory_space=pltpu.VMEM),
                          out_shape=jax.ShapeDtypeStruct(shape, F32), compiler_params=_cp())(dep)


def ada_mod(c_all, ada_w, ada_b_shard, name, tn=768, dep=None):
    B, D = c_all.shape
    L, _, NS = ada_w.shape
    extra = [] if dep is None else [dep]

    def body(c_ref, w_ref, b_ref, *rest):
        cv = c_ref[...]
        cond = (cv * _sigmoid(cv)).astype(BF16)
        rest[-1][...] = _dot(cond, w_ref[...].astype(BF16)) + b_ref[...]

    return pl.pallas_call(
        body, name=name, grid=(L, NS // tn),
        in_specs=[pl.BlockSpec((B, D), lambda l, j: (0, 0)), pl.BlockSpec((None, D, tn), lambda l, j: (l, 0, j)),
                  pl.BlockSpec((None, 1, tn), lambda l, j: (l, 0, j))] + [ANY] * len(extra),
        out_specs=pl.BlockSpec((None, B, tn), lambda l, j: (l, 0, j)),
        out_shape=jax.ShapeDtypeStruct((L, B, NS), F32),
        compiler_params=_cp(("arbitrary", "arbitrary")),
    )(c_all, ada_w, ada_b_shard, *extra)


def ada_grad(c_all, dmod, name, tn=768):
    B, D = c_all.shape
    L, _, NS = dmod.shape

    def body(c_ref, d_ref, o_ref):
        cv = c_ref[...]
        cond = (cv * _sigmoid(cv)).astype(BF16)
        o_ref[...] = _dot_tn(cond, d_ref[...].astype(BF16))

    return pl.pallas_call(
        body, name=name, grid=(L, NS // tn),
        in_specs=[pl.BlockSpec((B, D), lambda l, j: (0, 0)), pl.BlockSpec((None, B, tn), lambda l, j: (l, 0, j))],
        out_specs=pl.BlockSpec((None, D, tn), lambda l, j: (l, 0, j)),
        out_shape=jax.ShapeDtypeStruct((L, D, NS), F32),
        compiler_params=_cp(("arbitrary", "arbitrary")),
    )(c_all, dmod)


def sum_leading(x, name):
    n, R, C = x.shape

    def body(x_ref, o_ref):
        acc = x_ref[0].astype(F32)
        for i in range(1, n):
            acc = acc + x_ref[i].astype(F32)
        o_ref[...] = acc

    tr = _row_tile(R, 64)
    return pl.pallas_call(
        body, name=name, grid=(R // tr,),
        in_specs=[pl.BlockSpec((n, tr, C), lambda i: (0, i, 0))],
        out_specs=pl.BlockSpec((tr, C), lambda i: (i, 0)),
        out_shape=jax.ShapeDtypeStruct((R, C), F32),
        compiler_params=_cp(("arbitrary",)),
    )(x)


def sum_gathered(land, own, me_arr, name):
    n, R, C = land.shape

    def body(me_ref, land_ref, own_ref, o_ref):
        acc = jnp.zeros(o_ref.shape, F32)
        for i in range(n):
            acc = acc + jnp.where(me_ref[0] == i, own_ref[...], land_ref[i]).astype(F32)
        o_ref[...] = acc

    tr = _row_tile(R, 64)
    grid_spec = pltpu.PrefetchScalarGridSpec(
        num_scalar_prefetch=1, grid=(R // tr,),
        in_specs=[pl.BlockSpec((n, tr, C), lambda i, me: (0, i, 0)), pl.BlockSpec((tr, C), lambda i, me: (i, 0))],
        out_specs=pl.BlockSpec((tr, C), lambda i, me: (i, 0)))
    return pl.pallas_call(
        body, name=name, grid_spec=grid_spec,
        out_shape=jax.ShapeDtypeStruct((R, C), F32),
        compiler_params=_cp(("arbitrary",)),
    )(me_arr, land, own)


def _row_tile(R, cap=512):
    if R <= cap:
        return R
    for cand in (512, 384, 352, 256, 128, 64, 32, 16, 8):
        if cand <= cap and R % cand == 0:
            return cand
    return R


def adamw(w, g, m, v, name, emit_grad=False):
    R, C = w.shape
    tr = _row_tile(R, 256 if C > 1024 else 512)
    bc1 = 1.0 - ADAM_B1 ** ADAM_STEP
    bc2 = 1.0 - ADAM_B2 ** ADAM_STEP
    n_out = 4 if emit_grad else 3

    def body(w_ref, g_ref, m_ref, v_ref, *outs):
        d_ref, nm_ref, nv_ref = outs[-3:]
        gv = g_ref[...]
        if emit_grad:
            outs[0][...] = gv
        nm = ADAM_B1 * m_ref[...] + (1.0 - ADAM_B1) * gv
        nv = ADAM_B2 * v_ref[...] + (1.0 - ADAM_B2) * (gv * gv)
        nm_ref[...] = nm
        nv_ref[...] = nv
        d_ref[...] = -ADAM_LR * ((nm / bc1) / (jnp.sqrt(nv / bc2) + ADAM_EPS) + ADAM_WD * w_ref[...])

    blk = pl.BlockSpec((tr, C), lambda i: (i, 0))
    sd = jax.ShapeDtypeStruct((R, C), F32)
    return pl.pallas_call(
        body, name=name, grid=(R // tr,),
        in_specs=[blk, blk, blk, blk], out_specs=[blk] * n_out, out_shape=[sd] * n_out,
        compiler_params=_cp(("arbitrary",)),
    )(w, g, m, v)


def _as2d(a):
    if a.ndim == 1:
        return a.reshape(1, -1)
    return a.reshape(-1, a.shape[-1])


SMALL_PARAM_ELEMS = 1 << 18


def adamw_nd(w, g, m, v, name, emit_grad=False):
    if w.size <= SMALL_PARAM_ELEMS and not emit_grad:
        bc1 = 1.0 - ADAM_B1 ** ADAM_STEP
        bc2 = 1.0 - ADAM_B2 ** ADAM_STEP

        def body(w_ref, g_ref, m_ref, v_ref, d_ref, nm_ref, nv_ref):
            gv = g_ref[...]
            nm = ADAM_B1 * m_ref[...] + (1.0 - ADAM_B1) * gv
            nv = ADAM_B2 * v_ref[...] + (1.0 - ADAM_B2) * (gv * gv)
            nm_ref[...] = nm
            nv_ref[...] = nv
            d_ref[...] = -ADAM_LR * ((nm / bc1) / (jnp.sqrt(nv / bc2) + ADAM_EPS) + ADAM_WD * w_ref[...])

        sd = jax.ShapeDtypeStruct(w.shape, F32)
        whole = pl.BlockSpec(memory_space=pltpu.VMEM)
        return tuple(pl.pallas_call(body, name=name, in_specs=[whole] * 4, out_specs=[whole] * 3,
                                    out_shape=[sd, sd, sd], compiler_params=_cp())(w, g.reshape(w.shape), m, v))
    outs = adamw(_as2d(w), _as2d(g.reshape(w.shape)), _as2d(m), _as2d(v), name, emit_grad=emit_grad)
    return tuple(o.reshape(w.shape) for o in outs)


def _place():
    x, y, c = lax.axis_index("x"), lax.axis_index("y"), lax.axis_index("c")
    chips = [(1 - x, y), (x, 1 - y), (1 - x, 1 - y)]
    return x, y, c, chips


def allgather_small(xs, name):
    m_per, n = xs.shape

    def body(x_ref, out_ref, send_sems, recv_sems, local_sem):
        x, y, c, chips = _place()
        me, sibling = (x, y, c), (x, y, 1 - c)

        def rows(px, py, pc):
            return out_ref.at[pl.ds((4 * px + 2 * py + pc) * m_per, m_per), :]

        def copy(k, block, to, src=None):
            return pltpu.make_async_remote_copy(
                src_ref=rows(*block) if src is None else src, dst_ref=rows(*block),
                send_sem=send_sems.at[k], recv_sem=recv_sems.at[k], device_id=to, device_id_type=MESH_T)

        mine = pltpu.make_async_copy(x_ref, rows(*me), local_sem)
        mine.start()
        first = [copy(0, me, sibling, src=x_ref)]
        first += [copy(1 + j, me, (*chip, c), src=x_ref) for j, chip in enumerate(chips)]
        for cp in first:
            cp.start()
        passed = [copy(4 + j, (*chip, c), sibling) for j, chip in enumerate(chips)]
        for j, chip in enumerate(chips):
            copy(1 + j, (*chip, c), me).wait_recv()
            passed[j].start()
        copy(0, sibling, me).wait_recv()
        for j, chip in enumerate(chips):
            copy(4 + j, (*chip, 1 - c), me).wait_recv()
        for cp in first + passed:
            cp.wait_send()
        mine.wait()

    return pl.pallas_call(
        body, name=name,
        out_shape=jax.ShapeDtypeStruct((N_DEV * m_per, n), xs.dtype),
        in_specs=[pl.BlockSpec(memory_space=pltpu.VMEM)],
        out_specs=pl.BlockSpec(memory_space=pltpu.VMEM),
        scratch_shapes=[pltpu.SemaphoreType.DMA((7,)), pltpu.SemaphoreType.DMA((7,)), pltpu.SemaphoreType.DMA],
        compiler_params=_cp(),
    )(xs)


def gather_weights(shards, name):
    n = len(shards)

    def body(*refs):
        ins, outs = refs[:n], refs[n:2 * n]
        send_sems, recv_sems, fsend_sems, frecv_sems, local_sems = refs[2 * n:]
        x, y, c, chips = _place()
        me_chip = 2 * x + y
        sibling = (x, y, 1 - c)
        locs = []
        for a in range(n):
            cp = pltpu.make_async_copy(ins[a], outs[a].at[me_chip], local_sems.at[a])
            cp.start()
            locs.append(cp)

        def ici(a, k, src_chip, to):
            return pltpu.make_async_remote_copy(
                src_ref=ins[a].at[c], dst_ref=outs[a].at[src_chip, c],
                send_sem=send_sems.at[a * 3 + k], recv_sem=recv_sems.at[a * 3 + k], device_id=to, device_id_type=MESH_T)

        def d2d(a, k, src_chip, half):
            return pltpu.make_async_remote_copy(
                src_ref=outs[a].at[src_chip, half], dst_ref=outs[a].at[src_chip, half],
                send_sem=fsend_sems.at[a * 3 + k], recv_sem=frecv_sems.at[a * 3 + k],
                device_id=sibling, device_id_type=MESH_T)

        firsts = []
        for k, (cx, cy) in enumerate(chips):
            for a in range(n):
                cp = ici(a, k, me_chip, (cx, cy, c))
                cp.start()
                firsts.append(cp)
        passed = []
        for k, (cx, cy) in enumerate(chips):
            for a in range(n):
                ici(a, k, 2 * cx + cy, (cx, cy, c)).wait_recv()
                cp = d2d(a, k, 2 * cx + cy, c)
                cp.start()
                passed.append(cp)
        for k, (cx, cy) in enumerate(chips):
            for a in range(n):
                d2d(a, k, 2 * cx + cy, 1 - c).wait_recv()
        for cp in firsts + passed:
            cp.wait_send()
        for cp in locs:
            cp.wait()

    return pl.pallas_call(
        body, name=name,
        out_shape=[jax.ShapeDtypeStruct((N_CHIPS,) + s.shape, s.dtype) for s in shards],
        in_specs=[ANY] * n, out_specs=[ANY] * n,
        scratch_shapes=[pltpu.SemaphoreType.DMA((3 * n,)), pltpu.SemaphoreType.DMA((3 * n,)),
                        pltpu.SemaphoreType.DMA((3 * n,)), pltpu.SemaphoreType.DMA((3 * n,)),
                        pltpu.SemaphoreType.DMA((n,))],
        compiler_params=_cp(),
    )(*shards)


def _half_rows(ref, half, rh):
    idx = (slice(None),) * (len(ref.shape) - 2) + (pl.ds(pl.multiple_of(half * rh, 16), rh), slice(None))
    return ref.at[idx]


def sibling_exchange_halves(grads, name):
    n = len(grads)

    def body(*refs):
        ins, outs = refs[:n], refs[n:2 * n]
        send_sems, recv_sems = refs[2 * n:]
        x, y, c, _ = _place()
        cps = []
        for a in range(n):
            rh = ins[a].shape[-2] // 2
            cp = pltpu.make_async_remote_copy(
                src_ref=_half_rows(ins[a], 1 - c, rh), dst_ref=outs[a],
                send_sem=send_sems.at[a], recv_sem=recv_sems.at[a], device_id=(x, y, 1 - c), device_id_type=MESH_T)
            cp.start()
            cps.append(cp)
        for cp in cps:
            cp.wait()

    return pl.pallas_call(
        body, name=name,
        out_shape=[jax.ShapeDtypeStruct(g.shape[:-2] + (g.shape[-2] // 2, g.shape[-1]), g.dtype) for g in grads],
        in_specs=[ANY] * n, out_specs=[ANY] * n,
        scratch_shapes=[pltpu.SemaphoreType.DMA((n,)), pltpu.SemaphoreType.DMA((n,))],
        compiler_params=_cp(),
    )(*grads)


def pair_sum(g, recv, cidx, name):
    L, NS, R, C = g.shape
    rh = R // 2
    tr = _row_tile(rh, 256 if C > 1024 else 512)
    nt = rh // tr

    def body(c_ref, g_ref, r_ref, o_ref):
        o_ref[...] = (g_ref[...].astype(F32) + r_ref[...].astype(F32)).astype(BF16)

    grid_spec = pltpu.PrefetchScalarGridSpec(
        num_scalar_prefetch=1, grid=(L, NS, nt),
        in_specs=[pl.BlockSpec((None, None, tr, C), lambda l, s, t, c: (l, s, c[0] * nt + t, 0)),
                  pl.BlockSpec((None, None, tr, C), lambda l, s, t, c: (l, s, t, 0))],
        out_specs=pl.BlockSpec((None, None, tr, C), lambda l, s, t, c: (l, s, t, 0)))
    return pl.pallas_call(
        body, name=name, grid_spec=grid_spec,
        out_shape=jax.ShapeDtypeStruct((L, NS, rh, C), BF16),
        compiler_params=_cp(("arbitrary", "arbitrary", "arbitrary")),
    )(cidx, g, recv)


def chip_exchange(parts, name):
    n = len(parts)

    def body(*refs):
        ins, outs = refs[:n], refs[n:2 * n]
        send_sems, recv_sems = refs[2 * n:]
        x, y, c, chips = _place()
        cps = []
        for k, (cx, cy) in enumerate(chips):
            for a in range(n):
                cp = pltpu.make_async_remote_copy(
                    src_ref=ins[a].at[:, 2 * cx + cy], dst_ref=outs[a].at[k],
                    send_sem=send_sems.at[a * 3 + k], recv_sem=recv_sems.at[a * 3 + k],
                    device_id=(cx, cy, c), device_id_type=MESH_T)
                cp.start()
                cps.append(cp)
        for cp in cps:
            cp.wait()

    return pl.pallas_call(
        body, name=name,
        out_shape=[jax.ShapeDtypeStruct((3, p.shape[0]) + p.shape[2:], p.dtype) for p in parts],
        in_specs=[ANY] * n, out_specs=[ANY] * n,
        scratch_shapes=[pltpu.SemaphoreType.DMA((3 * n,)), pltpu.SemaphoreType.DMA((3 * n,))],
        compiler_params=_cp(),
    )(*parts)


def chip_sum(part, recv, chip_c, name, dest=None, slot=0, n_slots=1):
    _, NS, RH, C = part.shape
    tr = _row_tile(RH, 256 if C > 1024 else 512)
    nt = RH // tr

    def body(cc_ref, p_ref, r_ref, *rest):
        o_ref = rest[-1]
        acc = p_ref[...].astype(F32)
        for k in range(3):
            acc = acc + r_ref[k].astype(F32)
        o_ref[...] = acc

    in_specs = [pl.BlockSpec((None, None, tr, C), lambda t, cc: (0, cc[0], t, 0)),
                pl.BlockSpec((3, None, tr, C), lambda t, cc: (0, 0, t, 0))]
    args = [chip_c, part, recv]
    aliases = {}
    if dest is not None:
        in_specs.append(ANY)
        args.append(dest)
        aliases = {3: 0}
    grid_spec = pltpu.PrefetchScalarGridSpec(
        num_scalar_prefetch=1, grid=(nt,), in_specs=in_specs,
        out_specs=pl.BlockSpec((None, tr, C), lambda t, cc: (slot, cc[1] * nt + t, 0)))
    return pl.pallas_call(
        body, name=name, grid_spec=grid_spec,
        out_shape=jax.ShapeDtypeStruct((n_slots, 2 * RH, C), F32),
        input_output_aliases=aliases,
        compiler_params=_cp(("arbitrary",)),
    )(*args)


def sibling_fill_halves(bufs, name):
    n = len(bufs)

    def body(*refs):
        outs = refs[n:2 * n]
        send_sems, recv_sems = refs[2 * n:]
        x, y, c, _ = _place()
        cps = []
        for a in range(n):
            rh = outs[a].shape[-2] // 2
            cp = pltpu.make_async_remote_copy(
                src_ref=_half_rows(outs[a], c, rh), dst_ref=_half_rows(outs[a], c, rh),
                send_sem=send_sems.at[a], recv_sem=recv_sems.at[a], device_id=(x, y, 1 - c), device_id_type=MESH_T)
            cp.start()
            cps.append(cp)
        for a, cp in enumerate(cps):
            cp.wait_send()
        for a in range(n):
            rh = outs[a].shape[-2] // 2
            pltpu.make_async_remote_copy(
                src_ref=_half_rows(outs[a], 1 - c, rh), dst_ref=_half_rows(outs[a], 1 - c, rh),
                send_sem=send_sems.at[a], recv_sem=recv_sems.at[a], device_id=(x, y, 1 - c),
                device_id_type=MESH_T).wait_recv()

    return pl.pallas_call(
        body, name=name,
        out_shape=[jax.ShapeDtypeStruct(b.shape, b.dtype) for b in bufs],
        in_specs=[ANY] * n, out_specs=[ANY] * n,
        input_output_aliases={a: a for a in range(n)},
        scratch_shapes=[pltpu.SemaphoreType.DMA((n,)), pltpu.SemaphoreType.DMA((n,))],
        compiler_params=_cp(),
    )(*bufs)


HBM_SPEC = pl.BlockSpec(memory_space=pltpu.HBM)
SEM_SPEC = pl.BlockSpec(memory_space=pltpu.SEMAPHORE)
DATAFLOW_EFFECT = pltpu.SideEffectType.DATAFLOW_SIDE_EFFECTING


def split_start(bufs, copies_fn, n_copies, name, dep=None, token_shape=(8, 128)):
    nb = len(bufs)
    extra = [] if dep is None else [dep]

    def body(*refs):
        ne = len(extra)
        send_sems, recv_sems = refs[nb + ne], refs[nb + ne + 1]
        token = refs[-1]
        for cp in copies_fn(refs[:nb], send_sems, recv_sems):
            cp.start()
        token[...] = jnp.zeros_like(token)

    outs = pl.pallas_call(
        body, name=name,
        out_shape=(pltpu.SemaphoreType.DMA((n_copies,)), pltpu.SemaphoreType.DMA((n_copies,)),
                   *[pltpu.HBM(b.shape, b.dtype) for b in bufs], jax.ShapeDtypeStruct(token_shape, F32)),
        in_specs=[HBM_SPEC] * nb + [ANY] * len(extra),
        out_specs=(SEM_SPEC, SEM_SPEC, *[HBM_SPEC] * nb, pl.BlockSpec(memory_space=pltpu.VMEM)),
        input_output_aliases={i: 2 + i for i in range(nb)},
        compiler_params=pltpu.CompilerParams(has_side_effects=DATAFLOW_EFFECT),
    )(*[pltpu.with_memory_space_constraint(b, pltpu.HBM) for b in bufs], *extra)
    return outs[0], outs[1], list(outs[2:2 + nb]), outs[-1]


def split_wait(send_sems, recv_sems, bufs, after, copies_fn, name):
    nb = len(bufs)

    def body(*refs):
        for cp in copies_fn(refs[:nb], refs[nb], refs[nb + 1]):
            cp.wait_send()
            cp.wait_recv()

    outs = pl.pallas_call(
        body, name=name,
        out_shape=tuple(pltpu.HBM(b.shape, b.dtype) for b in bufs),
        in_specs=[HBM_SPEC] * nb + [SEM_SPEC, SEM_SPEC, ANY],
        out_specs=tuple([HBM_SPEC] * nb),
        input_output_aliases={i: i for i in range(nb)},
        compiler_params=pltpu.CompilerParams(has_side_effects=DATAFLOW_EFFECT),
    )(*bufs, send_sems, recv_sems, after)
    return list(outs)


def chip_exchange_copies(refs, send_sems, recv_sems):
    n = len(refs) // 2
    x, y, c, chips = _place()
    cps = []
    for a in range(n):
        for k, (cx, cy) in enumerate(chips):
            cps.append(pltpu.make_async_remote_copy(
                src_ref=refs[a].at[:, 2 * cx + cy], dst_ref=refs[n + a].at[k],
                send_sem=send_sems.at[a * 3 + k], recv_sem=recv_sems.at[a * 3 + k],
                device_id=(cx, cy, c), device_id_type=MESH_T))
    return cps


def allgather_direct_copies(refs, send_sems, recv_sems):
    x, y, c, _ = _place()
    me = 4 * x + 2 * y + c
    cps = []
    for r in range(1, N_DEV):
        fx, fy, fc = (r >> 2) & 1, (r >> 1) & 1, r & 1
        peer = (1 - x if fx else x, 1 - y if fy else y, 1 - c if fc else c)
        cps.append(pltpu.make_async_remote_copy(
            src_ref=refs[0], dst_ref=refs[1].at[me], send_sem=send_sems.at[r - 1], recv_sem=recv_sems.at[r - 1],
            device_id=peer, device_id_type=MESH_T))
    return cps


def pair_exchange_copies(refs, send_sems, recv_sems):
    n = len(refs) // 2
    x, y, c, _ = _place()
    return [pltpu.make_async_remote_copy(
        src_ref=_half_rows(refs[a], 1 - c, refs[a].shape[-2] // 2), dst_ref=refs[n + a],
        send_sem=send_sems.at[a], recv_sem=recv_sems.at[a], device_id=(x, y, 1 - c), device_id_type=MESH_T)
        for a in range(n)]


def weight_ici_copies(refs, send_sems, recv_sems):
    x, y, c, chips = _place()
    cps = []
    for a in range(len(refs)):
        for k, (cx, cy) in enumerate(chips):
            cps.append(pltpu.make_async_remote_copy(
                src_ref=refs[a].at[2 * x + y, c], dst_ref=refs[a].at[2 * x + y, c],
                send_sem=send_sems.at[a * 3 + k], recv_sem=recv_sems.at[a * 3 + k],
                device_id=(cx, cy, c), device_id_type=MESH_T))
    return cps


def weight_d2d_copies(refs, send_sems, recv_sems):
    x, y, c, chips = _place()
    cps = []
    for a in range(len(refs)):
        for k, (cx, cy) in enumerate(chips):
            cps.append(pltpu.make_async_remote_copy(
                src_ref=refs[a].at[2 * cx + cy, c], dst_ref=refs[a].at[2 * cx + cy, c],
                send_sem=send_sems.at[a * 3 + k], recv_sem=recv_sems.at[a * 3 + k],
                device_id=(x, y, 1 - c), device_id_type=MESH_T))
    return cps


def cast_place(w, lead, chip_arr, name, dep=None):
    R, C = w.shape[-2:]
    tr = _row_tile(R, 256 if C > 1024 else 512)

    def body(chip_ref, w_ref, *rest):
        rest[-1][...] = w_ref[...].astype(BF16)

    in_specs = [pl.BlockSpec((None,) * len(lead) + (tr, C), lambda t, ch: (*lead, t, 0))]
    args = [chip_arr, w]
    if dep is not None:
        in_specs.append(ANY)
        args.append(dep)
    grid_spec = pltpu.PrefetchScalarGridSpec(
        num_scalar_prefetch=1, grid=(R // tr,), in_specs=in_specs,
        out_specs=pl.BlockSpec((None, tr, C), lambda t, ch: (ch[0], t, 0)))
    return pl.pallas_call(
        body, name=name, grid_spec=grid_spec,
        out_shape=jax.ShapeDtypeStruct((N_CHIPS, R, C), BF16),
        compiler_params=_cp(("arbitrary",)),
    )(*args)


def _s5_prepare(lam_re, lam_im, log_dt, b_re, b_im, c_re, c_im, groups_per_block):
    G, P = lam_re.shape
    N = b_re.shape[-1]
    gb = groups_per_block
    nblk = G // gb
    dt = jnp.exp(log_dt)[:, None]
    e = jnp.exp(lam_re * dt)
    a_re = e * jnp.cos(lam_im * dt)
    a_im = e * jnp.sin(lam_im * dt)
    n2 = lam_re * lam_re + lam_im * lam_im
    co_re = ((a_re - 1.0) * lam_re + a_im * lam_im) / n2
    co_im = (a_im * lam_re - (a_re - 1.0) * lam_im) / n2
    bb_re = co_re[..., None] * b_re - co_im[..., None] * b_im
    bb_im = co_re[..., None] * b_im + co_im[..., None] * b_re
    eye = jnp.eye(gb, dtype=F32)

    def blockdiag_np(m):
        m = m.reshape(nblk, gb, N, P)
        return jnp.einsum('bgnp,gh->bgnhp', m, eye).reshape(nblk, gb * N, gb * P)

    b_np_re = jnp.swapaxes(bb_re, 1, 2)
    b_np_im = jnp.swapaxes(bb_im, 1, 2)
    return (a_re.reshape(nblk, 1, gb * P), a_im.reshape(nblk, 1, gb * P),
            blockdiag_np(b_np_re), blockdiag_np(b_np_im), blockdiag_np(c_re), blockdiag_np(c_im))


def _to_scan_order(a):
    S, D = a.shape
    return a.reshape(SCAN_LANES, S // SCAN_LANES, D).transpose(1, 0, 2).reshape(S, D)


def _from_scan_order(a):
    S, D = a.shape
    return a.reshape(S // SCAN_LANES, SCAN_LANES, D).transpose(1, 0, 2).reshape(S, D)


def _pad_rows(a, mult=8):
    r = (-a.shape[0]) % mult
    if r:
        a = jnp.concatenate([a, jnp.zeros((r, a.shape[1]), a.dtype)], axis=0)
    return a


def _pack_rows(arrs, width):
    parts, offs, o = [], [], 0
    for a in arrs:
        flat = a.reshape(-1)
        r = (-flat.shape[0]) % (16 * width)
        if r:
            flat = jnp.concatenate([flat, jnp.zeros((r,), flat.dtype)])
        p = flat.reshape(-1, width)
        parts.append(p)
        offs.append((o, a.shape, a.size))
        o += p.shape[0]
    if o % 64:
        parts.append(jnp.zeros((64 - o % 64, width), parts[0].dtype))
    return jnp.concatenate(parts, axis=0), offs


def _unpack_rows(packed, offs):
    outs = []
    for o, shape, size in offs:
        rows = -(-size // packed.shape[1])
        outs.append(packed[o:o + rows].reshape(-1)[:size].reshape(shape))
    return outs


def kernel(x, c, ada_w, ada_b, norm_pre, norm_post, ffn_w_in, ffn_w_out, ab_w_in, pool_w, pool_scale, sgu_ln_g, sgu_ln_b, sgu_w, sgu_b, ab_w_out, ssm_w_in, ssm_lam_re, ssm_lam_im, ssm_b_re, ssm_b_im, ssm_c_re, ssm_c_im, ssm_d, ssm_log_dt, ssm_w_glu, loss_target, m_ada_w, m_ada_b, m_norm_pre, m_norm_post, m_ffn_w_in, m_ffn_w_out, m_ab_w_in, m_pool_w, m_pool_scale, m_sgu_ln_g, m_sgu_ln_b, m_sgu_w, m_sgu_b, m_ab_w_out, m_ssm_w_in, m_ssm_lam_re, m_ssm_lam_im, m_ssm_b_re, m_ssm_b_im, m_ssm_c_re, m_ssm_c_im, m_ssm_d, m_ssm_log_dt, m_ssm_w_glu, v_ada_w, v_ada_b, v_norm_pre, v_norm_post, v_ffn_w_in, v_ffn_w_out, v_ab_w_in, v_pool_w, v_pool_scale, v_sgu_ln_g, v_sgu_ln_b, v_sgu_w, v_sgu_b, v_ab_w_out, v_ssm_w_in, v_ssm_lam_re, v_ssm_lam_im, v_ssm_b_re, v_ssm_b_im, v_ssm_c_re, v_ssm_c_im, v_ssm_d, v_ssm_log_dt, v_ssm_w_glu):
    weights = dict(ada_w=ada_w, ada_b=ada_b, norm_pre=norm_pre, norm_post=norm_post, ffn_w_in=ffn_w_in,
                   ffn_w_out=ffn_w_out, ab_w_in=ab_w_in, pool_w=pool_w, pool_scale=pool_scale, sgu_ln_g=sgu_ln_g,
                   sgu_ln_b=sgu_ln_b, sgu_w=sgu_w, sgu_b=sgu_b, ab_w_out=ab_w_out, ssm_w_in=ssm_w_in,
                   ssm_lam_re=ssm_lam_re, ssm_lam_im=ssm_lam_im, ssm_b_re=ssm_b_re, ssm_b_im=ssm_b_im,
                   ssm_c_re=ssm_c_re, ssm_c_im=ssm_c_im, ssm_d=ssm_d, ssm_log_dt=ssm_log_dt, ssm_w_glu=ssm_w_glu)
    m_in = dict(ada_w=m_ada_w, ada_b=m_ada_b, norm_pre=m_norm_pre, norm_post=m_norm_post, ffn_w_in=m_ffn_w_in,
                ffn_w_out=m_ffn_w_out, ab_w_in=m_ab_w_in, pool_w=m_pool_w, pool_scale=m_pool_scale,
                sgu_ln_g=m_sgu_ln_g, sgu_ln_b=m_sgu_ln_b, sgu_w=m_sgu_w, sgu_b=m_sgu_b, ab_w_out=m_ab_w_out,
                ssm_w_in=m_ssm_w_in, ssm_lam_re=m_ssm_lam_re, ssm_lam_im=m_ssm_lam_im, ssm_b_re=m_ssm_b_re,
                ssm_b_im=m_ssm_b_im, ssm_c_re=m_ssm_c_re, ssm_c_im=m_ssm_c_im, ssm_d=m_ssm_d,
                ssm_log_dt=m_ssm_log_dt, ssm_w_glu=m_ssm_w_glu)
    v_in = dict(ada_w=v_ada_w, ada_b=v_ada_b, norm_pre=v_norm_pre, norm_post=v_norm_post, ffn_w_in=v_ffn_w_in,
                ffn_w_out=v_ffn_w_out, ab_w_in=v_ab_w_in, pool_w=v_pool_w, pool_scale=v_pool_scale,
                sgu_ln_g=v_sgu_ln_g, sgu_ln_b=v_sgu_ln_b, sgu_w=v_sgu_w, sgu_b=v_sgu_b, ab_w_out=v_ab_w_out,
                ssm_w_in=v_ssm_w_in, ssm_lam_re=v_ssm_lam_re, ssm_lam_im=v_ssm_lam_im, ssm_b_re=v_ssm_b_re,
                ssm_b_im=v_ssm_b_im, ssm_c_re=v_ssm_c_re, ssm_c_im=v_ssm_c_im, ssm_d=v_ssm_d,
                ssm_log_dt=v_ssm_log_dt, ssm_w_glu=v_ssm_w_glu)
    names = list(weights.keys())

    xi, yi, ci = lax.axis_index("x"), lax.axis_index("y"), lax.axis_index("c")
    chip = 2 * xi + yi
    me = 4 * xi + 2 * yi + ci
    S, D = x.shape[1], x.shape[2]
    L = ada_w.shape[0]
    NSUB = norm_pre.shape[1]
    DS = norm_pre.shape[2]
    FS = ffn_w_in.shape[-1]
    FR = ffn_w_out.shape[-2]
    x0 = x[0]
    target = loss_target[0]

    small_parts = [_pad_rows(p) for p in (c.reshape(D // DS, DS), norm_pre.reshape(L * NSUB, DS),
                                          norm_post.reshape(L * NSUB, DS), ssm_d.reshape(1, DS))]
    small_offs = [0]
    for p in small_parts:
        small_offs.append(small_offs[-1] + p.shape[0])
    small_all = allgather_small(jnp.concatenate(small_parts, axis=0), "ag_small").reshape(N_DEV, -1, DS)
    c_all = small_all[:, :D // DS].reshape(N_DEV, D)
    per_chip = small_all[0::2]
    o = small_offs[1]
    g_pre_full = jnp.moveaxis(per_chip[:, o:o + L * NSUB], 0, 1).reshape(L, NSUB, D)
    o = small_offs[2]
    g_post_full = jnp.moveaxis(per_chip[:, o:o + L * NSUB], 0, 1).reshape(L, NSUB, D)
    o = small_offs[3]
    d_full = jnp.moveaxis(per_chip[:, o:o + 1], 0, 1).reshape(1, D)

    pieces = []
    for l in range(L):
        pieces.append((f"ffn_{l}_0", [(ffn_w_in, (l, 0)), (ffn_w_out, (l, 0))]))
        if l % 2 == 0:
            pieces.append((f"mix_{l}", [(ab_w_in, (l // 2,)), (ab_w_out, (l // 2,))]))
        else:
            pieces.append((f"mix_{l}", [(ssm_w_in, (l // 2,)), (ssm_w_glu, (l // 2,))]))
        pieces.append((f"ffn_{l}_1", [(ffn_w_in, (l, 1)), (ffn_w_out, (l, 1))]))
    tags = [tag for tag, _ in pieces]
    chip_arr = chip.reshape(1).astype(jnp.int32)
    wg_started, wg_passing = {}, {}

    def cast_piece(tag, ws, dep):
        wg_started[tag] = [cast_place(w, lead, chip_arr, f"wg_cast_{tag}_{i}", dep=dep).reshape(
            N_CHIPS, 2, w.shape[-2] // 2, w.shape[-1]) for i, (w, lead) in enumerate(ws)]

    def start_piece(tag, dep):
        bufs = wg_started[tag]
        wg_started[tag] = split_start(bufs, weight_ici_copies, 3 * len(bufs), f"wg_ici_start_{tag}", dep=dep,
                                      token_shape=(8, D))
        return wg_started[tag][3]

    def weights_arrived(i, after):
        ssem, rsem, bufs, _ = wg_started[tags[i]]
        lands = split_wait(ssem, rsem, bufs, after, weight_ici_copies, f"wg_ici_wait_{tags[i]}")
        wg_passing[i] = split_start(lands, weight_d2d_copies, 3 * len(lands), f"wg_d2d_start_{tags[i]}")
        return wg_passing[i][3][0, 0]

    def weights_of(i, after):
        ssem, rsem, lands, _ = wg_passing[i]
        return split_wait(ssem, rsem, lands, after, weight_d2d_copies, f"wg_d2d_wait_{tags[i]}")

    NS = ada_w.shape[-1]
    ada_b_shard = lax.dynamic_slice_in_dim(ada_b, chip * NS, NS, axis=1).reshape(L, 1, NS)
    mod_part = ada_mod(c_all, ada_w, ada_b_shard, "ada_mod")
    mod_all = allgather_small(mod_part.reshape(L * N_DEV, NS), "ag_mod").reshape(N_DEV, L, N_DEV, NS)
    mod_mine = lax.dynamic_index_in_dim(mod_all[0::2], me, axis=2, keepdims=False)
    mod = jnp.moveaxis(mod_mine, 0, 1).reshape(L, NSUB, 3, D)
    cast_piece(*pieces[0], mod_all)
    first_token = start_piece(tags[0], mod_all)
    wg_tokens = [first_token]
    for tag, ws in pieces[1:]:
        cast_piece(tag, ws, first_token)
    for tag in tags[1:WG_AHEAD]:
        wg_tokens.append(start_piece(tag, first_token))

    def vec_of(l, s):
        return jnp.concatenate([mod[l, s], g_pre_full[l, s][None], g_post_full[l, s][None],
                                jnp.zeros((3, D), F32)], axis=0)

    GB = 8
    s5_args = (ssm_lam_re[0], ssm_lam_im[0], ssm_log_dt[0], ssm_b_re[0], ssm_b_im[0], ssm_c_re[0], ssm_c_im[0])
    (a_re, a_im, bblk_re, bblk_im, cblk_re, cblk_im), s5_vjp = jax.vjp(lambda *p: _s5_prepare(*p, GB), *s5_args)
    bre16, bim16 = bblk_re.astype(BF16), bblk_im.astype(BF16)
    cre16, cim16 = cblk_re.astype(BF16), cblk_im.astype(BF16)
    cre16_t, cim16_t = jnp.swapaxes(cre16, 1, 2), jnp.swapaxes(cim16, 1, 2)

    pool_w16 = pool_w[0].astype(BF16)
    sgu_bexp = jnp.broadcast_to(sgu_b[0][:, :, None], sgu_w[0].shape)

    saved = {}
    ffn_w = {}
    xcur = x0
    stage = [0]

    def next_weights(after, vec):
        i = stage[0]
        stage[0] += 1
        weights_arrived(i, after)
        if i + WG_AHEAD < len(tags):
            vec = vec + start_piece(tags[i + WG_AHEAD], wg_passing[i][3])
        return weights_of(i, after), vec

    for l in range(L):
        v0 = vec_of(l, 0)
        g, v0p = next_weights(xcur if l else sum(wg_tokens, v0), v0)
        ffn_w[(l, 0)] = (g[0].reshape(N_CHIPS, D, FS), g[1].reshape(N_CHIPS, FR, D))
        out, h, u, f = ffn_fwd(xcur, v0p, *ffn_w[(l, 0)], f"ffn_fwd_{l}_0")
        saved[(l, 0)] = (xcur, v0, h, u, f)
        xcur = out
        v1 = vec_of(l, 1)
        mix_g, v1p = next_weights(xcur, v1)
        if l % 2 == 0:
            abin_g = mix_g[0].reshape(N_CHIPS, D, -1)
            about_g = mix_g[1].reshape(-1, D)
            h, z = pre_matmul(xcur, v1p, abin_g, f"mixa_in_{l}")
            ycat = mixa_core_fwd(z, pool_w16, pool_scale, sgu_ln_g, sgu_ln_b, sgu_w[0], sgu_bexp, f"mixa_core_{l}")
            out, f = matmul_post(ycat, about_g, xcur, v1, f"mixa_out_{l}")
            saved[(l, 1)] = (xcur, v1, h, z, ycat, f)
        else:
            sin_g = mix_g[0].reshape(1, -1, D)
            glu_g = mix_g[1].reshape(N_CHIPS, D, -1)
            h, uu = pre_matmul(xcur, v1p, sin_g, f"s5_in_{l}")
            us = _to_scan_order(uu)
            ys = s5_core_fwd(us, a_re, a_im, bre16, bim16, cre16_t, cim16_t, d_full, f"s5_core_{l}")
            yy = _from_scan_order(ys)
            out, f = s5_glu_fwd(yy, glu_g, xcur, v1, f"s5_glu_{l}")
            saved[(l, 1)] = (xcur, v1, h, us, yy, f)
        xcur = out
        v2 = vec_of(l, 2)
        g, v2p = next_weights(xcur, v2)
        ffn_w[(l, 1)] = (g[0].reshape(N_CHIPS, D, FS), g[1].reshape(N_CHIPS, FR, D))
        out, h, u, f = ffn_fwd(xcur, v2p, *ffn_w[(l, 1)], f"ffn_fwd_{l}_1")
        saved[(l, 2)] = (xcur, v2, h, u, f)
        xcur = out

    dcur, sq = loss_head(xcur, target, "loss_head")
    loss = lax.psum(sq[0, 0], ("x", "y", "c")) * (0.5 / D)

    accs = {}
    small_g = {}
    cidx = ci.reshape(1).astype(jnp.int32)
    chip_c = jnp.stack([chip, ci]).astype(jnp.int32)
    rs_open, rs_pending = [], []

    pin = [zero_after(loss.reshape(1, 1), "after_loss", (8, D))]
    last_start = [pin[0]]

    def pin_behind(token):
        pin.append(token)
        last_start[0] = token

    def pinned(v):
        for token in pin:
            v = v + (token if token.shape == v.shape else token[0, 0])
        pin.clear()
        return v

    def rs_begin(tag, items):
        arrs = [it[0] for it in items]
        lands = [lax.empty(a.shape[:-2] + (a.shape[-2] // 2, a.shape[-1]), BF16) for a in arrs]
        started = split_start(arrs + lands, pair_exchange_copies, len(arrs), f"rs_pair_start_{tag}",
                              token_shape=(8, D))
        rs_open.append((tag, items, started))
        pin_behind(started[3])

    def rs_advance(after):
        while rs_open:
            tag, items, (ssem, rsem, bufs, _) = rs_open.pop(0)
            n = len(items)
            bufs = split_wait(ssem, rsem, bufs, after, pair_exchange_copies, f"rs_pair_wait_{tag}")
            parts = [pair_sum(a, r, cidx, f"rs_pair_sum_{tag}_{i}") for i, (a, r) in enumerate(zip(bufs[:n], bufs[n:]))]
            lands = [lax.empty((3, 1) + p.shape[2:], BF16) for p in parts]
            started = split_start(parts + lands, chip_exchange_copies, 3 * n, f"rs_chip_start_{tag}",
                                  token_shape=(8, D))
            rs_pending.append((tag, items, started))
            pin_behind(started[3])

    def ffn_back(l, s, k, dcur, before_wgrads=None):
        xin, vv, h, u, f = saved[(l, s)]
        vv = pinned(vv)
        dx, df, du, act, acc = ffn_bwd(dcur, xin, f, u, vv, *ffn_w[(l, k)], f"ffn_bwd_{l}_{k}")
        accs[(l, s)] = acc
        rs_advance(dx)
        dep = None if before_wgrads is None else before_wgrads()
        g_win = tn_matmul(h, du, 1024, f"ffn_dwin_{l}_{k}", dep=dep)
        g_wout = tn_matmul_cols(act, df, FS, D, f"ffn_dwout_{l}_{k}")
        rs_begin(f"ffn_{l}_{k}", [(g_win[None], "ffn_w_in", 2 * l + k, 2 * L),
                                  (g_wout.reshape(1, N_CHIPS, FR, D), "ffn_w_out", 2 * l + k, 2 * L)])
        return dx

    grads = {}

    small_names = ["pool_w", "pool_scale", "sgu_ln_g", "sgu_ln_b", "sgu_w", "sgu_b", "ssm_lam_re", "ssm_lam_im",
                   "ssm_b_re", "ssm_b_im", "ssm_c_re", "ssm_c_im", "ssm_log_dt"]
    small_open = []

    def small_grads_start():
        acc_all = jnp.stack([jnp.stack([accs[(l, s)] for s in range(NSUB)]) for l in range(L)])
        dmod_mine = acc_all[:, :, A_SHIFT:A_GATE + 1].reshape(L, NSUB * 3 * D)
        packed, offs = _pack_rows([dmod_mine, acc_all[:, :, A_GPRE], acc_all[:, :, A_GPOST], dd_mine]
                                  + [small_g[n] for n in small_names], D)
        packed = pinned(packed).astype(BF16)
        started = split_start([packed, lax.empty((N_DEV,) + packed.shape, BF16)], allgather_direct_copies,
                              N_DEV - 1, "ag_grads_start")
        small_open.append((started, offs))
        pin_behind(started[3])
        return started[3]

    def small_grads_finish(after):
        (ssem, rsem, bufs, _), offs = small_open.pop()
        own, land = split_wait(ssem, rsem, bufs, after, allgather_direct_copies, "ag_grads_wait")
        me_arr = me.reshape(1).astype(jnp.int32)
        summed = _unpack_rows(sum_gathered(land, own, me_arr, "sum_small"), offs)
        grads.update({n: g for n, g in zip(small_names, summed[4:])})
        grads["ada_b"] = summed[0]
        grads["norm_pre"] = lax.dynamic_slice_in_dim(summed[1], chip * DS, DS, axis=2)
        grads["norm_post"] = lax.dynamic_slice_in_dim(summed[2], chip * DS, DS, axis=2)
        grads["ssm_d"] = lax.dynamic_slice_in_dim(summed[3], chip * DS, DS, axis=1)
        nm = L * NSUB * 3
        is_me = (jnp.arange(N_DEV) == me)[:, None, None]
        dmod_all = jnp.where(is_me, own[None, :nm], land[:, :nm]).reshape(N_DEV, L, NSUB * 3 * D)
        dmod_shard = lax.dynamic_slice_in_dim(jnp.moveaxis(dmod_all, 0, 1), chip * NS, NS, axis=2)
        grads["ada_w"] = ada_grad(c_all, dmod_shard, "ada_grad")

    for l in reversed(range(L)):
        dcur = ffn_back(l, 2, 1, dcur)
        if l % 2 == 0:
            xin, vv, h, z, ycat, f = saved[(l, 1)]
            vv = pinned(vv)
            df, dact, acc_post = post_bwd_matmul(dcur, f, vv, about_g, f"mixa_out_bwd_{l}")
            rs_advance(dact)
            g_about = tn_matmul_cols(ycat, df, 512, D, f"mixa_dwout_{l}")
            dz, dpw, dvecw, dsw, dsb = mixa_core_bwd(z, dact, pool_w16, pool_scale, sgu_ln_g, sgu_ln_b, sgu_w[0],
                                                     sgu_bexp, f"mixa_core_bwd_{l}")
            g_abin = tn_matmul_cols(h, dz, 512, abin_g.shape[-1], f"mixa_dwin_{l}")
            dcur, acc_pre = matmul_pre_bwd(dz, abin_g, xin, dcur, vv, f"mixa_in_bwd_{l}")
            accs[(l, 1)] = acc_pre + acc_post
            rs_begin(f"mixa_{l}", [(g_abin[None], "ab_w_in", 0, 1),
                                   (g_about.reshape(1, N_CHIPS, -1, D), "ab_w_out", 0, 1)])
            small_g.update(pool_w=dpw[None], pool_scale=dvecw[0:1], sgu_ln_g=dvecw[1:2], sgu_ln_b=dvecw[2:3],
                           sgu_w=dsw[None], sgu_b=dsb.T[None])
        else:
            xin, vv, h, us, yy, f = saved[(l, 1)]
            vv = pinned(vv)
            dy, dab, gact, acc_post = s5_glu_bwd(dcur, f, yy, vv, glu_g, f"s5_glu_bwd_{l}")
            rs_advance(dy)
            g_glu = tn_matmul_cols(gact, dab, 512, glu_g.shape[-1], f"s5_dwglu_{l}")
            dys = _to_scan_order(dy)
            dus, dbre, dbim, dcre_t, dcim_t, dar, dai, dd = s5_core_bwd(
                us, dys, a_re, a_im, bre16, bim16, cre16, cim16, d_full, f"s5_core_bwd_{l}")
            du = _from_scan_order(dus).astype(BF16)
            g_sin = tn_matmul_cols(h, du, 512, D, f"s5_dwin_{l}")
            dcur, acc_pre = matmul_pre_bwd(du, sin_g, xin, dcur, vv, f"s5_in_bwd_{l}")
            accs[(l, 1)] = acc_pre + acc_post
            rs_begin(f"s5_{l}", [(g_sin.reshape(1, N_CHIPS, -1, D), "ssm_w_in", 0, 1), (g_glu[None], "ssm_w_glu", 0, 1)])
            s5_grads = s5_vjp((dar, dai, dbre, dbim, jnp.swapaxes(dcre_t, 1, 2), jnp.swapaxes(dcim_t, 1, 2)))
            small_g.update(ssm_lam_re=s5_grads[0][None], ssm_lam_im=s5_grads[1][None], ssm_log_dt=s5_grads[2][None],
                           ssm_b_re=s5_grads[3][None], ssm_b_im=s5_grads[4][None], ssm_c_re=s5_grads[5][None],
                           ssm_c_im=s5_grads[6][None])
            dd_mine = dd
        dcur = ffn_back(l, 0, 0, dcur, before_wgrads=small_grads_start if l == 0 else None)
    grad_x = dcur[None]
    rs_advance(last_start[0])
    small_grads_finish(last_start[0])

    big_names = ["ffn_w_in", "ffn_w_out", "ab_w_in", "ab_w_out", "ssm_w_in", "ssm_w_glu"]
    deltas, new_m, new_v = {}, {}, {}

    def update(n):
        g = grads[n].reshape(weights[n].shape)
        outs = adamw_nd(weights[n], g, m_in[n], v_in[n], f"adamw_{n}", emit_grad=n in big_names)
        grads[n] = outs[0] if n in big_names else g
        deltas[n], new_m[n], new_v[n] = outs[-3:]

    for n in names:
        if n not in big_names:
            update(n)

    fin = {}
    after = pinned(deltas["ssm_log_dt"]) + deltas["ada_w"][0, 0, 0]
    for tag, items, (ssem, rsem, bufs, _) in rs_pending:
        bufs = split_wait(ssem, rsem, bufs, after, chip_exchange_copies, f"rs_chip_wait_{tag}")
        n = len(items)
        for i, (it, p, r) in enumerate(zip(items, bufs[:n], bufs[n:])):
            fin[it[1]] = chip_sum(p, r, chip_c, f"rs_chip_sum_{tag}_{i}", dest=fin.get(it[1]), slot=it[2],
                                  n_slots=it[3])
    filled = sibling_fill_halves([fin[n] for n in big_names], "rs_fill")
    for n, g in zip(big_names, filled):
        grads[n] = g
        update(n)

    return (loss, grad_x, *[grads[n] for n in names], *[deltas[n] for n in names],
            *[new_m[n] for n in names], *[new_v[n] for n in names])
```

```python
import math

import jax
import jax.numpy as jnp
from jax import lax
from jax.experimental import pallas as pl
from jax.experimental.pallas import tpu as pltpu

F32 = jnp.float32
BF16 = jnp.bfloat16
EPS = 1e-6
MESH_T = pl.DeviceIdType.MESH
VMEM_LIMIT_BYTES = 56 * 1024 * 1024
N_CHIPS = 4
N_DEV = 8
POOL_WINDOWS = (2, 4, 8, 16)
CHUNK = 128
SCAN_LANES = 8
SCAN_UNROLL = 4
WG_AHEAD = 2
FFN_BWD_CHUNK = 2048
ADAM_LR = 0.001
ADAM_B1 = 0.9
ADAM_B2 = 0.999
ADAM_EPS = 1e-08
ADAM_WD = 0.01
ADAM_STEP = 10
GELU_C = math.sqrt(2.0 / math.pi)
GELU_K = 0.044715

V_SHIFT, V_SCALE, V_GATE, V_GPRE, V_GPOST = 0, 1, 2, 3, 4
A_SHIFT, A_SCALE, A_GATE, A_GPRE, A_GPOST = 0, 1, 2, 3, 4

ANY = pl.BlockSpec(memory_space=pl.ANY)


def _cp(sem=None):
    if sem is None:
        return pltpu.CompilerParams(vmem_limit_bytes=VMEM_LIMIT_BYTES)
    return pltpu.CompilerParams(vmem_limit_bytes=VMEM_LIMIT_BYTES, dimension_semantics=sem)


def _dot(a, b):
    return jnp.dot(a, b, preferred_element_type=F32)


def _dot_nt(a, b):
    return lax.dot_general(a, b, (((1,), (1,)), ((), ())), preferred_element_type=F32)


def _dot_tn(a, b):
    return lax.dot_general(a, b, (((0,), (0,)), ((), ())), preferred_element_type=F32)


def _sigmoid(x):
    return 1.0 / (1.0 + jnp.exp(-x))


def _gelu(x):
    return 0.5 * x * (1.0 + jnp.tanh(GELU_C * (x + GELU_K * x * x * x)))


def _gelu_grad(x):
    t = jnp.tanh(GELU_C * (x + GELU_K * x * x * x))
    return 0.5 * (1.0 + t) + 0.5 * x * (1.0 - t * t) * GELU_C * (1.0 + 3.0 * GELU_K * x * x)


def _rowsum(v):
    return jnp.sum(v, axis=0, keepdims=True)


def _lanemean(v):
    return jnp.mean(v, axis=-1, keepdims=True)


def _row(ref, i):
    return ref[pl.ds(i, 1), :]


def _pre_fwd(x, vec_ref):
    r = lax.rsqrt(_lanemean(x * x) + EPS)
    return (x * r) * _row(vec_ref, V_GPRE) * (1.0 + _row(vec_ref, V_SCALE)) + _row(vec_ref, V_SHIFT)


def _pre_bwd(x, dh, vec_ref):
    g = _row(vec_ref, V_GPRE)
    sc = 1.0 + _row(vec_ref, V_SCALE)
    r = lax.rsqrt(_lanemean(x * x) + EPS)
    xn = x * r
    dhx = dh * xn
    t2 = dh * (g * sc)
    dx = r * (t2 - xn * _lanemean(t2 * xn))
    return dx, _rowsum(dh), _rowsum(dhx * g), _rowsum(dhx * sc)


def _post_fwd(x, f, vec_ref, rw):
    q = lax.rsqrt(_lanemean(f * f) + EPS)
    return x + (rw * _row(vec_ref, V_GATE)) * (f * q * _row(vec_ref, V_GPOST))


def _post_bwd(dout, f, vec_ref, rw):
    gp = _row(vec_ref, V_GPOST)
    gate = _row(vec_ref, V_GATE)
    q = lax.rsqrt(_lanemean(f * f) + EPS)
    fhat = f * q
    dgate = _rowsum(dout * (rw * fhat * gp))
    dy = dout * (rw * gate)
    dgpost = _rowsum(dy * fhat)
    t = dy * gp
    df = q * (t - fhat * _lanemean(t * fhat))
    return df, dgate, dgpost


def _acc_add(acc_ref, row, v):
    acc_ref[pl.ds(row, 1), :] += v


def ffn_fwd(x, vec, win_g, wout_g, name, tm=512):
    S, D = x.shape
    FS = win_g.shape[-1]
    FR = wout_g.shape[-2]

    def body(x_ref, vec_ref, win_hbm, wout_hbm, out_ref, h_ref, u_ref, f_ref, win_s, wout_s, sem):
        @pl.when(pl.program_id(0) == 0)
        def _():
            cps = [pltpu.make_async_copy(win_hbm.at[j], win_s.at[j], sem.at[j]) for j in range(N_CHIPS)]
            cps += [pltpu.make_async_copy(wout_hbm.at[j], wout_s.at[pl.ds(j * FR, FR), :], sem.at[N_CHIPS + j])
                    for j in range(N_CHIPS)]
            for cp in cps:
                cp.start()
            for cp in cps:
                cp.wait()

        xv = x_ref[...]
        h = _pre_fwd(xv, vec_ref).astype(BF16)
        h_ref[...] = h
        f = jnp.zeros((tm, D), F32)
        for j in range(2):
            a = _dot(h, win_s[j])
            b = _dot(h, win_s[2 + j])
            u_ref[j] = a.astype(BF16)
            u_ref[2 + j] = b.astype(BF16)
            act = (a * _sigmoid(a) * b).astype(BF16)
            f = f + _dot(act, wout_s[pl.ds(j * FS, FS), :])
        f_ref[...] = f
        out_ref[...] = _post_fwd(xv, f, vec_ref, 0.5)

    row = pl.BlockSpec((tm, D), lambda i: (i, 0))
    return pl.pallas_call(
        body, name=name, grid=(S // tm,),
        in_specs=[row, pl.BlockSpec((8, D), lambda i: (0, 0)), ANY, ANY],
        out_specs=[row, row, pl.BlockSpec((N_CHIPS, tm, FS), lambda i: (0, i, 0)), row],
        out_shape=[jax.ShapeDtypeStruct((S, D), F32), jax.ShapeDtypeStruct((S, D), BF16),
                   jax.ShapeDtypeStruct((N_CHIPS, S, FS), BF16), jax.ShapeDtypeStruct((S, D), F32)],
        scratch_shapes=[pltpu.VMEM((N_CHIPS, D, FS), BF16), pltpu.VMEM((N_CHIPS * FR, D), BF16),
                        pltpu.SemaphoreType.DMA((2 * N_CHIPS,))],
        compiler_params=_cp(("arbitrary",)),
    )(x, vec, win_g, wout_g)


def ffn_bwd(dout, x, f, u, vec, win_g, wout_g, name, tm=256):
    S, D = x.shape
    FS = win_g.shape[-1]
    FR = wout_g.shape[-2]

    def body(dout_ref, x_ref, f_ref, u_ref, vec_ref, win_hbm, wout_hbm,
             dx_ref, df_ref, du_ref, act_ref, acc_ref, win_s, wout_s, sem):
        @pl.when(pl.program_id(0) == 0)
        def _():
            cps = [pltpu.make_async_copy(win_hbm.at[j], win_s.at[j], sem.at[j]) for j in range(N_CHIPS)]
            cps += [pltpu.make_async_copy(wout_hbm.at[j], wout_s.at[pl.ds(j * FR, FR), :], sem.at[N_CHIPS + j])
                    for j in range(N_CHIPS)]
            for cp in cps:
                cp.start()
            acc_ref[...] = jnp.zeros_like(acc_ref)
            for cp in cps:
                cp.wait()

        dout_v = dout_ref[...]
        df, dgate, dgpost = _post_bwd(dout_v, f_ref[...], vec_ref, 0.5)
        dfb = df.astype(BF16)
        df_ref[...] = dfb
        dh = jnp.zeros((tm, D), F32)
        chunks = [(c0, min(FFN_BWD_CHUNK, FS - c0)) for c0 in range(0, FS, FFN_BWD_CHUNK)]
        for j in range(2):
            for c0, cw in chunks:
                a = u_ref[j, :, c0:c0 + cw].astype(F32)
                b = u_ref[2 + j, :, c0:c0 + cw].astype(F32)
                sig = _sigmoid(a)
                sl = a * sig
                dact = _dot_nt(dfb, wout_s[pl.ds(j * FS + c0, cw), :])
                da = (dact * b * (sig * (1.0 + a * (1.0 - sig)))).astype(BF16)
                db = (dact * sl).astype(BF16)
                du_ref[j, :, c0:c0 + cw] = da
                du_ref[2 + j, :, c0:c0 + cw] = db
                act_ref[:, j * FS + c0:j * FS + c0 + cw] = (sl * b).astype(BF16)
                dh = dh + _dot_nt(da, win_s[j, :, c0:c0 + cw]) + _dot_nt(db, win_s[2 + j, :, c0:c0 + cw])
        dx, dshift, dscale, dgpre = _pre_bwd(x_ref[...], dh, vec_ref)
        dx_ref[...] = dout_v + dx
        _acc_add(acc_ref, A_SHIFT, dshift)
        _acc_add(acc_ref, A_SCALE, dscale)
        _acc_add(acc_ref, A_GATE, dgate)
        _acc_add(acc_ref, A_GPRE, dgpre)
        _acc_add(acc_ref, A_GPOST, dgpost)

    row = pl.BlockSpec((tm, D), lambda i: (i, 0))
    ublk = pl.BlockSpec((N_CHIPS, tm, FS), lambda i: (0, i, 0))
    const = pl.BlockSpec((8, D), lambda i: (0, 0))
    return pl.pallas_call(
        body, name=name, grid=(S // tm,),
        in_specs=[row, row, row, ublk, const, ANY, ANY],
        out_specs=[row, row, ublk, pl.BlockSpec((tm, 2 * FS), lambda i: (i, 0)), const],
        out_shape=[jax.ShapeDtypeStruct((S, D), F32), jax.ShapeDtypeStruct((S, D), BF16),
                   jax.ShapeDtypeStruct((N_CHIPS, S, FS), BF16), jax.ShapeDtypeStruct((S, 2 * FS), BF16),
                   jax.ShapeDtypeStruct((8, D), F32)],
        scratch_shapes=[pltpu.VMEM((N_CHIPS, D, FS), BF16), pltpu.VMEM((N_CHIPS * FR, D), BF16),
                        pltpu.SemaphoreType.DMA((2 * N_CHIPS,))],
        compiler_params=_cp(("arbitrary",)),
    )(dout, x, f, u, vec, win_g, wout_g)


def tn_matmul(a, b, tk, name, dep=None):
    S, K = a.shape
    nb, _, tn = b.shape
    extra = [] if dep is None else [dep]

    def body(a_ref, b_ref, *rest):
        rest[-1][...] = _dot_tn(a_ref[...], b_ref[...]).astype(BF16)

    return pl.pallas_call(
        body, name=name, grid=(K // tk, nb),
        in_specs=[pl.BlockSpec((S, tk), lambda i, j: (0, i)), pl.BlockSpec((None, S, tn), lambda i, j: (j, 0, 0))]
        + [ANY] * len(extra),
        out_specs=pl.BlockSpec((None, tk, tn), lambda i, j: (j, i, 0)),
        out_shape=jax.ShapeDtypeStruct((nb, K, tn), BF16),
        compiler_params=_cp(("arbitrary", "arbitrary")),
    )(a, b, *extra)


def tn_matmul_cols(a, b, tk, tn, name):
    S, K = a.shape
    N = b.shape[1]

    def body(a_ref, b_ref, o_ref):
        o_ref[...] = _dot_tn(a_ref[...], b_ref[...]).astype(BF16)

    return pl.pallas_call(
        body, name=name, grid=(K // tk, N // tn),
        in_specs=[pl.BlockSpec((S, tk), lambda i, j: (0, i)), pl.BlockSpec((S, tn), lambda i, j: (0, j))],
        out_specs=pl.BlockSpec((None, tk, tn), lambda i, j: (j, i, 0)),
        out_shape=jax.ShapeDtypeStruct((N // tn, K, tn), BF16),
        compiler_params=_cp(("arbitrary", "arbitrary")),
    )(a, b)


def pre_matmul(x, vec, w3, name, tm=256):
    S, D = x.shape
    nj, _, Nj = w3.shape

    def body(x_ref, vec_ref, w_ref, h_ref, z_ref):
        h = _pre_fwd(x_ref[...], vec_ref).astype(BF16)
        h_ref[...] = h
        for j in range(nj):
            z_ref[:, j * Nj:(j + 1) * Nj] = _dot(h, w_ref[j])

    row = pl.BlockSpec((tm, D), lambda i: (i, 0))
    return pl.pallas_call(
        body, name=name, grid=(S // tm,),
        in_specs=[row, pl.BlockSpec((8, D), lambda i: (0, 0)), pl.BlockSpec((nj, D, Nj), lambda i: (0, 0, 0))],
        out_specs=[row, pl.BlockSpec((tm, nj * Nj), lambda i: (i, 0))],
        out_shape=[jax.ShapeDtypeStruct((S, D), BF16), jax.ShapeDtypeStruct((S, nj * Nj), F32)],
        compiler_params=_cp(("arbitrary",)),
    )(x, vec, w3)


def matmul_pre_bwd(dz, w3, x, dres, vec, name, tm=256):
    S, D = x.shape
    nj, _, Nj = w3.shape

    def body(dz_ref, w_ref, x_ref, dres_ref, vec_ref, dx_ref, acc_ref):
        @pl.when(pl.program_id(0) == 0)
        def _():
            acc_ref[...] = jnp.zeros_like(acc_ref)

        dh = jnp.zeros((tm, D), F32)
        for j in range(nj):
            dh = dh + _dot_nt(dz_ref[:, j * Nj:(j + 1) * Nj], w_ref[j])
        dx, dshift, dscale, dgpre = _pre_bwd(x_ref[...], dh, vec_ref)
        dx_ref[...] = dres_ref[...] + dx
        _acc_add(acc_ref, A_SHIFT, dshift)
        _acc_add(acc_ref, A_SCALE, dscale)
        _acc_add(acc_ref, A_GPRE, dgpre)

    row = pl.BlockSpec((tm, D), lambda i: (i, 0))
    const = pl.BlockSpec((8, D), lambda i: (0, 0))
    return pl.pallas_call(
        body, name=name, grid=(S // tm,),
        in_specs=[pl.BlockSpec((tm, nj * Nj), lambda i: (i, 0)), pl.BlockSpec((nj, D, Nj), lambda i: (0, 0, 0)),
                  row, row, const],
        out_specs=[row, const],
        out_shape=[jax.ShapeDtypeStruct((S, D), F32), jax.ShapeDtypeStruct((8, D), F32)],
        compiler_params=_cp(("arbitrary",)),
    )(dz, w3, x, dres, vec)


def matmul_post(act, w, x, vec, name, tm=256):
    S, D = x.shape
    K = act.shape[1]

    def body(act_ref, w_ref, x_ref, vec_ref, out_ref, f_ref):
        f = _dot(act_ref[...], w_ref[...])
        f_ref[...] = f
        out_ref[...] = _post_fwd(x_ref[...], f, vec_ref, 1.0)

    row = pl.BlockSpec((tm, D), lambda i: (i, 0))
    return pl.pallas_call(
        body, name=name, grid=(S // tm,),
        in_specs=[pl.BlockSpec((tm, K), lambda i: (i, 0)), pl.BlockSpec((K, D), lambda i: (0, 0)), row,
                  pl.BlockSpec((8, D), lambda i: (0, 0))],
        out_specs=[row, row],
        out_shape=[jax.ShapeDtypeStruct((S, D), F32), jax.ShapeDtypeStruct((S, D), F32)],
        compiler_params=_cp(("arbitrary",)),
    )(act, w, x, vec)


def post_bwd_matmul(dout, f, vec, w, name, tm=256):
    S, D = dout.shape
    K = w.shape[0]

    def body(dout_ref, f_ref, vec_ref, w_ref, df_ref, dact_ref, acc_ref):
        @pl.when(pl.program_id(0) == 0)
        def _():
            acc_ref[...] = jnp.zeros_like(acc_ref)

        df, dgate, dgpost = _post_bwd(dout_ref[...], f_ref[...], vec_ref, 1.0)
        dfb = df.astype(BF16)
        df_ref[...] = dfb
        dact_ref[...] = _dot_nt(dfb, w_ref[...])
        _acc_add(acc_ref, A_GATE, dgate)
        _acc_add(acc_ref, A_GPOST, dgpost)

    row = pl.BlockSpec((tm, D), lambda i: (i, 0))
    const = pl.BlockSpec((8, D), lambda i: (0, 0))
    return pl.pallas_call(
        body, name=name, grid=(S // tm,),
        in_specs=[row, row, const, pl.BlockSpec((K, D), lambda i: (0, 0))],
        out_specs=[row, pl.BlockSpec((tm, K), lambda i: (i, 0)), const],
        out_shape=[jax.ShapeDtypeStruct((S, D), BF16), jax.ShapeDtypeStruct((S, K), F32),
                   jax.ShapeDtypeStruct((8, D), F32)],
        compiler_params=_cp(("arbitrary",)),
    )(dout, f, vec, w)


def _band(w, transposed, prev):
    r = lax.broadcasted_iota(jnp.int32, (CHUNK, CHUNK), 1 if transposed else 0)
    c = lax.broadcasted_iota(jnp.int32, (CHUNK, CHUNK), 0 if transposed else 1)
    d = r - c
    m = (d + CHUNK < w) if prev else ((d >= 0) & (d < w))
    return jnp.where(m, 1.0, 0.0).astype(BF16)


def _split_hi_lo(a):
    hi = a.astype(BF16)
    lo = (a - hi.astype(F32)).astype(BF16)
    return hi, lo


def _pool_diff(a, ap, g, denom):
    w = POOL_WINDOWS[g]
    a_hi, a_lo = _split_hi_lo(a)
    p_hi, p_lo = _split_hi_lo(ap)
    mc = _band(w, False, False)
    mp = _band(w, False, True)
    win = _dot(mc, a_hi) + _dot(mc, a_lo) + _dot(mp, p_hi) + _dot(mp, p_lo)
    return win / denom - a


def _sgu_norm(v, lg, lb):
    mu = _lanemean(v)
    xc = v - mu
    rstd = lax.rsqrt(_lanemean(xc * xc) + EPS)
    xhat = xc * rstd
    return xhat, rstd, xhat * lg + lb


def _tril_mask():
    r = lax.broadcasted_iota(jnp.int32, (CHUNK, CHUNK), 0)
    c = lax.broadcasted_iota(jnp.int32, (CHUNK, CHUNK), 1)
    return r >= c


def _positions(i, w):
    r = lax.broadcasted_iota(jnp.int32, (CHUNK, 128), 0)
    pos = (i * CHUNK + r + 1).astype(F32)
    return jnp.minimum(pos, float(w))


def mixa_core_fwd(z, pool_w, pool_scale, ln_g, ln_b, sgu_w, sgu_bexp, name):
    S = z.shape[0]
    W = z.shape[1] // 3
    G = len(POOL_WINDOWS)
    GD = W // G

    def body(zc_ref, zp_ref, pw_ref, ps_ref, lg_ref, lb_ref, sw_ref, sb_ref, y_ref):
        i = pl.program_id(0)
        has_prev = jnp.where(i > 0, 1.0, 0.0)
        for g in range(G):
            sl = slice(g * GD, (g + 1) * GD)
            a = zc_ref[:, sl]
            ap = zp_ref[:, sl] * has_prev
            d = _pool_diff(a, ap, g, _positions(i, POOL_WINDOWS[g])).astype(BF16)
            y_ref[:, sl] = (_dot(d, pw_ref[g]) * ps_ref[:, sl]).astype(BF16)
        tril = _tril_mask()
        for hh in range(G):
            u = _gelu(zc_ref[:, W + hh * GD:W + (hh + 1) * GD])
            v = _gelu(zc_ref[:, 2 * W + hh * GD:2 * W + (hh + 1) * GD])
            sl = slice(hh * GD, (hh + 1) * GD)
            _, _, vn = _sgu_norm(v, lg_ref[:, sl], lb_ref[:, sl])
            wm = jnp.where(tril, sw_ref[hh], 0.0).astype(BF16)
            s = _dot(wm, vn.astype(BF16)) + sb_ref[hh]
            y_ref[:, W + hh * GD:W + (hh + 1) * GD] = (u * s).astype(BF16)

    vecw = pl.BlockSpec((1, W), lambda i: (0, 0))
    mats = pl.BlockSpec((G, GD, GD), lambda i: (0, 0, 0))
    return pl.pallas_call(
        body, name=name, grid=(S // CHUNK,),
        in_specs=[pl.BlockSpec((CHUNK, 3 * W), lambda i: (i, 0)),
                  pl.BlockSpec((CHUNK, W), lambda i: (jnp.maximum(i - 1, 0), 0)),
                  mats, vecw, vecw, vecw, mats, mats],
        out_specs=pl.BlockSpec((CHUNK, 2 * W), lambda i: (i, 0)),
        out_shape=jax.ShapeDtypeStruct((S, 2 * W), BF16),
        compiler_params=_cp(("arbitrary",)),
    )(z, z, pool_w, pool_scale, ln_g, ln_b, sgu_w, sgu_bexp)


def mixa_core_bwd(z, dy, pool_w, pool_scale, ln_g, ln_b, sgu_w, sgu_bexp, name):
    S = z.shape[0]
    W = z.shape[1] // 3
    G = len(POOL_WINDOWS)
    GD = W // G
    n_tiles = S // CHUNK

    def body(zc_ref, zp_ref, dyc_ref, dyn_ref, pw_ref, ps_ref, lg_ref, lb_ref, sw_ref, sb_ref,
             dz_ref, dpw_ref, dvec_ref, dsw_ref, dsb_ref):
        i = pl.program_id(0)

        @pl.when(i == 0)
        def _():
            dpw_ref[...] = jnp.zeros_like(dpw_ref)
            dvec_ref[...] = jnp.zeros_like(dvec_ref)
            dsw_ref[...] = jnp.zeros_like(dsw_ref)
            dsb_ref[...] = jnp.zeros_like(dsb_ref)

        has_prev = jnp.where(i > 0, 1.0, 0.0)
        has_next = jnp.where(i < n_tiles - 1, 1.0, 0.0)
        for g in range(G):
            w = POOL_WINDOWS[g]
            sl = slice(g * GD, (g + 1) * GD)
            a = zc_ref[:, sl]
            ap = zp_ref[:, sl] * has_prev
            den_c = _positions(i, w)
            den_n = _positions(i + 1, w)
            d = _pool_diff(a, ap, g, den_c).astype(BF16)
            ps = ps_ref[:, sl]
            pw = pw_ref[g]
            dyc = dyc_ref[:, sl]
            dvec_ref[pl.ds(0, 1), sl] += _rowsum(dyc * _dot(d, pw))
            dyp_c = (dyc * ps).astype(BF16)
            dyp_n = (dyn_ref[:, sl] * (ps * has_next)).astype(BF16)
            dpw_ref[g] += _dot_tn(d, dyp_c)
            dd_c = _dot_nt(dyp_c, pw)
            dd_n = _dot_nt(dyp_n, pw)
            da = (_dot(_band(w, True, False), (dd_c / den_c).astype(BF16))
                  + _dot(_band(w, True, True), (dd_n / den_n).astype(BF16)) - dd_c)
            dz_ref[:, sl] = da.astype(BF16)
        tril = _tril_mask()
        for hh in range(G):
            sl = slice(hh * GD, (hh + 1) * GD)
            zu = zc_ref[:, W + hh * GD:W + (hh + 1) * GD]
            zv = zc_ref[:, 2 * W + hh * GD:2 * W + (hh + 1) * GD]
            u = _gelu(zu)
            v = _gelu(zv)
            lg = lg_ref[:, sl]
            xhat, rstd, vn = _sgu_norm(v, lg, lb_ref[:, sl])
            vnb = vn.astype(BF16)
            wm = jnp.where(tril, sw_ref[hh], 0.0).astype(BF16)
            s = _dot(wm, vnb) + sb_ref[hh]
            dyb = dyc_ref[:, W + hh * GD:W + (hh + 1) * GD]
            du = dyb * s
            ds = dyb * u
            dsb_ref[:, hh:hh + 1] += jnp.sum(ds, axis=1, keepdims=True)
            dsb16 = ds.astype(BF16)
            dsw_ref[hh] += jnp.where(tril, _dot_nt(dsb16, vnb), 0.0)
            dvn = _dot_tn(wm, dsb16)
            dvec_ref[pl.ds(1, 1), sl] += _rowsum(dvn * xhat)
            dvec_ref[pl.ds(2, 1), sl] += _rowsum(dvn)
            dxh = dvn * lg
            dv = rstd * (dxh - _lanemean(dxh) - xhat * _lanemean(dxh * xhat))
            dz_ref[:, W + hh * GD:W + (hh + 1) * GD] = (du * _gelu_grad(zu)).astype(BF16)
            dz_ref[:, 2 * W + hh * GD:2 * W + (hh + 1) * GD] = (dv * _gelu_grad(zv)).astype(BF16)

    vecw = pl.BlockSpec((1, W), lambda i: (0, 0))
    mats = pl.BlockSpec((G, GD, GD), lambda i: (0, 0, 0))
    return pl.pallas_call(
        body, name=name, grid=(n_tiles,),
        in_specs=[pl.BlockSpec((CHUNK, 3 * W), lambda i: (i, 0)),
                  pl.BlockSpec((CHUNK, W), lambda i: (jnp.maximum(i - 1, 0), 0)),
                  pl.BlockSpec((CHUNK, 2 * W), lambda i: (i, 0)),
                  pl.BlockSpec((CHUNK, W), lambda i: (jnp.minimum(i + 1, n_tiles - 1), 0)),
                  mats, vecw, vecw, vecw, mats, mats],
        out_specs=[pl.BlockSpec((CHUNK, 3 * W), lambda i: (i, 0)), mats,
                   pl.BlockSpec((8, W), lambda i: (0, 0)), mats, pl.BlockSpec((CHUNK, G), lambda i: (0, 0))],
        out_shape=[jax.ShapeDtypeStruct((S, 3 * W), BF16), jax.ShapeDtypeStruct((G, GD, GD), F32),
                   jax.ShapeDtypeStruct((8, W), F32), jax.ShapeDtypeStruct((G, CHUNK, CHUNK), F32),
                   jax.ShapeDtypeStruct((CHUNK, G), F32)],
        compiler_params=_cp(("arbitrary",)),
    )(z, z, dy, dy, pool_w, pool_scale, ln_g, ln_b, sgu_w, sgu_bexp)


def _cmul(ar, ai, br, bi):
    return ar * br - ai * bi, ar * bi + ai * br


def _cpow(ar, ai, n):
    rr, ri = None, None
    br, bi = ar, ai
    while n:
        if n & 1:
            rr, ri = (br, bi) if rr is None else _cmul(rr, ri, br, bi)
        n >>= 1
        if n:
            br, bi = _cmul(br, bi, br, bi)
    return rr, ri


def _seg_rows(k):
    return pl.ds(pl.multiple_of(k * SCAN_LANES, SCAN_LANES), SCAN_LANES)


def _scan_fwd(xre, xim, carry, ar, ai, K):
    P = xre.shape[1]
    a8r = jnp.broadcast_to(ar, (SCAN_LANES, P))
    a8i = jnp.broadcast_to(ai, (SCAN_LANES, P))

    def local(k, c):
        pr, pi = c
        rows = _seg_rows(k)
        nr = a8r * pr - a8i * pi + xre[rows, :]
        ni = a8r * pi + a8i * pr + xim[rows, :]
        xre[rows, :] = nr
        xim[rows, :] = ni
        return nr, ni

    er, ei = lax.fori_loop(1, K, local, (xre[pl.ds(0, SCAN_LANES), :], xim[pl.ds(0, SCAN_LANES), :]),
                           unroll=SCAN_UNROLL)
    akr, aki = _cpow(ar, ai, K)
    cr = jnp.zeros((1, P), F32)
    ci = jnp.zeros((1, P), F32)
    carry[pl.ds(0, 1), :] = cr
    carry[pl.ds(SCAN_LANES, 1), :] = ci
    for j in range(1, SCAN_LANES):
        tr, ti = _cmul(akr, aki, cr, ci)
        cr = er[j - 1:j, :] + tr
        ci = ei[j - 1:j, :] + ti
        carry[pl.ds(j, 1), :] = cr
        carry[pl.ds(SCAN_LANES + j, 1), :] = ci
    cmr = carry[pl.ds(0, SCAN_LANES), :]
    cmi = carry[pl.ds(SCAN_LANES, SCAN_LANES), :]

    def fix(k, c):
        pr, pi = c
        rows = _seg_rows(k)
        tr, ti = _cmul(pr, pi, cmr, cmi)
        xre[rows, :] += tr
        xim[rows, :] += ti
        return _cmul(pr, pi, a8r, a8i)

    lax.fori_loop(0, K, fix, (a8r, a8i), unroll=SCAN_UNROLL)


def s5_core_fwd(u, ar, ai, bre, bim, cre_t, cim_t, dskip, name):
    S, D = u.shape
    nblk, UB, PB = bre.shape
    K = S // SCAN_LANES

    def body(u_ref, ar_ref, ai_ref, bre_ref, bim_ref, cre_ref, cim_ref, d_ref, y_ref, xre, xim, carry):
        uv = u_ref[...]
        ub = uv.astype(BF16)
        xre[...] = _dot(ub, bre_ref[...])
        xim[...] = _dot(ub, bim_ref[...])
        _scan_fwd(xre, xim, carry, ar_ref[...], ai_ref[...], K)
        y_ref[...] = (_dot(xre[...].astype(BF16), cre_ref[...]) - _dot(xim[...].astype(BF16), cim_ref[...])
                      + d_ref[...] * uv)

    ucol = pl.BlockSpec((S, UB), lambda i: (0, i))
    pvec = pl.BlockSpec((None, 1, PB), lambda i: (i, 0, 0))
    bmat = pl.BlockSpec((None, UB, PB), lambda i: (i, 0, 0))
    cmat = pl.BlockSpec((None, PB, UB), lambda i: (i, 0, 0))
    return pl.pallas_call(
        body, name=name, grid=(nblk,),
        in_specs=[ucol, pvec, pvec, bmat, bmat, cmat, cmat, pl.BlockSpec((1, UB), lambda i: (0, i))],
        out_specs=ucol,
        out_shape=jax.ShapeDtypeStruct((S, D), F32),
        scratch_shapes=[pltpu.VMEM((S, PB), F32), pltpu.VMEM((S, PB), F32), pltpu.VMEM((2 * SCAN_LANES, PB), F32)],
        compiler_params=_cp(("arbitrary",)),
    )(u, ar, ai, bre, bim, cre_t, cim_t, dskip)


def s5_core_bwd(u, dy, ar, ai, bre, bim, cre, cim, dskip, name):
    S, D = u.shape
    nblk, UB, PB = bre.shape
    K = S // SCAN_LANES

    def body(u_ref, dy_ref, ar_ref, ai_ref, bre_ref, bim_ref, cre_ref, cim_ref, d_ref,
             du_ref, dbre_ref, dbim_ref, dcre_ref, dcim_ref, dar_ref, dai_ref, dd_ref,
             xre, xim, gre, gim, carry, carry_b):
        ar = ar_ref[...]
        ai = ai_ref[...]
        uv = u_ref[...]
        ub = uv.astype(BF16)
        dyv = dy_ref[...]
        dyb = dyv.astype(BF16)
        xre[...] = _dot(ub, bre_ref[...])
        xim[...] = _dot(ub, bim_ref[...])
        _scan_fwd(xre, xim, carry, ar, ai, K)
        dcre_ref[...] = _dot_tn(xre[...].astype(BF16), dyb)
        dcim_ref[...] = -_dot_tn(xim[...].astype(BF16), dyb)
        gre[...] = _dot(dyb, cre_ref[...])
        gim[...] = -_dot(dyb, cim_ref[...])

        a8r = jnp.broadcast_to(ar, (SCAN_LANES, PB))
        a8i = jnp.broadcast_to(ai, (SCAN_LANES, PB))
        na8i = -a8i

        def local(s, c):
            k = K - 2 - s
            nr, ni = c
            rows = _seg_rows(k)
            tr = gre[rows, :] + a8r * nr + a8i * ni
            ti = gim[rows, :] + a8r * ni - a8i * nr
            gre[rows, :] = tr
            gim[rows, :] = ti
            return tr, ti

        last = _seg_rows(K - 1)
        fr, fi = lax.fori_loop(0, K - 1, local, (gre[last, :], gim[last, :]), unroll=SCAN_UNROLL)
        akr, aki = _cpow(ar, -ai, K)
        cr = jnp.zeros((1, PB), F32)
        ci = jnp.zeros((1, PB), F32)
        carry_b[pl.ds(SCAN_LANES - 1, 1), :] = cr
        carry_b[pl.ds(2 * SCAN_LANES - 1, 1), :] = ci
        for j in range(SCAN_LANES - 2, -1, -1):
            tr, ti = _cmul(akr, aki, cr, ci)
            cr = fr[j + 1:j + 2, :] + tr
            ci = fi[j + 1:j + 2, :] + ti
            carry_b[pl.ds(j, 1), :] = cr
            carry_b[pl.ds(SCAN_LANES + j, 1), :] = ci
        cbr = carry_b[pl.ds(0, SCAN_LANES), :]
        cbi = carry_b[pl.ds(SCAN_LANES, SCAN_LANES), :]

        def fix_rows(rows, c, xr, xi):
            pr, pi, dar, dai = c
            tr, ti = _cmul(pr, pi, cbr, cbi)
            g_r = gre[rows, :] + tr
            g_i = gim[rows, :] + ti
            gre[rows, :] = g_r
            gim[rows, :] = g_i
            dar = dar + g_r * xr + g_i * xi
            dai = dai + g_i * xr - g_r * xi
            nr, ni = _cmul(pr, pi, a8r, na8i)
            return nr, ni, dar, dai

        def fix(s, c):
            k = K - 1 - s
            prev = _seg_rows(k - 1)
            return fix_rows(_seg_rows(k), c, xre[prev, :], xim[prev, :])

        z8 = jnp.zeros((SCAN_LANES, PB), F32)
        c = lax.fori_loop(0, K - 1, fix, (a8r, na8i, z8, z8), unroll=SCAN_UNROLL)
        _, _, dar, dai = fix_rows(_seg_rows(0), c, carry[pl.ds(0, SCAN_LANES), :], carry[pl.ds(SCAN_LANES, SCAN_LANES), :])
        dar_ref[...] = _rowsum(dar)
        dai_ref[...] = _rowsum(dai)
        grb = gre[...].astype(BF16)
        gib = gim[...].astype(BF16)
        dbre_ref[...] = _dot_tn(ub, grb)
        dbim_ref[...] = _dot_tn(ub, gib)
        du_ref[...] = _dot_nt(grb, bre_ref[...]) + _dot_nt(gib, bim_ref[...]) + d_ref[...] * dyv
        dd_ref[...] = _rowsum(dyv * uv)

    ucol = pl.BlockSpec((S, UB), lambda i: (0, i))
    pvec = pl.BlockSpec((None, 1, PB), lambda i: (i, 0, 0))
    bmat = pl.BlockSpec((None, UB, PB), lambda i: (i, 0, 0))
    cmat = pl.BlockSpec((None, PB, UB), lambda i: (i, 0, 0))
    dvec = pl.BlockSpec((1, UB), lambda i: (0, i))
    return pl.pallas_call(
        body, name=name, grid=(nblk,),
        in_specs=[ucol, ucol, pvec, pvec, bmat, bmat, bmat, bmat, dvec],
        out_specs=[ucol, bmat, bmat, cmat, cmat, pvec, pvec, dvec],
        out_shape=[jax.ShapeDtypeStruct((S, D), F32),
                   jax.ShapeDtypeStruct((nblk, UB, PB), F32), jax.ShapeDtypeStruct((nblk, UB, PB), F32),
                   jax.ShapeDtypeStruct((nblk, PB, UB), F32), jax.ShapeDtypeStruct((nblk, PB, UB), F32),
                   jax.ShapeDtypeStruct((nblk, 1, PB), F32), jax.ShapeDtypeStruct((nblk, 1, PB), F32),
                   jax.ShapeDtypeStruct((1, D), F32)],
        scratch_shapes=[pltpu.VMEM((S, PB), F32), pltpu.VMEM((S, PB), F32), pltpu.VMEM((S, PB), F32),
                        pltpu.VMEM((S, PB), F32), pltpu.VMEM((2 * SCAN_LANES, PB), F32),
                        pltpu.VMEM((2 * SCAN_LANES, PB), F32)],
        compiler_params=_cp(("arbitrary",)),
    )(u, dy, ar, ai, bre, bim, cre, cim, dskip)


def s5_glu_fwd(y, wglu, x, vec, name, tm=256):
    S, D = x.shape
    NJ = wglu.shape[-1]

    def body(y_ref, w_ref, x_ref, vec_ref, out_ref, f_ref):
        g = _gelu(y_ref[...]).astype(BF16)
        f = jnp.concatenate([_dot(g, w_ref[j]) * _sigmoid(_dot(g, w_ref[2 + j])) for j in range(2)], axis=1)
        f_ref[...] = f
        out_ref[...] = _post_fwd(x_ref[...], f, vec_ref, 1.0)

    row = pl.BlockSpec((tm, D), lambda i: (i, 0))
    return pl.pallas_call(
        body, name=name, grid=(S // tm,),
        in_specs=[row, pl.BlockSpec((N_CHIPS, D, NJ), lambda i: (0, 0, 0)), row, pl.BlockSpec((8, D), lambda i: (0, 0))],
        out_specs=[row, row],
        out_shape=[jax.ShapeDtypeStruct((S, D), F32), jax.ShapeDtypeStruct((S, D), F32)],
        compiler_params=_cp(("arbitrary",)),
    )(y, wglu, x, vec)


def s5_glu_bwd(dout, f, y, vec, wglu, name, tm=256):
    S, D = dout.shape
    NJ = wglu.shape[-1]

    def body(dout_ref, f_ref, y_ref, vec_ref, w_ref, dy_ref, dab_ref, g_ref, acc_ref):
        @pl.when(pl.program_id(0) == 0)
        def _():
            acc_ref[...] = jnp.zeros_like(acc_ref)

        df, dgate, dgpost = _post_bwd(dout_ref[...], f_ref[...], vec_ref, 1.0)
        yv = y_ref[...]
        g = _gelu(yv).astype(BF16)
        g_ref[...] = g
        dg = jnp.zeros((tm, D), F32)
        for j in range(2):
            a = _dot(g, w_ref[j])
            sig = _sigmoid(_dot(g, w_ref[2 + j]))
            dfj = df[:, j * NJ:(j + 1) * NJ]
            da = (dfj * sig).astype(BF16)
            db = (dfj * a * sig * (1.0 - sig)).astype(BF16)
            dab_ref[:, j * NJ:(j + 1) * NJ] = da
            dab_ref[:, (2 + j) * NJ:(3 + j) * NJ] = db
            dg = dg + _dot_nt(da, w_ref[j]) + _dot_nt(db, w_ref[2 + j])
        dy_ref[...] = dg * _gelu_grad(yv)
        _acc_add(acc_ref, A_GATE, dgate)
        _acc_add(acc_ref, A_GPOST, dgpost)

    row = pl.BlockSpec((tm, D), lambda i: (i, 0))
    const = pl.BlockSpec((8, D), lambda i: (0, 0))
    return pl.pallas_call(
        body, name=name, grid=(S // tm,),
        in_specs=[row, row, row, const, pl.BlockSpec((N_CHIPS, D, NJ), lambda i: (0, 0, 0))],
        out_specs=[row, pl.BlockSpec((tm, N_CHIPS * NJ), lambda i: (i, 0)), row, const],
        out_shape=[jax.ShapeDtypeStruct((S, D), F32), jax.ShapeDtypeStruct((S, N_CHIPS * NJ), BF16),
                   jax.ShapeDtypeStruct((S, D), BF16), jax.ShapeDtypeStruct((8, D), F32)],
        compiler_params=_cp(("arbitrary",)),
    )(dout, f, y, vec, wglu)


def loss_head(y, target, name, tm=256):
    S, D = y.shape

    def body(y_ref, t_ref, d_ref, l_ref):
        @pl.when(pl.program_id(0) == 0)
        def _():
            l_ref[...] = jnp.zeros_like(l_ref)

        err = y_ref[...] - t_ref[...]
        d_ref[...] = err * (1.0 / D)
        l_ref[...] += jnp.sum(_rowsum(err * err), axis=1, keepdims=True)

    row = pl.BlockSpec((tm, D), lambda i: (i, 0))
    return pl.pallas_call(
        body, name=name, grid=(S // tm,),
        in_specs=[row, row],
        out_specs=[row, pl.BlockSpec((1, 1), lambda i: (0, 0))],
        out_shape=[jax.ShapeDtypeStruct((S, D), F32), jax.ShapeDtypeStruct((1, 1), F32)],
        compiler_params=_cp(("arbitrary",)),
    )(y, target)


def zero_after(dep, name, shape=(8, 128)):
    def body(dep_ref, o_ref):
        o_ref[...] = jnp.zeros_like(o_ref)

    return pl.pallas_call(body, name=name, in_specs=[ANY], out_specs=pl.BlockSpec(memory_space=pltpu.VMEM),
                          out_shape=jax.ShapeDtypeStruct(shape, F32), compiler_params=_cp())(dep)


def ada_mod(c_all, ada_w, ada_b_shard, name, tn=768, dep=None):
    B, D = c_all.shape
    L, _, NS = ada_w.shape
    extra = [] if dep is None else [dep]

    def body(c_ref, w_ref, b_ref, *rest):
        cv = c_ref[...]
        cond = (cv * _sigmoid(cv)).astype(BF16)
        rest[-1][...] = _dot(cond, w_ref[...].astype(BF16)) + b_ref[...]

    return pl.pallas_call(
        body, name=name, grid=(L, NS // tn),
        in_specs=[pl.BlockSpec((B, D), lambda l, j: (0, 0)), pl.BlockSpec((None, D, tn), lambda l, j: (l, 0, j)),
                  pl.BlockSpec((None, 1, tn), lambda l, j: (l, 0, j))] + [ANY] * len(extra),
        out_specs=pl.BlockSpec((None, B, tn), lambda l, j: (l, 0, j)),
        out_shape=jax.ShapeDtypeStruct((L, B, NS), F32),
        compiler_params=_cp(("arbitrary", "arbitrary")),
    )(c_all, ada_w, ada_b_shard, *extra)


def ada_grad(c_all, dmod, name, tn=768):
    B, D = c_all.shape
    L, _, NS = dmod.shape

    def body(c_ref, d_ref, o_ref):
        cv = c_ref[...]
        cond = (cv * _sigmoid(cv)).astype(BF16)
        o_ref[...] = _dot_tn(cond, d_ref[...].astype(BF16))

    return pl.pallas_call(
        body, name=name, grid=(L, NS // tn),
        in_specs=[pl.BlockSpec((B, D), lambda l, j: (0, 0)), pl.BlockSpec((None, B, tn), lambda l, j: (l, 0, j))],
        out_specs=pl.BlockSpec((None, D, tn), lambda l, j: (l, 0, j)),
        out_shape=jax.ShapeDtypeStruct((L, D, NS), F32),
        compiler_params=_cp(("arbitrary", "arbitrary")),
    )(c_all, dmod)


def sum_leading(x, name):
    n, R, C = x.shape

    def body(x_ref, o_ref):
        acc = x_ref[0].astype(F32)
        for i in range(1, n):
            acc = acc + x_ref[i].astype(F32)
        o_ref[...] = acc

    tr = _row_tile(R, 64)
    return pl.pallas_call(
        body, name=name, grid=(R // tr,),
        in_specs=[pl.BlockSpec((n, tr, C), lambda i: (0, i, 0))],
        out_specs=pl.BlockSpec((tr, C), lambda i: (i, 0)),
        out_shape=jax.ShapeDtypeStruct((R, C), F32),
        compiler_params=_cp(("arbitrary",)),
    )(x)


def sum_gathered(land, own, me_arr, name):
    n, R, C = land.shape

    def body(me_ref, land_ref, own_ref, o_ref):
        acc = jnp.zeros(o_ref.shape, F32)
        for i in range(n):
            acc = acc + jnp.where(me_ref[0] == i, own_ref[...], land_ref[i]).astype(F32)
        o_ref[...] = acc

    tr = _row_tile(R, 64)
    grid_spec = pltpu.PrefetchScalarGridSpec(
        num_scalar_prefetch=1, grid=(R // tr,),
        in_specs=[pl.BlockSpec((n, tr, C), lambda i, me: (0, i, 0)), pl.BlockSpec((tr, C), lambda i, me: (i, 0))],
        out_specs=pl.BlockSpec((tr, C), lambda i, me: (i, 0)))
    return pl.pallas_call(
        body, name=name, grid_spec=grid_spec,
        out_shape=jax.ShapeDtypeStruct((R, C), F32),
        compiler_params=_cp(("arbitrary",)),
    )(me_arr, land, own)


def _row_tile(R, cap=512):
    if R <= cap:
        return R
    for cand in (512, 384, 352, 256, 128, 64, 32, 16, 8):
        if cand <= cap and R % cand == 0:
            return cand
    return R


def adamw(w, g, m, v, name, emit_grad=False, dep=None):
    R, C = w.shape
    tr = _row_tile(R, 256 if C > 1024 else 512)
    bc1 = 1.0 - ADAM_B1 ** ADAM_STEP
    bc2 = 1.0 - ADAM_B2 ** ADAM_STEP
    n_out = 4 if emit_grad else 3
    extra = [] if dep is None else [dep]

    def body(w_ref, g_ref, m_ref, v_ref, *outs):
        d_ref, nm_ref, nv_ref = outs[-3:]
        gv = g_ref[...]
        if emit_grad:
            outs[-4][...] = gv
        nm = ADAM_B1 * m_ref[...] + (1.0 - ADAM_B1) * gv
        nv = ADAM_B2 * v_ref[...] + (1.0 - ADAM_B2) * (gv * gv)
        nm_ref[...] = nm
        nv_ref[...] = nv
        d_ref[...] = -ADAM_LR * ((nm / bc1) / (jnp.sqrt(nv / bc2) + ADAM_EPS) + ADAM_WD * w_ref[...])

    blk = pl.BlockSpec((tr, C), lambda i: (i, 0))
    sd = jax.ShapeDtypeStruct((R, C), F32)
    return pl.pallas_call(
        body, name=name, grid=(R // tr,),
        in_specs=[blk, blk, blk, blk] + [ANY] * len(extra), out_specs=[blk] * n_out, out_shape=[sd] * n_out,
        compiler_params=_cp(("arbitrary",)),
    )(w, g, m, v, *extra)


def _as2d(a):
    if a.ndim == 1:
        return a.reshape(1, -1)
    return a.reshape(-1, a.shape[-1])


SMALL_PARAM_ELEMS = 1 << 18


def adamw_nd(w, g, m, v, name, emit_grad=False, dep=None):
    extra = [] if dep is None else [dep]
    if w.size <= SMALL_PARAM_ELEMS and not emit_grad:
        bc1 = 1.0 - ADAM_B1 ** ADAM_STEP
        bc2 = 1.0 - ADAM_B2 ** ADAM_STEP

        def body(w_ref, g_ref, m_ref, v_ref, *rest):
            d_ref, nm_ref, nv_ref = rest[-3:]
            gv = g_ref[...]
            nm = ADAM_B1 * m_ref[...] + (1.0 - ADAM_B1) * gv
            nv = ADAM_B2 * v_ref[...] + (1.0 - ADAM_B2) * (gv * gv)
            nm_ref[...] = nm
            nv_ref[...] = nv
            d_ref[...] = -ADAM_LR * ((nm / bc1) / (jnp.sqrt(nv / bc2) + ADAM_EPS) + ADAM_WD * w_ref[...])

        sd = jax.ShapeDtypeStruct(w.shape, F32)
        whole = pl.BlockSpec(memory_space=pltpu.VMEM)
        return tuple(pl.pallas_call(body, name=name, in_specs=[whole] * 4 + [ANY] * len(extra), out_specs=[whole] * 3,
                                    out_shape=[sd, sd, sd], compiler_params=_cp())(w, g.reshape(w.shape), m, v, *extra))
    outs = adamw(_as2d(w), _as2d(g.reshape(w.shape)), _as2d(m), _as2d(v), name, emit_grad=emit_grad, dep=dep)
    return tuple(o.reshape(w.shape) for o in outs)


def _place():
    x, y, c = lax.axis_index("x"), lax.axis_index("y"), lax.axis_index("c")
    chips = [(1 - x, y), (x, 1 - y), (1 - x, 1 - y)]
    return x, y, c, chips


def allgather_small(xs, name):
    m_per, n = xs.shape

    def body(x_ref, out_ref, send_sems, recv_sems, local_sem):
        x, y, c, chips = _place()
        me, sibling = (x, y, c), (x, y, 1 - c)

        def rows(px, py, pc):
            return out_ref.at[pl.ds((4 * px + 2 * py + pc) * m_per, m_per), :]

        def copy(k, block, to, src=None):
            return pltpu.make_async_remote_copy(
                src_ref=rows(*block) if src is None else src, dst_ref=rows(*block),
                send_sem=send_sems.at[k], recv_sem=recv_sems.at[k], device_id=to, device_id_type=MESH_T)

        mine = pltpu.make_async_copy(x_ref, rows(*me), local_sem)
        mine.start()
        first = [copy(0, me, sibling, src=x_ref)]
        first += [copy(1 + j, me, (*chip, c), src=x_ref) for j, chip in enumerate(chips)]
        for cp in first:
            cp.start()
        passed = [copy(4 + j, (*chip, c), sibling) for j, chip in enumerate(chips)]
        for j, chip in enumerate(chips):
            copy(1 + j, (*chip, c), me).wait_recv()
            passed[j].start()
        copy(0, sibling, me).wait_recv()
        for j, chip in enumerate(chips):
            copy(4 + j, (*chip, 1 - c), me).wait_recv()
        for cp in first + passed:
            cp.wait_send()
        mine.wait()

    return pl.pallas_call(
        body, name=name,
        out_shape=jax.ShapeDtypeStruct((N_DEV * m_per, n), xs.dtype),
        in_specs=[pl.BlockSpec(memory_space=pltpu.VMEM)],
        out_specs=pl.BlockSpec(memory_space=pltpu.VMEM),
        scratch_shapes=[pltpu.SemaphoreType.DMA((7,)), pltpu.SemaphoreType.DMA((7,)), pltpu.SemaphoreType.DMA],
        compiler_params=_cp(),
    )(xs)


def gather_weights(shards, name):
    n = len(shards)

    def body(*refs):
        ins, outs = refs[:n], refs[n:2 * n]
        send_sems, recv_sems, fsend_sems, frecv_sems, local_sems = refs[2 * n:]
        x, y, c, chips = _place()
        me_chip = 2 * x + y
        sibling = (x, y, 1 - c)
        locs = []
        for a in range(n):
            cp = pltpu.make_async_copy(ins[a], outs[a].at[me_chip], local_sems.at[a])
            cp.start()
            locs.append(cp)

        def ici(a, k, src_chip, to):
            return pltpu.make_async_remote_copy(
                src_ref=ins[a].at[c], dst_ref=outs[a].at[src_chip, c],
                send_sem=send_sems.at[a * 3 + k], recv_sem=recv_sems.at[a * 3 + k], device_id=to, device_id_type=MESH_T)

        def d2d(a, k, src_chip, half):
            return pltpu.make_async_remote_copy(
                src_ref=outs[a].at[src_chip, half], dst_ref=outs[a].at[src_chip, half],
                send_sem=fsend_sems.at[a * 3 + k], recv_sem=frecv_sems.at[a * 3 + k],
                device_id=sibling, device_id_type=MESH_T)

        firsts = []
        for k, (cx, cy) in enumerate(chips):
            for a in range(n):
                cp = ici(a, k, me_chip, (cx, cy, c))
                cp.start()
                firsts.append(cp)
        passed = []
        for k, (cx, cy) in enumerate(chips):
            for a in range(n):
                ici(a, k, 2 * cx + cy, (cx, cy, c)).wait_recv()
                cp = d2d(a, k, 2 * cx + cy, c)
                cp.start()
                passed.append(cp)
        for k, (cx, cy) in enumerate(chips):
            for a in range(n):
                d2d(a, k, 2 * cx + cy, 1 - c).wait_recv()
        for cp in firsts + passed:
            cp.wait_send()
        for cp in locs:
            cp.wait()

    return pl.pallas_call(
        body, name=name,
        out_shape=[jax.ShapeDtypeStruct((N_CHIPS,) + s.shape, s.dtype) for s in shards],
        in_specs=[ANY] * n, out_specs=[ANY] * n,
        scratch_shapes=[pltpu.SemaphoreType.DMA((3 * n,)), pltpu.SemaphoreType.DMA((3 * n,)),
                        pltpu.SemaphoreType.DMA((3 * n,)), pltpu.SemaphoreType.DMA((3 * n,)),
                        pltpu.SemaphoreType.DMA((n,))],
        compiler_params=_cp(),
    )(*shards)


def _half_rows(ref, half, rh):
    idx = (slice(None),) * (len(ref.shape) - 2) + (pl.ds(pl.multiple_of(half * rh, 16), rh), slice(None))
    return ref.at[idx]


def sibling_exchange_halves(grads, name):
    n = len(grads)

    def body(*refs):
        ins, outs = refs[:n], refs[n:2 * n]
        send_sems, recv_sems = refs[2 * n:]
        x, y, c, _ = _place()
        cps = []
        for a in range(n):
            rh = ins[a].shape[-2] // 2
            cp = pltpu.make_async_remote_copy(
                src_ref=_half_rows(ins[a], 1 - c, rh), dst_ref=outs[a],
                send_sem=send_sems.at[a], recv_sem=recv_sems.at[a], device_id=(x, y, 1 - c), device_id_type=MESH_T)
            cp.start()
            cps.append(cp)
        for cp in cps:
            cp.wait()

    return pl.pallas_call(
        body, name=name,
        out_shape=[jax.ShapeDtypeStruct(g.shape[:-2] + (g.shape[-2] // 2, g.shape[-1]), g.dtype) for g in grads],
        in_specs=[ANY] * n, out_specs=[ANY] * n,
        scratch_shapes=[pltpu.SemaphoreType.DMA((n,)), pltpu.SemaphoreType.DMA((n,))],
        compiler_params=_cp(),
    )(*grads)


def pair_sum(g, recv, cidx, name):
    L, NS, R, C = g.shape
    rh = R // 2
    tr = _row_tile(rh, 256 if C > 1024 else 512)
    nt = rh // tr

    def body(c_ref, g_ref, r_ref, o_ref):
        o_ref[...] = (g_ref[...].astype(F32) + r_ref[...].astype(F32)).astype(BF16)

    grid_spec = pltpu.PrefetchScalarGridSpec(
        num_scalar_prefetch=1, grid=(L, NS, nt),
        in_specs=[pl.BlockSpec((None, None, tr, C), lambda l, s, t, c: (l, s, c[0] * nt + t, 0)),
                  pl.BlockSpec((None, None, tr, C), lambda l, s, t, c: (l, s, t, 0))],
        out_specs=pl.BlockSpec((None, None, tr, C), lambda l, s, t, c: (l, s, t, 0)))
    return pl.pallas_call(
        body, name=name, grid_spec=grid_spec,
        out_shape=jax.ShapeDtypeStruct((L, NS, rh, C), BF16),
        compiler_params=_cp(("arbitrary", "arbitrary", "arbitrary")),
    )(cidx, g, recv)


def chip_exchange(parts, name):
    n = len(parts)

    def body(*refs):
        ins, outs = refs[:n], refs[n:2 * n]
        send_sems, recv_sems = refs[2 * n:]
        x, y, c, chips = _place()
        cps = []
        for k, (cx, cy) in enumerate(chips):
            for a in range(n):
                cp = pltpu.make_async_remote_copy(
                    src_ref=ins[a].at[:, 2 * cx + cy], dst_ref=outs[a].at[k],
                    send_sem=send_sems.at[a * 3 + k], recv_sem=recv_sems.at[a * 3 + k],
                    device_id=(cx, cy, c), device_id_type=MESH_T)
                cp.start()
                cps.append(cp)
        for cp in cps:
            cp.wait()

    return pl.pallas_call(
        body, name=name,
        out_shape=[jax.ShapeDtypeStruct((3, p.shape[0]) + p.shape[2:], p.dtype) for p in parts],
        in_specs=[ANY] * n, out_specs=[ANY] * n,
        scratch_shapes=[pltpu.SemaphoreType.DMA((3 * n,)), pltpu.SemaphoreType.DMA((3 * n,))],
        compiler_params=_cp(),
    )(*parts)


def chip_sum(part, recv, chip_c, name, dest=None, slot=0, n_slots=1):
    _, NS, RH, C = part.shape
    tr = _row_tile(RH, 256 if C > 1024 else 512)
    nt = RH // tr

    def body(cc_ref, p_ref, r_ref, *rest):
        o_ref = rest[-1]
        acc = p_ref[...].astype(F32)
        for k in range(3):
            acc = acc + r_ref[k].astype(F32)
        o_ref[...] = acc

    in_specs = [pl.BlockSpec((None, None, tr, C), lambda t, cc: (0, cc[0], t, 0)),
                pl.BlockSpec((3, None, tr, C), lambda t, cc: (0, 0, t, 0))]
    args = [chip_c, part, recv]
    aliases = {}
    if dest is not None:
        in_specs.append(ANY)
        args.append(dest)
        aliases = {3: 0}
    grid_spec = pltpu.PrefetchScalarGridSpec(
        num_scalar_prefetch=1, grid=(nt,), in_specs=in_specs,
        out_specs=pl.BlockSpec((None, tr, C), lambda t, cc: (slot, cc[1] * nt + t, 0)))
    return pl.pallas_call(
        body, name=name, grid_spec=grid_spec,
        out_shape=jax.ShapeDtypeStruct((n_slots, 2 * RH, C), F32),
        input_output_aliases=aliases,
        compiler_params=_cp(("arbitrary",)),
    )(*args)


def sibling_fill_halves(bufs, name):
    n = len(bufs)

    def body(*refs):
        outs = refs[n:2 * n]
        send_sems, recv_sems = refs[2 * n:]
        x, y, c, _ = _place()
        cps = []
        for a in range(n):
            rh = outs[a].shape[-2] // 2
            cp = pltpu.make_async_remote_copy(
                src_ref=_half_rows(outs[a], c, rh), dst_ref=_half_rows(outs[a], c, rh),
                send_sem=send_sems.at[a], recv_sem=recv_sems.at[a], device_id=(x, y, 1 - c), device_id_type=MESH_T)
            cp.start()
            cps.append(cp)
        for a, cp in enumerate(cps):
            cp.wait_send()
        for a in range(n):
            rh = outs[a].shape[-2] // 2
            pltpu.make_async_remote_copy(
                src_ref=_half_rows(outs[a], 1 - c, rh), dst_ref=_half_rows(outs[a], 1 - c, rh),
                send_sem=send_sems.at[a], recv_sem=recv_sems.at[a], device_id=(x, y, 1 - c),
                device_id_type=MESH_T).wait_recv()

    return pl.pallas_call(
        body, name=name,
        out_shape=[jax.ShapeDtypeStruct(b.shape, b.dtype) for b in bufs],
        in_specs=[ANY] * n, out_specs=[ANY] * n,
        input_output_aliases={a: a for a in range(n)},
        scratch_shapes=[pltpu.SemaphoreType.DMA((n,)), pltpu.SemaphoreType.DMA((n,))],
        compiler_params=_cp(),
    )(*bufs)


HBM_SPEC = pl.BlockSpec(memory_space=pltpu.HBM)
SEM_SPEC = pl.BlockSpec(memory_space=pltpu.SEMAPHORE)
DATAFLOW_EFFECT = pltpu.SideEffectType.DATAFLOW_SIDE_EFFECTING


def split_start(bufs, copies_fn, n_copies, name, dep=None, token_shape=(8, 128)):
    nb = len(bufs)
    extra = [] if dep is None else [dep]

    def body(*refs):
        ne = len(extra)
        send_sems, recv_sems = refs[nb + ne], refs[nb + ne + 1]
        token = refs[-1]
        for cp in copies_fn(refs[:nb], send_sems, recv_sems):
            cp.start()
        token[...] = jnp.zeros_like(token)

    outs = pl.pallas_call(
        body, name=name,
        out_shape=(pltpu.SemaphoreType.DMA((n_copies,)), pltpu.SemaphoreType.DMA((n_copies,)),
                   *[pltpu.HBM(b.shape, b.dtype) for b in bufs], jax.ShapeDtypeStruct(token_shape, F32)),
        in_specs=[HBM_SPEC] * nb + [ANY] * len(extra),
        out_specs=(SEM_SPEC, SEM_SPEC, *[HBM_SPEC] * nb, pl.BlockSpec(memory_space=pltpu.VMEM)),
        input_output_aliases={i: 2 + i for i in range(nb)},
        compiler_params=pltpu.CompilerParams(has_side_effects=DATAFLOW_EFFECT),
    )(*[pltpu.with_memory_space_constraint(b, pltpu.HBM) for b in bufs], *extra)
    return outs[0], outs[1], list(outs[2:2 + nb]), outs[-1]


def split_wait(send_sems, recv_sems, bufs, after, copies_fn, name):
    nb = len(bufs)

    def body(*refs):
        for cp in copies_fn(refs[:nb], refs[nb], refs[nb + 1]):
            cp.wait_send()
            cp.wait_recv()

    outs = pl.pallas_call(
        body, name=name,
        out_shape=tuple(pltpu.HBM(b.shape, b.dtype) for b in bufs),
        in_specs=[HBM_SPEC] * nb + [SEM_SPEC, SEM_SPEC, ANY],
        out_specs=tuple([HBM_SPEC] * nb),
        input_output_aliases={i: i for i in range(nb)},
        compiler_params=pltpu.CompilerParams(has_side_effects=DATAFLOW_EFFECT),
    )(*bufs, send_sems, recv_sems, after)
    return list(outs)


def chip_exchange_copies(refs, send_sems, recv_sems):
    n = len(refs) // 2
    x, y, c, chips = _place()
    cps = []
    for a in range(n):
        for k, (cx, cy) in enumerate(chips):
            cps.append(pltpu.make_async_remote_copy(
                src_ref=refs[a].at[:, 2 * cx + cy], dst_ref=refs[n + a].at[k],
                send_sem=send_sems.at[a * 3 + k], recv_sem=recv_sems.at[a * 3 + k],
                device_id=(cx, cy, c), device_id_type=MESH_T))
    return cps


def fill_copies(refs, send_sems, recv_sems):
    x, y, c, _ = _place()
    cps = []
    for a in range(len(refs)):
        rh = refs[a].shape[-2] // 2
        cps.append(pltpu.make_async_remote_copy(
            src_ref=_half_rows(refs[a], c, rh), dst_ref=_half_rows(refs[a], c, rh),
            send_sem=send_sems.at[a], recv_sem=recv_sems.at[a], device_id=(x, y, 1 - c), device_id_type=MESH_T))
    return cps


def allgather_direct_copies(refs, send_sems, recv_sems):
    x, y, c, _ = _place()
    me = 4 * x + 2 * y + c
    cps = []
    for r in range(1, N_DEV):
        fx, fy, fc = (r >> 2) & 1, (r >> 1) & 1, r & 1
        peer = (1 - x if fx else x, 1 - y if fy else y, 1 - c if fc else c)
        cps.append(pltpu.make_async_remote_copy(
            src_ref=refs[0], dst_ref=refs[1].at[me], send_sem=send_sems.at[r - 1], recv_sem=recv_sems.at[r - 1],
            device_id=peer, device_id_type=MESH_T))
    return cps


def pair_exchange_copies(refs, send_sems, recv_sems):
    n = len(refs) // 2
    x, y, c, _ = _place()
    return [pltpu.make_async_remote_copy(
        src_ref=_half_rows(refs[a], 1 - c, refs[a].shape[-2] // 2), dst_ref=refs[n + a],
        send_sem=send_sems.at[a], recv_sem=recv_sems.at[a], device_id=(x, y, 1 - c), device_id_type=MESH_T)
        for a in range(n)]


def weight_ici_copies(refs, send_sems, recv_sems):
    x, y, c, chips = _place()
    cps = []
    for a in range(len(refs)):
        for k, (cx, cy) in enumerate(chips):
            cps.append(pltpu.make_async_remote_copy(
                src_ref=refs[a].at[2 * x + y, c], dst_ref=refs[a].at[2 * x + y, c],
                send_sem=send_sems.at[a * 3 + k], recv_sem=recv_sems.at[a * 3 + k],
                device_id=(cx, cy, c), device_id_type=MESH_T))
    return cps


def weight_d2d_copies(refs, send_sems, recv_sems):
    x, y, c, chips = _place()
    cps = []
    for a in range(len(refs)):
        for k, (cx, cy) in enumerate(chips):
            cps.append(pltpu.make_async_remote_copy(
                src_ref=refs[a].at[2 * cx + cy, c], dst_ref=refs[a].at[2 * cx + cy, c],
                send_sem=send_sems.at[a * 3 + k], recv_sem=recv_sems.at[a * 3 + k],
                device_id=(x, y, 1 - c), device_id_type=MESH_T))
    return cps


def cast_place(w, lead, chip_arr, name, dep=None):
    R, C = w.shape[-2:]
    tr = _row_tile(R, 256 if C > 1024 else 512)

    def body(chip_ref, w_ref, *rest):
        rest[-1][...] = w_ref[...].astype(BF16)

    in_specs = [pl.BlockSpec((None,) * len(lead) + (tr, C), lambda t, ch: (*lead, t, 0))]
    args = [chip_arr, w]
    if dep is not None:
        in_specs.append(ANY)
        args.append(dep)
    grid_spec = pltpu.PrefetchScalarGridSpec(
        num_scalar_prefetch=1, grid=(R // tr,), in_specs=in_specs,
        out_specs=pl.BlockSpec((None, tr, C), lambda t, ch: (ch[0], t, 0)))
    return pl.pallas_call(
        body, name=name, grid_spec=grid_spec,
        out_shape=jax.ShapeDtypeStruct((N_CHIPS, R, C), BF16),
        compiler_params=_cp(("arbitrary",)),
    )(*args)


def _s5_prepare(lam_re, lam_im, log_dt, b_re, b_im, c_re, c_im, groups_per_block):
    G, P = lam_re.shape
    N = b_re.shape[-1]
    gb = groups_per_block
    nblk = G // gb
    dt = jnp.exp(log_dt)[:, None]
    e = jnp.exp(lam_re * dt)
    a_re = e * jnp.cos(lam_im * dt)
    a_im = e * jnp.sin(lam_im * dt)
    n2 = lam_re * lam_re + lam_im * lam_im
    co_re = ((a_re - 1.0) * lam_re + a_im * lam_im) / n2
    co_im = (a_im * lam_re - (a_re - 1.0) * lam_im) / n2
    bb_re = co_re[..., None] * b_re - co_im[..., None] * b_im
    bb_im = co_re[..., None] * b_im + co_im[..., None] * b_re
    eye = jnp.eye(gb, dtype=F32)

    def blockdiag_np(m):
        m = m.reshape(nblk, gb, N, P)
        return jnp.einsum('bgnp,gh->bgnhp', m, eye).reshape(nblk, gb * N, gb * P)

    b_np_re = jnp.swapaxes(bb_re, 1, 2)
    b_np_im = jnp.swapaxes(bb_im, 1, 2)
    return (a_re.reshape(nblk, 1, gb * P), a_im.reshape(nblk, 1, gb * P),
            blockdiag_np(b_np_re), blockdiag_np(b_np_im), blockdiag_np(c_re), blockdiag_np(c_im))


def _to_scan_order(a):
    S, D = a.shape
    return a.reshape(SCAN_LANES, S // SCAN_LANES, D).transpose(1, 0, 2).reshape(S, D)


def _from_scan_order(a):
    S, D = a.shape
    return a.reshape(S // SCAN_LANES, SCAN_LANES, D).transpose(1, 0, 2).reshape(S, D)


def _pad_rows(a, mult=8):
    r = (-a.shape[0]) % mult
    if r:
        a = jnp.concatenate([a, jnp.zeros((r, a.shape[1]), a.dtype)], axis=0)
    return a


def _pack_rows(arrs, width):
    parts, offs, o = [], [], 0
    for a in arrs:
        flat = a.reshape(-1)
        r = (-flat.shape[0]) % (16 * width)
        if r:
            flat = jnp.concatenate([flat, jnp.zeros((r,), flat.dtype)])
        p = flat.reshape(-1, width)
        parts.append(p)
        offs.append((o, a.shape, a.size))
        o += p.shape[0]
    if o % 64:
        parts.append(jnp.zeros((64 - o % 64, width), parts[0].dtype))
    return jnp.concatenate(parts, axis=0), offs


def _unpack_rows(packed, offs):
    outs = []
    for o, shape, size in offs:
        rows = -(-size // packed.shape[1])
        outs.append(packed[o:o + rows].reshape(-1)[:size].reshape(shape))
    return outs


def kernel(x, c, ada_w, ada_b, norm_pre, norm_post, ffn_w_in, ffn_w_out, ab_w_in, pool_w, pool_scale, sgu_ln_g, sgu_ln_b, sgu_w, sgu_b, ab_w_out, ssm_w_in, ssm_lam_re, ssm_lam_im, ssm_b_re, ssm_b_im, ssm_c_re, ssm_c_im, ssm_d, ssm_log_dt, ssm_w_glu, loss_target, m_ada_w, m_ada_b, m_norm_pre, m_norm_post, m_ffn_w_in, m_ffn_w_out, m_ab_w_in, m_pool_w, m_pool_scale, m_sgu_ln_g, m_sgu_ln_b, m_sgu_w, m_sgu_b, m_ab_w_out, m_ssm_w_in, m_ssm_lam_re, m_ssm_lam_im, m_ssm_b_re, m_ssm_b_im, m_ssm_c_re, m_ssm_c_im, m_ssm_d, m_ssm_log_dt, m_ssm_w_glu, v_ada_w, v_ada_b, v_norm_pre, v_norm_post, v_ffn_w_in, v_ffn_w_out, v_ab_w_in, v_pool_w, v_pool_scale, v_sgu_ln_g, v_sgu_ln_b, v_sgu_w, v_sgu_b, v_ab_w_out, v_ssm_w_in, v_ssm_lam_re, v_ssm_lam_im, v_ssm_b_re, v_ssm_b_im, v_ssm_c_re, v_ssm_c_im, v_ssm_d, v_ssm_log_dt, v_ssm_w_glu):
    weights = dict(ada_w=ada_w, ada_b=ada_b, norm_pre=norm_pre, norm_post=norm_post, ffn_w_in=ffn_w_in,
                   ffn_w_out=ffn_w_out, ab_w_in=ab_w_in, pool_w=pool_w, pool_scale=pool_scale, sgu_ln_g=sgu_ln_g,
                   sgu_ln_b=sgu_ln_b, sgu_w=sgu_w, sgu_b=sgu_b, ab_w_out=ab_w_out, ssm_w_in=ssm_w_in,
                   ssm_lam_re=ssm_lam_re, ssm_lam_im=ssm_lam_im, ssm_b_re=ssm_b_re, ssm_b_im=ssm_b_im,
                   ssm_c_re=ssm_c_re, ssm_c_im=ssm_c_im, ssm_d=ssm_d, ssm_log_dt=ssm_log_dt, ssm_w_glu=ssm_w_glu)
    m_in = dict(ada_w=m_ada_w, ada_b=m_ada_b, norm_pre=m_norm_pre, norm_post=m_norm_post, ffn_w_in=m_ffn_w_in,
                ffn_w_out=m_ffn_w_out, ab_w_in=m_ab_w_in, pool_w=m_pool_w, pool_scale=m_pool_scale,
                sgu_ln_g=m_sgu_ln_g, sgu_ln_b=m_sgu_ln_b, sgu_w=m_sgu_w, sgu_b=m_sgu_b, ab_w_out=m_ab_w_out,
                ssm_w_in=m_ssm_w_in, ssm_lam_re=m_ssm_lam_re, ssm_lam_im=m_ssm_lam_im, ssm_b_re=m_ssm_b_re,
                ssm_b_im=m_ssm_b_im, ssm_c_re=m_ssm_c_re, ssm_c_im=m_ssm_c_im, ssm_d=m_ssm_d,
                ssm_log_dt=m_ssm_log_dt, ssm_w_glu=m_ssm_w_glu)
    v_in = dict(ada_w=v_ada_w, ada_b=v_ada_b, norm_pre=v_norm_pre, norm_post=v_norm_post, ffn_w_in=v_ffn_w_in,
                ffn_w_out=v_ffn_w_out, ab_w_in=v_ab_w_in, pool_w=v_pool_w, pool_scale=v_pool_scale,
                sgu_ln_g=v_sgu_ln_g, sgu_ln_b=v_sgu_ln_b, sgu_w=v_sgu_w, sgu_b=v_sgu_b, ab_w_out=v_ab_w_out,
                ssm_w_in=v_ssm_w_in, ssm_lam_re=v_ssm_lam_re, ssm_lam_im=v_ssm_lam_im, ssm_b_re=v_ssm_b_re,
                ssm_b_im=v_ssm_b_im, ssm_c_re=v_ssm_c_re, ssm_c_im=v_ssm_c_im, ssm_d=v_ssm_d,
                ssm_log_dt=v_ssm_log_dt, ssm_w_glu=v_ssm_w_glu)
    names = list(weights.keys())

    xi, yi, ci = lax.axis_index("x"), lax.axis_index("y"), lax.axis_index("c")
    chip = 2 * xi + yi
    me = 4 * xi + 2 * yi + ci
    S, D = x.shape[1], x.shape[2]
    L = ada_w.shape[0]
    NSUB = norm_pre.shape[1]
    DS = norm_pre.shape[2]
    FS = ffn_w_in.shape[-1]
    FR = ffn_w_out.shape[-2]
    x0 = x[0]
    target = loss_target[0]

    small_parts = [_pad_rows(p) for p in (c.reshape(D // DS, DS), norm_pre.reshape(L * NSUB, DS),
                                          norm_post.reshape(L * NSUB, DS), ssm_d.reshape(1, DS))]
    small_offs = [0]
    for p in small_parts:
        small_offs.append(small_offs[-1] + p.shape[0])
    small_all = allgather_small(jnp.concatenate(small_parts, axis=0), "ag_small").reshape(N_DEV, -1, DS)
    c_all = small_all[:, :D // DS].reshape(N_DEV, D)
    per_chip = small_all[0::2]
    o = small_offs[1]
    g_pre_full = jnp.moveaxis(per_chip[:, o:o + L * NSUB], 0, 1).reshape(L, NSUB, D)
    o = small_offs[2]
    g_post_full = jnp.moveaxis(per_chip[:, o:o + L * NSUB], 0, 1).reshape(L, NSUB, D)
    o = small_offs[3]
    d_full = jnp.moveaxis(per_chip[:, o:o + 1], 0, 1).reshape(1, D)

    pieces = []
    for l in range(L):
        pieces.append((f"ffn_{l}_0", [(ffn_w_in, (l, 0)), (ffn_w_out, (l, 0))]))
        if l % 2 == 0:
            pieces.append((f"mix_{l}", [(ab_w_in, (l // 2,)), (ab_w_out, (l // 2,))]))
        else:
            pieces.append((f"mix_{l}", [(ssm_w_in, (l // 2,)), (ssm_w_glu, (l // 2,))]))
        pieces.append((f"ffn_{l}_1", [(ffn_w_in, (l, 1)), (ffn_w_out, (l, 1))]))
    tags = [tag for tag, _ in pieces]
    chip_arr = chip.reshape(1).astype(jnp.int32)
    wg_started, wg_passing = {}, {}

    def cast_piece(tag, ws, dep):
        wg_started[tag] = [cast_place(w, lead, chip_arr, f"wg_cast_{tag}_{i}", dep=dep).reshape(
            N_CHIPS, 2, w.shape[-2] // 2, w.shape[-1]) for i, (w, lead) in enumerate(ws)]

    def start_piece(tag, dep):
        bufs = wg_started[tag]
        wg_started[tag] = split_start(bufs, weight_ici_copies, 3 * len(bufs), f"wg_ici_start_{tag}", dep=dep,
                                      token_shape=(8, D))
        return wg_started[tag][3]

    def weights_arrived(i, after):
        ssem, rsem, bufs, _ = wg_started[tags[i]]
        lands = split_wait(ssem, rsem, bufs, after, weight_ici_copies, f"wg_ici_wait_{tags[i]}")
        wg_passing[i] = split_start(lands, weight_d2d_copies, 3 * len(lands), f"wg_d2d_start_{tags[i]}")
        return wg_passing[i][3][0, 0]

    def weights_of(i, after):
        ssem, rsem, lands, _ = wg_passing[i]
        return split_wait(ssem, rsem, lands, after, weight_d2d_copies, f"wg_d2d_wait_{tags[i]}")

    NS = ada_w.shape[-1]
    ada_b_shard = lax.dynamic_slice_in_dim(ada_b, chip * NS, NS, axis=1).reshape(L, 1, NS)
    mod_part = ada_mod(c_all, ada_w, ada_b_shard, "ada_mod")
    mod_all = allgather_small(mod_part.reshape(L * N_DEV, NS), "ag_mod").reshape(N_DEV, L, N_DEV, NS)
    mod_mine = lax.dynamic_index_in_dim(mod_all[0::2], me, axis=2, keepdims=False)
    mod = jnp.moveaxis(mod_mine, 0, 1).reshape(L, NSUB, 3, D)
    cast_piece(*pieces[0], mod_all)
    first_token = start_piece(tags[0], mod_all)
    wg_tokens = [first_token]
    for tag, ws in pieces[1:]:
        cast_piece(tag, ws, first_token)
    for tag in tags[1:WG_AHEAD]:
        wg_tokens.append(start_piece(tag, first_token))

    def vec_of(l, s):
        return jnp.concatenate([mod[l, s], g_pre_full[l, s][None], g_post_full[l, s][None],
                                jnp.zeros((3, D), F32)], axis=0)

    GB = 8
    s5_args = (ssm_lam_re[0], ssm_lam_im[0], ssm_log_dt[0], ssm_b_re[0], ssm_b_im[0], ssm_c_re[0], ssm_c_im[0])
    (a_re, a_im, bblk_re, bblk_im, cblk_re, cblk_im), s5_vjp = jax.vjp(lambda *p: _s5_prepare(*p, GB), *s5_args)
    bre16, bim16 = bblk_re.astype(BF16), bblk_im.astype(BF16)
    cre16, cim16 = cblk_re.astype(BF16), cblk_im.astype(BF16)
    cre16_t, cim16_t = jnp.swapaxes(cre16, 1, 2), jnp.swapaxes(cim16, 1, 2)

    pool_w16 = pool_w[0].astype(BF16)
    sgu_bexp = jnp.broadcast_to(sgu_b[0][:, :, None], sgu_w[0].shape)

    saved = {}
    ffn_w = {}
    xcur = x0
    stage = [0]

    def next_weights(after, vec):
        i = stage[0]
        stage[0] += 1
        weights_arrived(i, after)
        if i + WG_AHEAD < len(tags):
            vec = vec + start_piece(tags[i + WG_AHEAD], wg_passing[i][3])
        return weights_of(i, after), vec

    for l in range(L):
        v0 = vec_of(l, 0)
        g, v0p = next_weights(xcur if l else sum(wg_tokens, v0), v0)
        ffn_w[(l, 0)] = (g[0].reshape(N_CHIPS, D, FS), g[1].reshape(N_CHIPS, FR, D))
        out, h, u, f = ffn_fwd(xcur, v0p, *ffn_w[(l, 0)], f"ffn_fwd_{l}_0")
        saved[(l, 0)] = (xcur, v0, h, u, f)
        xcur = out
        v1 = vec_of(l, 1)
        mix_g, v1p = next_weights(xcur, v1)
        if l % 2 == 0:
            abin_g = mix_g[0].reshape(N_CHIPS, D, -1)
            about_g = mix_g[1].reshape(-1, D)
            h, z = pre_matmul(xcur, v1p, abin_g, f"mixa_in_{l}")
            ycat = mixa_core_fwd(z, pool_w16, pool_scale, sgu_ln_g, sgu_ln_b, sgu_w[0], sgu_bexp, f"mixa_core_{l}")
            out, f = matmul_post(ycat, about_g, xcur, v1, f"mixa_out_{l}")
            saved[(l, 1)] = (xcur, v1, h, z, ycat, f)
        else:
            sin_g = mix_g[0].reshape(1, -1, D)
            glu_g = mix_g[1].reshape(N_CHIPS, D, -1)
            h, uu = pre_matmul(xcur, v1p, sin_g, f"s5_in_{l}")
            us = _to_scan_order(uu)
            ys = s5_core_fwd(us, a_re, a_im, bre16, bim16, cre16_t, cim16_t, d_full, f"s5_core_{l}")
            yy = _from_scan_order(ys)
            out, f = s5_glu_fwd(yy, glu_g, xcur, v1, f"s5_glu_{l}")
            saved[(l, 1)] = (xcur, v1, h, us, yy, f)
        xcur = out
        v2 = vec_of(l, 2)
        g, v2p = next_weights(xcur, v2)
        ffn_w[(l, 1)] = (g[0].reshape(N_CHIPS, D, FS), g[1].reshape(N_CHIPS, FR, D))
        out, h, u, f = ffn_fwd(xcur, v2p, *ffn_w[(l, 1)], f"ffn_fwd_{l}_1")
        saved[(l, 2)] = (xcur, v2, h, u, f)
        xcur = out

    dcur, sq = loss_head(xcur, target, "loss_head")
    loss = lax.psum(sq[0, 0], ("x", "y", "c")) * (0.5 / D)

    accs = {}
    small_g = {}
    cidx = ci.reshape(1).astype(jnp.int32)
    chip_c = jnp.stack([chip, ci]).astype(jnp.int32)
    rs_open, rs_pending = [], []

    pin = [zero_after(loss.reshape(1, 1), "after_loss", (8, D))]
    last_start = [pin[0]]

    def pin_behind(token):
        pin.append(token)
        last_start[0] = token

    def pinned(v):
        for token in pin:
            v = v + (token if token.shape == v.shape else token[0, 0])
        pin.clear()
        return v

    def rs_begin(tag, items):
        arrs = [it[0] for it in items]
        lands = [lax.empty(a.shape[:-2] + (a.shape[-2] // 2, a.shape[-1]), BF16) for a in arrs]
        started = split_start(arrs + lands, pair_exchange_copies, len(arrs), f"rs_pair_start_{tag}",
                              token_shape=(8, D))
        rs_open.append((tag, items, started))
        pin_behind(started[3])

    def rs_advance(after):
        while rs_open:
            tag, items, (ssem, rsem, bufs, _) = rs_open.pop(0)
            n = len(items)
            bufs = split_wait(ssem, rsem, bufs, after, pair_exchange_copies, f"rs_pair_wait_{tag}")
            parts = [pair_sum(a, r, cidx, f"rs_pair_sum_{tag}_{i}") for i, (a, r) in enumerate(zip(bufs[:n], bufs[n:]))]
            lands = [lax.empty((3, 1) + p.shape[2:], BF16) for p in parts]
            started = split_start(parts + lands, chip_exchange_copies, 3 * n, f"rs_chip_start_{tag}",
                                  token_shape=(8, D))
            rs_pending.append((tag, items, started))
            pin_behind(started[3])

    def ffn_back(l, s, k, dcur, before_wgrads=None):
        xin, vv, h, u, f = saved[(l, s)]
        vv = pinned(vv)
        dx, df, du, act, acc = ffn_bwd(dcur, xin, f, u, vv, *ffn_w[(l, k)], f"ffn_bwd_{l}_{k}")
        accs[(l, s)] = acc
        rs_advance(dx)
        dep = None if before_wgrads is None else before_wgrads()
        g_win = tn_matmul(h, du, 1024, f"ffn_dwin_{l}_{k}", dep=dep)
        g_wout = tn_matmul_cols(act, df, FS, D, f"ffn_dwout_{l}_{k}")
        rs_begin(f"ffn_{l}_{k}", [(g_win[None], "ffn_w_in", 2 * l + k, 2 * L),
                                  (g_wout.reshape(1, N_CHIPS, FR, D), "ffn_w_out", 2 * l + k, 2 * L)])
        return dx

    grads = {}

    small_names = ["pool_w", "pool_scale", "sgu_ln_g", "sgu_ln_b", "sgu_w", "sgu_b", "ssm_lam_re", "ssm_lam_im",
                   "ssm_b_re", "ssm_b_im", "ssm_c_re", "ssm_c_im", "ssm_log_dt"]
    small_open = []

    def small_grads_start():
        acc_all = jnp.stack([jnp.stack([accs[(l, s)] for s in range(NSUB)]) for l in range(L)])
        dmod_mine = acc_all[:, :, A_SHIFT:A_GATE + 1].reshape(L, NSUB * 3 * D)
        packed, offs = _pack_rows([dmod_mine, acc_all[:, :, A_GPRE], acc_all[:, :, A_GPOST], dd_mine]
                                  + [small_g[n] for n in small_names], D)
        packed = pinned(packed).astype(BF16)
        started = split_start([packed, lax.empty((N_DEV,) + packed.shape, BF16)], allgather_direct_copies,
                              N_DEV - 1, "ag_grads_start")
        small_open.append((started, offs))
        pin_behind(started[3])
        return started[3]

    def small_grads_finish(after):
        (ssem, rsem, bufs, _), offs = small_open.pop()
        own, land = split_wait(ssem, rsem, bufs, after, allgather_direct_copies, "ag_grads_wait")
        me_arr = me.reshape(1).astype(jnp.int32)
        summed = _unpack_rows(sum_gathered(land, own, me_arr, "sum_small"), offs)
        grads.update({n: g for n, g in zip(small_names, summed[4:])})
        grads["ada_b"] = summed[0]
        grads["norm_pre"] = lax.dynamic_slice_in_dim(summed[1], chip * DS, DS, axis=2)
        grads["norm_post"] = lax.dynamic_slice_in_dim(summed[2], chip * DS, DS, axis=2)
        grads["ssm_d"] = lax.dynamic_slice_in_dim(summed[3], chip * DS, DS, axis=1)
        nm = L * NSUB * 3
        is_me = (jnp.arange(N_DEV) == me)[:, None, None]
        dmod_all = jnp.where(is_me, own[None, :nm], land[:, :nm]).reshape(N_DEV, L, NSUB * 3 * D)
        dmod_shard = lax.dynamic_slice_in_dim(jnp.moveaxis(dmod_all, 0, 1), chip * NS, NS, axis=2)
        grads["ada_w"] = ada_grad(c_all, dmod_shard, "ada_grad")

    for l in reversed(range(L)):
        dcur = ffn_back(l, 2, 1, dcur)
        if l % 2 == 0:
            xin, vv, h, z, ycat, f = saved[(l, 1)]
            vv = pinned(vv)
            df, dact, acc_post = post_bwd_matmul(dcur, f, vv, about_g, f"mixa_out_bwd_{l}")
            rs_advance(dact)
            g_about = tn_matmul_cols(ycat, df, 512, D, f"mixa_dwout_{l}")
            dz, dpw, dvecw, dsw, dsb = mixa_core_bwd(z, dact, pool_w16, pool_scale, sgu_ln_g, sgu_ln_b, sgu_w[0],
                                                     sgu_bexp, f"mixa_core_bwd_{l}")
            g_abin = tn_matmul_cols(h, dz, 512, abin_g.shape[-1], f"mixa_dwin_{l}")
            dcur, acc_pre = matmul_pre_bwd(dz, abin_g, xin, dcur, vv, f"mixa_in_bwd_{l}")
            accs[(l, 1)] = acc_pre + acc_post
            rs_begin(f"mixa_{l}", [(g_abin[None], "ab_w_in", 0, 1),
                                   (g_about.reshape(1, N_CHIPS, -1, D), "ab_w_out", 0, 1)])
            small_g.update(pool_w=dpw[None], pool_scale=dvecw[0:1], sgu_ln_g=dvecw[1:2], sgu_ln_b=dvecw[2:3],
                           sgu_w=dsw[None], sgu_b=dsb.T[None])
        else:
            xin, vv, h, us, yy, f = saved[(l, 1)]
            vv = pinned(vv)
            dy, dab, gact, acc_post = s5_glu_bwd(dcur, f, yy, vv, glu_g, f"s5_glu_bwd_{l}")
            rs_advance(dy)
            g_glu = tn_matmul_cols(gact, dab, 512, glu_g.shape[-1], f"s5_dwglu_{l}")
            dys = _to_scan_order(dy)
            dus, dbre, dbim, dcre_t, dcim_t, dar, dai, dd = s5_core_bwd(
                us, dys, a_re, a_im, bre16, bim16, cre16, cim16, d_full, f"s5_core_bwd_{l}")
            du = _from_scan_order(dus).astype(BF16)
            g_sin = tn_matmul_cols(h, du, 512, D, f"s5_dwin_{l}")
            dcur, acc_pre = matmul_pre_bwd(du, sin_g, xin, dcur, vv, f"s5_in_bwd_{l}")
            accs[(l, 1)] = acc_pre + acc_post
            rs_begin(f"s5_{l}", [(g_sin.reshape(1, N_CHIPS, -1, D), "ssm_w_in", 0, 1), (g_glu[None], "ssm_w_glu", 0, 1)])
            s5_grads = s5_vjp((dar, dai, dbre, dbim, jnp.swapaxes(dcre_t, 1, 2), jnp.swapaxes(dcim_t, 1, 2)))
            small_g.update(ssm_lam_re=s5_grads[0][None], ssm_lam_im=s5_grads[1][None], ssm_log_dt=s5_grads[2][None],
                           ssm_b_re=s5_grads[3][None], ssm_b_im=s5_grads[4][None], ssm_c_re=s5_grads[5][None],
                           ssm_c_im=s5_grads[6][None])
            dd_mine = dd
        dcur = ffn_back(l, 0, 0, dcur, before_wgrads=small_grads_start if l == 0 else None)
    grad_x = dcur[None]
    rs_advance(last_start[0])
    small_grads_finish(last_start[0])

    big_names = ["ffn_w_in", "ffn_w_out", "ab_w_in", "ab_w_out", "ssm_w_in", "ssm_w_glu"]
    deltas, new_m, new_v = {}, {}, {}

    def update(n, dep=None):
        g = grads[n].reshape(weights[n].shape)
        outs = adamw_nd(weights[n], g, m_in[n], v_in[n], f"adamw_{n}", emit_grad=n in big_names, dep=dep)
        grads[n] = outs[0] if n in big_names else g
        deltas[n], new_m[n], new_v[n] = outs[-3:]

    fin = {}

    def chip_sums(pending, after):
        for tag, items, (ssem, rsem, bufs, _) in pending:
            bufs = split_wait(ssem, rsem, bufs, after, chip_exchange_copies, f"rs_chip_wait_{tag}")
            n = len(items)
            for i, (it, p, r) in enumerate(zip(items, bufs[:n], bufs[n:])):
                fin[it[1]] = chip_sum(p, r, chip_c, f"rs_chip_sum_{tag}_{i}", dest=fin.get(it[1]), slot=it[2],
                                      n_slots=it[3])

    chip_sums(rs_pending[:-1], pinned(grads["ada_b"]))
    update("ada_w", dep=fin["ab_w_in"])
    chip_sums(rs_pending[-1:], deltas["ada_w"])
    ssem, rsem, bufs, fill_token = split_start([fin[n] for n in big_names], fill_copies, len(big_names), "rs_fill_start")
    for n in names:
        if n not in big_names and n != "ada_w":
            update(n, dep=fill_token)
    filled = split_wait(ssem, rsem, bufs, deltas["ssm_log_dt"], fill_copies, "rs_fill_wait")
    for n, g in zip(big_names, filled):
        grads[n] = g
        update(n)

    return (loss, grad_x, *[grads[n] for n in names], *[deltas[n] for n in names],
            *[new_m[n] for n in names], *[new_v[n] for n in names])
```

```python
import math

import jax
import jax.numpy as jnp
from jax import lax
from jax.experimental import pallas as pl
from jax.experimental.pallas import tpu as pltpu

F32 = jnp.float32
BF16 = jnp.bfloat16
EPS = 1e-6
MESH_T = pl.DeviceIdType.MESH
VMEM_LIMIT_BYTES = 56 * 1024 * 1024
N_CHIPS = 4
N_DEV = 8
POOL_WINDOWS = (2, 4, 8, 16)
CHUNK = 128
SCAN_LANES = 8
SCAN_UNROLL = 4
WG_AHEAD = 2
FFN_BWD_CHUNK = 2048
ADAM_LR = 0.001
ADAM_B1 = 0.9
ADAM_B2 = 0.999
ADAM_EPS = 1e-08
ADAM_WD = 0.01
ADAM_STEP = 10
GELU_C = math.sqrt(2.0 / math.pi)
GELU_K = 0.044715

V_SHIFT, V_SCALE, V_GATE, V_GPRE, V_GPOST = 0, 1, 2, 3, 4
A_SHIFT, A_SCALE, A_GATE, A_GPRE, A_GPOST = 0, 1, 2, 3, 4

ANY = pl.BlockSpec(memory_space=pl.ANY)


def _cp(sem=None):
    if sem is None:
        return pltpu.CompilerParams(vmem_limit_bytes=VMEM_LIMIT_BYTES)
    return pltpu.CompilerParams(vmem_limit_bytes=VMEM_LIMIT_BYTES, dimension_semantics=sem)


def _dot(a, b):
    return jnp.dot(a, b, preferred_element_type=F32)


def _dot_nt(a, b):
    return lax.dot_general(a, b, (((1,), (1,)), ((), ())), preferred_element_type=F32)


def _dot_tn(a, b):
    return lax.dot_general(a, b, (((0,), (0,)), ((), ())), preferred_element_type=F32)


def _sigmoid(x):
    return 1.0 / (1.0 + jnp.exp(-x))


def _gelu(x):
    return 0.5 * x * (1.0 + jnp.tanh(GELU_C * (x + GELU_K * x * x * x)))


def _gelu_grad(x):
    t = jnp.tanh(GELU_C * (x + GELU_K * x * x * x))
    return 0.5 * (1.0 + t) + 0.5 * x * (1.0 - t * t) * GELU_C * (1.0 + 3.0 * GELU_K * x * x)


def _rowsum(v):
    return jnp.sum(v, axis=0, keepdims=True)


def _lanemean(v):
    return jnp.mean(v, axis=-1, keepdims=True)


def _row(ref, i):
    return ref[pl.ds(i, 1), :]


def _pre_fwd(x, vec_ref):
    r = lax.rsqrt(_lanemean(x * x) + EPS)
    return (x * r) * _row(vec_ref, V_GPRE) * (1.0 + _row(vec_ref, V_SCALE)) + _row(vec_ref, V_SHIFT)


def _pre_bwd(x, dh, vec_ref):
    g = _row(vec_ref, V_GPRE)
    sc = 1.0 + _row(vec_ref, V_SCALE)
    r = lax.rsqrt(_lanemean(x * x) + EPS)
    xn = x * r
    dhx = dh * xn
    t2 = dh * (g * sc)
    dx = r * (t2 - xn * _lanemean(t2 * xn))
    return dx, _rowsum(dh), _rowsum(dhx * g), _rowsum(dhx * sc)


def _post_fwd(x, f, vec_ref, rw):
    q = lax.rsqrt(_lanemean(f * f) + EPS)
    return x + (rw * _row(vec_ref, V_GATE)) * (f * q * _row(vec_ref, V_GPOST))


def _post_bwd(dout, f, vec_ref, rw):
    gp = _row(vec_ref, V_GPOST)
    gate = _row(vec_ref, V_GATE)
    q = lax.rsqrt(_lanemean(f * f) + EPS)
    fhat = f * q
    dgate = _rowsum(dout * (rw * fhat * gp))
    dy = dout * (rw * gate)
    dgpost = _rowsum(dy * fhat)
    t = dy * gp
    df = q * (t - fhat * _lanemean(t * fhat))
    return df, dgate, dgpost


def _acc_add(acc_ref, row, v):
    acc_ref[pl.ds(row, 1), :] += v


def ffn_fwd(x, vec, win_g, wout_g, name, tm=512):
    S, D = x.shape
    FS = win_g.shape[-1]
    FR = wout_g.shape[-2]

    def body(x_ref, vec_ref, win_hbm, wout_hbm, out_ref, h_ref, u_ref, f_ref, win_s, wout_s, sem):
        @pl.when(pl.program_id(0) == 0)
        def _():
            cps = [pltpu.make_async_copy(win_hbm.at[j], win_s.at[j], sem.at[j]) for j in range(N_CHIPS)]
            cps += [pltpu.make_async_copy(wout_hbm.at[j], wout_s.at[pl.ds(j * FR, FR), :], sem.at[N_CHIPS + j])
                    for j in range(N_CHIPS)]
            for cp in cps:
                cp.start()
            for cp in cps:
                cp.wait()

        xv = x_ref[...]
        h = _pre_fwd(xv, vec_ref).astype(BF16)
        h_ref[...] = h
        f = jnp.zeros((tm, D), F32)
        for j in range(2):
            a = _dot(h, win_s[j])
            b = _dot(h, win_s[2 + j])
            u_ref[j] = a.astype(BF16)
            u_ref[2 + j] = b.astype(BF16)
            act = (a * _sigmoid(a) * b).astype(BF16)
            f = f + _dot(act, wout_s[pl.ds(j * FS, FS), :])
        f_ref[...] = f
        out_ref[...] = _post_fwd(xv, f, vec_ref, 0.5)

    row = pl.BlockSpec((tm, D), lambda i: (i, 0))
    return pl.pallas_call(
        body, name=name, grid=(S // tm,),
        in_specs=[row, pl.BlockSpec((8, D), lambda i: (0, 0)), ANY, ANY],
        out_specs=[row, row, pl.BlockSpec((N_CHIPS, tm, FS), lambda i: (0, i, 0)), row],
        out_shape=[jax.ShapeDtypeStruct((S, D), F32), jax.ShapeDtypeStruct((S, D), BF16),
                   jax.ShapeDtypeStruct((N_CHIPS, S, FS), BF16), jax.ShapeDtypeStruct((S, D), F32)],
        scratch_shapes=[pltpu.VMEM((N_CHIPS, D, FS), BF16), pltpu.VMEM((N_CHIPS * FR, D), BF16),
                        pltpu.SemaphoreType.DMA((2 * N_CHIPS,))],
        compiler_params=_cp(("arbitrary",)),
    )(x, vec, win_g, wout_g)


def ffn_bwd(dout, x, f, u, vec, win_g, wout_g, name, tm=256):
    S, D = x.shape
    FS = win_g.shape[-1]
    FR = wout_g.shape[-2]

    def body(dout_ref, x_ref, f_ref, u_ref, vec_ref, win_hbm, wout_hbm,
             dx_ref, df_ref, du_ref, act_ref, acc_ref, win_s, wout_s, sem):
        @pl.when(pl.program_id(0) == 0)
        def _():
            cps = [pltpu.make_async_copy(win_hbm.at[j], win_s.at[j], sem.at[j]) for j in range(N_CHIPS)]
            cps += [pltpu.make_async_copy(wout_hbm.at[j], wout_s.at[pl.ds(j * FR, FR), :], sem.at[N_CHIPS + j])
                    for j in range(N_CHIPS)]
            for cp in cps:
                cp.start()
            acc_ref[...] = jnp.zeros_like(acc_ref)
            for cp in cps:
                cp.wait()

        dout_v = dout_ref[...]
        df, dgate, dgpost = _post_bwd(dout_v, f_ref[...], vec_ref, 0.5)
        dfb = df.astype(BF16)
        df_ref[...] = dfb
        dh = jnp.zeros((tm, D), F32)
        chunks = [(c0, min(FFN_BWD_CHUNK, FS - c0)) for c0 in range(0, FS, FFN_BWD_CHUNK)]
        for j in range(2):
            for c0, cw in chunks:
                a = u_ref[j, :, c0:c0 + cw].astype(F32)
                b = u_ref[2 + j, :, c0:c0 + cw].astype(F32)
                sig = _sigmoid(a)
                sl = a * sig
                dact = _dot_nt(dfb, wout_s[pl.ds(j * FS + c0, cw), :])
                da = (dact * b * (sig * (1.0 + a * (1.0 - sig)))).astype(BF16)
                db = (dact * sl).astype(BF16)
                du_ref[j, :, c0:c0 + cw] = da
                du_ref[2 + j, :, c0:c0 + cw] = db
                act_ref[:, j * FS + c0:j * FS + c0 + cw] = (sl * b).astype(BF16)
                dh = dh + _dot_nt(da, win_s[j, :, c0:c0 + cw]) + _dot_nt(db, win_s[2 + j, :, c0:c0 + cw])
        dx, dshift, dscale, dgpre = _pre_bwd(x_ref[...], dh, vec_ref)
        dx_ref[...] = dout_v + dx
        _acc_add(acc_ref, A_SHIFT, dshift)
        _acc_add(acc_ref, A_SCALE, dscale)
        _acc_add(acc_ref, A_GATE, dgate)
        _acc_add(acc_ref, A_GPRE, dgpre)
        _acc_add(acc_ref, A_GPOST, dgpost)

    row = pl.BlockSpec((tm, D), lambda i: (i, 0))
    ublk = pl.BlockSpec((N_CHIPS, tm, FS), lambda i: (0, i, 0))
    const = pl.BlockSpec((8, D), lambda i: (0, 0))
    return pl.pallas_call(
        body, name=name, grid=(S // tm,),
        in_specs=[row, row, row, ublk, const, ANY, ANY],
        out_specs=[row, row, ublk, pl.BlockSpec((tm, 2 * FS), lambda i: (i, 0)), const],
        out_shape=[jax.ShapeDtypeStruct((S, D), F32), jax.ShapeDtypeStruct((S, D), BF16),
                   jax.ShapeDtypeStruct((N_CHIPS, S, FS), BF16), jax.ShapeDtypeStruct((S, 2 * FS), BF16),
                   jax.ShapeDtypeStruct((8, D), F32)],
        scratch_shapes=[pltpu.VMEM((N_CHIPS, D, FS), BF16), pltpu.VMEM((N_CHIPS * FR, D), BF16),
                        pltpu.SemaphoreType.DMA((2 * N_CHIPS,))],
        compiler_params=_cp(("arbitrary",)),
    )(dout, x, f, u, vec, win_g, wout_g)


def tn_matmul(a, b, tk, name, dep=None):
    S, K = a.shape
    nb, _, tn = b.shape
    extra = [] if dep is None else [dep]

    def body(a_ref, b_ref, *rest):
        rest[-1][...] = _dot_tn(a_ref[...], b_ref[...]).astype(BF16)

    return pl.pallas_call(
        body, name=name, grid=(K // tk, nb),
        in_specs=[pl.BlockSpec((S, tk), lambda i, j: (0, i)), pl.BlockSpec((None, S, tn), lambda i, j: (j, 0, 0))]
        + [ANY] * len(extra),
        out_specs=pl.BlockSpec((None, tk, tn), lambda i, j: (j, i, 0)),
        out_shape=jax.ShapeDtypeStruct((nb, K, tn), BF16),
        compiler_params=_cp(("arbitrary", "arbitrary")),
    )(a, b, *extra)


def tn_matmul_cols(a, b, tk, tn, name):
    S, K = a.shape
    N = b.shape[1]

    def body(a_ref, b_ref, o_ref):
        o_ref[...] = _dot_tn(a_ref[...], b_ref[...]).astype(BF16)

    return pl.pallas_call(
        body, name=name, grid=(K // tk, N // tn),
        in_specs=[pl.BlockSpec((S, tk), lambda i, j: (0, i)), pl.BlockSpec((S, tn), lambda i, j: (0, j))],
        out_specs=pl.BlockSpec((None, tk, tn), lambda i, j: (j, i, 0)),
        out_shape=jax.ShapeDtypeStruct((N // tn, K, tn), BF16),
        compiler_params=_cp(("arbitrary", "arbitrary")),
    )(a, b)


def pre_matmul(x, vec, w3, name, tm=256):
    S, D = x.shape
    nj, _, Nj = w3.shape

    def body(x_ref, vec_ref, w_ref, h_ref, z_ref):
        h = _pre_fwd(x_ref[...], vec_ref).astype(BF16)
        h_ref[...] = h
        for j in range(nj):
            z_ref[:, j * Nj:(j + 1) * Nj] = _dot(h, w_ref[j])

    row = pl.BlockSpec((tm, D), lambda i: (i, 0))
    return pl.pallas_call(
        body, name=name, grid=(S // tm,),
        in_specs=[row, pl.BlockSpec((8, D), lambda i: (0, 0)), pl.BlockSpec((nj, D, Nj), lambda i: (0, 0, 0))],
        out_specs=[row, pl.BlockSpec((tm, nj * Nj), lambda i: (i, 0))],
        out_shape=[jax.ShapeDtypeStruct((S, D), BF16), jax.ShapeDtypeStruct((S, nj * Nj), F32)],
        compiler_params=_cp(("arbitrary",)),
    )(x, vec, w3)


def matmul_pre_bwd(dz, w3, x, dres, vec, name, tm=256):
    S, D = x.shape
    nj, _, Nj = w3.shape

    def body(dz_ref, w_ref, x_ref, dres_ref, vec_ref, dx_ref, acc_ref):
        @pl.when(pl.program_id(0) == 0)
        def _():
            acc_ref[...] = jnp.zeros_like(acc_ref)

        dh = jnp.zeros((tm, D), F32)
        for j in range(nj):
            dh = dh + _dot_nt(dz_ref[:, j * Nj:(j + 1) * Nj], w_ref[j])
        dx, dshift, dscale, dgpre = _pre_bwd(x_ref[...], dh, vec_ref)
        dx_ref[...] = dres_ref[...] + dx
        _acc_add(acc_ref, A_SHIFT, dshift)
        _acc_add(acc_ref, A_SCALE, dscale)
        _acc_add(acc_ref, A_GPRE, dgpre)

    row = pl.BlockSpec((tm, D), lambda i: (i, 0))
    const = pl.BlockSpec((8, D), lambda i: (0, 0))
    return pl.pallas_call(
        body, name=name, grid=(S // tm,),
        in_specs=[pl.BlockSpec((tm, nj * Nj), lambda i: (i, 0)), pl.BlockSpec((nj, D, Nj), lambda i: (0, 0, 0)),
                  row, row, const],
        out_specs=[row, const],
        out_shape=[jax.ShapeDtypeStruct((S, D), F32), jax.ShapeDtypeStruct((8, D), F32)],
        compiler_params=_cp(("arbitrary",)),
    )(dz, w3, x, dres, vec)


def matmul_post(act, w, x, vec, name, tm=256):
    S, D = x.shape
    K = act.shape[1]

    def body(act_ref, w_ref, x_ref, vec_ref, out_ref, f_ref):
        f = _dot(act_ref[...], w_ref[...])
        f_ref[...] = f
        out_ref[...] = _post_fwd(x_ref[...], f, vec_ref, 1.0)

    row = pl.BlockSpec((tm, D), lambda i: (i, 0))
    return pl.pallas_call(
        body, name=name, grid=(S // tm,),
        in_specs=[pl.BlockSpec((tm, K), lambda i: (i, 0)), pl.BlockSpec((K, D), lambda i: (0, 0)), row,
                  pl.BlockSpec((8, D), lambda i: (0, 0))],
        out_specs=[row, row],
        out_shape=[jax.ShapeDtypeStruct((S, D), F32), jax.ShapeDtypeStruct((S, D), F32)],
        compiler_params=_cp(("arbitrary",)),
    )(act, w, x, vec)


def post_bwd_matmul(dout, f, vec, w, name, tm=256):
    S, D = dout.shape
    K = w.shape[0]

    def body(dout_ref, f_ref, vec_ref, w_ref, df_ref, dact_ref, acc_ref):
        @pl.when(pl.program_id(0) == 0)
        def _():
            acc_ref[...] = jnp.zeros_like(acc_ref)

        df, dgate, dgpost = _post_bwd(dout_ref[...], f_ref[...], vec_ref, 1.0)
        dfb = df.astype(BF16)
        df_ref[...] = dfb
        dact_ref[...] = _dot_nt(dfb, w_ref[...])
        _acc_add(acc_ref, A_GATE, dgate)
        _acc_add(acc_ref, A_GPOST, dgpost)

    row = pl.BlockSpec((tm, D), lambda i: (i, 0))
    const = pl.BlockSpec((8, D), lambda i: (0, 0))
    return pl.pallas_call(
        body, name=name, grid=(S // tm,),
        in_specs=[row, row, const, pl.BlockSpec((K, D), lambda i: (0, 0))],
        out_specs=[row, pl.BlockSpec((tm, K), lambda i: (i, 0)), const],
        out_shape=[jax.ShapeDtypeStruct((S, D), BF16), jax.ShapeDtypeStruct((S, K), F32),
                   jax.ShapeDtypeStruct((8, D), F32)],
        compiler_params=_cp(("arbitrary",)),
    )(dout, f, vec, w)


def _band(w, transposed, prev):
    r = lax.broadcasted_iota(jnp.int32, (CHUNK, CHUNK), 1 if transposed else 0)
    c = lax.broadcasted_iota(jnp.int32, (CHUNK, CHUNK), 0 if transposed else 1)
    d = r - c
    m = (d + CHUNK < w) if prev else ((d >= 0) & (d < w))
    return jnp.where(m, 1.0, 0.0).astype(BF16)


def _split_hi_lo(a):
    hi = a.astype(BF16)
    lo = (a - hi.astype(F32)).astype(BF16)
    return hi, lo


def _pool_diff(a, ap, g, denom):
    w = POOL_WINDOWS[g]
    a_hi, a_lo = _split_hi_lo(a)
    p_hi, p_lo = _split_hi_lo(ap)
    mc = _band(w, False, False)
    mp = _band(w, False, True)
    win = _dot(mc, a_hi) + _dot(mc, a_lo) + _dot(mp, p_hi) + _dot(mp, p_lo)
    return win / denom - a


def _sgu_norm(v, lg, lb):
    mu = _lanemean(v)
    xc = v - mu
    rstd = lax.rsqrt(_lanemean(xc * xc) + EPS)
    xhat = xc * rstd
    return xhat, rstd, xhat * lg + lb


def _tril_mask():
    r = lax.broadcasted_iota(jnp.int32, (CHUNK, CHUNK), 0)
    c = lax.broadcasted_iota(jnp.int32, (CHUNK, CHUNK), 1)
    return r >= c


def _positions(i, w):
    r = lax.broadcasted_iota(jnp.int32, (CHUNK, 128), 0)
    pos = (i * CHUNK + r + 1).astype(F32)
    return jnp.minimum(pos, float(w))


def mixa_core_fwd(z, pool_w, pool_scale, ln_g, ln_b, sgu_w, sgu_bexp, name):
    S = z.shape[0]
    W = z.shape[1] // 3
    G = len(POOL_WINDOWS)
    GD = W // G

    def body(zc_ref, zp_ref, pw_ref, ps_ref, lg_ref, lb_ref, sw_ref, sb_ref, y_ref):
        i = pl.program_id(0)
        has_prev = jnp.where(i > 0, 1.0, 0.0)
        for g in range(G):
            sl = slice(g * GD, (g + 1) * GD)
            a = zc_ref[:, sl]
            ap = zp_ref[:, sl] * has_prev
            d = _pool_diff(a, ap, g, _positions(i, POOL_WINDOWS[g])).astype(BF16)
            y_ref[:, sl] = (_dot(d, pw_ref[g]) * ps_ref[:, sl]).astype(BF16)
        tril = _tril_mask()
        for hh in range(G):
            u = _gelu(zc_ref[:, W + hh * GD:W + (hh + 1) * GD])
            v = _gelu(zc_ref[:, 2 * W + hh * GD:2 * W + (hh + 1) * GD])
            sl = slice(hh * GD, (hh + 1) * GD)
            _, _, vn = _sgu_norm(v, lg_ref[:, sl], lb_ref[:, sl])
            wm = jnp.where(tril, sw_ref[hh], 0.0).astype(BF16)
            s = _dot(wm, vn.astype(BF16)) + sb_ref[hh]
            y_ref[:, W + hh * GD:W + (hh + 1) * GD] = (u * s).astype(BF16)

    vecw = pl.BlockSpec((1, W), lambda i: (0, 0))
    mats = pl.BlockSpec((G, GD, GD), lambda i: (0, 0, 0))
    return pl.pallas_call(
        body, name=name, grid=(S // CHUNK,),
        in_specs=[pl.BlockSpec((CHUNK, 3 * W), lambda i: (i, 0)),
                  pl.BlockSpec((CHUNK, W), lambda i: (jnp.maximum(i - 1, 0), 0)),
                  mats, vecw, vecw, vecw, mats, mats],
        out_specs=pl.BlockSpec((CHUNK, 2 * W), lambda i: (i, 0)),
        out_shape=jax.ShapeDtypeStruct((S, 2 * W), BF16),
        compiler_params=_cp(("arbitrary",)),
    )(z, z, pool_w, pool_scale, ln_g, ln_b, sgu_w, sgu_bexp)


def mixa_core_bwd(z, dy, pool_w, pool_scale, ln_g, ln_b, sgu_w, sgu_bexp, name):
    S = z.shape[0]
    W = z.shape[1] // 3
    G = len(POOL_WINDOWS)
    GD = W // G
    n_tiles = S // CHUNK

    def body(zc_ref, zp_ref, dyc_ref, dyn_ref, pw_ref, ps_ref, lg_ref, lb_ref, sw_ref, sb_ref,
             dz_ref, dpw_ref, dvec_ref, dsw_ref, dsb_ref):
        i = pl.program_id(0)

        @pl.when(i == 0)
        def _():
            dpw_ref[...] = jnp.zeros_like(dpw_ref)
            dvec_ref[...] = jnp.zeros_like(dvec_ref)
            dsw_ref[...] = jnp.zeros_like(dsw_ref)
            dsb_ref[...] = jnp.zeros_like(dsb_ref)

        has_prev = jnp.where(i > 0, 1.0, 0.0)
        has_next = jnp.where(i < n_tiles - 1, 1.0, 0.0)
        for g in range(G):
            w = POOL_WINDOWS[g]
            sl = slice(g * GD, (g + 1) * GD)
            a = zc_ref[:, sl]
            ap = zp_ref[:, sl] * has_prev
            den_c = _positions(i, w)
            den_n = _positions(i + 1, w)
            d = _pool_diff(a, ap, g, den_c).astype(BF16)
            ps = ps_ref[:, sl]
            pw = pw_ref[g]
            dyc = dyc_ref[:, sl]
            dvec_ref[pl.ds(0, 1), sl] += _rowsum(dyc * _dot(d, pw))
            dyp_c = (dyc * ps).astype(BF16)
            dyp_n = (dyn_ref[:, sl] * (ps * has_next)).astype(BF16)
            dpw_ref[g] += _dot_tn(d, dyp_c)
            dd_c = _dot_nt(dyp_c, pw)
            dd_n = _dot_nt(dyp_n, pw)
            da = (_dot(_band(w, True, False), (dd_c / den_c).astype(BF16))
                  + _dot(_band(w, True, True), (dd_n / den_n).astype(BF16)) - dd_c)
            dz_ref[:, sl] = da.astype(BF16)
        tril = _tril_mask()
        for hh in range(G):
            sl = slice(hh * GD, (hh + 1) * GD)
            zu = zc_ref[:, W + hh * GD:W + (hh + 1) * GD]
            zv = zc_ref[:, 2 * W + hh * GD:2 * W + (hh + 1) * GD]
            u = _gelu(zu)
            v = _gelu(zv)
            lg = lg_ref[:, sl]
            xhat, rstd, vn = _sgu_norm(v, lg, lb_ref[:, sl])
            vnb = vn.astype(BF16)
            wm = jnp.where(tril, sw_ref[hh], 0.0).astype(BF16)
            s = _dot(wm, vnb) + sb_ref[hh]
            dyb = dyc_ref[:, W + hh * GD:W + (hh + 1) * GD]
            du = dyb * s
            ds = dyb * u
            dsb_ref[:, hh:hh + 1] += jnp.sum(ds, axis=1, keepdims=True)
            dsb16 = ds.astype(BF16)
            dsw_ref[hh] += jnp.where(tril, _dot_nt(dsb16, vnb), 0.0)
            dvn = _dot_tn(wm, dsb16)
            dvec_ref[pl.ds(1, 1), sl] += _rowsum(dvn * xhat)
            dvec_ref[pl.ds(2, 1), sl] += _rowsum(dvn)
            dxh = dvn * lg
            dv = rstd * (dxh - _lanemean(dxh) - xhat * _lanemean(dxh * xhat))
            dz_ref[:, W + hh * GD:W + (hh + 1) * GD] = (du * _gelu_grad(zu)).astype(BF16)
            dz_ref[:, 2 * W + hh * GD:2 * W + (hh + 1) * GD] = (dv * _gelu_grad(zv)).astype(BF16)

    vecw = pl.BlockSpec((1, W), lambda i: (0, 0))
    mats = pl.BlockSpec((G, GD, GD), lambda i: (0, 0, 0))
    return pl.pallas_call(
        body, name=name, grid=(n_tiles,),
        in_specs=[pl.BlockSpec((CHUNK, 3 * W), lambda i: (i, 0)),
                  pl.BlockSpec((CHUNK, W), lambda i: (jnp.maximum(i - 1, 0), 0)),
                  pl.BlockSpec((CHUNK, 2 * W), lambda i: (i, 0)),
                  pl.BlockSpec((CHUNK, W), lambda i: (jnp.minimum(i + 1, n_tiles - 1), 0)),
                  mats, vecw, vecw, vecw, mats, mats],
        out_specs=[pl.BlockSpec((CHUNK, 3 * W), lambda i: (i, 0)), mats,
                   pl.BlockSpec((8, W), lambda i: (0, 0)), mats, pl.BlockSpec((CHUNK, G), lambda i: (0, 0))],
        out_shape=[jax.ShapeDtypeStruct((S, 3 * W), BF16), jax.ShapeDtypeStruct((G, GD, GD), F32),
                   jax.ShapeDtypeStruct((8, W), F32), jax.ShapeDtypeStruct((G, CHUNK, CHUNK), F32),
                   jax.ShapeDtypeStruct((CHUNK, G), F32)],
        compiler_params=_cp(("arbitrary",)),
    )(z, z, dy, dy, pool_w, pool_scale, ln_g, ln_b, sgu_w, sgu_bexp)


def _cmul(ar, ai, br, bi):
    return ar * br - ai * bi, ar * bi + ai * br


def _seg_rows(k):
    return pl.ds(pl.multiple_of(k * SCAN_LANES, SCAN_LANES), SCAN_LANES)


def _scan_fwd(xre, xim, carry, tab_re, tab_im, ar, ai, K):
    P = xre.shape[1]
    a8r = jnp.broadcast_to(ar, (SCAN_LANES, P))
    a8i = jnp.broadcast_to(ai, (SCAN_LANES, P))

    def local(k, c):
        pr, pi = c
        rows = _seg_rows(k)
        nr = a8r * pr - a8i * pi + xre[rows, :]
        ni = a8r * pi + a8i * pr + xim[rows, :]
        xre[rows, :] = nr
        xim[rows, :] = ni
        return nr, ni

    er, ei = lax.fori_loop(1, K, local, (xre[pl.ds(0, SCAN_LANES), :], xim[pl.ds(0, SCAN_LANES), :]),
                           unroll=SCAN_UNROLL)
    tab_re[pl.ds(0, SCAN_LANES), :] = a8r
    tab_im[pl.ds(0, SCAN_LANES), :] = a8i
    n, pr, pi = 1, ar, ai
    while n < K:
        m = min(n, K - n)
        tr, ti = _cmul(tab_re[pl.ds(0, SCAN_LANES * m), :], tab_im[pl.ds(0, SCAN_LANES * m), :], pr, pi)
        tab_re[pl.ds(SCAN_LANES * n, SCAN_LANES * m), :] = tr
        tab_im[pl.ds(SCAN_LANES * n, SCAN_LANES * m), :] = ti
        pr, pi = _cmul(pr, pi, pr, pi)
        n *= 2
    akr = tab_re[pl.ds(SCAN_LANES * (K - 1), 1), :]
    aki = tab_im[pl.ds(SCAN_LANES * (K - 1), 1), :]
    cr = jnp.zeros((1, P), F32)
    ci = jnp.zeros((1, P), F32)
    carry[pl.ds(0, 1), :] = cr
    carry[pl.ds(SCAN_LANES, 1), :] = ci
    for j in range(1, SCAN_LANES):
        tr, ti = _cmul(akr, aki, cr, ci)
        cr = er[j - 1:j, :] + tr
        ci = ei[j - 1:j, :] + ti
        carry[pl.ds(j, 1), :] = cr
        carry[pl.ds(SCAN_LANES + j, 1), :] = ci
    cmr = carry[pl.ds(0, SCAN_LANES), :]
    cmi = carry[pl.ds(SCAN_LANES, SCAN_LANES), :]

    def fix(k, c):
        rows = _seg_rows(k)
        tr, ti = _cmul(tab_re[rows, :], tab_im[rows, :], cmr, cmi)
        xre[rows, :] += tr
        xim[rows, :] += ti
        return c

    lax.fori_loop(0, K, fix, 0, unroll=SCAN_UNROLL)


def s5_core_fwd(u, ar, ai, bre, bim, cre_t, cim_t, dskip, name):
    S, D = u.shape
    nblk, UB, PB = bre.shape
    K = S // SCAN_LANES

    def body(u_ref, ar_ref, ai_ref, bre_ref, bim_ref, cre_ref, cim_ref, d_ref, y_ref, xre, xim, carry, tab_re, tab_im):
        uv = u_ref[...]
        ub = uv.astype(BF16)
        xre[...] = _dot(ub, bre_ref[...])
        xim[...] = _dot(ub, bim_ref[...])
        _scan_fwd(xre, xim, carry, tab_re, tab_im, ar_ref[...], ai_ref[...], K)
        y_ref[...] = (_dot(xre[...].astype(BF16), cre_ref[...]) - _dot(xim[...].astype(BF16), cim_ref[...])
                      + d_ref[...] * uv)

    ucol = pl.BlockSpec((S, UB), lambda i: (0, i))
    pvec = pl.BlockSpec((None, 1, PB), lambda i: (i, 0, 0))
    bmat = pl.BlockSpec((None, UB, PB), lambda i: (i, 0, 0))
    cmat = pl.BlockSpec((None, PB, UB), lambda i: (i, 0, 0))
    return pl.pallas_call(
        body, name=name, grid=(nblk,),
        in_specs=[ucol, pvec, pvec, bmat, bmat, cmat, cmat, pl.BlockSpec((1, UB), lambda i: (0, i))],
        out_specs=ucol,
        out_shape=jax.ShapeDtypeStruct((S, D), F32),
        scratch_shapes=[pltpu.VMEM((S, PB), F32), pltpu.VMEM((S, PB), F32), pltpu.VMEM((2 * SCAN_LANES, PB), F32),
                        pltpu.VMEM((S, PB), F32), pltpu.VMEM((S, PB), F32)],
        compiler_params=_cp(("arbitrary",)),
    )(u, ar, ai, bre, bim, cre_t, cim_t, dskip)


def s5_core_bwd(u, dy, ar, ai, bre, bim, cre, cim, dskip, name):
    S, D = u.shape
    nblk, UB, PB = bre.shape
    K = S // SCAN_LANES

    def body(u_ref, dy_ref, ar_ref, ai_ref, bre_ref, bim_ref, cre_ref, cim_ref, d_ref,
             du_ref, dbre_ref, dbim_ref, dcre_ref, dcim_ref, dar_ref, dai_ref, dd_ref,
             xre, xim, gre, gim, carry, carry_b, tab_re, tab_im):
        ar = ar_ref[...]
        ai = ai_ref[...]
        uv = u_ref[...]
        ub = uv.astype(BF16)
        dyv = dy_ref[...]
        dyb = dyv.astype(BF16)
        xre[...] = _dot(ub, bre_ref[...])
        xim[...] = _dot(ub, bim_ref[...])
        _scan_fwd(xre, xim, carry, tab_re, tab_im, ar, ai, K)
        dcre_ref[...] = _dot_tn(xre[...].astype(BF16), dyb)
        dcim_ref[...] = -_dot_tn(xim[...].astype(BF16), dyb)
        gre[...] = _dot(dyb, cre_ref[...])
        gim[...] = -_dot(dyb, cim_ref[...])

        a8r = jnp.broadcast_to(ar, (SCAN_LANES, PB))
        a8i = jnp.broadcast_to(ai, (SCAN_LANES, PB))

        def local(s, c):
            k = K - 2 - s
            nr, ni = c
            rows = _seg_rows(k)
            tr = gre[rows, :] + a8r * nr + a8i * ni
            ti = gim[rows, :] + a8r * ni - a8i * nr
            gre[rows, :] = tr
            gim[rows, :] = ti
            return tr, ti

        last = _seg_rows(K - 1)
        fr, fi = lax.fori_loop(0, K - 1, local, (gre[last, :], gim[last, :]), unroll=SCAN_UNROLL)
        akr = tab_re[pl.ds(SCAN_LANES * (K - 1), 1), :]
        aki = -tab_im[pl.ds(SCAN_LANES * (K - 1), 1), :]
        cr = jnp.zeros((1, PB), F32)
        ci = jnp.zeros((1, PB), F32)
        carry_b[pl.ds(SCAN_LANES - 1, 1), :] = cr
        carry_b[pl.ds(2 * SCAN_LANES - 1, 1), :] = ci
        for j in range(SCAN_LANES - 2, -1, -1):
            tr, ti = _cmul(akr, aki, cr, ci)
            cr = fr[j + 1:j + 2, :] + tr
            ci = fi[j + 1:j + 2, :] + ti
            carry_b[pl.ds(j, 1), :] = cr
            carry_b[pl.ds(SCAN_LANES + j, 1), :] = ci
        cbr = carry_b[pl.ds(0, SCAN_LANES), :]
        cbi = carry_b[pl.ds(SCAN_LANES, SCAN_LANES), :]

        def fix_rows(s, rows, c, xr, xi):
            dar, dai = c
            pw = _seg_rows(s)
            pr, pi = tab_re[pw, :], tab_im[pw, :]
            g_r = gre[rows, :] + (pr * cbr + pi * cbi)
            g_i = gim[rows, :] + (pr * cbi - pi * cbr)
            gre[rows, :] = g_r
            gim[rows, :] = g_i
            dar = dar + g_r * xr + g_i * xi
            dai = dai + g_i * xr - g_r * xi
            return dar, dai

        def fix(s, c):
            k = K - 1 - s
            prev = _seg_rows(k - 1)
            return fix_rows(s, _seg_rows(k), c, xre[prev, :], xim[prev, :])

        z8 = jnp.zeros((SCAN_LANES, PB), F32)
        c = lax.fori_loop(0, K - 1, fix, (z8, z8), unroll=SCAN_UNROLL)
        dar, dai = fix_rows(K - 1, _seg_rows(0), c, carry[pl.ds(0, SCAN_LANES), :], carry[pl.ds(SCAN_LANES, SCAN_LANES), :])
        dar_ref[...] = _rowsum(dar)
        dai_ref[...] = _rowsum(dai)
        grb = gre[...].astype(BF16)
        gib = gim[...].astype(BF16)
        dbre_ref[...] = _dot_tn(ub, grb)
        dbim_ref[...] = _dot_tn(ub, gib)
        du_ref[...] = _dot_nt(grb, bre_ref[...]) + _dot_nt(gib, bim_ref[...]) + d_ref[...] * dyv
        dd_ref[...] = _rowsum(dyv * uv)

    ucol = pl.BlockSpec((S, UB), lambda i: (0, i))
    pvec = pl.BlockSpec((None, 1, PB), lambda i: (i, 0, 0))
    bmat = pl.BlockSpec((None, UB, PB), lambda i: (i, 0, 0))
    cmat = pl.BlockSpec((None, PB, UB), lambda i: (i, 0, 0))
    dvec = pl.BlockSpec((1, UB), lambda i: (0, i))
    return pl.pallas_call(
        body, name=name, grid=(nblk,),
        in_specs=[ucol, ucol, pvec, pvec, bmat, bmat, bmat, bmat, dvec],
        out_specs=[ucol, bmat, bmat, cmat, cmat, pvec, pvec, dvec],
        out_shape=[jax.ShapeDtypeStruct((S, D), F32),
                   jax.ShapeDtypeStruct((nblk, UB, PB), F32), jax.ShapeDtypeStruct((nblk, UB, PB), F32),
                   jax.ShapeDtypeStruct((nblk, PB, UB), F32), jax.ShapeDtypeStruct((nblk, PB, UB), F32),
                   jax.ShapeDtypeStruct((nblk, 1, PB), F32), jax.ShapeDtypeStruct((nblk, 1, PB), F32),
                   jax.ShapeDtypeStruct((1, D), F32)],
        scratch_shapes=[pltpu.VMEM((S, PB), F32), pltpu.VMEM((S, PB), F32), pltpu.VMEM((S, PB), F32),
                        pltpu.VMEM((S, PB), F32), pltpu.VMEM((2 * SCAN_LANES, PB), F32),
                        pltpu.VMEM((2 * SCAN_LANES, PB), F32), pltpu.VMEM((S, PB), F32), pltpu.VMEM((S, PB), F32)],
        compiler_params=_cp(("arbitrary",)),
    )(u, dy, ar, ai, bre, bim, cre, cim, dskip)


def s5_glu_fwd(y, wglu, x, vec, name, tm=256):
    S, D = x.shape
    NJ = wglu.shape[-1]

    def body(y_ref, w_ref, x_ref, vec_ref, out_ref, f_ref):
        g = _gelu(y_ref[...]).astype(BF16)
        f = jnp.concatenate([_dot(g, w_ref[j]) * _sigmoid(_dot(g, w_ref[2 + j])) for j in range(2)], axis=1)
        f_ref[...] = f
        out_ref[...] = _post_fwd(x_ref[...], f, vec_ref, 1.0)

    row = pl.BlockSpec((tm, D), lambda i: (i, 0))
    return pl.pallas_call(
        body, name=name, grid=(S // tm,),
        in_specs=[row, pl.BlockSpec((N_CHIPS, D, NJ), lambda i: (0, 0, 0)), row, pl.BlockSpec((8, D), lambda i: (0, 0))],
        out_specs=[row, row],
        out_shape=[jax.ShapeDtypeStruct((S, D), F32), jax.ShapeDtypeStruct((S, D), F32)],
        compiler_params=_cp(("arbitrary",)),
    )(y, wglu, x, vec)


def s5_glu_bwd(dout, f, y, vec, wglu, name, tm=256):
    S, D = dout.shape
    NJ = wglu.shape[-1]

    def body(dout_ref, f_ref, y_ref, vec_ref, w_ref, dy_ref, dab_ref, g_ref, acc_ref):
        @pl.when(pl.program_id(0) == 0)
        def _():
            acc_ref[...] = jnp.zeros_like(acc_ref)

        df, dgate, dgpost = _post_bwd(dout_ref[...], f_ref[...], vec_ref, 1.0)
        yv = y_ref[...]
        g = _gelu(yv).astype(BF16)
        g_ref[...] = g
        dg = jnp.zeros((tm, D), F32)
        for j in range(2):
            a = _dot(g, w_ref[j])
            sig = _sigmoid(_dot(g, w_ref[2 + j]))
            dfj = df[:, j * NJ:(j + 1) * NJ]
            da = (dfj * sig).astype(BF16)
            db = (dfj * a * sig * (1.0 - sig)).astype(BF16)
            dab_ref[:, j * NJ:(j + 1) * NJ] = da
            dab_ref[:, (2 + j) * NJ:(3 + j) * NJ] = db
            dg = dg + _dot_nt(da, w_ref[j]) + _dot_nt(db, w_ref[2 + j])
        dy_ref[...] = dg * _gelu_grad(yv)
        _acc_add(acc_ref, A_GATE, dgate)
        _acc_add(acc_ref, A_GPOST, dgpost)

    row = pl.BlockSpec((tm, D), lambda i: (i, 0))
    const = pl.BlockSpec((8, D), lambda i: (0, 0))
    return pl.pallas_call(
        body, name=name, grid=(S // tm,),
        in_specs=[row, row, row, const, pl.BlockSpec((N_CHIPS, D, NJ), lambda i: (0, 0, 0))],
        out_specs=[row, pl.BlockSpec((tm, N_CHIPS * NJ), lambda i: (i, 0)), row, const],
        out_shape=[jax.ShapeDtypeStruct((S, D), F32), jax.ShapeDtypeStruct((S, N_CHIPS * NJ), BF16),
                   jax.ShapeDtypeStruct((S, D), BF16), jax.ShapeDtypeStruct((8, D), F32)],
        compiler_params=_cp(("arbitrary",)),
    )(dout, f, y, vec, wglu)


def loss_head(y, target, name, tm=256):
    S, D = y.shape

    def body(y_ref, t_ref, d_ref, l_ref):
        @pl.when(pl.program_id(0) == 0)
        def _():
            l_ref[...] = jnp.zeros_like(l_ref)

        err = y_ref[...] - t_ref[...]
        d_ref[...] = err * (1.0 / D)
        l_ref[...] += jnp.sum(_rowsum(err * err), axis=1, keepdims=True)

    row = pl.BlockSpec((tm, D), lambda i: (i, 0))
    return pl.pallas_call(
        body, name=name, grid=(S // tm,),
        in_specs=[row, row],
        out_specs=[row, pl.BlockSpec((1, 1), lambda i: (0, 0))],
        out_shape=[jax.ShapeDtypeStruct((S, D), F32), jax.ShapeDtypeStruct((1, 1), F32)],
        compiler_params=_cp(("arbitrary",)),
    )(y, target)


def zero_after(dep, name, shape=(8, 128)):
    def body(dep_ref, o_ref):
        o_ref[...] = jnp.zeros_like(o_ref)

    return pl.pallas_call(body, name=name, in_specs=[ANY], out_specs=pl.BlockSpec(memory_space=pltpu.VMEM),
                          out_shape=jax.ShapeDtypeStruct(shape, F32), compiler_params=_cp())(dep)


def ada_mod(c_all, ada_w, ada_b_shard, name, tn=768, dep=None):
    B, D = c_all.shape
    L, _, NS = ada_w.shape
    extra = [] if dep is None else [dep]

    def body(c_ref, w_ref, b_ref, *rest):
        cv = c_ref[...]
        cond = (cv * _sigmoid(cv)).astype(BF16)
        rest[-1][...] = _dot(cond, w_ref[...].astype(BF16)) + b_ref[...]

    return pl.pallas_call(
        body, name=name, grid=(L, NS // tn),
        in_specs=[pl.BlockSpec((B, D), lambda l, j: (0, 0)), pl.BlockSpec((None, D, tn), lambda l, j: (l, 0, j)),
                  pl.BlockSpec((None, 1, tn), lambda l, j: (l, 0, j))] + [ANY] * len(extra),
        out_specs=pl.BlockSpec((None, B, tn), lambda l, j: (l, 0, j)),
        out_shape=jax.ShapeDtypeStruct((L, B, NS), F32),
        compiler_params=_cp(("arbitrary", "arbitrary")),
    )(c_all, ada_w, ada_b_shard, *extra)


def ada_grad(c_all, dmod, name, tn=768):
    B, D = c_all.shape
    L, _, NS = dmod.shape

    def body(c_ref, d_ref, o_ref):
        cv = c_ref[...]
        cond = (cv * _sigmoid(cv)).astype(BF16)
        o_ref[...] = _dot_tn(cond, d_ref[...].astype(BF16))

    return pl.pallas_call(
        body, name=name, grid=(L, NS // tn),
        in_specs=[pl.BlockSpec((B, D), lambda l, j: (0, 0)), pl.BlockSpec((None, B, tn), lambda l, j: (l, 0, j))],
        out_specs=pl.BlockSpec((None, D, tn), lambda l, j: (l, 0, j)),
        out_shape=jax.ShapeDtypeStruct((L, D, NS), F32),
        compiler_params=_cp(("arbitrary", "arbitrary")),
    )(c_all, dmod)


def sum_gathered(land, own, me_arr, name):
    n, R, C = land.shape

    def body(me_ref, land_ref, own_ref, o_ref):
        acc = jnp.zeros(o_ref.shape, F32)
        for i in range(n):
            acc = acc + jnp.where(me_ref[0] == i, own_ref[...], land_ref[i]).astype(F32)
        o_ref[...] = acc

    tr = _row_tile(R, 64)
    grid_spec = pltpu.PrefetchScalarGridSpec(
        num_scalar_prefetch=1, grid=(R // tr,),
        in_specs=[pl.BlockSpec((n, tr, C), lambda i, me: (0, i, 0)), pl.BlockSpec((tr, C), lambda i, me: (i, 0))],
        out_specs=pl.BlockSpec((tr, C), lambda i, me: (i, 0)))
    return pl.pallas_call(
        body, name=name, grid_spec=grid_spec,
        out_shape=jax.ShapeDtypeStruct((R, C), F32),
        compiler_params=_cp(("arbitrary",)),
    )(me_arr, land, own)


def _row_tile(R, cap=512):
    if R <= cap:
        return R
    for cand in (512, 384, 352, 256, 128, 64, 32, 16, 8):
        if cand <= cap and R % cand == 0:
            return cand
    return R


def adamw(w, g, m, v, name, emit_grad=False, dep=None):
    R, C = w.shape
    tr = _row_tile(R, 256 if C > 1024 else 512)
    bc1 = 1.0 - ADAM_B1 ** ADAM_STEP
    bc2 = 1.0 - ADAM_B2 ** ADAM_STEP
    n_out = 4 if emit_grad else 3
    extra = [] if dep is None else [dep]

    def body(w_ref, g_ref, m_ref, v_ref, *outs):
        d_ref, nm_ref, nv_ref = outs[-3:]
        gv = g_ref[...]
        if emit_grad:
            outs[-4][...] = gv
        nm = ADAM_B1 * m_ref[...] + (1.0 - ADAM_B1) * gv
        nv = ADAM_B2 * v_ref[...] + (1.0 - ADAM_B2) * (gv * gv)
        nm_ref[...] = nm
        nv_ref[...] = nv
        d_ref[...] = -ADAM_LR * ((nm / bc1) / (jnp.sqrt(nv / bc2) + ADAM_EPS) + ADAM_WD * w_ref[...])

    blk = pl.BlockSpec((tr, C), lambda i: (i, 0))
    sd = jax.ShapeDtypeStruct((R, C), F32)
    return pl.pallas_call(
        body, name=name, grid=(R // tr,),
        in_specs=[blk, blk, blk, blk] + [ANY] * len(extra), out_specs=[blk] * n_out, out_shape=[sd] * n_out,
        compiler_params=_cp(("arbitrary",)),
    )(w, g, m, v, *extra)


def _as2d(a):
    if a.ndim == 1:
        return a.reshape(1, -1)
    return a.reshape(-1, a.shape[-1])


SMALL_PARAM_ELEMS = 1 << 18


def adamw_nd(w, g, m, v, name, emit_grad=False, dep=None):
    extra = [] if dep is None else [dep]
    if w.size <= SMALL_PARAM_ELEMS and not emit_grad:
        bc1 = 1.0 - ADAM_B1 ** ADAM_STEP
        bc2 = 1.0 - ADAM_B2 ** ADAM_STEP

        def body(w_ref, g_ref, m_ref, v_ref, *rest):
            d_ref, nm_ref, nv_ref = rest[-3:]
            gv = g_ref[...]
            nm = ADAM_B1 * m_ref[...] + (1.0 - ADAM_B1) * gv
            nv = ADAM_B2 * v_ref[...] + (1.0 - ADAM_B2) * (gv * gv)
            nm_ref[...] = nm
            nv_ref[...] = nv
            d_ref[...] = -ADAM_LR * ((nm / bc1) / (jnp.sqrt(nv / bc2) + ADAM_EPS) + ADAM_WD * w_ref[...])

        sd = jax.ShapeDtypeStruct(w.shape, F32)
        whole = pl.BlockSpec(memory_space=pltpu.VMEM)
        return tuple(pl.pallas_call(body, name=name, in_specs=[whole] * 4 + [ANY] * len(extra), out_specs=[whole] * 3,
                                    out_shape=[sd, sd, sd], compiler_params=_cp())(w, g.reshape(w.shape), m, v, *extra))
    outs = adamw(_as2d(w), _as2d(g.reshape(w.shape)), _as2d(m), _as2d(v), name, emit_grad=emit_grad, dep=dep)
    return tuple(o.reshape(w.shape) for o in outs)


def _place():
    x, y, c = lax.axis_index("x"), lax.axis_index("y"), lax.axis_index("c")
    chips = [(1 - x, y), (x, 1 - y), (1 - x, 1 - y)]
    return x, y, c, chips


def allgather_small(xs, name):
    m_per, n = xs.shape

    def body(x_ref, out_ref, send_sems, recv_sems, local_sem):
        x, y, c, chips = _place()
        me, sibling = (x, y, c), (x, y, 1 - c)

        def rows(px, py, pc):
            return out_ref.at[pl.ds((4 * px + 2 * py + pc) * m_per, m_per), :]

        def copy(k, block, to, src=None):
            return pltpu.make_async_remote_copy(
                src_ref=rows(*block) if src is None else src, dst_ref=rows(*block),
                send_sem=send_sems.at[k], recv_sem=recv_sems.at[k], device_id=to, device_id_type=MESH_T)

        mine = pltpu.make_async_copy(x_ref, rows(*me), local_sem)
        mine.start()
        first = [copy(0, me, sibling, src=x_ref)]
        first += [copy(1 + j, me, (*chip, c), src=x_ref) for j, chip in enumerate(chips)]
        for cp in first:
            cp.start()
        passed = [copy(4 + j, (*chip, c), sibling) for j, chip in enumerate(chips)]
        for j, chip in enumerate(chips):
            copy(1 + j, (*chip, c), me).wait_recv()
            passed[j].start()
        copy(0, sibling, me).wait_recv()
        for j, chip in enumerate(chips):
            copy(4 + j, (*chip, 1 - c), me).wait_recv()
        for cp in first + passed:
            cp.wait_send()
        mine.wait()

    return pl.pallas_call(
        body, name=name,
        out_shape=jax.ShapeDtypeStruct((N_DEV * m_per, n), xs.dtype),
        in_specs=[pl.BlockSpec(memory_space=pltpu.VMEM)],
        out_specs=pl.BlockSpec(memory_space=pltpu.VMEM),
        scratch_shapes=[pltpu.SemaphoreType.DMA((7,)), pltpu.SemaphoreType.DMA((7,)), pltpu.SemaphoreType.DMA],
        compiler_params=_cp(),
    )(xs)


def _half_rows(ref, half, rh):
    idx = (slice(None),) * (len(ref.shape) - 2) + (pl.ds(pl.multiple_of(half * rh, 16), rh), slice(None))
    return ref.at[idx]


def pair_sum(g, recv, cidx, name):
    L, NS, R, C = g.shape
    rh = R // 2
    tr = _row_tile(rh, 256 if C > 1024 else 512)
    nt = rh // tr

    def body(c_ref, g_ref, r_ref, o_ref):
        o_ref[...] = (g_ref[...].astype(F32) + r_ref[...].astype(F32)).astype(BF16)

    grid_spec = pltpu.PrefetchScalarGridSpec(
        num_scalar_prefetch=1, grid=(L, NS, nt),
        in_specs=[pl.BlockSpec((None, None, tr, C), lambda l, s, t, c: (l, s, c[0] * nt + t, 0)),
                  pl.BlockSpec((None, None, tr, C), lambda l, s, t, c: (l, s, t, 0))],
        out_specs=pl.BlockSpec((None, None, tr, C), lambda l, s, t, c: (l, s, t, 0)))
    return pl.pallas_call(
        body, name=name, grid_spec=grid_spec,
        out_shape=jax.ShapeDtypeStruct((L, NS, rh, C), BF16),
        compiler_params=_cp(("arbitrary", "arbitrary", "arbitrary")),
    )(cidx, g, recv)


def chip_sum(part, recv, chip_c, name, dest=None, slot=0, n_slots=1):
    _, NS, RH, C = part.shape
    tr = _row_tile(RH, 256 if C > 1024 else 512)
    nt = RH // tr

    def body(cc_ref, p_ref, r_ref, *rest):
        o_ref = rest[-1]
        acc = p_ref[...].astype(F32)
        for k in range(3):
            acc = acc + r_ref[k].astype(F32)
        o_ref[...] = acc

    in_specs = [pl.BlockSpec((None, None, tr, C), lambda t, cc: (0, cc[0], t, 0)),
                pl.BlockSpec((3, None, tr, C), lambda t, cc: (0, 0, t, 0))]
    args = [chip_c, part, recv]
    aliases = {}
    if dest is not None:
        in_specs.append(ANY)
        args.append(dest)
        aliases = {3: 0}
    grid_spec = pltpu.PrefetchScalarGridSpec(
        num_scalar_prefetch=1, grid=(nt,), in_specs=in_specs,
        out_specs=pl.BlockSpec((None, tr, C), lambda t, cc: (slot, cc[1] * nt + t, 0)))
    return pl.pallas_call(
        body, name=name, grid_spec=grid_spec,
        out_shape=jax.ShapeDtypeStruct((n_slots, 2 * RH, C), F32),
        input_output_aliases=aliases,
        compiler_params=_cp(("arbitrary",)),
    )(*args)


HBM_SPEC = pl.BlockSpec(memory_space=pltpu.HBM)
SEM_SPEC = pl.BlockSpec(memory_space=pltpu.SEMAPHORE)
DATAFLOW_EFFECT = pltpu.SideEffectType.DATAFLOW_SIDE_EFFECTING


def split_start(bufs, copies_fn, n_copies, name, dep=None, token_shape=(8, 128)):
    nb = len(bufs)
    extra = [] if dep is None else [dep]

    def body(*refs):
        ne = len(extra)
        send_sems, recv_sems = refs[nb + ne], refs[nb + ne + 1]
        token = refs[-1]
        for cp in copies_fn(refs[:nb], send_sems, recv_sems):
            cp.start()
        token[...] = jnp.zeros_like(token)

    outs = pl.pallas_call(
        body, name=name,
        out_shape=(pltpu.SemaphoreType.DMA((n_copies,)), pltpu.SemaphoreType.DMA((n_copies,)),
                   *[pltpu.HBM(b.shape, b.dtype) for b in bufs], jax.ShapeDtypeStruct(token_shape, F32)),
        in_specs=[HBM_SPEC] * nb + [ANY] * len(extra),
        out_specs=(SEM_SPEC, SEM_SPEC, *[HBM_SPEC] * nb, pl.BlockSpec(memory_space=pltpu.VMEM)),
        input_output_aliases={i: 2 + i for i in range(nb)},
        compiler_params=pltpu.CompilerParams(has_side_effects=DATAFLOW_EFFECT),
    )(*[pltpu.with_memory_space_constraint(b, pltpu.HBM) for b in bufs], *extra)
    return outs[0], outs[1], list(outs[2:2 + nb]), outs[-1]


def split_wait(send_sems, recv_sems, bufs, after, copies_fn, name):
    nb = len(bufs)

    def body(*refs):
        for cp in copies_fn(refs[:nb], refs[nb], refs[nb + 1]):
            cp.wait_send()
            cp.wait_recv()

    outs = pl.pallas_call(
        body, name=name,
        out_shape=tuple(pltpu.HBM(b.shape, b.dtype) for b in bufs),
        in_specs=[HBM_SPEC] * nb + [SEM_SPEC, SEM_SPEC, ANY],
        out_specs=tuple([HBM_SPEC] * nb),
        input_output_aliases={i: i for i in range(nb)},
        compiler_params=pltpu.CompilerParams(has_side_effects=DATAFLOW_EFFECT),
    )(*bufs, send_sems, recv_sems, after)
    return list(outs)


def chip_exchange_copies(refs, send_sems, recv_sems):
    n = len(refs) // 2
    x, y, c, chips = _place()
    cps = []
    for a in range(n):
        for k, (cx, cy) in enumerate(chips):
            cps.append(pltpu.make_async_remote_copy(
                src_ref=refs[a].at[:, 2 * cx + cy], dst_ref=refs[n + a].at[k],
                send_sem=send_sems.at[a * 3 + k], recv_sem=recv_sems.at[a * 3 + k],
                device_id=(cx, cy, c), device_id_type=MESH_T))
    return cps


def fill_copies(refs, send_sems, recv_sems):
    x, y, c, _ = _place()
    cps = []
    for a in range(len(refs)):
        rh = refs[a].shape[-2] // 2
        cps.append(pltpu.make_async_remote_copy(
            src_ref=_half_rows(refs[a], c, rh), dst_ref=_half_rows(refs[a], c, rh),
            send_sem=send_sems.at[a], recv_sem=recv_sems.at[a], device_id=(x, y, 1 - c), device_id_type=MESH_T))
    return cps


def allgather_direct_copies(refs, send_sems, recv_sems):
    x, y, c, _ = _place()
    me = 4 * x + 2 * y + c
    cps = []
    for r in range(1, N_DEV):
        fx, fy, fc = (r >> 2) & 1, (r >> 1) & 1, r & 1
        peer = (1 - x if fx else x, 1 - y if fy else y, 1 - c if fc else c)
        cps.append(pltpu.make_async_remote_copy(
            src_ref=refs[0], dst_ref=refs[1].at[me], send_sem=send_sems.at[r - 1], recv_sem=recv_sems.at[r - 1],
            device_id=peer, device_id_type=MESH_T))
    return cps


def pair_exchange_copies(refs, send_sems, recv_sems):
    n = len(refs) // 2
    x, y, c, _ = _place()
    return [pltpu.make_async_remote_copy(
        src_ref=_half_rows(refs[a], 1 - c, refs[a].shape[-2] // 2), dst_ref=refs[n + a],
        send_sem=send_sems.at[a], recv_sem=recv_sems.at[a], device_id=(x, y, 1 - c), device_id_type=MESH_T)
        for a in range(n)]


def weight_ici_copies(refs, send_sems, recv_sems):
    x, y, c, chips = _place()
    cps = []
    for a in range(len(refs)):
        for k, (cx, cy) in enumerate(chips):
            cps.append(pltpu.make_async_remote_copy(
                src_ref=refs[a].at[2 * x + y, c], dst_ref=refs[a].at[2 * x + y, c],
                send_sem=send_sems.at[a * 3 + k], recv_sem=recv_sems.at[a * 3 + k],
                device_id=(cx, cy, c), device_id_type=MESH_T))
    return cps


def weight_d2d_copies(refs, send_sems, recv_sems):
    x, y, c, chips = _place()
    cps = []
    for a in range(len(refs)):
        for k, (cx, cy) in enumerate(chips):
            cps.append(pltpu.make_async_remote_copy(
                src_ref=refs[a].at[2 * cx + cy, c], dst_ref=refs[a].at[2 * cx + cy, c],
                send_sem=send_sems.at[a * 3 + k], recv_sem=recv_sems.at[a * 3 + k],
                device_id=(x, y, 1 - c), device_id_type=MESH_T))
    return cps


def cast_place(w, lead, chip_arr, name, dep=None):
    R, C = w.shape[-2:]
    tr = _row_tile(R, 256 if C > 1024 else 512)

    def body(chip_ref, w_ref, *rest):
        rest[-1][...] = w_ref[...].astype(BF16)

    in_specs = [pl.BlockSpec((None,) * len(lead) + (tr, C), lambda t, ch: (*lead, t, 0))]
    args = [chip_arr, w]
    if dep is not None:
        in_specs.append(ANY)
        args.append(dep)
    grid_spec = pltpu.PrefetchScalarGridSpec(
        num_scalar_prefetch=1, grid=(R // tr,), in_specs=in_specs,
        out_specs=pl.BlockSpec((None, tr, C), lambda t, ch: (ch[0], t, 0)))
    return pl.pallas_call(
        body, name=name, grid_spec=grid_spec,
        out_shape=jax.ShapeDtypeStruct((N_CHIPS, R, C), BF16),
        compiler_params=_cp(("arbitrary",)),
    )(*args)


def _s5_prepare(lam_re, lam_im, log_dt, b_re, b_im, c_re, c_im, groups_per_block):
    G, P = lam_re.shape
    N = b_re.shape[-1]
    gb = groups_per_block
    nblk = G // gb
    dt = jnp.exp(log_dt)[:, None]
    e = jnp.exp(lam_re * dt)
    a_re = e * jnp.cos(lam_im * dt)
    a_im = e * jnp.sin(lam_im * dt)
    n2 = lam_re * lam_re + lam_im * lam_im
    co_re = ((a_re - 1.0) * lam_re + a_im * lam_im) / n2
    co_im = (a_im * lam_re - (a_re - 1.0) * lam_im) / n2
    bb_re = co_re[..., None] * b_re - co_im[..., None] * b_im
    bb_im = co_re[..., None] * b_im + co_im[..., None] * b_re
    eye = jnp.eye(gb, dtype=F32)

    def blockdiag_np(m):
        m = m.reshape(nblk, gb, N, P)
        return jnp.einsum('bgnp,gh->bgnhp', m, eye).reshape(nblk, gb * N, gb * P)

    b_np_re = jnp.swapaxes(bb_re, 1, 2)
    b_np_im = jnp.swapaxes(bb_im, 1, 2)
    return (a_re.reshape(nblk, 1, gb * P), a_im.reshape(nblk, 1, gb * P),
            blockdiag_np(b_np_re), blockdiag_np(b_np_im), blockdiag_np(c_re), blockdiag_np(c_im))


def _to_scan_order(a):
    S, D = a.shape
    return a.reshape(SCAN_LANES, S // SCAN_LANES, D).transpose(1, 0, 2).reshape(S, D)


def _from_scan_order(a):
    S, D = a.shape
    return a.reshape(S // SCAN_LANES, SCAN_LANES, D).transpose(1, 0, 2).reshape(S, D)


def _pad_rows(a, mult=8):
    r = (-a.shape[0]) % mult
    if r:
        a = jnp.concatenate([a, jnp.zeros((r, a.shape[1]), a.dtype)], axis=0)
    return a


def _pack_rows(arrs, width):
    parts, offs, o = [], [], 0
    for a in arrs:
        flat = a.reshape(-1)
        r = (-flat.shape[0]) % (16 * width)
        if r:
            flat = jnp.concatenate([flat, jnp.zeros((r,), flat.dtype)])
        p = flat.reshape(-1, width)
        parts.append(p)
        offs.append((o, a.shape, a.size))
        o += p.shape[0]
    if o % 64:
        parts.append(jnp.zeros((64 - o % 64, width), parts[0].dtype))
    return jnp.concatenate(parts, axis=0), offs


def _unpack_rows(packed, offs):
    outs = []
    for o, shape, size in offs:
        rows = -(-size // packed.shape[1])
        outs.append(packed[o:o + rows].reshape(-1)[:size].reshape(shape))
    return outs


def kernel(x, c, ada_w, ada_b, norm_pre, norm_post, ffn_w_in, ffn_w_out, ab_w_in, pool_w, pool_scale, sgu_ln_g, sgu_ln_b, sgu_w, sgu_b, ab_w_out, ssm_w_in, ssm_lam_re, ssm_lam_im, ssm_b_re, ssm_b_im, ssm_c_re, ssm_c_im, ssm_d, ssm_log_dt, ssm_w_glu, loss_target, m_ada_w, m_ada_b, m_norm_pre, m_norm_post, m_ffn_w_in, m_ffn_w_out, m_ab_w_in, m_pool_w, m_pool_scale, m_sgu_ln_g, m_sgu_ln_b, m_sgu_w, m_sgu_b, m_ab_w_out, m_ssm_w_in, m_ssm_lam_re, m_ssm_lam_im, m_ssm_b_re, m_ssm_b_im, m_ssm_c_re, m_ssm_c_im, m_ssm_d, m_ssm_log_dt, m_ssm_w_glu, v_ada_w, v_ada_b, v_norm_pre, v_norm_post, v_ffn_w_in, v_ffn_w_out, v_ab_w_in, v_pool_w, v_pool_scale, v_sgu_ln_g, v_sgu_ln_b, v_sgu_w, v_sgu_b, v_ab_w_out, v_ssm_w_in, v_ssm_lam_re, v_ssm_lam_im, v_ssm_b_re, v_ssm_b_im, v_ssm_c_re, v_ssm_c_im, v_ssm_d, v_ssm_log_dt, v_ssm_w_glu):
    weights = dict(ada_w=ada_w, ada_b=ada_b, norm_pre=norm_pre, norm_post=norm_post, ffn_w_in=ffn_w_in,
                   ffn_w_out=ffn_w_out, ab_w_in=ab_w_in, pool_w=pool_w, pool_scale=pool_scale, sgu_ln_g=sgu_ln_g,
                   sgu_ln_b=sgu_ln_b, sgu_w=sgu_w, sgu_b=sgu_b, ab_w_out=ab_w_out, ssm_w_in=ssm_w_in,
                   ssm_lam_re=ssm_lam_re, ssm_lam_im=ssm_lam_im, ssm_b_re=ssm_b_re, ssm_b_im=ssm_b_im,
                   ssm_c_re=ssm_c_re, ssm_c_im=ssm_c_im, ssm_d=ssm_d, ssm_log_dt=ssm_log_dt, ssm_w_glu=ssm_w_glu)
    m_in = dict(ada_w=m_ada_w, ada_b=m_ada_b, norm_pre=m_norm_pre, norm_post=m_norm_post, ffn_w_in=m_ffn_w_in,
                ffn_w_out=m_ffn_w_out, ab_w_in=m_ab_w_in, pool_w=m_pool_w, pool_scale=m_pool_scale,
                sgu_ln_g=m_sgu_ln_g, sgu_ln_b=m_sgu_ln_b, sgu_w=m_sgu_w, sgu_b=m_sgu_b, ab_w_out=m_ab_w_out,
                ssm_w_in=m_ssm_w_in, ssm_lam_re=m_ssm_lam_re, ssm_lam_im=m_ssm_lam_im, ssm_b_re=m_ssm_b_re,
                ssm_b_im=m_ssm_b_im, ssm_c_re=m_ssm_c_re, ssm_c_im=m_ssm_c_im, ssm_d=m_ssm_d,
                ssm_log_dt=m_ssm_log_dt, ssm_w_glu=m_ssm_w_glu)
    v_in = dict(ada_w=v_ada_w, ada_b=v_ada_b, norm_pre=v_norm_pre, norm_post=v_norm_post, ffn_w_in=v_ffn_w_in,
                ffn_w_out=v_ffn_w_out, ab_w_in=v_ab_w_in, pool_w=v_pool_w, pool_scale=v_pool_scale,
                sgu_ln_g=v_sgu_ln_g, sgu_ln_b=v_sgu_ln_b, sgu_w=v_sgu_w, sgu_b=v_sgu_b, ab_w_out=v_ab_w_out,
                ssm_w_in=v_ssm_w_in, ssm_lam_re=v_ssm_lam_re, ssm_lam_im=v_ssm_lam_im, ssm_b_re=v_ssm_b_re,
                ssm_b_im=v_ssm_b_im, ssm_c_re=v_ssm_c_re, ssm_c_im=v_ssm_c_im, ssm_d=v_ssm_d,
                ssm_log_dt=v_ssm_log_dt, ssm_w_glu=v_ssm_w_glu)
    names = list(weights.keys())

    xi, yi, ci = lax.axis_index("x"), lax.axis_index("y"), lax.axis_index("c")
    chip = 2 * xi + yi
    me = 4 * xi + 2 * yi + ci
    S, D = x.shape[1], x.shape[2]
    L = ada_w.shape[0]
    NSUB = norm_pre.shape[1]
    DS = norm_pre.shape[2]
    FS = ffn_w_in.shape[-1]
    FR = ffn_w_out.shape[-2]
    x0 = x[0]
    target = loss_target[0]

    small_parts = [_pad_rows(p) for p in (c.reshape(D // DS, DS), norm_pre.reshape(L * NSUB, DS),
                                          norm_post.reshape(L * NSUB, DS), ssm_d.reshape(1, DS))]
    small_offs = [0]
    for p in small_parts:
        small_offs.append(small_offs[-1] + p.shape[0])
    small_all = allgather_small(jnp.concatenate(small_parts, axis=0), "ag_small").reshape(N_DEV, -1, DS)
    c_all = small_all[:, :D // DS].reshape(N_DEV, D)
    per_chip = small_all[0::2]
    o = small_offs[1]
    g_pre_full = jnp.moveaxis(per_chip[:, o:o + L * NSUB], 0, 1).reshape(L, NSUB, D)
    o = small_offs[2]
    g_post_full = jnp.moveaxis(per_chip[:, o:o + L * NSUB], 0, 1).reshape(L, NSUB, D)
    o = small_offs[3]
    d_full = jnp.moveaxis(per_chip[:, o:o + 1], 0, 1).reshape(1, D)

    pieces = []
    for l in range(L):
        pieces.append((f"ffn_{l}_0", [(ffn_w_in, (l, 0)), (ffn_w_out, (l, 0))]))
        if l % 2 == 0:
            pieces.append((f"mix_{l}", [(ab_w_in, (l // 2,)), (ab_w_out, (l // 2,))]))
        else:
            pieces.append((f"mix_{l}", [(ssm_w_in, (l // 2,)), (ssm_w_glu, (l // 2,))]))
        pieces.append((f"ffn_{l}_1", [(ffn_w_in, (l, 1)), (ffn_w_out, (l, 1))]))
    tags = [tag for tag, _ in pieces]
    chip_arr = chip.reshape(1).astype(jnp.int32)
    wg_started, wg_passing = {}, {}

    def cast_piece(tag, ws, dep):
        wg_started[tag] = [cast_place(w, lead, chip_arr, f"wg_cast_{tag}_{i}", dep=dep).reshape(
            N_CHIPS, 2, w.shape[-2] // 2, w.shape[-1]) for i, (w, lead) in enumerate(ws)]

    def start_piece(tag, dep):
        bufs = wg_started[tag]
        wg_started[tag] = split_start(bufs, weight_ici_copies, 3 * len(bufs), f"wg_ici_start_{tag}", dep=dep,
                                      token_shape=(8, D))
        return wg_started[tag][3]

    def weights_arrived(i, after):
        ssem, rsem, bufs, _ = wg_started[tags[i]]
        lands = split_wait(ssem, rsem, bufs, after, weight_ici_copies, f"wg_ici_wait_{tags[i]}")
        wg_passing[i] = split_start(lands, weight_d2d_copies, 3 * len(lands), f"wg_d2d_start_{tags[i]}")
        return wg_passing[i][3][0, 0]

    def weights_of(i, after):
        ssem, rsem, lands, _ = wg_passing[i]
        return split_wait(ssem, rsem, lands, after, weight_d2d_copies, f"wg_d2d_wait_{tags[i]}")

    NS = ada_w.shape[-1]
    ada_b_shard = lax.dynamic_slice_in_dim(ada_b, chip * NS, NS, axis=1).reshape(L, 1, NS)
    mod_part = ada_mod(c_all, ada_w, ada_b_shard, "ada_mod")
    mod_all = allgather_small(mod_part.reshape(L * N_DEV, NS), "ag_mod").reshape(N_DEV, L, N_DEV, NS)
    mod_mine = lax.dynamic_index_in_dim(mod_all[0::2], me, axis=2, keepdims=False)
    mod = jnp.moveaxis(mod_mine, 0, 1).reshape(L, NSUB, 3, D)
    cast_piece(*pieces[0], mod_all)
    first_token = start_piece(tags[0], mod_all)
    wg_tokens = [first_token]
    for tag, ws in pieces[1:]:
        cast_piece(tag, ws, first_token)
    for tag in tags[1:WG_AHEAD]:
        wg_tokens.append(start_piece(tag, first_token))

    def vec_of(l, s):
        return jnp.concatenate([mod[l, s], g_pre_full[l, s][None], g_post_full[l, s][None],
                                jnp.zeros((3, D), F32)], axis=0)

    GB = 8
    s5_args = (ssm_lam_re[0], ssm_lam_im[0], ssm_log_dt[0], ssm_b_re[0], ssm_b_im[0], ssm_c_re[0], ssm_c_im[0])
    (a_re, a_im, bblk_re, bblk_im, cblk_re, cblk_im), s5_vjp = jax.vjp(lambda *p: _s5_prepare(*p, GB), *s5_args)
    bre16, bim16 = bblk_re.astype(BF16), bblk_im.astype(BF16)
    cre16, cim16 = cblk_re.astype(BF16), cblk_im.astype(BF16)
    cre16_t, cim16_t = jnp.swapaxes(cre16, 1, 2), jnp.swapaxes(cim16, 1, 2)

    pool_w16 = pool_w[0].astype(BF16)
    sgu_bexp = jnp.broadcast_to(sgu_b[0][:, :, None], sgu_w[0].shape)

    saved = {}
    ffn_w = {}
    xcur = x0
    stage = [0]

    def next_weights(after, vec):
        i = stage[0]
        stage[0] += 1
        weights_arrived(i, after)
        if i + WG_AHEAD < len(tags):
            vec = vec + start_piece(tags[i + WG_AHEAD], wg_passing[i][3])
        return weights_of(i, after), vec

    for l in range(L):
        v0 = vec_of(l, 0)
        g, v0p = next_weights(xcur if l else sum(wg_tokens, v0), v0)
        ffn_w[(l, 0)] = (g[0].reshape(N_CHIPS, D, FS), g[1].reshape(N_CHIPS, FR, D))
        out, h, u, f = ffn_fwd(xcur, v0p, *ffn_w[(l, 0)], f"ffn_fwd_{l}_0")
        saved[(l, 0)] = (xcur, v0, h, u, f)
        xcur = out
        v1 = vec_of(l, 1)
        mix_g, v1p = next_weights(xcur, v1)
        if l % 2 == 0:
            abin_g = mix_g[0].reshape(N_CHIPS, D, -1)
            about_g = mix_g[1].reshape(-1, D)
            h, z = pre_matmul(xcur, v1p, abin_g, f"mixa_in_{l}")
            ycat = mixa_core_fwd(z, pool_w16, pool_scale, sgu_ln_g, sgu_ln_b, sgu_w[0], sgu_bexp, f"mixa_core_{l}")
            out, f = matmul_post(ycat, about_g, xcur, v1, f"mixa_out_{l}")
            saved[(l, 1)] = (xcur, v1, h, z, ycat, f)
        else:
            sin_g = mix_g[0].reshape(1, -1, D)
            glu_g = mix_g[1].reshape(N_CHIPS, D, -1)
            h, uu = pre_matmul(xcur, v1p, sin_g, f"s5_in_{l}")
            us = _to_scan_order(uu)
            ys = s5_core_fwd(us, a_re, a_im, bre16, bim16, cre16_t, cim16_t, d_full, f"s5_core_{l}")
            yy = _from_scan_order(ys)
            out, f = s5_glu_fwd(yy, glu_g, xcur, v1, f"s5_glu_{l}")
            saved[(l, 1)] = (xcur, v1, h, us, yy, f)
        xcur = out
        v2 = vec_of(l, 2)
        g, v2p = next_weights(xcur, v2)
        ffn_w[(l, 1)] = (g[0].reshape(N_CHIPS, D, FS), g[1].reshape(N_CHIPS, FR, D))
        out, h, u, f = ffn_fwd(xcur, v2p, *ffn_w[(l, 1)], f"ffn_fwd_{l}_1")
        saved[(l, 2)] = (xcur, v2, h, u, f)
        xcur = out

    dcur, sq = loss_head(xcur, target, "loss_head")
    loss = lax.psum(sq[0, 0], ("x", "y", "c")) * (0.5 / D)

    accs = {}
    small_g = {}
    cidx = ci.reshape(1).astype(jnp.int32)
    chip_c = jnp.stack([chip, ci]).astype(jnp.int32)
    rs_open, rs_pending = [], []

    pin = [zero_after(loss.reshape(1, 1), "after_loss", (8, D))]
    last_start = [pin[0]]

    def pin_behind(token):
        pin.append(token)
        last_start[0] = token

    def pinned(v):
        for token in pin:
            v = v + (token if token.shape == v.shape else token[0, 0])
        pin.clear()
        return v

    def rs_begin(tag, items):
        arrs = [it[0] for it in items]
        lands = [lax.empty(a.shape[:-2] + (a.shape[-2] // 2, a.shape[-1]), BF16) for a in arrs]
        started = split_start(arrs + lands, pair_exchange_copies, len(arrs), f"rs_pair_start_{tag}",
                              token_shape=(8, D))
        rs_open.append((tag, items, started))
        pin_behind(started[3])

    def rs_advance(after):
        while rs_open:
            tag, items, (ssem, rsem, bufs, _) = rs_open.pop(0)
            n = len(items)
            bufs = split_wait(ssem, rsem, bufs, after, pair_exchange_copies, f"rs_pair_wait_{tag}")
            parts = [pair_sum(a, r, cidx, f"rs_pair_sum_{tag}_{i}") for i, (a, r) in enumerate(zip(bufs[:n], bufs[n:]))]
            lands = [lax.empty((3, 1) + p.shape[2:], BF16) for p in parts]
            started = split_start(parts + lands, chip_exchange_copies, 3 * n, f"rs_chip_start_{tag}",
                                  token_shape=(8, D))
            rs_pending.append((tag, items, started))
            pin_behind(started[3])

    def ffn_back(l, s, k, dcur, before_wgrads=None):
        xin, vv, h, u, f = saved[(l, s)]
        vv = pinned(vv)
        dx, df, du, act, acc = ffn_bwd(dcur, xin, f, u, vv, *ffn_w[(l, k)], f"ffn_bwd_{l}_{k}")
        accs[(l, s)] = acc
        rs_advance(dx)
        dep = None if before_wgrads is None else before_wgrads()
        g_win = tn_matmul(h, du, 1024, f"ffn_dwin_{l}_{k}", dep=dep)
        g_wout = tn_matmul_cols(act, df, FS, D, f"ffn_dwout_{l}_{k}")
        rs_begin(f"ffn_{l}_{k}", [(g_win[None], "ffn_w_in", 2 * l + k, 2 * L),
                                  (g_wout.reshape(1, N_CHIPS, FR, D), "ffn_w_out", 2 * l + k, 2 * L)])
        return dx

    grads = {}

    small_names = ["pool_w", "pool_scale", "sgu_ln_g", "sgu_ln_b", "sgu_w", "sgu_b", "ssm_lam_re", "ssm_lam_im",
                   "ssm_b_re", "ssm_b_im", "ssm_c_re", "ssm_c_im", "ssm_log_dt"]
    small_open = []

    def small_grads_start():
        acc_all = jnp.stack([jnp.stack([accs[(l, s)] for s in range(NSUB)]) for l in range(L)])
        dmod_mine = acc_all[:, :, A_SHIFT:A_GATE + 1].reshape(L, NSUB * 3 * D)
        packed, offs = _pack_rows([dmod_mine, acc_all[:, :, A_GPRE], acc_all[:, :, A_GPOST], dd_mine]
                                  + [small_g[n] for n in small_names], D)
        packed = pinned(packed).astype(BF16)
        started = split_start([packed, lax.empty((N_DEV,) + packed.shape, BF16)], allgather_direct_copies,
                              N_DEV - 1, "ag_grads_start")
        small_open.append((started, offs))
        pin_behind(started[3])
        return started[3]

    def small_grads_finish(after):
        (ssem, rsem, bufs, _), offs = small_open.pop()
        own, land = split_wait(ssem, rsem, bufs, after, allgather_direct_copies, "ag_grads_wait")
        me_arr = me.reshape(1).astype(jnp.int32)
        summed = _unpack_rows(sum_gathered(land, own, me_arr, "sum_small"), offs)
        grads.update({n: g for n, g in zip(small_names, summed[4:])})
        grads["ada_b"] = summed[0]
        grads["norm_pre"] = lax.dynamic_slice_in_dim(summed[1], chip * DS, DS, axis=2)
        grads["norm_post"] = lax.dynamic_slice_in_dim(summed[2], chip * DS, DS, axis=2)
        grads["ssm_d"] = lax.dynamic_slice_in_dim(summed[3], chip * DS, DS, axis=1)
        nm = L * NSUB * 3
        is_me = (jnp.arange(N_DEV) == me)[:, None, None]
        dmod_all = jnp.where(is_me, own[None, :nm], land[:, :nm]).reshape(N_DEV, L, NSUB * 3 * D)
        dmod_shard = lax.dynamic_slice_in_dim(jnp.moveaxis(dmod_all, 0, 1), chip * NS, NS, axis=2)
        grads["ada_w"] = ada_grad(c_all, dmod_shard, "ada_grad")

    for l in reversed(range(L)):
        dcur = ffn_back(l, 2, 1, dcur)
        if l % 2 == 0:
            xin, vv, h, z, ycat, f = saved[(l, 1)]
            vv = pinned(vv)
            df, dact, acc_post = post_bwd_matmul(dcur, f, vv, about_g, f"mixa_out_bwd_{l}")
            rs_advance(dact)
            g_about = tn_matmul_cols(ycat, df, 512, D, f"mixa_dwout_{l}")
            dz, dpw, dvecw, dsw, dsb = mixa_core_bwd(z, dact, pool_w16, pool_scale, sgu_ln_g, sgu_ln_b, sgu_w[0],
                                                     sgu_bexp, f"mixa_core_bwd_{l}")
            g_abin = tn_matmul_cols(h, dz, 512, abin_g.shape[-1], f"mixa_dwin_{l}")
            dcur, acc_pre = matmul_pre_bwd(dz, abin_g, xin, dcur, vv, f"mixa_in_bwd_{l}")
            accs[(l, 1)] = acc_pre + acc_post
            rs_begin(f"mixa_{l}", [(g_abin[None], "ab_w_in", 0, 1),
                                   (g_about.reshape(1, N_CHIPS, -1, D), "ab_w_out", 0, 1)])
            small_g.update(pool_w=dpw[None], pool_scale=dvecw[0:1], sgu_ln_g=dvecw[1:2], sgu_ln_b=dvecw[2:3],
                           sgu_w=dsw[None], sgu_b=dsb.T[None])
        else:
            xin, vv, h, us, yy, f = saved[(l, 1)]
            vv = pinned(vv)
            dy, dab, gact, acc_post = s5_glu_bwd(dcur, f, yy, vv, glu_g, f"s5_glu_bwd_{l}")
            rs_advance(dy)
            g_glu = tn_matmul_cols(gact, dab, 512, glu_g.shape[-1], f"s5_dwglu_{l}")
            dys = _to_scan_order(dy)
            dus, dbre, dbim, dcre_t, dcim_t, dar, dai, dd = s5_core_bwd(
                us, dys, a_re, a_im, bre16, bim16, cre16, cim16, d_full, f"s5_core_bwd_{l}")
            du = _from_scan_order(dus).astype(BF16)
            g_sin = tn_matmul_cols(h, du, 512, D, f"s5_dwin_{l}")
            dcur, acc_pre = matmul_pre_bwd(du, sin_g, xin, dcur, vv, f"s5_in_bwd_{l}")
            accs[(l, 1)] = acc_pre + acc_post
            rs_begin(f"s5_{l}", [(g_sin.reshape(1, N_CHIPS, -1, D), "ssm_w_in", 0, 1), (g_glu[None], "ssm_w_glu", 0, 1)])
            s5_grads = s5_vjp((dar, dai, dbre, dbim, jnp.swapaxes(dcre_t, 1, 2), jnp.swapaxes(dcim_t, 1, 2)))
            small_g.update(ssm_lam_re=s5_grads[0][None], ssm_lam_im=s5_grads[1][None], ssm_log_dt=s5_grads[2][None],
                           ssm_b_re=s5_grads[3][None], ssm_b_im=s5_grads[4][None], ssm_c_re=s5_grads[5][None],
                           ssm_c_im=s5_grads[6][None])
            dd_mine = dd
        dcur = ffn_back(l, 0, 0, dcur, before_wgrads=small_grads_start if l == 0 else None)
    grad_x = dcur[None]
    rs_advance(last_start[0])
    small_grads_finish(last_start[0])

    big_names = ["ffn_w_in", "ffn_w_out", "ab_w_in", "ab_w_out", "ssm_w_in", "ssm_w_glu"]
    deltas, new_m, new_v = {}, {}, {}

    def update(n, dep=None):
        g = grads[n].reshape(weights[n].shape)
        outs = adamw_nd(weights[n], g, m_in[n], v_in[n], f"adamw_{n}", emit_grad=n in big_names, dep=dep)
        grads[n] = outs[0] if n in big_names else g
        deltas[n], new_m[n], new_v[n] = outs[-3:]

    fin = {}

    def chip_sums(pending, after):
        for tag, items, (ssem, rsem, bufs, _) in pending:
            bufs = split_wait(ssem, rsem, bufs, after, chip_exchange_copies, f"rs_chip_wait_{tag}")
            n = len(items)
            for i, (it, p, r) in enumerate(zip(items, bufs[:n], bufs[n:])):
                fin[it[1]] = chip_sum(p, r, chip_c, f"rs_chip_sum_{tag}_{i}", dest=fin.get(it[1]), slot=it[2],
                                      n_slots=it[3])

    chip_sums(rs_pending[:-1], pinned(grads["ada_b"]))
    update("ada_w", dep=fin["ab_w_in"])
    chip_sums(rs_pending[-1:], deltas["ada_w"])
    ssem, rsem, bufs, fill_token = split_start([fin[n] for n in big_names], fill_copies, len(big_names), "rs_fill_start")
    for n in names:
        if n not in big_names and n != "ada_w":
            update(n, dep=fill_token)
    filled = split_wait(ssem, rsem, bufs, deltas["ssm_log_dt"], fill_copies, "rs_fill_wait")
    for n, g in zip(big_names, filled):
        grads[n] = g
        update(n)

    return (loss, grad_x, *[grads[n] for n in names], *[deltas[n] for n in names],
            *[new_m[n] for n in names], *[new_v[n] for n in names])
```

```python
import math

import jax
import jax.numpy as jnp
from jax import lax
from jax.experimental import pallas as pl
from jax.experimental.pallas import tpu as pltpu

F32 = jnp.float32
BF16 = jnp.bfloat16
EPS = 1e-6
MESH_T = pl.DeviceIdType.MESH
VMEM_LIMIT_BYTES = 56 * 1024 * 1024
N_CHIPS = 4
N_DEV = 8
POOL_WINDOWS = (2, 4, 8, 16)
CHUNK = 128
SCAN_LANES = 8
SCAN_UNROLL = 4
WG_AHEAD = 2
FFN_BWD_CHUNK = 2048
ADAM_LR = 0.001
ADAM_B1 = 0.9
ADAM_B2 = 0.999
ADAM_EPS = 1e-08
ADAM_WD = 0.01
ADAM_STEP = 10
GELU_C = math.sqrt(2.0 / math.pi)
GELU_K = 0.044715

V_SHIFT, V_SCALE, V_GATE, V_GPRE, V_GPOST = 0, 1, 2, 3, 4
A_SHIFT, A_SCALE, A_GATE, A_GPRE, A_GPOST = 0, 1, 2, 3, 4

ANY = pl.BlockSpec(memory_space=pl.ANY)


def _cp(sem=None):
    if sem is None:
        return pltpu.CompilerParams(vmem_limit_bytes=VMEM_LIMIT_BYTES)
    return pltpu.CompilerParams(vmem_limit_bytes=VMEM_LIMIT_BYTES, dimension_semantics=sem)


def _dot(a, b):
    return jnp.dot(a, b, preferred_element_type=F32)


def _dot_nt(a, b):
    return lax.dot_general(a, b, (((1,), (1,)), ((), ())), preferred_element_type=F32)


def _dot_tn(a, b):
    return lax.dot_general(a, b, (((0,), (0,)), ((), ())), preferred_element_type=F32)


def _sigmoid(x):
    return 1.0 / (1.0 + jnp.exp(-x))


def _gelu(x):
    return 0.5 * x * (1.0 + jnp.tanh(GELU_C * (x + GELU_K * x * x * x)))


def _gelu_grad(x):
    t = jnp.tanh(GELU_C * (x + GELU_K * x * x * x))
    return 0.5 * (1.0 + t) + 0.5 * x * (1.0 - t * t) * GELU_C * (1.0 + 3.0 * GELU_K * x * x)


def _rowsum(v):
    return jnp.sum(v, axis=0, keepdims=True)


def _lanemean(v):
    return jnp.mean(v, axis=-1, keepdims=True)


def _row(ref, i):
    return ref[pl.ds(i, 1), :]


def _pre_fwd(x, vec_ref):
    r = lax.rsqrt(_lanemean(x * x) + EPS)
    return (x * r) * _row(vec_ref, V_GPRE) * (1.0 + _row(vec_ref, V_SCALE)) + _row(vec_ref, V_SHIFT)


def _pre_bwd(x, dh, vec_ref):
    g = _row(vec_ref, V_GPRE)
    sc = 1.0 + _row(vec_ref, V_SCALE)
    r = lax.rsqrt(_lanemean(x * x) + EPS)
    xn = x * r
    dhx = dh * xn
    t2 = dh * (g * sc)
    dx = r * (t2 - xn * _lanemean(t2 * xn))
    return dx, _rowsum(dh), _rowsum(dhx * g), _rowsum(dhx * sc)


def _post_fwd(x, f, vec_ref, rw):
    q = lax.rsqrt(_lanemean(f * f) + EPS)
    return x + (rw * _row(vec_ref, V_GATE)) * (f * q * _row(vec_ref, V_GPOST))


def _post_bwd(dout, f, vec_ref, rw):
    gp = _row(vec_ref, V_GPOST)
    gate = _row(vec_ref, V_GATE)
    q = lax.rsqrt(_lanemean(f * f) + EPS)
    fhat = f * q
    dgate = _rowsum(dout * (rw * fhat * gp))
    dy = dout * (rw * gate)
    dgpost = _rowsum(dy * fhat)
    t = dy * gp
    df = q * (t - fhat * _lanemean(t * fhat))
    return df, dgate, dgpost


def _acc_add(acc_ref, row, v):
    acc_ref[pl.ds(row, 1), :] += v


def ffn_fwd(x, vec, win_g, wout_g, name, tm=512):
    S, D = x.shape
    FS = win_g.shape[-1]
    FR = wout_g.shape[-2]

    def body(x_ref, vec_ref, win_hbm, wout_hbm, out_ref, h_ref, u_ref, f_ref, win_s, wout_s, sem):
        @pl.when(pl.program_id(0) == 0)
        def _():
            cps = [pltpu.make_async_copy(win_hbm.at[j], win_s.at[j], sem.at[j]) for j in range(N_CHIPS)]
            cps += [pltpu.make_async_copy(wout_hbm.at[j], wout_s.at[pl.ds(j * FR, FR), :], sem.at[N_CHIPS + j])
                    for j in range(N_CHIPS)]
            for cp in cps:
                cp.start()
            for cp in cps:
                cp.wait()

        xv = x_ref[...]
        h = _pre_fwd(xv, vec_ref).astype(BF16)
        h_ref[...] = h
        f = jnp.zeros((tm, D), F32)
        for j in range(2):
            a = _dot(h, win_s[j])
            b = _dot(h, win_s[2 + j])
            u_ref[j] = a.astype(BF16)
            u_ref[2 + j] = b.astype(BF16)
            act = (a * _sigmoid(a) * b).astype(BF16)
            f = f + _dot(act, wout_s[pl.ds(j * FS, FS), :])
        f_ref[...] = f
        out_ref[...] = _post_fwd(xv, f, vec_ref, 0.5)

    row = pl.BlockSpec((tm, D), lambda i: (i, 0))
    return pl.pallas_call(
        body, name=name, grid=(S // tm,),
        in_specs=[row, pl.BlockSpec((8, D), lambda i: (0, 0)), ANY, ANY],
        out_specs=[row, row, pl.BlockSpec((N_CHIPS, tm, FS), lambda i: (0, i, 0)), row],
        out_shape=[jax.ShapeDtypeStruct((S, D), F32), jax.ShapeDtypeStruct((S, D), BF16),
                   jax.ShapeDtypeStruct((N_CHIPS, S, FS), BF16), jax.ShapeDtypeStruct((S, D), F32)],
        scratch_shapes=[pltpu.VMEM((N_CHIPS, D, FS), BF16), pltpu.VMEM((N_CHIPS * FR, D), BF16),
                        pltpu.SemaphoreType.DMA((2 * N_CHIPS,))],
        compiler_params=_cp(("arbitrary",)),
    )(x, vec, win_g, wout_g)


def ffn_bwd(dout, x, f, u, vec, win_g, wout_g, name, tm=256):
    S, D = x.shape
    FS = win_g.shape[-1]
    FR = wout_g.shape[-2]

    def body(dout_ref, x_ref, f_ref, u_ref, vec_ref, win_hbm, wout_hbm,
             dx_ref, df_ref, du_ref, act_ref, acc_ref, win_s, wout_s, sem):
        @pl.when(pl.program_id(0) == 0)
        def _():
            cps = [pltpu.make_async_copy(win_hbm.at[j], win_s.at[j], sem.at[j]) for j in range(N_CHIPS)]
            cps += [pltpu.make_async_copy(wout_hbm.at[j], wout_s.at[pl.ds(j * FR, FR), :], sem.at[N_CHIPS + j])
                    for j in range(N_CHIPS)]
            for cp in cps:
                cp.start()
            acc_ref[...] = jnp.zeros_like(acc_ref)
            for cp in cps:
                cp.wait()

        dout_v = dout_ref[...]
        df, dgate, dgpost = _post_bwd(dout_v, f_ref[...], vec_ref, 0.5)
        dfb = df.astype(BF16)
        df_ref[...] = dfb
        dh = jnp.zeros((tm, D), F32)
        chunks = [(c0, min(FFN_BWD_CHUNK, FS - c0)) for c0 in range(0, FS, FFN_BWD_CHUNK)]
        for j in range(2):
            for c0, cw in chunks:
                a = u_ref[j, :, c0:c0 + cw].astype(F32)
                b = u_ref[2 + j, :, c0:c0 + cw].astype(F32)
                sig = _sigmoid(a)
                sl = a * sig
                dact = _dot_nt(dfb, wout_s[pl.ds(j * FS + c0, cw), :])
                da = (dact * b * (sig * (1.0 + a * (1.0 - sig)))).astype(BF16)
                db = (dact * sl).astype(BF16)
                du_ref[j, :, c0:c0 + cw] = da
                du_ref[2 + j, :, c0:c0 + cw] = db
                act_ref[:, j * FS + c0:j * FS + c0 + cw] = (sl * b).astype(BF16)
                dh = dh + _dot_nt(da, win_s[j, :, c0:c0 + cw]) + _dot_nt(db, win_s[2 + j, :, c0:c0 + cw])
        dx, dshift, dscale, dgpre = _pre_bwd(x_ref[...], dh, vec_ref)
        dx_ref[...] = dout_v + dx
        _acc_add(acc_ref, A_SHIFT, dshift)
        _acc_add(acc_ref, A_SCALE, dscale)
        _acc_add(acc_ref, A_GATE, dgate)
        _acc_add(acc_ref, A_GPRE, dgpre)
        _acc_add(acc_ref, A_GPOST, dgpost)

    row = pl.BlockSpec((tm, D), lambda i: (i, 0))
    ublk = pl.BlockSpec((N_CHIPS, tm, FS), lambda i: (0, i, 0))
    const = pl.BlockSpec((8, D), lambda i: (0, 0))
    return pl.pallas_call(
        body, name=name, grid=(S // tm,),
        in_specs=[row, row, row, ublk, const, ANY, ANY],
        out_specs=[row, row, ublk, pl.BlockSpec((tm, 2 * FS), lambda i: (i, 0)), const],
        out_shape=[jax.ShapeDtypeStruct((S, D), F32), jax.ShapeDtypeStruct((S, D), BF16),
                   jax.ShapeDtypeStruct((N_CHIPS, S, FS), BF16), jax.ShapeDtypeStruct((S, 2 * FS), BF16),
                   jax.ShapeDtypeStruct((8, D), F32)],
        scratch_shapes=[pltpu.VMEM((N_CHIPS, D, FS), BF16), pltpu.VMEM((N_CHIPS * FR, D), BF16),
                        pltpu.SemaphoreType.DMA((2 * N_CHIPS,))],
        compiler_params=_cp(("arbitrary",)),
    )(dout, x, f, u, vec, win_g, wout_g)


def tn_matmul(a, b, tk, name, dep=None):
    S, K = a.shape
    nb, _, tn = b.shape
    extra = [] if dep is None else [dep]

    def body(a_ref, b_ref, *rest):
        rest[-1][...] = _dot_tn(a_ref[...], b_ref[...]).astype(BF16)

    return pl.pallas_call(
        body, name=name, grid=(K // tk, nb),
        in_specs=[pl.BlockSpec((S, tk), lambda i, j: (0, i)), pl.BlockSpec((None, S, tn), lambda i, j: (j, 0, 0))]
        + [ANY] * len(extra),
        out_specs=pl.BlockSpec((None, tk, tn), lambda i, j: (j, i, 0)),
        out_shape=jax.ShapeDtypeStruct((nb, K, tn), BF16),
        compiler_params=_cp(("arbitrary", "arbitrary")),
    )(a, b, *extra)


def tn_matmul_cols(a, b, tk, tn, name):
    S, K = a.shape
    N = b.shape[1]

    def body(a_ref, b_ref, o_ref):
        o_ref[...] = _dot_tn(a_ref[...], b_ref[...]).astype(BF16)

    return pl.pallas_call(
        body, name=name, grid=(K // tk, N // tn),
        in_specs=[pl.BlockSpec((S, tk), lambda i, j: (0, i)), pl.BlockSpec((S, tn), lambda i, j: (0, j))],
        out_specs=pl.BlockSpec((None, tk, tn), lambda i, j: (j, i, 0)),
        out_shape=jax.ShapeDtypeStruct((N // tn, K, tn), BF16),
        compiler_params=_cp(("arbitrary", "arbitrary")),
    )(a, b)


def pre_matmul(x, vec, w3, name, tm=256):
    S, D = x.shape
    nj, _, Nj = w3.shape

    def body(x_ref, vec_ref, w_ref, h_ref, z_ref):
        h = _pre_fwd(x_ref[...], vec_ref).astype(BF16)
        h_ref[...] = h
        for j in range(nj):
            z_ref[:, j * Nj:(j + 1) * Nj] = _dot(h, w_ref[j])

    row = pl.BlockSpec((tm, D), lambda i: (i, 0))
    return pl.pallas_call(
        body, name=name, grid=(S // tm,),
        in_specs=[row, pl.BlockSpec((8, D), lambda i: (0, 0)), pl.BlockSpec((nj, D, Nj), lambda i: (0, 0, 0))],
        out_specs=[row, pl.BlockSpec((tm, nj * Nj), lambda i: (i, 0))],
        out_shape=[jax.ShapeDtypeStruct((S, D), BF16), jax.ShapeDtypeStruct((S, nj * Nj), F32)],
        compiler_params=_cp(("arbitrary",)),
    )(x, vec, w3)


def matmul_pre_bwd(dz, w3, x, dres, vec, name, tm=256):
    S, D = x.shape
    nj, _, Nj = w3.shape

    def body(dz_ref, w_ref, x_ref, dres_ref, vec_ref, dx_ref, acc_ref):
        @pl.when(pl.program_id(0) == 0)
        def _():
            acc_ref[...] = jnp.zeros_like(acc_ref)

        dh = jnp.zeros((tm, D), F32)
        for j in range(nj):
            dh = dh + _dot_nt(dz_ref[:, j * Nj:(j + 1) * Nj], w_ref[j])
        dx, dshift, dscale, dgpre = _pre_bwd(x_ref[...], dh, vec_ref)
        dx_ref[...] = dres_ref[...] + dx
        _acc_add(acc_ref, A_SHIFT, dshift)
        _acc_add(acc_ref, A_SCALE, dscale)
        _acc_add(acc_ref, A_GPRE, dgpre)

    row = pl.BlockSpec((tm, D), lambda i: (i, 0))
    const = pl.BlockSpec((8, D), lambda i: (0, 0))
    return pl.pallas_call(
        body, name=name, grid=(S // tm,),
        in_specs=[pl.BlockSpec((tm, nj * Nj), lambda i: (i, 0)), pl.BlockSpec((nj, D, Nj), lambda i: (0, 0, 0)),
                  row, row, const],
        out_specs=[row, const],
        out_shape=[jax.ShapeDtypeStruct((S, D), F32), jax.ShapeDtypeStruct((8, D), F32)],
        compiler_params=_cp(("arbitrary",)),
    )(dz, w3, x, dres, vec)


def matmul_post(act, w, x, vec, name, tm=256):
    S, D = x.shape
    K = act.shape[1]

    def body(act_ref, w_ref, x_ref, vec_ref, out_ref, f_ref):
        f = _dot(act_ref[...], w_ref[...])
        f_ref[...] = f
        out_ref[...] = _post_fwd(x_ref[...], f, vec_ref, 1.0)

    row = pl.BlockSpec((tm, D), lambda i: (i, 0))
    return pl.pallas_call(
        body, name=name, grid=(S // tm,),
        in_specs=[pl.BlockSpec((tm, K), lambda i: (i, 0)), pl.BlockSpec((K, D), lambda i: (0, 0)), row,
                  pl.BlockSpec((8, D), lambda i: (0, 0))],
        out_specs=[row, row],
        out_shape=[jax.ShapeDtypeStruct((S, D), F32), jax.ShapeDtypeStruct((S, D), F32)],
        compiler_params=_cp(("arbitrary",)),
    )(act, w, x, vec)


def post_bwd_matmul(dout, f, vec, w, name, tm=256):
    S, D = dout.shape
    K = w.shape[0]

    def body(dout_ref, f_ref, vec_ref, w_ref, df_ref, dact_ref, acc_ref):
        @pl.when(pl.program_id(0) == 0)
        def _():
            acc_ref[...] = jnp.zeros_like(acc_ref)

        df, dgate, dgpost = _post_bwd(dout_ref[...], f_ref[...], vec_ref, 1.0)
        dfb = df.astype(BF16)
        df_ref[...] = dfb
        dact_ref[...] = _dot_nt(dfb, w_ref[...])
        _acc_add(acc_ref, A_GATE, dgate)
        _acc_add(acc_ref, A_GPOST, dgpost)

    row = pl.BlockSpec((tm, D), lambda i: (i, 0))
    const = pl.BlockSpec((8, D), lambda i: (0, 0))
    return pl.pallas_call(
        body, name=name, grid=(S // tm,),
        in_specs=[row, row, const, pl.BlockSpec((K, D), lambda i: (0, 0))],
        out_specs=[row, pl.BlockSpec((tm, K), lambda i: (i, 0)), const],
        out_shape=[jax.ShapeDtypeStruct((S, D), BF16), jax.ShapeDtypeStruct((S, K), F32),
                   jax.ShapeDtypeStruct((8, D), F32)],
        compiler_params=_cp(("arbitrary",)),
    )(dout, f, vec, w)


def _band(w, transposed, prev):
    r = lax.broadcasted_iota(jnp.int32, (CHUNK, CHUNK), 1 if transposed else 0)
    c = lax.broadcasted_iota(jnp.int32, (CHUNK, CHUNK), 0 if transposed else 1)
    d = r - c
    m = (d + CHUNK < w) if prev else ((d >= 0) & (d < w))
    return jnp.where(m, 1.0, 0.0).astype(BF16)


def _split_hi_lo(a):
    hi = a.astype(BF16)
    lo = (a - hi.astype(F32)).astype(BF16)
    return hi, lo


def _pool_diff(a, ap, g, denom):
    w = POOL_WINDOWS[g]
    a_hi, a_lo = _split_hi_lo(a)
    p_hi, p_lo = _split_hi_lo(ap)
    mc = _band(w, False, False)
    mp = _band(w, False, True)
    win = _dot(mc, a_hi) + _dot(mc, a_lo) + _dot(mp, p_hi) + _dot(mp, p_lo)
    return win / denom - a


def _sgu_norm(v, lg, lb):
    mu = _lanemean(v)
    xc = v - mu
    rstd = lax.rsqrt(_lanemean(xc * xc) + EPS)
    xhat = xc * rstd
    return xhat, rstd, xhat * lg + lb


def _tril_mask():
    r = lax.broadcasted_iota(jnp.int32, (CHUNK, CHUNK), 0)
    c = lax.broadcasted_iota(jnp.int32, (CHUNK, CHUNK), 1)
    return r >= c


def _positions(i, w):
    r = lax.broadcasted_iota(jnp.int32, (CHUNK, 128), 0)
    pos = (i * CHUNK + r + 1).astype(F32)
    return jnp.minimum(pos, float(w))


def mixa_core_fwd(z, pool_w, pool_scale, ln_g, ln_b, sgu_w, sgu_bexp, name):
    S = z.shape[0]
    W = z.shape[1] // 3
    G = len(POOL_WINDOWS)
    GD = W // G

    def body(zc_ref, zp_ref, pw_ref, ps_ref, lg_ref, lb_ref, sw_ref, sb_ref, y_ref):
        i = pl.program_id(0)
        has_prev = jnp.where(i > 0, 1.0, 0.0)
        for g in range(G):
            sl = slice(g * GD, (g + 1) * GD)
            a = zc_ref[:, sl]
            ap = zp_ref[:, sl] * has_prev
            d = _pool_diff(a, ap, g, _positions(i, POOL_WINDOWS[g])).astype(BF16)
            y_ref[:, sl] = (_dot(d, pw_ref[g]) * ps_ref[:, sl]).astype(BF16)
        tril = _tril_mask()
        for hh in range(G):
            u = _gelu(zc_ref[:, W + hh * GD:W + (hh + 1) * GD])
            v = _gelu(zc_ref[:, 2 * W + hh * GD:2 * W + (hh + 1) * GD])
            sl = slice(hh * GD, (hh + 1) * GD)
            _, _, vn = _sgu_norm(v, lg_ref[:, sl], lb_ref[:, sl])
            wm = jnp.where(tril, sw_ref[hh], 0.0).astype(BF16)
            s = _dot(wm, vn.astype(BF16)) + sb_ref[hh]
            y_ref[:, W + hh * GD:W + (hh + 1) * GD] = (u * s).astype(BF16)

    vecw = pl.BlockSpec((1, W), lambda i: (0, 0))
    mats = pl.BlockSpec((G, GD, GD), lambda i: (0, 0, 0))
    return pl.pallas_call(
        body, name=name, grid=(S // CHUNK,),
        in_specs=[pl.BlockSpec((CHUNK, 3 * W), lambda i: (i, 0)),
                  pl.BlockSpec((CHUNK, W), lambda i: (jnp.maximum(i - 1, 0), 0)),
                  mats, vecw, vecw, vecw, mats, mats],
        out_specs=pl.BlockSpec((CHUNK, 2 * W), lambda i: (i, 0)),
        out_shape=jax.ShapeDtypeStruct((S, 2 * W), BF16),
        compiler_params=_cp(("arbitrary",)),
    )(z, z, pool_w, pool_scale, ln_g, ln_b, sgu_w, sgu_bexp)


def mixa_core_bwd(z, dy, pool_w, pool_scale, ln_g, ln_b, sgu_w, sgu_bexp, name):
    S = z.shape[0]
    W = z.shape[1] // 3
    G = len(POOL_WINDOWS)
    GD = W // G
    n_tiles = S // CHUNK

    def body(zc_ref, zp_ref, dyc_ref, dyn_ref, pw_ref, ps_ref, lg_ref, lb_ref, sw_ref, sb_ref,
             dz_ref, dpw_ref, dvec_ref, dsw_ref, dsb_ref):
        i = pl.program_id(0)

        @pl.when(i == 0)
        def _():
            dpw_ref[...] = jnp.zeros_like(dpw_ref)
            dvec_ref[...] = jnp.zeros_like(dvec_ref)
            dsw_ref[...] = jnp.zeros_like(dsw_ref)
            dsb_ref[...] = jnp.zeros_like(dsb_ref)

        has_prev = jnp.where(i > 0, 1.0, 0.0)
        has_next = jnp.where(i < n_tiles - 1, 1.0, 0.0)
        for g in range(G):
            w = POOL_WINDOWS[g]
            sl = slice(g * GD, (g + 1) * GD)
            a = zc_ref[:, sl]
            ap = zp_ref[:, sl] * has_prev
            den_c = _positions(i, w)
            den_n = _positions(i + 1, w)
            d = _pool_diff(a, ap, g, den_c).astype(BF16)
            ps = ps_ref[:, sl]
            pw = pw_ref[g]
            dyc = dyc_ref[:, sl]
            dvec_ref[pl.ds(0, 1), sl] += _rowsum(dyc * _dot(d, pw))
            dyp_c = (dyc * ps).astype(BF16)
            dyp_n = (dyn_ref[:, sl] * (ps * has_next)).astype(BF16)
            dpw_ref[g] += _dot_tn(d, dyp_c)
            dd_c = _dot_nt(dyp_c, pw)
            dd_n = _dot_nt(dyp_n, pw)
            da = (_dot(_band(w, True, False), (dd_c / den_c).astype(BF16))
                  + _dot(_band(w, True, True), (dd_n / den_n).astype(BF16)) - dd_c)
            dz_ref[:, sl] = da.astype(BF16)
        tril = _tril_mask()
        for hh in range(G):
            sl = slice(hh * GD, (hh + 1) * GD)
            zu = zc_ref[:, W + hh * GD:W + (hh + 1) * GD]
            zv = zc_ref[:, 2 * W + hh * GD:2 * W + (hh + 1) * GD]
            u = _gelu(zu)
            v = _gelu(zv)
            lg = lg_ref[:, sl]
            xhat, rstd, vn = _sgu_norm(v, lg, lb_ref[:, sl])
            vnb = vn.astype(BF16)
            wm = jnp.where(tril, sw_ref[hh], 0.0).astype(BF16)
            s = _dot(wm, vnb) + sb_ref[hh]
            dyb = dyc_ref[:, W + hh * GD:W + (hh + 1) * GD]
            du = dyb * s
            ds = dyb * u
            dsb_ref[:, hh:hh + 1] += jnp.sum(ds, axis=1, keepdims=True)
            dsb16 = ds.astype(BF16)
            dsw_ref[hh] += jnp.where(tril, _dot_nt(dsb16, vnb), 0.0)
            dvn = _dot_tn(wm, dsb16)
            dvec_ref[pl.ds(1, 1), sl] += _rowsum(dvn * xhat)
            dvec_ref[pl.ds(2, 1), sl] += _rowsum(dvn)
            dxh = dvn * lg
            dv = rstd * (dxh - _lanemean(dxh) - xhat * _lanemean(dxh * xhat))
            dz_ref[:, W + hh * GD:W + (hh + 1) * GD] = (du * _gelu_grad(zu)).astype(BF16)
            dz_ref[:, 2 * W + hh * GD:2 * W + (hh + 1) * GD] = (dv * _gelu_grad(zv)).astype(BF16)

    vecw = pl.BlockSpec((1, W), lambda i: (0, 0))
    mats = pl.BlockSpec((G, GD, GD), lambda i: (0, 0, 0))
    return pl.pallas_call(
        body, name=name, grid=(n_tiles,),
        in_specs=[pl.BlockSpec((CHUNK, 3 * W), lambda i: (i, 0)),
                  pl.BlockSpec((CHUNK, W), lambda i: (jnp.maximum(i - 1, 0), 0)),
                  pl.BlockSpec((CHUNK, 2 * W), lambda i: (i, 0)),
                  pl.BlockSpec((CHUNK, W), lambda i: (jnp.minimum(i + 1, n_tiles - 1), 0)),
                  mats, vecw, vecw, vecw, mats, mats],
        out_specs=[pl.BlockSpec((CHUNK, 3 * W), lambda i: (i, 0)), mats,
                   pl.BlockSpec((8, W), lambda i: (0, 0)), mats, pl.BlockSpec((CHUNK, G), lambda i: (0, 0))],
        out_shape=[jax.ShapeDtypeStruct((S, 3 * W), BF16), jax.ShapeDtypeStruct((G, GD, GD), F32),
                   jax.ShapeDtypeStruct((8, W), F32), jax.ShapeDtypeStruct((G, CHUNK, CHUNK), F32),
                   jax.ShapeDtypeStruct((CHUNK, G), F32)],
        compiler_params=_cp(("arbitrary",)),
    )(z, z, dy, dy, pool_w, pool_scale, ln_g, ln_b, sgu_w, sgu_bexp)


def _cmul(ar, ai, br, bi):
    return ar * br - ai * bi, ar * bi + ai * br


def _cpow(ar, ai, n):
    rr, ri = None, None
    br, bi = ar, ai
    while n:
        if n & 1:
            rr, ri = (br, bi) if rr is None else _cmul(rr, ri, br, bi)
        n >>= 1
        if n:
            br, bi = _cmul(br, bi, br, bi)
    return rr, ri


def _seg_rows(k):
    return pl.ds(pl.multiple_of(k * SCAN_LANES, SCAN_LANES), SCAN_LANES)


def _scan_fwd(xre, xim, carry, ar, ai, K):
    P = xre.shape[1]
    a8r = jnp.broadcast_to(ar, (SCAN_LANES, P))
    a8i = jnp.broadcast_to(ai, (SCAN_LANES, P))

    def local(k, c):
        pr, pi = c
        rows = _seg_rows(k)
        nr = a8r * pr - a8i * pi + xre[rows, :]
        ni = a8r * pi + a8i * pr + xim[rows, :]
        xre[rows, :] = nr
        xim[rows, :] = ni
        return nr, ni

    er, ei = lax.fori_loop(1, K, local, (xre[pl.ds(0, SCAN_LANES), :], xim[pl.ds(0, SCAN_LANES), :]),
                           unroll=SCAN_UNROLL)
    akr, aki = _cpow(ar, ai, K)
    cr = jnp.zeros((1, P), F32)
    ci = jnp.zeros((1, P), F32)
    carry[pl.ds(0, 1), :] = cr
    carry[pl.ds(SCAN_LANES, 1), :] = ci
    for j in range(1, SCAN_LANES):
        tr, ti = _cmul(akr, aki, cr, ci)
        cr = er[j - 1:j, :] + tr
        ci = ei[j - 1:j, :] + ti
        carry[pl.ds(j, 1), :] = cr
        carry[pl.ds(SCAN_LANES + j, 1), :] = ci
    cmr = carry[pl.ds(0, SCAN_LANES), :]
    cmi = carry[pl.ds(SCAN_LANES, SCAN_LANES), :]

    def fix(k, c):
        pr, pi = c
        rows = _seg_rows(k)
        tr, ti = _cmul(pr, pi, cmr, cmi)
        xre[rows, :] += tr
        xim[rows, :] += ti
        return _cmul(pr, pi, a8r, a8i)

    lax.fori_loop(0, K, fix, (a8r, a8i), unroll=SCAN_UNROLL)


def s5_core_fwd(u, ar, ai, bre, bim, cre_t, cim_t, dskip, name):
    S, D = u.shape
    nblk, UB, PB = bre.shape
    K = S // SCAN_LANES

    def body(u_ref, ar_ref, ai_ref, bre_ref, bim_ref, cre_ref, cim_ref, d_ref, y_ref, xre, xim, carry):
        uv = u_ref[...]
        ub = uv.astype(BF16)
        xre[...] = _dot(ub, bre_ref[...])
        xim[...] = _dot(ub, bim_ref[...])
        _scan_fwd(xre, xim, carry, ar_ref[...], ai_ref[...], K)
        y_ref[...] = (_dot(xre[...].astype(BF16), cre_ref[...]) - _dot(xim[...].astype(BF16), cim_ref[...])
                      + d_ref[...] * uv)

    ucol = pl.BlockSpec((S, UB), lambda i: (0, i))
    pvec = pl.BlockSpec((None, 1, PB), lambda i: (i, 0, 0))
    bmat = pl.BlockSpec((None, UB, PB), lambda i: (i, 0, 0))
    cmat = pl.BlockSpec((None, PB, UB), lambda i: (i, 0, 0))
    return pl.pallas_call(
        body, name=name, grid=(nblk,),
        in_specs=[ucol, pvec, pvec, bmat, bmat, cmat, cmat, pl.BlockSpec((1, UB), lambda i: (0, i))],
        out_specs=ucol,
        out_shape=jax.ShapeDtypeStruct((S, D), F32),
        scratch_shapes=[pltpu.VMEM((S, PB), F32), pltpu.VMEM((S, PB), F32), pltpu.VMEM((2 * SCAN_LANES, PB), F32)],
        compiler_params=_cp(("arbitrary",)),
    )(u, ar, ai, bre, bim, cre_t, cim_t, dskip)


def s5_core_bwd(u, dy, ar, ai, bre, bim, cre, cim, dskip, name):
    S, D = u.shape
    nblk, UB, PB = bre.shape
    K = S // SCAN_LANES

    def body(u_ref, dy_ref, ar_ref, ai_ref, bre_ref, bim_ref, cre_ref, cim_ref, d_ref,
             du_ref, dbre_ref, dbim_ref, dcre_ref, dcim_ref, dar_ref, dai_ref, dd_ref,
             xre, xim, gre, gim, carry, carry_b):
        ar = ar_ref[...]
        ai = ai_ref[...]
        uv = u_ref[...]
        ub = uv.astype(BF16)
        dyv = dy_ref[...]
        dyb = dyv.astype(BF16)
        xre[...] = _dot(ub, bre_ref[...])
        xim[...] = _dot(ub, bim_ref[...])
        _scan_fwd(xre, xim, carry, ar, ai, K)
        dcre_ref[...] = _dot_tn(xre[...].astype(BF16), dyb)
        dcim_ref[...] = -_dot_tn(xim[...].astype(BF16), dyb)
        gre[...] = _dot(dyb, cre_ref[...])
        gim[...] = -_dot(dyb, cim_ref[...])

        a8r = jnp.broadcast_to(ar, (SCAN_LANES, PB))
        a8i = jnp.broadcast_to(ai, (SCAN_LANES, PB))
        na8i = -a8i

        def local(s, c):
            k = K - 2 - s
            nr, ni = c
            rows = _seg_rows(k)
            tr = gre[rows, :] + a8r * nr + a8i * ni
            ti = gim[rows, :] + a8r * ni - a8i * nr
            gre[rows, :] = tr
            gim[rows, :] = ti
            return tr, ti

        last = _seg_rows(K - 1)
        fr, fi = lax.fori_loop(0, K - 1, local, (gre[last, :], gim[last, :]), unroll=SCAN_UNROLL)
        akr, aki = _cpow(ar, -ai, K)
        cr = jnp.zeros((1, PB), F32)
        ci = jnp.zeros((1, PB), F32)
        carry_b[pl.ds(SCAN_LANES - 1, 1), :] = cr
        carry_b[pl.ds(2 * SCAN_LANES - 1, 1), :] = ci
        for j in range(SCAN_LANES - 2, -1, -1):
            tr, ti = _cmul(akr, aki, cr, ci)
            cr = fr[j + 1:j + 2, :] + tr
            ci = fi[j + 1:j + 2, :] + ti
            carry_b[pl.ds(j, 1), :] = cr
            carry_b[pl.ds(SCAN_LANES + j, 1), :] = ci
        cbr = carry_b[pl.ds(0, SCAN_LANES), :]
        cbi = carry_b[pl.ds(SCAN_LANES, SCAN_LANES), :]

        def fix_rows(rows, c, xr, xi):
            pr, pi, dar, dai = c
            tr, ti = _cmul(pr, pi, cbr, cbi)
            g_r = gre[rows, :] + tr
            g_i = gim[rows, :] + ti
            gre[rows, :] = g_r
            gim[rows, :] = g_i
            dar = dar + g_r * xr + g_i * xi
            dai = dai + g_i * xr - g_r * xi
            nr, ni = _cmul(pr, pi, a8r, na8i)
            return nr, ni, dar, dai

        def fix(s, c):
            k = K - 1 - s
            prev = _seg_rows(k - 1)
            return fix_rows(_seg_rows(k), c, xre[prev, :], xim[prev, :])

        z8 = jnp.zeros((SCAN_LANES, PB), F32)
        c = lax.fori_loop(0, K - 1, fix, (a8r, na8i, z8, z8), unroll=SCAN_UNROLL)
        _, _, dar, dai = fix_rows(_seg_rows(0), c, carry[pl.ds(0, SCAN_LANES), :], carry[pl.ds(SCAN_LANES, SCAN_LANES), :])
        dar_ref[...] = _rowsum(dar)
        dai_ref[...] = _rowsum(dai)
        grb = gre[...].astype(BF16)
        gib = gim[...].astype(BF16)
        dbre_ref[...] = _dot_tn(ub, grb)
        dbim_ref[...] = _dot_tn(ub, gib)
        du_ref[...] = _dot_nt(grb, bre_ref[...]) + _dot_nt(gib, bim_ref[...]) + d_ref[...] * dyv
        dd_ref[...] = _rowsum(dyv * uv)

    ucol = pl.BlockSpec((S, UB), lambda i: (0, i))
    pvec = pl.BlockSpec((None, 1, PB), lambda i: (i, 0, 0))
    bmat = pl.BlockSpec((None, UB, PB), lambda i: (i, 0, 0))
    cmat = pl.BlockSpec((None, PB, UB), lambda i: (i, 0, 0))
    dvec = pl.BlockSpec((1, UB), lambda i: (0, i))
    return pl.pallas_call(
        body, name=name, grid=(nblk,),
        in_specs=[ucol, ucol, pvec, pvec, bmat, bmat, bmat, bmat, dvec],
        out_specs=[ucol, bmat, bmat, cmat, cmat, pvec, pvec, dvec],
        out_shape=[jax.ShapeDtypeStruct((S, D), F32),
                   jax.ShapeDtypeStruct((nblk, UB, PB), F32), jax.ShapeDtypeStruct((nblk, UB, PB), F32),
                   jax.ShapeDtypeStruct((nblk, PB, UB), F32), jax.ShapeDtypeStruct((nblk, PB, UB), F32),
                   jax.ShapeDtypeStruct((nblk, 1, PB), F32), jax.ShapeDtypeStruct((nblk, 1, PB), F32),
                   jax.ShapeDtypeStruct((1, D), F32)],
        scratch_shapes=[pltpu.VMEM((S, PB), F32), pltpu.VMEM((S, PB), F32), pltpu.VMEM((S, PB), F32),
                        pltpu.VMEM((S, PB), F32), pltpu.VMEM((2 * SCAN_LANES, PB), F32),
                        pltpu.VMEM((2 * SCAN_LANES, PB), F32)],
        compiler_params=_cp(("arbitrary",)),
    )(u, dy, ar, ai, bre, bim, cre, cim, dskip)


def s5_glu_fwd(y, wglu, x, vec, name, tm=256):
    S, D = x.shape
    NJ = wglu.shape[-1]

    def body(y_ref, w_ref, x_ref, vec_ref, out_ref, f_ref):
        g = _gelu(y_ref[...]).astype(BF16)
        f = jnp.concatenate([_dot(g, w_ref[j]) * _sigmoid(_dot(g, w_ref[2 + j])) for j in range(2)], axis=1)
        f_ref[...] = f
        out_ref[...] = _post_fwd(x_ref[...], f, vec_ref, 1.0)

    row = pl.BlockSpec((tm, D), lambda i: (i, 0))
    return pl.pallas_call(
        body, name=name, grid=(S // tm,),
        in_specs=[row, pl.BlockSpec((N_CHIPS, D, NJ), lambda i: (0, 0, 0)), row, pl.BlockSpec((8, D), lambda i: (0, 0))],
        out_specs=[row, row],
        out_shape=[jax.ShapeDtypeStruct((S, D), F32), jax.ShapeDtypeStruct((S, D), F32)],
        compiler_params=_cp(("arbitrary",)),
    )(y, wglu, x, vec)


def s5_glu_bwd(dout, f, y, vec, wglu, name, tm=256):
    S, D = dout.shape
    NJ = wglu.shape[-1]

    def body(dout_ref, f_ref, y_ref, vec_ref, w_ref, dy_ref, dab_ref, g_ref, acc_ref):
        @pl.when(pl.program_id(0) == 0)
        def _():
            acc_ref[...] = jnp.zeros_like(acc_ref)

        df, dgate, dgpost = _post_bwd(dout_ref[...], f_ref[...], vec_ref, 1.0)
        yv = y_ref[...]
        g = _gelu(yv).astype(BF16)
        g_ref[...] = g
        dg = jnp.zeros((tm, D), F32)
        for j in range(2):
            a = _dot(g, w_ref[j])
            sig = _sigmoid(_dot(g, w_ref[2 + j]))
            dfj = df[:, j * NJ:(j + 1) * NJ]
            da = (dfj * sig).astype(BF16)
            db = (dfj * a * sig * (1.0 - sig)).astype(BF16)
            dab_ref[:, j * NJ:(j + 1) * NJ] = da
            dab_ref[:, (2 + j) * NJ:(3 + j) * NJ] = db
            dg = dg + _dot_nt(da, w_ref[j]) + _dot_nt(db, w_ref[2 + j])
        dy_ref[...] = dg * _gelu_grad(yv)
        _acc_add(acc_ref, A_GATE, dgate)
        _acc_add(acc_ref, A_GPOST, dgpost)

    row = pl.BlockSpec((tm, D), lambda i: (i, 0))
    const = pl.BlockSpec((8, D), lambda i: (0, 0))
    return pl.pallas_call(
        body, name=name, grid=(S // tm,),
        in_specs=[row, row, row, const, pl.BlockSpec((N_CHIPS, D, NJ), lambda i: (0, 0, 0))],
        out_specs=[row, pl.BlockSpec((tm, N_CHIPS * NJ), lambda i: (i, 0)), row, const],
        out_shape=[jax.ShapeDtypeStruct((S, D), F32), jax.ShapeDtypeStruct((S, N_CHIPS * NJ), BF16),
                   jax.ShapeDtypeStruct((S, D), BF16), jax.ShapeDtypeStruct((8, D), F32)],
        compiler_params=_cp(("arbitrary",)),
    )(dout, f, y, vec, wglu)


def loss_head(y, target, name, tm=256):
    S, D = y.shape

    def body(y_ref, t_ref, d_ref, l_ref):
        @pl.when(pl.program_id(0) == 0)
        def _():
            l_ref[...] = jnp.zeros_like(l_ref)

        err = y_ref[...] - t_ref[...]
        d_ref[...] = err * (1.0 / D)
        l_ref[...] += jnp.sum(_rowsum(err * err), axis=1, keepdims=True)

    row = pl.BlockSpec((tm, D), lambda i: (i, 0))
    return pl.pallas_call(
        body, name=name, grid=(S // tm,),
        in_specs=[row, row],
        out_specs=[row, pl.BlockSpec((1, 1), lambda i: (0, 0))],
        out_shape=[jax.ShapeDtypeStruct((S, D), F32), jax.ShapeDtypeStruct((1, 1), F32)],
        compiler_params=_cp(("arbitrary",)),
    )(y, target)


def zero_after(dep, name, shape=(8, 128)):
    def body(dep_ref, o_ref):
        o_ref[...] = jnp.zeros_like(o_ref)

    return pl.pallas_call(body, name=name, in_specs=[ANY], out_specs=pl.BlockSpec(memory_space=pltpu.VMEM),
                          out_shape=jax.ShapeDtypeStruct(shape, F32), compiler_params=_cp())(dep)


def ada_mod(c_all, ada_w, ada_b_shard, name, tn=768, dep=None):
    B, D = c_all.shape
    L, _, NS = ada_w.shape
    extra = [] if dep is None else [dep]

    def body(c_ref, w_ref, b_ref, *rest):
        cv = c_ref[...]
        cond = (cv * _sigmoid(cv)).astype(BF16)
        rest[-1][...] = _dot(cond, w_ref[...].astype(BF16)) + b_ref[...]

    return pl.pallas_call(
        body, name=name, grid=(L, NS // tn),
        in_specs=[pl.BlockSpec((B, D), lambda l, j: (0, 0)), pl.BlockSpec((None, D, tn), lambda l, j: (l, 0, j)),
                  pl.BlockSpec((None, 1, tn), lambda l, j: (l, 0, j))] + [ANY] * len(extra),
        out_specs=pl.BlockSpec((None, B, tn), lambda l, j: (l, 0, j)),
        out_shape=jax.ShapeDtypeStruct((L, B, NS), F32),
        compiler_params=_cp(("arbitrary", "arbitrary")),
    )(c_all, ada_w, ada_b_shard, *extra)


def ada_grad(c_all, dmod, name, tn=768):
    B, D = c_all.shape
    L, _, NS = dmod.shape

    def body(c_ref, d_ref, o_ref):
        cv = c_ref[...]
        cond = (cv * _sigmoid(cv)).astype(BF16)
        o_ref[...] = _dot_tn(cond, d_ref[...].astype(BF16))

    return pl.pallas_call(
        body, name=name, grid=(L, NS // tn),
        in_specs=[pl.BlockSpec((B, D), lambda l, j: (0, 0)), pl.BlockSpec((None, B, tn), lambda l, j: (l, 0, j))],
        out_specs=pl.BlockSpec((None, D, tn), lambda l, j: (l, 0, j)),
        out_shape=jax.ShapeDtypeStruct((L, D, NS), F32),
        compiler_params=_cp(("arbitrary", "arbitrary")),
    )(c_all, dmod)


def sum_gathered(land, own, me_arr, name):
    n, R, C = land.shape

    def body(me_ref, land_ref, own_ref, o_ref):
        acc = jnp.zeros(o_ref.shape, F32)
        for i in range(n):
            acc = acc + jnp.where(me_ref[0] == i, own_ref[...], land_ref[i]).astype(F32)
        o_ref[...] = acc

    tr = _row_tile(R, 64)
    grid_spec = pltpu.PrefetchScalarGridSpec(
        num_scalar_prefetch=1, grid=(R // tr,),
        in_specs=[pl.BlockSpec((n, tr, C), lambda i, me: (0, i, 0)), pl.BlockSpec((tr, C), lambda i, me: (i, 0))],
        out_specs=pl.BlockSpec((tr, C), lambda i, me: (i, 0)))
    return pl.pallas_call(
        body, name=name, grid_spec=grid_spec,
        out_shape=jax.ShapeDtypeStruct((R, C), F32),
        compiler_params=_cp(("arbitrary",)),
    )(me_arr, land, own)


def _row_tile(R, cap=512):
    if R <= cap:
        return R
    for cand in (512, 384, 352, 256, 128, 64, 32, 16, 8):
        if cand <= cap and R % cand == 0:
            return cand
    return R


def adamw(w, g, m, v, name, emit_grad=False, dep=None):
    R, C = w.shape
    tr = _row_tile(R, 256 if C > 1024 else 512)
    bc1 = 1.0 - ADAM_B1 ** ADAM_STEP
    bc2 = 1.0 - ADAM_B2 ** ADAM_STEP
    n_out = 4 if emit_grad else 3
    extra = [] if dep is None else [dep]

    def body(w_ref, g_ref, m_ref, v_ref, *outs):
        d_ref, nm_ref, nv_ref = outs[-3:]
        gv = g_ref[...]
        if emit_grad:
            outs[-4][...] = gv
        nm = ADAM_B1 * m_ref[...] + (1.0 - ADAM_B1) * gv
        nv = ADAM_B2 * v_ref[...] + (1.0 - ADAM_B2) * (gv * gv)
        nm_ref[...] = nm
        nv_ref[...] = nv
        d_ref[...] = -ADAM_LR * ((nm / bc1) / (jnp.sqrt(nv / bc2) + ADAM_EPS) + ADAM_WD * w_ref[...])

    blk = pl.BlockSpec((tr, C), lambda i: (i, 0))
    sd = jax.ShapeDtypeStruct((R, C), F32)
    return pl.pallas_call(
        body, name=name, grid=(R // tr,),
        in_specs=[blk, blk, blk, blk] + [ANY] * len(extra), out_specs=[blk] * n_out, out_shape=[sd] * n_out,
        compiler_params=_cp(("arbitrary",)),
    )(w, g, m, v, *extra)


def _as2d(a):
    if a.ndim == 1:
        return a.reshape(1, -1)
    return a.reshape(-1, a.shape[-1])


SMALL_PARAM_ELEMS = 1 << 18


def adamw_nd(w, g, m, v, name, emit_grad=False, dep=None):
    extra = [] if dep is None else [dep]
    if w.size <= SMALL_PARAM_ELEMS and not emit_grad:
        bc1 = 1.0 - ADAM_B1 ** ADAM_STEP
        bc2 = 1.0 - ADAM_B2 ** ADAM_STEP

        def body(w_ref, g_ref, m_ref, v_ref, *rest):
            d_ref, nm_ref, nv_ref = rest[-3:]
            gv = g_ref[...]
            nm = ADAM_B1 * m_ref[...] + (1.0 - ADAM_B1) * gv
            nv = ADAM_B2 * v_ref[...] + (1.0 - ADAM_B2) * (gv * gv)
            nm_ref[...] = nm
            nv_ref[...] = nv
            d_ref[...] = -ADAM_LR * ((nm / bc1) / (jnp.sqrt(nv / bc2) + ADAM_EPS) + ADAM_WD * w_ref[...])

        sd = jax.ShapeDtypeStruct(w.shape, F32)
        whole = pl.BlockSpec(memory_space=pltpu.VMEM)
        return tuple(pl.pallas_call(body, name=name, in_specs=[whole] * 4 + [ANY] * len(extra), out_specs=[whole] * 3,
                                    out_shape=[sd, sd, sd], compiler_params=_cp())(w, g.reshape(w.shape), m, v, *extra))
    outs = adamw(_as2d(w), _as2d(g.reshape(w.shape)), _as2d(m), _as2d(v), name, emit_grad=emit_grad, dep=dep)
    return tuple(o.reshape(w.shape) for o in outs)


def _place():
    x, y, c = lax.axis_index("x"), lax.axis_index("y"), lax.axis_index("c")
    chips = [(1 - x, y), (x, 1 - y), (1 - x, 1 - y)]
    return x, y, c, chips


def allgather_small(xs, name, dep=None):
    m_per, n = xs.shape
    extra = [] if dep is None else [dep]

    def body(x_ref, *rest):
        out_ref, send_sems, recv_sems, local_sem = rest[-4:]
        x, y, c, chips = _place()
        me, sibling = (x, y, c), (x, y, 1 - c)

        def rows(px, py, pc):
            return out_ref.at[pl.ds((4 * px + 2 * py + pc) * m_per, m_per), :]

        def copy(k, block, to, src=None):
            return pltpu.make_async_remote_copy(
                src_ref=rows(*block) if src is None else src, dst_ref=rows(*block),
                send_sem=send_sems.at[k], recv_sem=recv_sems.at[k], device_id=to, device_id_type=MESH_T)

        mine = pltpu.make_async_copy(x_ref, rows(*me), local_sem)
        mine.start()
        first = [copy(0, me, sibling, src=x_ref)]
        first += [copy(1 + j, me, (*chip, c), src=x_ref) for j, chip in enumerate(chips)]
        for cp in first:
            cp.start()
        passed = [copy(4 + j, (*chip, c), sibling) for j, chip in enumerate(chips)]
        for j, chip in enumerate(chips):
            copy(1 + j, (*chip, c), me).wait_recv()
            passed[j].start()
        copy(0, sibling, me).wait_recv()
        for j, chip in enumerate(chips):
            copy(4 + j, (*chip, 1 - c), me).wait_recv()
        for cp in first + passed:
            cp.wait_send()
        mine.wait()

    return pl.pallas_call(
        body, name=name,
        out_shape=jax.ShapeDtypeStruct((N_DEV * m_per, n), xs.dtype),
        in_specs=[pl.BlockSpec(memory_space=pltpu.VMEM)] + [ANY] * len(extra),
        out_specs=pl.BlockSpec(memory_space=pltpu.VMEM),
        scratch_shapes=[pltpu.SemaphoreType.DMA((7,)), pltpu.SemaphoreType.DMA((7,)), pltpu.SemaphoreType.DMA],
        compiler_params=_cp(),
    )(xs, *extra)


def _half_rows(ref, half, rh):
    idx = (slice(None),) * (len(ref.shape) - 2) + (pl.ds(pl.multiple_of(half * rh, 16), rh), slice(None))
    return ref.at[idx]


def pair_sum(g, recv, cidx, name):
    L, NS, R, C = g.shape
    rh = R // 2
    tr = _row_tile(rh, 256 if C > 1024 else 512)
    nt = rh // tr

    def body(c_ref, g_ref, r_ref, o_ref):
        o_ref[...] = (g_ref[...].astype(F32) + r_ref[...].astype(F32)).astype(BF16)

    grid_spec = pltpu.PrefetchScalarGridSpec(
        num_scalar_prefetch=1, grid=(L, NS, nt),
        in_specs=[pl.BlockSpec((None, None, tr, C), lambda l, s, t, c: (l, s, c[0] * nt + t, 0)),
                  pl.BlockSpec((None, None, tr, C), lambda l, s, t, c: (l, s, t, 0))],
        out_specs=pl.BlockSpec((None, None, tr, C), lambda l, s, t, c: (l, s, t, 0)))
    return pl.pallas_call(
        body, name=name, grid_spec=grid_spec,
        out_shape=jax.ShapeDtypeStruct((L, NS, rh, C), BF16),
        compiler_params=_cp(("arbitrary", "arbitrary", "arbitrary")),
    )(cidx, g, recv)


def chip_sum(part, recv, chip_c, name, dest=None, slot=0, n_slots=1):
    _, NS, RH, C = part.shape
    tr = _row_tile(RH, 256 if C > 1024 else 512)
    nt = RH // tr

    def body(cc_ref, p_ref, r_ref, *rest):
        o_ref = rest[-1]
        acc = p_ref[...].astype(F32)
        for k in range(3):
            acc = acc + r_ref[k].astype(F32)
        o_ref[...] = acc

    in_specs = [pl.BlockSpec((None, None, tr, C), lambda t, cc: (0, cc[0], t, 0)),
                pl.BlockSpec((3, None, tr, C), lambda t, cc: (0, 0, t, 0))]
    args = [chip_c, part, recv]
    aliases = {}
    if dest is not None:
        in_specs.append(ANY)
        args.append(dest)
        aliases = {3: 0}
    grid_spec = pltpu.PrefetchScalarGridSpec(
        num_scalar_prefetch=1, grid=(nt,), in_specs=in_specs,
        out_specs=pl.BlockSpec((None, tr, C), lambda t, cc: (slot, cc[1] * nt + t, 0)))
    return pl.pallas_call(
        body, name=name, grid_spec=grid_spec,
        out_shape=jax.ShapeDtypeStruct((n_slots, 2 * RH, C), F32),
        input_output_aliases=aliases,
        compiler_params=_cp(("arbitrary",)),
    )(*args)


HBM_SPEC = pl.BlockSpec(memory_space=pltpu.HBM)
SEM_SPEC = pl.BlockSpec(memory_space=pltpu.SEMAPHORE)
DATAFLOW_EFFECT = pltpu.SideEffectType.DATAFLOW_SIDE_EFFECTING


def split_start(bufs, copies_fn, n_copies, name, dep=None, token_shape=(8, 128)):
    nb = len(bufs)
    extra = [] if dep is None else [dep]

    def body(*refs):
        ne = len(extra)
        send_sems, recv_sems = refs[nb + ne], refs[nb + ne + 1]
        token = refs[-1]
        for cp in copies_fn(refs[:nb], send_sems, recv_sems):
            cp.start()
        token[...] = jnp.zeros_like(token)

    outs = pl.pallas_call(
        body, name=name,
        out_shape=(pltpu.SemaphoreType.DMA((n_copies,)), pltpu.SemaphoreType.DMA((n_copies,)),
                   *[pltpu.HBM(b.shape, b.dtype) for b in bufs], jax.ShapeDtypeStruct(token_shape, F32)),
        in_specs=[HBM_SPEC] * nb + [ANY] * len(extra),
        out_specs=(SEM_SPEC, SEM_SPEC, *[HBM_SPEC] * nb, pl.BlockSpec(memory_space=pltpu.VMEM)),
        input_output_aliases={i: 2 + i for i in range(nb)},
        compiler_params=pltpu.CompilerParams(has_side_effects=DATAFLOW_EFFECT),
    )(*[pltpu.with_memory_space_constraint(b, pltpu.HBM) for b in bufs], *extra)
    return outs[0], outs[1], list(outs[2:2 + nb]), outs[-1]


def split_wait(send_sems, recv_sems, bufs, after, copies_fn, name):
    nb = len(bufs)

    def body(*refs):
        for cp in copies_fn(refs[:nb], refs[nb], refs[nb + 1]):
            cp.wait_send()
            cp.wait_recv()

    outs = pl.pallas_call(
        body, name=name,
        out_shape=tuple(pltpu.HBM(b.shape, b.dtype) for b in bufs),
        in_specs=[HBM_SPEC] * nb + [SEM_SPEC, SEM_SPEC, ANY],
        out_specs=tuple([HBM_SPEC] * nb),
        input_output_aliases={i: i for i in range(nb)},
        compiler_params=pltpu.CompilerParams(has_side_effects=DATAFLOW_EFFECT),
    )(*bufs, send_sems, recv_sems, after)
    return list(outs)


def chip_exchange_copies(refs, send_sems, recv_sems):
    n = len(refs) // 2
    x, y, c, chips = _place()
    cps = []
    for a in range(n):
        for k, (cx, cy) in enumerate(chips):
            cps.append(pltpu.make_async_remote_copy(
                src_ref=refs[a].at[:, 2 * cx + cy], dst_ref=refs[n + a].at[k],
                send_sem=send_sems.at[a * 3 + k], recv_sem=recv_sems.at[a * 3 + k],
                device_id=(cx, cy, c), device_id_type=MESH_T))
    return cps


def fill_copies(refs, send_sems, recv_sems):
    x, y, c, _ = _place()
    cps = []
    for a in range(len(refs)):
        rh = refs[a].shape[-2] // 2
        cps.append(pltpu.make_async_remote_copy(
            src_ref=_half_rows(refs[a], c, rh), dst_ref=_half_rows(refs[a], c, rh),
            send_sem=send_sems.at[a], recv_sem=recv_sems.at[a], device_id=(x, y, 1 - c), device_id_type=MESH_T))
    return cps


def allgather_direct_copies(refs, send_sems, recv_sems):
    x, y, c, _ = _place()
    me = 4 * x + 2 * y + c
    cps = []
    for r in range(1, N_DEV):
        fx, fy, fc = (r >> 2) & 1, (r >> 1) & 1, r & 1
        peer = (1 - x if fx else x, 1 - y if fy else y, 1 - c if fc else c)
        cps.append(pltpu.make_async_remote_copy(
            src_ref=refs[0], dst_ref=refs[1].at[me], send_sem=send_sems.at[r - 1], recv_sem=recv_sems.at[r - 1],
            device_id=peer, device_id_type=MESH_T))
    return cps


def pair_exchange_copies(refs, send_sems, recv_sems):
    n = len(refs) // 2
    x, y, c, _ = _place()
    return [pltpu.make_async_remote_copy(
        src_ref=_half_rows(refs[a], 1 - c, refs[a].shape[-2] // 2), dst_ref=refs[n + a],
        send_sem=send_sems.at[a], recv_sem=recv_sems.at[a], device_id=(x, y, 1 - c), device_id_type=MESH_T)
        for a in range(n)]


def weight_ici_copies(refs, send_sems, recv_sems):
    x, y, c, chips = _place()
    cps = []
    for a in range(len(refs)):
        for k, (cx, cy) in enumerate(chips):
            cps.append(pltpu.make_async_remote_copy(
                src_ref=refs[a].at[2 * x + y, c], dst_ref=refs[a].at[2 * x + y, c],
                send_sem=send_sems.at[a * 3 + k], recv_sem=recv_sems.at[a * 3 + k],
                device_id=(cx, cy, c), device_id_type=MESH_T))
    return cps


def weight_d2d_copies(refs, send_sems, recv_sems):
    x, y, c, chips = _place()
    cps = []
    for a in range(len(refs)):
        for k, (cx, cy) in enumerate(chips):
            cps.append(pltpu.make_async_remote_copy(
                src_ref=refs[a].at[2 * cx + cy, c], dst_ref=refs[a].at[2 * cx + cy, c],
                send_sem=send_sems.at[a * 3 + k], recv_sem=recv_sems.at[a * 3 + k],
                device_id=(x, y, 1 - c), device_id_type=MESH_T))
    return cps


def cast_place(w, lead, chip_arr, name, dep=None):
    R, C = w.shape[-2:]
    tr = _row_tile(R, 256 if C > 1024 else 512)

    def body(chip_ref, w_ref, *rest):
        rest[-1][...] = w_ref[...].astype(BF16)

    in_specs = [pl.BlockSpec((None,) * len(lead) + (tr, C), lambda t, ch: (*lead, t, 0))]
    args = [chip_arr, w]
    if dep is not None:
        in_specs.append(ANY)
        args.append(dep)
    grid_spec = pltpu.PrefetchScalarGridSpec(
        num_scalar_prefetch=1, grid=(R // tr,), in_specs=in_specs,
        out_specs=pl.BlockSpec((None, tr, C), lambda t, ch: (ch[0], t, 0)))
    return pl.pallas_call(
        body, name=name, grid_spec=grid_spec,
        out_shape=jax.ShapeDtypeStruct((N_CHIPS, R, C), BF16),
        compiler_params=_cp(("arbitrary",)),
    )(*args)


def _s5_prepare(lam_re, lam_im, log_dt, b_re, b_im, c_re, c_im, groups_per_block):
    G, P = lam_re.shape
    N = b_re.shape[-1]
    gb = groups_per_block
    nblk = G // gb
    dt = jnp.exp(log_dt)[:, None]
    e = jnp.exp(lam_re * dt)
    a_re = e * jnp.cos(lam_im * dt)
    a_im = e * jnp.sin(lam_im * dt)
    n2 = lam_re * lam_re + lam_im * lam_im
    co_re = ((a_re - 1.0) * lam_re + a_im * lam_im) / n2
    co_im = (a_im * lam_re - (a_re - 1.0) * lam_im) / n2
    bb_re = co_re[..., None] * b_re - co_im[..., None] * b_im
    bb_im = co_re[..., None] * b_im + co_im[..., None] * b_re
    eye = jnp.eye(gb, dtype=F32)

    def blockdiag_np(m):
        m = m.reshape(nblk, gb, N, P)
        return jnp.einsum('bgnp,gh->bgnhp', m, eye).reshape(nblk, gb * N, gb * P)

    b_np_re = jnp.swapaxes(bb_re, 1, 2)
    b_np_im = jnp.swapaxes(bb_im, 1, 2)
    return (a_re.reshape(nblk, 1, gb * P), a_im.reshape(nblk, 1, gb * P),
            blockdiag_np(b_np_re), blockdiag_np(b_np_im), blockdiag_np(c_re), blockdiag_np(c_im))


def _to_scan_order(a):
    S, D = a.shape
    return a.reshape(SCAN_LANES, S // SCAN_LANES, D).transpose(1, 0, 2).reshape(S, D)


def _from_scan_order(a):
    S, D = a.shape
    return a.reshape(S // SCAN_LANES, SCAN_LANES, D).transpose(1, 0, 2).reshape(S, D)


def _pad_rows(a, mult=8):
    r = (-a.shape[0]) % mult
    if r:
        a = jnp.concatenate([a, jnp.zeros((r, a.shape[1]), a.dtype)], axis=0)
    return a


def _pack_rows(arrs, width):
    parts, offs, o = [], [], 0
    for a in arrs:
        flat = a.reshape(-1)
        r = (-flat.shape[0]) % (16 * width)
        if r:
            flat = jnp.concatenate([flat, jnp.zeros((r,), flat.dtype)])
        p = flat.reshape(-1, width)
        parts.append(p)
        offs.append((o, a.shape, a.size))
        o += p.shape[0]
    if o % 64:
        parts.append(jnp.zeros((64 - o % 64, width), parts[0].dtype))
    return jnp.concatenate(parts, axis=0), offs


def _unpack_rows(packed, offs):
    outs = []
    for o, shape, size in offs:
        rows = -(-size // packed.shape[1])
        outs.append(packed[o:o + rows].reshape(-1)[:size].reshape(shape))
    return outs


def kernel(x, c, ada_w, ada_b, norm_pre, norm_post, ffn_w_in, ffn_w_out, ab_w_in, pool_w, pool_scale, sgu_ln_g, sgu_ln_b, sgu_w, sgu_b, ab_w_out, ssm_w_in, ssm_lam_re, ssm_lam_im, ssm_b_re, ssm_b_im, ssm_c_re, ssm_c_im, ssm_d, ssm_log_dt, ssm_w_glu, loss_target, m_ada_w, m_ada_b, m_norm_pre, m_norm_post, m_ffn_w_in, m_ffn_w_out, m_ab_w_in, m_pool_w, m_pool_scale, m_sgu_ln_g, m_sgu_ln_b, m_sgu_w, m_sgu_b, m_ab_w_out, m_ssm_w_in, m_ssm_lam_re, m_ssm_lam_im, m_ssm_b_re, m_ssm_b_im, m_ssm_c_re, m_ssm_c_im, m_ssm_d, m_ssm_log_dt, m_ssm_w_glu, v_ada_w, v_ada_b, v_norm_pre, v_norm_post, v_ffn_w_in, v_ffn_w_out, v_ab_w_in, v_pool_w, v_pool_scale, v_sgu_ln_g, v_sgu_ln_b, v_sgu_w, v_sgu_b, v_ab_w_out, v_ssm_w_in, v_ssm_lam_re, v_ssm_lam_im, v_ssm_b_re, v_ssm_b_im, v_ssm_c_re, v_ssm_c_im, v_ssm_d, v_ssm_log_dt, v_ssm_w_glu):
    weights = dict(ada_w=ada_w, ada_b=ada_b, norm_pre=norm_pre, norm_post=norm_post, ffn_w_in=ffn_w_in,
                   ffn_w_out=ffn_w_out, ab_w_in=ab_w_in, pool_w=pool_w, pool_scale=pool_scale, sgu_ln_g=sgu_ln_g,
                   sgu_ln_b=sgu_ln_b, sgu_w=sgu_w, sgu_b=sgu_b, ab_w_out=ab_w_out, ssm_w_in=ssm_w_in,
                   ssm_lam_re=ssm_lam_re, ssm_lam_im=ssm_lam_im, ssm_b_re=ssm_b_re, ssm_b_im=ssm_b_im,
                   ssm_c_re=ssm_c_re, ssm_c_im=ssm_c_im, ssm_d=ssm_d, ssm_log_dt=ssm_log_dt, ssm_w_glu=ssm_w_glu)
    m_in = dict(ada_w=m_ada_w, ada_b=m_ada_b, norm_pre=m_norm_pre, norm_post=m_norm_post, ffn_w_in=m_ffn_w_in,
                ffn_w_out=m_ffn_w_out, ab_w_in=m_ab_w_in, pool_w=m_pool_w, pool_scale=m_pool_scale,
                sgu_ln_g=m_sgu_ln_g, sgu_ln_b=m_sgu_ln_b, sgu_w=m_sgu_w, sgu_b=m_sgu_b, ab_w_out=m_ab_w_out,
                ssm_w_in=m_ssm_w_in, ssm_lam_re=m_ssm_lam_re, ssm_lam_im=m_ssm_lam_im, ssm_b_re=m_ssm_b_re,
                ssm_b_im=m_ssm_b_im, ssm_c_re=m_ssm_c_re, ssm_c_im=m_ssm_c_im, ssm_d=m_ssm_d,
                ssm_log_dt=m_ssm_log_dt, ssm_w_glu=m_ssm_w_glu)
    v_in = dict(ada_w=v_ada_w, ada_b=v_ada_b, norm_pre=v_norm_pre, norm_post=v_norm_post, ffn_w_in=v_ffn_w_in,
                ffn_w_out=v_ffn_w_out, ab_w_in=v_ab_w_in, pool_w=v_pool_w, pool_scale=v_pool_scale,
                sgu_ln_g=v_sgu_ln_g, sgu_ln_b=v_sgu_ln_b, sgu_w=v_sgu_w, sgu_b=v_sgu_b, ab_w_out=v_ab_w_out,
                ssm_w_in=v_ssm_w_in, ssm_lam_re=v_ssm_lam_re, ssm_lam_im=v_ssm_lam_im, ssm_b_re=v_ssm_b_re,
                ssm_b_im=v_ssm_b_im, ssm_c_re=v_ssm_c_re, ssm_c_im=v_ssm_c_im, ssm_d=v_ssm_d,
                ssm_log_dt=v_ssm_log_dt, ssm_w_glu=v_ssm_w_glu)
    names = list(weights.keys())

    xi, yi, ci = lax.axis_index("x"), lax.axis_index("y"), lax.axis_index("c")
    chip = 2 * xi + yi
    me = 4 * xi + 2 * yi + ci
    S, D = x.shape[1], x.shape[2]
    L = ada_w.shape[0]
    NSUB = norm_pre.shape[1]
    DS = norm_pre.shape[2]
    FS = ffn_w_in.shape[-1]
    FR = ffn_w_out.shape[-2]
    x0 = x[0]
    target = loss_target[0]

    pieces = []
    for l in range(L):
        pieces.append((f"ffn_{l}_0", [(ffn_w_in, (l, 0)), (ffn_w_out, (l, 0))]))
        if l % 2 == 0:
            pieces.append((f"mix_{l}", [(ab_w_in, (l // 2,)), (ab_w_out, (l // 2,))]))
        else:
            pieces.append((f"mix_{l}", [(ssm_w_in, (l // 2,)), (ssm_w_glu, (l // 2,))]))
        pieces.append((f"ffn_{l}_1", [(ffn_w_in, (l, 1)), (ffn_w_out, (l, 1))]))
    tags = [tag for tag, _ in pieces]
    chip_arr = chip.reshape(1).astype(jnp.int32)
    wg_started, wg_passing = {}, {}

    def cast_piece(tag, ws, dep):
        wg_started[tag] = [cast_place(w, lead, chip_arr, f"wg_cast_{tag}_{i}", dep=dep).reshape(
            N_CHIPS, 2, w.shape[-2] // 2, w.shape[-1]) for i, (w, lead) in enumerate(ws)]

    def start_piece(tag, dep):
        bufs = wg_started[tag]
        wg_started[tag] = split_start(bufs, weight_ici_copies, 3 * len(bufs), f"wg_ici_start_{tag}", dep=dep,
                                      token_shape=(8, D))
        return wg_started[tag][3]

    def weights_arrived(i, after):
        ssem, rsem, bufs, _ = wg_started[tags[i]]
        lands = split_wait(ssem, rsem, bufs, after, weight_ici_copies, f"wg_ici_wait_{tags[i]}")
        wg_passing[i] = split_start(lands, weight_d2d_copies, 3 * len(lands), f"wg_d2d_start_{tags[i]}")
        return wg_passing[i][3][0, 0]

    def weights_of(i, after):
        ssem, rsem, lands, _ = wg_passing[i]
        return split_wait(ssem, rsem, lands, after, weight_d2d_copies, f"wg_d2d_wait_{tags[i]}")

    cast_piece(*pieces[0], None)
    first_token = start_piece(tags[0], None)
    for tag, ws in pieces[1:]:
        cast_piece(tag, ws, first_token)

    small_parts = [_pad_rows(p) for p in (c.reshape(D // DS, DS), norm_pre.reshape(L * NSUB, DS),
                                          norm_post.reshape(L * NSUB, DS), ssm_d.reshape(1, DS))]
    small_offs = [0]
    for p in small_parts:
        small_offs.append(small_offs[-1] + p.shape[0])
    small_all = allgather_small(jnp.concatenate(small_parts, axis=0), "ag_small", dep=wg_started[tags[-1]][-1])
    small_all = small_all.reshape(N_DEV, -1, DS)
    c_all = small_all[:, :D // DS].reshape(N_DEV, D)
    per_chip = small_all[0::2]
    o = small_offs[1]
    g_pre_full = jnp.moveaxis(per_chip[:, o:o + L * NSUB], 0, 1).reshape(L, NSUB, D)
    o = small_offs[2]
    g_post_full = jnp.moveaxis(per_chip[:, o:o + L * NSUB], 0, 1).reshape(L, NSUB, D)
    o = small_offs[3]
    d_full = jnp.moveaxis(per_chip[:, o:o + 1], 0, 1).reshape(1, D)

    NS = ada_w.shape[-1]
    ada_b_shard = lax.dynamic_slice_in_dim(ada_b, chip * NS, NS, axis=1).reshape(L, 1, NS)
    mod_part = ada_mod(c_all, ada_w, ada_b_shard, "ada_mod")
    mod_all = allgather_small(mod_part.reshape(L * N_DEV, NS), "ag_mod").reshape(N_DEV, L, N_DEV, NS)
    mod_mine = lax.dynamic_index_in_dim(mod_all[0::2], me, axis=2, keepdims=False)
    mod = jnp.moveaxis(mod_mine, 0, 1).reshape(L, NSUB, 3, D)
    wg_tokens = [first_token]
    for tag in tags[1:WG_AHEAD]:
        wg_tokens.append(start_piece(tag, mod_all))

    def vec_of(l, s):
        return jnp.concatenate([mod[l, s], g_pre_full[l, s][None], g_post_full[l, s][None],
                                jnp.zeros((3, D), F32)], axis=0)

    GB = 8
    s5_args = (ssm_lam_re[0], ssm_lam_im[0], ssm_log_dt[0], ssm_b_re[0], ssm_b_im[0], ssm_c_re[0], ssm_c_im[0])
    (a_re, a_im, bblk_re, bblk_im, cblk_re, cblk_im), s5_vjp = jax.vjp(lambda *p: _s5_prepare(*p, GB), *s5_args)
    bre16, bim16 = bblk_re.astype(BF16), bblk_im.astype(BF16)
    cre16, cim16 = cblk_re.astype(BF16), cblk_im.astype(BF16)
    cre16_t, cim16_t = jnp.swapaxes(cre16, 1, 2), jnp.swapaxes(cim16, 1, 2)

    pool_w16 = pool_w[0].astype(BF16)
    sgu_bexp = jnp.broadcast_to(sgu_b[0][:, :, None], sgu_w[0].shape)

    saved = {}
    ffn_w = {}
    xcur = x0
    stage = [0]

    def next_weights(after, vec):
        i = stage[0]
        stage[0] += 1
        weights_arrived(i, after)
        if i + WG_AHEAD < len(tags):
            vec = vec + start_piece(tags[i + WG_AHEAD], wg_passing[i][3])
        return weights_of(i, after), vec

    for l in range(L):
        v0 = vec_of(l, 0)
        g, v0p = next_weights(xcur if l else sum(wg_tokens, v0), v0)
        ffn_w[(l, 0)] = (g[0].reshape(N_CHIPS, D, FS), g[1].reshape(N_CHIPS, FR, D))
        out, h, u, f = ffn_fwd(xcur, v0p, *ffn_w[(l, 0)], f"ffn_fwd_{l}_0")
        saved[(l, 0)] = (xcur, v0, h, u, f)
        xcur = out
        v1 = vec_of(l, 1)
        mix_g, v1p = next_weights(xcur, v1)
        if l % 2 == 0:
            abin_g = mix_g[0].reshape(N_CHIPS, D, -1)
            about_g = mix_g[1].reshape(-1, D)
            h, z = pre_matmul(xcur, v1p, abin_g, f"mixa_in_{l}")
            ycat = mixa_core_fwd(z, pool_w16, pool_scale, sgu_ln_g, sgu_ln_b, sgu_w[0], sgu_bexp, f"mixa_core_{l}")
            out, f = matmul_post(ycat, about_g, xcur, v1, f"mixa_out_{l}")
            saved[(l, 1)] = (xcur, v1, h, z, ycat, f)
        else:
            sin_g = mix_g[0].reshape(1, -1, D)
            glu_g = mix_g[1].reshape(N_CHIPS, D, -1)
            h, uu = pre_matmul(xcur, v1p, sin_g, f"s5_in_{l}")
            us = _to_scan_order(uu)
            ys = s5_core_fwd(us, a_re, a_im, bre16, bim16, cre16_t, cim16_t, d_full, f"s5_core_{l}")
            yy = _from_scan_order(ys)
            out, f = s5_glu_fwd(yy, glu_g, xcur, v1, f"s5_glu_{l}")
            saved[(l, 1)] = (xcur, v1, h, us, yy, f)
        xcur = out
        v2 = vec_of(l, 2)
        g, v2p = next_weights(xcur, v2)
        ffn_w[(l, 1)] = (g[0].reshape(N_CHIPS, D, FS), g[1].reshape(N_CHIPS, FR, D))
        out, h, u, f = ffn_fwd(xcur, v2p, *ffn_w[(l, 1)], f"ffn_fwd_{l}_1")
        saved[(l, 2)] = (xcur, v2, h, u, f)
        xcur = out

    dcur, sq = loss_head(xcur, target, "loss_head")
    loss = lax.psum(sq[0, 0], ("x", "y", "c")) * (0.5 / D)

    accs = {}
    small_g = {}
    cidx = ci.reshape(1).astype(jnp.int32)
    chip_c = jnp.stack([chip, ci]).astype(jnp.int32)
    rs_open, rs_pending = [], []

    pin = [zero_after(loss.reshape(1, 1), "after_loss", (8, D))]
    last_start = [pin[0]]

    def pin_behind(token):
        pin.append(token)
        last_start[0] = token

    def pinned(v):
        for token in pin:
            v = v + (token if token.shape == v.shape else token[0, 0])
        pin.clear()
        return v

    def rs_begin(tag, items):
        arrs = [it[0] for it in items]
        lands = [lax.empty(a.shape[:-2] + (a.shape[-2] // 2, a.shape[-1]), BF16) for a in arrs]
        started = split_start(arrs + lands, pair_exchange_copies, len(arrs), f"rs_pair_start_{tag}",
                              token_shape=(8, D))
        rs_open.append((tag, items, started))
        pin_behind(started[3])

    def rs_advance(after):
        while rs_open:
            tag, items, (ssem, rsem, bufs, _) = rs_open.pop(0)
            n = len(items)
            bufs = split_wait(ssem, rsem, bufs, after, pair_exchange_copies, f"rs_pair_wait_{tag}")
            parts = [pair_sum(a, r, cidx, f"rs_pair_sum_{tag}_{i}") for i, (a, r) in enumerate(zip(bufs[:n], bufs[n:]))]
            lands = [lax.empty((3, 1) + p.shape[2:], BF16) for p in parts]
            started = split_start(parts + lands, chip_exchange_copies, 3 * n, f"rs_chip_start_{tag}",
                                  token_shape=(8, D))
            rs_pending.append((tag, items, started))
            pin_behind(started[3])

    def ffn_back(l, s, k, dcur, before_wgrads=None):
        xin, vv, h, u, f = saved[(l, s)]
        vv = pinned(vv)
        dx, df, du, act, acc = ffn_bwd(dcur, xin, f, u, vv, *ffn_w[(l, k)], f"ffn_bwd_{l}_{k}")
        accs[(l, s)] = acc
        rs_advance(dx)
        dep = None if before_wgrads is None else before_wgrads()
        g_win = tn_matmul(h, du, 1024, f"ffn_dwin_{l}_{k}", dep=dep)
        g_wout = tn_matmul_cols(act, df, FS, D, f"ffn_dwout_{l}_{k}")
        rs_begin(f"ffn_{l}_{k}", [(g_win[None], "ffn_w_in", 2 * l + k, 2 * L),
                                  (g_wout.reshape(1, N_CHIPS, FR, D), "ffn_w_out", 2 * l + k, 2 * L)])
        return dx

    grads = {}

    small_names = ["pool_w", "pool_scale", "sgu_ln_g", "sgu_ln_b", "sgu_w", "sgu_b", "ssm_lam_re", "ssm_lam_im",
                   "ssm_b_re", "ssm_b_im", "ssm_c_re", "ssm_c_im", "ssm_log_dt"]
    small_open = []

    def small_grads_start():
        acc_all = jnp.stack([jnp.stack([accs[(l, s)] for s in range(NSUB)]) for l in range(L)])
        dmod_mine = acc_all[:, :, A_SHIFT:A_GATE + 1].reshape(L, NSUB * 3 * D)
        packed, offs = _pack_rows([dmod_mine, acc_all[:, :, A_GPRE], acc_all[:, :, A_GPOST], dd_mine]
                                  + [small_g[n] for n in small_names], D)
        packed = pinned(packed).astype(BF16)
        started = split_start([packed, lax.empty((N_DEV,) + packed.shape, BF16)], allgather_direct_copies,
                              N_DEV - 1, "ag_grads_start")
        small_open.append((started, offs))
        pin_behind(started[3])
        return started[3]

    def small_grads_finish(after):
        (ssem, rsem, bufs, _), offs = small_open.pop()
        own, land = split_wait(ssem, rsem, bufs, after, allgather_direct_copies, "ag_grads_wait")
        me_arr = me.reshape(1).astype(jnp.int32)
        summed = _unpack_rows(sum_gathered(land, own, me_arr, "sum_small"), offs)
        grads.update({n: g for n, g in zip(small_names, summed[4:])})
        grads["ada_b"] = summed[0]
        grads["norm_pre"] = lax.dynamic_slice_in_dim(summed[1], chip * DS, DS, axis=2)
        grads["norm_post"] = lax.dynamic_slice_in_dim(summed[2], chip * DS, DS, axis=2)
        grads["ssm_d"] = lax.dynamic_slice_in_dim(summed[3], chip * DS, DS, axis=1)
        nm = L * NSUB * 3
        is_me = (jnp.arange(N_DEV) == me)[:, None, None]
        dmod_all = jnp.where(is_me, own[None, :nm], land[:, :nm]).reshape(N_DEV, L, NSUB * 3 * D)
        dmod_shard = lax.dynamic_slice_in_dim(jnp.moveaxis(dmod_all, 0, 1), chip * NS, NS, axis=2)
        grads["ada_w"] = ada_grad(c_all, dmod_shard, "ada_grad")

    for l in reversed(range(L)):
        dcur = ffn_back(l, 2, 1, dcur)
        if l % 2 == 0:
            xin, vv, h, z, ycat, f = saved[(l, 1)]
            vv = pinned(vv)
            df, dact, acc_post = post_bwd_matmul(dcur, f, vv, about_g, f"mixa_out_bwd_{l}")
            rs_advance(dact)
            g_about = tn_matmul_cols(ycat, df, 512, D, f"mixa_dwout_{l}")
            dz, dpw, dvecw, dsw, dsb = mixa_core_bwd(z, dact, pool_w16, pool_scale, sgu_ln_g, sgu_ln_b, sgu_w[0],
                                                     sgu_bexp, f"mixa_core_bwd_{l}")
            g_abin = tn_matmul_cols(h, dz, 512, abin_g.shape[-1], f"mixa_dwin_{l}")
            dcur, acc_pre = matmul_pre_bwd(dz, abin_g, xin, dcur, vv, f"mixa_in_bwd_{l}")
            accs[(l, 1)] = acc_pre + acc_post
            rs_begin(f"mixa_{l}", [(g_abin[None], "ab_w_in", 0, 1),
                                   (g_about.reshape(1, N_CHIPS, -1, D), "ab_w_out", 0, 1)])
            small_g.update(pool_w=dpw[None], pool_scale=dvecw[0:1], sgu_ln_g=dvecw[1:2], sgu_ln_b=dvecw[2:3],
                           sgu_w=dsw[None], sgu_b=dsb.T[None])
        else:
            xin, vv, h, us, yy, f = saved[(l, 1)]
            vv = pinned(vv)
            dy, dab, gact, acc_post = s5_glu_bwd(dcur, f, yy, vv, glu_g, f"s5_glu_bwd_{l}")
            rs_advance(dy)
            g_glu = tn_matmul_cols(gact, dab, 512, glu_g.shape[-1], f"s5_dwglu_{l}")
            dys = _to_scan_order(dy)
            dus, dbre, dbim, dcre_t, dcim_t, dar, dai, dd = s5_core_bwd(
                us, dys, a_re, a_im, bre16, bim16, cre16, cim16, d_full, f"s5_core_bwd_{l}")
            du = _from_scan_order(dus).astype(BF16)
            g_sin = tn_matmul_cols(h, du, 512, D, f"s5_dwin_{l}")
            dcur, acc_pre = matmul_pre_bwd(du, sin_g, xin, dcur, vv, f"s5_in_bwd_{l}")
            accs[(l, 1)] = acc_pre + acc_post
            rs_begin(f"s5_{l}", [(g_sin.reshape(1, N_CHIPS, -1, D), "ssm_w_in", 0, 1), (g_glu[None], "ssm_w_glu", 0, 1)])
            s5_grads = s5_vjp((dar, dai, dbre, dbim, jnp.swapaxes(dcre_t, 1, 2), jnp.swapaxes(dcim_t, 1, 2)))
            small_g.update(ssm_lam_re=s5_grads[0][None], ssm_lam_im=s5_grads[1][None], ssm_log_dt=s5_grads[2][None],
                           ssm_b_re=s5_grads[3][None], ssm_b_im=s5_grads[4][None], ssm_c_re=s5_grads[5][None],
                           ssm_c_im=s5_grads[6][None])
            dd_mine = dd
        dcur = ffn_back(l, 0, 0, dcur, before_wgrads=small_grads_start if l == 0 else None)
    grad_x = dcur[None]
    rs_advance(last_start[0])
    small_grads_finish(last_start[0])

    big_names = ["ffn_w_in", "ffn_w_out", "ab_w_in", "ab_w_out", "ssm_w_in", "ssm_w_glu"]
    deltas, new_m, new_v = {}, {}, {}

    def update(n, dep=None):
        g = grads[n].reshape(weights[n].shape)
        outs = adamw_nd(weights[n], g, m_in[n], v_in[n], f"adamw_{n}", emit_grad=n in big_names, dep=dep)
        grads[n] = outs[0] if n in big_names else g
        deltas[n], new_m[n], new_v[n] = outs[-3:]

    fin = {}

    def chip_sums(pending, after):
        for tag, items, (ssem, rsem, bufs, _) in pending:
            bufs = split_wait(ssem, rsem, bufs, after, chip_exchange_copies, f"rs_chip_wait_{tag}")
            n = len(items)
            for i, (it, p, r) in enumerate(zip(items, bufs[:n], bufs[n:])):
                fin[it[1]] = chip_sum(p, r, chip_c, f"rs_chip_sum_{tag}_{i}", dest=fin.get(it[1]), slot=it[2],
                                      n_slots=it[3])

    chip_sums(rs_pending[:-1], pinned(grads["ada_b"]))
    update("ada_w", dep=fin["ab_w_in"])
    chip_sums(rs_pending[-1:], deltas["ada_w"])
    ssem, rsem, bufs, fill_token = split_start([fin[n] for n in big_names], fill_copies, len(big_names), "rs_fill_start")
    for n in names:
        if n not in big_names and n != "ada_w":
            update(n, dep=fill_token)
    filled = split_wait(ssem, rsem, bufs, deltas["ssm_log_dt"], fill_copies, "rs_fill_wait")
    for n, g in zip(big_names, filled):
        grads[n] = g
        update(n)

    return (loss, grad_x, *[grads[n] for n in names], *[deltas[n] for n in names],
            *[new_m[n] for n in names], *[new_v[n] for n in names])
```

```python
import math

import jax
import jax.numpy as jnp
from jax import lax
from jax.experimental import pallas as pl
from jax.experimental.pallas import tpu as pltpu

F32 = jnp.float32
BF16 = jnp.bfloat16
EPS = 1e-6
MESH_T = pl.DeviceIdType.MESH
VMEM_LIMIT_BYTES = 56 * 1024 * 1024
N_CHIPS = 4
N_DEV = 8
POOL_WINDOWS = (2, 4, 8, 16)
CHUNK = 128
SCAN_LANES = 8
SCAN_UNROLL = 4
WG_AHEAD = 2
FFN_BWD_CHUNK = 2048
ADAM_LR = 0.001
ADAM_B1 = 0.9
ADAM_B2 = 0.999
ADAM_EPS = 1e-08
ADAM_WD = 0.01
ADAM_STEP = 10
GELU_C = math.sqrt(2.0 / math.pi)
GELU_K = 0.044715

V_SHIFT, V_SCALE, V_GATE, V_GPRE, V_GPOST = 0, 1, 2, 3, 4
A_SHIFT, A_SCALE, A_GATE, A_GPRE, A_GPOST = 0, 1, 2, 3, 4

ANY = pl.BlockSpec(memory_space=pl.ANY)


def _cp(sem=None):
    if sem is None:
        return pltpu.CompilerParams(vmem_limit_bytes=VMEM_LIMIT_BYTES)
    return pltpu.CompilerParams(vmem_limit_bytes=VMEM_LIMIT_BYTES, dimension_semantics=sem)


def _dot(a, b):
    return jnp.dot(a, b, preferred_element_type=F32)


def _dot_nt(a, b):
    return lax.dot_general(a, b, (((1,), (1,)), ((), ())), preferred_element_type=F32)


def _dot_tn(a, b):
    return lax.dot_general(a, b, (((0,), (0,)), ((), ())), preferred_element_type=F32)


def _sigmoid(x):
    return 1.0 / (1.0 + jnp.exp(-x))


def _gelu(x):
    return 0.5 * x * (1.0 + jnp.tanh(GELU_C * (x + GELU_K * x * x * x)))


def _gelu_grad(x):
    t = jnp.tanh(GELU_C * (x + GELU_K * x * x * x))
    return 0.5 * (1.0 + t) + 0.5 * x * (1.0 - t * t) * GELU_C * (1.0 + 3.0 * GELU_K * x * x)


def _rowsum(v):
    return jnp.sum(v, axis=0, keepdims=True)


def _lanemean(v):
    return jnp.mean(v, axis=-1, keepdims=True)


def _row(ref, i):
    return ref[pl.ds(i, 1), :]


def _pre_fwd(x, vec_ref):
    r = lax.rsqrt(_lanemean(x * x) + EPS)
    return (x * r) * _row(vec_ref, V_GPRE) * (1.0 + _row(vec_ref, V_SCALE)) + _row(vec_ref, V_SHIFT)


def _pre_bwd(x, dh, vec_ref):
    g = _row(vec_ref, V_GPRE)
    sc = 1.0 + _row(vec_ref, V_SCALE)
    r = lax.rsqrt(_lanemean(x * x) + EPS)
    xn = x * r
    dhx = dh * xn
    t2 = dh * (g * sc)
    dx = r * (t2 - xn * _lanemean(t2 * xn))
    return dx, _rowsum(dh), _rowsum(dhx * g), _rowsum(dhx * sc)


def _post_fwd(x, f, vec_ref, rw):
    q = lax.rsqrt(_lanemean(f * f) + EPS)
    return x + (rw * _row(vec_ref, V_GATE)) * (f * q * _row(vec_ref, V_GPOST))


def _post_bwd(dout, f, vec_ref, rw):
    gp = _row(vec_ref, V_GPOST)
    gate = _row(vec_ref, V_GATE)
    q = lax.rsqrt(_lanemean(f * f) + EPS)
    fhat = f * q
    dgate = _rowsum(dout * (rw * fhat * gp))
    dy = dout * (rw * gate)
    dgpost = _rowsum(dy * fhat)
    t = dy * gp
    df = q * (t - fhat * _lanemean(t * fhat))
    return df, dgate, dgpost


def _acc_add(acc_ref, row, v):
    acc_ref[pl.ds(row, 1), :] += v


def ffn_fwd(x, vec, win_g, wout_g, name, tm=512):
    S, D = x.shape
    FS = win_g.shape[-1]
    FR = wout_g.shape[-2]

    def body(x_ref, vec_ref, win_hbm, wout_hbm, out_ref, h_ref, u_ref, f_ref, win_s, wout_s, sem):
        @pl.when(pl.program_id(0) == 0)
        def _():
            cps = [pltpu.make_async_copy(win_hbm.at[j], win_s.at[j], sem.at[j]) for j in range(N_CHIPS)]
            cps += [pltpu.make_async_copy(wout_hbm.at[j], wout_s.at[pl.ds(j * FR, FR), :], sem.at[N_CHIPS + j])
                    for j in range(N_CHIPS)]
            for cp in cps:
                cp.start()
            for cp in cps:
                cp.wait()

        xv = x_ref[...]
        h = _pre_fwd(xv, vec_ref).astype(BF16)
        h_ref[...] = h
        f = jnp.zeros((tm, D), F32)
        for j in range(2):
            a = _dot(h, win_s[j])
            b = _dot(h, win_s[2 + j])
            u_ref[j] = a.astype(BF16)
            u_ref[2 + j] = b.astype(BF16)
            act = (a * _sigmoid(a) * b).astype(BF16)
            f = f + _dot(act, wout_s[pl.ds(j * FS, FS), :])
        f_ref[...] = f
        out_ref[...] = _post_fwd(xv, f, vec_ref, 0.5)

    row = pl.BlockSpec((tm, D), lambda i: (i, 0))
    return pl.pallas_call(
        body, name=name, grid=(S // tm,),
        in_specs=[row, pl.BlockSpec((8, D), lambda i: (0, 0)), ANY, ANY],
        out_specs=[row, row, pl.BlockSpec((N_CHIPS, tm, FS), lambda i: (0, i, 0)), row],
        out_shape=[jax.ShapeDtypeStruct((S, D), F32), jax.ShapeDtypeStruct((S, D), BF16),
                   jax.ShapeDtypeStruct((N_CHIPS, S, FS), BF16), jax.ShapeDtypeStruct((S, D), F32)],
        scratch_shapes=[pltpu.VMEM((N_CHIPS, D, FS), BF16), pltpu.VMEM((N_CHIPS * FR, D), BF16),
                        pltpu.SemaphoreType.DMA((2 * N_CHIPS,))],
        compiler_params=_cp(("arbitrary",)),
    )(x, vec, win_g, wout_g)


def ffn_bwd(dout, x, f, u, vec, win_g, wout_g, name, tm=256):
    S, D = x.shape
    FS = win_g.shape[-1]
    FR = wout_g.shape[-2]

    def body(dout_ref, x_ref, f_ref, u_ref, vec_ref, win_hbm, wout_hbm,
             dx_ref, df_ref, du_ref, act_ref, acc_ref, win_s, wout_s, sem):
        @pl.when(pl.program_id(0) == 0)
        def _():
            cps = [pltpu.make_async_copy(win_hbm.at[j], win_s.at[j], sem.at[j]) for j in range(N_CHIPS)]
            cps += [pltpu.make_async_copy(wout_hbm.at[j], wout_s.at[pl.ds(j * FR, FR), :], sem.at[N_CHIPS + j])
                    for j in range(N_CHIPS)]
            for cp in cps:
                cp.start()
            acc_ref[...] = jnp.zeros_like(acc_ref)
            for cp in cps:
                cp.wait()

        dout_v = dout_ref[...]
        df, dgate, dgpost = _post_bwd(dout_v, f_ref[...], vec_ref, 0.5)
        dfb = df.astype(BF16)
        df_ref[...] = dfb
        dh = jnp.zeros((tm, D), F32)
        chunks = [(c0, min(FFN_BWD_CHUNK, FS - c0)) for c0 in range(0, FS, FFN_BWD_CHUNK)]
        for j in range(2):
            for c0, cw in chunks:
                a = u_ref[j, :, c0:c0 + cw].astype(F32)
                b = u_ref[2 + j, :, c0:c0 + cw].astype(F32)
                sig = _sigmoid(a)
                sl = a * sig
                dact = _dot_nt(dfb, wout_s[pl.ds(j * FS + c0, cw), :])
                da = (dact * b * (sig * (1.0 + a * (1.0 - sig)))).astype(BF16)
                db = (dact * sl).astype(BF16)
                du_ref[j, :, c0:c0 + cw] = da
                du_ref[2 + j, :, c0:c0 + cw] = db
                act_ref[:, j * FS + c0:j * FS + c0 + cw] = (sl * b).astype(BF16)
                dh = dh + _dot_nt(da, win_s[j, :, c0:c0 + cw]) + _dot_nt(db, win_s[2 + j, :, c0:c0 + cw])
        dx, dshift, dscale, dgpre = _pre_bwd(x_ref[...], dh, vec_ref)
        dx_ref[...] = dout_v + dx
        _acc_add(acc_ref, A_SHIFT, dshift)
        _acc_add(acc_ref, A_SCALE, dscale)
        _acc_add(acc_ref, A_GATE, dgate)
        _acc_add(acc_ref, A_GPRE, dgpre)
        _acc_add(acc_ref, A_GPOST, dgpost)

    row = pl.BlockSpec((tm, D), lambda i: (i, 0))
    ublk = pl.BlockSpec((N_CHIPS, tm, FS), lambda i: (0, i, 0))
    const = pl.BlockSpec((8, D), lambda i: (0, 0))
    return pl.pallas_call(
        body, name=name, grid=(S // tm,),
        in_specs=[row, row, row, ublk, const, ANY, ANY],
        out_specs=[row, row, ublk, pl.BlockSpec((tm, 2 * FS), lambda i: (i, 0)), const],
        out_shape=[jax.ShapeDtypeStruct((S, D), F32), jax.ShapeDtypeStruct((S, D), BF16),
                   jax.ShapeDtypeStruct((N_CHIPS, S, FS), BF16), jax.ShapeDtypeStruct((S, 2 * FS), BF16),
                   jax.ShapeDtypeStruct((8, D), F32)],
        scratch_shapes=[pltpu.VMEM((N_CHIPS, D, FS), BF16), pltpu.VMEM((N_CHIPS * FR, D), BF16),
                        pltpu.SemaphoreType.DMA((2 * N_CHIPS,))],
        compiler_params=_cp(("arbitrary",)),
    )(dout, x, f, u, vec, win_g, wout_g)


def tn_matmul(a, b, tk, name, dep=None):
    S, K = a.shape
    nb, _, tn = b.shape
    extra = [] if dep is None else [dep]

    def body(a_ref, b_ref, *rest):
        rest[-1][...] = _dot_tn(a_ref[...], b_ref[...]).astype(BF16)

    return pl.pallas_call(
        body, name=name, grid=(K // tk, nb),
        in_specs=[pl.BlockSpec((S, tk), lambda i, j: (0, i)), pl.BlockSpec((None, S, tn), lambda i, j: (j, 0, 0))]
        + [ANY] * len(extra),
        out_specs=pl.BlockSpec((None, tk, tn), lambda i, j: (j, i, 0)),
        out_shape=jax.ShapeDtypeStruct((nb, K, tn), BF16),
        compiler_params=_cp(("arbitrary", "arbitrary")),
    )(a, b, *extra)


def tn_matmul_cols(a, b, tk, tn, name):
    S, K = a.shape
    N = b.shape[1]

    def body(a_ref, b_ref, o_ref):
        o_ref[...] = _dot_tn(a_ref[...], b_ref[...]).astype(BF16)

    return pl.pallas_call(
        body, name=name, grid=(K // tk, N // tn),
        in_specs=[pl.BlockSpec((S, tk), lambda i, j: (0, i)), pl.BlockSpec((S, tn), lambda i, j: (0, j))],
        out_specs=pl.BlockSpec((None, tk, tn), lambda i, j: (j, i, 0)),
        out_shape=jax.ShapeDtypeStruct((N // tn, K, tn), BF16),
        compiler_params=_cp(("arbitrary", "arbitrary")),
    )(a, b)


def pre_matmul(x, vec, w3, name, tm=512):
    S, D = x.shape
    nj, _, Nj = w3.shape

    def body(x_ref, vec_ref, w_ref, h_ref, z_ref):
        h = _pre_fwd(x_ref[...], vec_ref).astype(BF16)
        h_ref[...] = h
        for j in range(nj):
            z_ref[:, j * Nj:(j + 1) * Nj] = _dot(h, w_ref[j])

    row = pl.BlockSpec((tm, D), lambda i: (i, 0))
    return pl.pallas_call(
        body, name=name, grid=(S // tm,),
        in_specs=[row, pl.BlockSpec((8, D), lambda i: (0, 0)), pl.BlockSpec((nj, D, Nj), lambda i: (0, 0, 0))],
        out_specs=[row, pl.BlockSpec((tm, nj * Nj), lambda i: (i, 0))],
        out_shape=[jax.ShapeDtypeStruct((S, D), BF16), jax.ShapeDtypeStruct((S, nj * Nj), F32)],
        compiler_params=_cp(("arbitrary",)),
    )(x, vec, w3)


def matmul_pre_bwd(dz, w3, x, dres, vec, name, tm=512):
    S, D = x.shape
    nj, _, Nj = w3.shape

    def body(dz_ref, w_ref, x_ref, dres_ref, vec_ref, dx_ref, acc_ref):
        @pl.when(pl.program_id(0) == 0)
        def _():
            acc_ref[...] = jnp.zeros_like(acc_ref)

        dh = jnp.zeros((tm, D), F32)
        for j in range(nj):
            dh = dh + _dot_nt(dz_ref[:, j * Nj:(j + 1) * Nj], w_ref[j])
        dx, dshift, dscale, dgpre = _pre_bwd(x_ref[...], dh, vec_ref)
        dx_ref[...] = dres_ref[...] + dx
        _acc_add(acc_ref, A_SHIFT, dshift)
        _acc_add(acc_ref, A_SCALE, dscale)
        _acc_add(acc_ref, A_GPRE, dgpre)

    row = pl.BlockSpec((tm, D), lambda i: (i, 0))
    const = pl.BlockSpec((8, D), lambda i: (0, 0))
    return pl.pallas_call(
        body, name=name, grid=(S // tm,),
        in_specs=[pl.BlockSpec((tm, nj * Nj), lambda i: (i, 0)), pl.BlockSpec((nj, D, Nj), lambda i: (0, 0, 0)),
                  row, row, const],
        out_specs=[row, const],
        out_shape=[jax.ShapeDtypeStruct((S, D), F32), jax.ShapeDtypeStruct((8, D), F32)],
        compiler_params=_cp(("arbitrary",)),
    )(dz, w3, x, dres, vec)


def matmul_post(act, w, x, vec, name, tm=512):
    S, D = x.shape
    K = act.shape[1]

    def body(act_ref, w_ref, x_ref, vec_ref, out_ref, f_ref):
        f = _dot(act_ref[...], w_ref[...])
        f_ref[...] = f
        out_ref[...] = _post_fwd(x_ref[...], f, vec_ref, 1.0)

    row = pl.BlockSpec((tm, D), lambda i: (i, 0))
    return pl.pallas_call(
        body, name=name, grid=(S // tm,),
        in_specs=[pl.BlockSpec((tm, K), lambda i: (i, 0)), pl.BlockSpec((K, D), lambda i: (0, 0)), row,
                  pl.BlockSpec((8, D), lambda i: (0, 0))],
        out_specs=[row, row],
        out_shape=[jax.ShapeDtypeStruct((S, D), F32), jax.ShapeDtypeStruct((S, D), F32)],
        compiler_params=_cp(("arbitrary",)),
    )(act, w, x, vec)


def post_bwd_matmul(dout, f, vec, w, name, tm=512):
    S, D = dout.shape
    K = w.shape[0]

    def body(dout_ref, f_ref, vec_ref, w_ref, df_ref, dact_ref, acc_ref):
        @pl.when(pl.program_id(0) == 0)
        def _():
            acc_ref[...] = jnp.zeros_like(acc_ref)

        df, dgate, dgpost = _post_bwd(dout_ref[...], f_ref[...], vec_ref, 1.0)
        dfb = df.astype(BF16)
        df_ref[...] = dfb
        dact_ref[...] = _dot_nt(dfb, w_ref[...])
        _acc_add(acc_ref, A_GATE, dgate)
        _acc_add(acc_ref, A_GPOST, dgpost)

    row = pl.BlockSpec((tm, D), lambda i: (i, 0))
    const = pl.BlockSpec((8, D), lambda i: (0, 0))
    return pl.pallas_call(
        body, name=name, grid=(S // tm,),
        in_specs=[row, row, const, pl.BlockSpec((K, D), lambda i: (0, 0))],
        out_specs=[row, pl.BlockSpec((tm, K), lambda i: (i, 0)), const],
        out_shape=[jax.ShapeDtypeStruct((S, D), BF16), jax.ShapeDtypeStruct((S, K), F32),
                   jax.ShapeDtypeStruct((8, D), F32)],
        compiler_params=_cp(("arbitrary",)),
    )(dout, f, vec, w)


def _band(w, transposed, prev):
    r = lax.broadcasted_iota(jnp.int32, (CHUNK, CHUNK), 1 if transposed else 0)
    c = lax.broadcasted_iota(jnp.int32, (CHUNK, CHUNK), 0 if transposed else 1)
    d = r - c
    m = (d + CHUNK < w) if prev else ((d >= 0) & (d < w))
    return jnp.where(m, 1.0, 0.0).astype(BF16)


def _split_hi_lo(a):
    hi = a.astype(BF16)
    lo = (a - hi.astype(F32)).astype(BF16)
    return hi, lo


def _pool_diff(a, ap, g, denom):
    w = POOL_WINDOWS[g]
    a_hi, a_lo = _split_hi_lo(a)
    p_hi, p_lo = _split_hi_lo(ap)
    mc = _band(w, False, False)
    mp = _band(w, False, True)
    win = _dot(mc, a_hi) + _dot(mc, a_lo) + _dot(mp, p_hi) + _dot(mp, p_lo)
    return win / denom - a


def _sgu_norm(v, lg, lb):
    mu = _lanemean(v)
    xc = v - mu
    rstd = lax.rsqrt(_lanemean(xc * xc) + EPS)
    xhat = xc * rstd
    return xhat, rstd, xhat * lg + lb


def _tril_mask():
    r = lax.broadcasted_iota(jnp.int32, (CHUNK, CHUNK), 0)
    c = lax.broadcasted_iota(jnp.int32, (CHUNK, CHUNK), 1)
    return r >= c


def _positions(i, w):
    r = lax.broadcasted_iota(jnp.int32, (CHUNK, 128), 0)
    pos = (i * CHUNK + r + 1).astype(F32)
    return jnp.minimum(pos, float(w))


def mixa_core_fwd(z, pool_w, pool_scale, ln_g, ln_b, sgu_w, sgu_bexp, name):
    S = z.shape[0]
    W = z.shape[1] // 3
    G = len(POOL_WINDOWS)
    GD = W // G

    def body(zc_ref, zp_ref, pw_ref, ps_ref, lg_ref, lb_ref, sw_ref, sb_ref, y_ref):
        i = pl.program_id(0)
        has_prev = jnp.where(i > 0, 1.0, 0.0)
        for g in range(G):
            sl = slice(g * GD, (g + 1) * GD)
            a = zc_ref[:, sl]
            ap = zp_ref[:, sl] * has_prev
            d = _pool_diff(a, ap, g, _positions(i, POOL_WINDOWS[g])).astype(BF16)
            y_ref[:, sl] = (_dot(d, pw_ref[g]) * ps_ref[:, sl]).astype(BF16)
        tril = _tril_mask()
        for hh in range(G):
            u = _gelu(zc_ref[:, W + hh * GD:W + (hh + 1) * GD])
            v = _gelu(zc_ref[:, 2 * W + hh * GD:2 * W + (hh + 1) * GD])
            sl = slice(hh * GD, (hh + 1) * GD)
            _, _, vn = _sgu_norm(v, lg_ref[:, sl], lb_ref[:, sl])
            wm = jnp.where(tril, sw_ref[hh], 0.0).astype(BF16)
            s = _dot(wm, vn.astype(BF16)) + sb_ref[hh]
            y_ref[:, W + hh * GD:W + (hh + 1) * GD] = (u * s).astype(BF16)

    vecw = pl.BlockSpec((1, W), lambda i: (0, 0))
    mats = pl.BlockSpec((G, GD, GD), lambda i: (0, 0, 0))
    return pl.pallas_call(
        body, name=name, grid=(S // CHUNK,),
        in_specs=[pl.BlockSpec((CHUNK, 3 * W), lambda i: (i, 0)),
                  pl.BlockSpec((CHUNK, W), lambda i: (jnp.maximum(i - 1, 0), 0)),
                  mats, vecw, vecw, vecw, mats, mats],
        out_specs=pl.BlockSpec((CHUNK, 2 * W), lambda i: (i, 0)),
        out_shape=jax.ShapeDtypeStruct((S, 2 * W), BF16),
        compiler_params=_cp(("arbitrary",)),
    )(z, z, pool_w, pool_scale, ln_g, ln_b, sgu_w, sgu_bexp)


def mixa_core_bwd(z, dy, pool_w, pool_scale, ln_g, ln_b, sgu_w, sgu_bexp, name):
    S = z.shape[0]
    W = z.shape[1] // 3
    G = len(POOL_WINDOWS)
    GD = W // G
    n_tiles = S // CHUNK

    def body(zc_ref, zp_ref, dyc_ref, dyn_ref, pw_ref, ps_ref, lg_ref, lb_ref, sw_ref, sb_ref,
             dz_ref, dpw_ref, dvec_ref, dsw_ref, dsb_ref):
        i = pl.program_id(0)

        @pl.when(i == 0)
        def _():
            dpw_ref[...] = jnp.zeros_like(dpw_ref)
            dvec_ref[...] = jnp.zeros_like(dvec_ref)
            dsw_ref[...] = jnp.zeros_like(dsw_ref)
            dsb_ref[...] = jnp.zeros_like(dsb_ref)

        has_prev = jnp.where(i > 0, 1.0, 0.0)
        has_next = jnp.where(i < n_tiles - 1, 1.0, 0.0)
        for g in range(G):
            w = POOL_WINDOWS[g]
            sl = slice(g * GD, (g + 1) * GD)
            a = zc_ref[:, sl]
            ap = zp_ref[:, sl] * has_prev
            den_c = _positions(i, w)
            den_n = _positions(i + 1, w)
            d = _pool_diff(a, ap, g, den_c).astype(BF16)
            ps = ps_ref[:, sl]
            pw = pw_ref[g]
            dyc = dyc_ref[:, sl]
            dvec_ref[pl.ds(0, 1), sl] += _rowsum(dyc * _dot(d, pw))
            dyp_c = (dyc * ps).astype(BF16)
            dyp_n = (dyn_ref[:, sl] * (ps * has_next)).astype(BF16)
            dpw_ref[g] += _dot_tn(d, dyp_c)
            dd_c = _dot_nt(dyp_c, pw)
            dd_n = _dot_nt(dyp_n, pw)
            da = (_dot(_band(w, True, False), (dd_c / den_c).astype(BF16))
                  + _dot(_band(w, True, True), (dd_n / den_n).astype(BF16)) - dd_c)
            dz_ref[:, sl] = da.astype(BF16)
        tril = _tril_mask()
        for hh in range(G):
            sl = slice(hh * GD, (hh + 1) * GD)
            zu = zc_ref[:, W + hh * GD:W + (hh + 1) * GD]
            zv = zc_ref[:, 2 * W + hh * GD:2 * W + (hh + 1) * GD]
            u = _gelu(zu)
            v = _gelu(zv)
            lg = lg_ref[:, sl]
            xhat, rstd, vn = _sgu_norm(v, lg, lb_ref[:, sl])
            vnb = vn.astype(BF16)
            wm = jnp.where(tril, sw_ref[hh], 0.0).astype(BF16)
            s = _dot(wm, vnb) + sb_ref[hh]
            dyb = dyc_ref[:, W + hh * GD:W + (hh + 1) * GD]
            du = dyb * s
            ds = dyb * u
            dsb_ref[:, hh:hh + 1] += jnp.sum(ds, axis=1, keepdims=True)
            dsb16 = ds.astype(BF16)
            dsw_ref[hh] += jnp.where(tril, _dot_nt(dsb16, vnb), 0.0)
            dvn = _dot_tn(wm, dsb16)
            dvec_ref[pl.ds(1, 1), sl] += _rowsum(dvn * xhat)
            dvec_ref[pl.ds(2, 1), sl] += _rowsum(dvn)
            dxh = dvn * lg
            dv = rstd * (dxh - _lanemean(dxh) - xhat * _lanemean(dxh * xhat))
            dz_ref[:, W + hh * GD:W + (hh + 1) * GD] = (du * _gelu_grad(zu)).astype(BF16)
            dz_ref[:, 2 * W + hh * GD:2 * W + (hh + 1) * GD] = (dv * _gelu_grad(zv)).astype(BF16)

    vecw = pl.BlockSpec((1, W), lambda i: (0, 0))
    mats = pl.BlockSpec((G, GD, GD), lambda i: (0, 0, 0))
    return pl.pallas_call(
        body, name=name, grid=(n_tiles,),
        in_specs=[pl.BlockSpec((CHUNK, 3 * W), lambda i: (i, 0)),
                  pl.BlockSpec((CHUNK, W), lambda i: (jnp.maximum(i - 1, 0), 0)),
                  pl.BlockSpec((CHUNK, 2 * W), lambda i: (i, 0)),
                  pl.BlockSpec((CHUNK, W), lambda i: (jnp.minimum(i + 1, n_tiles - 1), 0)),
                  mats, vecw, vecw, vecw, mats, mats],
        out_specs=[pl.BlockSpec((CHUNK, 3 * W), lambda i: (i, 0)), mats,
                   pl.BlockSpec((8, W), lambda i: (0, 0)), mats, pl.BlockSpec((CHUNK, G), lambda i: (0, 0))],
        out_shape=[jax.ShapeDtypeStruct((S, 3 * W), BF16), jax.ShapeDtypeStruct((G, GD, GD), F32),
                   jax.ShapeDtypeStruct((8, W), F32), jax.ShapeDtypeStruct((G, CHUNK, CHUNK), F32),
                   jax.ShapeDtypeStruct((CHUNK, G), F32)],
        compiler_params=_cp(("arbitrary",)),
    )(z, z, dy, dy, pool_w, pool_scale, ln_g, ln_b, sgu_w, sgu_bexp)


def _cmul(ar, ai, br, bi):
    return ar * br - ai * bi, ar * bi + ai * br


def _cpow(ar, ai, n):
    rr, ri = None, None
    br, bi = ar, ai
    while n:
        if n & 1:
            rr, ri = (br, bi) if rr is None else _cmul(rr, ri, br, bi)
        n >>= 1
        if n:
            br, bi = _cmul(br, bi, br, bi)
    return rr, ri


def _seg_rows(k):
    return pl.ds(pl.multiple_of(k * SCAN_LANES, SCAN_LANES), SCAN_LANES)


def _scan_fwd(xre, xim, carry, ar, ai, K):
    P = xre.shape[1]
    a8r = jnp.broadcast_to(ar, (SCAN_LANES, P))
    a8i = jnp.broadcast_to(ai, (SCAN_LANES, P))

    def local(k, c):
        pr, pi = c
        rows = _seg_rows(k)
        nr = a8r * pr - a8i * pi + xre[rows, :]
        ni = a8r * pi + a8i * pr + xim[rows, :]
        xre[rows, :] = nr
        xim[rows, :] = ni
        return nr, ni

    er, ei = lax.fori_loop(1, K, local, (xre[pl.ds(0, SCAN_LANES), :], xim[pl.ds(0, SCAN_LANES), :]),
                           unroll=SCAN_UNROLL)
    akr, aki = _cpow(ar, ai, K)
    cr = jnp.zeros((1, P), F32)
    ci = jnp.zeros((1, P), F32)
    carry[pl.ds(0, 1), :] = cr
    carry[pl.ds(SCAN_LANES, 1), :] = ci
    for j in range(1, SCAN_LANES):
        tr, ti = _cmul(akr, aki, cr, ci)
        cr = er[j - 1:j, :] + tr
        ci = ei[j - 1:j, :] + ti
        carry[pl.ds(j, 1), :] = cr
        carry[pl.ds(SCAN_LANES + j, 1), :] = ci
    cmr = carry[pl.ds(0, SCAN_LANES), :]
    cmi = carry[pl.ds(SCAN_LANES, SCAN_LANES), :]

    def fix(k, c):
        pr, pi = c
        rows = _seg_rows(k)
        tr, ti = _cmul(pr, pi, cmr, cmi)
        xre[rows, :] += tr
        xim[rows, :] += ti
        return _cmul(pr, pi, a8r, a8i)

    lax.fori_loop(0, K, fix, (a8r, a8i), unroll=SCAN_UNROLL)


def s5_core_fwd(u, ar, ai, bre, bim, cre_t, cim_t, dskip, name):
    S, D = u.shape
    nblk, UB, PB = bre.shape
    K = S // SCAN_LANES

    def body(u_ref, ar_ref, ai_ref, bre_ref, bim_ref, cre_ref, cim_ref, d_ref, y_ref, xre, xim, carry):
        uv = u_ref[...]
        ub = uv.astype(BF16)
        xre[...] = _dot(ub, bre_ref[...])
        xim[...] = _dot(ub, bim_ref[...])
        _scan_fwd(xre, xim, carry, ar_ref[...], ai_ref[...], K)
        y_ref[...] = (_dot(xre[...].astype(BF16), cre_ref[...]) - _dot(xim[...].astype(BF16), cim_ref[...])
                      + d_ref[...] * uv)

    ucol = pl.BlockSpec((S, UB), lambda i: (0, i))
    pvec = pl.BlockSpec((None, 1, PB), lambda i: (i, 0, 0))
    bmat = pl.BlockSpec((None, UB, PB), lambda i: (i, 0, 0))
    cmat = pl.BlockSpec((None, PB, UB), lambda i: (i, 0, 0))
    return pl.pallas_call(
        body, name=name, grid=(nblk,),
        in_specs=[ucol, pvec, pvec, bmat, bmat, cmat, cmat, pl.BlockSpec((1, UB), lambda i: (0, i))],
        out_specs=ucol,
        out_shape=jax.ShapeDtypeStruct((S, D), F32),
        scratch_shapes=[pltpu.VMEM((S, PB), F32), pltpu.VMEM((S, PB), F32), pltpu.VMEM((2 * SCAN_LANES, PB), F32)],
        compiler_params=_cp(("arbitrary",)),
    )(u, ar, ai, bre, bim, cre_t, cim_t, dskip)


def s5_core_bwd(u, dy, ar, ai, bre, bim, cre, cim, dskip, name):
    S, D = u.shape
    nblk, UB, PB = bre.shape
    K = S // SCAN_LANES

    def body(u_ref, dy_ref, ar_ref, ai_ref, bre_ref, bim_ref, cre_ref, cim_ref, d_ref,
             du_ref, dbre_ref, dbim_ref, dcre_ref, dcim_ref, dar_ref, dai_ref, dd_ref,
             xre, xim, gre, gim, carry, carry_b):
        ar = ar_ref[...]
        ai = ai_ref[...]
        uv = u_ref[...]
        ub = uv.astype(BF16)
        dyv = dy_ref[...]
        dyb = dyv.astype(BF16)
        xre[...] = _dot(ub, bre_ref[...])
        xim[...] = _dot(ub, bim_ref[...])
        _scan_fwd(xre, xim, carry, ar, ai, K)
        dcre_ref[...] = _dot_tn(xre[...].astype(BF16), dyb)
        dcim_ref[...] = -_dot_tn(xim[...].astype(BF16), dyb)
        gre[...] = _dot(dyb, cre_ref[...])
        gim[...] = -_dot(dyb, cim_ref[...])

        a8r = jnp.broadcast_to(ar, (SCAN_LANES, PB))
        a8i = jnp.broadcast_to(ai, (SCAN_LANES, PB))
        na8i = -a8i

        def local(s, c):
            k = K - 2 - s
            nr, ni = c
            rows = _seg_rows(k)
            tr = gre[rows, :] + a8r * nr + a8i * ni
            ti = gim[rows, :] + a8r * ni - a8i * nr
            gre[rows, :] = tr
            gim[rows, :] = ti
            return tr, ti

        last = _seg_rows(K - 1)
        fr, fi = lax.fori_loop(0, K - 1, local, (gre[last, :], gim[last, :]), unroll=SCAN_UNROLL)
        akr, aki = _cpow(ar, -ai, K)
        cr = jnp.zeros((1, PB), F32)
        ci = jnp.zeros((1, PB), F32)
        carry_b[pl.ds(SCAN_LANES - 1, 1), :] = cr
        carry_b[pl.ds(2 * SCAN_LANES - 1, 1), :] = ci
        for j in range(SCAN_LANES - 2, -1, -1):
            tr, ti = _cmul(akr, aki, cr, ci)
            cr = fr[j + 1:j + 2, :] + tr
            ci = fi[j + 1:j + 2, :] + ti
            carry_b[pl.ds(j, 1), :] = cr
            carry_b[pl.ds(SCAN_LANES + j, 1), :] = ci
        cbr = carry_b[pl.ds(0, SCAN_LANES), :]
        cbi = carry_b[pl.ds(SCAN_LANES, SCAN_LANES), :]

        def fix_rows(rows, c, xr, xi):
            pr, pi, dar, dai = c
            tr, ti = _cmul(pr, pi, cbr, cbi)
            g_r = gre[rows, :] + tr
            g_i = gim[rows, :] + ti
            gre[rows, :] = g_r
            gim[rows, :] = g_i
            dar = dar + g_r * xr + g_i * xi
            dai = dai + g_i * xr - g_r * xi
            nr, ni = _cmul(pr, pi, a8r, na8i)
            return nr, ni, dar, dai

        def fix(s, c):
            k = K - 1 - s
            prev = _seg_rows(k - 1)
            return fix_rows(_seg_rows(k), c, xre[prev, :], xim[prev, :])

        z8 = jnp.zeros((SCAN_LANES, PB), F32)
        c = lax.fori_loop(0, K - 1, fix, (a8r, na8i, z8, z8), unroll=SCAN_UNROLL)
        _, _, dar, dai = fix_rows(_seg_rows(0), c, carry[pl.ds(0, SCAN_LANES), :], carry[pl.ds(SCAN_LANES, SCAN_LANES), :])
        dar_ref[...] = _rowsum(dar)
        dai_ref[...] = _rowsum(dai)
        grb = gre[...].astype(BF16)
        gib = gim[...].astype(BF16)
        dbre_ref[...] = _dot_tn(ub, grb)
        dbim_ref[...] = _dot_tn(ub, gib)
        du_ref[...] = _dot_nt(grb, bre_ref[...]) + _dot_nt(gib, bim_ref[...]) + d_ref[...] * dyv
        dd_ref[...] = _rowsum(dyv * uv)

    ucol = pl.BlockSpec((S, UB), lambda i: (0, i))
    pvec = pl.BlockSpec((None, 1, PB), lambda i: (i, 0, 0))
    bmat = pl.BlockSpec((None, UB, PB), lambda i: (i, 0, 0))
    cmat = pl.BlockSpec((None, PB, UB), lambda i: (i, 0, 0))
    dvec = pl.BlockSpec((1, UB), lambda i: (0, i))
    return pl.pallas_call(
        body, name=name, grid=(nblk,),
        in_specs=[ucol, ucol, pvec, pvec, bmat, bmat, bmat, bmat, dvec],
        out_specs=[ucol, bmat, bmat, cmat, cmat, pvec, pvec, dvec],
        out_shape=[jax.ShapeDtypeStruct((S, D), F32),
                   jax.ShapeDtypeStruct((nblk, UB, PB), F32), jax.ShapeDtypeStruct((nblk, UB, PB), F32),
                   jax.ShapeDtypeStruct((nblk, PB, UB), F32), jax.ShapeDtypeStruct((nblk, PB, UB), F32),
                   jax.ShapeDtypeStruct((nblk, 1, PB), F32), jax.ShapeDtypeStruct((nblk, 1, PB), F32),
                   jax.ShapeDtypeStruct((1, D), F32)],
        scratch_shapes=[pltpu.VMEM((S, PB), F32), pltpu.VMEM((S, PB), F32), pltpu.VMEM((S, PB), F32),
                        pltpu.VMEM((S, PB), F32), pltpu.VMEM((2 * SCAN_LANES, PB), F32),
                        pltpu.VMEM((2 * SCAN_LANES, PB), F32)],
        compiler_params=_cp(("arbitrary",)),
    )(u, dy, ar, ai, bre, bim, cre, cim, dskip)


def s5_glu_fwd(y, wglu, x, vec, name, tm=512):
    S, D = x.shape
    NJ = wglu.shape[-1]

    def body(y_ref, w_ref, x_ref, vec_ref, out_ref, f_ref):
        g = _gelu(y_ref[...]).astype(BF16)
        f = jnp.concatenate([_dot(g, w_ref[j]) * _sigmoid(_dot(g, w_ref[2 + j])) for j in range(2)], axis=1)
        f_ref[...] = f
        out_ref[...] = _post_fwd(x_ref[...], f, vec_ref, 1.0)

    row = pl.BlockSpec((tm, D), lambda i: (i, 0))
    return pl.pallas_call(
        body, name=name, grid=(S // tm,),
        in_specs=[row, pl.BlockSpec((N_CHIPS, D, NJ), lambda i: (0, 0, 0)), row, pl.BlockSpec((8, D), lambda i: (0, 0))],
        out_specs=[row, row],
        out_shape=[jax.ShapeDtypeStruct((S, D), F32), jax.ShapeDtypeStruct((S, D), F32)],
        compiler_params=_cp(("arbitrary",)),
    )(y, wglu, x, vec)


def s5_glu_bwd(dout, f, y, vec, wglu, name, tm=512):
    S, D = dout.shape
    NJ = wglu.shape[-1]

    def body(dout_ref, f_ref, y_ref, vec_ref, w_ref, dy_ref, dab_ref, g_ref, acc_ref):
        @pl.when(pl.program_id(0) == 0)
        def _():
            acc_ref[...] = jnp.zeros_like(acc_ref)

        df, dgate, dgpost = _post_bwd(dout_ref[...], f_ref[...], vec_ref, 1.0)
        yv = y_ref[...]
        g = _gelu(yv).astype(BF16)
        g_ref[...] = g
        dg = jnp.zeros((tm, D), F32)
        for j in range(2):
            a = _dot(g, w_ref[j])
            sig = _sigmoid(_dot(g, w_ref[2 + j]))
            dfj = df[:, j * NJ:(j + 1) * NJ]
            da = (dfj * sig).astype(BF16)
            db = (dfj * a * sig * (1.0 - sig)).astype(BF16)
            dab_ref[:, j * NJ:(j + 1) * NJ] = da
            dab_ref[:, (2 + j) * NJ:(3 + j) * NJ] = db
            dg = dg + _dot_nt(da, w_ref[j]) + _dot_nt(db, w_ref[2 + j])
        dy_ref[...] = dg * _gelu_grad(yv)
        _acc_add(acc_ref, A_GATE, dgate)
        _acc_add(acc_ref, A_GPOST, dgpost)

    row = pl.BlockSpec((tm, D), lambda i: (i, 0))
    const = pl.BlockSpec((8, D), lambda i: (0, 0))
    return pl.pallas_call(
        body, name=name, grid=(S // tm,),
        in_specs=[row, row, row, const, pl.BlockSpec((N_CHIPS, D, NJ), lambda i: (0, 0, 0))],
        out_specs=[row, pl.BlockSpec((tm, N_CHIPS * NJ), lambda i: (i, 0)), row, const],
        out_shape=[jax.ShapeDtypeStruct((S, D), F32), jax.ShapeDtypeStruct((S, N_CHIPS * NJ), BF16),
                   jax.ShapeDtypeStruct((S, D), BF16), jax.ShapeDtypeStruct((8, D), F32)],
        compiler_params=_cp(("arbitrary",)),
    )(dout, f, y, vec, wglu)


def loss_head(y, target, name, tm=512):
    S, D = y.shape

    def body(y_ref, t_ref, d_ref, l_ref):
        @pl.when(pl.program_id(0) == 0)
        def _():
            l_ref[...] = jnp.zeros_like(l_ref)

        err = y_ref[...] - t_ref[...]
        d_ref[...] = err * (1.0 / D)
        l_ref[...] += jnp.sum(_rowsum(err * err), axis=1, keepdims=True)

    row = pl.BlockSpec((tm, D), lambda i: (i, 0))
    return pl.pallas_call(
        body, name=name, grid=(S // tm,),
        in_specs=[row, row],
        out_specs=[row, pl.BlockSpec((1, 1), lambda i: (0, 0))],
        out_shape=[jax.ShapeDtypeStruct((S, D), F32), jax.ShapeDtypeStruct((1, 1), F32)],
        compiler_params=_cp(("arbitrary",)),
    )(y, target)


def zero_after(dep, name, shape=(8, 128)):
    def body(dep_ref, o_ref):
        o_ref[...] = jnp.zeros_like(o_ref)

    return pl.pallas_call(body, name=name, in_specs=[ANY], out_specs=pl.BlockSpec(memory_space=pltpu.VMEM),
                          out_shape=jax.ShapeDtypeStruct(shape, F32), compiler_params=_cp())(dep)


def ada_mod(c_all, ada_w, ada_b_shard, name, tn=768, dep=None):
    B, D = c_all.shape
    L, _, NS = ada_w.shape
    extra = [] if dep is None else [dep]

    def body(c_ref, w_ref, b_ref, *rest):
        cv = c_ref[...]
        cond = (cv * _sigmoid(cv)).astype(BF16)
        rest[-1][...] = _dot(cond, w_ref[...].astype(BF16)) + b_ref[...]

    return pl.pallas_call(
        body, name=name, grid=(L, NS // tn),
        in_specs=[pl.BlockSpec((B, D), lambda l, j: (0, 0)), pl.BlockSpec((None, D, tn), lambda l, j: (l, 0, j)),
                  pl.BlockSpec((None, 1, tn), lambda l, j: (l, 0, j))] + [ANY] * len(extra),
        out_specs=pl.BlockSpec((None, B, tn), lambda l, j: (l, 0, j)),
        out_shape=jax.ShapeDtypeStruct((L, B, NS), F32),
        compiler_params=_cp(("arbitrary", "arbitrary")),
    )(c_all, ada_w, ada_b_shard, *extra)


def ada_grad(c_all, dmod, name, tn=768):
    B, D = c_all.shape
    L, _, NS = dmod.shape

    def body(c_ref, d_ref, o_ref):
        cv = c_ref[...]
        cond = (cv * _sigmoid(cv)).astype(BF16)
        o_ref[...] = _dot_tn(cond, d_ref[...].astype(BF16))

    return pl.pallas_call(
        body, name=name, grid=(L, NS // tn),
        in_specs=[pl.BlockSpec((B, D), lambda l, j: (0, 0)), pl.BlockSpec((None, B, tn), lambda l, j: (l, 0, j))],
        out_specs=pl.BlockSpec((None, D, tn), lambda l, j: (l, 0, j)),
        out_shape=jax.ShapeDtypeStruct((L, D, NS), F32),
        compiler_params=_cp(("arbitrary", "arbitrary")),
    )(c_all, dmod)


def sum_gathered(land, own, me_arr, name):
    n, R, C = land.shape

    def body(me_ref, land_ref, own_ref, o_ref):
        acc = jnp.zeros(o_ref.shape, F32)
        for i in range(n):
            acc = acc + jnp.where(me_ref[0] == i, own_ref[...], land_ref[i]).astype(F32)
        o_ref[...] = acc

    tr = _row_tile(R, 64)
    grid_spec = pltpu.PrefetchScalarGridSpec(
        num_scalar_prefetch=1, grid=(R // tr,),
        in_specs=[pl.BlockSpec((n, tr, C), lambda i, me: (0, i, 0)), pl.BlockSpec((tr, C), lambda i, me: (i, 0))],
        out_specs=pl.BlockSpec((tr, C), lambda i, me: (i, 0)))
    return pl.pallas_call(
        body, name=name, grid_spec=grid_spec,
        out_shape=jax.ShapeDtypeStruct((R, C), F32),
        compiler_params=_cp(("arbitrary",)),
    )(me_arr, land, own)


def _row_tile(R, cap=512):
    if R <= cap:
        return R
    for cand in (512, 384, 352, 256, 128, 64, 32, 16, 8):
        if cand <= cap and R % cand == 0:
            return cand
    return R


def adamw(w, g, m, v, name, emit_grad=False, dep=None):
    R, C = w.shape
    tr = _row_tile(R, 256 if C > 1024 else 512)
    bc1 = 1.0 - ADAM_B1 ** ADAM_STEP
    bc2 = 1.0 - ADAM_B2 ** ADAM_STEP
    n_out = 4 if emit_grad else 3
    extra = [] if dep is None else [dep]

    def body(w_ref, g_ref, m_ref, v_ref, *outs):
        d_ref, nm_ref, nv_ref = outs[-3:]
        gv = g_ref[...]
        if emit_grad:
            outs[-4][...] = gv
        nm = ADAM_B1 * m_ref[...] + (1.0 - ADAM_B1) * gv
        nv = ADAM_B2 * v_ref[...] + (1.0 - ADAM_B2) * (gv * gv)
        nm_ref[...] = nm
        nv_ref[...] = nv
        d_ref[...] = -ADAM_LR * ((nm / bc1) / (jnp.sqrt(nv / bc2) + ADAM_EPS) + ADAM_WD * w_ref[...])

    blk = pl.BlockSpec((tr, C), lambda i: (i, 0))
    sd = jax.ShapeDtypeStruct((R, C), F32)
    return pl.pallas_call(
        body, name=name, grid=(R // tr,),
        in_specs=[blk, blk, blk, blk] + [ANY] * len(extra), out_specs=[blk] * n_out, out_shape=[sd] * n_out,
        compiler_params=_cp(("arbitrary",)),
    )(w, g, m, v, *extra)


def _as2d(a):
    if a.ndim == 1:
        return a.reshape(1, -1)
    return a.reshape(-1, a.shape[-1])


SMALL_PARAM_ELEMS = 1 << 18


def adamw_nd(w, g, m, v, name, emit_grad=False, dep=None):
    extra = [] if dep is None else [dep]
    if w.size <= SMALL_PARAM_ELEMS and not emit_grad:
        bc1 = 1.0 - ADAM_B1 ** ADAM_STEP
        bc2 = 1.0 - ADAM_B2 ** ADAM_STEP

        def body(w_ref, g_ref, m_ref, v_ref, *rest):
            d_ref, nm_ref, nv_ref = rest[-3:]
            gv = g_ref[...]
            nm = ADAM_B1 * m_ref[...] + (1.0 - ADAM_B1) * gv
            nv = ADAM_B2 * v_ref[...] + (1.0 - ADAM_B2) * (gv * gv)
            nm_ref[...] = nm
            nv_ref[...] = nv
            d_ref[...] = -ADAM_LR * ((nm / bc1) / (jnp.sqrt(nv / bc2) + ADAM_EPS) + ADAM_WD * w_ref[...])

        sd = jax.ShapeDtypeStruct(w.shape, F32)
        whole = pl.BlockSpec(memory_space=pltpu.VMEM)
        return tuple(pl.pallas_call(body, name=name, in_specs=[whole] * 4 + [ANY] * len(extra), out_specs=[whole] * 3,
                                    out_shape=[sd, sd, sd], compiler_params=_cp())(w, g.reshape(w.shape), m, v, *extra))
    outs = adamw(_as2d(w), _as2d(g.reshape(w.shape)), _as2d(m), _as2d(v), name, emit_grad=emit_grad, dep=dep)
    return tuple(o.reshape(w.shape) for o in outs)


def _place():
    x, y, c = lax.axis_index("x"), lax.axis_index("y"), lax.axis_index("c")
    chips = [(1 - x, y), (x, 1 - y), (1 - x, 1 - y)]
    return x, y, c, chips


def allgather_small(xs, name):
    m_per, n = xs.shape

    def body(x_ref, out_ref, send_sems, recv_sems, local_sem):
        x, y, c, chips = _place()
        me, sibling = (x, y, c), (x, y, 1 - c)

        def rows(px, py, pc):
            return out_ref.at[pl.ds((4 * px + 2 * py + pc) * m_per, m_per), :]

        def copy(k, block, to, src=None):
            return pltpu.make_async_remote_copy(
                src_ref=rows(*block) if src is None else src, dst_ref=rows(*block),
                send_sem=send_sems.at[k], recv_sem=recv_sems.at[k], device_id=to, device_id_type=MESH_T)

        mine = pltpu.make_async_copy(x_ref, rows(*me), local_sem)
        mine.start()
        first = [copy(0, me, sibling, src=x_ref)]
        first += [copy(1 + j, me, (*chip, c), src=x_ref) for j, chip in enumerate(chips)]
        for cp in first:
            cp.start()
        passed = [copy(4 + j, (*chip, c), sibling) for j, chip in enumerate(chips)]
        for j, chip in enumerate(chips):
            copy(1 + j, (*chip, c), me).wait_recv()
            passed[j].start()
        copy(0, sibling, me).wait_recv()
        for j, chip in enumerate(chips):
            copy(4 + j, (*chip, 1 - c), me).wait_recv()
        for cp in first + passed:
            cp.wait_send()
        mine.wait()

    return pl.pallas_call(
        body, name=name,
        out_shape=jax.ShapeDtypeStruct((N_DEV * m_per, n), xs.dtype),
        in_specs=[pl.BlockSpec(memory_space=pltpu.VMEM)],
        out_specs=pl.BlockSpec(memory_space=pltpu.VMEM),
        scratch_shapes=[pltpu.SemaphoreType.DMA((7,)), pltpu.SemaphoreType.DMA((7,)), pltpu.SemaphoreType.DMA],
        compiler_params=_cp(),
    )(xs)


def _half_rows(ref, half, rh):
    idx = (slice(None),) * (len(ref.shape) - 2) + (pl.ds(pl.multiple_of(half * rh, 16), rh), slice(None))
    return ref.at[idx]


def pair_sum(g, recv, cidx, name):
    L, NS, R, C = g.shape
    rh = R // 2
    tr = _row_tile(rh, 256 if C > 1024 else 512)
    nt = rh // tr

    def body(c_ref, g_ref, r_ref, o_ref):
        o_ref[...] = (g_ref[...].astype(F32) + r_ref[...].astype(F32)).astype(BF16)

    grid_spec = pltpu.PrefetchScalarGridSpec(
        num_scalar_prefetch=1, grid=(L, NS, nt),
        in_specs=[pl.BlockSpec((None, None, tr, C), lambda l, s, t, c: (l, s, c[0] * nt + t, 0)),
                  pl.BlockSpec((None, None, tr, C), lambda l, s, t, c: (l, s, t, 0))],
        out_specs=pl.BlockSpec((None, None, tr, C), lambda l, s, t, c: (l, s, t, 0)))
    return pl.pallas_call(
        body, name=name, grid_spec=grid_spec,
        out_shape=jax.ShapeDtypeStruct((L, NS, rh, C), BF16),
        compiler_params=_cp(("arbitrary", "arbitrary", "arbitrary")),
    )(cidx, g, recv)


def chip_sum(part, recv, chip_c, name, dest=None, slot=0, n_slots=1):
    _, NS, RH, C = part.shape
    tr = _row_tile(RH, 256 if C > 1024 else 512)
    nt = RH // tr

    def body(cc_ref, p_ref, r_ref, *rest):
        o_ref = rest[-1]
        acc = p_ref[...].astype(F32)
        for k in range(3):
            acc = acc + r_ref[k].astype(F32)
        o_ref[...] = acc

    in_specs = [pl.BlockSpec((None, None, tr, C), lambda t, cc: (0, cc[0], t, 0)),
                pl.BlockSpec((3, None, tr, C), lambda t, cc: (0, 0, t, 0))]
    args = [chip_c, part, recv]
    aliases = {}
    if dest is not None:
        in_specs.append(ANY)
        args.append(dest)
        aliases = {3: 0}
    grid_spec = pltpu.PrefetchScalarGridSpec(
        num_scalar_prefetch=1, grid=(nt,), in_specs=in_specs,
        out_specs=pl.BlockSpec((None, tr, C), lambda t, cc: (slot, cc[1] * nt + t, 0)))
    return pl.pallas_call(
        body, name=name, grid_spec=grid_spec,
        out_shape=jax.ShapeDtypeStruct((n_slots, 2 * RH, C), F32),
        input_output_aliases=aliases,
        compiler_params=_cp(("arbitrary",)),
    )(*args)


HBM_SPEC = pl.BlockSpec(memory_space=pltpu.HBM)
SEM_SPEC = pl.BlockSpec(memory_space=pltpu.SEMAPHORE)
DATAFLOW_EFFECT = pltpu.SideEffectType.DATAFLOW_SIDE_EFFECTING


def split_start(bufs, copies_fn, n_copies, name, dep=None, token_shape=(8, 128)):
    nb = len(bufs)
    extra = [] if dep is None else [dep]

    def body(*refs):
        ne = len(extra)
        send_sems, recv_sems = refs[nb + ne], refs[nb + ne + 1]
        token = refs[-1]
        for cp in copies_fn(refs[:nb], send_sems, recv_sems):
            cp.start()
        token[...] = jnp.zeros_like(token)

    outs = pl.pallas_call(
        body, name=name,
        out_shape=(pltpu.SemaphoreType.DMA((n_copies,)), pltpu.SemaphoreType.DMA((n_copies,)),
                   *[pltpu.HBM(b.shape, b.dtype) for b in bufs], jax.ShapeDtypeStruct(token_shape, F32)),
        in_specs=[HBM_SPEC] * nb + [ANY] * len(extra),
        out_specs=(SEM_SPEC, SEM_SPEC, *[HBM_SPEC] * nb, pl.BlockSpec(memory_space=pltpu.VMEM)),
        input_output_aliases={i: 2 + i for i in range(nb)},
        compiler_params=pltpu.CompilerParams(has_side_effects=DATAFLOW_EFFECT),
    )(*[pltpu.with_memory_space_constraint(b, pltpu.HBM) for b in bufs], *extra)
    return outs[0], outs[1], list(outs[2:2 + nb]), outs[-1]


def split_wait(send_sems, recv_sems, bufs, after, copies_fn, name):
    nb = len(bufs)

    def body(*refs):
        for cp in copies_fn(refs[:nb], refs[nb], refs[nb + 1]):
            cp.wait_send()
            cp.wait_recv()

    outs = pl.pallas_call(
        body, name=name,
        out_shape=tuple(pltpu.HBM(b.shape, b.dtype) for b in bufs),
        in_specs=[HBM_SPEC] * nb + [SEM_SPEC, SEM_SPEC, ANY],
        out_specs=tuple([HBM_SPEC] * nb),
        input_output_aliases={i: i for i in range(nb)},
        compiler_params=pltpu.CompilerParams(has_side_effects=DATAFLOW_EFFECT),
    )(*bufs, send_sems, recv_sems, after)
    return list(outs)


def chip_exchange_copies(refs, send_sems, recv_sems):
    n = len(refs) // 2
    x, y, c, chips = _place()
    cps = []
    for a in range(n):
        for k, (cx, cy) in enumerate(chips):
            cps.append(pltpu.make_async_remote_copy(
                src_ref=refs[a].at[:, 2 * cx + cy], dst_ref=refs[n + a].at[k],
                send_sem=send_sems.at[a * 3 + k], recv_sem=recv_sems.at[a * 3 + k],
                device_id=(cx, cy, c), device_id_type=MESH_T))
    return cps


def fill_copies(refs, send_sems, recv_sems):
    x, y, c, _ = _place()
    cps = []
    for a in range(len(refs)):
        rh = refs[a].shape[-2] // 2
        cps.append(pltpu.make_async_remote_copy(
            src_ref=_half_rows(refs[a], c, rh), dst_ref=_half_rows(refs[a], c, rh),
            send_sem=send_sems.at[a], recv_sem=recv_sems.at[a], device_id=(x, y, 1 - c), device_id_type=MESH_T))
    return cps


def allgather_direct_copies(refs, send_sems, recv_sems):
    x, y, c, _ = _place()
    me = 4 * x + 2 * y + c
    cps = []
    for r in range(1, N_DEV):
        fx, fy, fc = (r >> 2) & 1, (r >> 1) & 1, r & 1
        peer = (1 - x if fx else x, 1 - y if fy else y, 1 - c if fc else c)
        cps.append(pltpu.make_async_remote_copy(
            src_ref=refs[0], dst_ref=refs[1].at[me], send_sem=send_sems.at[r - 1], recv_sem=recv_sems.at[r - 1],
            device_id=peer, device_id_type=MESH_T))
    return cps


def pair_exchange_copies(refs, send_sems, recv_sems):
    n = len(refs) // 2
    x, y, c, _ = _place()
    return [pltpu.make_async_remote_copy(
        src_ref=_half_rows(refs[a], 1 - c, refs[a].shape[-2] // 2), dst_ref=refs[n + a],
        send_sem=send_sems.at[a], recv_sem=recv_sems.at[a], device_id=(x, y, 1 - c), device_id_type=MESH_T)
        for a in range(n)]


def weight_ici_copies(refs, send_sems, recv_sems):
    x, y, c, chips = _place()
    cps = []
    for a in range(len(refs)):
        for k, (cx, cy) in enumerate(chips):
            cps.append(pltpu.make_async_remote_copy(
                src_ref=refs[a].at[2 * x + y, c], dst_ref=refs[a].at[2 * x + y, c],
                send_sem=send_sems.at[a * 3 + k], recv_sem=recv_sems.at[a * 3 + k],
                device_id=(cx, cy, c), device_id_type=MESH_T))
    return cps


def weight_d2d_copies(refs, send_sems, recv_sems):
    x, y, c, chips = _place()
    cps = []
    for a in range(len(refs)):
        for k, (cx, cy) in enumerate(chips):
            cps.append(pltpu.make_async_remote_copy(
                src_ref=refs[a].at[2 * cx + cy, c], dst_ref=refs[a].at[2 * cx + cy, c],
                send_sem=send_sems.at[a * 3 + k], recv_sem=recv_sems.at[a * 3 + k],
                device_id=(x, y, 1 - c), device_id_type=MESH_T))
    return cps


def cast_place(w, lead, chip_arr, name, dep=None):
    R, C = w.shape[-2:]
    tr = _row_tile(R, 256 if C > 1024 else 512)

    def body(chip_ref, w_ref, *rest):
        rest[-1][...] = w_ref[...].astype(BF16)

    in_specs = [pl.BlockSpec((None,) * len(lead) + (tr, C), lambda t, ch: (*lead, t, 0))]
    args = [chip_arr, w]
    if dep is not None:
        in_specs.append(ANY)
        args.append(dep)
    grid_spec = pltpu.PrefetchScalarGridSpec(
        num_scalar_prefetch=1, grid=(R // tr,), in_specs=in_specs,
        out_specs=pl.BlockSpec((None, tr, C), lambda t, ch: (ch[0], t, 0)))
    return pl.pallas_call(
        body, name=name, grid_spec=grid_spec,
        out_shape=jax.ShapeDtypeStruct((N_CHIPS, R, C), BF16),
        compiler_params=_cp(("arbitrary",)),
    )(*args)


def _s5_prepare(lam_re, lam_im, log_dt, b_re, b_im, c_re, c_im, groups_per_block):
    G, P = lam_re.shape
    N = b_re.shape[-1]
    gb = groups_per_block
    nblk = G // gb
    dt = jnp.exp(log_dt)[:, None]
    e = jnp.exp(lam_re * dt)
    a_re = e * jnp.cos(lam_im * dt)
    a_im = e * jnp.sin(lam_im * dt)
    n2 = lam_re * lam_re + lam_im * lam_im
    co_re = ((a_re - 1.0) * lam_re + a_im * lam_im) / n2
    co_im = (a_im * lam_re - (a_re - 1.0) * lam_im) / n2
    bb_re = co_re[..., None] * b_re - co_im[..., None] * b_im
    bb_im = co_re[..., None] * b_im + co_im[..., None] * b_re
    eye = jnp.eye(gb, dtype=F32)

    def blockdiag_np(m):
        m = m.reshape(nblk, gb, N, P)
        return jnp.einsum('bgnp,gh->bgnhp', m, eye).reshape(nblk, gb * N, gb * P)

    b_np_re = jnp.swapaxes(bb_re, 1, 2)
    b_np_im = jnp.swapaxes(bb_im, 1, 2)
    return (a_re.reshape(nblk, 1, gb * P), a_im.reshape(nblk, 1, gb * P),
            blockdiag_np(b_np_re), blockdiag_np(b_np_im), blockdiag_np(c_re), blockdiag_np(c_im))


def _to_scan_order(a):
    S, D = a.shape
    return a.reshape(SCAN_LANES, S // SCAN_LANES, D).transpose(1, 0, 2).reshape(S, D)


def _from_scan_order(a):
    S, D = a.shape
    return a.reshape(S // SCAN_LANES, SCAN_LANES, D).transpose(1, 0, 2).reshape(S, D)


def _pad_rows(a, mult=8):
    r = (-a.shape[0]) % mult
    if r:
        a = jnp.concatenate([a, jnp.zeros((r, a.shape[1]), a.dtype)], axis=0)
    return a


def _pack_rows(arrs, width):
    parts, offs, o = [], [], 0
    for a in arrs:
        flat = a.reshape(-1)
        r = (-flat.shape[0]) % (16 * width)
        if r:
            flat = jnp.concatenate([flat, jnp.zeros((r,), flat.dtype)])
        p = flat.reshape(-1, width)
        parts.append(p)
        offs.append((o, a.shape, a.size))
        o += p.shape[0]
    if o % 64:
        parts.append(jnp.zeros((64 - o % 64, width), parts[0].dtype))
    return jnp.concatenate(parts, axis=0), offs


def _unpack_rows(packed, offs):
    outs = []
    for o, shape, size in offs:
        rows = -(-size // packed.shape[1])
        outs.append(packed[o:o + rows].reshape(-1)[:size].reshape(shape))
    return outs


def kernel(x, c, ada_w, ada_b, norm_pre, norm_post, ffn_w_in, ffn_w_out, ab_w_in, pool_w, pool_scale, sgu_ln_g, sgu_ln_b, sgu_w, sgu_b, ab_w_out, ssm_w_in, ssm_lam_re, ssm_lam_im, ssm_b_re, ssm_b_im, ssm_c_re, ssm_c_im, ssm_d, ssm_log_dt, ssm_w_glu, loss_target, m_ada_w, m_ada_b, m_norm_pre, m_norm_post, m_ffn_w_in, m_ffn_w_out, m_ab_w_in, m_pool_w, m_pool_scale, m_sgu_ln_g, m_sgu_ln_b, m_sgu_w, m_sgu_b, m_ab_w_out, m_ssm_w_in, m_ssm_lam_re, m_ssm_lam_im, m_ssm_b_re, m_ssm_b_im, m_ssm_c_re, m_ssm_c_im, m_ssm_d, m_ssm_log_dt, m_ssm_w_glu, v_ada_w, v_ada_b, v_norm_pre, v_norm_post, v_ffn_w_in, v_ffn_w_out, v_ab_w_in, v_pool_w, v_pool_scale, v_sgu_ln_g, v_sgu_ln_b, v_sgu_w, v_sgu_b, v_ab_w_out, v_ssm_w_in, v_ssm_lam_re, v_ssm_lam_im, v_ssm_b_re, v_ssm_b_im, v_ssm_c_re, v_ssm_c_im, v_ssm_d, v_ssm_log_dt, v_ssm_w_glu):
    weights = dict(ada_w=ada_w, ada_b=ada_b, norm_pre=norm_pre, norm_post=norm_post, ffn_w_in=ffn_w_in,
                   ffn_w_out=ffn_w_out, ab_w_in=ab_w_in, pool_w=pool_w, pool_scale=pool_scale, sgu_ln_g=sgu_ln_g,
                   sgu_ln_b=sgu_ln_b, sgu_w=sgu_w, sgu_b=sgu_b, ab_w_out=ab_w_out, ssm_w_in=ssm_w_in,
                   ssm_lam_re=ssm_lam_re, ssm_lam_im=ssm_lam_im, ssm_b_re=ssm_b_re, ssm_b_im=ssm_b_im,
                   ssm_c_re=ssm_c_re, ssm_c_im=ssm_c_im, ssm_d=ssm_d, ssm_log_dt=ssm_log_dt, ssm_w_glu=ssm_w_glu)
    m_in = dict(ada_w=m_ada_w, ada_b=m_ada_b, norm_pre=m_norm_pre, norm_post=m_norm_post, ffn_w_in=m_ffn_w_in,
                ffn_w_out=m_ffn_w_out, ab_w_in=m_ab_w_in, pool_w=m_pool_w, pool_scale=m_pool_scale,
                sgu_ln_g=m_sgu_ln_g, sgu_ln_b=m_sgu_ln_b, sgu_w=m_sgu_w, sgu_b=m_sgu_b, ab_w_out=m_ab_w_out,
                ssm_w_in=m_ssm_w_in, ssm_lam_re=m_ssm_lam_re, ssm_lam_im=m_ssm_lam_im, ssm_b_re=m_ssm_b_re,
                ssm_b_im=m_ssm_b_im, ssm_c_re=m_ssm_c_re, ssm_c_im=m_ssm_c_im, ssm_d=m_ssm_d,
                ssm_log_dt=m_ssm_log_dt, ssm_w_glu=m_ssm_w_glu)
    v_in = dict(ada_w=v_ada_w, ada_b=v_ada_b, norm_pre=v_norm_pre, norm_post=v_norm_post, ffn_w_in=v_ffn_w_in,
                ffn_w_out=v_ffn_w_out, ab_w_in=v_ab_w_in, pool_w=v_pool_w, pool_scale=v_pool_scale,
                sgu_ln_g=v_sgu_ln_g, sgu_ln_b=v_sgu_ln_b, sgu_w=v_sgu_w, sgu_b=v_sgu_b, ab_w_out=v_ab_w_out,
                ssm_w_in=v_ssm_w_in, ssm_lam_re=v_ssm_lam_re, ssm_lam_im=v_ssm_lam_im, ssm_b_re=v_ssm_b_re,
                ssm_b_im=v_ssm_b_im, ssm_c_re=v_ssm_c_re, ssm_c_im=v_ssm_c_im, ssm_d=v_ssm_d,
                ssm_log_dt=v_ssm_log_dt, ssm_w_glu=v_ssm_w_glu)
    names = list(weights.keys())

    xi, yi, ci = lax.axis_index("x"), lax.axis_index("y"), lax.axis_index("c")
    chip = 2 * xi + yi
    me = 4 * xi + 2 * yi + ci
    S, D = x.shape[1], x.shape[2]
    L = ada_w.shape[0]
    NSUB = norm_pre.shape[1]
    DS = norm_pre.shape[2]
    FS = ffn_w_in.shape[-1]
    FR = ffn_w_out.shape[-2]
    x0 = x[0]
    target = loss_target[0]

    small_parts = [_pad_rows(p) for p in (c.reshape(D // DS, DS), norm_pre.reshape(L * NSUB, DS),
                                          norm_post.reshape(L * NSUB, DS), ssm_d.reshape(1, DS))]
    small_offs = [0]
    for p in small_parts:
        small_offs.append(small_offs[-1] + p.shape[0])
    small_all = allgather_small(jnp.concatenate(small_parts, axis=0), "ag_small").reshape(N_DEV, -1, DS)
    c_all = small_all[:, :D // DS].reshape(N_DEV, D)
    per_chip = small_all[0::2]
    o = small_offs[1]
    g_pre_full = jnp.moveaxis(per_chip[:, o:o + L * NSUB], 0, 1).reshape(L, NSUB, D)
    o = small_offs[2]
    g_post_full = jnp.moveaxis(per_chip[:, o:o + L * NSUB], 0, 1).reshape(L, NSUB, D)
    o = small_offs[3]
    d_full = jnp.moveaxis(per_chip[:, o:o + 1], 0, 1).reshape(1, D)

    pieces = []
    for l in range(L):
        pieces.append((f"ffn_{l}_0", [(ffn_w_in, (l, 0)), (ffn_w_out, (l, 0))]))
        if l % 2 == 0:
            pieces.append((f"mix_{l}", [(ab_w_in, (l // 2,)), (ab_w_out, (l // 2,))]))
        else:
            pieces.append((f"mix_{l}", [(ssm_w_in, (l // 2,)), (ssm_w_glu, (l // 2,))]))
        pieces.append((f"ffn_{l}_1", [(ffn_w_in, (l, 1)), (ffn_w_out, (l, 1))]))
    tags = [tag for tag, _ in pieces]
    chip_arr = chip.reshape(1).astype(jnp.int32)
    wg_started, wg_passing = {}, {}

    def cast_piece(tag, ws, dep):
        wg_started[tag] = [cast_place(w, lead, chip_arr, f"wg_cast_{tag}_{i}", dep=dep).reshape(
            N_CHIPS, 2, w.shape[-2] // 2, w.shape[-1]) for i, (w, lead) in enumerate(ws)]

    def start_piece(tag, dep):
        bufs = wg_started[tag]
        wg_started[tag] = split_start(bufs, weight_ici_copies, 3 * len(bufs), f"wg_ici_start_{tag}", dep=dep,
                                      token_shape=(8, D))
        return wg_started[tag][3]

    def weights_arrived(i, after):
        ssem, rsem, bufs, _ = wg_started[tags[i]]
        lands = split_wait(ssem, rsem, bufs, after, weight_ici_copies, f"wg_ici_wait_{tags[i]}")
        wg_passing[i] = split_start(lands, weight_d2d_copies, 3 * len(lands), f"wg_d2d_start_{tags[i]}")
        return wg_passing[i][3][0, 0]

    def weights_of(i, after):
        ssem, rsem, lands, _ = wg_passing[i]
        return split_wait(ssem, rsem, lands, after, weight_d2d_copies, f"wg_d2d_wait_{tags[i]}")

    NS = ada_w.shape[-1]
    ada_b_shard = lax.dynamic_slice_in_dim(ada_b, chip * NS, NS, axis=1).reshape(L, 1, NS)
    mod_part = ada_mod(c_all, ada_w, ada_b_shard, "ada_mod")
    mod_all = allgather_small(mod_part.reshape(L * N_DEV, NS), "ag_mod").reshape(N_DEV, L, N_DEV, NS)
    mod_mine = lax.dynamic_index_in_dim(mod_all[0::2], me, axis=2, keepdims=False)
    mod = jnp.moveaxis(mod_mine, 0, 1).reshape(L, NSUB, 3, D)
    cast_piece(*pieces[0], mod_all)
    first_token = start_piece(tags[0], mod_all)
    wg_tokens = [first_token]
    for tag, ws in pieces[1:]:
        cast_piece(tag, ws, first_token)
    for tag in tags[1:WG_AHEAD]:
        wg_tokens.append(start_piece(tag, first_token))

    def vec_of(l, s):
        return jnp.concatenate([mod[l, s], g_pre_full[l, s][None], g_post_full[l, s][None],
                                jnp.zeros((3, D), F32)], axis=0)

    GB = 8
    s5_args = (ssm_lam_re[0], ssm_lam_im[0], ssm_log_dt[0], ssm_b_re[0], ssm_b_im[0], ssm_c_re[0], ssm_c_im[0])
    (a_re, a_im, bblk_re, bblk_im, cblk_re, cblk_im), s5_vjp = jax.vjp(lambda *p: _s5_prepare(*p, GB), *s5_args)
    bre16, bim16 = bblk_re.astype(BF16), bblk_im.astype(BF16)
    cre16, cim16 = cblk_re.astype(BF16), cblk_im.astype(BF16)
    cre16_t, cim16_t = jnp.swapaxes(cre16, 1, 2), jnp.swapaxes(cim16, 1, 2)

    pool_w16 = pool_w[0].astype(BF16)
    sgu_bexp = jnp.broadcast_to(sgu_b[0][:, :, None], sgu_w[0].shape)

    saved = {}
    ffn_w = {}
    xcur = x0
    stage = [0]

    def next_weights(after, vec):
        i = stage[0]
        stage[0] += 1
        weights_arrived(i, after)
        if i + WG_AHEAD < len(tags):
            vec = vec + start_piece(tags[i + WG_AHEAD], wg_passing[i][3])
        return weights_of(i, after), vec

    for l in range(L):
        v0 = vec_of(l, 0)
        g, v0p = next_weights(xcur if l else sum(wg_tokens, v0), v0)
        ffn_w[(l, 0)] = (g[0].reshape(N_CHIPS, D, FS), g[1].reshape(N_CHIPS, FR, D))
        out, h, u, f = ffn_fwd(xcur, v0p, *ffn_w[(l, 0)], f"ffn_fwd_{l}_0")
        saved[(l, 0)] = (xcur, v0, h, u, f)
        xcur = out
        v1 = vec_of(l, 1)
        mix_g, v1p = next_weights(xcur, v1)
        if l % 2 == 0:
            abin_g = mix_g[0].reshape(N_CHIPS, D, -1)
            about_g = mix_g[1].reshape(-1, D)
            h, z = pre_matmul(xcur, v1p, abin_g, f"mixa_in_{l}")
            ycat = mixa_core_fwd(z, pool_w16, pool_scale, sgu_ln_g, sgu_ln_b, sgu_w[0], sgu_bexp, f"mixa_core_{l}")
            out, f = matmul_post(ycat, about_g, xcur, v1, f"mixa_out_{l}")
            saved[(l, 1)] = (xcur, v1, h, z, ycat, f)
        else:
            sin_g = mix_g[0].reshape(1, -1, D)
            glu_g = mix_g[1].reshape(N_CHIPS, D, -1)
            h, uu = pre_matmul(xcur, v1p, sin_g, f"s5_in_{l}")
            us = _to_scan_order(uu)
            ys = s5_core_fwd(us, a_re, a_im, bre16, bim16, cre16_t, cim16_t, d_full, f"s5_core_{l}")
            yy = _from_scan_order(ys)
            out, f = s5_glu_fwd(yy, glu_g, xcur, v1, f"s5_glu_{l}")
            saved[(l, 1)] = (xcur, v1, h, us, yy, f)
        xcur = out
        v2 = vec_of(l, 2)
        g, v2p = next_weights(xcur, v2)
        ffn_w[(l, 1)] = (g[0].reshape(N_CHIPS, D, FS), g[1].reshape(N_CHIPS, FR, D))
        out, h, u, f = ffn_fwd(xcur, v2p, *ffn_w[(l, 1)], f"ffn_fwd_{l}_1")
        saved[(l, 2)] = (xcur, v2, h, u, f)
        xcur = out

    dcur, sq = loss_head(xcur, target, "loss_head")
    loss = lax.psum(sq[0, 0], ("x", "y", "c")) * (0.5 / D)

    accs = {}
    small_g = {}
    cidx = ci.reshape(1).astype(jnp.int32)
    chip_c = jnp.stack([chip, ci]).astype(jnp.int32)
    rs_open, rs_pending = [], []

    pin = [zero_after(loss.reshape(1, 1), "after_loss", (8, D))]
    last_start = [pin[0]]

    def pin_behind(token):
        pin.append(token)
        last_start[0] = token

    def pinned(v):
        for token in pin:
            v = v + (token if token.shape == v.shape else token[0, 0])
        pin.clear()
        return v

    def rs_begin(tag, items):
        arrs = [it[0] for it in items]
        lands = [lax.empty(a.shape[:-2] + (a.shape[-2] // 2, a.shape[-1]), BF16) for a in arrs]
        started = split_start(arrs + lands, pair_exchange_copies, len(arrs), f"rs_pair_start_{tag}",
                              token_shape=(8, D))
        rs_open.append((tag, items, started))
        pin_behind(started[3])

    def rs_advance(after):
        while rs_open:
            tag, items, (ssem, rsem, bufs, _) = rs_open.pop(0)
            n = len(items)
            bufs = split_wait(ssem, rsem, bufs, after, pair_exchange_copies, f"rs_pair_wait_{tag}")
            parts = [pair_sum(a, r, cidx, f"rs_pair_sum_{tag}_{i}") for i, (a, r) in enumerate(zip(bufs[:n], bufs[n:]))]
            lands = [lax.empty((3, 1) + p.shape[2:], BF16) for p in parts]
            started = split_start(parts + lands, chip_exchange_copies, 3 * n, f"rs_chip_start_{tag}",
                                  token_shape=(8, D))
            rs_pending.append((tag, items, started))
            pin_behind(started[3])

    def ffn_back(l, s, k, dcur, before_wgrads=None):
        xin, vv, h, u, f = saved[(l, s)]
        vv = pinned(vv)
        dx, df, du, act, acc = ffn_bwd(dcur, xin, f, u, vv, *ffn_w[(l, k)], f"ffn_bwd_{l}_{k}")
        accs[(l, s)] = acc
        rs_advance(dx)
        dep = None if before_wgrads is None else before_wgrads()
        g_win = tn_matmul(h, du, 1024, f"ffn_dwin_{l}_{k}", dep=dep)
        g_wout = tn_matmul_cols(act, df, FS, D, f"ffn_dwout_{l}_{k}")
        rs_begin(f"ffn_{l}_{k}", [(g_win[None], "ffn_w_in", 2 * l + k, 2 * L),
                                  (g_wout.reshape(1, N_CHIPS, FR, D), "ffn_w_out", 2 * l + k, 2 * L)])
        return dx

    grads = {}

    small_names = ["pool_w", "pool_scale", "sgu_ln_g", "sgu_ln_b", "sgu_w", "sgu_b", "ssm_lam_re", "ssm_lam_im",
                   "ssm_b_re", "ssm_b_im", "ssm_c_re", "ssm_c_im", "ssm_log_dt"]
    small_open = []

    def small_grads_start():
        acc_all = jnp.stack([jnp.stack([accs[(l, s)] for s in range(NSUB)]) for l in range(L)])
        dmod_mine = acc_all[:, :, A_SHIFT:A_GATE + 1].reshape(L, NSUB * 3 * D)
        packed, offs = _pack_rows([dmod_mine, acc_all[:, :, A_GPRE], acc_all[:, :, A_GPOST], dd_mine]
                                  + [small_g[n] for n in small_names], D)
        packed = pinned(packed).astype(BF16)
        started = split_start([packed, lax.empty((N_DEV,) + packed.shape, BF16)], allgather_direct_copies,
                              N_DEV - 1, "ag_grads_start")
        small_open.append((started, offs))
        pin_behind(started[3])
        return started[3]

    def small_grads_finish(after):
        (ssem, rsem, bufs, _), offs = small_open.pop()
        own, land = split_wait(ssem, rsem, bufs, after, allgather_direct_copies, "ag_grads_wait")
        me_arr = me.reshape(1).astype(jnp.int32)
        summed = _unpack_rows(sum_gathered(land, own, me_arr, "sum_small"), offs)
        grads.update({n: g for n, g in zip(small_names, summed[4:])})
        grads["ada_b"] = summed[0]
        grads["norm_pre"] = lax.dynamic_slice_in_dim(summed[1], chip * DS, DS, axis=2)
        grads["norm_post"] = lax.dynamic_slice_in_dim(summed[2], chip * DS, DS, axis=2)
        grads["ssm_d"] = lax.dynamic_slice_in_dim(summed[3], chip * DS, DS, axis=1)
        nm = L * NSUB * 3
        is_me = (jnp.arange(N_DEV) == me)[:, None, None]
        dmod_all = jnp.where(is_me, own[None, :nm], land[:, :nm]).reshape(N_DEV, L, NSUB * 3 * D)
        dmod_shard = lax.dynamic_slice_in_dim(jnp.moveaxis(dmod_all, 0, 1), chip * NS, NS, axis=2)
        grads["ada_w"] = ada_grad(c_all, dmod_shard, "ada_grad")

    for l in reversed(range(L)):
        dcur = ffn_back(l, 2, 1, dcur)
        if l % 2 == 0:
            xin, vv, h, z, ycat, f = saved[(l, 1)]
            vv = pinned(vv)
            df, dact, acc_post = post_bwd_matmul(dcur, f, vv, about_g, f"mixa_out_bwd_{l}")
            rs_advance(dact)
            g_about = tn_matmul_cols(ycat, df, 512, D, f"mixa_dwout_{l}")
            dz, dpw, dvecw, dsw, dsb = mixa_core_bwd(z, dact, pool_w16, pool_scale, sgu_ln_g, sgu_ln_b, sgu_w[0],
                                                     sgu_bexp, f"mixa_core_bwd_{l}")
            g_abin = tn_matmul_cols(h, dz, 512, abin_g.shape[-1], f"mixa_dwin_{l}")
            dcur, acc_pre = matmul_pre_bwd(dz, abin_g, xin, dcur, vv, f"mixa_in_bwd_{l}")
            accs[(l, 1)] = acc_pre + acc_post
            rs_begin(f"mixa_{l}", [(g_abin[None], "ab_w_in", 0, 1),
                                   (g_about.reshape(1, N_CHIPS, -1, D), "ab_w_out", 0, 1)])
            small_g.update(pool_w=dpw[None], pool_scale=dvecw[0:1], sgu_ln_g=dvecw[1:2], sgu_ln_b=dvecw[2:3],
                           sgu_w=dsw[None], sgu_b=dsb.T[None])
        else:
            xin, vv, h, us, yy, f = saved[(l, 1)]
            vv = pinned(vv)
            dy, dab, gact, acc_post = s5_glu_bwd(dcur, f, yy, vv, glu_g, f"s5_glu_bwd_{l}")
            rs_advance(dy)
            g_glu = tn_matmul_cols(gact, dab, 512, glu_g.shape[-1], f"s5_dwglu_{l}")
            dys = _to_scan_order(dy)
            dus, dbre, dbim, dcre_t, dcim_t, dar, dai, dd = s5_core_bwd(
                us, dys, a_re, a_im, bre16, bim16, cre16, cim16, d_full, f"s5_core_bwd_{l}")
            du = _from_scan_order(dus).astype(BF16)
            g_sin = tn_matmul_cols(h, du, 512, D, f"s5_dwin_{l}")
            dcur, acc_pre = matmul_pre_bwd(du, sin_g, xin, dcur, vv, f"s5_in_bwd_{l}")
            accs[(l, 1)] = acc_pre + acc_post
            rs_begin(f"s5_{l}", [(g_sin.reshape(1, N_CHIPS, -1, D), "ssm_w_in", 0, 1), (g_glu[None], "ssm_w_glu", 0, 1)])
            s5_grads = s5_vjp((dar, dai, dbre, dbim, jnp.swapaxes(dcre_t, 1, 2), jnp.swapaxes(dcim_t, 1, 2)))
            small_g.update(ssm_lam_re=s5_grads[0][None], ssm_lam_im=s5_grads[1][None], ssm_log_dt=s5_grads[2][None],
                           ssm_b_re=s5_grads[3][None], ssm_b_im=s5_grads[4][None], ssm_c_re=s5_grads[5][None],
                           ssm_c_im=s5_grads[6][None])
            dd_mine = dd
        dcur = ffn_back(l, 0, 0, dcur, before_wgrads=small_grads_start if l == 0 else None)
    grad_x = dcur[None]
    rs_advance(last_start[0])
    small_grads_finish(last_start[0])

    big_names = ["ffn_w_in", "ffn_w_out", "ab_w_in", "ab_w_out", "ssm_w_in", "ssm_w_glu"]
    deltas, new_m, new_v = {}, {}, {}

    def update(n, dep=None):
        g = grads[n].reshape(weights[n].shape)
        outs = adamw_nd(weights[n], g, m_in[n], v_in[n], f"adamw_{n}", emit_grad=n in big_names, dep=dep)
        grads[n] = outs[0] if n in big_names else g
        deltas[n], new_m[n], new_v[n] = outs[-3:]

    fin = {}

    def chip_sums(pending, after):
        for tag, items, (ssem, rsem, bufs, _) in pending:
            bufs = split_wait(ssem, rsem, bufs, after, chip_exchange_copies, f"rs_chip_wait_{tag}")
            n = len(items)
            for i, (it, p, r) in enumerate(zip(items, bufs[:n], bufs[n:])):
                fin[it[1]] = chip_sum(p, r, chip_c, f"rs_chip_sum_{tag}_{i}", dest=fin.get(it[1]), slot=it[2],
                                      n_slots=it[3])

    chip_sums(rs_pending[:-1], pinned(grads["ada_b"]))
    update("ada_w", dep=fin["ab_w_in"])
    chip_sums(rs_pending[-1:], deltas["ada_w"])
    ssem, rsem, bufs, fill_token = split_start([fin[n] for n in big_names], fill_copies, len(big_names), "rs_fill_start")
    for n in names:
        if n not in big_names and n != "ada_w":
            update(n, dep=fill_token)
    filled = split_wait(ssem, rsem, bufs, deltas["ssm_log_dt"], fill_copies, "rs_fill_wait")
    for n, g in zip(big_names, filled):
        grads[n] = g
        update(n)

    return (loss, grad_x, *[grads[n] for n in names], *[deltas[n] for n in names],
            *[new_m[n] for n in names], *[new_v[n] for n in names])
```

```python
import math

import jax
import jax.numpy as jnp
from jax import lax
from jax.experimental import pallas as pl
from jax.experimental.pallas import tpu as pltpu

F32 = jnp.float32
BF16 = jnp.bfloat16
EPS = 1e-6
MESH_T = pl.DeviceIdType.MESH
VMEM_LIMIT_BYTES = 56 * 1024 * 1024
N_CHIPS = 4
N_DEV = 8
POOL_WINDOWS = (2, 4, 8, 16)
CHUNK = 128
SCAN_LANES = 8
SCAN_UNROLL = 4
WG_AHEAD = 2
FFN_BWD_CHUNK = 2048
ADAM_LR = 0.001
ADAM_B1 = 0.9
ADAM_B2 = 0.999
ADAM_EPS = 1e-08
ADAM_WD = 0.01
ADAM_STEP = 10
GELU_C = math.sqrt(2.0 / math.pi)
GELU_K = 0.044715

V_SHIFT, V_SCALE, V_GATE, V_GPRE, V_GPOST = 0, 1, 2, 3, 4
A_SHIFT, A_SCALE, A_GATE, A_GPRE, A_GPOST = 0, 1, 2, 3, 4

ANY = pl.BlockSpec(memory_space=pl.ANY)


def _cp(sem=None):
    if sem is None:
        return pltpu.CompilerParams(vmem_limit_bytes=VMEM_LIMIT_BYTES)
    return pltpu.CompilerParams(vmem_limit_bytes=VMEM_LIMIT_BYTES, dimension_semantics=sem)


def _dot(a, b):
    return jnp.dot(a, b, preferred_element_type=F32)


def _dot_nt(a, b):
    return lax.dot_general(a, b, (((1,), (1,)), ((), ())), preferred_element_type=F32)


def _dot_tn(a, b):
    return lax.dot_general(a, b, (((0,), (0,)), ((), ())), preferred_element_type=F32)


def _sigmoid(x):
    return 1.0 / (1.0 + jnp.exp(-x))


def _gelu(x):
    return 0.5 * x * (1.0 + jnp.tanh(GELU_C * (x + GELU_K * x * x * x)))


def _gelu_grad(x):
    t = jnp.tanh(GELU_C * (x + GELU_K * x * x * x))
    return 0.5 * (1.0 + t) + 0.5 * x * (1.0 - t * t) * GELU_C * (1.0 + 3.0 * GELU_K * x * x)


def _rowsum(v):
    return jnp.sum(v, axis=0, keepdims=True)


def _lanemean(v):
    return jnp.mean(v, axis=-1, keepdims=True)


def _row(ref, i):
    return ref[pl.ds(i, 1), :]


def _pre_fwd(x, vec_ref):
    r = lax.rsqrt(_lanemean(x * x) + EPS)
    return (x * r) * _row(vec_ref, V_GPRE) * (1.0 + _row(vec_ref, V_SCALE)) + _row(vec_ref, V_SHIFT)


def _pre_bwd(x, dh, vec_ref):
    g = _row(vec_ref, V_GPRE)
    sc = 1.0 + _row(vec_ref, V_SCALE)
    r = lax.rsqrt(_lanemean(x * x) + EPS)
    xn = x * r
    dhx = dh * xn
    t2 = dh * (g * sc)
    dx = r * (t2 - xn * _lanemean(t2 * xn))
    return dx, _rowsum(dh), _rowsum(dhx * g), _rowsum(dhx * sc)


def _post_fwd(x, f, vec_ref, rw):
    q = lax.rsqrt(_lanemean(f * f) + EPS)
    return x + (rw * _row(vec_ref, V_GATE)) * (f * q * _row(vec_ref, V_GPOST))


def _post_bwd(dout, f, vec_ref, rw):
    gp = _row(vec_ref, V_GPOST)
    gate = _row(vec_ref, V_GATE)
    q = lax.rsqrt(_lanemean(f * f) + EPS)
    fhat = f * q
    dgate = _rowsum(dout * (rw * fhat * gp))
    dy = dout * (rw * gate)
    dgpost = _rowsum(dy * fhat)
    t = dy * gp
    df = q * (t - fhat * _lanemean(t * fhat))
    return df, dgate, dgpost


def _acc_add(acc_ref, row, v):
    acc_ref[pl.ds(row, 1), :] += v


def ffn_fwd(x, vec, win_g, wout_g, name, tm=512):
    S, D = x.shape
    FS = win_g.shape[-1]
    FR = wout_g.shape[-2]

    def body(x_ref, vec_ref, win_hbm, wout_hbm, out_ref, h_ref, u_ref, f_ref, win_s, wout_s, sem):
        @pl.when(pl.program_id(0) == 0)
        def _():
            cps = [pltpu.make_async_copy(win_hbm.at[j], win_s.at[j], sem.at[j]) for j in range(N_CHIPS)]
            cps += [pltpu.make_async_copy(wout_hbm.at[j], wout_s.at[pl.ds(j * FR, FR), :], sem.at[N_CHIPS + j])
                    for j in range(N_CHIPS)]
            for cp in cps:
                cp.start()
            for cp in cps:
                cp.wait()

        xv = x_ref[...]
        h = _pre_fwd(xv, vec_ref).astype(BF16)
        h_ref[...] = h
        f = jnp.zeros((tm, D), F32)
        for j in range(2):
            a = _dot(h, win_s[j])
            b = _dot(h, win_s[2 + j])
            u_ref[j] = a.astype(BF16)
            u_ref[2 + j] = b.astype(BF16)
            act = (a * _sigmoid(a) * b).astype(BF16)
            f = f + _dot(act, wout_s[pl.ds(j * FS, FS), :])
        f_ref[...] = f
        out_ref[...] = _post_fwd(xv, f, vec_ref, 0.5)

    row = pl.BlockSpec((tm, D), lambda i: (i, 0))
    return pl.pallas_call(
        body, name=name, grid=(S // tm,),
        in_specs=[row, pl.BlockSpec((8, D), lambda i: (0, 0)), ANY, ANY],
        out_specs=[row, row, pl.BlockSpec((N_CHIPS, tm, FS), lambda i: (0, i, 0)), row],
        out_shape=[jax.ShapeDtypeStruct((S, D), F32), jax.ShapeDtypeStruct((S, D), BF16),
                   jax.ShapeDtypeStruct((N_CHIPS, S, FS), BF16), jax.ShapeDtypeStruct((S, D), F32)],
        scratch_shapes=[pltpu.VMEM((N_CHIPS, D, FS), BF16), pltpu.VMEM((N_CHIPS * FR, D), BF16),
                        pltpu.SemaphoreType.DMA((2 * N_CHIPS,))],
        compiler_params=_cp(("arbitrary",)),
    )(x, vec, win_g, wout_g)


def ffn_bwd(dout, x, f, u, vec, win_g, wout_g, name, tm=256):
    S, D = x.shape
    FS = win_g.shape[-1]
    FR = wout_g.shape[-2]

    def body(dout_ref, x_ref, f_ref, u_ref, vec_ref, win_hbm, wout_hbm,
             dx_ref, df_ref, du_ref, act_ref, acc_ref, win_s, wout_s, sem):
        @pl.when(pl.program_id(0) == 0)
        def _():
            cps = [pltpu.make_async_copy(win_hbm.at[j], win_s.at[j], sem.at[j]) for j in range(N_CHIPS)]
            cps += [pltpu.make_async_copy(wout_hbm.at[j], wout_s.at[pl.ds(j * FR, FR), :], sem.at[N_CHIPS + j])
                    for j in range(N_CHIPS)]
            for cp in cps:
                cp.start()
            acc_ref[...] = jnp.zeros_like(acc_ref)
            for cp in cps:
                cp.wait()

        dout_v = dout_ref[...]
        df, dgate, dgpost = _post_bwd(dout_v, f_ref[...], vec_ref, 0.5)
        dfb = df.astype(BF16)
        df_ref[...] = dfb
        dh = jnp.zeros((tm, D), F32)
        chunks = [(c0, min(FFN_BWD_CHUNK, FS - c0)) for c0 in range(0, FS, FFN_BWD_CHUNK)]
        for j in range(2):
            for c0, cw in chunks:
                a = u_ref[j, :, c0:c0 + cw].astype(F32)
                b = u_ref[2 + j, :, c0:c0 + cw].astype(F32)
                sig = _sigmoid(a)
                sl = a * sig
                dact = _dot_nt(dfb, wout_s[pl.ds(j * FS + c0, cw), :])
                da = (dact * b * (sig * (1.0 + a * (1.0 - sig)))).astype(BF16)
                db = (dact * sl).astype(BF16)
                du_ref[j, :, c0:c0 + cw] = da
                du_ref[2 + j, :, c0:c0 + cw] = db
                act_ref[:, j * FS + c0:j * FS + c0 + cw] = (sl * b).astype(BF16)
                dh = dh + _dot_nt(da, win_s[j, :, c0:c0 + cw]) + _dot_nt(db, win_s[2 + j, :, c0:c0 + cw])
        dx, dshift, dscale, dgpre = _pre_bwd(x_ref[...], dh, vec_ref)
        dx_ref[...] = dout_v + dx
        _acc_add(acc_ref, A_SHIFT, dshift)
        _acc_add(acc_ref, A_SCALE, dscale)
        _acc_add(acc_ref, A_GATE, dgate)
        _acc_add(acc_ref, A_GPRE, dgpre)
        _acc_add(acc_ref, A_GPOST, dgpost)

    row = pl.BlockSpec((tm, D), lambda i: (i, 0))
    ublk = pl.BlockSpec((N_CHIPS, tm, FS), lambda i: (0, i, 0))
    const = pl.BlockSpec((8, D), lambda i: (0, 0))
    return pl.pallas_call(
        body, name=name, grid=(S // tm,),
        in_specs=[row, row, row, ublk, const, ANY, ANY],
        out_specs=[row, row, ublk, pl.BlockSpec((tm, 2 * FS), lambda i: (i, 0)), const],
        out_shape=[jax.ShapeDtypeStruct((S, D), F32), jax.ShapeDtypeStruct((S, D), BF16),
                   jax.ShapeDtypeStruct((N_CHIPS, S, FS), BF16), jax.ShapeDtypeStruct((S, 2 * FS), BF16),
                   jax.ShapeDtypeStruct((8, D), F32)],
        scratch_shapes=[pltpu.VMEM((N_CHIPS, D, FS), BF16), pltpu.VMEM((N_CHIPS * FR, D), BF16),
                        pltpu.SemaphoreType.DMA((2 * N_CHIPS,))],
        compiler_params=_cp(("arbitrary",)),
    )(dout, x, f, u, vec, win_g, wout_g)


def tn_matmul(a, b, tk, name, dep=None):
    S, K = a.shape
    nb, _, tn = b.shape
    extra = [] if dep is None else [dep]

    def body(a_ref, b_ref, *rest):
        rest[-1][...] = _dot_tn(a_ref[...], b_ref[...]).astype(BF16)

    return pl.pallas_call(
        body, name=name, grid=(K // tk, nb),
        in_specs=[pl.BlockSpec((S, tk), lambda i, j: (0, i)), pl.BlockSpec((None, S, tn), lambda i, j: (j, 0, 0))]
        + [ANY] * len(extra),
        out_specs=pl.BlockSpec((None, tk, tn), lambda i, j: (j, i, 0)),
        out_shape=jax.ShapeDtypeStruct((nb, K, tn), BF16),
        compiler_params=_cp(("arbitrary", "arbitrary")),
    )(a, b, *extra)


def tn_matmul_cols(a, b, tk, tn, name):
    S, K = a.shape
    N = b.shape[1]

    def body(a_ref, b_ref, o_ref):
        o_ref[...] = _dot_tn(a_ref[...], b_ref[...]).astype(BF16)

    return pl.pallas_call(
        body, name=name, grid=(K // tk, N // tn),
        in_specs=[pl.BlockSpec((S, tk), lambda i, j: (0, i)), pl.BlockSpec((S, tn), lambda i, j: (0, j))],
        out_specs=pl.BlockSpec((None, tk, tn), lambda i, j: (j, i, 0)),
        out_shape=jax.ShapeDtypeStruct((N // tn, K, tn), BF16),
        compiler_params=_cp(("arbitrary", "arbitrary")),
    )(a, b)


def pre_matmul(x, vec, w3, name, tm=512):
    S, D = x.shape
    nj, _, Nj = w3.shape

    def body(x_ref, vec_ref, w_ref, h_ref, z_ref):
        h = _pre_fwd(x_ref[...], vec_ref).astype(BF16)
        h_ref[...] = h
        for j in range(nj):
            z_ref[:, j * Nj:(j + 1) * Nj] = _dot(h, w_ref[j])

    row = pl.BlockSpec((tm, D), lambda i: (i, 0))
    return pl.pallas_call(
        body, name=name, grid=(S // tm,),
        in_specs=[row, pl.BlockSpec((8, D), lambda i: (0, 0)), pl.BlockSpec((nj, D, Nj), lambda i: (0, 0, 0))],
        out_specs=[row, pl.BlockSpec((tm, nj * Nj), lambda i: (i, 0))],
        out_shape=[jax.ShapeDtypeStruct((S, D), BF16), jax.ShapeDtypeStruct((S, nj * Nj), F32)],
        compiler_params=_cp(("arbitrary",)),
    )(x, vec, w3)


def matmul_pre_bwd(dz, w3, x, dres, vec, name, tm=512):
    S, D = x.shape
    nj, _, Nj = w3.shape

    def body(dz_ref, w_ref, x_ref, dres_ref, vec_ref, dx_ref, acc_ref):
        @pl.when(pl.program_id(0) == 0)
        def _():
            acc_ref[...] = jnp.zeros_like(acc_ref)

        dh = jnp.zeros((tm, D), F32)
        for j in range(nj):
            dh = dh + _dot_nt(dz_ref[:, j * Nj:(j + 1) * Nj], w_ref[j])
        dx, dshift, dscale, dgpre = _pre_bwd(x_ref[...], dh, vec_ref)
        dx_ref[...] = dres_ref[...] + dx
        _acc_add(acc_ref, A_SHIFT, dshift)
        _acc_add(acc_ref, A_SCALE, dscale)
        _acc_add(acc_ref, A_GPRE, dgpre)

    row = pl.BlockSpec((tm, D), lambda i: (i, 0))
    const = pl.BlockSpec((8, D), lambda i: (0, 0))
    return pl.pallas_call(
        body, name=name, grid=(S // tm,),
        in_specs=[pl.BlockSpec((tm, nj * Nj), lambda i: (i, 0)), pl.BlockSpec((nj, D, Nj), lambda i: (0, 0, 0)),
                  row, row, const],
        out_specs=[row, const],
        out_shape=[jax.ShapeDtypeStruct((S, D), F32), jax.ShapeDtypeStruct((8, D), F32)],
        compiler_params=_cp(("arbitrary",)),
    )(dz, w3, x, dres, vec)


def matmul_post(act, w, x, vec, name, tm=512):
    S, D = x.shape
    K = act.shape[1]

    def body(act_ref, w_ref, x_ref, vec_ref, out_ref, f_ref):
        f = _dot(act_ref[...], w_ref[...])
        f_ref[...] = f
        out_ref[...] = _post_fwd(x_ref[...], f, vec_ref, 1.0)

    row = pl.BlockSpec((tm, D), lambda i: (i, 0))
    return pl.pallas_call(
        body, name=name, grid=(S // tm,),
        in_specs=[pl.BlockSpec((tm, K), lambda i: (i, 0)), pl.BlockSpec((K, D), lambda i: (0, 0)), row,
                  pl.BlockSpec((8, D), lambda i: (0, 0))],
        out_specs=[row, row],
        out_shape=[jax.ShapeDtypeStruct((S, D), F32), jax.ShapeDtypeStruct((S, D), F32)],
        compiler_params=_cp(("arbitrary",)),
    )(act, w, x, vec)


def post_bwd_matmul(dout, f, vec, w, name, tm=512):
    S, D = dout.shape
    K = w.shape[0]

    def body(dout_ref, f_ref, vec_ref, w_ref, df_ref, dact_ref, acc_ref):
        @pl.when(pl.program_id(0) == 0)
        def _():
            acc_ref[...] = jnp.zeros_like(acc_ref)

        df, dgate, dgpost = _post_bwd(dout_ref[...], f_ref[...], vec_ref, 1.0)
        dfb = df.astype(BF16)
        df_ref[...] = dfb
        dact_ref[...] = _dot_nt(dfb, w_ref[...])
        _acc_add(acc_ref, A_GATE, dgate)
        _acc_add(acc_ref, A_GPOST, dgpost)

    row = pl.BlockSpec((tm, D), lambda i: (i, 0))
    const = pl.BlockSpec((8, D), lambda i: (0, 0))
    return pl.pallas_call(
        body, name=name, grid=(S // tm,),
        in_specs=[row, row, const, pl.BlockSpec((K, D), lambda i: (0, 0))],
        out_specs=[row, pl.BlockSpec((tm, K), lambda i: (i, 0)), const],
        out_shape=[jax.ShapeDtypeStruct((S, D), BF16), jax.ShapeDtypeStruct((S, K), F32),
                   jax.ShapeDtypeStruct((8, D), F32)],
        compiler_params=_cp(("arbitrary",)),
    )(dout, f, vec, w)


def _band(w, transposed, prev):
    r = lax.broadcasted_iota(jnp.int32, (CHUNK, CHUNK), 1 if transposed else 0)
    c = lax.broadcasted_iota(jnp.int32, (CHUNK, CHUNK), 0 if transposed else 1)
    d = r - c
    m = (d + CHUNK < w) if prev else ((d >= 0) & (d < w))
    return jnp.where(m, 1.0, 0.0).astype(BF16)


def _split_hi_lo(a):
    hi = a.astype(BF16)
    lo = (a - hi.astype(F32)).astype(BF16)
    return hi, lo


def _pool_diff(a, ap, g, denom):
    w = POOL_WINDOWS[g]
    a_hi, a_lo = _split_hi_lo(a)
    p_hi, p_lo = _split_hi_lo(ap)
    mc = _band(w, False, False)
    mp = _band(w, False, True)
    win = _dot(mc, a_hi) + _dot(mc, a_lo) + _dot(mp, p_hi) + _dot(mp, p_lo)
    return win / denom - a


def _sgu_norm(v, lg, lb):
    mu = _lanemean(v)
    xc = v - mu
    rstd = lax.rsqrt(_lanemean(xc * xc) + EPS)
    xhat = xc * rstd
    return xhat, rstd, xhat * lg + lb


def _tril_mask():
    r = lax.broadcasted_iota(jnp.int32, (CHUNK, CHUNK), 0)
    c = lax.broadcasted_iota(jnp.int32, (CHUNK, CHUNK), 1)
    return r >= c


def _positions(i, w):
    r = lax.broadcasted_iota(jnp.int32, (CHUNK, 128), 0)
    pos = (i * CHUNK + r + 1).astype(F32)
    return jnp.minimum(pos, float(w))


def mixa_core_fwd(z, pool_w, pool_scale, ln_g, ln_b, sgu_w, sgu_bexp, name):
    S = z.shape[0]
    W = z.shape[1] // 3
    G = len(POOL_WINDOWS)
    GD = W // G

    def body(zc_ref, zp_ref, pw_ref, ps_ref, lg_ref, lb_ref, sw_ref, sb_ref, y_ref):
        i = pl.program_id(0)
        has_prev = jnp.where(i > 0, 1.0, 0.0)
        for g in range(G):
            sl = slice(g * GD, (g + 1) * GD)
            a = zc_ref[:, sl]
            ap = zp_ref[:, sl] * has_prev
            d = _pool_diff(a, ap, g, _positions(i, POOL_WINDOWS[g])).astype(BF16)
            y_ref[:, sl] = (_dot(d, pw_ref[g]) * ps_ref[:, sl]).astype(BF16)
        tril = _tril_mask()
        for hh in range(G):
            u = _gelu(zc_ref[:, W + hh * GD:W + (hh + 1) * GD])
            v = _gelu(zc_ref[:, 2 * W + hh * GD:2 * W + (hh + 1) * GD])
            sl = slice(hh * GD, (hh + 1) * GD)
            _, _, vn = _sgu_norm(v, lg_ref[:, sl], lb_ref[:, sl])
            wm = jnp.where(tril, sw_ref[hh], 0.0).astype(BF16)
            s = _dot(wm, vn.astype(BF16)) + sb_ref[hh]
            y_ref[:, W + hh * GD:W + (hh + 1) * GD] = (u * s).astype(BF16)

    vecw = pl.BlockSpec((1, W), lambda i: (0, 0))
    mats = pl.BlockSpec((G, GD, GD), lambda i: (0, 0, 0))
    return pl.pallas_call(
        body, name=name, grid=(S // CHUNK,),
        in_specs=[pl.BlockSpec((CHUNK, 3 * W), lambda i: (i, 0)),
                  pl.BlockSpec((CHUNK, W), lambda i: (jnp.maximum(i - 1, 0), 0)),
                  mats, vecw, vecw, vecw, mats, mats],
        out_specs=pl.BlockSpec((CHUNK, 2 * W), lambda i: (i, 0)),
        out_shape=jax.ShapeDtypeStruct((S, 2 * W), BF16),
        compiler_params=_cp(("arbitrary",)),
    )(z, z, pool_w, pool_scale, ln_g, ln_b, sgu_w, sgu_bexp)


def mixa_core_bwd(z, dy, pool_w, pool_scale, ln_g, ln_b, sgu_w, sgu_bexp, name):
    S = z.shape[0]
    W = z.shape[1] // 3
    G = len(POOL_WINDOWS)
    GD = W // G
    n_tiles = S // CHUNK

    def body(zc_ref, zp_ref, dyc_ref, dyn_ref, pw_ref, ps_ref, lg_ref, lb_ref, sw_ref, sb_ref,
             dz_ref, dpw_ref, dvec_ref, dsw_ref, dsb_ref):
        i = pl.program_id(0)

        @pl.when(i == 0)
        def _():
            dpw_ref[...] = jnp.zeros_like(dpw_ref)
            dvec_ref[...] = jnp.zeros_like(dvec_ref)
            dsw_ref[...] = jnp.zeros_like(dsw_ref)
            dsb_ref[...] = jnp.zeros_like(dsb_ref)

        has_prev = jnp.where(i > 0, 1.0, 0.0)
        has_next = jnp.where(i < n_tiles - 1, 1.0, 0.0)
        for g in range(G):
            w = POOL_WINDOWS[g]
            sl = slice(g * GD, (g + 1) * GD)
            a = zc_ref[:, sl]
            ap = zp_ref[:, sl] * has_prev
            den_c = _positions(i, w)
            den_n = _positions(i + 1, w)
            d = _pool_diff(a, ap, g, den_c).astype(BF16)
            ps = ps_ref[:, sl]
            pw = pw_ref[g]
            dyc = dyc_ref[:, sl]
            dvec_ref[pl.ds(0, 1), sl] += _rowsum(dyc * _dot(d, pw))
            dyp_c = (dyc * ps).astype(BF16)
            dyp_n = (dyn_ref[:, sl] * (ps * has_next)).astype(BF16)
            dpw_ref[g] += _dot_tn(d, dyp_c)
            dd_c = _dot_nt(dyp_c, pw)
            dd_n = _dot_nt(dyp_n, pw)
            da = (_dot(_band(w, True, False), (dd_c / den_c).astype(BF16))
                  + _dot(_band(w, True, True), (dd_n / den_n).astype(BF16)) - dd_c)
            dz_ref[:, sl] = da.astype(BF16)
        tril = _tril_mask()
        for hh in range(G):
            sl = slice(hh * GD, (hh + 1) * GD)
            zu = zc_ref[:, W + hh * GD:W + (hh + 1) * GD]
            zv = zc_ref[:, 2 * W + hh * GD:2 * W + (hh + 1) * GD]
            u = _gelu(zu)
            v = _gelu(zv)
            lg = lg_ref[:, sl]
            xhat, rstd, vn = _sgu_norm(v, lg, lb_ref[:, sl])
            vnb = vn.astype(BF16)
            wm = jnp.where(tril, sw_ref[hh], 0.0).astype(BF16)
            s = _dot(wm, vnb) + sb_ref[hh]
            dyb = dyc_ref[:, W + hh * GD:W + (hh + 1) * GD]
            du = dyb * s
            ds = dyb * u
            dsb_ref[:, hh:hh + 1] += jnp.sum(ds, axis=1, keepdims=True)
            dsb16 = ds.astype(BF16)
            dsw_ref[hh] += jnp.where(tril, _dot_nt(dsb16, vnb), 0.0)
            dvn = _dot_tn(wm, dsb16)
            dvec_ref[pl.ds(1, 1), sl] += _rowsum(dvn * xhat)
            dvec_ref[pl.ds(2, 1), sl] += _rowsum(dvn)
            dxh = dvn * lg
            dv = rstd * (dxh - _lanemean(dxh) - xhat * _lanemean(dxh * xhat))
            dz_ref[:, W + hh * GD:W + (hh + 1) * GD] = (du * _gelu_grad(zu)).astype(BF16)
            dz_ref[:, 2 * W + hh * GD:2 * W + (hh + 1) * GD] = (dv * _gelu_grad(zv)).astype(BF16)

    vecw = pl.BlockSpec((1, W), lambda i: (0, 0))
    mats = pl.BlockSpec((G, GD, GD), lambda i: (0, 0, 0))
    return pl.pallas_call(
        body, name=name, grid=(n_tiles,),
        in_specs=[pl.BlockSpec((CHUNK, 3 * W), lambda i: (i, 0)),
                  pl.BlockSpec((CHUNK, W), lambda i: (jnp.maximum(i - 1, 0), 0)),
                  pl.BlockSpec((CHUNK, 2 * W), lambda i: (i, 0)),
                  pl.BlockSpec((CHUNK, W), lambda i: (jnp.minimum(i + 1, n_tiles - 1), 0)),
                  mats, vecw, vecw, vecw, mats, mats],
        out_specs=[pl.BlockSpec((CHUNK, 3 * W), lambda i: (i, 0)), mats,
                   pl.BlockSpec((8, W), lambda i: (0, 0)), mats, pl.BlockSpec((CHUNK, G), lambda i: (0, 0))],
        out_shape=[jax.ShapeDtypeStruct((S, 3 * W), BF16), jax.ShapeDtypeStruct((G, GD, GD), F32),
                   jax.ShapeDtypeStruct((8, W), F32), jax.ShapeDtypeStruct((G, CHUNK, CHUNK), F32),
                   jax.ShapeDtypeStruct((CHUNK, G), F32)],
        compiler_params=_cp(("arbitrary",)),
    )(z, z, dy, dy, pool_w, pool_scale, ln_g, ln_b, sgu_w, sgu_bexp)


def _cmul(ar, ai, br, bi):
    return ar * br - ai * bi, ar * bi + ai * br


def _cpow(ar, ai, n):
    rr, ri = None, None
    br, bi = ar, ai
    while n:
        if n & 1:
            rr, ri = (br, bi) if rr is None else _cmul(rr, ri, br, bi)
        n >>= 1
        if n:
            br, bi = _cmul(br, bi, br, bi)
    return rr, ri


def _seg_rows(k):
    return pl.ds(pl.multiple_of(k * SCAN_LANES, SCAN_LANES), SCAN_LANES)


def _scan_fwd(xre, xim, carry, ar, ai, K):
    P = xre.shape[1]
    a8r = jnp.broadcast_to(ar, (SCAN_LANES, P))
    a8i = jnp.broadcast_to(ai, (SCAN_LANES, P))

    def local(k, c):
        pr, pi = c
        rows = _seg_rows(k)
        nr = a8r * pr - a8i * pi + xre[rows, :]
        ni = a8r * pi + a8i * pr + xim[rows, :]
        xre[rows, :] = nr
        xim[rows, :] = ni
        return nr, ni

    er, ei = lax.fori_loop(1, K, local, (xre[pl.ds(0, SCAN_LANES), :], xim[pl.ds(0, SCAN_LANES), :]),
                           unroll=SCAN_UNROLL)
    akr, aki = _cpow(ar, ai, K)
    cr = jnp.zeros((1, P), F32)
    ci = jnp.zeros((1, P), F32)
    carry[pl.ds(0, 1), :] = cr
    carry[pl.ds(SCAN_LANES, 1), :] = ci
    for j in range(1, SCAN_LANES):
        tr, ti = _cmul(akr, aki, cr, ci)
        cr = er[j - 1:j, :] + tr
        ci = ei[j - 1:j, :] + ti
        carry[pl.ds(j, 1), :] = cr
        carry[pl.ds(SCAN_LANES + j, 1), :] = ci
    cmr = carry[pl.ds(0, SCAN_LANES), :]
    cmi = carry[pl.ds(SCAN_LANES, SCAN_LANES), :]

    def fix(k, c):
        pr, pi = c
        rows = _seg_rows(k)
        tr, ti = _cmul(pr, pi, cmr, cmi)
        xre[rows, :] += tr
        xim[rows, :] += ti
        return _cmul(pr, pi, a8r, a8i)

    lax.fori_loop(0, K, fix, (a8r, a8i), unroll=SCAN_UNROLL)


def s5_core_fwd(u, ar, ai, bre, bim, cre_t, cim_t, dskip, name):
    S, D = u.shape
    nblk, UB, PB = bre.shape
    K = S // SCAN_LANES

    def body(u_ref, ar_ref, ai_ref, bre_ref, bim_ref, cre_ref, cim_ref, d_ref, y_ref, xre, xim, carry):
        uv = u_ref[...]
        ub = uv.astype(BF16)
        xre[...] = _dot(ub, bre_ref[...])
        xim[...] = _dot(ub, bim_ref[...])
        _scan_fwd(xre, xim, carry, ar_ref[...], ai_ref[...], K)
        y_ref[...] = (_dot(xre[...].astype(BF16), cre_ref[...]) - _dot(xim[...].astype(BF16), cim_ref[...])
                      + d_ref[...] * uv)

    ucol = pl.BlockSpec((S, UB), lambda i: (0, i))
    pvec = pl.BlockSpec((None, 1, PB), lambda i: (i, 0, 0))
    bmat = pl.BlockSpec((None, UB, PB), lambda i: (i, 0, 0))
    cmat = pl.BlockSpec((None, PB, UB), lambda i: (i, 0, 0))
    return pl.pallas_call(
        body, name=name, grid=(nblk,),
        in_specs=[ucol, pvec, pvec, bmat, bmat, cmat, cmat, pl.BlockSpec((1, UB), lambda i: (0, i))],
        out_specs=ucol,
        out_shape=jax.ShapeDtypeStruct((S, D), F32),
        scratch_shapes=[pltpu.VMEM((S, PB), F32), pltpu.VMEM((S, PB), F32), pltpu.VMEM((2 * SCAN_LANES, PB), F32)],
        compiler_params=_cp(("arbitrary",)),
    )(u, ar, ai, bre, bim, cre_t, cim_t, dskip)


def s5_core_bwd(u, dy, ar, ai, bre, bim, cre, cim, dskip, name):
    S, D = u.shape
    nblk, UB, PB = bre.shape
    K = S // SCAN_LANES

    def body(u_ref, dy_ref, ar_ref, ai_ref, bre_ref, bim_ref, cre_ref, cim_ref, d_ref,
             du_ref, dbre_ref, dbim_ref, dcre_ref, dcim_ref, dar_ref, dai_ref, dd_ref,
             xre, xim, gre, gim, carry, carry_b):
        ar = ar_ref[...]
        ai = ai_ref[...]
        uv = u_ref[...]
        ub = uv.astype(BF16)
        dyv = dy_ref[...]
        dyb = dyv.astype(BF16)
        xre[...] = _dot(ub, bre_ref[...])
        xim[...] = _dot(ub, bim_ref[...])
        _scan_fwd(xre, xim, carry, ar, ai, K)
        dcre_ref[...] = _dot_tn(xre[...].astype(BF16), dyb)
        dcim_ref[...] = -_dot_tn(xim[...].astype(BF16), dyb)
        gre[...] = _dot(dyb, cre_ref[...])
        gim[...] = -_dot(dyb, cim_ref[...])

        a8r = jnp.broadcast_to(ar, (SCAN_LANES, PB))
        a8i = jnp.broadcast_to(ai, (SCAN_LANES, PB))
        na8i = -a8i

        def local(s, c):
            k = K - 2 - s
            nr, ni = c
            rows = _seg_rows(k)
            tr = gre[rows, :] + a8r * nr + a8i * ni
            ti = gim[rows, :] + a8r * ni - a8i * nr
            gre[rows, :] = tr
            gim[rows, :] = ti
            return tr, ti

        last = _seg_rows(K - 1)
        fr, fi = lax.fori_loop(0, K - 1, local, (gre[last, :], gim[last, :]), unroll=SCAN_UNROLL)
        akr, aki = _cpow(ar, -ai, K)
        cr = jnp.zeros((1, PB), F32)
        ci = jnp.zeros((1, PB), F32)
        carry_b[pl.ds(SCAN_LANES - 1, 1), :] = cr
        carry_b[pl.ds(2 * SCAN_LANES - 1, 1), :] = ci
        for j in range(SCAN_LANES - 2, -1, -1):
            tr, ti = _cmul(akr, aki, cr, ci)
            cr = fr[j + 1:j + 2, :] + tr
            ci = fi[j + 1:j + 2, :] + ti
            carry_b[pl.ds(j, 1), :] = cr
            carry_b[pl.ds(SCAN_LANES + j, 1), :] = ci
        cbr = carry_b[pl.ds(0, SCAN_LANES), :]
        cbi = carry_b[pl.ds(SCAN_LANES, SCAN_LANES), :]

        def fix_rows(rows, c, xr, xi):
            pr, pi, dar, dai = c
            tr, ti = _cmul(pr, pi, cbr, cbi)
            g_r = gre[rows, :] + tr
            g_i = gim[rows, :] + ti
            gre[rows, :] = g_r
            gim[rows, :] = g_i
            dar = dar + g_r * xr + g_i * xi
            dai = dai + g_i * xr - g_r * xi
            nr, ni = _cmul(pr, pi, a8r, na8i)
            return nr, ni, dar, dai

        def fix(s, c):
            k = K - 1 - s
            prev = _seg_rows(k - 1)
            return fix_rows(_seg_rows(k), c, xre[prev, :], xim[prev, :])

        z8 = jnp.zeros((SCAN_LANES, PB), F32)
        c = lax.fori_loop(0, K - 1, fix, (a8r, na8i, z8, z8), unroll=SCAN_UNROLL)
        _, _, dar, dai = fix_rows(_seg_rows(0), c, carry[pl.ds(0, SCAN_LANES), :], carry[pl.ds(SCAN_LANES, SCAN_LANES), :])
        dar_ref[...] = _rowsum(dar)
        dai_ref[...] = _rowsum(dai)
        grb = gre[...].astype(BF16)
        gib = gim[...].astype(BF16)
        dbre_ref[...] = _dot_tn(ub, grb)
        dbim_ref[...] = _dot_tn(ub, gib)
        du_ref[...] = _dot_nt(grb, bre_ref[...]) + _dot_nt(gib, bim_ref[...]) + d_ref[...] * dyv
        dd_ref[...] = _rowsum(dyv * uv)

    ucol = pl.BlockSpec((S, UB), lambda i: (0, i))
    pvec = pl.BlockSpec((None, 1, PB), lambda i: (i, 0, 0))
    bmat = pl.BlockSpec((None, UB, PB), lambda i: (i, 0, 0))
    cmat = pl.BlockSpec((None, PB, UB), lambda i: (i, 0, 0))
    dvec = pl.BlockSpec((1, UB), lambda i: (0, i))
    return pl.pallas_call(
        body, name=name, grid=(nblk,),
        in_specs=[ucol, ucol, pvec, pvec, bmat, bmat, bmat, bmat, dvec],
        out_specs=[ucol, bmat, bmat, cmat, cmat, pvec, pvec, dvec],
        out_shape=[jax.ShapeDtypeStruct((S, D), F32),
                   jax.ShapeDtypeStruct((nblk, UB, PB), F32), jax.ShapeDtypeStruct((nblk, UB, PB), F32),
                   jax.ShapeDtypeStruct((nblk, PB, UB), F32), jax.ShapeDtypeStruct((nblk, PB, UB), F32),
                   jax.ShapeDtypeStruct((nblk, 1, PB), F32), jax.ShapeDtypeStruct((nblk, 1, PB), F32),
                   jax.ShapeDtypeStruct((1, D), F32)],
        scratch_shapes=[pltpu.VMEM((S, PB), F32), pltpu.VMEM((S, PB), F32), pltpu.VMEM((S, PB), F32),
                        pltpu.VMEM((S, PB), F32), pltpu.VMEM((2 * SCAN_LANES, PB), F32),
                        pltpu.VMEM((2 * SCAN_LANES, PB), F32)],
        compiler_params=_cp(("arbitrary",)),
    )(u, dy, ar, ai, bre, bim, cre, cim, dskip)


def s5_glu_fwd(y, wglu, x, vec, name, tm=512):
    S, D = x.shape
    NJ = wglu.shape[-1]

    def body(y_ref, w_ref, x_ref, vec_ref, out_ref, f_ref):
        g = _gelu(y_ref[...]).astype(BF16)
        f = jnp.concatenate([_dot(g, w_ref[j]) * _sigmoid(_dot(g, w_ref[2 + j])) for j in range(2)], axis=1)
        f_ref[...] = f
        out_ref[...] = _post_fwd(x_ref[...], f, vec_ref, 1.0)

    row = pl.BlockSpec((tm, D), lambda i: (i, 0))
    return pl.pallas_call(
        body, name=name, grid=(S // tm,),
        in_specs=[row, pl.BlockSpec((N_CHIPS, D, NJ), lambda i: (0, 0, 0)), row, pl.BlockSpec((8, D), lambda i: (0, 0))],
        out_specs=[row, row],
        out_shape=[jax.ShapeDtypeStruct((S, D), F32), jax.ShapeDtypeStruct((S, D), F32)],
        compiler_params=_cp(("arbitrary",)),
    )(y, wglu, x, vec)


def s5_glu_bwd(dout, f, y, vec, wglu, name, tm=512):
    S, D = dout.shape
    NJ = wglu.shape[-1]

    def body(dout_ref, f_ref, y_ref, vec_ref, w_ref, dy_ref, dab_ref, g_ref, acc_ref):
        @pl.when(pl.program_id(0) == 0)
        def _():
            acc_ref[...] = jnp.zeros_like(acc_ref)

        df, dgate, dgpost = _post_bwd(dout_ref[...], f_ref[...], vec_ref, 1.0)
        yv = y_ref[...]
        g = _gelu(yv).astype(BF16)
        g_ref[...] = g
        dg = jnp.zeros((tm, D), F32)
        for j in range(2):
            a = _dot(g, w_ref[j])
            sig = _sigmoid(_dot(g, w_ref[2 + j]))
            dfj = df[:, j * NJ:(j + 1) * NJ]
            da = (dfj * sig).astype(BF16)
            db = (dfj * a * sig * (1.0 - sig)).astype(BF16)
            dab_ref[:, j * NJ:(j + 1) * NJ] = da
            dab_ref[:, (2 + j) * NJ:(3 + j) * NJ] = db
            dg = dg + _dot_nt(da, w_ref[j]) + _dot_nt(db, w_ref[2 + j])
        dy_ref[...] = dg * _gelu_grad(yv)
        _acc_add(acc_ref, A_GATE, dgate)
        _acc_add(acc_ref, A_GPOST, dgpost)

    row = pl.BlockSpec((tm, D), lambda i: (i, 0))
    const = pl.BlockSpec((8, D), lambda i: (0, 0))
    return pl.pallas_call(
        body, name=name, grid=(S // tm,),
        in_specs=[row, row, row, const, pl.BlockSpec((N_CHIPS, D, NJ), lambda i: (0, 0, 0))],
        out_specs=[row, pl.BlockSpec((tm, N_CHIPS * NJ), lambda i: (i, 0)), row, const],
        out_shape=[jax.ShapeDtypeStruct((S, D), F32), jax.ShapeDtypeStruct((S, N_CHIPS * NJ), BF16),
                   jax.ShapeDtypeStruct((S, D), BF16), jax.ShapeDtypeStruct((8, D), F32)],
        compiler_params=_cp(("arbitrary",)),
    )(dout, f, y, vec, wglu)


def loss_head(y, target, name, tm=512):
    S, D = y.shape

    def body(y_ref, t_ref, d_ref, l_ref):
        @pl.when(pl.program_id(0) == 0)
        def _():
            l_ref[...] = jnp.zeros_like(l_ref)

        err = y_ref[...] - t_ref[...]
        d_ref[...] = err * (1.0 / D)
        l_ref[...] += jnp.sum(_rowsum(err * err), axis=1, keepdims=True)

    row = pl.BlockSpec((tm, D), lambda i: (i, 0))
    return pl.pallas_call(
        body, name=name, grid=(S // tm,),
        in_specs=[row, row],
        out_specs=[row, pl.BlockSpec((1, 1), lambda i: (0, 0))],
        out_shape=[jax.ShapeDtypeStruct((S, D), F32), jax.ShapeDtypeStruct((1, 1), F32)],
        compiler_params=_cp(("arbitrary",)),
    )(y, target)


def zero_after(dep, name, shape=(8, 128)):
    def body(dep_ref, o_ref):
        o_ref[...] = jnp.zeros_like(o_ref)

    return pl.pallas_call(body, name=name, in_specs=[ANY], out_specs=pl.BlockSpec(memory_space=pltpu.VMEM),
                          out_shape=jax.ShapeDtypeStruct(shape, F32), compiler_params=_cp())(dep)


def ada_mod(c_all, ada_w, ada_b_shard, name, tn=768, dep=None):
    B, D = c_all.shape
    L, _, NS = ada_w.shape
    extra = [] if dep is None else [dep]

    def body(c_ref, w_ref, b_ref, *rest):
        cv = c_ref[...]
        cond = (cv * _sigmoid(cv)).astype(BF16)
        rest[-1][...] = _dot(cond, w_ref[...].astype(BF16)) + b_ref[...]

    return pl.pallas_call(
        body, name=name, grid=(L, NS // tn),
        in_specs=[pl.BlockSpec((B, D), lambda l, j: (0, 0)), pl.BlockSpec((None, D, tn), lambda l, j: (l, 0, j)),
                  pl.BlockSpec((None, 1, tn), lambda l, j: (l, 0, j))] + [ANY] * len(extra),
        out_specs=pl.BlockSpec((None, B, tn), lambda l, j: (l, 0, j)),
        out_shape=jax.ShapeDtypeStruct((L, B, NS), F32),
        compiler_params=_cp(("arbitrary", "arbitrary")),
    )(c_all, ada_w, ada_b_shard, *extra)


def ada_grad(c_all, dmod, name, tn=768):
    B, D = c_all.shape
    L, _, NS = dmod.shape

    def body(c_ref, d_ref, o_ref):
        cv = c_ref[...]
        cond = (cv * _sigmoid(cv)).astype(BF16)
        o_ref[...] = _dot_tn(cond, d_ref[...].astype(BF16))

    return pl.pallas_call(
        body, name=name, grid=(L, NS // tn),
        in_specs=[pl.BlockSpec((B, D), lambda l, j: (0, 0)), pl.BlockSpec((None, B, tn), lambda l, j: (l, 0, j))],
        out_specs=pl.BlockSpec((None, D, tn), lambda l, j: (l, 0, j)),
        out_shape=jax.ShapeDtypeStruct((L, D, NS), F32),
        compiler_params=_cp(("arbitrary", "arbitrary")),
    )(c_all, dmod)


def sum_gathered(land, own, me_arr, name):
    n, R, C = land.shape

    def body(me_ref, land_ref, own_ref, o_ref):
        acc = jnp.zeros(o_ref.shape, F32)
        for i in range(n):
            acc = acc + jnp.where(me_ref[0] == i, own_ref[...], land_ref[i]).astype(F32)
        o_ref[...] = acc

    tr = _row_tile(R, 64)
    grid_spec = pltpu.PrefetchScalarGridSpec(
        num_scalar_prefetch=1, grid=(R // tr,),
        in_specs=[pl.BlockSpec((n, tr, C), lambda i, me: (0, i, 0)), pl.BlockSpec((tr, C), lambda i, me: (i, 0))],
        out_specs=pl.BlockSpec((tr, C), lambda i, me: (i, 0)))
    return pl.pallas_call(
        body, name=name, grid_spec=grid_spec,
        out_shape=jax.ShapeDtypeStruct((R, C), F32),
        compiler_params=_cp(("arbitrary",)),
    )(me_arr, land, own)


def _row_tile(R, cap=512):
    if R <= cap:
        return R
    for cand in (512, 384, 352, 256, 128, 64, 32, 16, 8):
        if cand <= cap and R % cand == 0:
            return cand
    return R


def adamw(w, g, m, v, name, emit_grad=False, dep=None):
    R, C = w.shape
    tr = _row_tile(R, 256 if C > 1024 else 512)
    bc1 = 1.0 - ADAM_B1 ** ADAM_STEP
    bc2 = 1.0 - ADAM_B2 ** ADAM_STEP
    n_out = 4 if emit_grad else 3
    extra = [] if dep is None else [dep]

    def body(w_ref, g_ref, m_ref, v_ref, *outs):
        d_ref, nm_ref, nv_ref = outs[-3:]
        gv = g_ref[...]
        if emit_grad:
            outs[-4][...] = gv
        nm = ADAM_B1 * m_ref[...] + (1.0 - ADAM_B1) * gv
        nv = ADAM_B2 * v_ref[...] + (1.0 - ADAM_B2) * (gv * gv)
        nm_ref[...] = nm
        nv_ref[...] = nv
        d_ref[...] = -ADAM_LR * ((nm / bc1) / (jnp.sqrt(nv / bc2) + ADAM_EPS) + ADAM_WD * w_ref[...])

    blk = pl.BlockSpec((tr, C), lambda i: (i, 0))
    sd = jax.ShapeDtypeStruct((R, C), F32)
    return pl.pallas_call(
        body, name=name, grid=(R // tr,),
        in_specs=[blk, blk, blk, blk] + [ANY] * len(extra), out_specs=[blk] * n_out, out_shape=[sd] * n_out,
        compiler_params=_cp(("arbitrary",)),
    )(w, g, m, v, *extra)


def _as2d(a):
    if a.ndim == 1:
        return a.reshape(1, -1)
    return a.reshape(-1, a.shape[-1])


SMALL_PARAM_ELEMS = 1 << 18


def adamw_nd(w, g, m, v, name, emit_grad=False, dep=None):
    extra = [] if dep is None else [dep]
    if w.size <= SMALL_PARAM_ELEMS and not emit_grad:
        bc1 = 1.0 - ADAM_B1 ** ADAM_STEP
        bc2 = 1.0 - ADAM_B2 ** ADAM_STEP

        def body(w_ref, g_ref, m_ref, v_ref, *rest):
            d_ref, nm_ref, nv_ref = rest[-3:]
            gv = g_ref[...]
            nm = ADAM_B1 * m_ref[...] + (1.0 - ADAM_B1) * gv
            nv = ADAM_B2 * v_ref[...] + (1.0 - ADAM_B2) * (gv * gv)
            nm_ref[...] = nm
            nv_ref[...] = nv
            d_ref[...] = -ADAM_LR * ((nm / bc1) / (jnp.sqrt(nv / bc2) + ADAM_EPS) + ADAM_WD * w_ref[...])

        sd = jax.ShapeDtypeStruct(w.shape, F32)
        whole = pl.BlockSpec(memory_space=pltpu.VMEM)
        return tuple(pl.pallas_call(body, name=name, in_specs=[whole] * 4 + [ANY] * len(extra), out_specs=[whole] * 3,
                                    out_shape=[sd, sd, sd], compiler_params=_cp())(w, g.reshape(w.shape), m, v, *extra))
    outs = adamw(_as2d(w), _as2d(g.reshape(w.shape)), _as2d(m), _as2d(v), name, emit_grad=emit_grad, dep=dep)
    return tuple(o.reshape(w.shape) for o in outs)


def _place():
    x, y, c = lax.axis_index("x"), lax.axis_index("y"), lax.axis_index("c")
    chips = [(1 - x, y), (x, 1 - y), (1 - x, 1 - y)]
    return x, y, c, chips


def allgather_small(xs, name):
    m_per, n = xs.shape

    def body(x_ref, out_ref, send_sems, recv_sems, local_sem):
        x, y, c, chips = _place()
        me, sibling = (x, y, c), (x, y, 1 - c)

        def rows(px, py, pc):
            return out_ref.at[pl.ds((4 * px + 2 * py + pc) * m_per, m_per), :]

        def copy(k, block, to, src=None):
            return pltpu.make_async_remote_copy(
                src_ref=rows(*block) if src is None else src, dst_ref=rows(*block),
                send_sem=send_sems.at[k], recv_sem=recv_sems.at[k], device_id=to, device_id_type=MESH_T)

        mine = pltpu.make_async_copy(x_ref, rows(*me), local_sem)
        mine.start()
        first = [copy(0, me, sibling, src=x_ref)]
        first += [copy(1 + j, me, (*chip, c), src=x_ref) for j, chip in enumerate(chips)]
        for cp in first:
            cp.start()
        passed = [copy(4 + j, (*chip, c), sibling) for j, chip in enumerate(chips)]
        for j, chip in enumerate(chips):
            copy(1 + j, (*chip, c), me).wait_recv()
            passed[j].start()
        copy(0, sibling, me).wait_recv()
        for j, chip in enumerate(chips):
            copy(4 + j, (*chip, 1 - c), me).wait_recv()
        for cp in first + passed:
            cp.wait_send()
        mine.wait()

    return pl.pallas_call(
        body, name=name,
        out_shape=jax.ShapeDtypeStruct((N_DEV * m_per, n), xs.dtype),
        in_specs=[pl.BlockSpec(memory_space=pltpu.VMEM)],
        out_specs=pl.BlockSpec(memory_space=pltpu.VMEM),
        scratch_shapes=[pltpu.SemaphoreType.DMA((7,)), pltpu.SemaphoreType.DMA((7,)), pltpu.SemaphoreType.DMA],
        compiler_params=_cp(),
    )(xs)


def _half_rows(ref, half, rh):
    idx = (slice(None),) * (len(ref.shape) - 2) + (pl.ds(pl.multiple_of(half * rh, 16), rh), slice(None))
    return ref.at[idx]


def pair_sum(g, recv, cidx, name):
    L, NS, R, C = g.shape
    rh = R // 2
    tr = _row_tile(rh, 512)
    nt = rh // tr

    def body(c_ref, g_ref, r_ref, o_ref):
        o_ref[...] = (g_ref[...].astype(F32) + r_ref[...].astype(F32)).astype(BF16)

    grid_spec = pltpu.PrefetchScalarGridSpec(
        num_scalar_prefetch=1, grid=(L, NS, nt),
        in_specs=[pl.BlockSpec((None, None, tr, C), lambda l, s, t, c: (l, s, c[0] * nt + t, 0)),
                  pl.BlockSpec((None, None, tr, C), lambda l, s, t, c: (l, s, t, 0))],
        out_specs=pl.BlockSpec((None, None, tr, C), lambda l, s, t, c: (l, s, t, 0)))
    return pl.pallas_call(
        body, name=name, grid_spec=grid_spec,
        out_shape=jax.ShapeDtypeStruct((L, NS, rh, C), BF16),
        compiler_params=_cp(("arbitrary", "arbitrary", "arbitrary")),
    )(cidx, g, recv)


def chip_sum(part, recv, chip_c, name, dest=None, slot=0, n_slots=1):
    _, NS, RH, C = part.shape
    tr = _row_tile(RH, 512)
    nt = RH // tr

    def body(cc_ref, p_ref, r_ref, *rest):
        o_ref = rest[-1]
        acc = p_ref[...].astype(F32)
        for k in range(3):
            acc = acc + r_ref[k].astype(F32)
        o_ref[...] = acc

    in_specs = [pl.BlockSpec((None, None, tr, C), lambda t, cc: (0, cc[0], t, 0)),
                pl.BlockSpec((3, None, tr, C), lambda t, cc: (0, 0, t, 0))]
    args = [chip_c, part, recv]
    aliases = {}
    if dest is not None:
        in_specs.append(ANY)
        args.append(dest)
        aliases = {3: 0}
    grid_spec = pltpu.PrefetchScalarGridSpec(
        num_scalar_prefetch=1, grid=(nt,), in_specs=in_specs,
        out_specs=pl.BlockSpec((None, tr, C), lambda t, cc: (slot, cc[1] * nt + t, 0)))
    return pl.pallas_call(
        body, name=name, grid_spec=grid_spec,
        out_shape=jax.ShapeDtypeStruct((n_slots, 2 * RH, C), F32),
        input_output_aliases=aliases,
        compiler_params=_cp(("arbitrary",)),
    )(*args)


HBM_SPEC = pl.BlockSpec(memory_space=pltpu.HBM)
SEM_SPEC = pl.BlockSpec(memory_space=pltpu.SEMAPHORE)
DATAFLOW_EFFECT = pltpu.SideEffectType.DATAFLOW_SIDE_EFFECTING


def split_start(bufs, copies_fn, n_copies, name, dep=None, token_shape=(8, 128)):
    nb = len(bufs)
    extra = [] if dep is None else [dep]

    def body(*refs):
        ne = len(extra)
        send_sems, recv_sems = refs[nb + ne], refs[nb + ne + 1]
        token = refs[-1]
        for cp in copies_fn(refs[:nb], send_sems, recv_sems):
            cp.start()
        token[...] = jnp.zeros_like(token)

    outs = pl.pallas_call(
        body, name=name,
        out_shape=(pltpu.SemaphoreType.DMA((n_copies,)), pltpu.SemaphoreType.DMA((n_copies,)),
                   *[pltpu.HBM(b.shape, b.dtype) for b in bufs], jax.ShapeDtypeStruct(token_shape, F32)),
        in_specs=[HBM_SPEC] * nb + [ANY] * len(extra),
        out_specs=(SEM_SPEC, SEM_SPEC, *[HBM_SPEC] * nb, pl.BlockSpec(memory_space=pltpu.VMEM)),
        input_output_aliases={i: 2 + i for i in range(nb)},
        compiler_params=pltpu.CompilerParams(has_side_effects=DATAFLOW_EFFECT),
    )(*[pltpu.with_memory_space_constraint(b, pltpu.HBM) for b in bufs], *extra)
    return outs[0], outs[1], list(outs[2:2 + nb]), outs[-1]


def split_wait(send_sems, recv_sems, bufs, after, copies_fn, name):
    nb = len(bufs)

    def body(*refs):
        for cp in copies_fn(refs[:nb], refs[nb], refs[nb + 1]):
            cp.wait_send()
            cp.wait_recv()

    outs = pl.pallas_call(
        body, name=name,
        out_shape=tuple(pltpu.HBM(b.shape, b.dtype) for b in bufs),
        in_specs=[HBM_SPEC] * nb + [SEM_SPEC, SEM_SPEC, ANY],
        out_specs=tuple([HBM_SPEC] * nb),
        input_output_aliases={i: i for i in range(nb)},
        compiler_params=pltpu.CompilerParams(has_side_effects=DATAFLOW_EFFECT),
    )(*bufs, send_sems, recv_sems, after)
    return list(outs)


def chip_exchange_copies(refs, send_sems, recv_sems):
    n = len(refs) // 2
    x, y, c, chips = _place()
    cps = []
    for a in range(n):
        for k, (cx, cy) in enumerate(chips):
            cps.append(pltpu.make_async_remote_copy(
                src_ref=refs[a].at[:, 2 * cx + cy], dst_ref=refs[n + a].at[k],
                send_sem=send_sems.at[a * 3 + k], recv_sem=recv_sems.at[a * 3 + k],
                device_id=(cx, cy, c), device_id_type=MESH_T))
    return cps


def fill_copies(refs, send_sems, recv_sems):
    x, y, c, _ = _place()
    cps = []
    for a in range(len(refs)):
        rh = refs[a].shape[-2] // 2
        cps.append(pltpu.make_async_remote_copy(
            src_ref=_half_rows(refs[a], c, rh), dst_ref=_half_rows(refs[a], c, rh),
            send_sem=send_sems.at[a], recv_sem=recv_sems.at[a], device_id=(x, y, 1 - c), device_id_type=MESH_T))
    return cps


def allgather_direct_copies(refs, send_sems, recv_sems):
    x, y, c, _ = _place()
    me = 4 * x + 2 * y + c
    cps = []
    for r in range(1, N_DEV):
        fx, fy, fc = (r >> 2) & 1, (r >> 1) & 1, r & 1
        peer = (1 - x if fx else x, 1 - y if fy else y, 1 - c if fc else c)
        cps.append(pltpu.make_async_remote_copy(
            src_ref=refs[0], dst_ref=refs[1].at[me], send_sem=send_sems.at[r - 1], recv_sem=recv_sems.at[r - 1],
            device_id=peer, device_id_type=MESH_T))
    return cps


def pair_exchange_copies(refs, send_sems, recv_sems):
    n = len(refs) // 2
    x, y, c, _ = _place()
    return [pltpu.make_async_remote_copy(
        src_ref=_half_rows(refs[a], 1 - c, refs[a].shape[-2] // 2), dst_ref=refs[n + a],
        send_sem=send_sems.at[a], recv_sem=recv_sems.at[a], device_id=(x, y, 1 - c), device_id_type=MESH_T)
        for a in range(n)]


def weight_ici_copies(refs, send_sems, recv_sems):
    x, y, c, chips = _place()
    cps = []
    for a in range(len(refs)):
        for k, (cx, cy) in enumerate(chips):
            cps.append(pltpu.make_async_remote_copy(
                src_ref=refs[a].at[2 * x + y, c], dst_ref=refs[a].at[2 * x + y, c],
                send_sem=send_sems.at[a * 3 + k], recv_sem=recv_sems.at[a * 3 + k],
                device_id=(cx, cy, c), device_id_type=MESH_T))
    return cps


def weight_d2d_copies(refs, send_sems, recv_sems):
    x, y, c, chips = _place()
    cps = []
    for a in range(len(refs)):
        for k, (cx, cy) in enumerate(chips):
            cps.append(pltpu.make_async_remote_copy(
                src_ref=refs[a].at[2 * cx + cy, c], dst_ref=refs[a].at[2 * cx + cy, c],
                send_sem=send_sems.at[a * 3 + k], recv_sem=recv_sems.at[a * 3 + k],
                device_id=(x, y, 1 - c), device_id_type=MESH_T))
    return cps


def cast_place(w, lead, chip_arr, name, dep=None):
    R, C = w.shape[-2:]
    tr = _row_tile(R, 512)

    def body(chip_ref, w_ref, *rest):
        rest[-1][...] = w_ref[...].astype(BF16)

    in_specs = [pl.BlockSpec((None,) * len(lead) + (tr, C), lambda t, ch: (*lead, t, 0))]
    args = [chip_arr, w]
    if dep is not None:
        in_specs.append(ANY)
        args.append(dep)
    grid_spec = pltpu.PrefetchScalarGridSpec(
        num_scalar_prefetch=1, grid=(R // tr,), in_specs=in_specs,
        out_specs=pl.BlockSpec((None, tr, C), lambda t, ch: (ch[0], t, 0)))
    return pl.pallas_call(
        body, name=name, grid_spec=grid_spec,
        out_shape=jax.ShapeDtypeStruct((N_CHIPS, R, C), BF16),
        compiler_params=_cp(("arbitrary",)),
    )(*args)


def _s5_prepare(lam_re, lam_im, log_dt, b_re, b_im, c_re, c_im, groups_per_block):
    G, P = lam_re.shape
    N = b_re.shape[-1]
    gb = groups_per_block
    nblk = G // gb
    dt = jnp.exp(log_dt)[:, None]
    e = jnp.exp(lam_re * dt)
    a_re = e * jnp.cos(lam_im * dt)
    a_im = e * jnp.sin(lam_im * dt)
    n2 = lam_re * lam_re + lam_im * lam_im
    co_re = ((a_re - 1.0) * lam_re + a_im * lam_im) / n2
    co_im = (a_im * lam_re - (a_re - 1.0) * lam_im) / n2
    bb_re = co_re[..., None] * b_re - co_im[..., None] * b_im
    bb_im = co_re[..., None] * b_im + co_im[..., None] * b_re
    eye = jnp.eye(gb, dtype=F32)

    def blockdiag_np(m):
        m = m.reshape(nblk, gb, N, P)
        return jnp.einsum('bgnp,gh->bgnhp', m, eye).reshape(nblk, gb * N, gb * P)

    b_np_re = jnp.swapaxes(bb_re, 1, 2)
    b_np_im = jnp.swapaxes(bb_im, 1, 2)
    return (a_re.reshape(nblk, 1, gb * P), a_im.reshape(nblk, 1, gb * P),
            blockdiag_np(b_np_re), blockdiag_np(b_np_im), blockdiag_np(c_re), blockdiag_np(c_im))


def _to_scan_order(a):
    S, D = a.shape
    return a.reshape(SCAN_LANES, S // SCAN_LANES, D).transpose(1, 0, 2).reshape(S, D)


def _from_scan_order(a):
    S, D = a.shape
    return a.reshape(S // SCAN_LANES, SCAN_LANES, D).transpose(1, 0, 2).reshape(S, D)


def _pad_rows(a, mult=8):
    r = (-a.shape[0]) % mult
    if r:
        a = jnp.concatenate([a, jnp.zeros((r, a.shape[1]), a.dtype)], axis=0)
    return a


def _pack_rows(arrs, width):
    parts, offs, o = [], [], 0
    for a in arrs:
        flat = a.reshape(-1)
        r = (-flat.shape[0]) % (16 * width)
        if r:
            flat = jnp.concatenate([flat, jnp.zeros((r,), flat.dtype)])
        p = flat.reshape(-1, width)
        parts.append(p)
        offs.append((o, a.shape, a.size))
        o += p.shape[0]
    if o % 64:
        parts.append(jnp.zeros((64 - o % 64, width), parts[0].dtype))
    return jnp.concatenate(parts, axis=0), offs


def _unpack_rows(packed, offs):
    outs = []
    for o, shape, size in offs:
        rows = -(-size // packed.shape[1])
        outs.append(packed[o:o + rows].reshape(-1)[:size].reshape(shape))
    return outs


def kernel(x, c, ada_w, ada_b, norm_pre, norm_post, ffn_w_in, ffn_w_out, ab_w_in, pool_w, pool_scale, sgu_ln_g, sgu_ln_b, sgu_w, sgu_b, ab_w_out, ssm_w_in, ssm_lam_re, ssm_lam_im, ssm_b_re, ssm_b_im, ssm_c_re, ssm_c_im, ssm_d, ssm_log_dt, ssm_w_glu, loss_target, m_ada_w, m_ada_b, m_norm_pre, m_norm_post, m_ffn_w_in, m_ffn_w_out, m_ab_w_in, m_pool_w, m_pool_scale, m_sgu_ln_g, m_sgu_ln_b, m_sgu_w, m_sgu_b, m_ab_w_out, m_ssm_w_in, m_ssm_lam_re, m_ssm_lam_im, m_ssm_b_re, m_ssm_b_im, m_ssm_c_re, m_ssm_c_im, m_ssm_d, m_ssm_log_dt, m_ssm_w_glu, v_ada_w, v_ada_b, v_norm_pre, v_norm_post, v_ffn_w_in, v_ffn_w_out, v_ab_w_in, v_pool_w, v_pool_scale, v_sgu_ln_g, v_sgu_ln_b, v_sgu_w, v_sgu_b, v_ab_w_out, v_ssm_w_in, v_ssm_lam_re, v_ssm_lam_im, v_ssm_b_re, v_ssm_b_im, v_ssm_c_re, v_ssm_c_im, v_ssm_d, v_ssm_log_dt, v_ssm_w_glu):
    weights = dict(ada_w=ada_w, ada_b=ada_b, norm_pre=norm_pre, norm_post=norm_post, ffn_w_in=ffn_w_in,
                   ffn_w_out=ffn_w_out, ab_w_in=ab_w_in, pool_w=pool_w, pool_scale=pool_scale, sgu_ln_g=sgu_ln_g,
                   sgu_ln_b=sgu_ln_b, sgu_w=sgu_w, sgu_b=sgu_b, ab_w_out=ab_w_out, ssm_w_in=ssm_w_in,
                   ssm_lam_re=ssm_lam_re, ssm_lam_im=ssm_lam_im, ssm_b_re=ssm_b_re, ssm_b_im=ssm_b_im,
                   ssm_c_re=ssm_c_re, ssm_c_im=ssm_c_im, ssm_d=ssm_d, ssm_log_dt=ssm_log_dt, ssm_w_glu=ssm_w_glu)
    m_in = dict(ada_w=m_ada_w, ada_b=m_ada_b, norm_pre=m_norm_pre, norm_post=m_norm_post, ffn_w_in=m_ffn_w_in,
                ffn_w_out=m_ffn_w_out, ab_w_in=m_ab_w_in, pool_w=m_pool_w, pool_scale=m_pool_scale,
                sgu_ln_g=m_sgu_ln_g, sgu_ln_b=m_sgu_ln_b, sgu_w=m_sgu_w, sgu_b=m_sgu_b, ab_w_out=m_ab_w_out,
                ssm_w_in=m_ssm_w_in, ssm_lam_re=m_ssm_lam_re, ssm_lam_im=m_ssm_lam_im, ssm_b_re=m_ssm_b_re,
                ssm_b_im=m_ssm_b_im, ssm_c_re=m_ssm_c_re, ssm_c_im=m_ssm_c_im, ssm_d=m_ssm_d,
                ssm_log_dt=m_ssm_log_dt, ssm_w_glu=m_ssm_w_glu)
    v_in = dict(ada_w=v_ada_w, ada_b=v_ada_b, norm_pre=v_norm_pre, norm_post=v_norm_post, ffn_w_in=v_ffn_w_in,
                ffn_w_out=v_ffn_w_out, ab_w_in=v_ab_w_in, pool_w=v_pool_w, pool_scale=v_pool_scale,
                sgu_ln_g=v_sgu_ln_g, sgu_ln_b=v_sgu_ln_b, sgu_w=v_sgu_w, sgu_b=v_sgu_b, ab_w_out=v_ab_w_out,
                ssm_w_in=v_ssm_w_in, ssm_lam_re=v_ssm_lam_re, ssm_lam_im=v_ssm_lam_im, ssm_b_re=v_ssm_b_re,
                ssm_b_im=v_ssm_b_im, ssm_c_re=v_ssm_c_re, ssm_c_im=v_ssm_c_im, ssm_d=v_ssm_d,
                ssm_log_dt=v_ssm_log_dt, ssm_w_glu=v_ssm_w_glu)
    names = list(weights.keys())

    xi, yi, ci = lax.axis_index("x"), lax.axis_index("y"), lax.axis_index("c")
    chip = 2 * xi + yi
    me = 4 * xi + 2 * yi + ci
    S, D = x.shape[1], x.shape[2]
    L = ada_w.shape[0]
    NSUB = norm_pre.shape[1]
    DS = norm_pre.shape[2]
    FS = ffn_w_in.shape[-1]
    FR = ffn_w_out.shape[-2]
    x0 = x[0]
    target = loss_target[0]

    small_parts = [_pad_rows(p) for p in (c.reshape(D // DS, DS), norm_pre.reshape(L * NSUB, DS),
                                          norm_post.reshape(L * NSUB, DS), ssm_d.reshape(1, DS))]
    small_offs = [0]
    for p in small_parts:
        small_offs.append(small_offs[-1] + p.shape[0])
    small_all = allgather_small(jnp.concatenate(small_parts, axis=0), "ag_small").reshape(N_DEV, -1, DS)
    c_all = small_all[:, :D // DS].reshape(N_DEV, D)
    per_chip = small_all[0::2]
    o = small_offs[1]
    g_pre_full = jnp.moveaxis(per_chip[:, o:o + L * NSUB], 0, 1).reshape(L, NSUB, D)
    o = small_offs[2]
    g_post_full = jnp.moveaxis(per_chip[:, o:o + L * NSUB], 0, 1).reshape(L, NSUB, D)
    o = small_offs[3]
    d_full = jnp.moveaxis(per_chip[:, o:o + 1], 0, 1).reshape(1, D)

    pieces = []
    for l in range(L):
        pieces.append((f"ffn_{l}_0", [(ffn_w_in, (l, 0)), (ffn_w_out, (l, 0))]))
        if l % 2 == 0:
            pieces.append((f"mix_{l}", [(ab_w_in, (l // 2,)), (ab_w_out, (l // 2,))]))
        else:
            pieces.append((f"mix_{l}", [(ssm_w_in, (l // 2,)), (ssm_w_glu, (l // 2,))]))
        pieces.append((f"ffn_{l}_1", [(ffn_w_in, (l, 1)), (ffn_w_out, (l, 1))]))
    tags = [tag for tag, _ in pieces]
    chip_arr = chip.reshape(1).astype(jnp.int32)
    wg_started, wg_passing = {}, {}

    def cast_piece(tag, ws, dep):
        wg_started[tag] = [cast_place(w, lead, chip_arr, f"wg_cast_{tag}_{i}", dep=dep).reshape(
            N_CHIPS, 2, w.shape[-2] // 2, w.shape[-1]) for i, (w, lead) in enumerate(ws)]

    def start_piece(tag, dep):
        bufs = wg_started[tag]
        wg_started[tag] = split_start(bufs, weight_ici_copies, 3 * len(bufs), f"wg_ici_start_{tag}", dep=dep,
                                      token_shape=(8, D))
        return wg_started[tag][3]

    def weights_arrived(i, after):
        ssem, rsem, bufs, _ = wg_started[tags[i]]
        lands = split_wait(ssem, rsem, bufs, after, weight_ici_copies, f"wg_ici_wait_{tags[i]}")
        wg_passing[i] = split_start(lands, weight_d2d_copies, 3 * len(lands), f"wg_d2d_start_{tags[i]}")
        return wg_passing[i][3][0, 0]

    def weights_of(i, after):
        ssem, rsem, lands, _ = wg_passing[i]
        return split_wait(ssem, rsem, lands, after, weight_d2d_copies, f"wg_d2d_wait_{tags[i]}")

    NS = ada_w.shape[-1]
    ada_b_shard = lax.dynamic_slice_in_dim(ada_b, chip * NS, NS, axis=1).reshape(L, 1, NS)
    mod_part = ada_mod(c_all, ada_w, ada_b_shard, "ada_mod")
    mod_all = allgather_small(mod_part.reshape(L * N_DEV, NS), "ag_mod").reshape(N_DEV, L, N_DEV, NS)
    mod_mine = lax.dynamic_index_in_dim(mod_all[0::2], me, axis=2, keepdims=False)
    mod = jnp.moveaxis(mod_mine, 0, 1).reshape(L, NSUB, 3, D)
    cast_piece(*pieces[0], mod_all)
    first_token = start_piece(tags[0], mod_all)
    wg_tokens = [first_token]
    for tag, ws in pieces[1:]:
        cast_piece(tag, ws, first_token)
    for tag in tags[1:WG_AHEAD]:
        wg_tokens.append(start_piece(tag, first_token))

    def vec_of(l, s):
        return jnp.concatenate([mod[l, s], g_pre_full[l, s][None], g_post_full[l, s][None],
                                jnp.zeros((3, D), F32)], axis=0)

    GB = 8
    s5_args = (ssm_lam_re[0], ssm_lam_im[0], ssm_log_dt[0], ssm_b_re[0], ssm_b_im[0], ssm_c_re[0], ssm_c_im[0])
    (a_re, a_im, bblk_re, bblk_im, cblk_re, cblk_im), s5_vjp = jax.vjp(lambda *p: _s5_prepare(*p, GB), *s5_args)
    bre16, bim16 = bblk_re.astype(BF16), bblk_im.astype(BF16)
    cre16, cim16 = cblk_re.astype(BF16), cblk_im.astype(BF16)
    cre16_t, cim16_t = jnp.swapaxes(cre16, 1, 2), jnp.swapaxes(cim16, 1, 2)

    pool_w16 = pool_w[0].astype(BF16)
    sgu_bexp = jnp.broadcast_to(sgu_b[0][:, :, None], sgu_w[0].shape)

    saved = {}
    ffn_w = {}
    xcur = x0
    stage = [0]

    def next_weights(after, vec):
        i = stage[0]
        stage[0] += 1
        weights_arrived(i, after)
        if i + WG_AHEAD < len(tags):
            vec = vec + start_piece(tags[i + WG_AHEAD], wg_passing[i][3])
        return weights_of(i, after), vec

    for l in range(L):
        v0 = vec_of(l, 0)
        g, v0p = next_weights(xcur if l else sum(wg_tokens, v0), v0)
        ffn_w[(l, 0)] = (g[0].reshape(N_CHIPS, D, FS), g[1].reshape(N_CHIPS, FR, D))
        out, h, u, f = ffn_fwd(xcur, v0p, *ffn_w[(l, 0)], f"ffn_fwd_{l}_0")
        saved[(l, 0)] = (xcur, v0, h, u, f)
        xcur = out
        v1 = vec_of(l, 1)
        mix_g, v1p = next_weights(xcur, v1)
        if l % 2 == 0:
            abin_g = mix_g[0].reshape(N_CHIPS, D, -1)
            about_g = mix_g[1].reshape(-1, D)
            h, z = pre_matmul(xcur, v1p, abin_g, f"mixa_in_{l}")
            ycat = mixa_core_fwd(z, pool_w16, pool_scale, sgu_ln_g, sgu_ln_b, sgu_w[0], sgu_bexp, f"mixa_core_{l}")
            out, f = matmul_post(ycat, about_g, xcur, v1, f"mixa_out_{l}")
            saved[(l, 1)] = (xcur, v1, h, z, ycat, f)
        else:
            sin_g = mix_g[0].reshape(1, -1, D)
            glu_g = mix_g[1].reshape(N_CHIPS, D, -1)
            h, uu = pre_matmul(xcur, v1p, sin_g, f"s5_in_{l}")
            us = _to_scan_order(uu)
            ys = s5_core_fwd(us, a_re, a_im, bre16, bim16, cre16_t, cim16_t, d_full, f"s5_core_{l}")
            yy = _from_scan_order(ys)
            out, f = s5_glu_fwd(yy, glu_g, xcur, v1, f"s5_glu_{l}")
            saved[(l, 1)] = (xcur, v1, h, us, yy, f)
        xcur = out
        v2 = vec_of(l, 2)
        g, v2p = next_weights(xcur, v2)
        ffn_w[(l, 1)] = (g[0].reshape(N_CHIPS, D, FS), g[1].reshape(N_CHIPS, FR, D))
        out, h, u, f = ffn_fwd(xcur, v2p, *ffn_w[(l, 1)], f"ffn_fwd_{l}_1")
        saved[(l, 2)] = (xcur, v2, h, u, f)
        xcur = out

    dcur, sq = loss_head(xcur, target, "loss_head")
    loss = lax.psum(sq[0, 0], ("x", "y", "c")) * (0.5 / D)

    accs = {}
    small_g = {}
    cidx = ci.reshape(1).astype(jnp.int32)
    chip_c = jnp.stack([chip, ci]).astype(jnp.int32)
    rs_open, rs_pending = [], []

    pin = [zero_after(loss.reshape(1, 1), "after_loss", (8, D))]
    last_start = [pin[0]]

    def pin_behind(token):
        pin.append(token)
        last_start[0] = token

    def pinned(v):
        for token in pin:
            v = v + (token if token.shape == v.shape else token[0, 0])
        pin.clear()
        return v

    def rs_begin(tag, items):
        arrs = [it[0] for it in items]
        lands = [lax.empty(a.shape[:-2] + (a.shape[-2] // 2, a.shape[-1]), BF16) for a in arrs]
        started = split_start(arrs + lands, pair_exchange_copies, len(arrs), f"rs_pair_start_{tag}",
                              token_shape=(8, D))
        rs_open.append((tag, items, started))
        pin_behind(started[3])

    def rs_advance(after):
        while rs_open:
            tag, items, (ssem, rsem, bufs, _) = rs_open.pop(0)
            n = len(items)
            bufs = split_wait(ssem, rsem, bufs, after, pair_exchange_copies, f"rs_pair_wait_{tag}")
            parts = [pair_sum(a, r, cidx, f"rs_pair_sum_{tag}_{i}") for i, (a, r) in enumerate(zip(bufs[:n], bufs[n:]))]
            lands = [lax.empty((3, 1) + p.shape[2:], BF16) for p in parts]
            started = split_start(parts + lands, chip_exchange_copies, 3 * n, f"rs_chip_start_{tag}",
                                  token_shape=(8, D))
            rs_pending.append((tag, items, started))
            pin_behind(started[3])

    def ffn_back(l, s, k, dcur, before_wgrads=None):
        xin, vv, h, u, f = saved[(l, s)]
        vv = pinned(vv)
        dx, df, du, act, acc = ffn_bwd(dcur, xin, f, u, vv, *ffn_w[(l, k)], f"ffn_bwd_{l}_{k}")
        accs[(l, s)] = acc
        rs_advance(dx)
        dep = None if before_wgrads is None else before_wgrads()
        g_win = tn_matmul(h, du, 1024, f"ffn_dwin_{l}_{k}", dep=dep)
        g_wout = tn_matmul_cols(act, df, FS, D, f"ffn_dwout_{l}_{k}")
        rs_begin(f"ffn_{l}_{k}", [(g_win[None], "ffn_w_in", 2 * l + k, 2 * L),
                                  (g_wout.reshape(1, N_CHIPS, FR, D), "ffn_w_out", 2 * l + k, 2 * L)])
        return dx

    grads = {}

    small_names = ["pool_w", "pool_scale", "sgu_ln_g", "sgu_ln_b", "sgu_w", "sgu_b", "ssm_lam_re", "ssm_lam_im",
                   "ssm_b_re", "ssm_b_im", "ssm_c_re", "ssm_c_im", "ssm_log_dt"]
    small_open = []

    def small_grads_start():
        acc_all = jnp.stack([jnp.stack([accs[(l, s)] for s in range(NSUB)]) for l in range(L)])
        dmod_mine = acc_all[:, :, A_SHIFT:A_GATE + 1].reshape(L, NSUB * 3 * D)
        packed, offs = _pack_rows([dmod_mine, acc_all[:, :, A_GPRE], acc_all[:, :, A_GPOST], dd_mine]
                                  + [small_g[n] for n in small_names], D)
        packed = pinned(packed).astype(BF16)
        started = split_start([packed, lax.empty((N_DEV,) + packed.shape, BF16)], allgather_direct_copies,
                              N_DEV - 1, "ag_grads_start")
        small_open.append((started, offs))
        pin_behind(started[3])
        return started[3]

    def small_grads_finish(after):
        (ssem, rsem, bufs, _), offs = small_open.pop()
        own, land = split_wait(ssem, rsem, bufs, after, allgather_direct_copies, "ag_grads_wait")
        me_arr = me.reshape(1).astype(jnp.int32)
        summed = _unpack_rows(sum_gathered(land, own, me_arr, "sum_small"), offs)
        grads.update({n: g for n, g in zip(small_names, summed[4:])})
        grads["ada_b"] = summed[0]
        grads["norm_pre"] = lax.dynamic_slice_in_dim(summed[1], chip * DS, DS, axis=2)
        grads["norm_post"] = lax.dynamic_slice_in_dim(summed[2], chip * DS, DS, axis=2)
        grads["ssm_d"] = lax.dynamic_slice_in_dim(summed[3], chip * DS, DS, axis=1)
        nm = L * NSUB * 3
        is_me = (jnp.arange(N_DEV) == me)[:, None, None]
        dmod_all = jnp.where(is_me, own[None, :nm], land[:, :nm]).reshape(N_DEV, L, NSUB * 3 * D)
        dmod_shard = lax.dynamic_slice_in_dim(jnp.moveaxis(dmod_all, 0, 1), chip * NS, NS, axis=2)
        grads["ada_w"] = ada_grad(c_all, dmod_shard, "ada_grad")

    for l in reversed(range(L)):
        dcur = ffn_back(l, 2, 1, dcur)
        if l % 2 == 0:
            xin, vv, h, z, ycat, f = saved[(l, 1)]
            vv = pinned(vv)
            df, dact, acc_post = post_bwd_matmul(dcur, f, vv, about_g, f"mixa_out_bwd_{l}")
            rs_advance(dact)
            g_about = tn_matmul_cols(ycat, df, 512, D, f"mixa_dwout_{l}")
            dz, dpw, dvecw, dsw, dsb = mixa_core_bwd(z, dact, pool_w16, pool_scale, sgu_ln_g, sgu_ln_b, sgu_w[0],
                                                     sgu_bexp, f"mixa_core_bwd_{l}")
            g_abin = tn_matmul_cols(h, dz, 512, abin_g.shape[-1], f"mixa_dwin_{l}")
            dcur, acc_pre = matmul_pre_bwd(dz, abin_g, xin, dcur, vv, f"mixa_in_bwd_{l}")
            accs[(l, 1)] = acc_pre + acc_post
            rs_begin(f"mixa_{l}", [(g_abin[None], "ab_w_in", 0, 1),
                                   (g_about.reshape(1, N_CHIPS, -1, D), "ab_w_out", 0, 1)])
            small_g.update(pool_w=dpw[None], pool_scale=dvecw[0:1], sgu_ln_g=dvecw[1:2], sgu_ln_b=dvecw[2:3],
                           sgu_w=dsw[None], sgu_b=dsb.T[None])
        else:
            xin, vv, h, us, yy, f = saved[(l, 1)]
            vv = pinned(vv)
            dy, dab, gact, acc_post = s5_glu_bwd(dcur, f, yy, vv, glu_g, f"s5_glu_bwd_{l}")
            rs_advance(dy)
            g_glu = tn_matmul_cols(gact, dab, 512, glu_g.shape[-1], f"s5_dwglu_{l}")
            dys = _to_scan_order(dy)
            dus, dbre, dbim, dcre_t, dcim_t, dar, dai, dd = s5_core_bwd(
                us, dys, a_re, a_im, bre16, bim16, cre16, cim16, d_full, f"s5_core_bwd_{l}")
            du = _from_scan_order(dus).astype(BF16)
            g_sin = tn_matmul_cols(h, du, 512, D, f"s5_dwin_{l}")
            dcur, acc_pre = matmul_pre_bwd(du, sin_g, xin, dcur, vv, f"s5_in_bwd_{l}")
            accs[(l, 1)] = acc_pre + acc_post
            rs_begin(f"s5_{l}", [(g_sin.reshape(1, N_CHIPS, -1, D), "ssm_w_in", 0, 1), (g_glu[None], "ssm_w_glu", 0, 1)])
            s5_grads = s5_vjp((dar, dai, dbre, dbim, jnp.swapaxes(dcre_t, 1, 2), jnp.swapaxes(dcim_t, 1, 2)))
            small_g.update(ssm_lam_re=s5_grads[0][None], ssm_lam_im=s5_grads[1][None], ssm_log_dt=s5_grads[2][None],
                           ssm_b_re=s5_grads[3][None], ssm_b_im=s5_grads[4][None], ssm_c_re=s5_grads[5][None],
                           ssm_c_im=s5_grads[6][None])
            dd_mine = dd
        dcur = ffn_back(l, 0, 0, dcur, before_wgrads=small_grads_start if l == 0 else None)
    grad_x = dcur[None]
    rs_advance(last_start[0])
    small_grads_finish(last_start[0])

    big_names = ["ffn_w_in", "ffn_w_out", "ab_w_in", "ab_w_out", "ssm_w_in", "ssm_w_glu"]
    deltas, new_m, new_v = {}, {}, {}

    def update(n, dep=None):
        g = grads[n].reshape(weights[n].shape)
        outs = adamw_nd(weights[n], g, m_in[n], v_in[n], f"adamw_{n}", emit_grad=n in big_names, dep=dep)
        grads[n] = outs[0] if n in big_names else g
        deltas[n], new_m[n], new_v[n] = outs[-3:]

    fin = {}

    def chip_sums(pending, after):
        for tag, items, (ssem, rsem, bufs, _) in pending:
            bufs = split_wait(ssem, rsem, bufs, after, chip_exchange_copies, f"rs_chip_wait_{tag}")
            n = len(items)
            for i, (it, p, r) in enumerate(zip(items, bufs[:n], bufs[n:])):
                fin[it[1]] = chip_sum(p, r, chip_c, f"rs_chip_sum_{tag}_{i}", dest=fin.get(it[1]), slot=it[2],
                                      n_slots=it[3])

    chip_sums(rs_pending[:-1], pinned(grads["ada_b"]))
    update("ada_w", dep=fin["ab_w_in"])
    chip_sums(rs_pending[-1:], deltas["ada_w"])
    ssem, rsem, bufs, fill_token = split_start([fin[n] for n in big_names], fill_copies, len(big_names), "rs_fill_start")
    for n in names:
        if n not in big_names and n != "ada_w":
            update(n, dep=fill_token)
    filled = split_wait(ssem, rsem, bufs, deltas["ssm_log_dt"], fill_copies, "rs_fill_wait")
    for n, g in zip(big_names, filled):
        grads[n] = g
        update(n)

    return (loss, grad_x, *[grads[n] for n in names], *[deltas[n] for n in names],
            *[new_m[n] for n in names], *[new_v[n] for n in names])
```

```python
import math

import jax
import jax.numpy as jnp
from jax import lax
from jax.experimental import pallas as pl
from jax.experimental.pallas import tpu as pltpu

F32 = jnp.float32
BF16 = jnp.bfloat16
EPS = 1e-6
MESH_T = pl.DeviceIdType.MESH
VMEM_LIMIT_BYTES = 56 * 1024 * 1024
N_CHIPS = 4
N_DEV = 8
POOL_WINDOWS = (2, 4, 8, 16)
CHUNK = 128
SCAN_LANES = 8
SCAN_UNROLL = 4
WG_AHEAD = 2
FFN_BWD_CHUNK = 2048
ADAM_LR = 0.001
ADAM_B1 = 0.9
ADAM_B2 = 0.999
ADAM_EPS = 1e-08
ADAM_WD = 0.01
ADAM_STEP = 10
GELU_C = math.sqrt(2.0 / math.pi)
GELU_K = 0.044715

V_SHIFT, V_SCALE, V_GATE, V_GPRE, V_GPOST = 0, 1, 2, 3, 4
A_SHIFT, A_SCALE, A_GATE, A_GPRE, A_GPOST = 0, 1, 2, 3, 4

ANY = pl.BlockSpec(memory_space=pl.ANY)


def _cp(sem=None):
    if sem is None:
        return pltpu.CompilerParams(vmem_limit_bytes=VMEM_LIMIT_BYTES)
    return pltpu.CompilerParams(vmem_limit_bytes=VMEM_LIMIT_BYTES, dimension_semantics=sem)


def _dot(a, b):
    return jnp.dot(a, b, preferred_element_type=F32)


def _dot_nt(a, b):
    return lax.dot_general(a, b, (((1,), (1,)), ((), ())), preferred_element_type=F32)


def _dot_tn(a, b):
    return lax.dot_general(a, b, (((0,), (0,)), ((), ())), preferred_element_type=F32)


def _sigmoid(x):
    return 1.0 / (1.0 + jnp.exp(-x))


def _gelu(x):
    return 0.5 * x * (1.0 + jnp.tanh(GELU_C * (x + GELU_K * x * x * x)))


def _gelu_grad(x):
    t = jnp.tanh(GELU_C * (x + GELU_K * x * x * x))
    return 0.5 * (1.0 + t) + 0.5 * x * (1.0 - t * t) * GELU_C * (1.0 + 3.0 * GELU_K * x * x)


def _rowsum(v):
    return jnp.sum(v, axis=0, keepdims=True)


def _lanemean(v):
    return jnp.mean(v, axis=-1, keepdims=True)


def _row(ref, i):
    return ref[pl.ds(i, 1), :]


def _pre_fwd(x, vec_ref):
    r = lax.rsqrt(_lanemean(x * x) + EPS)
    return (x * r) * _row(vec_ref, V_GPRE) * (1.0 + _row(vec_ref, V_SCALE)) + _row(vec_ref, V_SHIFT)


def _pre_bwd(x, dh, vec_ref):
    g = _row(vec_ref, V_GPRE)
    sc = 1.0 + _row(vec_ref, V_SCALE)
    r = lax.rsqrt(_lanemean(x * x) + EPS)
    xn = x * r
    dhx = dh * xn
    t2 = dh * (g * sc)
    dx = r * (t2 - xn * _lanemean(t2 * xn))
    return dx, _rowsum(dh), _rowsum(dhx * g), _rowsum(dhx * sc)


def _post_fwd(x, f, vec_ref, rw):
    q = lax.rsqrt(_lanemean(f * f) + EPS)
    return x + (rw * _row(vec_ref, V_GATE)) * (f * q * _row(vec_ref, V_GPOST))


def _post_bwd(dout, f, vec_ref, rw):
    gp = _row(vec_ref, V_GPOST)
    gate = _row(vec_ref, V_GATE)
    q = lax.rsqrt(_lanemean(f * f) + EPS)
    fhat = f * q
    dgate = _rowsum(dout * (rw * fhat * gp))
    dy = dout * (rw * gate)
    dgpost = _rowsum(dy * fhat)
    t = dy * gp
    df = q * (t - fhat * _lanemean(t * fhat))
    return df, dgate, dgpost


def _acc_add(acc_ref, row, v):
    acc_ref[pl.ds(row, 1), :] += v


def ffn_fwd(x, vec, win_g, wout_g, name, tm=512):
    S, D = x.shape
    FS = win_g.shape[-1]
    FR = wout_g.shape[-2]

    def body(x_ref, vec_ref, win_hbm, wout_hbm, out_ref, h_ref, u_ref, f_ref, win_s, wout_s, sem):
        def weight_copies():
            cps = [pltpu.make_async_copy(win_hbm.at[j], win_s.at[j], sem.at[j]) for j in range(N_CHIPS)]
            return cps + [pltpu.make_async_copy(wout_hbm.at[j], wout_s.at[pl.ds(j * FR, FR), :], sem.at[N_CHIPS + j])
                          for j in range(N_CHIPS)]

        @pl.when(pl.program_id(0) == 0)
        def _():
            for cp in weight_copies():
                cp.start()

        xv = x_ref[...]
        h = _pre_fwd(xv, vec_ref).astype(BF16)
        h_ref[...] = h

        @pl.when(pl.program_id(0) == 0)
        def _():
            for cp in weight_copies():
                cp.wait()

        f = jnp.zeros((tm, D), F32)
        for j in range(2):
            a = _dot(h, win_s[j])
            b = _dot(h, win_s[2 + j])
            u_ref[j] = a.astype(BF16)
            u_ref[2 + j] = b.astype(BF16)
            act = (a * _sigmoid(a) * b).astype(BF16)
            f = f + _dot(act, wout_s[pl.ds(j * FS, FS), :])
        f_ref[...] = f
        out_ref[...] = _post_fwd(xv, f, vec_ref, 0.5)

    row = pl.BlockSpec((tm, D), lambda i: (i, 0))
    return pl.pallas_call(
        body, name=name, grid=(S // tm,),
        in_specs=[row, pl.BlockSpec((8, D), lambda i: (0, 0)), ANY, ANY],
        out_specs=[row, row, pl.BlockSpec((N_CHIPS, tm, FS), lambda i: (0, i, 0)), row],
        out_shape=[jax.ShapeDtypeStruct((S, D), F32), jax.ShapeDtypeStruct((S, D), BF16),
                   jax.ShapeDtypeStruct((N_CHIPS, S, FS), BF16), jax.ShapeDtypeStruct((S, D), F32)],
        scratch_shapes=[pltpu.VMEM((N_CHIPS, D, FS), BF16), pltpu.VMEM((N_CHIPS * FR, D), BF16),
                        pltpu.SemaphoreType.DMA((2 * N_CHIPS,))],
        compiler_params=_cp(("arbitrary",)),
    )(x, vec, win_g, wout_g)


def ffn_bwd(dout, x, f, u, vec, win_g, wout_g, name, tm=256):
    S, D = x.shape
    FS = win_g.shape[-1]
    FR = wout_g.shape[-2]

    def body(dout_ref, x_ref, f_ref, u_ref, vec_ref, win_hbm, wout_hbm,
             dx_ref, df_ref, du_ref, act_ref, acc_ref, win_s, wout_s, sem):
        def weight_copies():
            cps = [pltpu.make_async_copy(win_hbm.at[j], win_s.at[j], sem.at[j]) for j in range(N_CHIPS)]
            return cps + [pltpu.make_async_copy(wout_hbm.at[j], wout_s.at[pl.ds(j * FR, FR), :], sem.at[N_CHIPS + j])
                          for j in range(N_CHIPS)]

        @pl.when(pl.program_id(0) == 0)
        def _():
            for cp in weight_copies():
                cp.start()
            acc_ref[...] = jnp.zeros_like(acc_ref)

        dout_v = dout_ref[...]
        df, dgate, dgpost = _post_bwd(dout_v, f_ref[...], vec_ref, 0.5)
        dfb = df.astype(BF16)
        df_ref[...] = dfb

        @pl.when(pl.program_id(0) == 0)
        def _():
            for cp in weight_copies():
                cp.wait()

        dh = jnp.zeros((tm, D), F32)
        chunks = [(c0, min(FFN_BWD_CHUNK, FS - c0)) for c0 in range(0, FS, FFN_BWD_CHUNK)]
        for j in range(2):
            for c0, cw in chunks:
                a = u_ref[j, :, c0:c0 + cw].astype(F32)
                b = u_ref[2 + j, :, c0:c0 + cw].astype(F32)
                sig = _sigmoid(a)
                sl = a * sig
                dact = _dot_nt(dfb, wout_s[pl.ds(j * FS + c0, cw), :])
                da = (dact * b * (sig * (1.0 + a * (1.0 - sig)))).astype(BF16)
                db = (dact * sl).astype(BF16)
                du_ref[j, :, c0:c0 + cw] = da
                du_ref[2 + j, :, c0:c0 + cw] = db
                act_ref[:, j * FS + c0:j * FS + c0 + cw] = (sl * b).astype(BF16)
                dh = dh + _dot_nt(da, win_s[j, :, c0:c0 + cw]) + _dot_nt(db, win_s[2 + j, :, c0:c0 + cw])
        dx, dshift, dscale, dgpre = _pre_bwd(x_ref[...], dh, vec_ref)
        dx_ref[...] = dout_v + dx
        _acc_add(acc_ref, A_SHIFT, dshift)
        _acc_add(acc_ref, A_SCALE, dscale)
        _acc_add(acc_ref, A_GATE, dgate)
        _acc_add(acc_ref, A_GPRE, dgpre)
        _acc_add(acc_ref, A_GPOST, dgpost)

    row = pl.BlockSpec((tm, D), lambda i: (i, 0))
    ublk = pl.BlockSpec((N_CHIPS, tm, FS), lambda i: (0, i, 0))
    const = pl.BlockSpec((8, D), lambda i: (0, 0))
    return pl.pallas_call(
        body, name=name, grid=(S // tm,),
        in_specs=[row, row, row, ublk, const, ANY, ANY],
        out_specs=[row, row, ublk, pl.BlockSpec((tm, 2 * FS), lambda i: (i, 0)), const],
        out_shape=[jax.ShapeDtypeStruct((S, D), F32), jax.ShapeDtypeStruct((S, D), BF16),
                   jax.ShapeDtypeStruct((N_CHIPS, S, FS), BF16), jax.ShapeDtypeStruct((S, 2 * FS), BF16),
                   jax.ShapeDtypeStruct((8, D), F32)],
        scratch_shapes=[pltpu.VMEM((N_CHIPS, D, FS), BF16), pltpu.VMEM((N_CHIPS * FR, D), BF16),
                        pltpu.SemaphoreType.DMA((2 * N_CHIPS,))],
        compiler_params=_cp(("arbitrary",)),
    )(dout, x, f, u, vec, win_g, wout_g)


def tn_matmul(a, b, tk, name, dep=None):
    S, K = a.shape
    nb, _, tn = b.shape
    extra = [] if dep is None else [dep]

    def body(a_ref, b_ref, *rest):
        rest[-1][...] = _dot_tn(a_ref[...], b_ref[...]).astype(BF16)

    return pl.pallas_call(
        body, name=name, grid=(K // tk, nb),
        in_specs=[pl.BlockSpec((S, tk), lambda i, j: (0, i)), pl.BlockSpec((None, S, tn), lambda i, j: (j, 0, 0))]
        + [ANY] * len(extra),
        out_specs=pl.BlockSpec((None, tk, tn), lambda i, j: (j, i, 0)),
        out_shape=jax.ShapeDtypeStruct((nb, K, tn), BF16),
        compiler_params=_cp(("arbitrary", "arbitrary")),
    )(a, b, *extra)


def tn_matmul_cols(a, b, tk, tn, name):
    S, K = a.shape
    N = b.shape[1]

    def body(a_ref, b_ref, o_ref):
        o_ref[...] = _dot_tn(a_ref[...], b_ref[...]).astype(BF16)

    return pl.pallas_call(
        body, name=name, grid=(K // tk, N // tn),
        in_specs=[pl.BlockSpec((S, tk), lambda i, j: (0, i)), pl.BlockSpec((S, tn), lambda i, j: (0, j))],
        out_specs=pl.BlockSpec((None, tk, tn), lambda i, j: (j, i, 0)),
        out_shape=jax.ShapeDtypeStruct((N // tn, K, tn), BF16),
        compiler_params=_cp(("arbitrary", "arbitrary")),
    )(a, b)


def pre_matmul(x, vec, w3, name, tm=512):
    S, D = x.shape
    nj, _, Nj = w3.shape

    def body(x_ref, vec_ref, w_ref, h_ref, z_ref):
        h = _pre_fwd(x_ref[...], vec_ref).astype(BF16)
        h_ref[...] = h
        for j in range(nj):
            z_ref[:, j * Nj:(j + 1) * Nj] = _dot(h, w_ref[j])

    row = pl.BlockSpec((tm, D), lambda i: (i, 0))
    return pl.pallas_call(
        body, name=name, grid=(S // tm,),
        in_specs=[row, pl.BlockSpec((8, D), lambda i: (0, 0)), pl.BlockSpec((nj, D, Nj), lambda i: (0, 0, 0))],
        out_specs=[row, pl.BlockSpec((tm, nj * Nj), lambda i: (i, 0))],
        out_shape=[jax.ShapeDtypeStruct((S, D), BF16), jax.ShapeDtypeStruct((S, nj * Nj), F32)],
        compiler_params=_cp(("arbitrary",)),
    )(x, vec, w3)


def matmul_pre_bwd(dz, w3, x, dres, vec, name, tm=512):
    S, D = x.shape
    nj, _, Nj = w3.shape

    def body(dz_ref, w_ref, x_ref, dres_ref, vec_ref, dx_ref, acc_ref):
        @pl.when(pl.program_id(0) == 0)
        def _():
            acc_ref[...] = jnp.zeros_like(acc_ref)

        dh = jnp.zeros((tm, D), F32)
        for j in range(nj):
            dh = dh + _dot_nt(dz_ref[:, j * Nj:(j + 1) * Nj], w_ref[j])
        dx, dshift, dscale, dgpre = _pre_bwd(x_ref[...], dh, vec_ref)
        dx_ref[...] = dres_ref[...] + dx
        _acc_add(acc_ref, A_SHIFT, dshift)
        _acc_add(acc_ref, A_SCALE, dscale)
        _acc_add(acc_ref, A_GPRE, dgpre)

    row = pl.BlockSpec((tm, D), lambda i: (i, 0))
    const = pl.BlockSpec((8, D), lambda i: (0, 0))
    return pl.pallas_call(
        body, name=name, grid=(S // tm,),
        in_specs=[pl.BlockSpec((tm, nj * Nj), lambda i: (i, 0)), pl.BlockSpec((nj, D, Nj), lambda i: (0, 0, 0)),
                  row, row, const],
        out_specs=[row, const],
        out_shape=[jax.ShapeDtypeStruct((S, D), F32), jax.ShapeDtypeStruct((8, D), F32)],
        compiler_params=_cp(("arbitrary",)),
    )(dz, w3, x, dres, vec)


def matmul_post(act, w, x, vec, name, tm=512):
    S, D = x.shape
    K = act.shape[1]

    def body(act_ref, w_ref, x_ref, vec_ref, out_ref, f_ref):
        f = _dot(act_ref[...], w_ref[...])
        f_ref[...] = f
        out_ref[...] = _post_fwd(x_ref[...], f, vec_ref, 1.0)

    row = pl.BlockSpec((tm, D), lambda i: (i, 0))
    return pl.pallas_call(
        body, name=name, grid=(S // tm,),
        in_specs=[pl.BlockSpec((tm, K), lambda i: (i, 0)), pl.BlockSpec((K, D), lambda i: (0, 0)), row,
                  pl.BlockSpec((8, D), lambda i: (0, 0))],
        out_specs=[row, row],
        out_shape=[jax.ShapeDtypeStruct((S, D), F32), jax.ShapeDtypeStruct((S, D), F32)],
        compiler_params=_cp(("arbitrary",)),
    )(act, w, x, vec)


def post_bwd_matmul(dout, f, vec, w, name, tm=512):
    S, D = dout.shape
    K = w.shape[0]

    def body(dout_ref, f_ref, vec_ref, w_ref, df_ref, dact_ref, acc_ref):
        @pl.when(pl.program_id(0) == 0)
        def _():
            acc_ref[...] = jnp.zeros_like(acc_ref)

        df, dgate, dgpost = _post_bwd(dout_ref[...], f_ref[...], vec_ref, 1.0)
        dfb = df.astype(BF16)
        df_ref[...] = dfb
        dact_ref[...] = _dot_nt(dfb, w_ref[...])
        _acc_add(acc_ref, A_GATE, dgate)
        _acc_add(acc_ref, A_GPOST, dgpost)

    row = pl.BlockSpec((tm, D), lambda i: (i, 0))
    const = pl.BlockSpec((8, D), lambda i: (0, 0))
    return pl.pallas_call(
        body, name=name, grid=(S // tm,),
        in_specs=[row, row, const, pl.BlockSpec((K, D), lambda i: (0, 0))],
        out_specs=[row, pl.BlockSpec((tm, K), lambda i: (i, 0)), const],
        out_shape=[jax.ShapeDtypeStruct((S, D), BF16), jax.ShapeDtypeStruct((S, K), F32),
                   jax.ShapeDtypeStruct((8, D), F32)],
        compiler_params=_cp(("arbitrary",)),
    )(dout, f, vec, w)


def _band(w, transposed, prev):
    r = lax.broadcasted_iota(jnp.int32, (CHUNK, CHUNK), 1 if transposed else 0)
    c = lax.broadcasted_iota(jnp.int32, (CHUNK, CHUNK), 0 if transposed else 1)
    d = r - c
    m = (d + CHUNK < w) if prev else ((d >= 0) & (d < w))
    return jnp.where(m, 1.0, 0.0).astype(BF16)


def _split_hi_lo(a):
    hi = a.astype(BF16)
    lo = (a - hi.astype(F32)).astype(BF16)
    return hi, lo


def _pool_diff(a, ap, g, denom):
    w = POOL_WINDOWS[g]
    a_hi, a_lo = _split_hi_lo(a)
    p_hi, p_lo = _split_hi_lo(ap)
    mc = _band(w, False, False)
    mp = _band(w, False, True)
    win = _dot(mc, a_hi) + _dot(mc, a_lo) + _dot(mp, p_hi) + _dot(mp, p_lo)
    return win / denom - a


def _sgu_norm(v, lg, lb):
    mu = _lanemean(v)
    xc = v - mu
    rstd = lax.rsqrt(_lanemean(xc * xc) + EPS)
    xhat = xc * rstd
    return xhat, rstd, xhat * lg + lb


def _tril_mask():
    r = lax.broadcasted_iota(jnp.int32, (CHUNK, CHUNK), 0)
    c = lax.broadcasted_iota(jnp.int32, (CHUNK, CHUNK), 1)
    return r >= c


def _positions(i, w):
    r = lax.broadcasted_iota(jnp.int32, (CHUNK, 128), 0)
    pos = (i * CHUNK + r + 1).astype(F32)
    return jnp.minimum(pos, float(w))


def mixa_core_fwd(z, pool_w, pool_scale, ln_g, ln_b, sgu_w, sgu_bexp, name):
    S = z.shape[0]
    W = z.shape[1] // 3
    G = len(POOL_WINDOWS)
    GD = W // G

    def body(zc_ref, zp_ref, pw_ref, ps_ref, lg_ref, lb_ref, sw_ref, sb_ref, y_ref):
        i = pl.program_id(0)
        has_prev = jnp.where(i > 0, 1.0, 0.0)
        for g in range(G):
            sl = slice(g * GD, (g + 1) * GD)
            a = zc_ref[:, sl]
            ap = zp_ref[:, sl] * has_prev
            d = _pool_diff(a, ap, g, _positions(i, POOL_WINDOWS[g])).astype(BF16)
            y_ref[:, sl] = (_dot(d, pw_ref[g]) * ps_ref[:, sl]).astype(BF16)
        tril = _tril_mask()
        for hh in range(G):
            u = _gelu(zc_ref[:, W + hh * GD:W + (hh + 1) * GD])
            v = _gelu(zc_ref[:, 2 * W + hh * GD:2 * W + (hh + 1) * GD])
            sl = slice(hh * GD, (hh + 1) * GD)
            _, _, vn = _sgu_norm(v, lg_ref[:, sl], lb_ref[:, sl])
            wm = jnp.where(tril, sw_ref[hh], 0.0).astype(BF16)
            s = _dot(wm, vn.astype(BF16)) + sb_ref[hh]
            y_ref[:, W + hh * GD:W + (hh + 1) * GD] = (u * s).astype(BF16)

    vecw = pl.BlockSpec((1, W), lambda i: (0, 0))
    mats = pl.BlockSpec((G, GD, GD), lambda i: (0, 0, 0))
    return pl.pallas_call(
        body, name=name, grid=(S // CHUNK,),
        in_specs=[pl.BlockSpec((CHUNK, 3 * W), lambda i: (i, 0)),
                  pl.BlockSpec((CHUNK, W), lambda i: (jnp.maximum(i - 1, 0), 0)),
                  mats, vecw, vecw, vecw, mats, mats],
        out_specs=pl.BlockSpec((CHUNK, 2 * W), lambda i: (i, 0)),
        out_shape=jax.ShapeDtypeStruct((S, 2 * W), BF16),
        compiler_params=_cp(("arbitrary",)),
    )(z, z, pool_w, pool_scale, ln_g, ln_b, sgu_w, sgu_bexp)


def mixa_core_bwd(z, dy, pool_w, pool_scale, ln_g, ln_b, sgu_w, sgu_bexp, name):
    S = z.shape[0]
    W = z.shape[1] // 3
    G = len(POOL_WINDOWS)
    GD = W // G
    n_tiles = S // CHUNK

    def body(zc_ref, zp_ref, dyc_ref, dyn_ref, pw_ref, ps_ref, lg_ref, lb_ref, sw_ref, sb_ref,
             dz_ref, dpw_ref, dvec_ref, dsw_ref, dsb_ref):
        i = pl.program_id(0)

        @pl.when(i == 0)
        def _():
            dpw_ref[...] = jnp.zeros_like(dpw_ref)
            dvec_ref[...] = jnp.zeros_like(dvec_ref)
            dsw_ref[...] = jnp.zeros_like(dsw_ref)
            dsb_ref[...] = jnp.zeros_like(dsb_ref)

        has_prev = jnp.where(i > 0, 1.0, 0.0)
        has_next = jnp.where(i < n_tiles - 1, 1.0, 0.0)
        for g in range(G):
            w = POOL_WINDOWS[g]
            sl = slice(g * GD, (g + 1) * GD)
            a = zc_ref[:, sl]
            ap = zp_ref[:, sl] * has_prev
            den_c = _positions(i, w)
            den_n = _positions(i + 1, w)
            d = _pool_diff(a, ap, g, den_c).astype(BF16)
            ps = ps_ref[:, sl]
            pw = pw_ref[g]
            dyc = dyc_ref[:, sl]
            dvec_ref[pl.ds(0, 1), sl] += _rowsum(dyc * _dot(d, pw))
            dyp_c = (dyc * ps).astype(BF16)
            dyp_n = (dyn_ref[:, sl] * (ps * has_next)).astype(BF16)
            dpw_ref[g] += _dot_tn(d, dyp_c)
            dd_c = _dot_nt(dyp_c, pw)
            dd_n = _dot_nt(dyp_n, pw)
            da = (_dot(_band(w, True, False), (dd_c / den_c).astype(BF16))
                  + _dot(_band(w, True, True), (dd_n / den_n).astype(BF16)) - dd_c)
            dz_ref[:, sl] = da.astype(BF16)
        tril = _tril_mask()
        for hh in range(G):
            sl = slice(hh * GD, (hh + 1) * GD)
            zu = zc_ref[:, W + hh * GD:W + (hh + 1) * GD]
            zv = zc_ref[:, 2 * W + hh * GD:2 * W + (hh + 1) * GD]
            u = _gelu(zu)
            v = _gelu(zv)
            lg = lg_ref[:, sl]
            xhat, rstd, vn = _sgu_norm(v, lg, lb_ref[:, sl])
            vnb = vn.astype(BF16)
            wm = jnp.where(tril, sw_ref[hh], 0.0).astype(BF16)
            s = _dot(wm, vnb) + sb_ref[hh]
            dyb = dyc_ref[:, W + hh * GD:W + (hh + 1) * GD]
            du = dyb * s
            ds = dyb * u
            dsb_ref[:, hh:hh + 1] += jnp.sum(ds, axis=1, keepdims=True)
            dsb16 = ds.astype(BF16)
            dsw_ref[hh] += jnp.where(tril, _dot_nt(dsb16, vnb), 0.0)
            dvn = _dot_tn(wm, dsb16)
            dvec_ref[pl.ds(1, 1), sl] += _rowsum(dvn * xhat)
            dvec_ref[pl.ds(2, 1), sl] += _rowsum(dvn)
            dxh = dvn * lg
            dv = rstd * (dxh - _lanemean(dxh) - xhat * _lanemean(dxh * xhat))
            dz_ref[:, W + hh * GD:W + (hh + 1) * GD] = (du * _gelu_grad(zu)).astype(BF16)
            dz_ref[:, 2 * W + hh * GD:2 * W + (hh + 1) * GD] = (dv * _gelu_grad(zv)).astype(BF16)

    vecw = pl.BlockSpec((1, W), lambda i: (0, 0))
    mats = pl.BlockSpec((G, GD, GD), lambda i: (0, 0, 0))
    return pl.pallas_call(
        body, name=name, grid=(n_tiles,),
        in_specs=[pl.BlockSpec((CHUNK, 3 * W), lambda i: (i, 0)),
                  pl.BlockSpec((CHUNK, W), lambda i: (jnp.maximum(i - 1, 0), 0)),
                  pl.BlockSpec((CHUNK, 2 * W), lambda i: (i, 0)),
                  pl.BlockSpec((CHUNK, W), lambda i: (jnp.minimum(i + 1, n_tiles - 1), 0)),
                  mats, vecw, vecw, vecw, mats, mats],
        out_specs=[pl.BlockSpec((CHUNK, 3 * W), lambda i: (i, 0)), mats,
                   pl.BlockSpec((8, W), lambda i: (0, 0)), mats, pl.BlockSpec((CHUNK, G), lambda i: (0, 0))],
        out_shape=[jax.ShapeDtypeStruct((S, 3 * W), BF16), jax.ShapeDtypeStruct((G, GD, GD), F32),
                   jax.ShapeDtypeStruct((8, W), F32), jax.ShapeDtypeStruct((G, CHUNK, CHUNK), F32),
                   jax.ShapeDtypeStruct((CHUNK, G), F32)],
        compiler_params=_cp(("arbitrary",)),
    )(z, z, dy, dy, pool_w, pool_scale, ln_g, ln_b, sgu_w, sgu_bexp)


def _cmul(ar, ai, br, bi):
    return ar * br - ai * bi, ar * bi + ai * br


def _cpow(ar, ai, n):
    rr, ri = None, None
    br, bi = ar, ai
    while n:
        if n & 1:
            rr, ri = (br, bi) if rr is None else _cmul(rr, ri, br, bi)
        n >>= 1
        if n:
            br, bi = _cmul(br, bi, br, bi)
    return rr, ri


def _seg_rows(k):
    return pl.ds(pl.multiple_of(k * SCAN_LANES, SCAN_LANES), SCAN_LANES)


def _scan_fwd(xre, xim, carry, ar, ai, K):
    P = xre.shape[1]
    a8r = jnp.broadcast_to(ar, (SCAN_LANES, P))
    a8i = jnp.broadcast_to(ai, (SCAN_LANES, P))

    def local(k, c):
        pr, pi = c
        rows = _seg_rows(k)
        nr = a8r * pr - a8i * pi + xre[rows, :]
        ni = a8r * pi + a8i * pr + xim[rows, :]
        xre[rows, :] = nr
        xim[rows, :] = ni
        return nr, ni

    er, ei = lax.fori_loop(1, K, local, (xre[pl.ds(0, SCAN_LANES), :], xim[pl.ds(0, SCAN_LANES), :]),
                           unroll=SCAN_UNROLL)
    akr, aki = _cpow(ar, ai, K)
    cr = jnp.zeros((1, P), F32)
    ci = jnp.zeros((1, P), F32)
    carry[pl.ds(0, 1), :] = cr
    carry[pl.ds(SCAN_LANES, 1), :] = ci
    for j in range(1, SCAN_LANES):
        tr, ti = _cmul(akr, aki, cr, ci)
        cr = er[j - 1:j, :] + tr
        ci = ei[j - 1:j, :] + ti
        carry[pl.ds(j, 1), :] = cr
        carry[pl.ds(SCAN_LANES + j, 1), :] = ci
    cmr = carry[pl.ds(0, SCAN_LANES), :]
    cmi = carry[pl.ds(SCAN_LANES, SCAN_LANES), :]

    def fix(k, c):
        pr, pi = c
        rows = _seg_rows(k)
        tr, ti = _cmul(pr, pi, cmr, cmi)
        xre[rows, :] += tr
        xim[rows, :] += ti
        return _cmul(pr, pi, a8r, a8i)

    lax.fori_loop(0, K, fix, (a8r, a8i), unroll=SCAN_UNROLL)


def s5_core_fwd(u, ar, ai, bre, bim, cre_t, cim_t, dskip, name):
    S, D = u.shape
    nblk, UB, PB = bre.shape
    K = S // SCAN_LANES

    def body(u_ref, ar_ref, ai_ref, bre_ref, bim_ref, cre_ref, cim_ref, d_ref, y_ref, xre, xim, carry):
        uv = u_ref[...]
        ub = uv.astype(BF16)
        xre[...] = _dot(ub, bre_ref[...])
        xim[...] = _dot(ub, bim_ref[...])
        _scan_fwd(xre, xim, carry, ar_ref[...], ai_ref[...], K)
        y_ref[...] = (_dot(xre[...].astype(BF16), cre_ref[...]) - _dot(xim[...].astype(BF16), cim_ref[...])
                      + d_ref[...] * uv)

    ucol = pl.BlockSpec((S, UB), lambda i: (0, i))
    pvec = pl.BlockSpec((None, 1, PB), lambda i: (i, 0, 0))
    bmat = pl.BlockSpec((None, UB, PB), lambda i: (i, 0, 0))
    cmat = pl.BlockSpec((None, PB, UB), lambda i: (i, 0, 0))
    return pl.pallas_call(
        body, name=name, grid=(nblk,),
        in_specs=[ucol, pvec, pvec, bmat, bmat, cmat, cmat, pl.BlockSpec((1, UB), lambda i: (0, i))],
        out_specs=ucol,
        out_shape=jax.ShapeDtypeStruct((S, D), F32),
        scratch_shapes=[pltpu.VMEM((S, PB), F32), pltpu.VMEM((S, PB), F32), pltpu.VMEM((2 * SCAN_LANES, PB), F32)],
        compiler_params=_cp(("arbitrary",)),
    )(u, ar, ai, bre, bim, cre_t, cim_t, dskip)


def s5_core_bwd(u, dy, ar, ai, bre, bim, cre, cim, dskip, name):
    S, D = u.shape
    nblk, UB, PB = bre.shape
    K = S // SCAN_LANES

    def body(u_ref, dy_ref, ar_ref, ai_ref, bre_ref, bim_ref, cre_ref, cim_ref, d_ref,
             du_ref, dbre_ref, dbim_ref, dcre_ref, dcim_ref, dar_ref, dai_ref, dd_ref,
             xre, xim, gre, gim, carry, carry_b):
        ar = ar_ref[...]
        ai = ai_ref[...]
        uv = u_ref[...]
        ub = uv.astype(BF16)
        dyv = dy_ref[...]
        dyb = dyv.astype(BF16)
        xre[...] = _dot(ub, bre_ref[...])
        xim[...] = _dot(ub, bim_ref[...])
        _scan_fwd(xre, xim, carry, ar, ai, K)
        dcre_ref[...] = _dot_tn(xre[...].astype(BF16), dyb)
        dcim_ref[...] = -_dot_tn(xim[...].astype(BF16), dyb)
        gre[...] = _dot(dyb, cre_ref[...])
        gim[...] = -_dot(dyb, cim_ref[...])

        a8r = jnp.broadcast_to(ar, (SCAN_LANES, PB))
        a8i = jnp.broadcast_to(ai, (SCAN_LANES, PB))
        na8i = -a8i

        def local(s, c):
            k = K - 2 - s
            nr, ni = c
            rows = _seg_rows(k)
            tr = gre[rows, :] + a8r * nr + a8i * ni
            ti = gim[rows, :] + a8r * ni - a8i * nr
            gre[rows, :] = tr
            gim[rows, :] = ti
            return tr, ti

        last = _seg_rows(K - 1)
        fr, fi = lax.fori_loop(0, K - 1, local, (gre[last, :], gim[last, :]), unroll=SCAN_UNROLL)
        akr, aki = _cpow(ar, -ai, K)
        cr = jnp.zeros((1, PB), F32)
        ci = jnp.zeros((1, PB), F32)
        carry_b[pl.ds(SCAN_LANES - 1, 1), :] = cr
        carry_b[pl.ds(2 * SCAN_LANES - 1, 1), :] = ci
        for j in range(SCAN_LANES - 2, -1, -1):
            tr, ti = _cmul(akr, aki, cr, ci)
            cr = fr[j + 1:j + 2, :] + tr
            ci = fi[j + 1:j + 2, :] + ti
            carry_b[pl.ds(j, 1), :] = cr
            carry_b[pl.ds(SCAN_LANES + j, 1), :] = ci
        cbr = carry_b[pl.ds(0, SCAN_LANES), :]
        cbi = carry_b[pl.ds(SCAN_LANES, SCAN_LANES), :]

        def fix_rows(rows, c, xr, xi):
            pr, pi, dar, dai = c
            tr, ti = _cmul(pr, pi, cbr, cbi)
            g_r = gre[rows, :] + tr
            g_i = gim[rows, :] + ti
            gre[rows, :] = g_r
            gim[rows, :] = g_i
            dar = dar + g_r * xr + g_i * xi
            dai = dai + g_i * xr - g_r * xi
            nr, ni = _cmul(pr, pi, a8r, na8i)
            return nr, ni, dar, dai

        def fix(s, c):
            k = K - 1 - s
            prev = _seg_rows(k - 1)
            return fix_rows(_seg_rows(k), c, xre[prev, :], xim[prev, :])

        z8 = jnp.zeros((SCAN_LANES, PB), F32)
        c = lax.fori_loop(0, K - 1, fix, (a8r, na8i, z8, z8), unroll=SCAN_UNROLL)
        _, _, dar, dai = fix_rows(_seg_rows(0), c, carry[pl.ds(0, SCAN_LANES), :], carry[pl.ds(SCAN_LANES, SCAN_LANES), :])
        dar_ref[...] = _rowsum(dar)
        dai_ref[...] = _rowsum(dai)
        grb = gre[...].astype(BF16)
        gib = gim[...].astype(BF16)
        dbre_ref[...] = _dot_tn(ub, grb)
        dbim_ref[...] = _dot_tn(ub, gib)
        du_ref[...] = _dot_nt(grb, bre_ref[...]) + _dot_nt(gib, bim_ref[...]) + d_ref[...] * dyv
        dd_ref[...] = _rowsum(dyv * uv)

    ucol = pl.BlockSpec((S, UB), lambda i: (0, i))
    pvec = pl.BlockSpec((None, 1, PB), lambda i: (i, 0, 0))
    bmat = pl.BlockSpec((None, UB, PB), lambda i: (i, 0, 0))
    cmat = pl.BlockSpec((None, PB, UB), lambda i: (i, 0, 0))
    dvec = pl.BlockSpec((1, UB), lambda i: (0, i))
    return pl.pallas_call(
        body, name=name, grid=(nblk,),
        in_specs=[ucol, ucol, pvec, pvec, bmat, bmat, bmat, bmat, dvec],
        out_specs=[ucol, bmat, bmat, cmat, cmat, pvec, pvec, dvec],
        out_shape=[jax.ShapeDtypeStruct((S, D), F32),
                   jax.ShapeDtypeStruct((nblk, UB, PB), F32), jax.ShapeDtypeStruct((nblk, UB, PB), F32),
                   jax.ShapeDtypeStruct((nblk, PB, UB), F32), jax.ShapeDtypeStruct((nblk, PB, UB), F32),
                   jax.ShapeDtypeStruct((nblk, 1, PB), F32), jax.ShapeDtypeStruct((nblk, 1, PB), F32),
                   jax.ShapeDtypeStruct((1, D), F32)],
        scratch_shapes=[pltpu.VMEM((S, PB), F32), pltpu.VMEM((S, PB), F32), pltpu.VMEM((S, PB), F32),
                        pltpu.VMEM((S, PB), F32), pltpu.VMEM((2 * SCAN_LANES, PB), F32),
                        pltpu.VMEM((2 * SCAN_LANES, PB), F32)],
        compiler_params=_cp(("arbitrary",)),
    )(u, dy, ar, ai, bre, bim, cre, cim, dskip)


def s5_glu_fwd(y, wglu, x, vec, name, tm=512):
    S, D = x.shape
    NJ = wglu.shape[-1]

    def body(y_ref, w_ref, x_ref, vec_ref, out_ref, f_ref):
        g = _gelu(y_ref[...]).astype(BF16)
        f = jnp.concatenate([_dot(g, w_ref[j]) * _sigmoid(_dot(g, w_ref[2 + j])) for j in range(2)], axis=1)
        f_ref[...] = f
        out_ref[...] = _post_fwd(x_ref[...], f, vec_ref, 1.0)

    row = pl.BlockSpec((tm, D), lambda i: (i, 0))
    return pl.pallas_call(
        body, name=name, grid=(S // tm,),
        in_specs=[row, pl.BlockSpec((N_CHIPS, D, NJ), lambda i: (0, 0, 0)), row, pl.BlockSpec((8, D), lambda i: (0, 0))],
        out_specs=[row, row],
        out_shape=[jax.ShapeDtypeStruct((S, D), F32), jax.ShapeDtypeStruct((S, D), F32)],
        compiler_params=_cp(("arbitrary",)),
    )(y, wglu, x, vec)


def s5_glu_bwd(dout, f, y, vec, wglu, name, tm=512):
    S, D = dout.shape
    NJ = wglu.shape[-1]

    def body(dout_ref, f_ref, y_ref, vec_ref, w_ref, dy_ref, dab_ref, g_ref, acc_ref):
        @pl.when(pl.program_id(0) == 0)
        def _():
            acc_ref[...] = jnp.zeros_like(acc_ref)

        df, dgate, dgpost = _post_bwd(dout_ref[...], f_ref[...], vec_ref, 1.0)
        yv = y_ref[...]
        g = _gelu(yv).astype(BF16)
        g_ref[...] = g
        dg = jnp.zeros((tm, D), F32)
        for j in range(2):
            a = _dot(g, w_ref[j])
            sig = _sigmoid(_dot(g, w_ref[2 + j]))
            dfj = df[:, j * NJ:(j + 1) * NJ]
            da = (dfj * sig).astype(BF16)
            db = (dfj * a * sig * (1.0 - sig)).astype(BF16)
            dab_ref[:, j * NJ:(j + 1) * NJ] = da
            dab_ref[:, (2 + j) * NJ:(3 + j) * NJ] = db
            dg = dg + _dot_nt(da, w_ref[j]) + _dot_nt(db, w_ref[2 + j])
        dy_ref[...] = dg * _gelu_grad(yv)
        _acc_add(acc_ref, A_GATE, dgate)
        _acc_add(acc_ref, A_GPOST, dgpost)

    row = pl.BlockSpec((tm, D), lambda i: (i, 0))
    const = pl.BlockSpec((8, D), lambda i: (0, 0))
    return pl.pallas_call(
        body, name=name, grid=(S // tm,),
        in_specs=[row, row, row, const, pl.BlockSpec((N_CHIPS, D, NJ), lambda i: (0, 0, 0))],
        out_specs=[row, pl.BlockSpec((tm, N_CHIPS * NJ), lambda i: (i, 0)), row, const],
        out_shape=[jax.ShapeDtypeStruct((S, D), F32), jax.ShapeDtypeStruct((S, N_CHIPS * NJ), BF16),
                   jax.ShapeDtypeStruct((S, D), BF16), jax.ShapeDtypeStruct((8, D), F32)],
        compiler_params=_cp(("arbitrary",)),
    )(dout, f, y, vec, wglu)


def loss_head(y, target, name, tm=512):
    S, D = y.shape

    def body(y_ref, t_ref, d_ref, l_ref):
        @pl.when(pl.program_id(0) == 0)
        def _():
            l_ref[...] = jnp.zeros_like(l_ref)

        err = y_ref[...] - t_ref[...]
        d_ref[...] = err * (1.0 / D)
        l_ref[...] += jnp.sum(_rowsum(err * err), axis=1, keepdims=True)

    row = pl.BlockSpec((tm, D), lambda i: (i, 0))
    return pl.pallas_call(
        body, name=name, grid=(S // tm,),
        in_specs=[row, row],
        out_specs=[row, pl.BlockSpec((1, 1), lambda i: (0, 0))],
        out_shape=[jax.ShapeDtypeStruct((S, D), F32), jax.ShapeDtypeStruct((1, 1), F32)],
        compiler_params=_cp(("arbitrary",)),
    )(y, target)


def zero_after(dep, name, shape=(8, 128)):
    def body(dep_ref, o_ref):
        o_ref[...] = jnp.zeros_like(o_ref)

    return pl.pallas_call(body, name=name, in_specs=[ANY], out_specs=pl.BlockSpec(memory_space=pltpu.VMEM),
                          out_shape=jax.ShapeDtypeStruct(shape, F32), compiler_params=_cp())(dep)


def ada_mod(c_all, ada_w, ada_b_shard, name, tn=768, dep=None):
    B, D = c_all.shape
    L, _, NS = ada_w.shape
    extra = [] if dep is None else [dep]

    def body(c_ref, w_ref, b_ref, *rest):
        cv = c_ref[...]
        cond = (cv * _sigmoid(cv)).astype(BF16)
        rest[-1][...] = _dot(cond, w_ref[...].astype(BF16)) + b_ref[...]

    return pl.pallas_call(
        body, name=name, grid=(L, NS // tn),
        in_specs=[pl.BlockSpec((B, D), lambda l, j: (0, 0)), pl.BlockSpec((None, D, tn), lambda l, j: (l, 0, j)),
                  pl.BlockSpec((None, 1, tn), lambda l, j: (l, 0, j))] + [ANY] * len(extra),
        out_specs=pl.BlockSpec((None, B, tn), lambda l, j: (l, 0, j)),
        out_shape=jax.ShapeDtypeStruct((L, B, NS), F32),
        compiler_params=_cp(("arbitrary", "arbitrary")),
    )(c_all, ada_w, ada_b_shard, *extra)


def ada_grad(c_all, dmod, name, tn=768):
    B, D = c_all.shape
    L, _, NS = dmod.shape

    def body(c_ref, d_ref, o_ref):
        cv = c_ref[...]
        cond = (cv * _sigmoid(cv)).astype(BF16)
        o_ref[...] = _dot_tn(cond, d_ref[...].astype(BF16))

    return pl.pallas_call(
        body, name=name, grid=(L, NS // tn),
        in_specs=[pl.BlockSpec((B, D), lambda l, j: (0, 0)), pl.BlockSpec((None, B, tn), lambda l, j: (l, 0, j))],
        out_specs=pl.BlockSpec((None, D, tn), lambda l, j: (l, 0, j)),
        out_shape=jax.ShapeDtypeStruct((L, D, NS), F32),
        compiler_params=_cp(("arbitrary", "arbitrary")),
    )(c_all, dmod)


def sum_gathered(land, own, me_arr, name):
    n, R, C = land.shape

    def body(me_ref, land_ref, own_ref, o_ref):
        acc = jnp.zeros(o_ref.shape, F32)
        for i in range(n):
            acc = acc + jnp.where(me_ref[0] == i, own_ref[...], land_ref[i]).astype(F32)
        o_ref[...] = acc

    tr = _row_tile(R, 64)
    grid_spec = pltpu.PrefetchScalarGridSpec(
        num_scalar_prefetch=1, grid=(R // tr,),
        in_specs=[pl.BlockSpec((n, tr, C), lambda i, me: (0, i, 0)), pl.BlockSpec((tr, C), lambda i, me: (i, 0))],
        out_specs=pl.BlockSpec((tr, C), lambda i, me: (i, 0)))
    return pl.pallas_call(
        body, name=name, grid_spec=grid_spec,
        out_shape=jax.ShapeDtypeStruct((R, C), F32),
        compiler_params=_cp(("arbitrary",)),
    )(me_arr, land, own)


def _row_tile(R, cap=512):
    if R <= cap:
        return R
    for cand in (512, 384, 352, 256, 128, 64, 32, 16, 8):
        if cand <= cap and R % cand == 0:
            return cand
    return R


def adamw(w, g, m, v, name, emit_grad=False, dep=None):
    R, C = w.shape
    tr = _row_tile(R, 256 if C > 1024 else 512)
    bc1 = 1.0 - ADAM_B1 ** ADAM_STEP
    bc2 = 1.0 - ADAM_B2 ** ADAM_STEP
    n_out = 4 if emit_grad else 3
    extra = [] if dep is None else [dep]

    def body(w_ref, g_ref, m_ref, v_ref, *outs):
        d_ref, nm_ref, nv_ref = outs[-3:]
        gv = g_ref[...]
        if emit_grad:
            outs[-4][...] = gv
        nm = ADAM_B1 * m_ref[...] + (1.0 - ADAM_B1) * gv
        nv = ADAM_B2 * v_ref[...] + (1.0 - ADAM_B2) * (gv * gv)
        nm_ref[...] = nm
        nv_ref[...] = nv
        d_ref[...] = -ADAM_LR * ((nm / bc1) / (jnp.sqrt(nv / bc2) + ADAM_EPS) + ADAM_WD * w_ref[...])

    blk = pl.BlockSpec((tr, C), lambda i: (i, 0))
    sd = jax.ShapeDtypeStruct((R, C), F32)
    return pl.pallas_call(
        body, name=name, grid=(R // tr,),
        in_specs=[blk, blk, blk, blk] + [ANY] * len(extra), out_specs=[blk] * n_out, out_shape=[sd] * n_out,
        compiler_params=_cp(("arbitrary",)),
    )(w, g, m, v, *extra)


def _as2d(a):
    if a.ndim == 1:
        return a.reshape(1, -1)
    return a.reshape(-1, a.shape[-1])


SMALL_PARAM_ELEMS = 1 << 18


def adamw_nd(w, g, m, v, name, emit_grad=False, dep=None):
    extra = [] if dep is None else [dep]
    if w.size <= SMALL_PARAM_ELEMS and not emit_grad:
        bc1 = 1.0 - ADAM_B1 ** ADAM_STEP
        bc2 = 1.0 - ADAM_B2 ** ADAM_STEP

        def body(w_ref, g_ref, m_ref, v_ref, *rest):
            d_ref, nm_ref, nv_ref = rest[-3:]
            gv = g_ref[...]
            nm = ADAM_B1 * m_ref[...] + (1.0 - ADAM_B1) * gv
            nv = ADAM_B2 * v_ref[...] + (1.0 - ADAM_B2) * (gv * gv)
            nm_ref[...] = nm
            nv_ref[...] = nv
            d_ref[...] = -ADAM_LR * ((nm / bc1) / (jnp.sqrt(nv / bc2) + ADAM_EPS) + ADAM_WD * w_ref[...])

        sd = jax.ShapeDtypeStruct(w.shape, F32)
        whole = pl.BlockSpec(memory_space=pltpu.VMEM)
        return tuple(pl.pallas_call(body, name=name, in_specs=[whole] * 4 + [ANY] * len(extra), out_specs=[whole] * 3,
                                    out_shape=[sd, sd, sd], compiler_params=_cp())(w, g.reshape(w.shape), m, v, *extra))
    outs = adamw(_as2d(w), _as2d(g.reshape(w.shape)), _as2d(m), _as2d(v), name, emit_grad=emit_grad, dep=dep)
    return tuple(o.reshape(w.shape) for o in outs)


def _place():
    x, y, c = lax.axis_index("x"), lax.axis_index("y"), lax.axis_index("c")
    chips = [(1 - x, y), (x, 1 - y), (1 - x, 1 - y)]
    return x, y, c, chips


def allgather_small(xs, name):
    m_per, n = xs.shape

    def body(x_ref, out_ref, send_sems, recv_sems, local_sem):
        x, y, c, chips = _place()
        me, sibling = (x, y, c), (x, y, 1 - c)

        def rows(px, py, pc):
            return out_ref.at[pl.ds((4 * px + 2 * py + pc) * m_per, m_per), :]

        def copy(k, block, to, src=None):
            return pltpu.make_async_remote_copy(
                src_ref=rows(*block) if src is None else src, dst_ref=rows(*block),
                send_sem=send_sems.at[k], recv_sem=recv_sems.at[k], device_id=to, device_id_type=MESH_T)

        mine = pltpu.make_async_copy(x_ref, rows(*me), local_sem)
        mine.start()
        first = [copy(0, me, sibling, src=x_ref)]
        first += [copy(1 + j, me, (*chip, c), src=x_ref) for j, chip in enumerate(chips)]
        for cp in first:
            cp.start()
        passed = [copy(4 + j, (*chip, c), sibling) for j, chip in enumerate(chips)]
        for j, chip in enumerate(chips):
            copy(1 + j, (*chip, c), me).wait_recv()
            passed[j].start()
        copy(0, sibling, me).wait_recv()
        for j, chip in enumerate(chips):
            copy(4 + j, (*chip, 1 - c), me).wait_recv()
        for cp in first + passed:
            cp.wait_send()
        mine.wait()

    return pl.pallas_call(
        body, name=name,
        out_shape=jax.ShapeDtypeStruct((N_DEV * m_per, n), xs.dtype),
        in_specs=[pl.BlockSpec(memory_space=pltpu.VMEM)],
        out_specs=pl.BlockSpec(memory_space=pltpu.VMEM),
        scratch_shapes=[pltpu.SemaphoreType.DMA((7,)), pltpu.SemaphoreType.DMA((7,)), pltpu.SemaphoreType.DMA],
        compiler_params=_cp(),
    )(xs)


def _half_rows(ref, half, rh):
    idx = (slice(None),) * (len(ref.shape) - 2) + (pl.ds(pl.multiple_of(half * rh, 16), rh), slice(None))
    return ref.at[idx]


def pair_sum(g, recv, cidx, name):
    L, NS, R, C = g.shape
    rh = R // 2
    tr = _row_tile(rh, 512)
    nt = rh // tr

    def body(c_ref, g_ref, r_ref, o_ref):
        o_ref[...] = (g_ref[...].astype(F32) + r_ref[...].astype(F32)).astype(BF16)

    grid_spec = pltpu.PrefetchScalarGridSpec(
        num_scalar_prefetch=1, grid=(L, NS, nt),
        in_specs=[pl.BlockSpec((None, None, tr, C), lambda l, s, t, c: (l, s, c[0] * nt + t, 0)),
                  pl.BlockSpec((None, None, tr, C), lambda l, s, t, c: (l, s, t, 0))],
        out_specs=pl.BlockSpec((None, None, tr, C), lambda l, s, t, c: (l, s, t, 0)))
    return pl.pallas_call(
        body, name=name, grid_spec=grid_spec,
        out_shape=jax.ShapeDtypeStruct((L, NS, rh, C), BF16),
        compiler_params=_cp(("arbitrary", "arbitrary", "arbitrary")),
    )(cidx, g, recv)


def chip_sum(part, recv, chip_c, name, dest=None, slot=0, n_slots=1):
    _, NS, RH, C = part.shape
    tr = _row_tile(RH, 512)
    nt = RH // tr

    def body(cc_ref, p_ref, r_ref, *rest):
        o_ref = rest[-1]
        acc = p_ref[...].astype(F32)
        for k in range(3):
            acc = acc + r_ref[k].astype(F32)
        o_ref[...] = acc

    in_specs = [pl.BlockSpec((None, None, tr, C), lambda t, cc: (0, cc[0], t, 0)),
                pl.BlockSpec((3, None, tr, C), lambda t, cc: (0, 0, t, 0))]
    args = [chip_c, part, recv]
    aliases = {}
    if dest is not None:
        in_specs.append(ANY)
        args.append(dest)
        aliases = {3: 0}
    grid_spec = pltpu.PrefetchScalarGridSpec(
        num_scalar_prefetch=1, grid=(nt,), in_specs=in_specs,
        out_specs=pl.BlockSpec((None, tr, C), lambda t, cc: (slot, cc[1] * nt + t, 0)))
    return pl.pallas_call(
        body, name=name, grid_spec=grid_spec,
        out_shape=jax.ShapeDtypeStruct((n_slots, 2 * RH, C), F32),
        input_output_aliases=aliases,
        compiler_params=_cp(("arbitrary",)),
    )(*args)


HBM_SPEC = pl.BlockSpec(memory_space=pltpu.HBM)
SEM_SPEC = pl.BlockSpec(memory_space=pltpu.SEMAPHORE)
DATAFLOW_EFFECT = pltpu.SideEffectType.DATAFLOW_SIDE_EFFECTING


def split_start(bufs, copies_fn, n_copies, name, dep=None, token_shape=(8, 128)):
    nb = len(bufs)
    extra = [] if dep is None else [dep]

    def body(*refs):
        ne = len(extra)
        send_sems, recv_sems = refs[nb + ne], refs[nb + ne + 1]
        token = refs[-1]
        for cp in copies_fn(refs[:nb], send_sems, recv_sems):
            cp.start()
        token[...] = jnp.zeros_like(token)

    outs = pl.pallas_call(
        body, name=name,
        out_shape=(pltpu.SemaphoreType.DMA((n_copies,)), pltpu.SemaphoreType.DMA((n_copies,)),
                   *[pltpu.HBM(b.shape, b.dtype) for b in bufs], jax.ShapeDtypeStruct(token_shape, F32)),
        in_specs=[HBM_SPEC] * nb + [ANY] * len(extra),
        out_specs=(SEM_SPEC, SEM_SPEC, *[HBM_SPEC] * nb, pl.BlockSpec(memory_space=pltpu.VMEM)),
        input_output_aliases={i: 2 + i for i in range(nb)},
        compiler_params=pltpu.CompilerParams(has_side_effects=DATAFLOW_EFFECT),
    )(*[pltpu.with_memory_space_constraint(b, pltpu.HBM) for b in bufs], *extra)
    return outs[0], outs[1], list(outs[2:2 + nb]), outs[-1]


def split_wait(send_sems, recv_sems, bufs, after, copies_fn, name):
    nb = len(bufs)

    def body(*refs):
        for cp in copies_fn(refs[:nb], refs[nb], refs[nb + 1]):
            cp.wait_send()
            cp.wait_recv()

    outs = pl.pallas_call(
        body, name=name,
        out_shape=tuple(pltpu.HBM(b.shape, b.dtype) for b in bufs),
        in_specs=[HBM_SPEC] * nb + [SEM_SPEC, SEM_SPEC, ANY],
        out_specs=tuple([HBM_SPEC] * nb),
        input_output_aliases={i: i for i in range(nb)},
        compiler_params=pltpu.CompilerParams(has_side_effects=DATAFLOW_EFFECT),
    )(*bufs, send_sems, recv_sems, after)
    return list(outs)


def chip_exchange_copies(refs, send_sems, recv_sems):
    n = len(refs) // 2
    x, y, c, chips = _place()
    cps = []
    for a in range(n):
        for k, (cx, cy) in enumerate(chips):
            cps.append(pltpu.make_async_remote_copy(
                src_ref=refs[a].at[:, 2 * cx + cy], dst_ref=refs[n + a].at[k],
                send_sem=send_sems.at[a * 3 + k], recv_sem=recv_sems.at[a * 3 + k],
                device_id=(cx, cy, c), device_id_type=MESH_T))
    return cps


def fill_copies(refs, send_sems, recv_sems):
    x, y, c, _ = _place()
    cps = []
    for a in range(len(refs)):
        rh = refs[a].shape[-2] // 2
        cps.append(pltpu.make_async_remote_copy(
            src_ref=_half_rows(refs[a], c, rh), dst_ref=_half_rows(refs[a], c, rh),
            send_sem=send_sems.at[a], recv_sem=recv_sems.at[a], device_id=(x, y, 1 - c), device_id_type=MESH_T))
    return cps


def allgather_direct_copies(refs, send_sems, recv_sems):
    x, y, c, _ = _place()
    me = 4 * x + 2 * y + c
    cps = []
    for r in range(1, N_DEV):
        fx, fy, fc = (r >> 2) & 1, (r >> 1) & 1, r & 1
        peer = (1 - x if fx else x, 1 - y if fy else y, 1 - c if fc else c)
        cps.append(pltpu.make_async_remote_copy(
            src_ref=refs[0], dst_ref=refs[1].at[me], send_sem=send_sems.at[r - 1], recv_sem=recv_sems.at[r - 1],
            device_id=peer, device_id_type=MESH_T))
    return cps


def pair_exchange_copies(refs, send_sems, recv_sems):
    n = len(refs) // 2
    x, y, c, _ = _place()
    return [pltpu.make_async_remote_copy(
        src_ref=_half_rows(refs[a], 1 - c, refs[a].shape[-2] // 2), dst_ref=refs[n + a],
        send_sem=send_sems.at[a], recv_sem=recv_sems.at[a], device_id=(x, y, 1 - c), device_id_type=MESH_T)
        for a in range(n)]


def weight_ici_copies(refs, send_sems, recv_sems):
    x, y, c, chips = _place()
    cps = []
    for a in range(len(refs)):
        for k, (cx, cy) in enumerate(chips):
            cps.append(pltpu.make_async_remote_copy(
                src_ref=refs[a].at[2 * x + y, c], dst_ref=refs[a].at[2 * x + y, c],
                send_sem=send_sems.at[a * 3 + k], recv_sem=recv_sems.at[a * 3 + k],
                device_id=(cx, cy, c), device_id_type=MESH_T))
    return cps


def weight_d2d_copies(refs, send_sems, recv_sems):
    x, y, c, chips = _place()
    cps = []
    for a in range(len(refs)):
        for k, (cx, cy) in enumerate(chips):
            cps.append(pltpu.make_async_remote_copy(
                src_ref=refs[a].at[2 * cx + cy, c], dst_ref=refs[a].at[2 * cx + cy, c],
                send_sem=send_sems.at[a * 3 + k], recv_sem=recv_sems.at[a * 3 + k],
                device_id=(x, y, 1 - c), device_id_type=MESH_T))
    return cps


def cast_place(w, lead, chip_arr, name, dep=None):
    R, C = w.shape[-2:]
    tr = _row_tile(R, 512)

    def body(chip_ref, w_ref, *rest):
        rest[-1][...] = w_ref[...].astype(BF16)

    in_specs = [pl.BlockSpec((None,) * len(lead) + (tr, C), lambda t, ch: (*lead, t, 0))]
    args = [chip_arr, w]
    if dep is not None:
        in_specs.append(ANY)
        args.append(dep)
    grid_spec = pltpu.PrefetchScalarGridSpec(
        num_scalar_prefetch=1, grid=(R // tr,), in_specs=in_specs,
        out_specs=pl.BlockSpec((None, tr, C), lambda t, ch: (ch[0], t, 0)))
    return pl.pallas_call(
        body, name=name, grid_spec=grid_spec,
        out_shape=jax.ShapeDtypeStruct((N_CHIPS, R, C), BF16),
        compiler_params=_cp(("arbitrary",)),
    )(*args)


def _s5_prepare(lam_re, lam_im, log_dt, b_re, b_im, c_re, c_im, groups_per_block):
    G, P = lam_re.shape
    N = b_re.shape[-1]
    gb = groups_per_block
    nblk = G // gb
    dt = jnp.exp(log_dt)[:, None]
    e = jnp.exp(lam_re * dt)
    a_re = e * jnp.cos(lam_im * dt)
    a_im = e * jnp.sin(lam_im * dt)
    n2 = lam_re * lam_re + lam_im * lam_im
    co_re = ((a_re - 1.0) * lam_re + a_im * lam_im) / n2
    co_im = (a_im * lam_re - (a_re - 1.0) * lam_im) / n2
    bb_re = co_re[..., None] * b_re - co_im[..., None] * b_im
    bb_im = co_re[..., None] * b_im + co_im[..., None] * b_re
    eye = jnp.eye(gb, dtype=F32)

    def blockdiag_np(m):
        m = m.reshape(nblk, gb, N, P)
        return jnp.einsum('bgnp,gh->bgnhp', m, eye).reshape(nblk, gb * N, gb * P)

    b_np_re = jnp.swapaxes(bb_re, 1, 2)
    b_np_im = jnp.swapaxes(bb_im, 1, 2)
    return (a_re.reshape(nblk, 1, gb * P), a_im.reshape(nblk, 1, gb * P),
            blockdiag_np(b_np_re), blockdiag_np(b_np_im), blockdiag_np(c_re), blockdiag_np(c_im))


def _to_scan_order(a):
    S, D = a.shape
    return a.reshape(SCAN_LANES, S // SCAN_LANES, D).transpose(1, 0, 2).reshape(S, D)


def _from_scan_order(a):
    S, D = a.shape
    return a.reshape(S // SCAN_LANES, SCAN_LANES, D).transpose(1, 0, 2).reshape(S, D)


def _pad_rows(a, mult=8):
    r = (-a.shape[0]) % mult
    if r:
        a = jnp.concatenate([a, jnp.zeros((r, a.shape[1]), a.dtype)], axis=0)
    return a


def _pack_rows(arrs, width):
    parts, offs, o = [], [], 0
    for a in arrs:
        flat = a.reshape(-1)
        r = (-flat.shape[0]) % (16 * width)
        if r:
            flat = jnp.concatenate([flat, jnp.zeros((r,), flat.dtype)])
        p = flat.reshape(-1, width)
        parts.append(p)
        offs.append((o, a.shape, a.size))
        o += p.shape[0]
    if o % 64:
        parts.append(jnp.zeros((64 - o % 64, width), parts[0].dtype))
    return jnp.concatenate(parts, axis=0), offs


def _unpack_rows(packed, offs):
    outs = []
    for o, shape, size in offs:
        rows = -(-size // packed.shape[1])
        outs.append(packed[o:o + rows].reshape(-1)[:size].reshape(shape))
    return outs


def kernel(x, c, ada_w, ada_b, norm_pre, norm_post, ffn_w_in, ffn_w_out, ab_w_in, pool_w, pool_scale, sgu_ln_g, sgu_ln_b, sgu_w, sgu_b, ab_w_out, ssm_w_in, ssm_lam_re, ssm_lam_im, ssm_b_re, ssm_b_im, ssm_c_re, ssm_c_im, ssm_d, ssm_log_dt, ssm_w_glu, loss_target, m_ada_w, m_ada_b, m_norm_pre, m_norm_post, m_ffn_w_in, m_ffn_w_out, m_ab_w_in, m_pool_w, m_pool_scale, m_sgu_ln_g, m_sgu_ln_b, m_sgu_w, m_sgu_b, m_ab_w_out, m_ssm_w_in, m_ssm_lam_re, m_ssm_lam_im, m_ssm_b_re, m_ssm_b_im, m_ssm_c_re, m_ssm_c_im, m_ssm_d, m_ssm_log_dt, m_ssm_w_glu, v_ada_w, v_ada_b, v_norm_pre, v_norm_post, v_ffn_w_in, v_ffn_w_out, v_ab_w_in, v_pool_w, v_pool_scale, v_sgu_ln_g, v_sgu_ln_b, v_sgu_w, v_sgu_b, v_ab_w_out, v_ssm_w_in, v_ssm_lam_re, v_ssm_lam_im, v_ssm_b_re, v_ssm_b_im, v_ssm_c_re, v_ssm_c_im, v_ssm_d, v_ssm_log_dt, v_ssm_w_glu):
    weights = dict(ada_w=ada_w, ada_b=ada_b, norm_pre=norm_pre, norm_post=norm_post, ffn_w_in=ffn_w_in,
                   ffn_w_out=ffn_w_out, ab_w_in=ab_w_in, pool_w=pool_w, pool_scale=pool_scale, sgu_ln_g=sgu_ln_g,
                   sgu_ln_b=sgu_ln_b, sgu_w=sgu_w, sgu_b=sgu_b, ab_w_out=ab_w_out, ssm_w_in=ssm_w_in,
                   ssm_lam_re=ssm_lam_re, ssm_lam_im=ssm_lam_im, ssm_b_re=ssm_b_re, ssm_b_im=ssm_b_im,
                   ssm_c_re=ssm_c_re, ssm_c_im=ssm_c_im, ssm_d=ssm_d, ssm_log_dt=ssm_log_dt, ssm_w_glu=ssm_w_glu)
    m_in = dict(ada_w=m_ada_w, ada_b=m_ada_b, norm_pre=m_norm_pre, norm_post=m_norm_post, ffn_w_in=m_ffn_w_in,
                ffn_w_out=m_ffn_w_out, ab_w_in=m_ab_w_in, pool_w=m_pool_w, pool_scale=m_pool_scale,
                sgu_ln_g=m_sgu_ln_g, sgu_ln_b=m_sgu_ln_b, sgu_w=m_sgu_w, sgu_b=m_sgu_b, ab_w_out=m_ab_w_out,
                ssm_w_in=m_ssm_w_in, ssm_lam_re=m_ssm_lam_re, ssm_lam_im=m_ssm_lam_im, ssm_b_re=m_ssm_b_re,
                ssm_b_im=m_ssm_b_im, ssm_c_re=m_ssm_c_re, ssm_c_im=m_ssm_c_im, ssm_d=m_ssm_d,
                ssm_log_dt=m_ssm_log_dt, ssm_w_glu=m_ssm_w_glu)
    v_in = dict(ada_w=v_ada_w, ada_b=v_ada_b, norm_pre=v_norm_pre, norm_post=v_norm_post, ffn_w_in=v_ffn_w_in,
                ffn_w_out=v_ffn_w_out, ab_w_in=v_ab_w_in, pool_w=v_pool_w, pool_scale=v_pool_scale,
                sgu_ln_g=v_sgu_ln_g, sgu_ln_b=v_sgu_ln_b, sgu_w=v_sgu_w, sgu_b=v_sgu_b, ab_w_out=v_ab_w_out,
                ssm_w_in=v_ssm_w_in, ssm_lam_re=v_ssm_lam_re, ssm_lam_im=v_ssm_lam_im, ssm_b_re=v_ssm_b_re,
                ssm_b_im=v_ssm_b_im, ssm_c_re=v_ssm_c_re, ssm_c_im=v_ssm_c_im, ssm_d=v_ssm_d,
                ssm_log_dt=v_ssm_log_dt, ssm_w_glu=v_ssm_w_glu)
    names = list(weights.keys())

    xi, yi, ci = lax.axis_index("x"), lax.axis_index("y"), lax.axis_index("c")
    chip = 2 * xi + yi
    me = 4 * xi + 2 * yi + ci
    S, D = x.shape[1], x.shape[2]
    L = ada_w.shape[0]
    NSUB = norm_pre.shape[1]
    DS = norm_pre.shape[2]
    FS = ffn_w_in.shape[-1]
    FR = ffn_w_out.shape[-2]
    x0 = x[0]
    target = loss_target[0]

    small_parts = [_pad_rows(p) for p in (c.reshape(D // DS, DS), norm_pre.reshape(L * NSUB, DS),
                                          norm_post.reshape(L * NSUB, DS), ssm_d.reshape(1, DS))]
    small_offs = [0]
    for p in small_parts:
        small_offs.append(small_offs[-1] + p.shape[0])
    small_all = allgather_small(jnp.concatenate(small_parts, axis=0), "ag_small").reshape(N_DEV, -1, DS)
    c_all = small_all[:, :D // DS].reshape(N_DEV, D)
    per_chip = small_all[0::2]
    o = small_offs[1]
    g_pre_full = jnp.moveaxis(per_chip[:, o:o + L * NSUB], 0, 1).reshape(L, NSUB, D)
    o = small_offs[2]
    g_post_full = jnp.moveaxis(per_chip[:, o:o + L * NSUB], 0, 1).reshape(L, NSUB, D)
    o = small_offs[3]
    d_full = jnp.moveaxis(per_chip[:, o:o + 1], 0, 1).reshape(1, D)

    pieces = []
    for l in range(L):
        pieces.append((f"ffn_{l}_0", [(ffn_w_in, (l, 0)), (ffn_w_out, (l, 0))]))
        if l % 2 == 0:
            pieces.append((f"mix_{l}", [(ab_w_in, (l // 2,)), (ab_w_out, (l // 2,))]))
        else:
            pieces.append((f"mix_{l}", [(ssm_w_in, (l // 2,)), (ssm_w_glu, (l // 2,))]))
        pieces.append((f"ffn_{l}_1", [(ffn_w_in, (l, 1)), (ffn_w_out, (l, 1))]))
    tags = [tag for tag, _ in pieces]
    chip_arr = chip.reshape(1).astype(jnp.int32)
    wg_started, wg_passing = {}, {}

    def cast_piece(tag, ws, dep):
        wg_started[tag] = [cast_place(w, lead, chip_arr, f"wg_cast_{tag}_{i}", dep=dep).reshape(
            N_CHIPS, 2, w.shape[-2] // 2, w.shape[-1]) for i, (w, lead) in enumerate(ws)]

    def start_piece(tag, dep):
        bufs = wg_started[tag]
        wg_started[tag] = split_start(bufs, weight_ici_copies, 3 * len(bufs), f"wg_ici_start_{tag}", dep=dep,
                                      token_shape=(8, D))
        return wg_started[tag][3]

    def weights_arrived(i, after):
        ssem, rsem, bufs, _ = wg_started[tags[i]]
        lands = split_wait(ssem, rsem, bufs, after, weight_ici_copies, f"wg_ici_wait_{tags[i]}")
        wg_passing[i] = split_start(lands, weight_d2d_copies, 3 * len(lands), f"wg_d2d_start_{tags[i]}")
        return wg_passing[i][3][0, 0]

    def weights_of(i, after):
        ssem, rsem, lands, _ = wg_passing[i]
        return split_wait(ssem, rsem, lands, after, weight_d2d_copies, f"wg_d2d_wait_{tags[i]}")

    NS = ada_w.shape[-1]
    ada_b_shard = lax.dynamic_slice_in_dim(ada_b, chip * NS, NS, axis=1).reshape(L, 1, NS)
    mod_part = ada_mod(c_all, ada_w, ada_b_shard, "ada_mod")
    mod_all = allgather_small(mod_part.reshape(L * N_DEV, NS), "ag_mod").reshape(N_DEV, L, N_DEV, NS)
    mod_mine = lax.dynamic_index_in_dim(mod_all[0::2], me, axis=2, keepdims=False)
    mod = jnp.moveaxis(mod_mine, 0, 1).reshape(L, NSUB, 3, D)
    cast_piece(*pieces[0], mod_all)
    first_token = start_piece(tags[0], mod_all)
    wg_tokens = [first_token]
    for tag, ws in pieces[1:]:
        cast_piece(tag, ws, first_token)
    for tag in tags[1:WG_AHEAD]:
        wg_tokens.append(start_piece(tag, first_token))

    def vec_of(l, s):
        return jnp.concatenate([mod[l, s], g_pre_full[l, s][None], g_post_full[l, s][None],
                                jnp.zeros((3, D), F32)], axis=0)

    GB = 8
    s5_args = (ssm_lam_re[0], ssm_lam_im[0], ssm_log_dt[0], ssm_b_re[0], ssm_b_im[0], ssm_c_re[0], ssm_c_im[0])
    (a_re, a_im, bblk_re, bblk_im, cblk_re, cblk_im), s5_vjp = jax.vjp(lambda *p: _s5_prepare(*p, GB), *s5_args)
    bre16, bim16 = bblk_re.astype(BF16), bblk_im.astype(BF16)
    cre16, cim16 = cblk_re.astype(BF16), cblk_im.astype(BF16)
    cre16_t, cim16_t = jnp.swapaxes(cre16, 1, 2), jnp.swapaxes(cim16, 1, 2)

    pool_w16 = pool_w[0].astype(BF16)
    sgu_bexp = jnp.broadcast_to(sgu_b[0][:, :, None], sgu_w[0].shape)

    saved = {}
    ffn_w = {}
    xcur = x0
    stage = [0]

    def next_weights(after, vec):
        i = stage[0]
        stage[0] += 1
        weights_arrived(i, after)
        if i + WG_AHEAD < len(tags):
            vec = vec + start_piece(tags[i + WG_AHEAD], wg_passing[i][3])
        return weights_of(i, after), vec

    for l in range(L):
        v0 = vec_of(l, 0)
        g, v0p = next_weights(xcur if l else sum(wg_tokens, v0), v0)
        ffn_w[(l, 0)] = (g[0].reshape(N_CHIPS, D, FS), g[1].reshape(N_CHIPS, FR, D))
        out, h, u, f = ffn_fwd(xcur, v0p, *ffn_w[(l, 0)], f"ffn_fwd_{l}_0")
        saved[(l, 0)] = (xcur, v0, h, u, f)
        xcur = out
        v1 = vec_of(l, 1)
        mix_g, v1p = next_weights(xcur, v1)
        if l % 2 == 0:
            abin_g = mix_g[0].reshape(N_CHIPS, D, -1)
            about_g = mix_g[1].reshape(-1, D)
            h, z = pre_matmul(xcur, v1p, abin_g, f"mixa_in_{l}")
            ycat = mixa_core_fwd(z, pool_w16, pool_scale, sgu_ln_g, sgu_ln_b, sgu_w[0], sgu_bexp, f"mixa_core_{l}")
            out, f = matmul_post(ycat, about_g, xcur, v1, f"mixa_out_{l}")
            saved[(l, 1)] = (xcur, v1, h, z, ycat, f)
        else:
            sin_g = mix_g[0].reshape(1, -1, D)
            glu_g = mix_g[1].reshape(N_CHIPS, D, -1)
            h, uu = pre_matmul(xcur, v1p, sin_g, f"s5_in_{l}")
            us = _to_scan_order(uu)
            ys = s5_core_fwd(us, a_re, a_im, bre16, bim16, cre16_t, cim16_t, d_full, f"s5_core_{l}")
            yy = _from_scan_order(ys)
            out, f = s5_glu_fwd(yy, glu_g, xcur, v1, f"s5_glu_{l}")
            saved[(l, 1)] = (xcur, v1, h, us, yy, f)
        xcur = out
        v2 = vec_of(l, 2)
        g, v2p = next_weights(xcur, v2)
        ffn_w[(l, 1)] = (g[0].reshape(N_CHIPS, D, FS), g[1].reshape(N_CHIPS, FR, D))
        out, h, u, f = ffn_fwd(xcur, v2p, *ffn_w[(l, 1)], f"ffn_fwd_{l}_1")
        saved[(l, 2)] = (xcur, v2, h, u, f)
        xcur = out

    dcur, sq = loss_head(xcur, target, "loss_head")
    loss = lax.psum(sq[0, 0], ("x", "y", "c")) * (0.5 / D)

    accs = {}
    small_g = {}
    cidx = ci.reshape(1).astype(jnp.int32)
    chip_c = jnp.stack([chip, ci]).astype(jnp.int32)
    rs_open, rs_pending = [], []

    pin = [zero_after(loss.reshape(1, 1), "after_loss", (8, D))]
    last_start = [pin[0]]

    def pin_behind(token):
        pin.append(token)
        last_start[0] = token

    def pinned(v):
        for token in pin:
            v = v + (token if token.shape == v.shape else token[0, 0])
        pin.clear()
        return v

    def rs_begin(tag, items):
        arrs = [it[0] for it in items]
        lands = [lax.empty(a.shape[:-2] + (a.shape[-2] // 2, a.shape[-1]), BF16) for a in arrs]
        started = split_start(arrs + lands, pair_exchange_copies, len(arrs), f"rs_pair_start_{tag}",
                              token_shape=(8, D))
        rs_open.append((tag, items, started))
        pin_behind(started[3])

    def rs_advance(after):
        while rs_open:
            tag, items, (ssem, rsem, bufs, _) = rs_open.pop(0)
            n = len(items)
            bufs = split_wait(ssem, rsem, bufs, after, pair_exchange_copies, f"rs_pair_wait_{tag}")
            parts = [pair_sum(a, r, cidx, f"rs_pair_sum_{tag}_{i}") for i, (a, r) in enumerate(zip(bufs[:n], bufs[n:]))]
            lands = [lax.empty((3, 1) + p.shape[2:], BF16) for p in parts]
            started = split_start(parts + lands, chip_exchange_copies, 3 * n, f"rs_chip_start_{tag}",
                                  token_shape=(8, D))
            rs_pending.append((tag, items, started))
            pin_behind(started[3])

    def ffn_back(l, s, k, dcur, before_wgrads=None):
        xin, vv, h, u, f = saved[(l, s)]
        vv = pinned(vv)
        dx, df, du, act, acc = ffn_bwd(dcur, xin, f, u, vv, *ffn_w[(l, k)], f"ffn_bwd_{l}_{k}")
        accs[(l, s)] = acc
        rs_advance(dx)
        dep = None if before_wgrads is None else before_wgrads()
        g_win = tn_matmul(h, du, 1024, f"ffn_dwin_{l}_{k}", dep=dep)
        g_wout = tn_matmul_cols(act, df, FS, D, f"ffn_dwout_{l}_{k}")
        rs_begin(f"ffn_{l}_{k}", [(g_win[None], "ffn_w_in", 2 * l + k, 2 * L),
                                  (g_wout.reshape(1, N_CHIPS, FR, D), "ffn_w_out", 2 * l + k, 2 * L)])
        return dx

    grads = {}

    small_names = ["pool_w", "pool_scale", "sgu_ln_g", "sgu_ln_b", "sgu_w", "sgu_b", "ssm_lam_re", "ssm_lam_im",
                   "ssm_b_re", "ssm_b_im", "ssm_c_re", "ssm_c_im", "ssm_log_dt"]
    small_open = []

    def small_grads_start():
        acc_all = jnp.stack([jnp.stack([accs[(l, s)] for s in range(NSUB)]) for l in range(L)])
        dmod_mine = acc_all[:, :, A_SHIFT:A_GATE + 1].reshape(L, NSUB * 3 * D)
        packed, offs = _pack_rows([dmod_mine, acc_all[:, :, A_GPRE], acc_all[:, :, A_GPOST], dd_mine]
                                  + [small_g[n] for n in small_names], D)
        packed = pinned(packed).astype(BF16)
        started = split_start([packed, lax.empty((N_DEV,) + packed.shape, BF16)], allgather_direct_copies,
                              N_DEV - 1, "ag_grads_start")
        small_open.append((started, offs))
        pin_behind(started[3])
        return started[3]

    def small_grads_finish(after):
        (ssem, rsem, bufs, _), offs = small_open.pop()
        own, land = split_wait(ssem, rsem, bufs, after, allgather_direct_copies, "ag_grads_wait")
        me_arr = me.reshape(1).astype(jnp.int32)
        summed = _unpack_rows(sum_gathered(land, own, me_arr, "sum_small"), offs)
        grads.update({n: g for n, g in zip(small_names, summed[4:])})
        grads["ada_b"] = summed[0]
        grads["norm_pre"] = lax.dynamic_slice_in_dim(summed[1], chip * DS, DS, axis=2)
        grads["norm_post"] = lax.dynamic_slice_in_dim(summed[2], chip * DS, DS, axis=2)
        grads["ssm_d"] = lax.dynamic_slice_in_dim(summed[3], chip * DS, DS, axis=1)
        nm = L * NSUB * 3
        is_me = (jnp.arange(N_DEV) == me)[:, None, None]
        dmod_all = jnp.where(is_me, own[None, :nm], land[:, :nm]).reshape(N_DEV, L, NSUB * 3 * D)
        dmod_shard = lax.dynamic_slice_in_dim(jnp.moveaxis(dmod_all, 0, 1), chip * NS, NS, axis=2)
        grads["ada_w"] = ada_grad(c_all, dmod_shard, "ada_grad")

    for l in reversed(range(L)):
        dcur = ffn_back(l, 2, 1, dcur)
        if l % 2 == 0:
            xin, vv, h, z, ycat, f = saved[(l, 1)]
            vv = pinned(vv)
            df, dact, acc_post = post_bwd_matmul(dcur, f, vv, about_g, f"mixa_out_bwd_{l}")
            rs_advance(dact)
            g_about = tn_matmul_cols(ycat, df, 512, D, f"mixa_dwout_{l}")
            dz, dpw, dvecw, dsw, dsb = mixa_core_bwd(z, dact, pool_w16, pool_scale, sgu_ln_g, sgu_ln_b, sgu_w[0],
                                                     sgu_bexp, f"mixa_core_bwd_{l}")
            g_abin = tn_matmul_cols(h, dz, 512, abin_g.shape[-1], f"mixa_dwin_{l}")
            dcur, acc_pre = matmul_pre_bwd(dz, abin_g, xin, dcur, vv, f"mixa_in_bwd_{l}")
            accs[(l, 1)] = acc_pre + acc_post
            rs_begin(f"mixa_{l}", [(g_abin[None], "ab_w_in", 0, 1),
                                   (g_about.reshape(1, N_CHIPS, -1, D), "ab_w_out", 0, 1)])
            small_g.update(pool_w=dpw[None], pool_scale=dvecw[0:1], sgu_ln_g=dvecw[1:2], sgu_ln_b=dvecw[2:3],
                           sgu_w=dsw[None], sgu_b=dsb.T[None])
        else:
            xin, vv, h, us, yy, f = saved[(l, 1)]
            vv = pinned(vv)
            dy, dab, gact, acc_post = s5_glu_bwd(dcur, f, yy, vv, glu_g, f"s5_glu_bwd_{l}")
            rs_advance(dy)
            g_glu = tn_matmul_cols(gact, dab, 512, glu_g.shape[-1], f"s5_dwglu_{l}")
            dys = _to_scan_order(dy)
            dus, dbre, dbim, dcre_t, dcim_t, dar, dai, dd = s5_core_bwd(
                us, dys, a_re, a_im, bre16, bim16, cre16, cim16, d_full, f"s5_core_bwd_{l}")
            du = _from_scan_order(dus).astype(BF16)
            g_sin = tn_matmul_cols(h, du, 512, D, f"s5_dwin_{l}")
            dcur, acc_pre = matmul_pre_bwd(du, sin_g, xin, dcur, vv, f"s5_in_bwd_{l}")
            accs[(l, 1)] = acc_pre + acc_post
            rs_begin(f"s5_{l}", [(g_sin.reshape(1, N_CHIPS, -1, D), "ssm_w_in", 0, 1), (g_glu[None], "ssm_w_glu", 0, 1)])
            s5_grads = s5_vjp((dar, dai, dbre, dbim, jnp.swapaxes(dcre_t, 1, 2), jnp.swapaxes(dcim_t, 1, 2)))
            small_g.update(ssm_lam_re=s5_grads[0][None], ssm_lam_im=s5_grads[1][None], ssm_log_dt=s5_grads[2][None],
                           ssm_b_re=s5_grads[3][None], ssm_b_im=s5_grads[4][None], ssm_c_re=s5_grads[5][None],
                           ssm_c_im=s5_grads[6][None])
            dd_mine = dd
        dcur = ffn_back(l, 0, 0, dcur, before_wgrads=small_grads_start if l == 0 else None)
    grad_x = dcur[None]
    rs_advance(last_start[0])
    small_grads_finish(last_start[0])

    big_names = ["ffn_w_in", "ffn_w_out", "ab_w_in", "ab_w_out", "ssm_w_in", "ssm_w_glu"]
    deltas, new_m, new_v = {}, {}, {}

    def update(n, dep=None):
        g = grads[n].reshape(weights[n].shape)
        outs = adamw_nd(weights[n], g, m_in[n], v_in[n], f"adamw_{n}", emit_grad=n in big_names, dep=dep)
        grads[n] = outs[0] if n in big_names else g
        deltas[n], new_m[n], new_v[n] = outs[-3:]

    fin = {}

    def chip_sums(pending, after):
        for tag, items, (ssem, rsem, bufs, _) in pending:
            bufs = split_wait(ssem, rsem, bufs, after, chip_exchange_copies, f"rs_chip_wait_{tag}")
            n = len(items)
            for i, (it, p, r) in enumerate(zip(items, bufs[:n], bufs[n:])):
                fin[it[1]] = chip_sum(p, r, chip_c, f"rs_chip_sum_{tag}_{i}", dest=fin.get(it[1]), slot=it[2],
                                      n_slots=it[3])

    chip_sums(rs_pending[:-1], pinned(grads["ada_b"]))
    update("ada_w", dep=fin["ab_w_in"])
    chip_sums(rs_pending[-1:], deltas["ada_w"])
    ssem, rsem, bufs, fill_token = split_start([fin[n] for n in big_names], fill_copies, len(big_names), "rs_fill_start")
    for n in names:
        if n not in big_names and n != "ada_w":
            update(n, dep=fill_token)
    filled = split_wait(ssem, rsem, bufs, deltas["ssm_log_dt"], fill_copies, "rs_fill_wait")
    for n, g in zip(big_names, filled):
        grads[n] = g
        update(n)

    return (loss, grad_x, *[grads[n] for n in names], *[deltas[n] for n in names],
            *[new_m[n] for n in names], *[new_v[n] for n in names])
```
